```python
import jax, jax.numpy as jnp
from jax import lax
import numpy as np

D_MODEL = 2048
BATCH = 8
SEQ = 4096
DEPTH = 2

CHUNK = 64
N_MIXERS = 2
EPS = 1e-6

SSM_EXPAND = 2
D_INNER = SSM_EXPAND * D_MODEL
SSM_HEADDIM = 64
SSM_HEADS = D_INNER // SSM_HEADDIM
SSM_GROUPS = 8
SSM_HEADS_PER_GROUP = SSM_HEADS // SSM_GROUPS
SSM_STATE = 128
SSM_CONV = 4
SSD_CHUNK = CHUNK
CONV_DIM = D_INNER + 2 * SSM_GROUPS * SSM_STATE
SSM_IN = D_INNER + CONV_DIM + SSM_HEADS
DT_MIN = 1e-3
DT_MAX = 1e-1

POOL_EXPAND = 2
D_POOL = POOL_EXPAND * D_MODEL
POOL_WINDOWS = (2, 4, 8, 16)
POOL_GROUPS = len(POOL_WINDOWS)
POOL_GROUP_DIM = D_POOL // POOL_GROUPS

kernel_name = "hybrid_ssd_multiscale_pool_trunk"


def rmsnorm(x, g):
    xf = x.astype(jnp.float32)
    y = xf * lax.rsqrt(jnp.mean(xf * xf, axis=-1, keepdims=True) + EPS)
    return (y * g.astype(jnp.float32)).astype(x.dtype)


def causal_depthwise_conv(x, w, b):
    k_taps = w.shape[0]
    length = x.shape[1]
    xp = jnp.pad(x, ((0, 0), (k_taps - 1, 0), (0, 0)))
    y = b
    for k in range(k_taps):
        y = y + xp[:, k:k + length] * w[k]
    return y


def ssd_chunked_scan(xdt, a_dt, bm, cm):
    bsz, length, g, r, p = xdt.shape
    n = bm.shape[-1]
    n_chunks = length // SSD_CHUNK

    def to_chunks(t):
        t = t.reshape((bsz, n_chunks, SSD_CHUNK) + t.shape[2:])
        return jnp.moveaxis(t, 1, 0)

    mask = jnp.tril(jnp.ones((SSD_CHUNK, SSD_CHUNK), dtype=bool))[None, :, :, None, None]

    def step(state, inp):
        xc, ac, bc, cc = inp
        cs = jnp.cumsum(ac, axis=1)
        seg = cs[:, :, None] - cs[:, None, :]
        decay = jnp.exp(jnp.where(mask, seg, -jnp.inf))
        cb = jnp.einsum("blgn,bsgn->blsg", cc, bc)
        y_diag = jnp.einsum("blsg,blsgr,bsgrp->blgrp", cb, decay, xc)
        y_off = jnp.einsum("blgn,bgrpn,blgr->blgrp", cc, state, jnp.exp(cs))
        last = cs[:, -1]
        w_in = jnp.exp(last[:, None] - cs)
        new_state = state * jnp.exp(last)[..., None, None] + jnp.einsum(
            "bsgn,bsgr,bsgrp->bgrpn", bc, w_in, xc)
        return new_state, y_diag + y_off

    state0 = jnp.zeros((bsz, g, r, p, n), jnp.float32)
    _, ys = lax.scan(step, state0, (to_chunks(xdt), to_chunks(a_dt), to_chunks(bm), to_chunks(cm)))
    return jnp.moveaxis(ys, 0, 1).reshape(bsz, length, g, r, p)


def ssm_mixer(h, w_in, conv_w, conv_b, dt_bias, a_log, d_skip, norm_g, w_out):
    bsz, length, _ = h.shape
    g, r, p, n = SSM_GROUPS, SSM_HEADS_PER_GROUP, SSM_HEADDIM, SSM_STATE
    proj = h @ w_in
    z = proj[..., :D_INNER]
    xbc = proj[..., D_INNER:D_INNER + CONV_DIM]
    dt = proj[..., D_INNER + CONV_DIM:]
    xbc = jax.nn.silu(causal_depthwise_conv(xbc, conv_w, conv_b)).astype(jnp.float32)
    xs = xbc[..., :D_INNER].reshape(bsz, length, g, r, p)
    bm = xbc[..., D_INNER:D_INNER + g * n].reshape(bsz, length, g, n)
    cm = xbc[..., D_INNER + g * n:].reshape(bsz, length, g, n)
    dt = jax.nn.softplus(dt.astype(jnp.float32) + dt_bias.astype(jnp.float32))
    dt = dt.reshape(bsz, length, g, r)
    a = -jnp.exp(a_log.astype(jnp.float32)).reshape(g, r)
    y = ssd_chunked_scan(xs * dt[..., None], dt * a, bm, cm)
    y = y + d_skip.astype(jnp.float32).reshape(g, r, 1) * xs
    y = y.reshape(bsz, length, D_INNER) * jax.nn.silu(z.astype(jnp.float32))
    y = rmsnorm(y, norm_g)
    return y.astype(h.dtype) @ w_out


def pool_mixer(h, w_in, w_group, scale, w_out):
    bsz, length, _ = h.shape
    proj = h @ w_in
    u = proj[..., :D_POOL].astype(jnp.float32).reshape(bsz, length, POOL_GROUPS, POOL_GROUP_DIM)
    gate = proj[..., D_POOL:].astype(jnp.float32)
    cs = jnp.cumsum(u, axis=1)
    pos = jnp.arange(1, length + 1, dtype=jnp.int32)
    means = []
    for gi, win in enumerate(POOL_WINDOWS):
        c = cs[:, :, gi]
        shifted = jnp.pad(c, ((0, 0), (win, 0), (0, 0)))[:, :length]
        cnt = jnp.minimum(pos, win).astype(jnp.float32)[None, :, None]
        means.append((c - shifted) / cnt)
    mixed = jnp.stack(means, axis=2) - u
    mixed = jnp.einsum("blgc,gcd->blgd", mixed, w_group.astype(jnp.float32))
    mixed = mixed.reshape(bsz, length, D_POOL) * scale.astype(jnp.float32)
    y = mixed * jax.nn.silu(gate)
    return y.astype(h.dtype) @ w_out


def _fwd_setup_inputs(seed: int = 0) -> dict:
    key = jax.random.key(seed)
    ks = jax.random.split(key, 20)
    n_a = (DEPTH + N_MIXERS - 1) // N_MIXERS
    n_b = DEPTH // N_MIXERS
    f32 = jnp.float32
    x = jax.random.normal(ks[0], (BATCH, SEQ, D_MODEL), f32)
    ln_g = 1.0 + 0.05 * jax.random.normal(ks[1], (DEPTH, D_MODEL), f32)
    final_g = 1.0 + 0.05 * jax.random.normal(ks[2], (D_MODEL,), f32)
    ssm_w_in = jax.random.normal(ks[3], (n_a, D_MODEL, SSM_IN), f32) * D_MODEL ** -0.5
    ssm_conv_w = jax.random.normal(ks[4], (n_a, SSM_CONV, CONV_DIM), f32) * SSM_CONV ** -0.5
    ssm_conv_b = 0.01 * jax.random.normal(ks[5], (n_a, CONV_DIM), f32)
    u_dt = jax.random.uniform(ks[6], (n_a, SSM_HEADS), f32)
    dt0 = jnp.exp(u_dt * (np.log(DT_MAX) - np.log(DT_MIN)) + np.log(DT_MIN))
    ssm_dt_bias = dt0 + jnp.log(-jnp.expm1(-dt0))
    ssm_a_log = jnp.log(jax.random.uniform(ks[7], (n_a, SSM_HEADS), f32, 1.0, 16.0))
    ssm_d = 1.0 + 0.1 * jax.random.normal(ks[8], (n_a, SSM_HEADS), f32)
    ssm_norm_g = 1.0 + 0.05 * jax.random.normal(ks[9], (n_a, D_INNER), f32)
    ssm_w_out = jax.random.normal(ks[10], (n_a, D_INNER, D_MODEL), f32) * D_INNER ** -0.5
    pool_w_in = jax.random.normal(ks[11], (n_b, D_MODEL, 2 * D_POOL), f32) * D_MODEL ** -0.5
    pool_w_group = jax.random.normal(ks[12], (n_b, POOL_GROUPS, POOL_GROUP_DIM, POOL_GROUP_DIM), f32) * POOL_GROUP_DIM ** -0.5
    pool_scale = 1.0 + 0.1 * jax.random.normal(ks[13], (n_b, D_POOL), f32)
    pool_w_out = jax.random.normal(ks[14], (n_b, D_POOL, D_MODEL), f32) * D_POOL ** -0.5
    return {"x": x, "ln_g": ln_g, "final_g": final_g,
            "ssm_w_in": ssm_w_in, "ssm_conv_w": ssm_conv_w, "ssm_conv_b": ssm_conv_b,
            "ssm_dt_bias": ssm_dt_bias, "ssm_a_log": ssm_a_log, "ssm_d": ssm_d,
            "ssm_norm_g": ssm_norm_g, "ssm_w_out": ssm_w_out,
            "pool_w_in": pool_w_in, "pool_w_group": pool_w_group,
            "pool_scale": pool_scale, "pool_w_out": pool_w_out}


def _fwd_reference(x, ln_g, final_g, ssm_w_in, ssm_conv_w, ssm_conv_b, ssm_dt_bias, ssm_a_log,
              ssm_d, ssm_norm_g, ssm_w_out, pool_w_in, pool_w_group, pool_scale, pool_w_out):
    for i in range(DEPTH):
        j = i // N_MIXERS
        hn = rmsnorm(x, ln_g[i])
        if i % N_MIXERS == 0:
            x = x + ssm_mixer(hn, ssm_w_in[j], ssm_conv_w[j], ssm_conv_b[j], ssm_dt_bias[j],
                              ssm_a_log[j], ssm_d[j], ssm_norm_g[j], ssm_w_out[j])
        else:
            x = x + pool_mixer(hn, pool_w_in[j], pool_w_group[j], pool_scale[j], pool_w_out[j])
    return rmsnorm(x, final_g)


import jax as _jax
import jax.numpy as _jnp

TWIN_FORMAT = 'train_step'
FWD_PARAMS = ['x', 'ln_g', 'final_g', 'ssm_w_in', 'ssm_conv_w', 'ssm_conv_b', 'ssm_dt_bias', 'ssm_a_log', 'ssm_d', 'ssm_norm_g', 'ssm_w_out', 'pool_w_in', 'pool_w_group', 'pool_scale', 'pool_w_out']
TWIN_WEIGHTS = ['ln_g', 'final_g', 'ssm_w_in', 'ssm_conv_w', 'ssm_conv_b', 'ssm_dt_bias', 'ssm_a_log', 'ssm_d', 'ssm_norm_g', 'ssm_w_out', 'pool_w_in', 'pool_w_group', 'pool_scale', 'pool_w_out']
TWIN_DIFF_INPUT = 'x'
TWIN_INPUTS = ['x', 'ln_g', 'final_g', 'ssm_w_in', 'ssm_conv_w', 'ssm_conv_b', 'ssm_dt_bias', 'ssm_a_log', 'ssm_d', 'ssm_norm_g', 'ssm_w_out', 'pool_w_in', 'pool_w_group', 'pool_scale', 'pool_w_out', 'loss_target', 'm_ln_g', 'm_final_g', 'm_ssm_w_in', 'm_ssm_conv_w', 'm_ssm_conv_b', 'm_ssm_dt_bias', 'm_ssm_a_log', 'm_ssm_d', 'm_ssm_norm_g', 'm_ssm_w_out', 'm_pool_w_in', 'm_pool_w_group', 'm_pool_scale', 'm_pool_w_out', 'v_ln_g', 'v_final_g', 'v_ssm_w_in', 'v_ssm_conv_w', 'v_ssm_conv_b', 'v_ssm_dt_bias', 'v_ssm_a_log', 'v_ssm_d', 'v_ssm_norm_g', 'v_ssm_w_out', 'v_pool_w_in', 'v_pool_w_group', 'v_pool_scale', 'v_pool_w_out']
TWIN_OUTPUTS = ['loss', 'grad_x', 'grad_ln_g', 'grad_final_g', 'grad_ssm_w_in', 'grad_ssm_conv_w', 'grad_ssm_conv_b', 'grad_ssm_dt_bias', 'grad_ssm_a_log', 'grad_ssm_d', 'grad_ssm_norm_g', 'grad_ssm_w_out', 'grad_pool_w_in', 'grad_pool_w_group', 'grad_pool_scale', 'grad_pool_w_out', 'delta_ln_g', 'delta_final_g', 'delta_ssm_w_in', 'delta_ssm_conv_w', 'delta_ssm_conv_b', 'delta_ssm_dt_bias', 'delta_ssm_a_log', 'delta_ssm_d', 'delta_ssm_norm_g', 'delta_ssm_w_out', 'delta_pool_w_in', 'delta_pool_w_group', 'delta_pool_scale', 'delta_pool_w_out', 'new_m_ln_g', 'new_m_final_g', 'new_m_ssm_w_in', 'new_m_ssm_conv_w', 'new_m_ssm_conv_b', 'new_m_ssm_dt_bias', 'new_m_ssm_a_log', 'new_m_ssm_d', 'new_m_ssm_norm_g', 'new_m_ssm_w_out', 'new_m_pool_w_in', 'new_m_pool_w_group', 'new_m_pool_scale', 'new_m_pool_w_out', 'new_v_ln_g', 'new_v_final_g', 'new_v_ssm_w_in', 'new_v_ssm_conv_w', 'new_v_ssm_conv_b', 'new_v_ssm_dt_bias', 'new_v_ssm_a_log', 'new_v_ssm_d', 'new_v_ssm_norm_g', 'new_v_ssm_w_out', 'new_v_pool_w_in', 'new_v_pool_w_group', 'new_v_pool_scale', 'new_v_pool_w_out']
TWIN_LEAF_KINDS = {'loss': 'loss', 'grad_x': 'grad_x', 'grad_ln_g': 'grad_w', 'grad_final_g': 'grad_w', 'grad_ssm_w_in': 'grad_w', 'grad_ssm_conv_w': 'grad_w', 'grad_ssm_conv_b': 'grad_w', 'grad_ssm_dt_bias': 'grad_w', 'grad_ssm_a_log': 'grad_w', 'grad_ssm_d': 'grad_w', 'grad_ssm_norm_g': 'grad_w', 'grad_ssm_w_out': 'grad_w', 'grad_pool_w_in': 'grad_w', 'grad_pool_w_group': 'grad_w', 'grad_pool_scale': 'grad_w', 'grad_pool_w_out': 'grad_w', 'delta_ln_g': 'delta_w', 'delta_final_g': 'delta_w', 'delta_ssm_w_in': 'delta_w', 'delta_ssm_conv_w': 'delta_w', 'delta_ssm_conv_b': 'delta_w', 'delta_ssm_dt_bias': 'delta_w', 'delta_ssm_a_log': 'delta_w', 'delta_ssm_d': 'delta_w', 'delta_ssm_norm_g': 'delta_w', 'delta_ssm_w_out': 'delta_w', 'delta_pool_w_in': 'delta_w', 'delta_pool_w_group': 'delta_w', 'delta_pool_scale': 'delta_w', 'delta_pool_w_out': 'delta_w', 'new_m_ln_g': 'new_m', 'new_m_final_g': 'new_m', 'new_m_ssm_w_in': 'new_m', 'new_m_ssm_conv_w': 'new_m', 'new_m_ssm_conv_b': 'new_m', 'new_m_ssm_dt_bias': 'new_m', 'new_m_ssm_a_log': 'new_m', 'new_m_ssm_d': 'new_m', 'new_m_ssm_norm_g': 'new_m', 'new_m_ssm_w_out': 'new_m', 'new_m_pool_w_in': 'new_m', 'new_m_pool_w_group': 'new_m', 'new_m_pool_scale': 'new_m', 'new_m_pool_w_out': 'new_m', 'new_v_ln_g': 'new_v', 'new_v_final_g': 'new_v', 'new_v_ssm_w_in': 'new_v', 'new_v_ssm_conv_w': 'new_v', 'new_v_ssm_conv_b': 'new_v', 'new_v_ssm_dt_bias': 'new_v', 'new_v_ssm_a_log': 'new_v', 'new_v_ssm_d': 'new_v', 'new_v_ssm_norm_g': 'new_v', 'new_v_ssm_w_out': 'new_v', 'new_v_pool_w_in': 'new_v', 'new_v_pool_w_group': 'new_v', 'new_v_pool_scale': 'new_v', 'new_v_pool_w_out': 'new_v'}


def _forward(args):
    return _fwd_reference(*[args[k] for k in FWD_PARAMS])


def _output_shape():
    def fwd():
        inp = _fwd_setup_inputs(0)
        return _fwd_reference(*[inp[k] for k in FWD_PARAMS])
    out = _jax.eval_shape(fwd)
    return out.shape, out.dtype

N_MICROBATCH = 1
ADAM_LR = 0.001
ADAM_B1 = 0.9
ADAM_B2 = 0.999
ADAM_EPS = 1e-08
ADAM_WD = 0.01
ADAM_STEP = 10
PER_EXAMPLE_BATCH_AXIS = {'x': 0, 'loss_target': 0}
SHARED_INPUTS = []
_WEIGHT_DTYPES = {'ln_g': _jnp.float32, 'final_g': _jnp.float32, 'ssm_w_in': _jnp.float32, 'ssm_conv_w': _jnp.float32, 'ssm_conv_b': _jnp.float32, 'ssm_dt_bias': _jnp.float32, 'ssm_a_log': _jnp.float32, 'ssm_d': _jnp.float32, 'ssm_norm_g': _jnp.float32, 'ssm_w_out': _jnp.float32, 'pool_w_in': _jnp.float32, 'pool_w_group': _jnp.float32, 'pool_scale': _jnp.float32, 'pool_w_out': _jnp.float32}
MOMENT_SCALE = {'ln_g': 8.288695e-02, 'final_g': 1.602829e+01, 'ssm_w_in': 4.563536e-02, 'ssm_conv_w': 4.158673e-02, 'ssm_conv_b': 5.884342e-02, 'ssm_dt_bias': 1.266992e-01, 'ssm_a_log': 1.522435e-01, 'ssm_d': 2.464177e-01, 'ssm_norm_g': 4.781589e-02, 'ssm_w_out': 7.008289e-02, 'pool_w_in': 2.223599e-02, 'pool_w_group': 2.186480e-02, 'pool_scale': 2.158873e-02, 'pool_w_out': 3.108304e-02}


def _to_microbatches(a, axis):
    t = _jnp.moveaxis(a, axis, 0)
    t = t.reshape((N_MICROBATCH, t.shape[0] // N_MICROBATCH) + t.shape[1:])
    return _jnp.moveaxis(t, 1, axis + 1)


def setup_inputs(seed: int = 0) -> dict:
    inp = _fwd_setup_inputs(seed)
    key = _jax.random.fold_in(_jax.random.key(seed), 7919)
    shape, _ = _output_shape()
    out = dict(inp)
    out["loss_target"] = _jax.random.normal(_jax.random.fold_in(key, 0), shape, _jnp.float32)
    for i, name in enumerate(TWIN_WEIGHTS):
        w = inp[name].astype(_jnp.float32)
        if MOMENT_SCALE is None:
            s = _jnp.sqrt(_jnp.mean(_jnp.square(w)) + 1e-30)
        else:
            s = MOMENT_SCALE[name]
        km, kv = _jax.random.split(_jax.random.fold_in(key, i + 1))
        out[name] = w
        out["m_" + name] = s * _jax.random.normal(km, w.shape, _jnp.float32)
        out["v_" + name] = (s * s) * _jax.random.uniform(kv, w.shape, _jnp.float32, 0.5, 1.5)
    if N_MICROBATCH > 1:
        for name, axis in PER_EXAMPLE_BATCH_AXIS.items():
            out[name] = _to_microbatches(out[name], axis)
    return {'x': out['x'], 'ln_g': out['ln_g'], 'final_g': out['final_g'], 'ssm_w_in': out['ssm_w_in'], 'ssm_conv_w': out['ssm_conv_w'], 'ssm_conv_b': out['ssm_conv_b'], 'ssm_dt_bias': out['ssm_dt_bias'], 'ssm_a_log': out['ssm_a_log'], 'ssm_d': out['ssm_d'], 'ssm_norm_g': out['ssm_norm_g'], 'ssm_w_out': out['ssm_w_out'], 'pool_w_in': out['pool_w_in'], 'pool_w_group': out['pool_w_group'], 'pool_scale': out['pool_scale'], 'pool_w_out': out['pool_w_out'], 'loss_target': out['loss_target'], 'm_ln_g': out['m_ln_g'], 'm_final_g': out['m_final_g'], 'm_ssm_w_in': out['m_ssm_w_in'], 'm_ssm_conv_w': out['m_ssm_conv_w'], 'm_ssm_conv_b': out['m_ssm_conv_b'], 'm_ssm_dt_bias': out['m_ssm_dt_bias'], 'm_ssm_a_log': out['m_ssm_a_log'], 'm_ssm_d': out['m_ssm_d'], 'm_ssm_norm_g': out['m_ssm_norm_g'], 'm_ssm_w_out': out['m_ssm_w_out'], 'm_pool_w_in': out['m_pool_w_in'], 'm_pool_w_group': out['m_pool_w_group'], 'm_pool_scale': out['m_pool_scale'], 'm_pool_w_out': out['m_pool_w_out'], 'v_ln_g': out['v_ln_g'], 'v_final_g': out['v_final_g'], 'v_ssm_w_in': out['v_ssm_w_in'], 'v_ssm_conv_w': out['v_ssm_conv_w'], 'v_ssm_conv_b': out['v_ssm_conv_b'], 'v_ssm_dt_bias': out['v_ssm_dt_bias'], 'v_ssm_a_log': out['v_ssm_a_log'], 'v_ssm_d': out['v_ssm_d'], 'v_ssm_norm_g': out['v_ssm_norm_g'], 'v_ssm_w_out': out['v_ssm_w_out'], 'v_pool_w_in': out['v_pool_w_in'], 'v_pool_w_group': out['v_pool_w_group'], 'v_pool_scale': out['v_pool_scale'], 'v_pool_w_out': out['v_pool_w_out']}


def _loss(weights, diff, rest, loss_target):
    with _jax.named_scope("forward"):
        args = {**rest, TWIN_DIFF_INPUT: diff, **{k: w.astype(_WEIGHT_DTYPES[k]) for k, w in weights.items()}}
        y = _forward(args)
    with _jax.named_scope("loss_head"):
        err = _jnp.square(y.astype(_jnp.float32) - loss_target)
        return 0.5 * _jnp.sum(_jnp.mean(err, axis=-1)) if err.ndim else 0.5 * err


def _adamw(w, g, m, v):
    m = ADAM_B1 * m + (1.0 - ADAM_B1) * g
    v = ADAM_B2 * v + (1.0 - ADAM_B2) * _jnp.square(g)
    m_hat = m / (1.0 - ADAM_B1 ** ADAM_STEP)
    v_hat = v / (1.0 - ADAM_B2 ** ADAM_STEP)
    delta = -ADAM_LR * (m_hat / (_jnp.sqrt(v_hat) + ADAM_EPS) + ADAM_WD * w)
    return delta, m, v


def reference(x, ln_g, final_g, ssm_w_in, ssm_conv_w, ssm_conv_b, ssm_dt_bias, ssm_a_log, ssm_d, ssm_norm_g, ssm_w_out, pool_w_in, pool_w_group, pool_scale, pool_w_out, loss_target, m_ln_g, m_final_g, m_ssm_w_in, m_ssm_conv_w, m_ssm_conv_b, m_ssm_dt_bias, m_ssm_a_log, m_ssm_d, m_ssm_norm_g, m_ssm_w_out, m_pool_w_in, m_pool_w_group, m_pool_scale, m_pool_w_out, v_ln_g, v_final_g, v_ssm_w_in, v_ssm_conv_w, v_ssm_conv_b, v_ssm_dt_bias, v_ssm_a_log, v_ssm_d, v_ssm_norm_g, v_ssm_w_out, v_pool_w_in, v_pool_w_group, v_pool_scale, v_pool_w_out):
    given = dict(x=x, ln_g=ln_g, final_g=final_g, ssm_w_in=ssm_w_in, ssm_conv_w=ssm_conv_w, ssm_conv_b=ssm_conv_b, ssm_dt_bias=ssm_dt_bias, ssm_a_log=ssm_a_log, ssm_d=ssm_d, ssm_norm_g=ssm_norm_g, ssm_w_out=ssm_w_out, pool_w_in=pool_w_in, pool_w_group=pool_w_group, pool_scale=pool_scale, pool_w_out=pool_w_out, loss_target=loss_target, m_ln_g=m_ln_g, m_final_g=m_final_g, m_ssm_w_in=m_ssm_w_in, m_ssm_conv_w=m_ssm_conv_w, m_ssm_conv_b=m_ssm_conv_b, m_ssm_dt_bias=m_ssm_dt_bias, m_ssm_a_log=m_ssm_a_log, m_ssm_d=m_ssm_d, m_ssm_norm_g=m_ssm_norm_g, m_ssm_w_out=m_ssm_w_out, m_pool_w_in=m_pool_w_in, m_pool_w_group=m_pool_w_group, m_pool_scale=m_pool_scale, m_pool_w_out=m_pool_w_out, v_ln_g=v_ln_g, v_final_g=v_final_g, v_ssm_w_in=v_ssm_w_in, v_ssm_conv_w=v_ssm_conv_w, v_ssm_conv_b=v_ssm_conv_b, v_ssm_dt_bias=v_ssm_dt_bias, v_ssm_a_log=v_ssm_a_log, v_ssm_d=v_ssm_d, v_ssm_norm_g=v_ssm_norm_g, v_ssm_w_out=v_ssm_w_out, v_pool_w_in=v_pool_w_in, v_pool_w_group=v_pool_w_group, v_pool_scale=v_pool_scale, v_pool_w_out=v_pool_w_out)
    weights = {n: given[n] for n in TWIN_WEIGHTS}
    shared = {n: given[n] for n in SHARED_INPUTS}
    per_example = {n: given[n] for n in ['x']}
    grad_fn = _jax.value_and_grad(_loss, argnums=(0, 1))

    def one_microbatch(ex, loss_target):
        ex = dict(ex)
        diff = ex.pop(TWIN_DIFF_INPUT)
        return grad_fn(weights, diff, {**shared, **ex}, loss_target)

    if N_MICROBATCH == 1:
        loss, (grad_w, grad_x) = one_microbatch(per_example, given["loss_target"])
    else:
        def body(carry, xs):
            loss_sum, grad_sum = carry
            l_k, (gw_k, gx_k) = one_microbatch(xs[0], xs[1])
            with _jax.named_scope("update"):
                return (loss_sum + l_k, _jax.tree.map(_jnp.add, grad_sum, gw_k)), gx_k

        init = (_jnp.zeros((), _jnp.float32), _jax.tree.map(_jnp.zeros_like, weights))
        (loss, grad_w), grad_x = _jax.lax.scan(body, init, (per_example, given["loss_target"]))
    with _jax.named_scope("update"):
        delta_w, new_m, new_v = {}, {}, {}
        for n in TWIN_WEIGHTS:
            delta_w[n], new_m[n], new_v[n] = _adamw(weights[n], grad_w[n], given["m_" + n], given["v_" + n])
    return (loss, grad_x, *[grad_w[n] for n in TWIN_WEIGHTS], *[delta_w[n] for n in TWIN_WEIGHTS],
            *[new_m[n] for n in TWIN_WEIGHTS], *[new_v[n] for n in TWIN_WEIGHTS])
```

```python
import math

import jax
import jax.numpy as jnp
import numpy as np
from jax import lax
from jax.experimental import pallas as pl
from jax.experimental.pallas import tpu as pltpu

F32 = jnp.float32
BF = jnp.bfloat16
U32 = jnp.uint32

N_DEV = 8
EPS = 1e-6
SSD_CHUNK = 64
SSM_STATE = 128
CONV_TAPS = 4
HEAD_LANES = 128
POOL_WINDOWS = (2, 4, 8, 16)
ADAM_LR, ADAM_B1, ADAM_B2, ADAM_EPS, ADAM_WD, ADAM_STEP = 0.001, 0.9, 0.999, 1e-08, 0.01, 10
VMEM_LIMIT = 56 * 1024 * 1024
NEG_BIG = -1e30

NN = ((1,), (0,))
NT = ((1,), (1,))
TN = ((0,), (0,))
MESH = pl.DeviceIdType.MESH


def _t(dim, pref):
    return pref if dim % pref == 0 else dim


def _row_tile(rows, pref, mult=8):
    best = rows
    for cand in range(mult, min(rows, pref) + 1, mult):
        if rows % cand == 0:
            best = cand
    return best


def _params(sem=None):
    return pltpu.CompilerParams(dimension_semantics=sem, vmem_limit_bytes=VMEM_LIMIT)


def _silu(x):
    return x * (1.0 / (1.0 + jnp.exp(-x)))


def _dsilu(x):
    s = 1.0 / (1.0 + jnp.exp(-x))
    return s * (1.0 + x * (1.0 - s))


def _hdot(a, b, dims=NN):
    return lax.dot_general(a, b, (dims, ((), ())), precision=lax.Precision.HIGHEST, preferred_element_type=F32)


def _bdot(a, b, dims=NN):
    return lax.dot_general(a.astype(BF), b.astype(BF), (dims, ((), ())), preferred_element_type=F32)


def _pack_pairs(x):
    h = x.shape[1] // 2
    hi = lax.bitcast_convert_type(x[:, :h].astype(jnp.bfloat16).astype(F32), U32)
    lo = lax.bitcast_convert_type(x[:, h:].astype(jnp.bfloat16).astype(F32), U32)
    return lax.bitcast_convert_type(hi | (lo >> 16), F32)


def _unpack_pairs(w):
    u = lax.bitcast_convert_type(w, U32)
    hi = lax.bitcast_convert_type(u & jnp.uint32(0xFFFF0000), F32)
    lo = lax.bitcast_convert_type(u << 16, F32)
    return hi, lo


def _mm(name, grid, a, a_spec, b, b_spec, dims, outs, o_specs, acc_shape, extra=(), extra_specs=(), epi=None):
    nk = grid[-1]
    n_extra, n_out = len(extra), len(outs)

    def body(*refs):
        a_ref, b_ref = refs[0], refs[1]
        ex = refs[2:2 + n_extra]
        o_refs = refs[2 + n_extra:2 + n_extra + n_out]
        acc = refs[-1]
        k = pl.program_id(len(grid) - 1)

        @pl.when(k == 0)
        def _():
            acc[...] = jnp.zeros_like(acc)

        acc[...] += _bdot(a_ref[...], b_ref[...], dims)

        @pl.when(k == nk - 1)
        def _():
            res = acc[...]
            res = (res,) if epi is None else epi(res, *[e[...] for e in ex])
            for o, r in zip(o_refs, res):
                o[...] = r.astype(o.dtype)

    sem = ("parallel",) * (len(grid) - 1) + ("arbitrary",)
    return pl.pallas_call(
        body, grid=grid, in_specs=[a_spec, b_spec, *extra_specs], out_specs=list(o_specs), out_shape=list(outs),
        scratch_shapes=[pltpu.VMEM(acc_shape, F32)], compiler_params=_params(sem), name=name)(a, b, *extra)


def _add_epi(acc, add):
    return (acc + add,)


def _pack_epi(acc):
    return (_pack_pairs(acc),)


def _mm_nt(name, a, b, n_off, n, out_dtype=F32, tm=1024, tn=512):
    m, kk = a.shape
    tm, tn = _t(m, tm), math.gcd(_t(n, tn), n_off)
    return _mm(name, (m // tm, n // tn, 1),
               a, pl.BlockSpec((tm, kk), lambda i, j, k: (i, 0)),
               b, pl.BlockSpec((tn, kk), lambda i, j, k: (n_off // tn + j, 0)), NT,
               [jax.ShapeDtypeStruct((m, n), out_dtype)], [pl.BlockSpec((tm, tn), lambda i, j, k: (i, j))], (tm, tn))[0]


def _mm_nn(name, a, b, k_off=0, kk=None, add=None, out_dtype=F32, tm=1024, tn=1024, tk=1024):
    m = a.shape[0]
    kk = a.shape[1] if kk is None else kk
    n = b.shape[1]
    tm, tn, tk = _t(m, tm), _t(n, tn), math.gcd(_t(kk, tk), k_off)
    extra, especs, epi = (), (), None
    if add is not None:
        extra, especs, epi = (add,), (pl.BlockSpec((tm, tn), lambda i, j, k: (i, j)),), _add_epi
    return _mm(name, (m // tm, n // tn, kk // tk),
               a, pl.BlockSpec((tm, tk), lambda i, j, k: (i, k)),
               b, pl.BlockSpec((tk, tn), lambda i, j, k: (k_off // tk + k, j)), NN,
               [jax.ShapeDtypeStruct((m, n), out_dtype)], [pl.BlockSpec((tm, tn), lambda i, j, k: (i, j))], (tm, tn),
               extra, especs, epi)[0]


def _mm_nt_k(name, a, b, k_off=0, add=None, tm=1024, tn=1024, tk=1024):
    m, ka = a.shape
    n = b.shape[0]
    tm, tn, tk = _t(m, tm), _t(n, tn), math.gcd(_t(ka, tk), k_off)
    extra, especs, epi = (), (), None
    if add is not None:
        extra, especs, epi = (add,), (pl.BlockSpec((tm, tn), lambda i, j, k: (i, j)),), _add_epi
    return _mm(name, (m // tm, n // tn, ka // tk),
               a, pl.BlockSpec((tm, tk), lambda i, j, k: (i, k)),
               b, pl.BlockSpec((tn, tk), lambda i, j, k: (j, k_off // tk + k)), NT,
               [jax.ShapeDtypeStruct((m, n), F32)], [pl.BlockSpec((tm, tn), lambda i, j, k: (i, j))], (tm, tn),
               extra, especs, epi)[0]


def _mm_tn_packed(name, a, b, pw, tm=512, tk=1024):
    kk, m = a.shape
    n = b.shape[1]
    tm, tk = _t(m, tm), _t(kk, tk)
    return _mm(name, (m // tm, n // pw, kk // tk),
               a, pl.BlockSpec((tk, tm), lambda i, j, k: (k, i)),
               b, pl.BlockSpec((tk, pw), lambda i, j, k: (k, j)), TN,
               [jax.ShapeDtypeStruct((n // pw, m, pw // 2), F32)],
               [pl.BlockSpec((None, tm, pw // 2), lambda i, j, k: (j, i, 0))], (tm, pw), epi=_pack_epi)[0]


def _rms_fwd(name, x, g, tm=256):
    l, d = x.shape
    tm = _t(l, tm)

    def body(x_ref, g_ref, h_ref, r_ref):
        xv = x_ref[...]
        r = lax.rsqrt(jnp.mean(xv * xv, axis=-1, keepdims=True) + EPS)
        h_ref[...] = (xv * r * g_ref[...]).astype(BF)
        r_ref[...] = r

    return pl.pallas_call(
        body, grid=(l // tm,),
        in_specs=[pl.BlockSpec((tm, d), lambda i: (i, 0)), pl.BlockSpec((1, d), lambda i: (0, 0))],
        out_specs=[pl.BlockSpec((tm, d), lambda i: (i, 0)), pl.BlockSpec((tm, 1), lambda i: (i, 0))],
        out_shape=[jax.ShapeDtypeStruct((l, d), BF), jax.ShapeDtypeStruct((l, 1), F32)],
        compiler_params=_params(("parallel",)), name=name)(x, g)


def _rms_bwd(name, dh, x, r, g, dres, tm=256):
    l, d = x.shape
    tm = _t(l, tm)

    def body(dh_ref, x_ref, r_ref, g_ref, dres_ref, dx_ref, dg_ref):
        i = pl.program_id(0)
        rr = r_ref[...]
        xhat = x_ref[...] * rr
        dhv = dh_ref[...]
        dxh = dhv * g_ref[...]
        dx_ref[...] = dres_ref[...] + rr * (dxh - xhat * jnp.mean(dxh * xhat, axis=-1, keepdims=True))

        @pl.when(i == 0)
        def _():
            dg_ref[...] = jnp.zeros_like(dg_ref)

        dg_ref[...] += jnp.sum(dhv * xhat, axis=0, keepdims=True)

    row = pl.BlockSpec((tm, d), lambda i: (i, 0))
    vec = pl.BlockSpec((1, d), lambda i: (0, 0))
    return pl.pallas_call(
        body, grid=(l // tm,), in_specs=[row, row, pl.BlockSpec((tm, 1), lambda i: (i, 0)), vec, row],
        out_specs=[row, vec], out_shape=[jax.ShapeDtypeStruct((l, d), F32), jax.ShapeDtypeStruct((1, d), F32)],
        compiler_params=_params(("arbitrary",)), name=name)(dh, x, r, g, dres)


def _loss_head(x, g, target, tm=256):
    l, d = x.shape
    tm = _t(l, tm)

    def body(x_ref, g_ref, t_ref, dx_ref, dg_ref, loss_ref):
        i = pl.program_id(0)
        xv = x_ref[...]
        gv = g_ref[...]
        r = lax.rsqrt(jnp.mean(xv * xv, axis=-1, keepdims=True) + EPS)
        xhat = xv * r
        e = xhat * gv - t_ref[...]
        dy = e * (1.0 / d)
        dxh = dy * gv
        dx_ref[...] = r * (dxh - xhat * jnp.mean(dxh * xhat, axis=-1, keepdims=True))

        @pl.when(i == 0)
        def _():
            dg_ref[...] = jnp.zeros_like(dg_ref)
            loss_ref[...] = jnp.zeros_like(loss_ref)

        dg_ref[...] += jnp.sum(dy * xhat, axis=0, keepdims=True)
        loss_ref[...] += 0.5 * jnp.sum(jnp.sum(e * e, axis=-1, keepdims=True) * (1.0 / d), axis=0, keepdims=True)

    row = pl.BlockSpec((tm, d), lambda i: (i, 0))
    vec = pl.BlockSpec((1, d), lambda i: (0, 0))
    return pl.pallas_call(
        body, grid=(l // tm,), in_specs=[row, vec, row],
        out_specs=[row, vec, pl.BlockSpec((1, 1), lambda i: (0, 0))],
        out_shape=[jax.ShapeDtypeStruct((l, d), F32), jax.ShapeDtypeStruct((1, d), F32),
                   jax.ShapeDtypeStruct((1, 1), F32)],
        compiler_params=_params(("arbitrary",)), name="loss_head")(x, g, target)


CONV_HALO = 8


def _conv_pre(x_ref, w_ref, b_ref, i, tr):
    r0 = pl.multiple_of(i * tr, tr)
    cur = x_ref[pl.ds(r0, tr), :]
    prev = x_ref[pl.ds(pl.multiple_of(jnp.maximum(r0 - CONV_HALO, 0), CONV_HALO), CONV_HALO), :]
    prev = jnp.where(i > 0, prev, 0.0)
    ext = jnp.concatenate([prev, cur], axis=0)
    taps = []
    for k in range(CONV_TAPS):
        s = CONV_TAPS - 1 - k
        taps.append(cur if s == 0 else pltpu.roll(ext, s, 0)[CONV_HALO:])
    pre = b_ref[...] + sum(w_ref[k:k + 1, :] * taps[k] for k in range(CONV_TAPS))
    return r0, pre, taps


def _conv_fwd(pm, col_off, conv_w, conv_b, cw=256, tr=512):
    l = pm.shape[0]
    c = conv_w.shape[1]
    cw, tr = _t(c, cw), _t(l, tr)
    assert col_off % cw == 0

    def body(x_ref, w_ref, b_ref, o_ref):
        def step(i, carry):
            r0, pre, _ = _conv_pre(x_ref, w_ref, b_ref, i, tr)
            o_ref[pl.ds(r0, tr), :] = _silu(pre)
            return carry
        lax.fori_loop(0, l // tr, step, 0)

    return pl.pallas_call(
        body, grid=(c // cw,),
        in_specs=[pl.BlockSpec((l, cw), lambda j: (0, col_off // cw + j)),
                  pl.BlockSpec((CONV_TAPS, cw), lambda j: (0, j)), pl.BlockSpec((1, cw), lambda j: (0, j))],
        out_specs=pl.BlockSpec((l, cw), lambda j: (0, j)), out_shape=jax.ShapeDtypeStruct((l, c), F32),
        compiler_params=_params(("parallel",)), name="conv_fwd")(pm, conv_w, conv_b)


def _conv_bwd(pm, col_off, conv_w, conv_b, dy, cw=256, tr=512):
    l = pm.shape[0]
    c = conv_w.shape[1]
    cw, tr = _t(c, cw), _t(l, tr)
    nt = l // tr

    def body(x_ref, w_ref, b_ref, dy_ref, dx_ref, dw_ref, db_ref, dpre_ref):
        def step1(i, carry):
            dws, db = carry
            r0, pre, taps = _conv_pre(x_ref, w_ref, b_ref, i, tr)
            dpre = dy_ref[pl.ds(r0, tr), :] * _dsilu(pre)
            dpre_ref[pl.ds(r0, tr), :] = dpre
            dws = tuple(dws[k] + jnp.sum(dpre * taps[k], axis=0, keepdims=True) for k in range(CONV_TAPS))
            return dws, db + jnp.sum(dpre, axis=0, keepdims=True)

        z = jnp.zeros((1, cw), F32)
        dws, db = lax.fori_loop(0, nt, step1, ((z,) * CONV_TAPS, z))
        for k in range(CONV_TAPS):
            dw_ref[k:k + 1, :] = dws[k]
        db_ref[...] = db

        def step2(i, carry):
            r0 = pl.multiple_of(i * tr, tr)
            cur = dpre_ref[pl.ds(r0, tr), :]
            nxt = dpre_ref[pl.ds(pl.multiple_of(jnp.minimum(r0 + tr, l - CONV_HALO), CONV_HALO), CONV_HALO), :]
            nxt = jnp.where(i < nt - 1, nxt, 0.0)
            ext = jnp.concatenate([cur, nxt], axis=0)
            acc = w_ref[CONV_TAPS - 1:CONV_TAPS, :] * cur
            for k in range(CONV_TAPS - 1):
                s = CONV_TAPS - 1 - k
                acc = acc + w_ref[k:k + 1, :] * pltpu.roll(ext, tr + CONV_HALO - s, 0)[:tr]
            dx_ref[pl.ds(r0, tr), :] = acc.astype(dx_ref.dtype)
            return carry
        lax.fori_loop(0, nt, step2, 0)

    col = pl.BlockSpec((l, cw), lambda j: (0, j))
    return pl.pallas_call(
        body, grid=(c // cw,),
        in_specs=[pl.BlockSpec((l, cw), lambda j: (0, col_off // cw + j)),
                  pl.BlockSpec((CONV_TAPS, cw), lambda j: (0, j)), pl.BlockSpec((1, cw), lambda j: (0, j)), col],
        out_specs=[col, pl.BlockSpec((CONV_TAPS, cw), lambda j: (0, j)), pl.BlockSpec((1, cw), lambda j: (0, j))],
        out_shape=[jax.ShapeDtypeStruct((l, c), BF), jax.ShapeDtypeStruct((CONV_TAPS, c), F32),
                   jax.ShapeDtypeStruct((1, c), F32)],
        scratch_shapes=[pltpu.VMEM((l, cw), F32)],
        compiler_params=_params(("parallel",)), name="conv_bwd")(pm, conv_w, conv_b, dy)


def _ssd_consts(r, p, t):
    rp, rt = r * p, r * t
    tri = np.tril(np.ones((t, t), np.float32))
    ep = np.zeros((HEAD_LANES, rp), np.float32)
    ep[np.arange(rp) // p, np.arange(rp)] = 1.0
    es = np.zeros((HEAD_LANES, rt), np.float32)
    es[np.arange(rt) // t, np.arange(rt)] = 1.0
    itile = (np.arange(t)[:, None] == (np.arange(rt) % t)[None, :]).astype(np.float32)
    lmask = (np.arange(t)[:, None] >= (np.arange(rt) % t)[None, :]).astype(np.float32)
    bmask = ((np.arange(rt) // t)[:, None] == (np.arange(rp) // p)[None, :]).astype(np.float32)
    return [jnp.asarray(a) for a in (tri, ep, ep.T.copy(), es, es.T.copy(), itile, lmask, bmask)]


def _ssd_common(xs, bm, dtr, sel, dtb, alog, tri, ep, es, itile, lmask, bmask, r, t):
    pre = _hdot(dtr, sel) + dtb
    dt = jnp.maximum(pre, 0.0) + jnp.log(1.0 + jnp.exp(-jnp.abs(pre)))
    a = -jnp.exp(alog)
    cs = _hdot(tri, dt * a)
    last = cs[t - 1:t, :]
    ecs, w = jnp.exp(cs), jnp.exp(last - cs)
    dtx, ecsx, wx = _hdot(dt, ep), _hdot(ecs, ep), _hdot(w, ep)
    csx = _hdot(cs, es)
    csrow = jnp.sum(csx * itile, axis=0, keepdims=True)
    lx = jnp.exp(jnp.where(lmask > 0.0, csx - csrow, NEG_BIG))
    xdt = xs * dtx
    xblk = (jnp.concatenate([xdt] * r, axis=0) * bmask).astype(BF)
    btile = jnp.concatenate([bm] * r, axis=0).astype(BF)
    return pre, dt, a, cs, last, ecs, w, dtx, ecsx, wx, lx, xdt, xblk, btile


def _ssd_specs(l, h, g_n, r, p, n, t, rev):
    nc = l // t
    rp = r * p
    hp = h * p
    cidx = (lambda c: nc - 1 - c) if rev else (lambda c: c)
    xs_spec = pl.BlockSpec((t, rp), lambda g, c: (cidx(c), g))
    b_spec = pl.BlockSpec((t, n), lambda g, c: (cidx(c), hp // n + g))
    c_spec = pl.BlockSpec((t, n), lambda g, c: (cidx(c), (hp + g_n * n) // n + g))
    dtr_spec = pl.BlockSpec((t, h), lambda g, c: (cidx(c), 0))
    sel_spec = pl.BlockSpec((None, h, HEAD_LANES), lambda g, c: (g, 0, 0))
    par_spec = pl.BlockSpec((None, 1, HEAD_LANES), lambda g, c: (g, 0, 0))
    st_spec = pl.BlockSpec((None, None, rp, n), lambda g, c: (cidx(c), g, 0, 0))
    return nc, xs_spec, b_spec, c_spec, dtr_spec, sel_spec, par_spec, st_spec


def _const_specs(consts):
    return [pl.BlockSpec(a.shape, lambda g, c: (0, 0)) for a in consts]


def _ssd_fwd(xbc, dtr, sel, dtb, alog, dsk, dims):
    h, g_n, r, p, n, t = dims
    l = xbc.shape[0]
    rp, hp = r * p, h * p
    nc, xs_spec, b_spec, c_spec, dtr_spec, sel_spec, par_spec, st_spec = _ssd_specs(l, h, g_n, r, p, n, t, False)
    consts = _ssd_consts(r, p, t)

    def body(xs_ref, b_ref, c_ref, dtr_ref, sel_ref, dtb_ref, alog_ref, dsk_ref,
             tri_ref, ep_ref, ept_ref, es_ref, est_ref, it_ref, lm_ref, bmk_ref, y_ref, st_ref, s_ref):
        @pl.when(pl.program_id(1) == 0)
        def _():
            s_ref[...] = jnp.zeros_like(s_ref)

        xs, bm, cm = xs_ref[...], b_ref[...], c_ref[...]
        ep = ep_ref[...]
        (_, _, _, _, last, _, _, _, ecsx, wx, lx, xdt, xblk, btile) = _ssd_common(
            xs, bm, dtr_ref[...], sel_ref[...], dtb_ref[...], alog_ref[...], tri_ref[...], ep, es_ref[...],
            it_ref[...], lm_ref[...], bmk_ref[...], r, t)
        s_in = s_ref[...]
        st_ref[...] = s_in
        cbx = _bdot(cm, btile, NT)
        yd = _bdot(cbx * lx, xblk)
        yo = ecsx * _bdot(cm, s_in, NT)
        dskx = _hdot(jnp.zeros((t, HEAD_LANES), F32) + dsk_ref[...], ep)
        y_ref[...] = yd + yo + dskx * xs
        elast = jnp.sum(ept_ref[...] * jnp.exp(last), axis=1, keepdims=True)
        s_ref[...] = elast * s_in + _bdot(xdt * wx, bm, TN)

    return pl.pallas_call(
        body, grid=(g_n, nc),
        in_specs=[xs_spec, b_spec, c_spec, dtr_spec, sel_spec, par_spec, par_spec, par_spec, *_const_specs(consts)],
        out_specs=[xs_spec, st_spec],
        out_shape=[jax.ShapeDtypeStruct((l, hp), F32), jax.ShapeDtypeStruct((nc, g_n, rp, n), F32)],
        scratch_shapes=[pltpu.VMEM((rp, n), F32)],
        compiler_params=_params(("parallel", "arbitrary")), name="ssd_fwd")(
            xbc, xbc, xbc, dtr, sel, dtb, alog, dsk, *consts)


def _ssd_bwd(xbc, dtr, sel, dtb, alog, dsk, states, dy, dims):
    h, g_n, r, p, n, t = dims
    l = xbc.shape[0]
    rp, hp = r * p, h * p
    conv = xbc.shape[1]
    nc, xs_spec, b_spec, c_spec, dtr_spec, sel_spec, par_spec, st_spec = _ssd_specs(l, h, g_n, r, p, n, t, True)
    consts = _ssd_consts(r, p, t)

    def fold_rows(v, rows):
        return sum(v[k * rows:(k + 1) * rows, :] for k in range(r))

    def body(xs_ref, b_ref, c_ref, dtr_ref, sel_ref, dtb_ref, alog_ref, dsk_ref, st_ref, dy_ref,
             tri_ref, ep_ref, ept_ref, es_ref, est_ref, it_ref, lm_ref, bmk_ref,
             dxs_ref, db_ref, dc_ref, ddt_ref, dbias_ref, dalog_ref, dd_ref, ds_ref):
        @pl.when(pl.program_id(1) == 0)
        def _():
            ds_ref[...] = jnp.zeros_like(ds_ref)
            dbias_ref[...] = jnp.zeros_like(dbias_ref)
            dalog_ref[...] = jnp.zeros_like(dalog_ref)
            dd_ref[...] = jnp.zeros_like(dd_ref)

        xs, bm, cm, dy = xs_ref[...], b_ref[...], c_ref[...], dy_ref[...]
        ep, ept, est, itile = ep_ref[...], ept_ref[...], est_ref[...], it_ref[...]
        tri = tri_ref[...]
        (pre, dt, a, cs, last, ecs, w, dtx, ecsx, wx, lx, xdt, xblk, btile) = _ssd_common(
            xs, bm, dtr_ref[...], sel_ref[...], dtb_ref[...], alog_ref[...], tri, ep, es_ref[...],
            itile, lm_ref[...], bmk_ref[...], r, t)
        s_in = st_ref[...]
        ds_out = ds_ref[...]
        elast_row = jnp.exp(last)
        elast = jnp.sum(ept * elast_row, axis=1, keepdims=True)

        dskx = _hdot(jnp.zeros((t, HEAD_LANES), F32) + dsk_ref[...], ep)
        dxs = dskx * dy
        dd_ref[...] += jnp.sum(_hdot(dy * xs, ept), axis=0, keepdims=True)

        cbx = _bdot(cm, btile, NT)
        amat = cbx * lx
        da = _bdot(dy, xblk, NT)
        dxdt = fold_rows(_bdot(amat, dy, TN) * bmk_ref[...], t)
        dcbx = da * lx
        q = da * amat
        dc = _bdot(dcbx, btile)
        db = fold_rows(_bdot(dcbx, cm, TN), t)
        dcs = _hdot(q, est) - _hdot(itile * jnp.sum(q, axis=0, keepdims=True), est)

        g0 = _bdot(cm, s_in, NT)
        dg0 = dy * ecsx
        dc = dc + _bdot(dg0, s_in)
        dcs = dcs + _hdot(dy * g0, ept) * ecs

        z = _bdot(bm, ds_out, NT)
        dxdt = dxdt + z * wx
        dw = _hdot(z * xdt, ept)
        db = db + _bdot(xdt * wx, ds_out)
        dlast = jnp.sum(jnp.sum(ds_out * s_in, axis=1, keepdims=True) * ept, axis=0, keepdims=True) * elast_row
        dww = dw * w
        dcs = dcs - dww
        dlast = dlast + jnp.sum(dww, axis=0, keepdims=True)
        ds_ref[...] = elast * ds_out + _bdot(dg0, cm, TN)

        row = lax.broadcasted_iota(jnp.int32, (t, HEAD_LANES), 0)
        dcs = dcs + jnp.where(row == t - 1, dlast, 0.0)
        dadt = _hdot(tri, dcs, TN)
        dalog_ref[...] += jnp.sum(dadt * dt, axis=0, keepdims=True) * a
        ddt = dadt * a + _hdot(dxdt * xs, ept)
        dpre = ddt * (1.0 / (1.0 + jnp.exp(-pre)))
        ddt_ref[...] = dpre
        dbias_ref[...] += jnp.sum(dpre, axis=0, keepdims=True)
        dxs_ref[...] = dxs + dxdt * dtx
        db_ref[...] = db
        dc_ref[...] = dc

    cidx = lambda c: nc - 1 - c
    return pl.pallas_call(
        body, grid=(g_n, nc),
        in_specs=[xs_spec, b_spec, c_spec, dtr_spec, sel_spec, par_spec, par_spec, par_spec, st_spec, xs_spec,
                  *_const_specs(consts)],
        out_specs=[xs_spec, pl.BlockSpec((t, n), lambda g, c: (cidx(c), g)),
                   pl.BlockSpec((t, n), lambda g, c: (cidx(c), g)),
                   pl.BlockSpec((None, t, HEAD_LANES), lambda g, c: (g, cidx(c), 0)), par_spec, par_spec, par_spec],
        out_shape=[jax.ShapeDtypeStruct((l, hp), F32), jax.ShapeDtypeStruct((l, g_n * n), F32),
                   jax.ShapeDtypeStruct((l, g_n * n), F32), jax.ShapeDtypeStruct((g_n, l, HEAD_LANES), F32),
                   jax.ShapeDtypeStruct((g_n, 1, HEAD_LANES), F32), jax.ShapeDtypeStruct((g_n, 1, HEAD_LANES), F32),
                   jax.ShapeDtypeStruct((g_n, 1, HEAD_LANES), F32)],
        scratch_shapes=[pltpu.VMEM((rp, n), F32)],
        compiler_params=_params(("parallel", "arbitrary")), name="ssd_bwd")(
            xbc, xbc, xbc, dtr, sel, dtb, alog, dsk, states, dy, *consts)


def _gate_norm_fwd(y, pm, g, tm=128):
    l, di = y.shape
    tm = _t(l, tm)

    def body(y_ref, z_ref, g_ref, o_ref, r_ref):
        v = y_ref[...] * _silu(z_ref[...])
        rr = lax.rsqrt(jnp.mean(v * v, axis=-1, keepdims=True) + EPS)
        o_ref[...] = (v * rr * g_ref[...]).astype(BF)
        r_ref[...] = rr

    row = pl.BlockSpec((tm, di), lambda i: (i, 0))
    return pl.pallas_call(
        body, grid=(l // tm,), in_specs=[row, row, pl.BlockSpec((1, di), lambda i: (0, 0))],
        out_specs=[row, pl.BlockSpec((tm, 1), lambda i: (i, 0))],
        out_shape=[jax.ShapeDtypeStruct((l, di), BF), jax.ShapeDtypeStruct((l, 1), F32)],
        compiler_params=_params(("parallel",)), name="gate_norm_fwd")(y, pm, g)


def _gate_norm_bwd(dyn, y, pm, rstd, g, tm=128):
    l, di = y.shape
    tm = _t(l, tm)

    def body(dyn_ref, y_ref, z_ref, r_ref, g_ref, dy_ref, dz_ref, dg_ref):
        i = pl.program_id(0)
        yv, zv, rr = y_ref[...], z_ref[...], r_ref[...]
        sz = _silu(zv)
        vhat = yv * sz * rr
        dn = dyn_ref[...]
        dvh = dn * g_ref[...]
        dv = rr * (dvh - vhat * jnp.mean(dvh * vhat, axis=-1, keepdims=True))
        dy_ref[...] = dv * sz
        dz_ref[...] = (dv * yv * _dsilu(zv)).astype(BF)

        @pl.when(i == 0)
        def _():
            dg_ref[...] = jnp.zeros_like(dg_ref)

        dg_ref[...] += jnp.sum(dn * vhat, axis=0, keepdims=True)

    row = pl.BlockSpec((tm, di), lambda i: (i, 0))
    vec = pl.BlockSpec((1, di), lambda i: (0, 0))
    return pl.pallas_call(
        body, grid=(l // tm,), in_specs=[row, row, row, pl.BlockSpec((tm, 1), lambda i: (i, 0)), vec],
        out_specs=[row, row, vec],
        out_shape=[jax.ShapeDtypeStruct((l, di), F32), jax.ShapeDtypeStruct((l, di), BF),
                   jax.ShapeDtypeStruct((1, di), F32)],
        compiler_params=_params(("arbitrary",)), name="gate_norm_bwd")(dyn, y, pm, rstd, g)


POOL_HALO = 16


def _pool_mix(name, src, dp, backward, out_dtype, cw=256, tr=512):
    l = src.shape[0]
    gd = dp // len(POOL_WINDOWS)
    cwl, trl = _t(gd, cw), _t(l, tr)
    nt = l // trl

    def body(x_ref, o_ref):
        gi = pl.program_id(0)
        for wi, win in enumerate(POOL_WINDOWS):
            @pl.when(gi == wi)
            def _(win=win):
                def step(i, carry):
                    r0 = pl.multiple_of(i * trl, trl)
                    cur = x_ref[pl.ds(r0, trl), :]
                    trow = r0 + lax.broadcasted_iota(jnp.int32, (trl, 1), 0)
                    cnt = jnp.minimum(trow + 1, win).astype(F32)
                    if not backward:
                        halo = x_ref[pl.ds(pl.multiple_of(jnp.maximum(r0 - POOL_HALO, 0), POOL_HALO), POOL_HALO), :]
                        halo = jnp.where(i > 0, halo, 0.0)
                        s = jnp.concatenate([halo, cur], axis=0)
                        sh = 1
                        while sh < win:
                            s = s + pltpu.roll(s, sh, 0)
                            sh *= 2
                        res = s[POOL_HALO:] / cnt - cur
                    else:
                        halo = x_ref[pl.ds(pl.multiple_of(jnp.minimum(r0 + trl, l - POOL_HALO), POOL_HALO),
                                           POOL_HALO), :]
                        hrow = r0 + trl + lax.broadcasted_iota(jnp.int32, (POOL_HALO, 1), 0)
                        hcnt = jnp.minimum(hrow + 1, win).astype(F32)
                        halo = jnp.where(i < nt - 1, halo / hcnt, 0.0)
                        s = jnp.concatenate([cur / cnt, halo], axis=0)
                        sh = 1
                        while sh < win:
                            s = s + pltpu.roll(s, trl + POOL_HALO - sh, 0)
                            sh *= 2
                        res = s[:trl] - cur
                    o_ref[pl.ds(r0, trl), :] = res.astype(o_ref.dtype)
                    return carry
                lax.fori_loop(0, nt, step, 0)

    return pl.pallas_call(
        body, grid=(len(POOL_WINDOWS), gd // cwl),
        in_specs=[pl.BlockSpec((l, cwl), lambda g, j: (0, g * (gd // cwl) + j))],
        out_specs=pl.BlockSpec((l, cwl), lambda g, j: (0, g * (gd // cwl) + j)),
        out_shape=jax.ShapeDtypeStruct((l, dp), out_dtype),
        compiler_params=_params(("parallel", "parallel")), name=name)(src)


def _pool_gate_bwd(dyp, mg, pu, scale, tm=128):
    l, dp = mg.shape
    tm = _t(l, tm)

    def body(dyp_ref, mg_ref, gate_ref, sc_ref, dmg_ref, dgate_ref, dsc_ref):
        i = pl.program_id(0)
        d, m, gt, sc = dyp_ref[...], mg_ref[...], gate_ref[...], sc_ref[...]
        sg = _silu(gt)
        dmg_ref[...] = (d * sc * sg).astype(BF)
        dgate_ref[...] = (d * m * sc * _dsilu(gt)).astype(BF)

        @pl.when(i == 0)
        def _():
            dsc_ref[...] = jnp.zeros_like(dsc_ref)

        dsc_ref[...] += jnp.sum(d * m * sg, axis=0, keepdims=True)

    row = pl.BlockSpec((tm, dp), lambda i: (i, 0))
    vec = pl.BlockSpec((1, dp), lambda i: (0, 0))
    return pl.pallas_call(
        body, grid=(l // tm,), in_specs=[row, row, pl.BlockSpec((tm, dp), lambda i: (i, 1)), vec],
        out_specs=[row, row, vec],
        out_shape=[jax.ShapeDtypeStruct((l, dp), BF), jax.ShapeDtypeStruct((l, dp), BF),
                   jax.ShapeDtypeStruct((1, dp), F32)],
        compiler_params=_params(("arbitrary",)), name="pool_gate_bwd")(dyp, mg, pu, scale)


def _group_mm_fwd(mp, wg, pu, scale, tm=1024, tn=1024, tk=1024):
    l, dp = mp.shape
    ng, gd = wg.shape[0], wg.shape[1]
    tm, tn, tk = _t(l, tm), _t(gd, tn), _t(gd, tk)

    def epi(acc, gate, sc):
        return acc, acc * sc * _silu(gate)

    out = pl.BlockSpec((tm, tn), lambda g, i, j, k: (i, g * (gd // tn) + j))
    return _mm("group_mm_fwd", (ng, l // tm, gd // tn, gd // tk),
               mp, pl.BlockSpec((tm, tk), lambda g, i, j, k: (i, g * (gd // tk) + k)),
               wg, pl.BlockSpec((None, tk, tn), lambda g, i, j, k: (g, k, j)), NN,
               [jax.ShapeDtypeStruct((l, dp), F32), jax.ShapeDtypeStruct((l, dp), BF)], [out, out], (tm, tn),
               (pu, scale),
               (pl.BlockSpec((tm, tn), lambda g, i, j, k: (i, (dp + g * gd) // tn + j)),
                pl.BlockSpec((1, tn), lambda g, i, j, k: (0, g * (gd // tn) + j))), epi)


def _group_mm_bwd_data(dmg, wg, tm=1024, tn=1024, tk=1024):
    l, dp = dmg.shape
    ng, gd = wg.shape[0], wg.shape[1]
    tm, tn, tk = _t(l, tm), _t(gd, tn), _t(gd, tk)
    return _mm("group_mm_bwd_data", (ng, l // tm, gd // tn, gd // tk),
               dmg, pl.BlockSpec((tm, tk), lambda g, i, j, k: (i, g * (gd // tk) + k)),
               wg, pl.BlockSpec((None, tn, tk), lambda g, i, j, k: (g, j, k)), NT,
               [jax.ShapeDtypeStruct((l, dp), F32)],
               [pl.BlockSpec((tm, tn), lambda g, i, j, k: (i, g * (gd // tn) + j))], (tm, tn))[0]


def _group_mm_bwd_weight(mp, dmg, ng, tm=512, tk=1024):
    l, dp = mp.shape
    gd = dp // ng
    tm, tk = _t(gd, tm), _t(l, tk)
    return _mm("group_mm_bwd_weight", (ng, gd // tm, l // tk),
               mp, pl.BlockSpec((tk, tm), lambda g, i, k: (k, g * (gd // tm) + i)),
               dmg, pl.BlockSpec((tk, gd), lambda g, i, k: (k, g)), TN,
               [jax.ShapeDtypeStruct((ng, gd, gd // 2), F32)],
               [pl.BlockSpec((None, tm, gd // 2), lambda g, i, k: (g, i, 0))], (tm, gd), epi=_pack_epi)[0]


def _cast_bf16(name, w, tr=256):
    r, c = w.shape
    tr = _row_tile(r, tr, 16)

    def body(w_ref, o_ref):
        o_ref[...] = w_ref[...].astype(BF)

    blk = pl.BlockSpec((tr, c), lambda i: (i, 0))
    return pl.pallas_call(body, grid=(r // tr,), in_specs=[blk], out_specs=blk,
                          out_shape=jax.ShapeDtypeStruct((r, c), BF), compiler_params=_params(("parallel",)),
                          name=name)(w)


def _pack_rows(name, w, tr=256):
    r, c = w.shape
    tr = _row_tile(r, tr)

    def body(w_ref, o_ref):
        o_ref[...] = _pack_pairs(w_ref[...])

    return pl.pallas_call(body, grid=(r // tr,), in_specs=[pl.BlockSpec((tr, c), lambda i: (i, 0))],
                          out_specs=pl.BlockSpec((tr, c // 2), lambda i: (i, 0)),
                          out_shape=jax.ShapeDtypeStruct((r, c // 2), F32), compiler_params=_params(("parallel",)),
                          name=name)(w)


def _unpack_rows(name, w, tr=512):
    r, h = w.shape
    tr = _row_tile(r, tr, 16)

    def body(w_ref, o_ref):
        hi, lo = _unpack_pairs(w_ref[...])
        o_ref[:, :h] = hi.astype(BF)
        o_ref[:, h:] = lo.astype(BF)

    return pl.pallas_call(body, grid=(r // tr,), in_specs=[pl.BlockSpec((tr, h), lambda i: (i, 0))],
                          out_specs=pl.BlockSpec((tr, 2 * h), lambda i: (i, 0)),
                          out_shape=jax.ShapeDtypeStruct((r, 2 * h), BF), compiler_params=_params(("parallel",)),
                          name=name)(w)


def _reduce_packed(name, recv, tr=256):
    nd, r, h = recv.shape
    tr = _row_tile(r, tr)

    def body(p_ref, o_ref):
        hi, lo = _unpack_pairs(p_ref[0])
        for k in range(1, nd):
            a, b = _unpack_pairs(p_ref[k])
            hi, lo = hi + a, lo + b
        o_ref[:, :h] = hi
        o_ref[:, h:] = lo

    return pl.pallas_call(body, grid=(r // tr,), in_specs=[pl.BlockSpec((nd, tr, h), lambda i: (0, i, 0))],
                          out_specs=pl.BlockSpec((tr, 2 * h), lambda i: (i, 0)),
                          out_shape=jax.ShapeDtypeStruct((r, 2 * h), F32), compiler_params=_params(("parallel",)),
                          name=name)(recv)


def _adamw_math(w, g, m, v):
    m2 = ADAM_B1 * m + (1.0 - ADAM_B1) * g
    v2 = ADAM_B2 * v + (1.0 - ADAM_B2) * (g * g)
    m_hat = m2 / (1.0 - ADAM_B1 ** ADAM_STEP)
    v_hat = v2 / (1.0 - ADAM_B2 ** ADAM_STEP)
    delta = -ADAM_LR * (m_hat / (jnp.sqrt(v_hat) + ADAM_EPS) + ADAM_WD * w)
    return delta, m2, v2


def _adamw(name, w, g, m, v, tr=256):
    r, c = w.shape
    tr = _row_tile(r, tr)

    def body(w_ref, g_ref, m_ref, v_ref, d_ref, m2_ref, v2_ref):
        d, m2, v2 = _adamw_math(w_ref[...], g_ref[...], m_ref[...], v_ref[...])
        d_ref[...] = d
        m2_ref[...] = m2
        v2_ref[...] = v2

    blk = pl.BlockSpec((tr, c), lambda i: (i, 0))
    shp = jax.ShapeDtypeStruct((r, c), F32)
    return pl.pallas_call(body, grid=(r // tr,), in_specs=[blk] * 4, out_specs=[blk] * 3, out_shape=[shp] * 3,
                          compiler_params=_params(("parallel",)), name=name)(w, g, m, v)


def _sum_slots(name, a):
    nd, r, c = a.shape

    def body(a_ref, o_ref):
        s = a_ref[0]
        for k in range(1, nd):
            s = s + a_ref[k]
        o_ref[...] = s

    return pl.pallas_call(body, out_shape=jax.ShapeDtypeStruct((r, c), F32), name=name)(a)


def _mesh_pos():
    return lax.axis_index("x"), lax.axis_index("y"), lax.axis_index("c")


def _all_gather(name, shards):
    n_arr = len(shards)

    def body(*refs):
        src = refs[:n_arr]
        dst = refs[n_arr:2 * n_arr]
        send_sems, recv_sems, local_sems = refs[2 * n_arr:]
        x, y, c = _mesh_pos()
        me, sibling = (x, y, c), (x, y, 1 - c)
        chips = [(1 - x, y), (x, 1 - y), (1 - x, 1 - y)]

        def slot(pos):
            return 4 * pos[0] + 2 * pos[1] + pos[2]

        def copy(a, k, block, to, from_src):
            return pltpu.make_async_remote_copy(
                src_ref=src[a] if from_src else dst[a].at[slot(block)], dst_ref=dst[a].at[slot(block)],
                send_sem=send_sems.at[a * 7 + k], recv_sem=recv_sems.at[a * 7 + k], device_id=to, device_id_type=MESH)

        mine = [pltpu.make_async_copy(src[a], dst[a].at[slot(me)], local_sems.at[a]) for a in range(n_arr)]
        for cp in mine:
            cp.start()
        first = []
        for a in range(n_arr):
            first.append(copy(a, 0, me, sibling, True))
            first += [copy(a, 1 + j, me, (*chip, c), True) for j, chip in enumerate(chips)]
        for cp in first:
            cp.start()
        passed = []
        for j, chip in enumerate(chips):
            for a in range(n_arr):
                copy(a, 1 + j, (*chip, c), me, False).wait_recv()
                fwd = copy(a, 4 + j, (*chip, c), sibling, False)
                fwd.start()
                passed.append(fwd)
        for a in range(n_arr):
            copy(a, 0, sibling, me, False).wait_recv()
            for j, chip in enumerate(chips):
                copy(a, 4 + j, (*chip, 1 - c), me, False).wait_recv()
        for cp in first + passed:
            cp.wait_send()
        for cp in mine:
            cp.wait()

    hbm = pl.BlockSpec(memory_space=pl.ANY)
    return pl.pallas_call(
        body, in_specs=[hbm] * n_arr, out_specs=[hbm] * n_arr,
        out_shape=[jax.ShapeDtypeStruct((N_DEV, *s.shape), s.dtype) for s in shards],
        scratch_shapes=[pltpu.SemaphoreType.DMA((n_arr * 7,)), pltpu.SemaphoreType.DMA((n_arr * 7,)),
                        pltpu.SemaphoreType.DMA((n_arr,))],
        compiler_params=pltpu.CompilerParams(has_side_effects=True), name=name)(*shards)


def _all_to_all(name, blocks):
    n_arr = len(blocks)

    def body(*refs):
        src = refs[:n_arr]
        dst = refs[n_arr:2 * n_arr]
        send_sems, recv_sems, local_sems = refs[2 * n_arr:]
        x, y, c = _mesh_pos()
        my_slot = 4 * x + 2 * y + c
        peers = []
        for k in range(1, N_DEV):
            fx, fy, fc = (k >> 2) & 1, (k >> 1) & 1, k & 1
            peers.append((1 - x if fx else x, 1 - y if fy else y, 1 - c if fc else c))

        def slot(pos):
            return 4 * pos[0] + 2 * pos[1] + pos[2]

        def copy(a, k, peer, sending):
            return pltpu.make_async_remote_copy(
                src_ref=src[a].at[slot(peer)], dst_ref=dst[a].at[my_slot if sending else slot(peer)],
                send_sem=send_sems.at[a * 7 + k], recv_sem=recv_sems.at[a * 7 + k], device_id=peer,
                device_id_type=MESH)

        mine = [pltpu.make_async_copy(src[a].at[my_slot], dst[a].at[my_slot], local_sems.at[a])
                for a in range(n_arr)]
        for cp in mine:
            cp.start()
        sends = [copy(a, k, peer, True) for a in range(n_arr) for k, peer in enumerate(peers)]
        for cp in sends:
            cp.start()
        for a in range(n_arr):
            for k, peer in enumerate(peers):
                copy(a, k, peer, False).wait_recv()
        for cp in sends:
            cp.wait_send()
        for cp in mine:
            cp.wait()

    hbm = pl.BlockSpec(memory_space=pl.ANY)
    return pl.pallas_call(
        body, in_specs=[hbm] * n_arr, out_specs=[hbm] * n_arr,
        out_shape=[jax.ShapeDtypeStruct(b.shape, b.dtype) for b in blocks],
        scratch_shapes=[pltpu.SemaphoreType.DMA((n_arr * 7,)), pltpu.SemaphoreType.DMA((n_arr * 7,)),
                        pltpu.SemaphoreType.DMA((n_arr,))],
        compiler_params=pltpu.CompilerParams(has_side_effects=True), name=name)(*blocks)


def _head_rows(v, g_n, r):
    return jnp.pad(v.reshape(g_n, 1, r), ((0, 0), (0, 0), (0, HEAD_LANES - r)))


def _local_step(x, target, ln_g, final_g, wt_in, conv_w, conv_b, dt_bias, a_log, d_skip, norm_g, w_out,
                wp_in, wg, scale, wp_out):
    l, d = x.shape
    di = w_out.shape[0]
    h = dt_bias.shape[1]
    p = di // h
    conv = conv_b.shape[1]
    n, t = SSM_STATE, SSD_CHUNK
    g_n = (conv - di) // (2 * n)
    r = h // g_n
    nm = di + conv
    dp = wp_out.shape[0]
    ng = wg.shape[0]
    dims = (h, g_n, r, p, n, t)
    pw = _t(d, 1024)

    h0, r0 = _rms_fwd("rms0_fwd", x, ln_g[0:1])
    pm = _mm_nt("in_proj_main", h0, wt_in, 0, nm)
    dtr = _mm_nt("in_proj_dt", h0, wt_in, nm, h, tn=h)
    xbc = _conv_fwd(pm, di, conv_w, conv_b)
    sel = np.zeros((g_n, h, HEAD_LANES), np.float32)
    for gg in range(g_n):
        sel[gg, gg * r + np.arange(r), np.arange(r)] = 1.0
    sel = jnp.asarray(sel)
    dtb, alog, dsk = _head_rows(dt_bias, g_n, r), _head_rows(a_log, g_n, r), _head_rows(d_skip, g_n, r)
    y, states = _ssd_fwd(xbc, dtr, sel, dtb, alog, dsk, dims)
    yn, r_n = _gate_norm_fwd(y, pm, norm_g)
    x1 = _mm_nn("ssm_out_proj", yn, w_out, add=x)

    h1, r1 = _rms_fwd("rms1_fwd", x1, ln_g[1:2])
    pu = _mm_nn("pool_in_proj", h1, wp_in, tn=512)
    mp = _pool_mix("pool_mix_fwd", pu, dp, False, BF)
    mg, yp = _group_mm_fwd(mp, wg, pu, scale)
    x2 = _mm_nn("pool_out_proj", yp, wp_out, add=x1)

    dx2, d_final_g, loss = _loss_head(x2, final_g, target)

    dyp = _mm_nt("pool_out_bwd_data", dx2, wp_out, 0, dp, tn=1024)
    gw_pout = _mm_tn_packed("pool_out_bwd_weight", yp, dx2, pw)
    dmg, dgate, d_scale = _pool_gate_bwd(dyp, mg, pu, scale)
    dmp = _group_mm_bwd_data(dmg, wg)
    gw_g = _group_mm_bwd_weight(mp, dmg, ng)
    du = _pool_mix("pool_mix_bwd", dmp, dp, True, BF)
    dh1 = _mm_nt_k("pool_in_bwd_data_u", du, wp_in, 0)
    dh1 = _mm_nt_k("pool_in_bwd_data_gate", dgate, wp_in, dp, add=dh1)
    pw_in = 2 * dp // N_DEV
    gw_pin = jnp.concatenate([_mm_tn_packed("pool_in_bwd_weight_u", h1, du, pw_in),
                              _mm_tn_packed("pool_in_bwd_weight_gate", h1, dgate, pw_in)], axis=0)
    dx1, d_ln1 = _rms_bwd("rms1_bwd", dh1, x1, r1, ln_g[1:2], dx2)

    dyn = _mm_nt("ssm_out_bwd_data", dx1, w_out, 0, di, tn=1024)
    gw_out = _mm_tn_packed("ssm_out_bwd_weight", yn, dx1, pw)
    dy, dz, d_norm_g = _gate_norm_bwd(dyn, y, pm, r_n, norm_g)
    dxs, db, dc, ddt_g, dbias_g, dalog_g, dd_g = _ssd_bwd(xbc, dtr, sel, dtb, alog, dsk, states, dy, dims)
    dxbc = jnp.concatenate([dxs, db, dc], axis=1)
    dxr, d_conv_w, d_conv_b = _conv_bwd(pm, di, conv_w, conv_b, dxbc)
    ddt = jnp.transpose(ddt_g[:, :, :r], (1, 0, 2)).reshape(l, h)
    dh0 = _mm_nn("in_proj_bwd_data_z", dz, wt_in, 0)
    dh0 = _mm_nn("in_proj_bwd_data_xbc", dxr, wt_in, di, add=dh0)
    dh0 = _mm_nn("in_proj_bwd_data_dt", ddt, wt_in, nm, add=dh0)
    gwt_in = jnp.concatenate([_mm_tn_packed("in_proj_bwd_weight_z", dz, h0, pw),
                              _mm_tn_packed("in_proj_bwd_weight_xbc", dxr, h0, pw),
                              _mm_tn_packed("in_proj_bwd_weight_dt", ddt, h0, pw)], axis=1)
    grad_x, d_ln0 = _rms_bwd("rms0_bwd", dh0, x, r0, ln_g[0:1], dx1)

    def heads(v):
        return v[:, 0, :r].reshape(1, h)

    small = dict(ln_g=jnp.concatenate([d_ln0, d_ln1], axis=0), final_g=d_final_g, conv_w=d_conv_w, conv_b=d_conv_b,
                 dt_bias=heads(dbias_g), a_log=heads(dalog_g), d_skip=heads(dd_g), norm_g=d_norm_g, scale=d_scale)
    big = dict(wt_in=gwt_in, w_out=gw_out, wp_in=gw_pin, wg=gw_g, wp_out=gw_pout)
    return loss, grad_x, small, big


SMALL_ORDER = ("ln_g", "final_g", "conv_w", "conv_b", "dt_bias", "a_log", "d_skip", "norm_g", "scale", "loss")


def _flatten_small(parts):
    flat = jnp.concatenate([parts[k].reshape(-1) for k in SMALL_ORDER])
    n = flat.shape[0]
    rows = -(-n // 1024) * 8
    return jnp.pad(flat, (0, rows * 128 - n)).reshape(rows, 128)


def _split_small(flat, shapes):
    flat = flat.reshape(-1)
    out, off = {}, 0
    for k in SMALL_ORDER:
        size = int(np.prod(shapes[k]))
        out[k] = flat[off:off + size].reshape(shapes[k])
        off += size
    return out


def kernel(x, ln_g, final_g, ssm_w_in, ssm_conv_w, ssm_conv_b, ssm_dt_bias, ssm_a_log, ssm_d, ssm_norm_g, ssm_w_out, pool_w_in, pool_w_group, pool_scale, pool_w_out, loss_target, m_ln_g, m_final_g, m_ssm_w_in, m_ssm_conv_w, m_ssm_conv_b, m_ssm_dt_bias, m_ssm_a_log, m_ssm_d, m_ssm_norm_g, m_ssm_w_out, m_pool_w_in, m_pool_w_group, m_pool_scale, m_pool_w_out, v_ln_g, v_final_g, v_ssm_w_in, v_ssm_conv_w, v_ssm_conv_b, v_ssm_dt_bias, v_ssm_a_log, v_ssm_d, v_ssm_norm_g, v_ssm_w_out, v_pool_w_in, v_pool_w_group, v_pool_scale, v_pool_w_out):
    l, d = x.shape[1], x.shape[2]
    me = 4 * lax.axis_index("x") + 2 * lax.axis_index("y") + lax.axis_index("c")
    ng, gds, gd = pool_w_group.shape[1], pool_w_group.shape[2], pool_w_group.shape[3]
    sin_s = ssm_w_in.shape[2]
    dp_s = pool_w_out.shape[1]
    conv_s = ssm_conv_w.shape[2]

    wt_in_s = _pack_rows("pack_w_in", jnp.transpose(ssm_w_in[0]))
    w_out_s = _cast_bf16("cast_w_out", ssm_w_out[0])
    wp_in_s = _cast_bf16("cast_pool_w_in", pool_w_in[0])
    wg_s = _cast_bf16("cast_pool_w_group", pool_w_group[0].reshape(ng * gds, gd))
    wp_out_s = _cast_bf16("cast_pool_w_out", pool_w_out[0])
    small_s = jnp.concatenate([ssm_conv_w[0].reshape(-1), pool_scale[0]])
    n_small = small_s.shape[0]
    small_s = jnp.pad(small_s, (0, -(-n_small // 1024) * 1024 - n_small)).reshape(-1, 128)
    wt_in_g, w_out_g, wp_in_g, wg_g, wp_out_g, small_g = _all_gather(
        "all_gather_weights", [wt_in_s, w_out_s, wp_in_s, wg_s, wp_out_s, small_s])

    wt_in = _unpack_rows("unpack_w_in", wt_in_g.reshape(N_DEV * sin_s, d // 2))
    w_out = w_out_g.reshape(-1, d)
    wp_in = jnp.transpose(wp_in_g, (1, 0, 2)).reshape(d, -1)
    wg = jnp.transpose(wg_g.reshape(N_DEV, ng, gds, gd), (1, 0, 2, 3)).reshape(ng, gd, gd)
    wp_out = wp_out_g.reshape(-1, d)
    small_all = small_g.reshape(N_DEV, -1)[:, :n_small]
    conv_w = jnp.transpose(small_all[:, :CONV_TAPS * conv_s].reshape(N_DEV, CONV_TAPS, conv_s), (1, 0, 2))
    conv_w = conv_w.reshape(CONV_TAPS, -1)
    scale = small_all[:, CONV_TAPS * conv_s:].reshape(1, -1)

    loss, grad_x, small, big = _local_step(
        x[0], loss_target[0], ln_g, final_g.reshape(1, d), wt_in, conv_w, ssm_conv_b, ssm_dt_bias, ssm_a_log, ssm_d,
        ssm_norm_g, w_out, wp_in, wg, scale, wp_out)

    def rows_major(gp):
        q, _, hw = gp.shape
        return jnp.transpose(gp.reshape(q, N_DEV, -1, hw), (1, 0, 2, 3))

    send = [rows_major(big["wt_in"]), rows_major(big["w_out"]), big["wp_in"][:, None],
            jnp.transpose(big["wg"].reshape(ng, N_DEV, gds, gd // 2), (1, 0, 2, 3)), rows_major(big["wp_out"])]
    recv = _all_to_all("exchange_weight_grads", send)

    def reduced(name, rv):
        q = rv.shape[1]
        cols = [_reduce_packed(f"{name}_{j}", rv[:, j]) for j in range(q)]
        return cols[0] if q == 1 else jnp.concatenate(cols, axis=1)

    g_w_in = jnp.transpose(reduced("reduce_w_in", recv[0]))[None]
    g_w_out = reduced("reduce_w_out", recv[1])[None]
    g_wp_in = reduced("reduce_pool_w_in", recv[2])[None]
    g_wg = _reduce_packed("reduce_pool_w_group", recv[3].reshape(N_DEV, ng * gds, gd // 2)).reshape(1, ng, gds, gd)
    g_wp_out = reduced("reduce_pool_w_out", recv[4])[None]

    small["loss"] = loss
    shapes = {k: small[k].shape for k in SMALL_ORDER}
    summed = _split_small(_sum_slots("sum_small_grads", _all_gather("all_gather_small_grads",
                                                                      [_flatten_small(small)])[0]), shapes)
    g_conv_w = lax.dynamic_slice_in_dim(summed["conv_w"], me * conv_s, conv_s, axis=1)[None]
    g_scale = lax.dynamic_slice_in_dim(summed["scale"], me * dp_s, dp_s, axis=1)
    grads = dict(ln_g=summed["ln_g"], final_g=summed["final_g"].reshape(d), ssm_w_in=g_w_in, ssm_conv_w=g_conv_w,
                 ssm_conv_b=summed["conv_b"], ssm_dt_bias=summed["dt_bias"], ssm_a_log=summed["a_log"],
                 ssm_d=summed["d_skip"], ssm_norm_g=summed["norm_g"], ssm_w_out=g_w_out, pool_w_in=g_wp_in,
                 pool_w_group=g_wg, pool_scale=g_scale, pool_w_out=g_wp_out)

    weights = dict(ln_g=ln_g, final_g=final_g, ssm_w_in=ssm_w_in, ssm_conv_w=ssm_conv_w, ssm_conv_b=ssm_conv_b,
                   ssm_dt_bias=ssm_dt_bias, ssm_a_log=ssm_a_log, ssm_d=ssm_d, ssm_norm_g=ssm_norm_g,
                   ssm_w_out=ssm_w_out, pool_w_in=pool_w_in, pool_w_group=pool_w_group, pool_scale=pool_scale,
                   pool_w_out=pool_w_out)
    m_in = dict(ln_g=m_ln_g, final_g=m_final_g, ssm_w_in=m_ssm_w_in, ssm_conv_w=m_ssm_conv_w, ssm_conv_b=m_ssm_conv_b,
                ssm_dt_bias=m_ssm_dt_bias, ssm_a_log=m_ssm_a_log, ssm_d=m_ssm_d, ssm_norm_g=m_ssm_norm_g,
                ssm_w_out=m_ssm_w_out, pool_w_in=m_pool_w_in, pool_w_group=m_pool_w_group, pool_scale=m_pool_scale,
                pool_w_out=m_pool_w_out)
    v_in = dict(ln_g=v_ln_g, final_g=v_final_g, ssm_w_in=v_ssm_w_in, ssm_conv_w=v_ssm_conv_w, ssm_conv_b=v_ssm_conv_b,
                ssm_dt_bias=v_ssm_dt_bias, ssm_a_log=v_ssm_a_log, ssm_d=v_ssm_d, ssm_norm_g=v_ssm_norm_g,
                ssm_w_out=v_ssm_w_out, pool_w_in=v_pool_w_in, pool_w_group=v_pool_w_group, pool_scale=v_pool_scale,
                pool_w_out=v_pool_w_out)
    names = list(weights)
    big_names = ("ssm_w_in", "ssm_w_out", "pool_w_in", "pool_w_group", "pool_w_out")
    delta, new_m, new_v = {}, {}, {}
    for k in big_names:
        shp = weights[k].shape
        two_d = (-1, shp[-1])
        dk, mk, vk = _adamw(f"adamw_{k}", weights[k].reshape(two_d), grads[k].reshape(two_d), m_in[k].reshape(two_d),
                            v_in[k].reshape(two_d))
        delta[k], new_m[k], new_v[k] = dk.reshape(shp), mk.reshape(shp), vk.reshape(shp)
    small_names = [k for k in names if k not in big_names]

    def packed(tree):
        flat = jnp.concatenate([tree[k].reshape(-1) for k in small_names])
        nn = flat.shape[0]
        return jnp.pad(flat, (0, -(-nn // 1024) * 1024 - nn), constant_values=1.0).reshape(-1, 128)

    ds, ms, vs = _adamw("adamw_small", packed(weights), packed(grads), packed(m_in), packed(v_in))
    off = 0
    for k in small_names:
        shp = weights[k].shape
        size = int(np.prod(shp))
        for res, arr in ((delta, ds), (new_m, ms), (new_v, vs)):
            res[k] = arr.reshape(-1)[off:off + size].reshape(shp)
        off += size

    return (summed["loss"].reshape(()), grad_x[None], *[grads[k] for k in names], *[delta[k] for k in names],
            *[new_m[k] for k in names], *[new_v[k] for k in names])
```

```python
import math

import jax
import jax.numpy as jnp
import numpy as np
from jax import lax
from jax.experimental import pallas as pl
from jax.experimental.pallas import tpu as pltpu

F32 = jnp.float32
BF = jnp.bfloat16
U32 = jnp.uint32

N_DEV = 8
EPS = 1e-6
SSD_CHUNK = 64
SSM_STATE = 128
CONV_TAPS = 4
HEAD_LANES = 128
POOL_WINDOWS = (2, 4, 8, 16)
ADAM_LR, ADAM_B1, ADAM_B2, ADAM_EPS, ADAM_WD, ADAM_STEP = 0.001, 0.9, 0.999, 1e-08, 0.01, 10
VMEM_LIMIT = 56 * 1024 * 1024
NEG_BIG = -1e30

NN = ((1,), (0,))
NT = ((1,), (1,))
TN = ((0,), (0,))
MESH = pl.DeviceIdType.MESH


def _t(dim, pref):
    return pref if dim % pref == 0 else dim


def _row_tile(rows, pref, mult=8):
    best = rows
    for cand in range(mult, min(rows, pref) + 1, mult):
        if rows % cand == 0:
            best = cand
    return best


def _params(sem=None):
    return pltpu.CompilerParams(dimension_semantics=sem, vmem_limit_bytes=VMEM_LIMIT)


def _silu(x):
    return x * (1.0 / (1.0 + jnp.exp(-x)))


def _dsilu(x):
    s = 1.0 / (1.0 + jnp.exp(-x))
    return s * (1.0 + x * (1.0 - s))


def _hdot(a, b, dims=NN):
    return lax.dot_general(a, b, (dims, ((), ())), precision=lax.Precision.HIGHEST, preferred_element_type=F32)


def _bdot(a, b, dims=NN):
    return lax.dot_general(a.astype(BF), b.astype(BF), (dims, ((), ())), preferred_element_type=F32)


def _pack_pairs(x):
    h = x.shape[1] // 2
    hi = lax.bitcast_convert_type(x[:, :h].astype(jnp.bfloat16).astype(F32), U32)
    lo = lax.bitcast_convert_type(x[:, h:].astype(jnp.bfloat16).astype(F32), U32)
    return lax.bitcast_convert_type(hi | (lo >> 16), F32)


def _unpack_pairs(w):
    u = lax.bitcast_convert_type(w, U32)
    hi = lax.bitcast_convert_type(u & jnp.uint32(0xFFFF0000), F32)
    lo = lax.bitcast_convert_type(u << 16, F32)
    return hi, lo


def _mesh_pos():
    return lax.axis_index("x"), lax.axis_index("y"), lax.axis_index("c")


def _slot(pos):
    return 4 * pos[0] + 2 * pos[1] + pos[2]


class _Gather:
    def __init__(self, arrays):
        self.arrays = list(arrays)
        self.out_shapes = [jax.ShapeDtypeStruct((N_DEV, *s.shape), s.dtype) for s in arrays]

    def phases(self, src, dst, send_sems, recv_sems, local_sems):
        n_arr = len(self.arrays)
        x, y, c = _mesh_pos()
        me, sibling = (x, y, c), (x, y, 1 - c)
        chips = [(1 - x, y), (x, 1 - y), (1 - x, 1 - y)]

        def copy(a, k, block, to, from_src):
            return pltpu.make_async_remote_copy(
                src_ref=src[a] if from_src else dst[a].at[_slot(block)], dst_ref=dst[a].at[_slot(block)],
                send_sem=send_sems.at[a * 7 + k], recv_sem=recv_sems.at[a * 7 + k], device_id=to, device_id_type=MESH)

        def mine(a):
            return pltpu.make_async_copy(src[a], dst[a].at[_slot(me)], local_sems.at[a])

        def first(a):
            return [copy(a, 0, me, sibling, True)] + [copy(a, 1 + j, me, (*chip, c), True)
                                                     for j, chip in enumerate(chips)]

        def start():
            for a in range(n_arr):
                mine(a).start()
                for cp in first(a):
                    cp.start()

        def middle():
            for j, chip in enumerate(chips):
                for a in range(n_arr):
                    copy(a, 1 + j, (*chip, c), me, False).wait_recv()
                    copy(a, 4 + j, (*chip, c), sibling, False).start()

        def finish():
            for a in range(n_arr):
                copy(a, 0, sibling, me, False).wait_recv()
                for j, chip in enumerate(chips):
                    copy(a, 4 + j, (*chip, 1 - c), me, False).wait_recv()
                for cp in first(a):
                    cp.wait_send()
                for j, chip in enumerate(chips):
                    copy(a, 4 + j, (*chip, c), sibling, False).wait_send()
                mine(a).wait()

        return start, middle, finish


class _Scatter:
    def __init__(self, arrays):
        self.arrays = list(arrays)
        self.out_shapes = [jax.ShapeDtypeStruct(b.shape, b.dtype) for b in arrays]

    def phases(self, src, dst, send_sems, recv_sems, local_sems):
        n_arr = len(self.arrays)
        x, y, c = _mesh_pos()
        me = (x, y, c)
        peers = []
        for k in range(1, N_DEV):
            fx, fy, fc = (k >> 2) & 1, (k >> 1) & 1, k & 1
            peers.append((1 - x if fx else x, 1 - y if fy else y, 1 - c if fc else c))

        def copy(a, k, peer, sending):
            return pltpu.make_async_remote_copy(
                src_ref=src[a].at[_slot(peer)], dst_ref=dst[a].at[_slot(me) if sending else _slot(peer)],
                send_sem=send_sems.at[a * 7 + k], recv_sem=recv_sems.at[a * 7 + k], device_id=peer,
                device_id_type=MESH)

        def mine(a):
            return pltpu.make_async_copy(src[a].at[_slot(me)], dst[a].at[_slot(me)], local_sems.at[a])

        def start():
            for a in range(n_arr):
                mine(a).start()
                for k, peer in enumerate(peers):
                    copy(a, k, peer, True).start()

        def finish():
            for a in range(n_arr):
                for k, peer in enumerate(peers):
                    copy(a, k, peer, False).wait_recv()
                for k, peer in enumerate(peers):
                    copy(a, k, peer, True).wait_send()
                mine(a).wait()

        return start, None, finish


def _hosted_call(name, body, grid, in_specs, out_specs, out_shape, scratch_shapes, sem, operands, exch=None):
    if exch is None:
        return pl.pallas_call(body, grid=grid, in_specs=in_specs, out_specs=out_specs, out_shape=out_shape,
                              scratch_shapes=scratch_shapes, compiler_params=_params(sem), name=name)(*operands)
    n_in, n_out, n_scr, ne = len(in_specs), len(out_specs), len(scratch_shapes), len(exch.arrays)
    total = math.prod(grid)

    def wrapped(*refs):
        ins, ex_in = refs[:n_in], refs[n_in:n_in + ne]
        outs = refs[n_in + ne:n_in + ne + n_out]
        ex_out = refs[n_in + ne + n_out:n_in + 2 * ne + n_out]
        scr = refs[n_in + 2 * ne + n_out:n_in + 2 * ne + n_out + n_scr]
        step = 0
        for axis, size in enumerate(grid):
            step = step * size + pl.program_id(axis)
        start, middle, finish = exch.phases(ex_in, ex_out, *refs[-3:])
        pl.when(step == 0)(start)
        if middle is not None:
            pl.when(step == total // 2)(middle)
        body(*ins, *outs, *scr)
        pl.when(step == total - 1)(finish)

    hbm = pl.BlockSpec(memory_space=pl.ANY)
    sems = [pltpu.SemaphoreType.DMA((ne * 7,)), pltpu.SemaphoreType.DMA((ne * 7,)), pltpu.SemaphoreType.DMA((ne,))]
    return pl.pallas_call(
        wrapped, grid=grid, in_specs=[*in_specs, *[hbm] * ne], out_specs=[*out_specs, *[hbm] * ne],
        out_shape=[*out_shape, *exch.out_shapes], scratch_shapes=[*scratch_shapes, *sems],
        compiler_params=pltpu.CompilerParams(dimension_semantics=("arbitrary",) * len(grid),
                                             vmem_limit_bytes=VMEM_LIMIT, has_side_effects=True),
        name=name)(*operands, *exch.arrays)


def _exchange_alone(name, exch):
    def body():
        pass

    return _hosted_call(name, body, (1,), [], [], [], [], None, (), exch)


def _mm(name, grid, a, a_spec, b, b_spec, dims, outs, o_specs, acc_shape, extra=(), extra_specs=(), epi=None,
        exch=None):
    nk = grid[-1]
    n_extra, n_out = len(extra), len(outs)

    def body(*refs):
        a_ref, b_ref = refs[0], refs[1]
        ex = refs[2:2 + n_extra]
        o_refs = refs[2 + n_extra:2 + n_extra + n_out]
        acc = refs[-1]
        k = pl.program_id(len(grid) - 1)

        @pl.when(k == 0)
        def _():
            acc[...] = jnp.zeros_like(acc)

        acc[...] += _bdot(a_ref[...], b_ref[...], dims)

        @pl.when(k == nk - 1)
        def _():
            res = acc[...]
            res = (res,) if epi is None else epi(res, *[e[...] for e in ex])
            for o, r in zip(o_refs, res):
                o[...] = r.astype(o.dtype)

    sem = ("parallel",) * (len(grid) - 1) + ("arbitrary",)
    return _hosted_call(name, body, grid, [a_spec, b_spec, *extra_specs], list(o_specs), list(outs),
                        [pltpu.VMEM(acc_shape, F32)], sem, (a, b, *extra), exch)


def _add_epi(acc, add):
    return (acc + add,)


def _pack_epi(acc):
    return (_pack_pairs(acc),)


def _mm_nt(name, a, b, n_off, n, out_dtype=F32, tm=1024, tn=512, exch=None):
    m, kk = a.shape
    tm, tn = _t(m, tm), math.gcd(_t(n, tn), n_off)
    res = _mm(name, (m // tm, n // tn, 1),
              a, pl.BlockSpec((tm, kk), lambda i, j, k: (i, 0)),
              b, pl.BlockSpec((tn, kk), lambda i, j, k: (n_off // tn + j, 0)), NT,
              [jax.ShapeDtypeStruct((m, n), out_dtype)], [pl.BlockSpec((tm, tn), lambda i, j, k: (i, j))], (tm, tn),
              exch=exch)
    return res[0] if exch is None else res


def _mm_nn_blocked(name, a, b3, tm=1024, tn=512):
    m, kk = a.shape
    nb, _, cs = b3.shape
    tm, tn = _t(m, tm), _t(cs, tn)
    per = cs // tn
    return _mm(name, (m // tm, nb * per, 1),
               a, pl.BlockSpec((tm, kk), lambda i, j, k: (i, 0)),
               b3, pl.BlockSpec((None, kk, tn), lambda i, j, k: (j // per, 0, j % per)), NN,
               [jax.ShapeDtypeStruct((m, nb * cs), F32)], [pl.BlockSpec((tm, tn), lambda i, j, k: (i, j))],
               (tm, tn))[0]


def _mm_nt_blocked(name, a, b3, blk_off, add=None, tm=1024, tn=1024):
    m = a.shape[0]
    _, n, cs = b3.shape
    nk = a.shape[1] // cs
    tm, tn = _t(m, tm), _t(n, tn)
    extra, especs, epi = (), (), None
    if add is not None:
        extra, especs, epi = (add,), (pl.BlockSpec((tm, tn), lambda i, j, k: (i, j)),), _add_epi
    return _mm(name, (m // tm, n // tn, nk),
               a, pl.BlockSpec((tm, cs), lambda i, j, k: (i, k)),
               b3, pl.BlockSpec((None, tn, cs), lambda i, j, k: (blk_off + k, j, 0)), NT,
               [jax.ShapeDtypeStruct((m, n), F32)], [pl.BlockSpec((tm, tn), lambda i, j, k: (i, j))], (tm, tn),
               extra, especs, epi)[0]


def _mm_nn(name, a, b, k_off=0, kk=None, add=None, out_dtype=F32, tm=1024, tn=1024, tk=1024):
    m = a.shape[0]
    kk = a.shape[1] if kk is None else kk
    n = b.shape[1]
    tm, tn, tk = _t(m, tm), _t(n, tn), math.gcd(_t(kk, tk), k_off)
    extra, especs, epi = (), (), None
    if add is not None:
        extra, especs, epi = (add,), (pl.BlockSpec((tm, tn), lambda i, j, k: (i, j)),), _add_epi
    return _mm(name, (m // tm, n // tn, kk // tk),
               a, pl.BlockSpec((tm, tk), lambda i, j, k: (i, k)),
               b, pl.BlockSpec((tk, tn), lambda i, j, k: (k_off // tk + k, j)), NN,
               [jax.ShapeDtypeStruct((m, n), out_dtype)], [pl.BlockSpec((tm, tn), lambda i, j, k: (i, j))], (tm, tn),
               extra, especs, epi)[0]


def _mm_tn_packed(name, a, b, pw, tm=512, tk=1024):
    kk, m = a.shape
    n = b.shape[1]
    tm, tk = _t(m, tm), _t(kk, tk)
    return _mm(name, (m // tm, n // pw, kk // tk),
               a, pl.BlockSpec((tk, tm), lambda i, j, k: (k, i)),
               b, pl.BlockSpec((tk, pw), lambda i, j, k: (k, j)), TN,
               [jax.ShapeDtypeStruct((n // pw, m, pw // 2), F32)],
               [pl.BlockSpec((None, tm, pw // 2), lambda i, j, k: (j, i, 0))], (tm, pw), epi=_pack_epi)[0]


def _rms_fwd(name, x, g, tm=256):
    l, d = x.shape
    tm = _t(l, tm)

    def body(x_ref, g_ref, h_ref, r_ref):
        xv = x_ref[...]
        r = lax.rsqrt(jnp.mean(xv * xv, axis=-1, keepdims=True) + EPS)
        h_ref[...] = (xv * r * g_ref[...]).astype(BF)
        r_ref[...] = r

    return pl.pallas_call(
        body, grid=(l // tm,),
        in_specs=[pl.BlockSpec((tm, d), lambda i: (i, 0)), pl.BlockSpec((1, d), lambda i: (0, 0))],
        out_specs=[pl.BlockSpec((tm, d), lambda i: (i, 0)), pl.BlockSpec((tm, 1), lambda i: (i, 0))],
        out_shape=[jax.ShapeDtypeStruct((l, d), BF), jax.ShapeDtypeStruct((l, 1), F32)],
        compiler_params=_params(("parallel",)), name=name)(x, g)


def _rms_bwd(name, dh, x, r, g, dres, tm=256):
    l, d = x.shape
    tm = _t(l, tm)

    def body(dh_ref, x_ref, r_ref, g_ref, dres_ref, dx_ref, dg_ref):
        i = pl.program_id(0)
        rr = r_ref[...]
        xhat = x_ref[...] * rr
        dhv = dh_ref[...]
        dxh = dhv * g_ref[...]
        dx_ref[...] = dres_ref[...] + rr * (dxh - xhat * jnp.mean(dxh * xhat, axis=-1, keepdims=True))

        @pl.when(i == 0)
        def _():
            dg_ref[...] = jnp.zeros_like(dg_ref)

        dg_ref[...] += jnp.sum(dhv * xhat, axis=0, keepdims=True)

    row = pl.BlockSpec((tm, d), lambda i: (i, 0))
    vec = pl.BlockSpec((1, d), lambda i: (0, 0))
    return pl.pallas_call(
        body, grid=(l // tm,), in_specs=[row, row, pl.BlockSpec((tm, 1), lambda i: (i, 0)), vec, row],
        out_specs=[row, vec], out_shape=[jax.ShapeDtypeStruct((l, d), F32), jax.ShapeDtypeStruct((1, d), F32)],
        compiler_params=_params(("arbitrary",)), name=name)(dh, x, r, g, dres)


def _loss_head(x, g, target, tm=256):
    l, d = x.shape
    tm = _t(l, tm)

    def body(x_ref, g_ref, t_ref, dx_ref, dg_ref, loss_ref):
        i = pl.program_id(0)
        xv = x_ref[...]
        gv = g_ref[...]
        r = lax.rsqrt(jnp.mean(xv * xv, axis=-1, keepdims=True) + EPS)
        xhat = xv * r
        e = xhat * gv - t_ref[...]
        dy = e * (1.0 / d)
        dxh = dy * gv
        dx_ref[...] = r * (dxh - xhat * jnp.mean(dxh * xhat, axis=-1, keepdims=True))

        @pl.when(i == 0)
        def _():
            dg_ref[...] = jnp.zeros_like(dg_ref)
            loss_ref[...] = jnp.zeros_like(loss_ref)

        dg_ref[...] += jnp.sum(dy * xhat, axis=0, keepdims=True)
        loss_ref[...] += 0.5 * jnp.sum(jnp.sum(e * e, axis=-1, keepdims=True) * (1.0 / d), axis=0, keepdims=True)

    row = pl.BlockSpec((tm, d), lambda i: (i, 0))
    vec = pl.BlockSpec((1, d), lambda i: (0, 0))
    return pl.pallas_call(
        body, grid=(l // tm,), in_specs=[row, vec, row],
        out_specs=[row, vec, pl.BlockSpec((1, 1), lambda i: (0, 0))],
        out_shape=[jax.ShapeDtypeStruct((l, d), F32), jax.ShapeDtypeStruct((1, d), F32),
                   jax.ShapeDtypeStruct((1, 1), F32)],
        compiler_params=_params(("arbitrary",)), name="loss_head")(x, g, target)


CONV_HALO = 8


def _conv_pre(x_ref, w_ref, b_ref, i, tr):
    r0 = pl.multiple_of(i * tr, tr)
    cur = x_ref[pl.ds(r0, tr), :]
    prev = x_ref[pl.ds(pl.multiple_of(jnp.maximum(r0 - CONV_HALO, 0), CONV_HALO), CONV_HALO), :]
    prev = jnp.where(i > 0, prev, 0.0)
    ext = jnp.concatenate([prev, cur], axis=0)
    taps = []
    for k in range(CONV_TAPS):
        s = CONV_TAPS - 1 - k
        taps.append(cur if s == 0 else pltpu.roll(ext, s, 0)[CONV_HALO:])
    pre = b_ref[...] + sum(w_ref[k:k + 1, :] * taps[k] for k in range(CONV_TAPS))
    return r0, pre, taps


def _conv_fwd(pm, col_off, conv_w, conv_b, cw=256, tr=512):
    l = pm.shape[0]
    c = conv_w.shape[1]
    cw, tr = _t(c, cw), _t(l, tr)
    assert col_off % cw == 0

    def body(x_ref, w_ref, b_ref, o_ref):
        def step(i, carry):
            r0, pre, _ = _conv_pre(x_ref, w_ref, b_ref, i, tr)
            o_ref[pl.ds(r0, tr), :] = _silu(pre)
            return carry
        lax.fori_loop(0, l // tr, step, 0)

    return pl.pallas_call(
        body, grid=(c // cw,),
        in_specs=[pl.BlockSpec((l, cw), lambda j: (0, col_off // cw + j)),
                  pl.BlockSpec((CONV_TAPS, cw), lambda j: (0, j)), pl.BlockSpec((1, cw), lambda j: (0, j))],
        out_specs=pl.BlockSpec((l, cw), lambda j: (0, j)), out_shape=jax.ShapeDtypeStruct((l, c), F32),
        compiler_params=_params(("parallel",)), name="conv_fwd")(pm, conv_w, conv_b)


def _conv_bwd(pm, col_off, conv_w, conv_b, dy, cw=256, tr=512):
    l = pm.shape[0]
    c = conv_w.shape[1]
    cw, tr = _t(c, cw), _t(l, tr)
    nt = l // tr

    def body(x_ref, w_ref, b_ref, dy_ref, dx_ref, dw_ref, db_ref, dpre_ref):
        def step1(i, carry):
            dws, db = carry
            r0, pre, taps = _conv_pre(x_ref, w_ref, b_ref, i, tr)
            dpre = dy_ref[pl.ds(r0, tr), :] * _dsilu(pre)
            dpre_ref[pl.ds(r0, tr), :] = dpre
            dws = tuple(dws[k] + jnp.sum(dpre * taps[k], axis=0, keepdims=True) for k in range(CONV_TAPS))
            return dws, db + jnp.sum(dpre, axis=0, keepdims=True)

        z = jnp.zeros((1, cw), F32)
        dws, db = lax.fori_loop(0, nt, step1, ((z,) * CONV_TAPS, z))
        for k in range(CONV_TAPS):
            dw_ref[k:k + 1, :] = dws[k]
        db_ref[...] = db

        def step2(i, carry):
            r0 = pl.multiple_of(i * tr, tr)
            cur = dpre_ref[pl.ds(r0, tr), :]
            nxt = dpre_ref[pl.ds(pl.multiple_of(jnp.minimum(r0 + tr, l - CONV_HALO), CONV_HALO), CONV_HALO), :]
            nxt = jnp.where(i < nt - 1, nxt, 0.0)
            ext = jnp.concatenate([cur, nxt], axis=0)
            acc = w_ref[CONV_TAPS - 1:CONV_TAPS, :] * cur
            for k in range(CONV_TAPS - 1):
                s = CONV_TAPS - 1 - k
                acc = acc + w_ref[k:k + 1, :] * pltpu.roll(ext, tr + CONV_HALO - s, 0)[:tr]
            dx_ref[pl.ds(r0, tr), :] = acc.astype(dx_ref.dtype)
            return carry
        lax.fori_loop(0, nt, step2, 0)

    col = pl.BlockSpec((l, cw), lambda j: (0, j))
    return pl.pallas_call(
        body, grid=(c // cw,),
        in_specs=[pl.BlockSpec((l, cw), lambda j: (0, col_off // cw + j)),
                  pl.BlockSpec((CONV_TAPS, cw), lambda j: (0, j)), pl.BlockSpec((1, cw), lambda j: (0, j)), col],
        out_specs=[col, pl.BlockSpec((CONV_TAPS, cw), lambda j: (0, j)), pl.BlockSpec((1, cw), lambda j: (0, j))],
        out_shape=[jax.ShapeDtypeStruct((l, c), BF), jax.ShapeDtypeStruct((CONV_TAPS, c), F32),
                   jax.ShapeDtypeStruct((1, c), F32)],
        scratch_shapes=[pltpu.VMEM((l, cw), F32)],
        compiler_params=_params(("parallel",)), name="conv_bwd")(pm, conv_w, conv_b, dy)


def _ssd_consts(r, p, t):
    rp, rt = r * p, r * t
    tri = np.tril(np.ones((t, t), np.float32))
    ep = np.zeros((HEAD_LANES, rp), np.float32)
    ep[np.arange(rp) // p, np.arange(rp)] = 1.0
    es = np.zeros((HEAD_LANES, rt), np.float32)
    es[np.arange(rt) // t, np.arange(rt)] = 1.0
    itile = (np.arange(t)[:, None] == (np.arange(rt) % t)[None, :]).astype(np.float32)
    lmask = (np.arange(t)[:, None] >= (np.arange(rt) % t)[None, :]).astype(np.float32)
    bmask = ((np.arange(rt) // t)[:, None] == (np.arange(rp) // p)[None, :]).astype(np.float32)
    return [jnp.asarray(a) for a in (tri, ep, ep.T.copy(), es, es.T.copy(), itile, lmask, bmask)]


def _ssd_common(xs, bm, dtr, sel, dtb, alog, tri, ep, es, itile, lmask, bmask, r, t):
    pre = _hdot(dtr, sel) + dtb
    dt = jnp.maximum(pre, 0.0) + jnp.log(1.0 + jnp.exp(-jnp.abs(pre)))
    a = -jnp.exp(alog)
    cs = _hdot(tri, dt * a)
    last = cs[t - 1:t, :]
    ecs, w = jnp.exp(cs), jnp.exp(last - cs)
    dtx, ecsx, wx = _hdot(dt, ep), _hdot(ecs, ep), _hdot(w, ep)
    csx = _hdot(cs, es)
    csrow = jnp.sum(csx * itile, axis=0, keepdims=True)
    lx = jnp.exp(jnp.where(lmask > 0.0, csx - csrow, NEG_BIG))
    xdt = xs * dtx
    xblk = (jnp.concatenate([xdt] * r, axis=0) * bmask).astype(BF)
    btile = jnp.concatenate([bm] * r, axis=0).astype(BF)
    return pre, dt, a, cs, last, ecs, w, dtx, ecsx, wx, lx, xdt, xblk, btile


def _ssd_specs(l, h, g_n, r, p, n, t, rev):
    nc = l // t
    rp = r * p
    hp = h * p
    cidx = (lambda c: nc - 1 - c) if rev else (lambda c: c)
    xs_spec = pl.BlockSpec((t, rp), lambda g, c: (cidx(c), g))
    b_spec = pl.BlockSpec((t, n), lambda g, c: (cidx(c), hp // n + g))
    c_spec = pl.BlockSpec((t, n), lambda g, c: (cidx(c), (hp + g_n * n) // n + g))
    dtr_spec = pl.BlockSpec((t, h), lambda g, c: (cidx(c), 0))
    sel_spec = pl.BlockSpec((None, h, HEAD_LANES), lambda g, c: (g, 0, 0))
    par_spec = pl.BlockSpec((None, 1, HEAD_LANES), lambda g, c: (g, 0, 0))
    st_spec = pl.BlockSpec((None, None, rp, n), lambda g, c: (cidx(c), g, 0, 0))
    return nc, xs_spec, b_spec, c_spec, dtr_spec, sel_spec, par_spec, st_spec


def _const_specs(consts):
    return [pl.BlockSpec(a.shape, lambda g, c: (0, 0)) for a in consts]


def _ssd_fwd(xbc, dtr, sel, dtb, alog, dsk, dims, exch=None):
    h, g_n, r, p, n, t = dims
    l = xbc.shape[0]
    rp, hp = r * p, h * p
    nc, xs_spec, b_spec, c_spec, dtr_spec, sel_spec, par_spec, st_spec = _ssd_specs(l, h, g_n, r, p, n, t, False)
    consts = _ssd_consts(r, p, t)

    def body(xs_ref, b_ref, c_ref, dtr_ref, sel_ref, dtb_ref, alog_ref, dsk_ref,
             tri_ref, ep_ref, ept_ref, es_ref, est_ref, it_ref, lm_ref, bmk_ref, y_ref, st_ref, s_ref):
        @pl.when(pl.program_id(1) == 0)
        def _():
            s_ref[...] = jnp.zeros_like(s_ref)

        xs, bm, cm = xs_ref[...], b_ref[...], c_ref[...]
        ep = ep_ref[...]
        (_, _, _, _, last, _, _, _, ecsx, wx, lx, xdt, xblk, btile) = _ssd_common(
            xs, bm, dtr_ref[...], sel_ref[...], dtb_ref[...], alog_ref[...], tri_ref[...], ep, es_ref[...],
            it_ref[...], lm_ref[...], bmk_ref[...], r, t)
        s_in = s_ref[...]
        st_ref[...] = s_in
        cbx = _bdot(cm, btile, NT)
        yd = _bdot(cbx * lx, xblk)
        yo = ecsx * _bdot(cm, s_in, NT)
        dskx = _hdot(jnp.zeros((t, HEAD_LANES), F32) + dsk_ref[...], ep)
        y_ref[...] = yd + yo + dskx * xs
        elast = jnp.sum(ept_ref[...] * jnp.exp(last), axis=1, keepdims=True)
        s_ref[...] = elast * s_in + _bdot(xdt * wx, bm, TN)

    return _hosted_call(
        "ssd_fwd", body, (g_n, nc),
        [xs_spec, b_spec, c_spec, dtr_spec, sel_spec, par_spec, par_spec, par_spec, *_const_specs(consts)],
        [xs_spec, st_spec],
        [jax.ShapeDtypeStruct((l, hp), F32), jax.ShapeDtypeStruct((nc, g_n, rp, n), F32)],
        [pltpu.VMEM((rp, n), F32)], ("parallel", "arbitrary"),
        (xbc, xbc, xbc, dtr, sel, dtb, alog, dsk, *consts), exch)


def _ssd_bwd(xbc, dtr, sel, dtb, alog, dsk, states, dy, dims, exch=None):
    h, g_n, r, p, n, t = dims
    l = xbc.shape[0]
    rp, hp = r * p, h * p
    conv = xbc.shape[1]
    nc, xs_spec, b_spec, c_spec, dtr_spec, sel_spec, par_spec, st_spec = _ssd_specs(l, h, g_n, r, p, n, t, True)
    consts = _ssd_consts(r, p, t)

    def fold_rows(v, rows):
        return sum(v[k * rows:(k + 1) * rows, :] for k in range(r))

    def body(xs_ref, b_ref, c_ref, dtr_ref, sel_ref, dtb_ref, alog_ref, dsk_ref, st_ref, dy_ref,
             tri_ref, ep_ref, ept_ref, es_ref, est_ref, it_ref, lm_ref, bmk_ref,
             dxs_ref, db_ref, dc_ref, ddt_ref, dbias_ref, dalog_ref, dd_ref, ds_ref):
        @pl.when(pl.program_id(1) == 0)
        def _():
            ds_ref[...] = jnp.zeros_like(ds_ref)
            dbias_ref[...] = jnp.zeros_like(dbias_ref)
            dalog_ref[...] = jnp.zeros_like(dalog_ref)
            dd_ref[...] = jnp.zeros_like(dd_ref)

        xs, bm, cm, dy = xs_ref[...], b_ref[...], c_ref[...], dy_ref[...]
        ep, ept, est, itile = ep_ref[...], ept_ref[...], est_ref[...], it_ref[...]
        tri = tri_ref[...]
        (pre, dt, a, cs, last, ecs, w, dtx, ecsx, wx, lx, xdt, xblk, btile) = _ssd_common(
            xs, bm, dtr_ref[...], sel_ref[...], dtb_ref[...], alog_ref[...], tri, ep, es_ref[...],
            itile, lm_ref[...], bmk_ref[...], r, t)
        s_in = st_ref[...]
        ds_out = ds_ref[...]
        elast_row = jnp.exp(last)
        elast = jnp.sum(ept * elast_row, axis=1, keepdims=True)

        dskx = _hdot(jnp.zeros((t, HEAD_LANES), F32) + dsk_ref[...], ep)
        dxs = dskx * dy
        dd_ref[...] += jnp.sum(_hdot(dy * xs, ept), axis=0, keepdims=True)

        cbx = _bdot(cm, btile, NT)
        amat = cbx * lx
        da = _bdot(dy, xblk, NT)
        dxdt = fold_rows(_bdot(amat, dy, TN) * bmk_ref[...], t)
        dcbx = da * lx
        q = da * amat
        dc = _bdot(dcbx, btile)
        db = fold_rows(_bdot(dcbx, cm, TN), t)
        dcs = _hdot(q, est) - _hdot(itile * jnp.sum(q, axis=0, keepdims=True), est)

        g0 = _bdot(cm, s_in, NT)
        dg0 = dy * ecsx
        dc = dc + _bdot(dg0, s_in)
        dcs = dcs + _hdot(dy * g0, ept) * ecs

        z = _bdot(bm, ds_out, NT)
        dxdt = dxdt + z * wx
        dw = _hdot(z * xdt, ept)
        db = db + _bdot(xdt * wx, ds_out)
        dlast = jnp.sum(jnp.sum(ds_out * s_in, axis=1, keepdims=True) * ept, axis=0, keepdims=True) * elast_row
        dww = dw * w
        dcs = dcs - dww
        dlast = dlast + jnp.sum(dww, axis=0, keepdims=True)
        ds_ref[...] = elast * ds_out + _bdot(dg0, cm, TN)

        row = lax.broadcasted_iota(jnp.int32, (t, HEAD_LANES), 0)
        dcs = dcs + jnp.where(row == t - 1, dlast, 0.0)
        dadt = _hdot(tri, dcs, TN)
        dalog_ref[...] += jnp.sum(dadt * dt, axis=0, keepdims=True) * a
        ddt = dadt * a + _hdot(dxdt * xs, ept)
        dpre = ddt * (1.0 / (1.0 + jnp.exp(-pre)))
        ddt_ref[...] = dpre
        dbias_ref[...] += jnp.sum(dpre, axis=0, keepdims=True)
        dxs_ref[...] = dxs + dxdt * dtx
        db_ref[...] = db
        dc_ref[...] = dc

    cidx = lambda c: nc - 1 - c
    return _hosted_call(
        "ssd_bwd", body, (g_n, nc),
        [xs_spec, b_spec, c_spec, dtr_spec, sel_spec, par_spec, par_spec, par_spec, st_spec, xs_spec,
         *_const_specs(consts)],
        [xs_spec, pl.BlockSpec((t, n), lambda g, c: (cidx(c), g)),
         pl.BlockSpec((t, n), lambda g, c: (cidx(c), g)),
         pl.BlockSpec((None, t, HEAD_LANES), lambda g, c: (g, cidx(c), 0)), par_spec, par_spec, par_spec],
        [jax.ShapeDtypeStruct((l, hp), F32), jax.ShapeDtypeStruct((l, g_n * n), F32),
         jax.ShapeDtypeStruct((l, g_n * n), F32), jax.ShapeDtypeStruct((g_n, l, HEAD_LANES), F32),
         jax.ShapeDtypeStruct((g_n, 1, HEAD_LANES), F32), jax.ShapeDtypeStruct((g_n, 1, HEAD_LANES), F32),
         jax.ShapeDtypeStruct((g_n, 1, HEAD_LANES), F32)],
        [pltpu.VMEM((rp, n), F32)], ("parallel", "arbitrary"),
        (xbc, xbc, xbc, dtr, sel, dtb, alog, dsk, states, dy, *consts), exch)


def _gate_norm_fwd(y, pm, g, tm=128):
    l, di = y.shape
    tm = _t(l, tm)

    def body(y_ref, z_ref, g_ref, o_ref, r_ref):
        v = y_ref[...] * _silu(z_ref[...])
        rr = lax.rsqrt(jnp.mean(v * v, axis=-1, keepdims=True) + EPS)
        o_ref[...] = (v * rr * g_ref[...]).astype(BF)
        r_ref[...] = rr

    row = pl.BlockSpec((tm, di), lambda i: (i, 0))
    return pl.pallas_call(
        body, grid=(l // tm,), in_specs=[row, row, pl.BlockSpec((1, di), lambda i: (0, 0))],
        out_specs=[row, pl.BlockSpec((tm, 1), lambda i: (i, 0))],
        out_shape=[jax.ShapeDtypeStruct((l, di), BF), jax.ShapeDtypeStruct((l, 1), F32)],
        compiler_params=_params(("parallel",)), name="gate_norm_fwd")(y, pm, g)


def _gate_norm_bwd(dyn, y, pm, rstd, g, tm=128):
    l, di = y.shape
    tm = _t(l, tm)

    def body(dyn_ref, y_ref, z_ref, r_ref, g_ref, dy_ref, dz_ref, dg_ref):
        i = pl.program_id(0)
        yv, zv, rr = y_ref[...], z_ref[...], r_ref[...]
        sz = _silu(zv)
        vhat = yv * sz * rr
        dn = dyn_ref[...]
        dvh = dn * g_ref[...]
        dv = rr * (dvh - vhat * jnp.mean(dvh * vhat, axis=-1, keepdims=True))
        dy_ref[...] = dv * sz
        dz_ref[...] = (dv * yv * _dsilu(zv)).astype(BF)

        @pl.when(i == 0)
        def _():
            dg_ref[...] = jnp.zeros_like(dg_ref)

        dg_ref[...] += jnp.sum(dn * vhat, axis=0, keepdims=True)

    row = pl.BlockSpec((tm, di), lambda i: (i, 0))
    vec = pl.BlockSpec((1, di), lambda i: (0, 0))
    return pl.pallas_call(
        body, grid=(l // tm,), in_specs=[row, row, row, pl.BlockSpec((tm, 1), lambda i: (i, 0)), vec],
        out_specs=[row, row, vec],
        out_shape=[jax.ShapeDtypeStruct((l, di), F32), jax.ShapeDtypeStruct((l, di), BF),
                   jax.ShapeDtypeStruct((1, di), F32)],
        compiler_params=_params(("arbitrary",)), name="gate_norm_bwd")(dyn, y, pm, rstd, g)


POOL_HALO = 16


def _pool_mix(name, src, dp, backward, out_dtype, cw=256, tr=512):
    l = src.shape[0]
    gd = dp // len(POOL_WINDOWS)
    cwl, trl = _t(gd, cw), _t(l, tr)
    nt = l // trl

    def body(x_ref, o_ref):
        gi = pl.program_id(0)
        for wi, win in enumerate(POOL_WINDOWS):
            @pl.when(gi == wi)
            def _(win=win):
                def step(i, carry):
                    r0 = pl.multiple_of(i * trl, trl)
                    cur = x_ref[pl.ds(r0, trl), :]
                    trow = r0 + lax.broadcasted_iota(jnp.int32, (trl, 1), 0)
                    cnt = jnp.minimum(trow + 1, win).astype(F32)
                    if not backward:
                        halo = x_ref[pl.ds(pl.multiple_of(jnp.maximum(r0 - POOL_HALO, 0), POOL_HALO), POOL_HALO), :]
                        halo = jnp.where(i > 0, halo, 0.0)
                        s = jnp.concatenate([halo, cur], axis=0)
                        sh = 1
                        while sh < win:
                            s = s + pltpu.roll(s, sh, 0)
                            sh *= 2
                        res = s[POOL_HALO:] / cnt - cur
                    else:
                        halo = x_ref[pl.ds(pl.multiple_of(jnp.minimum(r0 + trl, l - POOL_HALO), POOL_HALO),
                                           POOL_HALO), :]
                        hrow = r0 + trl + lax.broadcasted_iota(jnp.int32, (POOL_HALO, 1), 0)
                        hcnt = jnp.minimum(hrow + 1, win).astype(F32)
                        halo = jnp.where(i < nt - 1, halo / hcnt, 0.0)
                        s = jnp.concatenate([cur / cnt, halo], axis=0)
                        sh = 1
                        while sh < win:
                            s = s + pltpu.roll(s, trl + POOL_HALO - sh, 0)
                            sh *= 2
                        res = s[:trl] - cur
                    o_ref[pl.ds(r0, trl), :] = res.astype(o_ref.dtype)
                    return carry
                lax.fori_loop(0, nt, step, 0)

    return pl.pallas_call(
        body, grid=(len(POOL_WINDOWS), gd // cwl),
        in_specs=[pl.BlockSpec((l, cwl), lambda g, j: (0, g * (gd // cwl) + j))],
        out_specs=pl.BlockSpec((l, cwl), lambda g, j: (0, g * (gd // cwl) + j)),
        out_shape=jax.ShapeDtypeStruct((l, dp), out_dtype),
        compiler_params=_params(("parallel", "parallel")), name=name)(src)


def _pool_gate_bwd(dyp, mg, pu, scale, tm=128):
    l, dp = mg.shape
    tm = _t(l, tm)

    def body(dyp_ref, mg_ref, gate_ref, sc_ref, dmg_ref, dgate_ref, dsc_ref):
        i = pl.program_id(0)
        d, m, gt, sc = dyp_ref[...], mg_ref[...], gate_ref[...], sc_ref[...]
        sg = _silu(gt)
        dmg_ref[...] = (d * sc * sg).astype(BF)
        dgate_ref[...] = (d * m * sc * _dsilu(gt)).astype(BF)

        @pl.when(i == 0)
        def _():
            dsc_ref[...] = jnp.zeros_like(dsc_ref)

        dsc_ref[...] += jnp.sum(d * m * sg, axis=0, keepdims=True)

    row = pl.BlockSpec((tm, dp), lambda i: (i, 0))
    vec = pl.BlockSpec((1, dp), lambda i: (0, 0))
    return pl.pallas_call(
        body, grid=(l // tm,), in_specs=[row, row, pl.BlockSpec((tm, dp), lambda i: (i, 1)), vec],
        out_specs=[row, row, vec],
        out_shape=[jax.ShapeDtypeStruct((l, dp), BF), jax.ShapeDtypeStruct((l, dp), BF),
                   jax.ShapeDtypeStruct((1, dp), F32)],
        compiler_params=_params(("arbitrary",)), name="pool_gate_bwd")(dyp, mg, pu, scale)


def _group_mm_fwd(mp, wg, pu, scale, tm=1024, tn=1024, tk=1024):
    l, dp = mp.shape
    ng, gd = wg.shape[0], wg.shape[1]
    tm, tn, tk = _t(l, tm), _t(gd, tn), _t(gd, tk)

    def epi(acc, gate, sc):
        return acc, acc * sc * _silu(gate)

    out = pl.BlockSpec((tm, tn), lambda g, i, j, k: (i, g * (gd // tn) + j))
    return _mm("group_mm_fwd", (ng, l // tm, gd // tn, gd // tk),
               mp, pl.BlockSpec((tm, tk), lambda g, i, j, k: (i, g * (gd // tk) + k)),
               wg, pl.BlockSpec((None, tk, tn), lambda g, i, j, k: (g, k, j)), NN,
               [jax.ShapeDtypeStruct((l, dp), F32), jax.ShapeDtypeStruct((l, dp), BF)], [out, out], (tm, tn),
               (pu, scale),
               (pl.BlockSpec((tm, tn), lambda g, i, j, k: (i, (dp + g * gd) // tn + j)),
                pl.BlockSpec((1, tn), lambda g, i, j, k: (0, g * (gd // tn) + j))), epi)


def _group_mm_bwd_data(dmg, wg, tm=1024, tn=1024, tk=1024):
    l, dp = dmg.shape
    ng, gd = wg.shape[0], wg.shape[1]
    tm, tn, tk = _t(l, tm), _t(gd, tn), _t(gd, tk)
    return _mm("group_mm_bwd_data", (ng, l // tm, gd // tn, gd // tk),
               dmg, pl.BlockSpec((tm, tk), lambda g, i, j, k: (i, g * (gd // tk) + k)),
               wg, pl.BlockSpec((None, tn, tk), lambda g, i, j, k: (g, j, k)), NT,
               [jax.ShapeDtypeStruct((l, dp), F32)],
               [pl.BlockSpec((tm, tn), lambda g, i, j, k: (i, g * (gd // tn) + j))], (tm, tn))[0]


def _group_mm_bwd_weight(mp, dmg, ng, tm=512, tk=1024):
    l, dp = mp.shape
    gd = dp // ng
    tm, tk = _t(gd, tm), _t(l, tk)
    return _mm("group_mm_bwd_weight", (ng, gd // tm, l // tk),
               mp, pl.BlockSpec((tk, tm), lambda g, i, k: (k, g * (gd // tm) + i)),
               dmg, pl.BlockSpec((tk, gd), lambda g, i, k: (k, g)), TN,
               [jax.ShapeDtypeStruct((ng, gd, gd // 2), F32)],
               [pl.BlockSpec((None, tm, gd // 2), lambda g, i, k: (g, i, 0))], (tm, gd), epi=_pack_epi)[0]


def _cast_bf16(name, w, tr=256):
    r, c = w.shape
    tr = _row_tile(r, tr, 16)

    def body(w_ref, o_ref):
        o_ref[...] = w_ref[...].astype(BF)

    blk = pl.BlockSpec((tr, c), lambda i: (i, 0))
    return pl.pallas_call(body, grid=(r // tr,), in_specs=[blk], out_specs=blk,
                          out_shape=jax.ShapeDtypeStruct((r, c), BF), compiler_params=_params(("parallel",)),
                          name=name)(w)


def _pack_rows(name, w, tr=256):
    r, c = w.shape
    tr = _row_tile(r, tr)

    def body(w_ref, o_ref):
        o_ref[...] = _pack_pairs(w_ref[...])

    return pl.pallas_call(body, grid=(r // tr,), in_specs=[pl.BlockSpec((tr, c), lambda i: (i, 0))],
                          out_specs=pl.BlockSpec((tr, c // 2), lambda i: (i, 0)),
                          out_shape=jax.ShapeDtypeStruct((r, c // 2), F32), compiler_params=_params(("parallel",)),
                          name=name)(w)


def _unpack_rows(name, w, tr=512):
    r, h = w.shape
    tr = _row_tile(r, tr, 16)

    def body(w_ref, o_ref):
        hi, lo = _unpack_pairs(w_ref[...])
        o_ref[:, :h] = hi.astype(BF)
        o_ref[:, h:] = lo.astype(BF)

    return pl.pallas_call(body, grid=(r // tr,), in_specs=[pl.BlockSpec((tr, h), lambda i: (i, 0))],
                          out_specs=pl.BlockSpec((tr, 2 * h), lambda i: (i, 0)),
                          out_shape=jax.ShapeDtypeStruct((r, 2 * h), BF), compiler_params=_params(("parallel",)),
                          name=name)(w)


def _reduce_packed(name, recv, tr=256):
    nd, r, h = recv.shape
    tr = _row_tile(r, tr)

    def body(p_ref, o_ref):
        hi, lo = _unpack_pairs(p_ref[0])
        for k in range(1, nd):
            a, b = _unpack_pairs(p_ref[k])
            hi, lo = hi + a, lo + b
        o_ref[:, :h] = hi
        o_ref[:, h:] = lo

    return pl.pallas_call(body, grid=(r // tr,), in_specs=[pl.BlockSpec((nd, tr, h), lambda i: (0, i, 0))],
                          out_specs=pl.BlockSpec((tr, 2 * h), lambda i: (i, 0)),
                          out_shape=jax.ShapeDtypeStruct((r, 2 * h), F32), compiler_params=_params(("parallel",)),
                          name=name)(recv)


def _adamw_math(w, g, m, v):
    m2 = ADAM_B1 * m + (1.0 - ADAM_B1) * g
    v2 = ADAM_B2 * v + (1.0 - ADAM_B2) * (g * g)
    m_hat = m2 / (1.0 - ADAM_B1 ** ADAM_STEP)
    v_hat = v2 / (1.0 - ADAM_B2 ** ADAM_STEP)
    delta = -ADAM_LR * (m_hat / (jnp.sqrt(v_hat) + ADAM_EPS) + ADAM_WD * w)
    return delta, m2, v2


def _adamw(name, w, g, m, v, tr=256):
    r, c = w.shape
    tr = _row_tile(r, tr)

    def body(w_ref, g_ref, m_ref, v_ref, d_ref, m2_ref, v2_ref):
        d, m2, v2 = _adamw_math(w_ref[...], g_ref[...], m_ref[...], v_ref[...])
        d_ref[...] = d
        m2_ref[...] = m2
        v2_ref[...] = v2

    blk = pl.BlockSpec((tr, c), lambda i: (i, 0))
    shp = jax.ShapeDtypeStruct((r, c), F32)
    return pl.pallas_call(body, grid=(r // tr,), in_specs=[blk] * 4, out_specs=[blk] * 3, out_shape=[shp] * 3,
                          compiler_params=_params(("parallel",)), name=name)(w, g, m, v)


def _sum_slots(name, a):
    nd, r, c = a.shape

    def body(a_ref, o_ref):
        s = a_ref[0]
        for k in range(1, nd):
            s = s + a_ref[k]
        o_ref[...] = s

    return pl.pallas_call(body, out_shape=jax.ShapeDtypeStruct((r, c), F32), name=name)(a)


def _head_rows(v, g_n, r):
    return jnp.pad(v.reshape(g_n, 1, r), ((0, 0), (0, 0), (0, HEAD_LANES - r)))


class _Later:
    def __init__(self, gather_pool_in, pool_in_of, gather_rest, rest_of, scatter_of):
        self.gather_pool_in, self.pool_in_of = gather_pool_in, pool_in_of
        self.gather_rest, self.rest_of, self.scatter_of = gather_rest, rest_of, scatter_of


def _local_step(x, target, ln_g, final_g, wt_in, conv_w, conv_b, dt_bias, a_log, d_skip, norm_g, scale, later):
    l, d = x.shape
    di = norm_g.shape[1]
    h = dt_bias.shape[1]
    p = di // h
    conv = conv_b.shape[1]
    n, t = SSM_STATE, SSD_CHUNK
    g_n = (conv - di) // (2 * n)
    r = h // g_n
    nm = di + conv
    dp = scale.shape[1]
    ng = len(POOL_WINDOWS)
    dims = (h, g_n, r, p, n, t)
    pw = _t(d, 1024)

    h0, r0 = _rms_fwd("rms0_fwd", x, ln_g[0:1])
    pm = _mm_nt("in_proj_main", h0, wt_in, 0, nm, exch=later.gather_pool_in)
    if later.gather_pool_in is not None:
        pm, *arrived = pm
    else:
        arrived = None
    wp_in = later.pool_in_of(arrived)
    dtr = _mm_nt("in_proj_dt", h0, wt_in, nm, h, tn=h)
    xbc = _conv_fwd(pm, di, conv_w, conv_b)
    sel = np.zeros((g_n, h, HEAD_LANES), np.float32)
    for gg in range(g_n):
        sel[gg, gg * r + np.arange(r), np.arange(r)] = 1.0
    sel = jnp.asarray(sel)
    dtb, alog, dsk = _head_rows(dt_bias, g_n, r), _head_rows(a_log, g_n, r), _head_rows(d_skip, g_n, r)
    y, states, *arrived = _ssd_fwd(xbc, dtr, sel, dtb, alog, dsk, dims, exch=later.gather_rest)
    w_out, wg, wp_out = later.rest_of(arrived)
    yn, r_n = _gate_norm_fwd(y, pm, norm_g)
    x1 = _mm_nn("ssm_out_proj", yn, w_out, add=x)

    h1, r1 = _rms_fwd("rms1_fwd", x1, ln_g[1:2])
    pu = _mm_nn_blocked("pool_in_proj", h1, wp_in)
    mp = _pool_mix("pool_mix_fwd", pu, dp, False, BF)
    mg, yp = _group_mm_fwd(mp, wg, pu, scale)
    x2 = _mm_nn("pool_out_proj", yp, wp_out, add=x1)

    dx2, d_final_g, loss = _loss_head(x2, final_g, target)

    dyp = _mm_nt("pool_out_bwd_data", dx2, wp_out, 0, dp, tn=1024)
    gw_pout = _mm_tn_packed("pool_out_bwd_weight", yp, dx2, pw)
    dmg, dgate, d_scale = _pool_gate_bwd(dyp, mg, pu, scale)
    dmp = _group_mm_bwd_data(dmg, wg)
    gw_g = _group_mm_bwd_weight(mp, dmg, ng)
    du = _pool_mix("pool_mix_bwd", dmp, dp, True, BF)
    dh1 = _mm_nt_blocked("pool_in_bwd_data_u", du, wp_in, 0)
    dh1 = _mm_nt_blocked("pool_in_bwd_data_gate", dgate, wp_in, N_DEV // 2, add=dh1)
    pw_in = 2 * dp // N_DEV
    gw_pin = jnp.concatenate([_mm_tn_packed("pool_in_bwd_weight_u", h1, du, pw_in),
                              _mm_tn_packed("pool_in_bwd_weight_gate", h1, dgate, pw_in)], axis=0)
    dx1, d_ln1 = _rms_bwd("rms1_bwd", dh1, x1, r1, ln_g[1:2], dx2)

    dyn = _mm_nt("ssm_out_bwd_data", dx1, w_out, 0, di, tn=1024)
    gw_out = _mm_tn_packed("ssm_out_bwd_weight", yn, dx1, pw)
    dy, dz, d_norm_g = _gate_norm_bwd(dyn, y, pm, r_n, norm_g)
    big = dict(w_out=gw_out, wp_in=gw_pin, wg=gw_g, wp_out=gw_pout)
    dxs, db, dc, ddt_g, dbias_g, dalog_g, dd_g, *received = _ssd_bwd(
        xbc, dtr, sel, dtb, alog, dsk, states, dy, dims, exch=later.scatter_of(big))
    dxbc = jnp.concatenate([dxs, db, dc], axis=1)
    dxr, d_conv_w, d_conv_b = _conv_bwd(pm, di, conv_w, conv_b, dxbc)
    ddt = jnp.transpose(ddt_g[:, :, :r], (1, 0, 2)).reshape(l, h)
    dh0 = _mm_nn("in_proj_bwd_data_z", dz, wt_in, 0)
    dh0 = _mm_nn("in_proj_bwd_data_xbc", dxr, wt_in, di, add=dh0)
    dh0 = _mm_nn("in_proj_bwd_data_dt", ddt, wt_in, nm, add=dh0)
    gwt_in = jnp.concatenate([_mm_tn_packed("in_proj_bwd_weight_z", dz, h0, pw),
                              _mm_tn_packed("in_proj_bwd_weight_xbc", dxr, h0, pw),
                              _mm_tn_packed("in_proj_bwd_weight_dt", ddt, h0, pw)], axis=1)
    grad_x, d_ln0 = _rms_bwd("rms0_bwd", dh0, x, r0, ln_g[0:1], dx1)

    def heads(v):
        return v[:, 0, :r].reshape(1, h)

    small = dict(ln_g=jnp.concatenate([d_ln0, d_ln1], axis=0), final_g=d_final_g, conv_w=d_conv_w, conv_b=d_conv_b,
                 dt_bias=heads(dbias_g), a_log=heads(dalog_g), d_skip=heads(dd_g), norm_g=d_norm_g, scale=d_scale)
    big["wt_in"] = gwt_in
    return loss, grad_x, small, big, received


SMALL_ORDER = ("ln_g", "final_g", "conv_w", "conv_b", "dt_bias", "a_log", "d_skip", "norm_g", "scale", "loss")


def _flatten_small(parts):
    flat = jnp.concatenate([parts[k].reshape(-1) for k in SMALL_ORDER])
    n = flat.shape[0]
    rows = -(-n // 1024) * 8
    return jnp.pad(flat, (0, rows * 128 - n)).reshape(rows, 128)


def _split_small(flat, shapes):
    flat = flat.reshape(-1)
    out, off = {}, 0
    for k in SMALL_ORDER:
        size = int(np.prod(shapes[k]))
        out[k] = flat[off:off + size].reshape(shapes[k])
        off += size
    return out


def kernel(x, ln_g, final_g, ssm_w_in, ssm_conv_w, ssm_conv_b, ssm_dt_bias, ssm_a_log, ssm_d, ssm_norm_g, ssm_w_out, pool_w_in, pool_w_group, pool_scale, pool_w_out, loss_target, m_ln_g, m_final_g, m_ssm_w_in, m_ssm_conv_w, m_ssm_conv_b, m_ssm_dt_bias, m_ssm_a_log, m_ssm_d, m_ssm_norm_g, m_ssm_w_out, m_pool_w_in, m_pool_w_group, m_pool_scale, m_pool_w_out, v_ln_g, v_final_g, v_ssm_w_in, v_ssm_conv_w, v_ssm_conv_b, v_ssm_dt_bias, v_ssm_a_log, v_ssm_d, v_ssm_norm_g, v_ssm_w_out, v_pool_w_in, v_pool_w_group, v_pool_scale, v_pool_w_out):
    l, d = x.shape[1], x.shape[2]
    me = 4 * lax.axis_index("x") + 2 * lax.axis_index("y") + lax.axis_index("c")
    ng, gds, gd = pool_w_group.shape[1], pool_w_group.shape[2], pool_w_group.shape[3]
    sin_s = ssm_w_in.shape[2]
    dp_s = pool_w_out.shape[1]
    conv_s = ssm_conv_w.shape[2]

    wt_in_s = _pack_rows("pack_w_in", jnp.transpose(ssm_w_in[0]))
    w_out_s = _cast_bf16("cast_w_out", ssm_w_out[0])
    wp_in_s = _cast_bf16("cast_pool_w_in", pool_w_in[0])
    wg_s = _cast_bf16("cast_pool_w_group", pool_w_group[0].reshape(ng * gds, gd))
    wp_out_s = _cast_bf16("cast_pool_w_out", pool_w_out[0])
    small_s = jnp.concatenate([ssm_conv_w[0].reshape(-1), pool_scale[0]])
    n_small = small_s.shape[0]
    small_s = jnp.pad(small_s, (0, -(-n_small // 1024) * 1024 - n_small)).reshape(-1, 128)
    wt_in_g, small_g = _exchange_alone("all_gather_w_in", _Gather([wt_in_s, small_s]))
    wt_in = _unpack_rows("unpack_w_in", wt_in_g.reshape(N_DEV * sin_s, d // 2))
    small_all = small_g.reshape(N_DEV, -1)[:, :n_small]
    conv_w = jnp.transpose(small_all[:, :CONV_TAPS * conv_s].reshape(N_DEV, CONV_TAPS, conv_s), (1, 0, 2))
    conv_w = conv_w.reshape(CONV_TAPS, -1)
    scale = small_all[:, CONV_TAPS * conv_s:].reshape(1, -1)

    def rest_of(arrived):
        w_out_g, wg_g, wp_out_g = arrived
        wg = jnp.transpose(wg_g.reshape(N_DEV, ng, gds, gd), (1, 0, 2, 3)).reshape(ng, gd, gd)
        return w_out_g.reshape(-1, d), wg, wp_out_g.reshape(-1, d)

    def rows_major(gp):
        q, _, hw = gp.shape
        return jnp.transpose(gp.reshape(q, N_DEV, -1, hw), (1, 0, 2, 3))

    def scatter_of(big):
        return _Scatter([rows_major(big["w_out"]), big["wp_in"][:, None],
                         jnp.transpose(big["wg"].reshape(ng, N_DEV, gds, gd // 2), (1, 0, 2, 3)),
                         rows_major(big["wp_out"])])

    later = _Later(_Gather([wp_in_s]), lambda arrived: arrived[0], _Gather([w_out_s, wg_s, wp_out_s]), rest_of,
                   scatter_of)

    loss, grad_x, small, big, recv = _local_step(
        x[0], loss_target[0], ln_g, final_g.reshape(1, d), wt_in, conv_w, ssm_conv_b, ssm_dt_bias, ssm_a_log, ssm_d,
        ssm_norm_g, scale, later)
    recv_w_in, = _exchange_alone("exchange_w_in_grad", _Scatter([rows_major(big["wt_in"])]))

    def reduced(name, rv):
        q = rv.shape[1]
        cols = [_reduce_packed(f"{name}_{j}", rv[:, j]) for j in range(q)]
        return cols[0] if q == 1 else jnp.concatenate(cols, axis=1)

    g_w_in = jnp.transpose(reduced("reduce_w_in", recv_w_in))[None]
    g_w_out = reduced("reduce_w_out", recv[0])[None]
    g_wp_in = reduced("reduce_pool_w_in", recv[1])[None]
    g_wg = _reduce_packed("reduce_pool_w_group", recv[2].reshape(N_DEV, ng * gds, gd // 2)).reshape(1, ng, gds, gd)
    g_wp_out = reduced("reduce_pool_w_out", recv[3])[None]

    small["loss"] = loss
    shapes = {k: small[k].shape for k in SMALL_ORDER}
    gathered_small, = _exchange_alone("all_gather_small_grads", _Gather([_flatten_small(small)]))
    summed = _split_small(_sum_slots("sum_small_grads", gathered_small), shapes)
    g_conv_w = lax.dynamic_slice_in_dim(summed["conv_w"], me * conv_s, conv_s, axis=1)[None]
    g_scale = lax.dynamic_slice_in_dim(summed["scale"], me * dp_s, dp_s, axis=1)
    grads = dict(ln_g=summed["ln_g"], final_g=summed["final_g"].reshape(d), ssm_w_in=g_w_in, ssm_conv_w=g_conv_w,
                 ssm_conv_b=summed["conv_b"], ssm_dt_bias=summed["dt_bias"], ssm_a_log=summed["a_log"],
                 ssm_d=summed["d_skip"], ssm_norm_g=summed["norm_g"], ssm_w_out=g_w_out, pool_w_in=g_wp_in,
                 pool_w_group=g_wg, pool_scale=g_scale, pool_w_out=g_wp_out)

    weights = dict(ln_g=ln_g, final_g=final_g, ssm_w_in=ssm_w_in, ssm_conv_w=ssm_conv_w, ssm_conv_b=ssm_conv_b,
                   ssm_dt_bias=ssm_dt_bias, ssm_a_log=ssm_a_log, ssm_d=ssm_d, ssm_norm_g=ssm_norm_g,
                   ssm_w_out=ssm_w_out, pool_w_in=pool_w_in, pool_w_group=pool_w_group, pool_scale=pool_scale,
                   pool_w_out=pool_w_out)
    m_in = dict(ln_g=m_ln_g, final_g=m_final_g, ssm_w_in=m_ssm_w_in, ssm_conv_w=m_ssm_conv_w, ssm_conv_b=m_ssm_conv_b,
                ssm_dt_bias=m_ssm_dt_bias, ssm_a_log=m_ssm_a_log, ssm_d=m_ssm_d, ssm_norm_g=m_ssm_norm_g,
                ssm_w_out=m_ssm_w_out, pool_w_in=m_pool_w_in, pool_w_group=m_pool_w_group, pool_scale=m_pool_scale,
                pool_w_out=m_pool_w_out)
    v_in = dict(ln_g=v_ln_g, final_g=v_final_g, ssm_w_in=v_ssm_w_in, ssm_conv_w=v_ssm_conv_w, ssm_conv_b=v_ssm_conv_b,
                ssm_dt_bias=v_ssm_dt_bias, ssm_a_log=v_ssm_a_log, ssm_d=v_ssm_d, ssm_norm_g=v_ssm_norm_g,
                ssm_w_out=v_ssm_w_out, pool_w_in=v_pool_w_in, pool_w_group=v_pool_w_group, pool_scale=v_pool_scale,
                pool_w_out=v_pool_w_out)
    names = list(weights)
    big_names = ("ssm_w_in", "ssm_w_out", "pool_w_in", "pool_w_group", "pool_w_out")
    delta, new_m, new_v = {}, {}, {}
    for k in big_names:
        shp = weights[k].shape
        two_d = (-1, shp[-1])
        dk, mk, vk = _adamw(f"adamw_{k}", weights[k].reshape(two_d), grads[k].reshape(two_d), m_in[k].reshape(two_d),
                            v_in[k].reshape(two_d))
        delta[k], new_m[k], new_v[k] = dk.reshape(shp), mk.reshape(shp), vk.reshape(shp)
    small_names = [k for k in names if k not in big_names]

    def packed(tree):
        flat = jnp.concatenate([tree[k].reshape(-1) for k in small_names])
        nn = flat.shape[0]
        return jnp.pad(flat, (0, -(-nn // 1024) * 1024 - nn), constant_values=1.0).reshape(-1, 128)

    ds, ms, vs = _adamw("adamw_small", packed(weights), packed(grads), packed(m_in), packed(v_in))
    off = 0
    for k in small_names:
        shp = weights[k].shape
        size = int(np.prod(shp))
        for res, arr in ((delta, ds), (new_m, ms), (new_v, vs)):
            res[k] = arr.reshape(-1)[off:off + size].reshape(shp)
        off += size

    return (summed["loss"].reshape(()), grad_x[None], *[grads[k] for k in names], *[delta[k] for k in names],
            *[new_m[k] for k in names], *[new_v[k] for k in names])
```

```python
import math

import jax
import jax.numpy as jnp
import numpy as np
from jax import lax
from jax.experimental import pallas as pl
from jax.experimental.pallas import tpu as pltpu

F32 = jnp.float32
BF = jnp.bfloat16
U32 = jnp.uint32

N_DEV = 8
EPS = 1e-6
SSD_CHUNK = 64
SSM_STATE = 128
CONV_TAPS = 4
HEAD_LANES = 128
POOL_WINDOWS = (2, 4, 8, 16)
ADAM_LR, ADAM_B1, ADAM_B2, ADAM_EPS, ADAM_WD, ADAM_STEP = 0.001, 0.9, 0.999, 1e-08, 0.01, 10
VMEM_LIMIT = 56 * 1024 * 1024
NEG_BIG = -1e30

NN = ((1,), (0,))
NT = ((1,), (1,))
TN = ((0,), (0,))
MESH = pl.DeviceIdType.MESH


def _t(dim, pref):
    return pref if dim % pref == 0 else dim


def _row_tile(rows, pref, mult=8):
    best = rows
    for cand in range(mult, min(rows, pref) + 1, mult):
        if rows % cand == 0:
            best = cand
    return best


def _params(sem=None):
    return pltpu.CompilerParams(dimension_semantics=sem, vmem_limit_bytes=VMEM_LIMIT)


def _silu(x):
    return x * (1.0 / (1.0 + jnp.exp(-x)))


def _dsilu(x):
    s = 1.0 / (1.0 + jnp.exp(-x))
    return s * (1.0 + x * (1.0 - s))


def _bdot(a, b, dims=NN):
    return lax.dot_general(a.astype(BF), b.astype(BF), (dims, ((), ())), preferred_element_type=F32)


def _pack_pairs(x):
    h = x.shape[1] // 2
    hi = lax.bitcast_convert_type(x[:, :h].astype(jnp.bfloat16).astype(F32), U32)
    lo = lax.bitcast_convert_type(x[:, h:].astype(jnp.bfloat16).astype(F32), U32)
    return lax.bitcast_convert_type(hi | (lo >> 16), F32)


def _unpack_pairs(w):
    u = lax.bitcast_convert_type(w, U32)
    hi = lax.bitcast_convert_type(u & jnp.uint32(0xFFFF0000), F32)
    lo = lax.bitcast_convert_type(u << 16, F32)
    return hi, lo


def _mesh_pos():
    return lax.axis_index("x"), lax.axis_index("y"), lax.axis_index("c")


def _slot(pos):
    return 4 * pos[0] + 2 * pos[1] + pos[2]


class _Gather:
    def __init__(self, arrays):
        self.arrays = list(arrays)
        self.out_shapes = [jax.ShapeDtypeStruct((N_DEV, *s.shape), s.dtype) for s in arrays]

    def phases(self, src, dst, send_sems, recv_sems, local_sems):
        n_arr = len(self.arrays)
        x, y, c = _mesh_pos()
        me, sibling = (x, y, c), (x, y, 1 - c)
        chips = [(1 - x, y), (x, 1 - y), (1 - x, 1 - y)]

        def copy(a, k, block, to, from_src):
            return pltpu.make_async_remote_copy(
                src_ref=src[a] if from_src else dst[a].at[_slot(block)], dst_ref=dst[a].at[_slot(block)],
                send_sem=send_sems.at[a * 7 + k], recv_sem=recv_sems.at[a * 7 + k], device_id=to, device_id_type=MESH)

        def mine(a):
            return pltpu.make_async_copy(src[a], dst[a].at[_slot(me)], local_sems.at[a])

        def first(a):
            return [copy(a, 0, me, sibling, True)] + [copy(a, 1 + j, me, (*chip, c), True)
                                                     for j, chip in enumerate(chips)]

        def start():
            for a in range(n_arr):
                mine(a).start()
                for cp in first(a):
                    cp.start()

        def middle():
            for j, chip in enumerate(chips):
                for a in range(n_arr):
                    copy(a, 1 + j, (*chip, c), me, False).wait_recv()
                    copy(a, 4 + j, (*chip, c), sibling, False).start()

        def finish():
            for a in range(n_arr):
                copy(a, 0, sibling, me, False).wait_recv()
                for j, chip in enumerate(chips):
                    copy(a, 4 + j, (*chip, 1 - c), me, False).wait_recv()
                for cp in first(a):
                    cp.wait_send()
                for j, chip in enumerate(chips):
                    copy(a, 4 + j, (*chip, c), sibling, False).wait_send()
                mine(a).wait()

        return start, middle, finish


class _Scatter:
    def __init__(self, arrays):
        self.arrays = list(arrays)
        self.out_shapes = [jax.ShapeDtypeStruct(b.shape, b.dtype) for b in arrays]

    def phases(self, src, dst, send_sems, recv_sems, local_sems):
        n_arr = len(self.arrays)
        x, y, c = _mesh_pos()
        me = (x, y, c)
        peers = []
        for k in range(1, N_DEV):
            fx, fy, fc = (k >> 2) & 1, (k >> 1) & 1, k & 1
            peers.append((1 - x if fx else x, 1 - y if fy else y, 1 - c if fc else c))

        def copy(a, k, peer, sending):
            return pltpu.make_async_remote_copy(
                src_ref=src[a].at[_slot(peer)], dst_ref=dst[a].at[_slot(me) if sending else _slot(peer)],
                send_sem=send_sems.at[a * 7 + k], recv_sem=recv_sems.at[a * 7 + k], device_id=peer,
                device_id_type=MESH)

        def mine(a):
            return pltpu.make_async_copy(src[a].at[_slot(me)], dst[a].at[_slot(me)], local_sems.at[a])

        def start():
            for a in range(n_arr):
                mine(a).start()
                for k, peer in enumerate(peers):
                    copy(a, k, peer, True).start()

        def finish():
            for a in range(n_arr):
                for k, peer in enumerate(peers):
                    copy(a, k, peer, False).wait_recv()
                for k, peer in enumerate(peers):
                    copy(a, k, peer, True).wait_send()
                mine(a).wait()

        return start, None, finish


def _hosted_call(name, body, grid, in_specs, out_specs, out_shape, scratch_shapes, sem, operands, exch=None):
    if exch is None:
        return pl.pallas_call(body, grid=grid, in_specs=in_specs, out_specs=out_specs, out_shape=out_shape,
                              scratch_shapes=scratch_shapes, compiler_params=_params(sem), name=name)(*operands)
    n_in, n_out, n_scr, ne = len(in_specs), len(out_specs), len(scratch_shapes), len(exch.arrays)
    total = math.prod(grid)

    def wrapped(*refs):
        ins, ex_in = refs[:n_in], refs[n_in:n_in + ne]
        outs = refs[n_in + ne:n_in + ne + n_out]
        ex_out = refs[n_in + ne + n_out:n_in + 2 * ne + n_out]
        scr = refs[n_in + 2 * ne + n_out:n_in + 2 * ne + n_out + n_scr]
        step = 0
        for axis, size in enumerate(grid):
            step = step * size + pl.program_id(axis)
        start, middle, finish = exch.phases(ex_in, ex_out, *refs[-3:])
        pl.when(step == 0)(start)
        if middle is not None:
            pl.when(step == total // 2)(middle)
        body(*ins, *outs, *scr)
        pl.when(step == total - 1)(finish)

    hbm = pl.BlockSpec(memory_space=pl.ANY)
    sems = [pltpu.SemaphoreType.DMA((ne * 7,)), pltpu.SemaphoreType.DMA((ne * 7,)), pltpu.SemaphoreType.DMA((ne,))]
    return pl.pallas_call(
        wrapped, grid=grid, in_specs=[*in_specs, *[hbm] * ne], out_specs=[*out_specs, *[hbm] * ne],
        out_shape=[*out_shape, *exch.out_shapes], scratch_shapes=[*scratch_shapes, *sems],
        compiler_params=pltpu.CompilerParams(dimension_semantics=("arbitrary",) * len(grid),
                                             vmem_limit_bytes=VMEM_LIMIT, has_side_effects=True),
        name=name)(*operands, *exch.arrays)


def _exchange_alone(name, exch):
    def body():
        pass

    return _hosted_call(name, body, (1,), [], [], [], [], None, (), exch)


def _mm(name, grid, a, a_spec, b, b_spec, dims, outs, o_specs, acc_shape, extra=(), extra_specs=(), epi=None,
        exch=None):
    nk = grid[-1]
    n_extra, n_out = len(extra), len(outs)

    def body(*refs):
        a_ref, b_ref = refs[0], refs[1]
        ex = refs[2:2 + n_extra]
        o_refs = refs[2 + n_extra:2 + n_extra + n_out]
        acc = refs[-1]
        k = pl.program_id(len(grid) - 1)

        @pl.when(k == 0)
        def _():
            acc[...] = jnp.zeros_like(acc)

        acc[...] += _bdot(a_ref[...], b_ref[...], dims)

        @pl.when(k == nk - 1)
        def _():
            res = acc[...]
            res = (res,) if epi is None else epi(res, *[e[...] for e in ex])
            for o, r in zip(o_refs, res):
                o[...] = r.astype(o.dtype)

    sem = ("parallel",) * (len(grid) - 1) + ("arbitrary",)
    return _hosted_call(name, body, grid, [a_spec, b_spec, *extra_specs], list(o_specs), list(outs),
                        [pltpu.VMEM(acc_shape, F32)], sem, (a, b, *extra), exch)


def _add_epi(acc, add):
    return (acc + add,)


def _pack_epi(acc):
    return (_pack_pairs(acc),)


def _mm_nt(name, a, b, n_off, n, out_dtype=F32, tm=1024, tn=512, exch=None):
    m, kk = a.shape
    tm, tn = _t(m, tm), math.gcd(_t(n, tn), n_off)
    res = _mm(name, (m // tm, n // tn, 1),
              a, pl.BlockSpec((tm, kk), lambda i, j, k: (i, 0)),
              b, pl.BlockSpec((tn, kk), lambda i, j, k: (n_off // tn + j, 0)), NT,
              [jax.ShapeDtypeStruct((m, n), out_dtype)], [pl.BlockSpec((tm, tn), lambda i, j, k: (i, j))], (tm, tn),
              exch=exch)
    return res[0] if exch is None else res


def _mm_nn_blocked(name, a, b3, tm=1024, tn=512):
    m, kk = a.shape
    nb, _, cs = b3.shape
    tm, tn = _t(m, tm), _t(cs, tn)
    per = cs // tn
    return _mm(name, (m // tm, nb * per, 1),
               a, pl.BlockSpec((tm, kk), lambda i, j, k: (i, 0)),
               b3, pl.BlockSpec((None, kk, tn), lambda i, j, k: (j // per, 0, j % per)), NN,
               [jax.ShapeDtypeStruct((m, nb * cs), F32)], [pl.BlockSpec((tm, tn), lambda i, j, k: (i, j))],
               (tm, tn))[0]


def _mm_nt_blocked(name, a, b3, blk_off, add=None, tm=1024, tn=1024):
    m = a.shape[0]
    _, n, cs = b3.shape
    nk = a.shape[1] // cs
    tm, tn = _t(m, tm), _t(n, tn)
    extra, especs, epi = (), (), None
    if add is not None:
        extra, especs, epi = (add,), (pl.BlockSpec((tm, tn), lambda i, j, k: (i, j)),), _add_epi
    return _mm(name, (m // tm, n // tn, nk),
               a, pl.BlockSpec((tm, cs), lambda i, j, k: (i, k)),
               b3, pl.BlockSpec((None, tn, cs), lambda i, j, k: (blk_off + k, j, 0)), NT,
               [jax.ShapeDtypeStruct((m, n), F32)], [pl.BlockSpec((tm, tn), lambda i, j, k: (i, j))], (tm, tn),
               extra, especs, epi)[0]


def _mm_nn(name, a, b, k_off=0, kk=None, add=None, out_dtype=F32, tm=1024, tn=1024, tk=1024):
    m = a.shape[0]
    kk = a.shape[1] if kk is None else kk
    n = b.shape[1]
    tm, tn, tk = _t(m, tm), _t(n, tn), math.gcd(_t(kk, tk), k_off)
    extra, especs, epi = (), (), None
    if add is not None:
        extra, especs, epi = (add,), (pl.BlockSpec((tm, tn), lambda i, j, k: (i, j)),), _add_epi
    return _mm(name, (m // tm, n // tn, kk // tk),
               a, pl.BlockSpec((tm, tk), lambda i, j, k: (i, k)),
               b, pl.BlockSpec((tk, tn), lambda i, j, k: (k_off // tk + k, j)), NN,
               [jax.ShapeDtypeStruct((m, n), out_dtype)], [pl.BlockSpec((tm, tn), lambda i, j, k: (i, j))], (tm, tn),
               extra, especs, epi)[0]


def _mm_tn_packed(name, a, b, pw, tm=512, tk=1024):
    kk, m = a.shape
    n = b.shape[1]
    tm, tk = _t(m, tm), _t(kk, tk)
    return _mm(name, (m // tm, n // pw, kk // tk),
               a, pl.BlockSpec((tk, tm), lambda i, j, k: (k, i)),
               b, pl.BlockSpec((tk, pw), lambda i, j, k: (k, j)), TN,
               [jax.ShapeDtypeStruct((n // pw, m, pw // 2), F32)],
               [pl.BlockSpec((None, tm, pw // 2), lambda i, j, k: (j, i, 0))], (tm, pw), epi=_pack_epi)[0]


def _rms_fwd(name, x, g, tm=256):
    l, d = x.shape
    tm = _t(l, tm)

    def body(x_ref, g_ref, h_ref, r_ref):
        xv = x_ref[...]
        r = lax.rsqrt(jnp.mean(xv * xv, axis=-1, keepdims=True) + EPS)
        h_ref[...] = (xv * r * g_ref[...]).astype(BF)
        r_ref[...] = r

    return pl.pallas_call(
        body, grid=(l // tm,),
        in_specs=[pl.BlockSpec((tm, d), lambda i: (i, 0)), pl.BlockSpec((1, d), lambda i: (0, 0))],
        out_specs=[pl.BlockSpec((tm, d), lambda i: (i, 0)), pl.BlockSpec((tm, 1), lambda i: (i, 0))],
        out_shape=[jax.ShapeDtypeStruct((l, d), BF), jax.ShapeDtypeStruct((l, 1), F32)],
        compiler_params=_params(("parallel",)), name=name)(x, g)


def _rms_bwd(name, dh, x, r, g, dres, tm=256):
    l, d = x.shape
    tm = _t(l, tm)

    def body(dh_ref, x_ref, r_ref, g_ref, dres_ref, dx_ref, dg_ref):
        i = pl.program_id(0)
        rr = r_ref[...]
        xhat = x_ref[...] * rr
        dhv = dh_ref[...]
        dxh = dhv * g_ref[...]
        dx_ref[...] = dres_ref[...] + rr * (dxh - xhat * jnp.mean(dxh * xhat, axis=-1, keepdims=True))

        @pl.when(i == 0)
        def _():
            dg_ref[...] = jnp.zeros_like(dg_ref)

        dg_ref[...] += jnp.sum(dhv * xhat, axis=0, keepdims=True)

    row = pl.BlockSpec((tm, d), lambda i: (i, 0))
    vec = pl.BlockSpec((1, d), lambda i: (0, 0))
    return pl.pallas_call(
        body, grid=(l // tm,), in_specs=[row, row, pl.BlockSpec((tm, 1), lambda i: (i, 0)), vec, row],
        out_specs=[row, vec], out_shape=[jax.ShapeDtypeStruct((l, d), F32), jax.ShapeDtypeStruct((1, d), F32)],
        compiler_params=_params(("arbitrary",)), name=name)(dh, x, r, g, dres)


def _loss_head(x, g, target, tm=256):
    l, d = x.shape
    tm = _t(l, tm)

    def body(x_ref, g_ref, t_ref, dx_ref, dg_ref, loss_ref):
        i = pl.program_id(0)
        xv = x_ref[...]
        gv = g_ref[...]
        r = lax.rsqrt(jnp.mean(xv * xv, axis=-1, keepdims=True) + EPS)
        xhat = xv * r
        e = xhat * gv - t_ref[...]
        dy = e * (1.0 / d)
        dxh = dy * gv
        dx_ref[...] = r * (dxh - xhat * jnp.mean(dxh * xhat, axis=-1, keepdims=True))

        @pl.when(i == 0)
        def _():
            dg_ref[...] = jnp.zeros_like(dg_ref)
            loss_ref[...] = jnp.zeros_like(loss_ref)

        dg_ref[...] += jnp.sum(dy * xhat, axis=0, keepdims=True)
        loss_ref[...] += 0.5 * jnp.sum(jnp.sum(e * e, axis=-1, keepdims=True) * (1.0 / d), axis=0, keepdims=True)

    row = pl.BlockSpec((tm, d), lambda i: (i, 0))
    vec = pl.BlockSpec((1, d), lambda i: (0, 0))
    return pl.pallas_call(
        body, grid=(l // tm,), in_specs=[row, vec, row],
        out_specs=[row, vec, pl.BlockSpec((1, 1), lambda i: (0, 0))],
        out_shape=[jax.ShapeDtypeStruct((l, d), F32), jax.ShapeDtypeStruct((1, d), F32),
                   jax.ShapeDtypeStruct((1, 1), F32)],
        compiler_params=_params(("arbitrary",)), name="loss_head")(x, g, target)


CONV_HALO = 8


def _conv_pre(x_ref, w_ref, b_ref, i, tr):
    r0 = pl.multiple_of(i * tr, tr)
    cur = x_ref[pl.ds(r0, tr), :]
    prev = x_ref[pl.ds(pl.multiple_of(jnp.maximum(r0 - CONV_HALO, 0), CONV_HALO), CONV_HALO), :]
    prev = jnp.where(i > 0, prev, 0.0)
    ext = jnp.concatenate([prev, cur], axis=0)
    taps = []
    for k in range(CONV_TAPS):
        s = CONV_TAPS - 1 - k
        taps.append(cur if s == 0 else pltpu.roll(ext, s, 0)[CONV_HALO:])
    pre = b_ref[...] + sum(w_ref[k:k + 1, :] * taps[k] for k in range(CONV_TAPS))
    return r0, pre, taps


def _conv_fwd(pm, col_off, conv_w, conv_b, cw=256, tr=512):
    l = pm.shape[0]
    c = conv_w.shape[1]
    cw, tr = _t(c, cw), _t(l, tr)
    assert col_off % cw == 0

    def body(x_ref, w_ref, b_ref, o_ref):
        def step(i, carry):
            r0, pre, _ = _conv_pre(x_ref, w_ref, b_ref, i, tr)
            o_ref[pl.ds(r0, tr), :] = _silu(pre)
            return carry
        lax.fori_loop(0, l // tr, step, 0)

    return pl.pallas_call(
        body, grid=(c // cw,),
        in_specs=[pl.BlockSpec((l, cw), lambda j: (0, col_off // cw + j)),
                  pl.BlockSpec((CONV_TAPS, cw), lambda j: (0, j)), pl.BlockSpec((1, cw), lambda j: (0, j))],
        out_specs=pl.BlockSpec((l, cw), lambda j: (0, j)), out_shape=jax.ShapeDtypeStruct((l, c), F32),
        compiler_params=_params(("parallel",)), name="conv_fwd")(pm, conv_w, conv_b)


def _conv_bwd(pm, col_off, conv_w, conv_b, dy, cw=256, tr=512):
    l = pm.shape[0]
    c = conv_w.shape[1]
    cw, tr = _t(c, cw), _t(l, tr)
    nt = l // tr

    def body(x_ref, w_ref, b_ref, dy_ref, dx_ref, dw_ref, db_ref, dpre_ref):
        def step1(i, carry):
            dws, db = carry
            r0, pre, taps = _conv_pre(x_ref, w_ref, b_ref, i, tr)
            dpre = dy_ref[pl.ds(r0, tr), :] * _dsilu(pre)
            dpre_ref[pl.ds(r0, tr), :] = dpre
            dws = tuple(dws[k] + jnp.sum(dpre * taps[k], axis=0, keepdims=True) for k in range(CONV_TAPS))
            return dws, db + jnp.sum(dpre, axis=0, keepdims=True)

        z = jnp.zeros((1, cw), F32)
        dws, db = lax.fori_loop(0, nt, step1, ((z,) * CONV_TAPS, z))
        for k in range(CONV_TAPS):
            dw_ref[k:k + 1, :] = dws[k]
        db_ref[...] = db

        def step2(i, carry):
            r0 = pl.multiple_of(i * tr, tr)
            cur = dpre_ref[pl.ds(r0, tr), :]
            nxt = dpre_ref[pl.ds(pl.multiple_of(jnp.minimum(r0 + tr, l - CONV_HALO), CONV_HALO), CONV_HALO), :]
            nxt = jnp.where(i < nt - 1, nxt, 0.0)
            ext = jnp.concatenate([cur, nxt], axis=0)
            acc = w_ref[CONV_TAPS - 1:CONV_TAPS, :] * cur
            for k in range(CONV_TAPS - 1):
                s = CONV_TAPS - 1 - k
                acc = acc + w_ref[k:k + 1, :] * pltpu.roll(ext, tr + CONV_HALO - s, 0)[:tr]
            dx_ref[pl.ds(r0, tr), :] = acc.astype(dx_ref.dtype)
            return carry
        lax.fori_loop(0, nt, step2, 0)

    col = pl.BlockSpec((l, cw), lambda j: (0, j))
    return pl.pallas_call(
        body, grid=(c // cw,),
        in_specs=[pl.BlockSpec((l, cw), lambda j: (0, col_off // cw + j)),
                  pl.BlockSpec((CONV_TAPS, cw), lambda j: (0, j)), pl.BlockSpec((1, cw), lambda j: (0, j)), col],
        out_specs=[col, pl.BlockSpec((CONV_TAPS, cw), lambda j: (0, j)), pl.BlockSpec((1, cw), lambda j: (0, j))],
        out_shape=[jax.ShapeDtypeStruct((l, c), BF), jax.ShapeDtypeStruct((CONV_TAPS, c), F32),
                   jax.ShapeDtypeStruct((1, c), F32)],
        scratch_shapes=[pltpu.VMEM((l, cw), F32)],
        compiler_params=_params(("parallel",)), name="conv_bwd")(pm, conv_w, conv_b, dy)


def _split(x, pieces):
    out, rest = [], x
    for _ in range(pieces):
        piece = rest.astype(BF)
        out.append(piece)
        rest = rest - piece.astype(F32)
    return out


def _rows_times(xs, m_stack, pieces):
    x = xs[0] if len(xs) == 1 else jnp.concatenate(xs, axis=0)
    out = _bdot(jnp.concatenate(_split(x, pieces), axis=1), m_stack)
    sizes = [v.shape[0] for v in xs]
    offs = np.cumsum([0] + sizes)
    return [out[offs[i]:offs[i + 1]] for i in range(len(xs))]


def _times_rows(m_stack, x, pieces):
    return _bdot(m_stack, jnp.concatenate(_split(x, pieces), axis=0))


EXPAND_PIECES = 3
FOLD_PIECES = 2


def _ssd_consts(r, p, t):
    assert p == t, "heads expand to P lanes of the inputs and to T lanes of the decay matrices alike"
    rp = r * p
    tri = np.tril(np.ones((t, t), np.float32))
    ep = np.zeros((HEAD_LANES, rp), np.float32)
    ep[np.arange(rp) // p, np.arange(rp)] = 1.0
    itile = (np.arange(t)[:, None] == (np.arange(rp) % t)[None, :]).astype(np.float32)
    lmask = (np.arange(t)[:, None] >= (np.arange(rp) % t)[None, :]).astype(np.float32)
    bmask = ((np.arange(rp) // t)[:, None] == (np.arange(rp) // p)[None, :]).astype(np.float32)
    return [jnp.asarray(np.concatenate([ep] * EXPAND_PIECES, axis=0), BF),
            jnp.asarray(np.concatenate([ep.T] * FOLD_PIECES, axis=0), BF),
            jnp.asarray(np.concatenate([tri] * EXPAND_PIECES, axis=1), BF),
            jnp.asarray(np.concatenate([tri.T] * EXPAND_PIECES, axis=1), BF),
            jnp.asarray(ep.T.copy()), jnp.asarray(itile), jnp.asarray(lmask), jnp.asarray(bmask)]


def _ssd_common(xs, bm, pre_raw, dtb, alog, e_stack, tri_stack, itile, lmask, bmask, r, t):
    pre = pre_raw + dtb
    dt = jnp.maximum(pre, 0.0) + jnp.log(1.0 + jnp.exp(-jnp.abs(pre)))
    a = -jnp.exp(alog)
    cs = _times_rows(tri_stack, dt * a, EXPAND_PIECES)
    last = cs[t - 1:t, :]
    ecs, w = jnp.exp(cs), jnp.exp(last - cs)
    dtx, ecsx, wx, csx = _rows_times([dt, ecs, w, cs], e_stack, EXPAND_PIECES)
    csrow = jnp.sum(csx * itile, axis=0, keepdims=True)
    lx = jnp.exp(jnp.where(lmask > 0.0, csx - csrow, NEG_BIG))
    xdt = xs * dtx
    xblk = (jnp.concatenate([xdt] * r, axis=0) * bmask).astype(BF)
    btile = jnp.concatenate([bm] * r, axis=0).astype(BF)
    return pre, dt, a, cs, last, ecs, w, dtx, ecsx, wx, lx, xdt, xblk, btile


def _ssd_specs(l, g_n, r, p, n, t, conv, rev):
    nc = l // t
    rp = r * p
    cidx = (lambda c: nc - 1 - c) if rev else (lambda c: c)
    row_spec = lambda width: pl.BlockSpec((t, width), lambda c: (cidx(c), 0))
    dtr_spec = pl.BlockSpec((g_n, t, HEAD_LANES), lambda c: (0, cidx(c), 0))
    par_spec = pl.BlockSpec((g_n, 1, HEAD_LANES), lambda c: (0, 0, 0))
    dskx_spec = pl.BlockSpec((g_n, 1, rp), lambda c: (0, 0, 0))
    st_spec = pl.BlockSpec((None, g_n, rp, n), lambda c: (cidx(c), 0, 0, 0))
    return nc, row_spec, dtr_spec, par_spec, dskx_spec, st_spec


def _const_specs(consts):
    return [pl.BlockSpec(a.shape, lambda c: (0, 0)) for a in consts]


def _ssd_fwd(xbc, dtr, dtb, alog, dskx, dims, exch=None):
    h, g_n, r, p, n, t = dims
    l, conv = xbc.shape
    rp, hp = r * p, h * p
    nc, row_spec, dtr_spec, par_spec, dskx_spec, st_spec = _ssd_specs(l, g_n, r, p, n, t, conv, False)
    consts = _ssd_consts(r, p, t)

    def body(x_ref, dtr_ref, dtb_ref, alog_ref, dskx_ref,
             e_ref, et_ref, tri_ref, trit_ref, ept_ref, it_ref, lm_ref, bmk_ref, y_ref, st_ref, s_ref):
        @pl.when(pl.program_id(0) == 0)
        def _():
            s_ref[...] = jnp.zeros_like(s_ref)

        for g in range(g_n):
            xs = x_ref[:, g * rp:(g + 1) * rp]
            bm = x_ref[:, hp + g * n:hp + (g + 1) * n]
            cm = x_ref[:, hp + (g_n + g) * n:hp + (g_n + g + 1) * n]
            (_, _, _, _, last, _, _, _, ecsx, wx, lx, xdt, xblk, btile) = _ssd_common(
                xs, bm, dtr_ref[g], dtb_ref[g], alog_ref[g], e_ref[...], tri_ref[...], it_ref[...], lm_ref[...],
                bmk_ref[...], r, t)
            s_in = s_ref[g]
            st_ref[g] = s_in
            cbx = _bdot(cm, btile, NT)
            yd = _bdot(cbx * lx, xblk)
            yo = ecsx * _bdot(cm, s_in, NT)
            y_ref[:, g * rp:(g + 1) * rp] = yd + yo + dskx_ref[g] * xs
            elast = jnp.sum(ept_ref[...] * jnp.exp(last), axis=1, keepdims=True)
            s_ref[g] = elast * s_in + _bdot(xdt * wx, bm, TN)

    return _hosted_call(
        "ssd_fwd", body, (nc,),
        [row_spec(conv), dtr_spec, par_spec, par_spec, dskx_spec, *_const_specs(consts)],
        [row_spec(hp), st_spec],
        [jax.ShapeDtypeStruct((l, hp), F32), jax.ShapeDtypeStruct((nc, g_n, rp, n), F32)],
        [pltpu.VMEM((g_n, rp, n), F32)], ("arbitrary",),
        (xbc, dtr, dtb, alog, dskx, *consts), exch)


def _ssd_bwd(xbc, dtr, dtb, alog, dskx, states, dy, dims, exch=None):
    h, g_n, r, p, n, t = dims
    l, conv = xbc.shape
    rp, hp = r * p, h * p
    nc, row_spec, dtr_spec, par_spec, dskx_spec, st_spec = _ssd_specs(l, g_n, r, p, n, t, conv, True)
    consts = _ssd_consts(r, p, t)

    def fold_rows(v, rows):
        return sum(v[k * rows:(k + 1) * rows, :] for k in range(r))

    def body(x_ref, dtr_ref, dtb_ref, alog_ref, dskx_ref, st_ref, dy_ref,
             e_ref, et_ref, tri_ref, trit_ref, ept_ref, it_ref, lm_ref, bmk_ref,
             dx_ref, ddt_ref, dbias_ref, dalog_ref, dd_ref, ds_ref):
        @pl.when(pl.program_id(0) == 0)
        def _():
            ds_ref[...] = jnp.zeros_like(ds_ref)
            dbias_ref[...] = jnp.zeros_like(dbias_ref)
            dalog_ref[...] = jnp.zeros_like(dalog_ref)
            dd_ref[...] = jnp.zeros_like(dd_ref)

        ept, itile, bmask = ept_ref[...], it_ref[...], bmk_ref[...]
        last_row = lax.broadcasted_iota(jnp.int32, (t, HEAD_LANES), 0) == t - 1
        for g in range(g_n):
            xs = x_ref[:, g * rp:(g + 1) * rp]
            bm = x_ref[:, hp + g * n:hp + (g + 1) * n]
            cm = x_ref[:, hp + (g_n + g) * n:hp + (g_n + g + 1) * n]
            dy = dy_ref[:, g * rp:(g + 1) * rp]
            (pre, dt, a, cs, last, ecs, w, dtx, ecsx, wx, lx, xdt, xblk, btile) = _ssd_common(
                xs, bm, dtr_ref[g], dtb_ref[g], alog_ref[g], e_ref[...], tri_ref[...], itile, lm_ref[...], bmask, r, t)
            s_in = st_ref[g]
            ds_out = ds_ref[g]
            elast_row = jnp.exp(last)
            elast = jnp.sum(ept * elast_row, axis=1, keepdims=True)

            cbx = _bdot(cm, btile, NT)
            amat = cbx * lx
            da = _bdot(dy, xblk, NT)
            dxdt = fold_rows(_bdot(amat, dy, TN) * bmask, t)
            dcbx = da * lx
            q = da * amat
            dc = _bdot(dcbx, btile)
            db = fold_rows(_bdot(dcbx, cm, TN), t)

            g0 = _bdot(cm, s_in, NT)
            dg0 = dy * ecsx
            dc = dc + _bdot(dg0, s_in)

            z = _bdot(bm, ds_out, NT)
            dxdt = dxdt + z * wx
            db = db + _bdot(xdt * wx, ds_out)
            ds_ref[g] = elast * ds_out + _bdot(dg0, cm, TN)

            skip = jnp.sum(dy * xs, axis=0, keepdims=True)
            skip = jnp.concatenate([skip, jnp.zeros((15, rp), F32)], axis=0)
            f_q, f_yo, f_w, f_dt, f_skip = _rows_times(
                [q - itile * jnp.sum(q, axis=0, keepdims=True), dy * g0, z * xdt, dxdt * xs, skip],
                et_ref[...], FOLD_PIECES)
            dd_ref[g] += f_skip[0:1]
            dww = f_w * w
            dcs = f_q + f_yo * ecs - dww
            dlast = jnp.sum(jnp.sum(ds_out * s_in, axis=1, keepdims=True) * ept, axis=0, keepdims=True) * elast_row
            dlast = dlast + jnp.sum(dww, axis=0, keepdims=True)

            dcs = dcs + jnp.where(last_row, dlast, 0.0)
            dadt = _times_rows(trit_ref[...], dcs, EXPAND_PIECES)
            dalog_ref[g] += jnp.sum(dadt * dt, axis=0, keepdims=True) * a
            dpre = (dadt * a + f_dt) * (1.0 / (1.0 + jnp.exp(-pre)))
            ddt_ref[g] = dpre
            dbias_ref[g] += jnp.sum(dpre, axis=0, keepdims=True)
            dx_ref[:, g * rp:(g + 1) * rp] = dskx_ref[g] * dy + dxdt * dtx
            dx_ref[:, hp + g * n:hp + (g + 1) * n] = db
            dx_ref[:, hp + (g_n + g) * n:hp + (g_n + g + 1) * n] = dc

    par_shape = jax.ShapeDtypeStruct((g_n, 1, HEAD_LANES), F32)
    return _hosted_call(
        "ssd_bwd", body, (nc,),
        [row_spec(conv), dtr_spec, par_spec, par_spec, dskx_spec, st_spec, row_spec(hp), *_const_specs(consts)],
        [row_spec(conv), dtr_spec, par_spec, par_spec, par_spec],
        [jax.ShapeDtypeStruct((l, conv), F32), jax.ShapeDtypeStruct((g_n, l, HEAD_LANES), F32),
         par_shape, par_shape, par_shape],
        [pltpu.VMEM((g_n, rp, n), F32)], ("arbitrary",),
        (xbc, dtr, dtb, alog, dskx, states, dy, *consts), exch)


def _gate_norm_fwd(y, pm, g, tm=128):
    l, di = y.shape
    tm = _t(l, tm)

    def body(y_ref, z_ref, g_ref, o_ref, r_ref):
        v = y_ref[...] * _silu(z_ref[...])
        rr = lax.rsqrt(jnp.mean(v * v, axis=-1, keepdims=True) + EPS)
        o_ref[...] = (v * rr * g_ref[...]).astype(BF)
        r_ref[...] = rr

    row = pl.BlockSpec((tm, di), lambda i: (i, 0))
    return pl.pallas_call(
        body, grid=(l // tm,), in_specs=[row, row, pl.BlockSpec((1, di), lambda i: (0, 0))],
        out_specs=[row, pl.BlockSpec((tm, 1), lambda i: (i, 0))],
        out_shape=[jax.ShapeDtypeStruct((l, di), BF), jax.ShapeDtypeStruct((l, 1), F32)],
        compiler_params=_params(("parallel",)), name="gate_norm_fwd")(y, pm, g)


def _gate_norm_bwd(dyn, y, pm, rstd, g, tm=128):
    l, di = y.shape
    tm = _t(l, tm)

    def body(dyn_ref, y_ref, z_ref, r_ref, g_ref, dy_ref, dz_ref, dg_ref):
        i = pl.program_id(0)
        yv, zv, rr = y_ref[...], z_ref[...], r_ref[...]
        sz = _silu(zv)
        vhat = yv * sz * rr
        dn = dyn_ref[...]
        dvh = dn * g_ref[...]
        dv = rr * (dvh - vhat * jnp.mean(dvh * vhat, axis=-1, keepdims=True))
        dy_ref[...] = dv * sz
        dz_ref[...] = (dv * yv * _dsilu(zv)).astype(BF)

        @pl.when(i == 0)
        def _():
            dg_ref[...] = jnp.zeros_like(dg_ref)

        dg_ref[...] += jnp.sum(dn * vhat, axis=0, keepdims=True)

    row = pl.BlockSpec((tm, di), lambda i: (i, 0))
    vec = pl.BlockSpec((1, di), lambda i: (0, 0))
    return pl.pallas_call(
        body, grid=(l // tm,), in_specs=[row, row, row, pl.BlockSpec((tm, 1), lambda i: (i, 0)), vec],
        out_specs=[row, row, vec],
        out_shape=[jax.ShapeDtypeStruct((l, di), F32), jax.ShapeDtypeStruct((l, di), BF),
                   jax.ShapeDtypeStruct((1, di), F32)],
        compiler_params=_params(("arbitrary",)), name="gate_norm_bwd")(dyn, y, pm, rstd, g)


POOL_HALO = 16


def _pool_mix(name, src, dp, backward, out_dtype, cw=256, tr=512):
    l = src.shape[0]
    gd = dp // len(POOL_WINDOWS)
    cwl, trl = _t(gd, cw), _t(l, tr)
    nt = l // trl

    def body(x_ref, o_ref):
        gi = pl.program_id(0)
        for wi, win in enumerate(POOL_WINDOWS):
            @pl.when(gi == wi)
            def _(win=win):
                def step(i, carry):
                    r0 = pl.multiple_of(i * trl, trl)
                    cur = x_ref[pl.ds(r0, trl), :]
                    trow = r0 + lax.broadcasted_iota(jnp.int32, (trl, 1), 0)
                    cnt = jnp.minimum(trow + 1, win).astype(F32)
                    if not backward:
                        halo = x_ref[pl.ds(pl.multiple_of(jnp.maximum(r0 - POOL_HALO, 0), POOL_HALO), POOL_HALO), :]
                        halo = jnp.where(i > 0, halo, 0.0)
                        s = jnp.concatenate([halo, cur], axis=0)
                        sh = 1
                        while sh < win:
                            s = s + pltpu.roll(s, sh, 0)
                            sh *= 2
                        res = s[POOL_HALO:] / cnt - cur
                    else:
                        halo = x_ref[pl.ds(pl.multiple_of(jnp.minimum(r0 + trl, l - POOL_HALO), POOL_HALO),
                                           POOL_HALO), :]
                        hrow = r0 + trl + lax.broadcasted_iota(jnp.int32, (POOL_HALO, 1), 0)
                        hcnt = jnp.minimum(hrow + 1, win).astype(F32)
                        halo = jnp.where(i < nt - 1, halo / hcnt, 0.0)
                        s = jnp.concatenate([cur / cnt, halo], axis=0)
                        sh = 1
                        while sh < win:
                            s = s + pltpu.roll(s, trl + POOL_HALO - sh, 0)
                            sh *= 2
                        res = s[:trl] - cur
                    o_ref[pl.ds(r0, trl), :] = res.astype(o_ref.dtype)
                    return carry
                lax.fori_loop(0, nt, step, 0)

    return pl.pallas_call(
        body, grid=(len(POOL_WINDOWS), gd // cwl),
        in_specs=[pl.BlockSpec((l, cwl), lambda g, j: (0, g * (gd // cwl) + j))],
        out_specs=pl.BlockSpec((l, cwl), lambda g, j: (0, g * (gd // cwl) + j)),
        out_shape=jax.ShapeDtypeStruct((l, dp), out_dtype),
        compiler_params=_params(("parallel", "parallel")), name=name)(src)


def _pool_gate_bwd(dyp, mg, pu, scale, tm=128):
    l, dp = mg.shape
    tm = _t(l, tm)

    def body(dyp_ref, mg_ref, gate_ref, sc_ref, dmg_ref, dgate_ref, dsc_ref):
        i = pl.program_id(0)
        d, m, gt, sc = dyp_ref[...], mg_ref[...], gate_ref[...], sc_ref[...]
        sg = _silu(gt)
        dmg_ref[...] = (d * sc * sg).astype(BF)
        dgate_ref[...] = (d * m * sc * _dsilu(gt)).astype(BF)

        @pl.when(i == 0)
        def _():
            dsc_ref[...] = jnp.zeros_like(dsc_ref)

        dsc_ref[...] += jnp.sum(d * m * sg, axis=0, keepdims=True)

    row = pl.BlockSpec((tm, dp), lambda i: (i, 0))
    vec = pl.BlockSpec((1, dp), lambda i: (0, 0))
    return pl.pallas_call(
        body, grid=(l // tm,), in_specs=[row, row, pl.BlockSpec((tm, dp), lambda i: (i, 1)), vec],
        out_specs=[row, row, vec],
        out_shape=[jax.ShapeDtypeStruct((l, dp), BF), jax.ShapeDtypeStruct((l, dp), BF),
                   jax.ShapeDtypeStruct((1, dp), F32)],
        compiler_params=_params(("arbitrary",)), name="pool_gate_bwd")(dyp, mg, pu, scale)


def _group_mm_fwd(mp, wg, pu, scale, tm=1024, tn=1024, tk=1024):
    l, dp = mp.shape
    ng, gd = wg.shape[0], wg.shape[1]
    tm, tn, tk = _t(l, tm), _t(gd, tn), _t(gd, tk)

    def epi(acc, gate, sc):
        return acc, acc * sc * _silu(gate)

    out = pl.BlockSpec((tm, tn), lambda g, i, j, k: (i, g * (gd // tn) + j))
    return _mm("group_mm_fwd", (ng, l // tm, gd // tn, gd // tk),
               mp, pl.BlockSpec((tm, tk), lambda g, i, j, k: (i, g * (gd // tk) + k)),
               wg, pl.BlockSpec((None, tk, tn), lambda g, i, j, k: (g, k, j)), NN,
               [jax.ShapeDtypeStruct((l, dp), F32), jax.ShapeDtypeStruct((l, dp), BF)], [out, out], (tm, tn),
               (pu, scale),
               (pl.BlockSpec((tm, tn), lambda g, i, j, k: (i, (dp + g * gd) // tn + j)),
                pl.BlockSpec((1, tn), lambda g, i, j, k: (0, g * (gd // tn) + j))), epi)


def _group_mm_bwd_data(dmg, wg, tm=1024, tn=1024, tk=1024):
    l, dp = dmg.shape
    ng, gd = wg.shape[0], wg.shape[1]
    tm, tn, tk = _t(l, tm), _t(gd, tn), _t(gd, tk)
    return _mm("group_mm_bwd_data", (ng, l // tm, gd // tn, gd // tk),
               dmg, pl.BlockSpec((tm, tk), lambda g, i, j, k: (i, g * (gd // tk) + k)),
               wg, pl.BlockSpec((None, tn, tk), lambda g, i, j, k: (g, j, k)), NT,
               [jax.ShapeDtypeStruct((l, dp), F32)],
               [pl.BlockSpec((tm, tn), lambda g, i, j, k: (i, g * (gd // tn) + j))], (tm, tn))[0]


def _group_mm_bwd_weight(mp, dmg, ng, tm=512, tk=1024):
    l, dp = mp.shape
    gd = dp // ng
    tm, tk = _t(gd, tm), _t(l, tk)
    return _mm("group_mm_bwd_weight", (ng, gd // tm, l // tk),
               mp, pl.BlockSpec((tk, tm), lambda g, i, k: (k, g * (gd // tm) + i)),
               dmg, pl.BlockSpec((tk, gd), lambda g, i, k: (k, g)), TN,
               [jax.ShapeDtypeStruct((ng, gd, gd // 2), F32)],
               [pl.BlockSpec((None, tm, gd // 2), lambda g, i, k: (g, i, 0))], (tm, gd), epi=_pack_epi)[0]


def _cast_bf16(name, w, tr=256):
    r, c = w.shape
    tr = _row_tile(r, tr, 16)

    def body(w_ref, o_ref):
        o_ref[...] = w_ref[...].astype(BF)

    blk = pl.BlockSpec((tr, c), lambda i: (i, 0))
    return pl.pallas_call(body, grid=(r // tr,), in_specs=[blk], out_specs=blk,
                          out_shape=jax.ShapeDtypeStruct((r, c), BF), compiler_params=_params(("parallel",)),
                          name=name)(w)


def _pack_rows(name, w, tr=256):
    r, c = w.shape
    tr = _row_tile(r, tr)

    def body(w_ref, o_ref):
        o_ref[...] = _pack_pairs(w_ref[...])

    return pl.pallas_call(body, grid=(r // tr,), in_specs=[pl.BlockSpec((tr, c), lambda i: (i, 0))],
                          out_specs=pl.BlockSpec((tr, c // 2), lambda i: (i, 0)),
                          out_shape=jax.ShapeDtypeStruct((r, c // 2), F32), compiler_params=_params(("parallel",)),
                          name=name)(w)


def _unpack_rows(name, w, tr=512):
    r, h = w.shape
    tr = _row_tile(r, tr, 16)

    def body(w_ref, o_ref):
        hi, lo = _unpack_pairs(w_ref[...])
        o_ref[:, :h] = hi.astype(BF)
        o_ref[:, h:] = lo.astype(BF)

    return pl.pallas_call(body, grid=(r // tr,), in_specs=[pl.BlockSpec((tr, h), lambda i: (i, 0))],
                          out_specs=pl.BlockSpec((tr, 2 * h), lambda i: (i, 0)),
                          out_shape=jax.ShapeDtypeStruct((r, 2 * h), BF), compiler_params=_params(("parallel",)),
                          name=name)(w)


def _reduce_packed(name, recv, tr=256):
    nd, r, h = recv.shape
    tr = _row_tile(r, tr)

    def body(p_ref, o_ref):
        hi, lo = _unpack_pairs(p_ref[0])
        for k in range(1, nd):
            a, b = _unpack_pairs(p_ref[k])
            hi, lo = hi + a, lo + b
        o_ref[:, :h] = hi
        o_ref[:, h:] = lo

    return pl.pallas_call(body, grid=(r // tr,), in_specs=[pl.BlockSpec((nd, tr, h), lambda i: (0, i, 0))],
                          out_specs=pl.BlockSpec((tr, 2 * h), lambda i: (i, 0)),
                          out_shape=jax.ShapeDtypeStruct((r, 2 * h), F32), compiler_params=_params(("parallel",)),
                          name=name)(recv)


def _adamw_math(w, g, m, v):
    m2 = ADAM_B1 * m + (1.0 - ADAM_B1) * g
    v2 = ADAM_B2 * v + (1.0 - ADAM_B2) * (g * g)
    m_hat = m2 / (1.0 - ADAM_B1 ** ADAM_STEP)
    v_hat = v2 / (1.0 - ADAM_B2 ** ADAM_STEP)
    delta = -ADAM_LR * (m_hat / (jnp.sqrt(v_hat) + ADAM_EPS) + ADAM_WD * w)
    return delta, m2, v2


def _adamw(name, w, g, m, v, tr=256):
    r, c = w.shape
    tr = _row_tile(r, tr)

    def body(w_ref, g_ref, m_ref, v_ref, d_ref, m2_ref, v2_ref):
        d, m2, v2 = _adamw_math(w_ref[...], g_ref[...], m_ref[...], v_ref[...])
        d_ref[...] = d
        m2_ref[...] = m2
        v2_ref[...] = v2

    blk = pl.BlockSpec((tr, c), lambda i: (i, 0))
    shp = jax.ShapeDtypeStruct((r, c), F32)
    return pl.pallas_call(body, grid=(r // tr,), in_specs=[blk] * 4, out_specs=[blk] * 3, out_shape=[shp] * 3,
                          compiler_params=_params(("parallel",)), name=name)(w, g, m, v)


def _sum_slots(name, a):
    nd, r, c = a.shape

    def body(a_ref, o_ref):
        s = a_ref[0]
        for k in range(1, nd):
            s = s + a_ref[k]
        o_ref[...] = s

    return pl.pallas_call(body, out_shape=jax.ShapeDtypeStruct((r, c), F32), name=name)(a)


def _head_rows(v, g_n, r):
    return jnp.pad(v.reshape(g_n, 1, r), ((0, 0), (0, 0), (0, HEAD_LANES - r)))


class _Later:
    def __init__(self, gather_pool_in, pool_in_of, gather_rest, rest_of, scatter_of):
        self.gather_pool_in, self.pool_in_of = gather_pool_in, pool_in_of
        self.gather_rest, self.rest_of, self.scatter_of = gather_rest, rest_of, scatter_of


def _local_step(x, target, ln_g, final_g, wt_in, conv_w, conv_b, dt_bias, a_log, d_skip, norm_g, scale, later):
    l, d = x.shape
    di = norm_g.shape[1]
    h = dt_bias.shape[1]
    p = di // h
    conv = conv_b.shape[1]
    n, t = SSM_STATE, SSD_CHUNK
    g_n = (conv - di) // (2 * n)
    r = h // g_n
    nm = di + conv
    dp = scale.shape[1]
    ng = len(POOL_WINDOWS)
    dims = (h, g_n, r, p, n, t)
    pw = _t(d, 1024)

    h0, r0 = _rms_fwd("rms0_fwd", x, ln_g[0:1])
    pm = _mm_nt("in_proj_main", h0, wt_in, 0, nm, exch=later.gather_pool_in)
    if later.gather_pool_in is not None:
        pm, *arrived = pm
    else:
        arrived = None
    wp_in = later.pool_in_of(arrived)
    dtr = _mm_nt("in_proj_dt", h0, wt_in, nm, h, tn=h)
    xbc = _conv_fwd(pm, di, conv_w, conv_b)
    dtb, alog = _head_rows(dt_bias, g_n, r), _head_rows(a_log, g_n, r)
    dskx = jnp.repeat(d_skip.reshape(g_n, 1, r), p, axis=2)
    dtr_g = jnp.pad(jnp.transpose(dtr.reshape(l, g_n, r), (1, 0, 2)), ((0, 0), (0, 0), (0, HEAD_LANES - r)))
    y, states, *arrived = _ssd_fwd(xbc, dtr_g, dtb, alog, dskx, dims, exch=later.gather_rest)
    w_out, wg, wp_out = later.rest_of(arrived)
    yn, r_n = _gate_norm_fwd(y, pm, norm_g)
    x1 = _mm_nn("ssm_out_proj", yn, w_out, add=x)

    h1, r1 = _rms_fwd("rms1_fwd", x1, ln_g[1:2])
    pu = _mm_nn_blocked("pool_in_proj", h1, wp_in)
    mp = _pool_mix("pool_mix_fwd", pu, dp, False, BF)
    mg, yp = _group_mm_fwd(mp, wg, pu, scale)
    x2 = _mm_nn("pool_out_proj", yp, wp_out, add=x1)

    dx2, d_final_g, loss = _loss_head(x2, final_g, target)

    dyp = _mm_nt("pool_out_bwd_data", dx2, wp_out, 0, dp, tn=1024)
    gw_pout = _mm_tn_packed("pool_out_bwd_weight", yp, dx2, pw)
    dmg, dgate, d_scale = _pool_gate_bwd(dyp, mg, pu, scale)
    dmp = _group_mm_bwd_data(dmg, wg)
    gw_g = _group_mm_bwd_weight(mp, dmg, ng)
    du = _pool_mix("pool_mix_bwd", dmp, dp, True, BF)
    dh1 = _mm_nt_blocked("pool_in_bwd_data_u", du, wp_in, 0)
    dh1 = _mm_nt_blocked("pool_in_bwd_data_gate", dgate, wp_in, N_DEV // 2, add=dh1)
    pw_in = 2 * dp // N_DEV
    gw_pin = jnp.concatenate([_mm_tn_packed("pool_in_bwd_weight_u", h1, du, pw_in),
                              _mm_tn_packed("pool_in_bwd_weight_gate", h1, dgate, pw_in)], axis=0)
    dx1, d_ln1 = _rms_bwd("rms1_bwd", dh1, x1, r1, ln_g[1:2], dx2)

    dyn = _mm_nt("ssm_out_bwd_data", dx1, w_out, 0, di, tn=1024)
    gw_out = _mm_tn_packed("ssm_out_bwd_weight", yn, dx1, pw)
    dy, dz, d_norm_g = _gate_norm_bwd(dyn, y, pm, r_n, norm_g)
    big = dict(w_out=gw_out, wp_in=gw_pin, wg=gw_g, wp_out=gw_pout)
    dxbc, ddt_g, dbias_g, dalog_g, dd_g, *received = _ssd_bwd(
        xbc, dtr_g, dtb, alog, dskx, states, dy, dims, exch=later.scatter_of(big))
    dxr, d_conv_w, d_conv_b = _conv_bwd(pm, di, conv_w, conv_b, dxbc)
    ddt = jnp.transpose(ddt_g[:, :, :r], (1, 0, 2)).reshape(l, h)
    dh0 = _mm_nn("in_proj_bwd_data_z", dz, wt_in, 0)
    dh0 = _mm_nn("in_proj_bwd_data_xbc", dxr, wt_in, di, add=dh0)
    dh0 = _mm_nn("in_proj_bwd_data_dt", ddt, wt_in, nm, add=dh0)
    gwt_in = jnp.concatenate([_mm_tn_packed("in_proj_bwd_weight_z", dz, h0, pw),
                              _mm_tn_packed("in_proj_bwd_weight_xbc", dxr, h0, pw),
                              _mm_tn_packed("in_proj_bwd_weight_dt", ddt, h0, pw)], axis=1)
    grad_x, d_ln0 = _rms_bwd("rms0_bwd", dh0, x, r0, ln_g[0:1], dx1)

    def heads(v):
        return v[:, 0, :r].reshape(1, h)

    small = dict(ln_g=jnp.concatenate([d_ln0, d_ln1], axis=0), final_g=d_final_g, conv_w=d_conv_w, conv_b=d_conv_b,
                 dt_bias=heads(dbias_g), a_log=heads(dalog_g), d_skip=heads(dd_g), norm_g=d_norm_g, scale=d_scale)
    big["wt_in"] = gwt_in
    return loss, grad_x, small, big, received


SMALL_ORDER = ("ln_g", "final_g", "conv_w", "conv_b", "dt_bias", "a_log", "d_skip", "norm_g", "scale", "loss")


def _flatten_small(parts):
    flat = jnp.concatenate([parts[k].reshape(-1) for k in SMALL_ORDER])
    n = flat.shape[0]
    rows = -(-n // 1024) * 8
    return jnp.pad(flat, (0, rows * 128 - n)).reshape(rows, 128)


def _split_small(flat, shapes):
    flat = flat.reshape(-1)
    out, off = {}, 0
    for k in SMALL_ORDER:
        size = int(np.prod(shapes[k]))
        out[k] = flat[off:off + size].reshape(shapes[k])
        off += size
    return out


def kernel(x, ln_g, final_g, ssm_w_in, ssm_conv_w, ssm_conv_b, ssm_dt_bias, ssm_a_log, ssm_d, ssm_norm_g, ssm_w_out, pool_w_in, pool_w_group, pool_scale, pool_w_out, loss_target, m_ln_g, m_final_g, m_ssm_w_in, m_ssm_conv_w, m_ssm_conv_b, m_ssm_dt_bias, m_ssm_a_log, m_ssm_d, m_ssm_norm_g, m_ssm_w_out, m_pool_w_in, m_pool_w_group, m_pool_scale, m_pool_w_out, v_ln_g, v_final_g, v_ssm_w_in, v_ssm_conv_w, v_ssm_conv_b, v_ssm_dt_bias, v_ssm_a_log, v_ssm_d, v_ssm_norm_g, v_ssm_w_out, v_pool_w_in, v_pool_w_group, v_pool_scale, v_pool_w_out):
    l, d = x.shape[1], x.shape[2]
    me = 4 * lax.axis_index("x") + 2 * lax.axis_index("y") + lax.axis_index("c")
    ng, gds, gd = pool_w_group.shape[1], pool_w_group.shape[2], pool_w_group.shape[3]
    sin_s = ssm_w_in.shape[2]
    dp_s = pool_w_out.shape[1]
    conv_s = ssm_conv_w.shape[2]

    wt_in_s = _pack_rows("pack_w_in", jnp.transpose(ssm_w_in[0]))
    w_out_s = _cast_bf16("cast_w_out", ssm_w_out[0])
    wp_in_s = _cast_bf16("cast_pool_w_in", pool_w_in[0])
    wg_s = _cast_bf16("cast_pool_w_group", pool_w_group[0].reshape(ng * gds, gd))
    wp_out_s = _cast_bf16("cast_pool_w_out", pool_w_out[0])
    small_s = jnp.concatenate([ssm_conv_w[0].reshape(-1), pool_scale[0]])
    n_small = small_s.shape[0]
    small_s = jnp.pad(small_s, (0, -(-n_small // 1024) * 1024 - n_small)).reshape(-1, 128)
    wt_in_g, small_g = _exchange_alone("all_gather_w_in", _Gather([wt_in_s, small_s]))
    wt_in = _unpack_rows("unpack_w_in", wt_in_g.reshape(N_DEV * sin_s, d // 2))
    small_all = small_g.reshape(N_DEV, -1)[:, :n_small]
    conv_w = jnp.transpose(small_all[:, :CONV_TAPS * conv_s].reshape(N_DEV, CONV_TAPS, conv_s), (1, 0, 2))
    conv_w = conv_w.reshape(CONV_TAPS, -1)
    scale = small_all[:, CONV_TAPS * conv_s:].reshape(1, -1)

    def rest_of(arrived):
        w_out_g, wg_g, wp_out_g = arrived
        wg = jnp.transpose(wg_g.reshape(N_DEV, ng, gds, gd), (1, 0, 2, 3)).reshape(ng, gd, gd)
        return w_out_g.reshape(-1, d), wg, wp_out_g.reshape(-1, d)

    def rows_major(gp):
        q, _, hw = gp.shape
        return jnp.transpose(gp.reshape(q, N_DEV, -1, hw), (1, 0, 2, 3))

    def scatter_of(big):
        return _Scatter([rows_major(big["w_out"]), big["wp_in"][:, None],
                         jnp.transpose(big["wg"].reshape(ng, N_DEV, gds, gd // 2), (1, 0, 2, 3)),
                         rows_major(big["wp_out"])])

    later = _Later(_Gather([wp_in_s]), lambda arrived: arrived[0], _Gather([w_out_s, wg_s, wp_out_s]), rest_of,
                   scatter_of)

    loss, grad_x, small, big, recv = _local_step(
        x[0], loss_target[0], ln_g, final_g.reshape(1, d), wt_in, conv_w, ssm_conv_b, ssm_dt_bias, ssm_a_log, ssm_d,
        ssm_norm_g, scale, later)
    recv_w_in, = _exchange_alone("exchange_w_in_grad", _Scatter([rows_major(big["wt_in"])]))

    def reduced(name, rv):
        q = rv.shape[1]
        cols = [_reduce_packed(f"{name}_{j}", rv[:, j]) for j in range(q)]
        return cols[0] if q == 1 else jnp.concatenate(cols, axis=1)

    g_w_in = jnp.transpose(reduced("reduce_w_in", recv_w_in))[None]
    g_w_out = reduced("reduce_w_out", recv[0])[None]
    g_wp_in = reduced("reduce_pool_w_in", recv[1])[None]
    g_wg = _reduce_packed("reduce_pool_w_group", recv[2].reshape(N_DEV, ng * gds, gd // 2)).reshape(1, ng, gds, gd)
    g_wp_out = reduced("reduce_pool_w_out", recv[3])[None]

    small["loss"] = loss
    shapes = {k: small[k].shape for k in SMALL_ORDER}
    gathered_small, = _exchange_alone("all_gather_small_grads", _Gather([_flatten_small(small)]))
    summed = _split_small(_sum_slots("sum_small_grads", gathered_small), shapes)
    g_conv_w = lax.dynamic_slice_in_dim(summed["conv_w"], me * conv_s, conv_s, axis=1)[None]
    g_scale = lax.dynamic_slice_in_dim(summed["scale"], me * dp_s, dp_s, axis=1)
    grads = dict(ln_g=summed["ln_g"], final_g=summed["final_g"].reshape(d), ssm_w_in=g_w_in, ssm_conv_w=g_conv_w,
                 ssm_conv_b=summed["conv_b"], ssm_dt_bias=summed["dt_bias"], ssm_a_log=summed["a_log"],
                 ssm_d=summed["d_skip"], ssm_norm_g=summed["norm_g"], ssm_w_out=g_w_out, pool_w_in=g_wp_in,
                 pool_w_group=g_wg, pool_scale=g_scale, pool_w_out=g_wp_out)

    weights = dict(ln_g=ln_g, final_g=final_g, ssm_w_in=ssm_w_in, ssm_conv_w=ssm_conv_w, ssm_conv_b=ssm_conv_b,
                   ssm_dt_bias=ssm_dt_bias, ssm_a_log=ssm_a_log, ssm_d=ssm_d, ssm_norm_g=ssm_norm_g,
                   ssm_w_out=ssm_w_out, pool_w_in=pool_w_in, pool_w_group=pool_w_group, pool_scale=pool_scale,
                   pool_w_out=pool_w_out)
    m_in = dict(ln_g=m_ln_g, final_g=m_final_g, ssm_w_in=m_ssm_w_in, ssm_conv_w=m_ssm_conv_w, ssm_conv_b=m_ssm_conv_b,
                ssm_dt_bias=m_ssm_dt_bias, ssm_a_log=m_ssm_a_log, ssm_d=m_ssm_d, ssm_norm_g=m_ssm_norm_g,
                ssm_w_out=m_ssm_w_out, pool_w_in=m_pool_w_in, pool_w_group=m_pool_w_group, pool_scale=m_pool_scale,
                pool_w_out=m_pool_w_out)
    v_in = dict(ln_g=v_ln_g, final_g=v_final_g, ssm_w_in=v_ssm_w_in, ssm_conv_w=v_ssm_conv_w, ssm_conv_b=v_ssm_conv_b,
                ssm_dt_bias=v_ssm_dt_bias, ssm_a_log=v_ssm_a_log, ssm_d=v_ssm_d, ssm_norm_g=v_ssm_norm_g,
                ssm_w_out=v_ssm_w_out, pool_w_in=v_pool_w_in, pool_w_group=v_pool_w_group, pool_scale=v_pool_scale,
                pool_w_out=v_pool_w_out)
    names = list(weights)
    big_names = ("ssm_w_in", "ssm_w_out", "pool_w_in", "pool_w_group", "pool_w_out")
    delta, new_m, new_v = {}, {}, {}
    for k in big_names:
        shp = weights[k].shape
        two_d = (-1, shp[-1])
        dk, mk, vk = _adamw(f"adamw_{k}", weights[k].reshape(two_d), grads[k].reshape(two_d), m_in[k].reshape(two_d),
                            v_in[k].reshape(two_d))
        delta[k], new_m[k], new_v[k] = dk.reshape(shp), mk.reshape(shp), vk.reshape(shp)
    small_names = [k for k in names if k not in big_names]

    def packed(tree):
        flat = jnp.concatenate([tree[k].reshape(-1) for k in small_names])
        nn = flat.shape[0]
        return jnp.pad(flat, (0, -(-nn // 1024) * 1024 - nn), constant_values=1.0).reshape(-1, 128)

    ds, ms, vs = _adamw("adamw_small", packed(weights), packed(grads), packed(m_in), packed(v_in))
    off = 0
    for k in small_names:
        shp = weights[k].shape
        size = int(np.prod(shp))
        for res, arr in ((delta, ds), (new_m, ms), (new_v, vs)):
            res[k] = arr.reshape(-1)[off:off + size].reshape(shp)
        off += size

    return (summed["loss"].reshape(()), grad_x[None], *[grads[k] for k in names], *[delta[k] for k in names],
            *[new_m[k] for k in names], *[new_v[k] for k in names])
```

```python
import math

import jax
import jax.numpy as jnp
import numpy as np
from jax import lax
from jax.experimental import pallas as pl
from jax.experimental.pallas import tpu as pltpu

F32 = jnp.float32
BF = jnp.bfloat16
U32 = jnp.uint32

N_DEV = 8
EPS = 1e-6
SSD_CHUNK = 64
SSM_STATE = 128
CONV_TAPS = 4
HEAD_LANES = 128
POOL_WINDOWS = (2, 4, 8, 16)
ADAM_LR, ADAM_B1, ADAM_B2, ADAM_EPS, ADAM_WD, ADAM_STEP = 0.001, 0.9, 0.999, 1e-08, 0.01, 10
VMEM_LIMIT = 56 * 1024 * 1024
NEG_BIG = -1e30
HAND_ON_AT = 6

NN = ((1,), (0,))
NT = ((1,), (1,))
TN = ((0,), (0,))
MESH = pl.DeviceIdType.MESH


def _t(dim, pref):
    return pref if dim % pref == 0 else dim


def _row_tile(rows, pref, mult=8):
    best = rows
    for cand in range(mult, min(rows, pref) + 1, mult):
        if rows % cand == 0:
            best = cand
    return best


def _params(sem=None):
    return pltpu.CompilerParams(dimension_semantics=sem, vmem_limit_bytes=VMEM_LIMIT)


def _silu(x):
    return x * (1.0 / (1.0 + jnp.exp(-x)))


def _dsilu(x):
    s = 1.0 / (1.0 + jnp.exp(-x))
    return s * (1.0 + x * (1.0 - s))


def _bdot(a, b, dims=NN):
    return lax.dot_general(a.astype(BF), b.astype(BF), (dims, ((), ())), preferred_element_type=F32)


def _pack_pairs(x):
    h = x.shape[1] // 2
    hi = lax.bitcast_convert_type(x[:, :h].astype(jnp.bfloat16).astype(F32), U32)
    lo = lax.bitcast_convert_type(x[:, h:].astype(jnp.bfloat16).astype(F32), U32)
    return lax.bitcast_convert_type(hi | (lo >> 16), F32)


def _unpack_pairs(w):
    u = lax.bitcast_convert_type(w, U32)
    hi = lax.bitcast_convert_type(u & jnp.uint32(0xFFFF0000), F32)
    lo = lax.bitcast_convert_type(u << 16, F32)
    return hi, lo


def _mesh_pos():
    return lax.axis_index("x"), lax.axis_index("y"), lax.axis_index("c")


def _slot(pos):
    return 4 * pos[0] + 2 * pos[1] + pos[2]


class _Gather:
    def __init__(self, arrays):
        self.arrays = list(arrays)
        self.out_shapes = [jax.ShapeDtypeStruct((N_DEV, *s.shape), s.dtype) for s in arrays]

    def phases(self, src, dst, send_sems, recv_sems, local_sems):
        n_arr = len(self.arrays)
        x, y, c = _mesh_pos()
        me, sibling = (x, y, c), (x, y, 1 - c)
        chips = [(1 - x, y), (x, 1 - y), (1 - x, 1 - y)]

        def copy(a, k, block, to, from_src):
            return pltpu.make_async_remote_copy(
                src_ref=src[a] if from_src else dst[a].at[_slot(block)], dst_ref=dst[a].at[_slot(block)],
                send_sem=send_sems.at[a * 7 + k], recv_sem=recv_sems.at[a * 7 + k], device_id=to, device_id_type=MESH)

        def mine(a):
            return pltpu.make_async_copy(src[a], dst[a].at[_slot(me)], local_sems.at[a])

        def first(a):
            return [copy(a, 0, me, sibling, True)] + [copy(a, 1 + j, me, (*chip, c), True)
                                                     for j, chip in enumerate(chips)]

        def start():
            for a in range(n_arr):
                mine(a).start()
                for cp in first(a):
                    cp.start()

        def middle():
            for j, chip in enumerate(chips):
                for a in range(n_arr):
                    copy(a, 1 + j, (*chip, c), me, False).wait_recv()
                    copy(a, 4 + j, (*chip, c), sibling, False).start()

        def finish():
            for a in range(n_arr):
                copy(a, 0, sibling, me, False).wait_recv()
                for j, chip in enumerate(chips):
                    copy(a, 4 + j, (*chip, 1 - c), me, False).wait_recv()
                for cp in first(a):
                    cp.wait_send()
                for j, chip in enumerate(chips):
                    copy(a, 4 + j, (*chip, c), sibling, False).wait_send()
                mine(a).wait()

        return start, middle, finish


class _Scatter:
    def __init__(self, arrays):
        self.arrays = list(arrays)
        self.out_shapes = [jax.ShapeDtypeStruct(b.shape, b.dtype) for b in arrays]

    def phases(self, src, dst, send_sems, recv_sems, local_sems):
        n_arr = len(self.arrays)
        x, y, c = _mesh_pos()
        me = (x, y, c)
        peers = []
        for k in range(1, N_DEV):
            fx, fy, fc = (k >> 2) & 1, (k >> 1) & 1, k & 1
            peers.append((1 - x if fx else x, 1 - y if fy else y, 1 - c if fc else c))

        def copy(a, k, peer, sending):
            return pltpu.make_async_remote_copy(
                src_ref=src[a].at[_slot(peer)], dst_ref=dst[a].at[_slot(me) if sending else _slot(peer)],
                send_sem=send_sems.at[a * 7 + k], recv_sem=recv_sems.at[a * 7 + k], device_id=peer,
                device_id_type=MESH)

        def mine(a):
            return pltpu.make_async_copy(src[a].at[_slot(me)], dst[a].at[_slot(me)], local_sems.at[a])

        def start():
            for a in range(n_arr):
                mine(a).start()
                for k, peer in enumerate(peers):
                    copy(a, k, peer, True).start()

        def finish():
            for a in range(n_arr):
                for k, peer in enumerate(peers):
                    copy(a, k, peer, False).wait_recv()
                for k, peer in enumerate(peers):
                    copy(a, k, peer, True).wait_send()
                mine(a).wait()

        return start, None, finish


def _hosted_call(name, body, grid, in_specs, out_specs, out_shape, scratch_shapes, sem, operands, exch=None):
    if exch is None:
        return pl.pallas_call(body, grid=grid, in_specs=in_specs, out_specs=out_specs, out_shape=out_shape,
                              scratch_shapes=scratch_shapes, compiler_params=_params(sem), name=name)(*operands)
    n_in, n_out, n_scr, ne = len(in_specs), len(out_specs), len(scratch_shapes), len(exch.arrays)
    total = math.prod(grid)

    def wrapped(*refs):
        ins, ex_in = refs[:n_in], refs[n_in:n_in + ne]
        outs = refs[n_in + ne:n_in + ne + n_out]
        ex_out = refs[n_in + ne + n_out:n_in + 2 * ne + n_out]
        scr = refs[n_in + 2 * ne + n_out:n_in + 2 * ne + n_out + n_scr]
        step = 0
        for axis, size in enumerate(grid):
            step = step * size + pl.program_id(axis)
        start, middle, finish = exch.phases(ex_in, ex_out, *refs[-3:])
        pl.when(step == 0)(start)
        if middle is not None:
            pl.when(step == (total * HAND_ON_AT) // 8)(middle)
        body(*ins, *outs, *scr)
        pl.when(step == total - 1)(finish)

    hbm = pl.BlockSpec(memory_space=pl.ANY)
    sems = [pltpu.SemaphoreType.DMA((ne * 7,)), pltpu.SemaphoreType.DMA((ne * 7,)), pltpu.SemaphoreType.DMA((ne,))]
    return pl.pallas_call(
        wrapped, grid=grid, in_specs=[*in_specs, *[hbm] * ne], out_specs=[*out_specs, *[hbm] * ne],
        out_shape=[*out_shape, *exch.out_shapes], scratch_shapes=[*scratch_shapes, *sems],
        compiler_params=pltpu.CompilerParams(dimension_semantics=("arbitrary",) * len(grid),
                                             vmem_limit_bytes=VMEM_LIMIT, has_side_effects=True),
        name=name)(*operands, *exch.arrays)


def _exchange_alone(name, exch):
    def body():
        pass

    return _hosted_call(name, body, (1,), [], [], [], [], None, (), exch)


def _mm(name, grid, a, a_spec, b, b_spec, dims, outs, o_specs, acc_shape, extra=(), extra_specs=(), epi=None,
        exch=None):
    nk = grid[-1]
    n_extra, n_out = len(extra), len(outs)

    def body(*refs):
        a_ref, b_ref = refs[0], refs[1]
        ex = refs[2:2 + n_extra]
        o_refs = refs[2 + n_extra:2 + n_extra + n_out]

        def write(res):
            res = (res,) if epi is None else epi(res, *[e[...] for e in ex])
            for o, r in zip(o_refs, res):
                o[...] = r.astype(o.dtype)

        if nk == 1:
            write(_bdot(a_ref[...], b_ref[...], dims))
            return
        acc = refs[-1]
        k = pl.program_id(len(grid) - 1)

        @pl.when(k == 0)
        def _():
            acc[...] = _bdot(a_ref[...], b_ref[...], dims)

        @pl.when(jnp.logical_and(k > 0, k < nk - 1))
        def _():
            acc[...] += _bdot(a_ref[...], b_ref[...], dims)

        @pl.when(k == nk - 1)
        def _():
            write(acc[...] + _bdot(a_ref[...], b_ref[...], dims))

    sem = ("parallel",) * (len(grid) - 1) + ("arbitrary",)
    scratch = [] if nk == 1 else [pltpu.VMEM(acc_shape, F32)]
    return _hosted_call(name, body, grid, [a_spec, b_spec, *extra_specs], list(o_specs), list(outs),
                        scratch, sem, (a, b, *extra), exch)


def _add_epi(acc, add):
    return (acc + add,)


def _pack_epi(acc):
    return (_pack_pairs(acc),)


def _mm_nt(name, a, b, n_off, n, out_dtype=F32, tm=1024, tn=512, exch=None):
    m, kk = a.shape
    tm, tn = _t(m, tm), math.gcd(_t(n, tn), n_off)
    res = _mm(name, (m // tm, n // tn, 1),
              a, pl.BlockSpec((tm, kk), lambda i, j, k: (i, 0)),
              b, pl.BlockSpec((tn, kk), lambda i, j, k: (n_off // tn + j, 0)), NT,
              [jax.ShapeDtypeStruct((m, n), out_dtype)], [pl.BlockSpec((tm, tn), lambda i, j, k: (i, j))], (tm, tn),
              exch=exch)
    return res[0] if exch is None else res


def _mm_nn_blocked(name, a, b3, tm=1024, tn=512):
    m, kk = a.shape
    nb, _, cs = b3.shape
    tm, tn = _t(m, tm), _t(cs, tn)
    per = cs // tn
    return _mm(name, (m // tm, nb * per, 1),
               a, pl.BlockSpec((tm, kk), lambda i, j, k: (i, 0)),
               b3, pl.BlockSpec((None, kk, tn), lambda i, j, k: (j // per, 0, j % per)), NN,
               [jax.ShapeDtypeStruct((m, nb * cs), F32)], [pl.BlockSpec((tm, tn), lambda i, j, k: (i, j))],
               (tm, tn))[0]


def _mm_nt_blocked(name, a, b3, blk_off, add=None, tm=1024, tn=1024):
    m = a.shape[0]
    _, n, cs = b3.shape
    nk = a.shape[1] // cs
    tm, tn = _t(m, tm), _t(n, tn)
    extra, especs, epi = (), (), None
    if add is not None:
        extra, especs, epi = (add,), (pl.BlockSpec((tm, tn), lambda i, j, k: (i, j)),), _add_epi
    return _mm(name, (m // tm, n // tn, nk),
               a, pl.BlockSpec((tm, cs), lambda i, j, k: (i, k)),
               b3, pl.BlockSpec((None, tn, cs), lambda i, j, k: (blk_off + k, j, 0)), NT,
               [jax.ShapeDtypeStruct((m, n), F32)], [pl.BlockSpec((tm, tn), lambda i, j, k: (i, j))], (tm, tn),
               extra, especs, epi)[0]


def _mm_nn(name, a, b, k_off=0, kk=None, add=None, out_dtype=F32, tm=1024, tn=1024, tk=1024):
    m = a.shape[0]
    kk = a.shape[1] if kk is None else kk
    n = b.shape[1]
    tm, tn, tk = _t(m, tm), _t(n, tn), math.gcd(_t(kk, tk), k_off)
    extra, especs, epi = (), (), None
    if add is not None:
        extra, especs, epi = (add,), (pl.BlockSpec((tm, tn), lambda i, j, k: (i, j)),), _add_epi
    return _mm(name, (m // tm, n // tn, kk // tk),
               a, pl.BlockSpec((tm, tk), lambda i, j, k: (i, k)),
               b, pl.BlockSpec((tk, tn), lambda i, j, k: (k_off // tk + k, j)), NN,
               [jax.ShapeDtypeStruct((m, n), out_dtype)], [pl.BlockSpec((tm, tn), lambda i, j, k: (i, j))], (tm, tn),
               extra, especs, epi)[0]


TN_ACC_ELEMENTS = 1 << 20


def _mm_tn_packed(name, a, b, pw, tk=1024):
    kk, m = a.shape
    n = b.shape[1]
    tm, tk = _t(m, TN_ACC_ELEMENTS // pw), _t(kk, tk)
    return _mm(name, (m // tm, n // pw, kk // tk),
               a, pl.BlockSpec((tk, tm), lambda i, j, k: (k, i)),
               b, pl.BlockSpec((tk, pw), lambda i, j, k: (k, j)), TN,
               [jax.ShapeDtypeStruct((n // pw, m, pw // 2), F32)],
               [pl.BlockSpec((None, tm, pw // 2), lambda i, j, k: (j, i, 0))], (tm, pw), epi=_pack_epi)[0]


def _rms_fwd(name, x, g, tm=256):
    l, d = x.shape
    tm = _t(l, tm)

    def body(x_ref, g_ref, h_ref, r_ref):
        xv = x_ref[...]
        r = lax.rsqrt(jnp.mean(xv * xv, axis=-1, keepdims=True) + EPS)
        h_ref[...] = (xv * r * g_ref[...]).astype(BF)
        r_ref[...] = r

    return pl.pallas_call(
        body, grid=(l // tm,),
        in_specs=[pl.BlockSpec((tm, d), lambda i: (i, 0)), pl.BlockSpec((1, d), lambda i: (0, 0))],
        out_specs=[pl.BlockSpec((tm, d), lambda i: (i, 0)), pl.BlockSpec((tm, 1), lambda i: (i, 0))],
        out_shape=[jax.ShapeDtypeStruct((l, d), BF), jax.ShapeDtypeStruct((l, 1), F32)],
        compiler_params=_params(("parallel",)), name=name)(x, g)


def _rms_bwd(name, dh, x, r, g, dres, tm=256):
    l, d = x.shape
    tm = _t(l, tm)

    def body(dh_ref, x_ref, r_ref, g_ref, dres_ref, dx_ref, dxb_ref, dg_ref):
        i = pl.program_id(0)
        rr = r_ref[...]
        xhat = x_ref[...] * rr
        dhv = dh_ref[...]
        dxh = dhv * g_ref[...]
        dx = dres_ref[...] + rr * (dxh - xhat * jnp.mean(dxh * xhat, axis=-1, keepdims=True))
        dx_ref[...] = dx
        dxb_ref[...] = dx.astype(BF)

        @pl.when(i == 0)
        def _():
            dg_ref[...] = jnp.zeros_like(dg_ref)

        dg_ref[...] += jnp.sum(dhv * xhat, axis=0, keepdims=True)

    row = pl.BlockSpec((tm, d), lambda i: (i, 0))
    vec = pl.BlockSpec((1, d), lambda i: (0, 0))
    return pl.pallas_call(
        body, grid=(l // tm,), in_specs=[row, row, pl.BlockSpec((tm, 1), lambda i: (i, 0)), vec, row],
        out_specs=[row, row, vec],
        out_shape=[jax.ShapeDtypeStruct((l, d), F32), jax.ShapeDtypeStruct((l, d), BF),
                   jax.ShapeDtypeStruct((1, d), F32)],
        compiler_params=_params(("arbitrary",)), name=name)(dh, x, r, g, dres)


def _loss_head(x, g, target, tm=256):
    l, d = x.shape
    tm = _t(l, tm)

    def body(x_ref, g_ref, t_ref, dx_ref, dxb_ref, dg_ref, loss_ref):
        i = pl.program_id(0)
        xv = x_ref[...]
        gv = g_ref[...]
        r = lax.rsqrt(jnp.mean(xv * xv, axis=-1, keepdims=True) + EPS)
        xhat = xv * r
        e = xhat * gv - t_ref[...]
        dy = e * (1.0 / d)
        dxh = dy * gv
        dx = r * (dxh - xhat * jnp.mean(dxh * xhat, axis=-1, keepdims=True))
        dx_ref[...] = dx
        dxb_ref[...] = dx.astype(BF)

        @pl.when(i == 0)
        def _():
            dg_ref[...] = jnp.zeros_like(dg_ref)
            loss_ref[...] = jnp.zeros_like(loss_ref)

        dg_ref[...] += jnp.sum(dy * xhat, axis=0, keepdims=True)
        loss_ref[...] += 0.5 * jnp.sum(jnp.sum(e * e, axis=-1, keepdims=True) * (1.0 / d), axis=0, keepdims=True)

    row = pl.BlockSpec((tm, d), lambda i: (i, 0))
    vec = pl.BlockSpec((1, d), lambda i: (0, 0))
    return pl.pallas_call(
        body, grid=(l // tm,), in_specs=[row, vec, row],
        out_specs=[row, row, vec, pl.BlockSpec((1, 1), lambda i: (0, 0))],
        out_shape=[jax.ShapeDtypeStruct((l, d), F32), jax.ShapeDtypeStruct((l, d), BF),
                   jax.ShapeDtypeStruct((1, d), F32), jax.ShapeDtypeStruct((1, 1), F32)],
        compiler_params=_params(("arbitrary",)), name="loss_head")(x, g, target)


CONV_HALO = 8


def _conv_pre(x_ref, w_ref, b_ref, i, tr):
    r0 = pl.multiple_of(i * tr, tr)
    cur = x_ref[pl.ds(r0, tr), :]
    prev = x_ref[pl.ds(pl.multiple_of(jnp.maximum(r0 - CONV_HALO, 0), CONV_HALO), CONV_HALO), :]
    prev = jnp.where(i > 0, prev, 0.0)
    ext = jnp.concatenate([prev, cur], axis=0)
    taps = []
    for k in range(CONV_TAPS):
        s = CONV_TAPS - 1 - k
        taps.append(cur if s == 0 else pltpu.roll(ext, s, 0)[CONV_HALO:])
    pre = b_ref[...] + sum(w_ref[k:k + 1, :] * taps[k] for k in range(CONV_TAPS))
    return r0, pre, taps


def _conv_fwd(pm, col_off, conv_w, conv_b, cw=256, tr=512):
    l = pm.shape[0]
    c = conv_w.shape[1]
    cw, tr = _t(c, cw), _t(l, tr)
    assert col_off % cw == 0

    def body(x_ref, w_ref, b_ref, o_ref):
        def step(i, carry):
            r0, pre, _ = _conv_pre(x_ref, w_ref, b_ref, i, tr)
            o_ref[pl.ds(r0, tr), :] = _silu(pre)
            return carry
        lax.fori_loop(0, l // tr, step, 0)

    return pl.pallas_call(
        body, grid=(c // cw,),
        in_specs=[pl.BlockSpec((l, cw), lambda j: (0, col_off // cw + j)),
                  pl.BlockSpec((CONV_TAPS, cw), lambda j: (0, j)), pl.BlockSpec((1, cw), lambda j: (0, j))],
        out_specs=pl.BlockSpec((l, cw), lambda j: (0, j)), out_shape=jax.ShapeDtypeStruct((l, c), F32),
        compiler_params=_params(("parallel",)), name="conv_fwd")(pm, conv_w, conv_b)


def _conv_bwd(pm, col_off, conv_w, conv_b, dy, dproj, cw=256, tr=512):
    l = pm.shape[0]
    c = conv_w.shape[1]
    cw, tr = _t(c, cw), _t(l, tr)
    nt = l // tr

    def body(x_ref, w_ref, b_ref, dy_ref, _, dx_ref, dw_ref, db_ref, dpre_ref):
        def step1(i, carry):
            dws, db = carry
            r0, pre, taps = _conv_pre(x_ref, w_ref, b_ref, i, tr)
            dpre = dy_ref[pl.ds(r0, tr), :] * _dsilu(pre)
            dpre_ref[pl.ds(r0, tr), :] = dpre
            dws = tuple(dws[k] + jnp.sum(dpre * taps[k], axis=0, keepdims=True) for k in range(CONV_TAPS))
            return dws, db + jnp.sum(dpre, axis=0, keepdims=True)

        z = jnp.zeros((1, cw), F32)
        dws, db = lax.fori_loop(0, nt, step1, ((z,) * CONV_TAPS, z))
        for k in range(CONV_TAPS):
            dw_ref[k:k + 1, :] = dws[k]
        db_ref[...] = db

        def step2(i, carry):
            r0 = pl.multiple_of(i * tr, tr)
            cur = dpre_ref[pl.ds(r0, tr), :]
            nxt = dpre_ref[pl.ds(pl.multiple_of(jnp.minimum(r0 + tr, l - CONV_HALO), CONV_HALO), CONV_HALO), :]
            nxt = jnp.where(i < nt - 1, nxt, 0.0)
            ext = jnp.concatenate([cur, nxt], axis=0)
            acc = w_ref[CONV_TAPS - 1:CONV_TAPS, :] * cur
            for k in range(CONV_TAPS - 1):
                s = CONV_TAPS - 1 - k
                acc = acc + w_ref[k:k + 1, :] * pltpu.roll(ext, tr + CONV_HALO - s, 0)[:tr]
            dx_ref[pl.ds(r0, tr), :] = acc.astype(dx_ref.dtype)
            return carry
        lax.fori_loop(0, nt, step2, 0)

    col = pl.BlockSpec((l, cw), lambda j: (0, j))
    shifted = pl.BlockSpec((l, cw), lambda j: (0, col_off // cw + j))
    return pl.pallas_call(
        body, grid=(c // cw,),
        in_specs=[shifted, pl.BlockSpec((CONV_TAPS, cw), lambda j: (0, j)), pl.BlockSpec((1, cw), lambda j: (0, j)),
                  col, pl.BlockSpec(memory_space=pl.ANY)],
        out_specs=[shifted, pl.BlockSpec((CONV_TAPS, cw), lambda j: (0, j)), pl.BlockSpec((1, cw), lambda j: (0, j))],
        out_shape=[jax.ShapeDtypeStruct(dproj.shape, dproj.dtype), jax.ShapeDtypeStruct((CONV_TAPS, c), F32),
                   jax.ShapeDtypeStruct((1, c), F32)],
        scratch_shapes=[pltpu.VMEM((l, cw), F32)], input_output_aliases={4: 0},
        compiler_params=_params(("parallel",)), name="conv_bwd")(pm, conv_w, conv_b, dy, dproj)


def _split(x, pieces):
    out, rest = [], x
    for _ in range(pieces):
        piece = rest.astype(BF)
        out.append(piece)
        rest = rest - piece.astype(F32)
    return out


def _rows_times(xs, m_stack, pieces):
    x = xs[0] if len(xs) == 1 else jnp.concatenate(xs, axis=0)
    out = _bdot(jnp.concatenate(_split(x, pieces), axis=1), m_stack)
    sizes = [v.shape[0] for v in xs]
    offs = np.cumsum([0] + sizes)
    return [out[offs[i]:offs[i + 1]] for i in range(len(xs))]


def _times_rows(m_stack, x, pieces):
    return _bdot(m_stack, jnp.concatenate(_split(x, pieces), axis=0))


EXPAND_PIECES = 3
FOLD_PIECES = 2


def _ssd_consts(r, p, t):
    assert p == t, "heads expand to P lanes of the inputs and to T lanes of the decay matrices alike"
    rp = r * p
    tri = np.tril(np.ones((t, t), np.float32))
    ep = np.zeros((HEAD_LANES, rp), np.float32)
    ep[np.arange(rp) // p, np.arange(rp)] = 1.0
    itile = (np.arange(t)[:, None] == (np.arange(rp) % t)[None, :]).astype(np.float32)
    lmask = (np.arange(t)[:, None] >= (np.arange(rp) % t)[None, :]).astype(np.float32)
    bmask = ((np.arange(rp) // t)[:, None] == (np.arange(rp) // p)[None, :]).astype(np.float32)
    return [jnp.asarray(np.concatenate([ep] * EXPAND_PIECES, axis=0), BF),
            jnp.asarray(np.concatenate([ep.T] * FOLD_PIECES, axis=0), BF),
            jnp.asarray(np.concatenate([tri] * EXPAND_PIECES, axis=1), BF),
            jnp.asarray(np.concatenate([tri.T] * EXPAND_PIECES, axis=1), BF),
            jnp.asarray(ep.T.copy()), jnp.asarray(itile), jnp.asarray(lmask), jnp.asarray(bmask)]


def _ssd_common(xs, bm, pre_raw, dtb, alog, e_stack, tri_stack, itile, lmask, bmask, r, t):
    pre = pre_raw + dtb
    dt = jnp.maximum(pre, 0.0) + jnp.log(1.0 + jnp.exp(-jnp.abs(pre)))
    a = -jnp.exp(alog)
    cs = _times_rows(tri_stack, dt * a, EXPAND_PIECES)
    last = cs[t - 1:t, :]
    ecs, w = jnp.exp(cs), jnp.exp(last - cs)
    dtx, ecsx, wx, csx = _rows_times([dt, ecs, w, cs], e_stack, EXPAND_PIECES)
    csrow = jnp.sum(csx * itile, axis=0, keepdims=True)
    lx = jnp.exp(jnp.where(lmask > 0.0, csx - csrow, NEG_BIG))
    xdt = xs * dtx
    xblk = (jnp.concatenate([xdt] * r, axis=0) * bmask).astype(BF)
    btile = jnp.concatenate([bm] * r, axis=0).astype(BF)
    return pre, dt, a, cs, last, ecs, w, dtx, ecsx, wx, lx, xdt, xblk, btile


def _ssd_specs(l, g_n, r, p, n, t, conv, rev):
    nc = l // t
    rp = r * p
    cidx = (lambda c: nc - 1 - c) if rev else (lambda c: c)
    row_spec = lambda width: pl.BlockSpec((t, width), lambda c: (cidx(c), 0))
    dtr_spec = pl.BlockSpec((g_n, t, HEAD_LANES), lambda c: (0, cidx(c), 0))
    par_spec = pl.BlockSpec((g_n, 1, HEAD_LANES), lambda c: (0, 0, 0))
    dskx_spec = pl.BlockSpec((g_n, 1, rp), lambda c: (0, 0, 0))
    st_spec = pl.BlockSpec((None, g_n, rp, n), lambda c: (cidx(c), 0, 0, 0))
    return nc, row_spec, dtr_spec, par_spec, dskx_spec, st_spec


def _const_specs(consts):
    return [pl.BlockSpec(a.shape, lambda c: (0, 0)) for a in consts]


def _ssd_fwd(xbc, dtr, dtb, alog, dskx, dims, exch=None):
    h, g_n, r, p, n, t = dims
    l, conv = xbc.shape
    rp, hp = r * p, h * p
    nc, row_spec, dtr_spec, par_spec, dskx_spec, st_spec = _ssd_specs(l, g_n, r, p, n, t, conv, False)
    consts = _ssd_consts(r, p, t)

    def body(x_ref, dtr_ref, dtb_ref, alog_ref, dskx_ref,
             e_ref, et_ref, tri_ref, trit_ref, ept_ref, it_ref, lm_ref, bmk_ref, y_ref, st_ref, s_ref):
        @pl.when(pl.program_id(0) == 0)
        def _():
            s_ref[...] = jnp.zeros_like(s_ref)

        for g in range(g_n):
            xs = x_ref[:, g * rp:(g + 1) * rp]
            bm = x_ref[:, hp + g * n:hp + (g + 1) * n]
            cm = x_ref[:, hp + (g_n + g) * n:hp + (g_n + g + 1) * n]
            (_, _, _, _, last, _, _, _, ecsx, wx, lx, xdt, xblk, btile) = _ssd_common(
                xs, bm, dtr_ref[g], dtb_ref[g], alog_ref[g], e_ref[...], tri_ref[...], it_ref[...], lm_ref[...],
                bmk_ref[...], r, t)
            s_in = s_ref[g]
            st_ref[g] = s_in
            cbx = _bdot(cm, btile, NT)
            yd = _bdot(cbx * lx, xblk)
            yo = ecsx * _bdot(cm, s_in, NT)
            y_ref[:, g * rp:(g + 1) * rp] = yd + yo + dskx_ref[g] * xs
            elast = jnp.sum(ept_ref[...] * jnp.exp(last), axis=1, keepdims=True)
            s_ref[g] = elast * s_in + _bdot(xdt * wx, bm, TN)

    return _hosted_call(
        "ssd_fwd", body, (nc,),
        [row_spec(conv), dtr_spec, par_spec, par_spec, dskx_spec, *_const_specs(consts)],
        [row_spec(hp), st_spec],
        [jax.ShapeDtypeStruct((l, hp), F32), jax.ShapeDtypeStruct((nc, g_n, rp, n), F32)],
        [pltpu.VMEM((g_n, rp, n), F32)], ("arbitrary",),
        (xbc, dtr, dtb, alog, dskx, *consts), exch)


def _ssd_bwd(xbc, dtr, dtb, alog, dskx, states, dy, dims, exch=None):
    h, g_n, r, p, n, t = dims
    l, conv = xbc.shape
    rp, hp = r * p, h * p
    nc, row_spec, dtr_spec, par_spec, dskx_spec, st_spec = _ssd_specs(l, g_n, r, p, n, t, conv, True)
    consts = _ssd_consts(r, p, t)

    def fold_rows(v, rows):
        return sum(v[k * rows:(k + 1) * rows, :] for k in range(r))

    def body(x_ref, dtr_ref, dtb_ref, alog_ref, dskx_ref, st_ref, dy_ref,
             e_ref, et_ref, tri_ref, trit_ref, ept_ref, it_ref, lm_ref, bmk_ref,
             dx_ref, ddt_ref, dbias_ref, dalog_ref, dd_ref, ds_ref):
        @pl.when(pl.program_id(0) == 0)
        def _():
            ds_ref[...] = jnp.zeros_like(ds_ref)
            dbias_ref[...] = jnp.zeros_like(dbias_ref)
            dalog_ref[...] = jnp.zeros_like(dalog_ref)
            dd_ref[...] = jnp.zeros_like(dd_ref)

        ept, itile, bmask = ept_ref[...], it_ref[...], bmk_ref[...]
        last_row = lax.broadcasted_iota(jnp.int32, (t, HEAD_LANES), 0) == t - 1
        for g in range(g_n):
            xs = x_ref[:, g * rp:(g + 1) * rp]
            bm = x_ref[:, hp + g * n:hp + (g + 1) * n]
            cm = x_ref[:, hp + (g_n + g) * n:hp + (g_n + g + 1) * n]
            dy = dy_ref[:, g * rp:(g + 1) * rp]
            (pre, dt, a, cs, last, ecs, w, dtx, ecsx, wx, lx, xdt, xblk, btile) = _ssd_common(
                xs, bm, dtr_ref[g], dtb_ref[g], alog_ref[g], e_ref[...], tri_ref[...], itile, lm_ref[...], bmask, r, t)
            s_in = st_ref[g]
            ds_out = ds_ref[g]
            elast_row = jnp.exp(last)
            elast = jnp.sum(ept * elast_row, axis=1, keepdims=True)

            cbx = _bdot(cm, btile, NT)
            amat = cbx * lx
            da = _bdot(dy, xblk, NT)
            dxdt = fold_rows(_bdot(amat, dy, TN) * bmask, t)
            dcbx = da * lx
            q = da * amat
            dc = _bdot(dcbx, btile)
            db = fold_rows(_bdot(dcbx, cm, TN), t)

            g0 = _bdot(cm, s_in, NT)
            dg0 = dy * ecsx
            dc = dc + _bdot(dg0, s_in)

            z = _bdot(bm, ds_out, NT)
            dxdt = dxdt + z * wx
            db = db + _bdot(xdt * wx, ds_out)
            ds_ref[g] = elast * ds_out + _bdot(dg0, cm, TN)

            skip = jnp.sum(dy * xs, axis=0, keepdims=True)
            skip = jnp.concatenate([skip, jnp.zeros((15, rp), F32)], axis=0)
            f_q, f_yo, f_w, f_dt, f_skip = _rows_times(
                [q - itile * jnp.sum(q, axis=0, keepdims=True), dy * g0, z * xdt, dxdt * xs, skip],
                et_ref[...], FOLD_PIECES)
            dd_ref[g] += f_skip[0:1]
            dww = f_w * w
            dcs = f_q + f_yo * ecs - dww
            dlast = jnp.sum(jnp.sum(ds_out * s_in, axis=1, keepdims=True) * ept, axis=0, keepdims=True) * elast_row
            dlast = dlast + jnp.sum(dww, axis=0, keepdims=True)

            dcs = dcs + jnp.where(last_row, dlast, 0.0)
            dadt = _times_rows(trit_ref[...], dcs, EXPAND_PIECES)
            dalog_ref[g] += jnp.sum(dadt * dt, axis=0, keepdims=True) * a
            dpre = (dadt * a + f_dt) * (1.0 / (1.0 + jnp.exp(-pre)))
            ddt_ref[g] = dpre
            dbias_ref[g] += jnp.sum(dpre, axis=0, keepdims=True)
            dx_ref[:, g * rp:(g + 1) * rp] = dskx_ref[g] * dy + dxdt * dtx
            dx_ref[:, hp + g * n:hp + (g + 1) * n] = db
            dx_ref[:, hp + (g_n + g) * n:hp + (g_n + g + 1) * n] = dc

    par_shape = jax.ShapeDtypeStruct((g_n, 1, HEAD_LANES), F32)
    return _hosted_call(
        "ssd_bwd", body, (nc,),
        [row_spec(conv), dtr_spec, par_spec, par_spec, dskx_spec, st_spec, row_spec(hp), *_const_specs(consts)],
        [row_spec(conv), dtr_spec, par_spec, par_spec, par_spec],
        [jax.ShapeDtypeStruct((l, conv), F32), jax.ShapeDtypeStruct((g_n, l, HEAD_LANES), F32),
         par_shape, par_shape, par_shape],
        [pltpu.VMEM((g_n, rp, n), F32)], ("arbitrary",),
        (xbc, dtr, dtb, alog, dskx, states, dy, *consts), exch)


def _gate_norm_fwd(y, pm, g, tm=128):
    l, di = y.shape
    tm = _t(l, tm)

    def body(y_ref, z_ref, g_ref, o_ref, r_ref):
        v = y_ref[...] * _silu(z_ref[...])
        rr = lax.rsqrt(jnp.mean(v * v, axis=-1, keepdims=True) + EPS)
        o_ref[...] = (v * rr * g_ref[...]).astype(BF)
        r_ref[...] = rr

    row = pl.BlockSpec((tm, di), lambda i: (i, 0))
    return pl.pallas_call(
        body, grid=(l // tm,), in_specs=[row, row, pl.BlockSpec((1, di), lambda i: (0, 0))],
        out_specs=[row, pl.BlockSpec((tm, 1), lambda i: (i, 0))],
        out_shape=[jax.ShapeDtypeStruct((l, di), BF), jax.ShapeDtypeStruct((l, 1), F32)],
        compiler_params=_params(("parallel",)), name="gate_norm_fwd")(y, pm, g)


def _gate_norm_bwd(dyn, y, pm, rstd, g, tm=128):
    l, di = y.shape
    tm = _t(l, tm)

    def body(dyn_ref, y_ref, z_ref, r_ref, g_ref, dy_ref, dz_ref, dg_ref):
        i = pl.program_id(0)
        yv, zv, rr = y_ref[...], z_ref[...], r_ref[...]
        sz = _silu(zv)
        vhat = yv * sz * rr
        dn = dyn_ref[...]
        dvh = dn * g_ref[...]
        dv = rr * (dvh - vhat * jnp.mean(dvh * vhat, axis=-1, keepdims=True))
        dy_ref[...] = dv * sz
        dz_ref[...] = (dv * yv * _dsilu(zv)).astype(BF)

        @pl.when(i == 0)
        def _():
            dg_ref[...] = jnp.zeros_like(dg_ref)

        dg_ref[...] += jnp.sum(dn * vhat, axis=0, keepdims=True)

    row = pl.BlockSpec((tm, di), lambda i: (i, 0))
    vec = pl.BlockSpec((1, di), lambda i: (0, 0))
    return pl.pallas_call(
        body, grid=(l // tm,), in_specs=[row, row, row, pl.BlockSpec((tm, 1), lambda i: (i, 0)), vec],
        out_specs=[row, row, vec],
        out_shape=[jax.ShapeDtypeStruct((l, di), F32), jax.ShapeDtypeStruct((l, pm.shape[1]), BF),
                   jax.ShapeDtypeStruct((1, di), F32)],
        compiler_params=_params(("arbitrary",)), name="gate_norm_bwd")(dyn, y, pm, rstd, g)


POOL_HALO = 16


def _pool_mix(name, src, dp, backward, out_dtype, into=None, cw=256, tr=512):
    l = src.shape[0]
    gd = dp // len(POOL_WINDOWS)
    cwl, trl = _t(gd, cw), _t(l, tr)
    nt = l // trl

    def body(x_ref, *rest):
        o_ref = rest[-1]
        gi = pl.program_id(0)
        for wi, win in enumerate(POOL_WINDOWS):
            @pl.when(gi == wi)
            def _(win=win):
                def step(i, carry):
                    r0 = pl.multiple_of(i * trl, trl)
                    cur = x_ref[pl.ds(r0, trl), :]
                    trow = r0 + lax.broadcasted_iota(jnp.int32, (trl, 1), 0)
                    cnt = jnp.minimum(trow + 1, win).astype(F32)
                    if not backward:
                        halo = x_ref[pl.ds(pl.multiple_of(jnp.maximum(r0 - POOL_HALO, 0), POOL_HALO), POOL_HALO), :]
                        halo = jnp.where(i > 0, halo, 0.0)
                        s = jnp.concatenate([halo, cur], axis=0)
                        sh = 1
                        while sh < win:
                            s = s + pltpu.roll(s, sh, 0)
                            sh *= 2
                        res = s[POOL_HALO:] / cnt - cur
                    else:
                        halo = x_ref[pl.ds(pl.multiple_of(jnp.minimum(r0 + trl, l - POOL_HALO), POOL_HALO),
                                           POOL_HALO), :]
                        hrow = r0 + trl + lax.broadcasted_iota(jnp.int32, (POOL_HALO, 1), 0)
                        hcnt = jnp.minimum(hrow + 1, win).astype(F32)
                        halo = jnp.where(i < nt - 1, halo / hcnt, 0.0)
                        s = jnp.concatenate([cur / cnt, halo], axis=0)
                        sh = 1
                        while sh < win:
                            s = s + pltpu.roll(s, trl + POOL_HALO - sh, 0)
                            sh *= 2
                        res = s[:trl] - cur
                    o_ref[pl.ds(r0, trl), :] = res.astype(o_ref.dtype)
                    return carry
                lax.fori_loop(0, nt, step, 0)

    col = pl.BlockSpec((l, cwl), lambda g, j: (0, g * (gd // cwl) + j))
    if into is None:
        return pl.pallas_call(
            body, grid=(len(POOL_WINDOWS), gd // cwl), in_specs=[col], out_specs=col,
            out_shape=jax.ShapeDtypeStruct((l, dp), out_dtype),
            compiler_params=_params(("parallel", "parallel")), name=name)(src)
    return pl.pallas_call(
        body, grid=(len(POOL_WINDOWS), gd // cwl), in_specs=[col, pl.BlockSpec(memory_space=pl.ANY)], out_specs=col,
        out_shape=jax.ShapeDtypeStruct(into.shape, into.dtype), input_output_aliases={1: 0},
        compiler_params=_params(("parallel", "parallel")), name=name)(src, into)


def _pool_gate_bwd(dyp, mg, pu, scale, tm=128):
    l, dp = mg.shape
    tm = _t(l, tm)

    def body(dyp_ref, mg_ref, gate_ref, sc_ref, dmg_ref, dgate_ref, dsc_ref):
        i = pl.program_id(0)
        d, m, gt, sc = dyp_ref[...], mg_ref[...], gate_ref[...], sc_ref[...]
        sg = _silu(gt)
        dmg_ref[...] = (d * sc * sg).astype(BF)
        dgate_ref[...] = (d * m * sc * _dsilu(gt)).astype(BF)

        @pl.when(i == 0)
        def _():
            dsc_ref[...] = jnp.zeros_like(dsc_ref)

        dsc_ref[...] += jnp.sum(d * m * sg, axis=0, keepdims=True)

    row = pl.BlockSpec((tm, dp), lambda i: (i, 0))
    right = pl.BlockSpec((tm, dp), lambda i: (i, 1))
    vec = pl.BlockSpec((1, dp), lambda i: (0, 0))
    return pl.pallas_call(
        body, grid=(l // tm,), in_specs=[row, row, right, vec], out_specs=[row, right, vec],
        out_shape=[jax.ShapeDtypeStruct((l, dp), BF), jax.ShapeDtypeStruct((l, 2 * dp), BF),
                   jax.ShapeDtypeStruct((1, dp), F32)],
        compiler_params=_params(("arbitrary",)), name="pool_gate_bwd")(dyp, mg, pu, scale)


def _group_mm_fwd(mp, wg, pu, scale, tm=1024, tn=1024, tk=1024):
    l, dp = mp.shape
    ng, gd = wg.shape[0], wg.shape[1]
    tm, tn, tk = _t(l, tm), _t(gd, tn), _t(gd, tk)

    def epi(acc, gate, sc):
        return acc, acc * sc * _silu(gate)

    out = pl.BlockSpec((tm, tn), lambda g, i, j, k: (i, g * (gd // tn) + j))
    return _mm("group_mm_fwd", (ng, l // tm, gd // tn, gd // tk),
               mp, pl.BlockSpec((tm, tk), lambda g, i, j, k: (i, g * (gd // tk) + k)),
               wg, pl.BlockSpec((None, tk, tn), lambda g, i, j, k: (g, k, j)), NN,
               [jax.ShapeDtypeStruct((l, dp), F32), jax.ShapeDtypeStruct((l, dp), BF)], [out, out], (tm, tn),
               (pu, scale),
               (pl.BlockSpec((tm, tn), lambda g, i, j, k: (i, (dp + g * gd) // tn + j)),
                pl.BlockSpec((1, tn), lambda g, i, j, k: (0, g * (gd // tn) + j))), epi)


def _group_mm_bwd_data(dmg, wg, tm=1024, tn=1024, tk=1024):
    l, dp = dmg.shape
    ng, gd = wg.shape[0], wg.shape[1]
    tm, tn, tk = _t(l, tm), _t(gd, tn), _t(gd, tk)
    return _mm("group_mm_bwd_data", (ng, l // tm, gd // tn, gd // tk),
               dmg, pl.BlockSpec((tm, tk), lambda g, i, j, k: (i, g * (gd // tk) + k)),
               wg, pl.BlockSpec((None, tn, tk), lambda g, i, j, k: (g, j, k)), NT,
               [jax.ShapeDtypeStruct((l, dp), F32)],
               [pl.BlockSpec((tm, tn), lambda g, i, j, k: (i, g * (gd // tn) + j))], (tm, tn))[0]


def _group_mm_bwd_weight(mp, dmg, ng, tm=512, tk=1024):
    l, dp = mp.shape
    gd = dp // ng
    tm, tk = _t(gd, tm), _t(l, tk)
    return _mm("group_mm_bwd_weight", (ng, gd // tm, l // tk),
               mp, pl.BlockSpec((tk, tm), lambda g, i, k: (k, g * (gd // tm) + i)),
               dmg, pl.BlockSpec((tk, gd), lambda g, i, k: (k, g)), TN,
               [jax.ShapeDtypeStruct((ng, gd, gd // 2), F32)],
               [pl.BlockSpec((None, tm, gd // 2), lambda g, i, k: (g, i, 0))], (tm, gd), epi=_pack_epi)[0]


def _cast_bf16(name, w, tr=256):
    r, c = w.shape
    tr = _row_tile(r, tr, 16)

    def body(w_ref, o_ref):
        o_ref[...] = w_ref[...].astype(BF)

    blk = pl.BlockSpec((tr, c), lambda i: (i, 0))
    return pl.pallas_call(body, grid=(r // tr,), in_specs=[blk], out_specs=blk,
                          out_shape=jax.ShapeDtypeStruct((r, c), BF), compiler_params=_params(("parallel",)),
                          name=name)(w)


def _pack_rows(name, w, tr=256):
    r, c = w.shape
    tr = _row_tile(r, tr)

    def body(w_ref, o_ref):
        o_ref[...] = _pack_pairs(w_ref[...])

    return pl.pallas_call(body, grid=(r // tr,), in_specs=[pl.BlockSpec((tr, c), lambda i: (i, 0))],
                          out_specs=pl.BlockSpec((tr, c // 2), lambda i: (i, 0)),
                          out_shape=jax.ShapeDtypeStruct((r, c // 2), F32), compiler_params=_params(("parallel",)),
                          name=name)(w)


def _unpack_rows(name, w, tr=512):
    r, h = w.shape
    tr = _row_tile(r, tr, 16)

    def body(w_ref, o_ref):
        hi, lo = _unpack_pairs(w_ref[...])
        o_ref[:, :h] = hi.astype(BF)
        o_ref[:, h:] = lo.astype(BF)

    return pl.pallas_call(body, grid=(r // tr,), in_specs=[pl.BlockSpec((tr, h), lambda i: (i, 0))],
                          out_specs=pl.BlockSpec((tr, 2 * h), lambda i: (i, 0)),
                          out_shape=jax.ShapeDtypeStruct((r, 2 * h), BF), compiler_params=_params(("parallel",)),
                          name=name)(w)


def _reduce_packed(name, recv, tr=256):
    nd, r, h = recv.shape
    tr = _row_tile(r, tr)

    def body(p_ref, o_ref):
        hi, lo = _unpack_pairs(p_ref[0])
        for k in range(1, nd):
            a, b = _unpack_pairs(p_ref[k])
            hi, lo = hi + a, lo + b
        o_ref[:, :h] = hi
        o_ref[:, h:] = lo

    return pl.pallas_call(body, grid=(r // tr,), in_specs=[pl.BlockSpec((nd, tr, h), lambda i: (0, i, 0))],
                          out_specs=pl.BlockSpec((tr, 2 * h), lambda i: (i, 0)),
                          out_shape=jax.ShapeDtypeStruct((r, 2 * h), F32), compiler_params=_params(("parallel",)),
                          name=name)(recv)


def _adamw_math(w, g, m, v):
    m2 = ADAM_B1 * m + (1.0 - ADAM_B1) * g
    v2 = ADAM_B2 * v + (1.0 - ADAM_B2) * (g * g)
    m_hat = m2 / (1.0 - ADAM_B1 ** ADAM_STEP)
    v_hat = v2 / (1.0 - ADAM_B2 ** ADAM_STEP)
    delta = -ADAM_LR * (m_hat / (jnp.sqrt(v_hat) + ADAM_EPS) + ADAM_WD * w)
    return delta, m2, v2


def _adamw(name, w, g, m, v, tr=256):
    r, c = w.shape
    tr = _row_tile(r, tr)

    def body(w_ref, g_ref, m_ref, v_ref, d_ref, m2_ref, v2_ref):
        d, m2, v2 = _adamw_math(w_ref[...], g_ref[...], m_ref[...], v_ref[...])
        d_ref[...] = d
        m2_ref[...] = m2
        v2_ref[...] = v2

    blk = pl.BlockSpec((tr, c), lambda i: (i, 0))
    shp = jax.ShapeDtypeStruct((r, c), F32)
    return pl.pallas_call(body, grid=(r // tr,), in_specs=[blk] * 4, out_specs=[blk] * 3, out_shape=[shp] * 3,
                          compiler_params=_params(("parallel",)), name=name)(w, g, m, v)


def _sum_slots(name, a):
    nd, r, c = a.shape

    def body(a_ref, o_ref):
        s = a_ref[0]
        for k in range(1, nd):
            s = s + a_ref[k]
        o_ref[...] = s

    return pl.pallas_call(body, out_shape=jax.ShapeDtypeStruct((r, c), F32), name=name)(a)


def _head_rows(v, g_n, r):
    return jnp.pad(v.reshape(g_n, 1, r), ((0, 0), (0, 0), (0, HEAD_LANES - r)))


class _Later:
    def __init__(self, gather_pool_in, pool_in_of, gather_rest, rest_of, scatter_of):
        self.gather_pool_in, self.pool_in_of = gather_pool_in, pool_in_of
        self.gather_rest, self.rest_of, self.scatter_of = gather_rest, rest_of, scatter_of


def _local_step(x, target, ln_g, final_g, wt_in, conv_w, conv_b, dt_bias, a_log, d_skip, norm_g, scale, later):
    l, d = x.shape
    di = norm_g.shape[1]
    h = dt_bias.shape[1]
    p = di // h
    conv = conv_b.shape[1]
    n, t = SSM_STATE, SSD_CHUNK
    g_n = (conv - di) // (2 * n)
    r = h // g_n
    nm = di + conv
    dp = scale.shape[1]
    ng = len(POOL_WINDOWS)
    dims = (h, g_n, r, p, n, t)
    pw = d

    h0, r0 = _rms_fwd("rms0_fwd", x, ln_g[0:1])
    pm = _mm_nt("in_proj_main", h0, wt_in, 0, nm, exch=later.gather_pool_in)
    if later.gather_pool_in is not None:
        pm, *arrived = pm
    else:
        arrived = None
    wp_in = later.pool_in_of(arrived)
    dtr = _mm_nt("in_proj_dt", h0, wt_in, nm, h, tn=h)
    xbc = _conv_fwd(pm, di, conv_w, conv_b)
    dtb, alog = _head_rows(dt_bias, g_n, r), _head_rows(a_log, g_n, r)
    dskx = jnp.repeat(d_skip.reshape(g_n, 1, r), p, axis=2)
    dtr_g = jnp.pad(jnp.transpose(dtr.reshape(l, g_n, r), (1, 0, 2)), ((0, 0), (0, 0), (0, HEAD_LANES - r)))
    y, states, *arrived = _ssd_fwd(xbc, dtr_g, dtb, alog, dskx, dims, exch=later.gather_rest)
    w_out, wg, wp_out = later.rest_of(arrived)
    yn, r_n = _gate_norm_fwd(y, pm, norm_g)
    x1 = _mm_nn("ssm_out_proj", yn, w_out, add=x)

    h1, r1 = _rms_fwd("rms1_fwd", x1, ln_g[1:2])
    pu = _mm_nn_blocked("pool_in_proj", h1, wp_in)
    mp = _pool_mix("pool_mix_fwd", pu, dp, False, BF)
    mg, yp = _group_mm_fwd(mp, wg, pu, scale)
    x2 = _mm_nn("pool_out_proj", yp, wp_out, add=x1)

    dx2, dx2_b, d_final_g, loss = _loss_head(x2, final_g, target)

    dyp = _mm_nt("pool_out_bwd_data", dx2_b, wp_out, 0, dp, tn=1024)
    gw_pout = _mm_tn_packed("pool_out_bwd_weight", yp, dx2_b, pw)
    dmg, dpu, d_scale = _pool_gate_bwd(dyp, mg, pu, scale)
    dmp = _group_mm_bwd_data(dmg, wg)
    gw_g = _group_mm_bwd_weight(mp, dmg, ng)
    dpu = _pool_mix("pool_mix_bwd", dmp, dp, True, BF, into=dpu)
    dh1 = _mm_nt_blocked("pool_in_bwd_data", dpu, wp_in, 0)
    gw_pin = _mm_tn_packed("pool_in_bwd_weight", h1, dpu, 2 * dp // N_DEV)
    dx1, dx1_b, d_ln1 = _rms_bwd("rms1_bwd", dh1, x1, r1, ln_g[1:2], dx2)

    dyn = _mm_nt("ssm_out_bwd_data", dx1_b, w_out, 0, di, tn=1024)
    gw_out = _mm_tn_packed("ssm_out_bwd_weight", yn, dx1_b, pw)
    dy, dproj, d_norm_g = _gate_norm_bwd(dyn, y, pm, r_n, norm_g)
    big = dict(w_out=gw_out, wp_in=gw_pin, wg=gw_g, wp_out=gw_pout)
    dxbc, ddt_g, dbias_g, dalog_g, dd_g, *received = _ssd_bwd(
        xbc, dtr_g, dtb, alog, dskx, states, dy, dims, exch=later.scatter_of(big))
    dproj, d_conv_w, d_conv_b = _conv_bwd(pm, di, conv_w, conv_b, dxbc, dproj)
    ddt = jnp.transpose(ddt_g[:, :, :r], (1, 0, 2)).reshape(l, h)
    gwt_in = jnp.concatenate([_mm_tn_packed("in_proj_bwd_weight", dproj, h0, pw),
                              _mm_tn_packed("in_proj_bwd_weight_dt", ddt, h0, pw)], axis=1)
    dh0 = _mm_nn("in_proj_bwd_data", dproj, wt_in, 0)
    dh0 = _mm_nn("in_proj_bwd_data_dt", ddt, wt_in, nm, add=dh0)
    grad_x, _, d_ln0 = _rms_bwd("rms0_bwd", dh0, x, r0, ln_g[0:1], dx1)

    def heads(v):
        return v[:, 0, :r].reshape(1, h)

    small = dict(ln_g=jnp.concatenate([d_ln0, d_ln1], axis=0), final_g=d_final_g, conv_w=d_conv_w, conv_b=d_conv_b,
                 dt_bias=heads(dbias_g), a_log=heads(dalog_g), d_skip=heads(dd_g), norm_g=d_norm_g, scale=d_scale)
    big["wt_in"] = gwt_in
    return loss, grad_x, small, big, received


SMALL_ORDER = ("ln_g", "final_g", "conv_w", "conv_b", "dt_bias", "a_log", "d_skip", "norm_g", "scale", "loss")


def _flatten_small(parts):
    flat = jnp.concatenate([parts[k].reshape(-1) for k in SMALL_ORDER])
    n = flat.shape[0]
    rows = -(-n // 1024) * 8
    return jnp.pad(flat, (0, rows * 128 - n)).reshape(rows, 128)


def _split_small(flat, shapes):
    flat = flat.reshape(-1)
    out, off = {}, 0
    for k in SMALL_ORDER:
        size = int(np.prod(shapes[k]))
        out[k] = flat[off:off + size].reshape(shapes[k])
        off += size
    return out


def kernel(x, ln_g, final_g, ssm_w_in, ssm_conv_w, ssm_conv_b, ssm_dt_bias, ssm_a_log, ssm_d, ssm_norm_g, ssm_w_out, pool_w_in, pool_w_group, pool_scale, pool_w_out, loss_target, m_ln_g, m_final_g, m_ssm_w_in, m_ssm_conv_w, m_ssm_conv_b, m_ssm_dt_bias, m_ssm_a_log, m_ssm_d, m_ssm_norm_g, m_ssm_w_out, m_pool_w_in, m_pool_w_group, m_pool_scale, m_pool_w_out, v_ln_g, v_final_g, v_ssm_w_in, v_ssm_conv_w, v_ssm_conv_b, v_ssm_dt_bias, v_ssm_a_log, v_ssm_d, v_ssm_norm_g, v_ssm_w_out, v_pool_w_in, v_pool_w_group, v_pool_scale, v_pool_w_out):
    l, d = x.shape[1], x.shape[2]
    me = 4 * lax.axis_index("x") + 2 * lax.axis_index("y") + lax.axis_index("c")
    ng, gds, gd = pool_w_group.shape[1], pool_w_group.shape[2], pool_w_group.shape[3]
    sin_s = ssm_w_in.shape[2]
    dp_s = pool_w_out.shape[1]
    conv_s = ssm_conv_w.shape[2]

    wt_in_s = _pack_rows("pack_w_in", jnp.transpose(ssm_w_in[0]))
    w_out_s = _cast_bf16("cast_w_out", ssm_w_out[0])
    wp_in_s = _cast_bf16("cast_pool_w_in", pool_w_in[0])
    wg_s = _cast_bf16("cast_pool_w_group", pool_w_group[0].reshape(ng * gds, gd))
    wp_out_s = _cast_bf16("cast_pool_w_out", pool_w_out[0])
    small_s = jnp.concatenate([ssm_conv_w[0].reshape(-1), pool_scale[0]])
    n_small = small_s.shape[0]
    small_s = jnp.pad(small_s, (0, -(-n_small // 1024) * 1024 - n_small)).reshape(-1, 128)
    wt_in_g, small_g = _exchange_alone("all_gather_w_in", _Gather([wt_in_s, small_s]))
    wt_in = _unpack_rows("unpack_w_in", wt_in_g.reshape(N_DEV * sin_s, d // 2))
    small_all = small_g.reshape(N_DEV, -1)[:, :n_small]
    conv_w = jnp.transpose(small_all[:, :CONV_TAPS * conv_s].reshape(N_DEV, CONV_TAPS, conv_s), (1, 0, 2))
    conv_w = conv_w.reshape(CONV_TAPS, -1)
    scale = small_all[:, CONV_TAPS * conv_s:].reshape(1, -1)

    def rest_of(arrived):
        w_out_g, wg_g, wp_out_g = arrived
        wg = jnp.transpose(wg_g.reshape(N_DEV, ng, gds, gd), (1, 0, 2, 3)).reshape(ng, gd, gd)
        return w_out_g.reshape(-1, d), wg, wp_out_g.reshape(-1, d)

    def rows_major(gp):
        q, _, hw = gp.shape
        return jnp.transpose(gp.reshape(q, N_DEV, -1, hw), (1, 0, 2, 3))

    def scatter_of(big):
        return _Scatter([rows_major(big["w_out"]), big["wp_in"][:, None],
                         jnp.transpose(big["wg"].reshape(ng, N_DEV, gds, gd // 2), (1, 0, 2, 3)),
                         rows_major(big["wp_out"])])

    later = _Later(_Gather([wp_in_s]), lambda arrived: arrived[0], _Gather([w_out_s, wg_s, wp_out_s]), rest_of,
                   scatter_of)

    loss, grad_x, small, big, recv = _local_step(
        x[0], loss_target[0], ln_g, final_g.reshape(1, d), wt_in, conv_w, ssm_conv_b, ssm_dt_bias, ssm_a_log, ssm_d,
        ssm_norm_g, scale, later)
    recv_w_in, = _exchange_alone("exchange_w_in_grad", _Scatter([rows_major(big["wt_in"])]))

    def reduced(name, rv):
        q = rv.shape[1]
        cols = [_reduce_packed(f"{name}_{j}", rv[:, j]) for j in range(q)]
        return cols[0] if q == 1 else jnp.concatenate(cols, axis=1)

    g_w_in = jnp.transpose(reduced("reduce_w_in", recv_w_in))[None]
    g_w_out = reduced("reduce_w_out", recv[0])[None]
    g_wp_in = reduced("reduce_pool_w_in", recv[1])[None]
    g_wg = _reduce_packed("reduce_pool_w_group", recv[2].reshape(N_DEV, ng * gds, gd // 2)).reshape(1, ng, gds, gd)
    g_wp_out = reduced("reduce_pool_w_out", recv[3])[None]

    small["loss"] = loss
    shapes = {k: small[k].shape for k in SMALL_ORDER}
    gathered_small, = _exchange_alone("all_gather_small_grads", _Gather([_flatten_small(small)]))
    summed = _split_small(_sum_slots("sum_small_grads", gathered_small), shapes)
    g_conv_w = lax.dynamic_slice_in_dim(summed["conv_w"], me * conv_s, conv_s, axis=1)[None]
    g_scale = lax.dynamic_slice_in_dim(summed["scale"], me * dp_s, dp_s, axis=1)
    grads = dict(ln_g=summed["ln_g"], final_g=summed["final_g"].reshape(d), ssm_w_in=g_w_in, ssm_conv_w=g_conv_w,
                 ssm_conv_b=summed["conv_b"], ssm_dt_bias=summed["dt_bias"], ssm_a_log=summed["a_log"],
                 ssm_d=summed["d_skip"], ssm_norm_g=summed["norm_g"], ssm_w_out=g_w_out, pool_w_in=g_wp_in,
                 pool_w_group=g_wg, pool_scale=g_scale, pool_w_out=g_wp_out)

    weights = dict(ln_g=ln_g, final_g=final_g, ssm_w_in=ssm_w_in, ssm_conv_w=ssm_conv_w, ssm_conv_b=ssm_conv_b,
                   ssm_dt_bias=ssm_dt_bias, ssm_a_log=ssm_a_log, ssm_d=ssm_d, ssm_norm_g=ssm_norm_g,
                   ssm_w_out=ssm_w_out, pool_w_in=pool_w_in, pool_w_group=pool_w_group, pool_scale=pool_scale,
                   pool_w_out=pool_w_out)
    m_in = dict(ln_g=m_ln_g, final_g=m_final_g, ssm_w_in=m_ssm_w_in, ssm_conv_w=m_ssm_conv_w, ssm_conv_b=m_ssm_conv_b,
                ssm_dt_bias=m_ssm_dt_bias, ssm_a_log=m_ssm_a_log, ssm_d=m_ssm_d, ssm_norm_g=m_ssm_norm_g,
                ssm_w_out=m_ssm_w_out, pool_w_in=m_pool_w_in, pool_w_group=m_pool_w_group, pool_scale=m_pool_scale,
                pool_w_out=m_pool_w_out)
    v_in = dict(ln_g=v_ln_g, final_g=v_final_g, ssm_w_in=v_ssm_w_in, ssm_conv_w=v_ssm_conv_w, ssm_conv_b=v_ssm_conv_b,
                ssm_dt_bias=v_ssm_dt_bias, ssm_a_log=v_ssm_a_log, ssm_d=v_ssm_d, ssm_norm_g=v_ssm_norm_g,
                ssm_w_out=v_ssm_w_out, pool_w_in=v_pool_w_in, pool_w_group=v_pool_w_group, pool_scale=v_pool_scale,
                pool_w_out=v_pool_w_out)
    names = list(weights)
    big_names = ("ssm_w_in", "ssm_w_out", "pool_w_in", "pool_w_group", "pool_w_out")
    delta, new_m, new_v = {}, {}, {}
    for k in big_names:
        shp = weights[k].shape
        two_d = (-1, shp[-1])
        dk, mk, vk = _adamw(f"adamw_{k}", weights[k].reshape(two_d), grads[k].reshape(two_d), m_in[k].reshape(two_d),
                            v_in[k].reshape(two_d))
        delta[k], new_m[k], new_v[k] = dk.reshape(shp), mk.reshape(shp), vk.reshape(shp)
    small_names = [k for k in names if k not in big_names]

    def packed(tree):
        flat = jnp.concatenate([tree[k].reshape(-1) for k in small_names])
        nn = flat.shape[0]
        return jnp.pad(flat, (0, -(-nn // 1024) * 1024 - nn), constant_values=1.0).reshape(-1, 128)

    ds, ms, vs = _adamw("adamw_small", packed(weights), packed(grads), packed(m_in), packed(v_in))
    off = 0
    for k in small_names:
        shp = weights[k].shape
        size = int(np.prod(shp))
        for res, arr in ((delta, ds), (new_m, ms), (new_v, vs)):
            res[k] = arr.reshape(-1)[off:off + size].reshape(shp)
        off += size

    return (summed["loss"].reshape(()), grad_x[None], *[grads[k] for k in names], *[delta[k] for k in names],
            *[new_m[k] for k in names], *[new_v[k] for k in names])
```

```python
import math

import jax
import jax.numpy as jnp
import numpy as np
from jax import lax
from jax.experimental import pallas as pl
from jax.experimental.pallas import tpu as pltpu

F32 = jnp.float32
BF = jnp.bfloat16
U32 = jnp.uint32

N_DEV = 8
EPS = 1e-6
SSD_CHUNK = 64
SSM_STATE = 128
CONV_TAPS = 4
HEAD_LANES = 128
POOL_WINDOWS = (2, 4, 8, 16)
ADAM_LR, ADAM_B1, ADAM_B2, ADAM_EPS, ADAM_WD, ADAM_STEP = 0.001, 0.9, 0.999, 1e-08, 0.01, 10
VMEM_LIMIT = 56 * 1024 * 1024
NEG_BIG = -1e30
HAND_ON_AT = 6

NN = ((1,), (0,))
NT = ((1,), (1,))
TN = ((0,), (0,))
MESH = pl.DeviceIdType.MESH


def _t(dim, pref):
    return pref if dim % pref == 0 else dim


def _row_tile(rows, pref, mult=8):
    best = rows
    for cand in range(mult, min(rows, pref) + 1, mult):
        if rows % cand == 0:
            best = cand
    return best


def _params(sem=None):
    return pltpu.CompilerParams(dimension_semantics=sem, vmem_limit_bytes=VMEM_LIMIT)


def _silu(x):
    return x * (1.0 / (1.0 + jnp.exp(-x)))


def _dsilu(x):
    s = 1.0 / (1.0 + jnp.exp(-x))
    return s * (1.0 + x * (1.0 - s))


def _bdot(a, b, dims=NN):
    return lax.dot_general(a.astype(BF), b.astype(BF), (dims, ((), ())), preferred_element_type=F32)


def _pack_pairs(x):
    h = x.shape[1] // 2
    hi = lax.bitcast_convert_type(x[:, :h].astype(jnp.bfloat16).astype(F32), U32)
    lo = lax.bitcast_convert_type(x[:, h:].astype(jnp.bfloat16).astype(F32), U32)
    return lax.bitcast_convert_type(hi | (lo >> 16), F32)


def _unpack_pairs(w):
    u = lax.bitcast_convert_type(w, U32)
    hi = lax.bitcast_convert_type(u & jnp.uint32(0xFFFF0000), F32)
    lo = lax.bitcast_convert_type(u << 16, F32)
    return hi, lo


def _mesh_pos():
    return lax.axis_index("x"), lax.axis_index("y"), lax.axis_index("c")


def _slot(pos):
    return 4 * pos[0] + 2 * pos[1] + pos[2]


class _Gather:
    def __init__(self, arrays):
        self.arrays = list(arrays)
        self.out_shapes = [jax.ShapeDtypeStruct((N_DEV, *s.shape), s.dtype) for s in arrays]

    def phases(self, src, dst, send_sems, recv_sems, local_sems):
        n_arr = len(self.arrays)
        x, y, c = _mesh_pos()
        me, sibling = (x, y, c), (x, y, 1 - c)
        chips = [(1 - x, y), (x, 1 - y), (1 - x, 1 - y)]

        def copy(a, k, block, to, from_src):
            return pltpu.make_async_remote_copy(
                src_ref=src[a] if from_src else dst[a].at[_slot(block)], dst_ref=dst[a].at[_slot(block)],
                send_sem=send_sems.at[a * 7 + k], recv_sem=recv_sems.at[a * 7 + k], device_id=to, device_id_type=MESH)

        def mine(a):
            return pltpu.make_async_copy(src[a], dst[a].at[_slot(me)], local_sems.at[a])

        def first(a):
            return [copy(a, 0, me, sibling, True)] + [copy(a, 1 + j, me, (*chip, c), True)
                                                     for j, chip in enumerate(chips)]

        def start():
            for a in range(n_arr):
                mine(a).start()
                for cp in first(a):
                    cp.start()

        def middle():
            for j, chip in enumerate(chips):
                for a in range(n_arr):
                    copy(a, 1 + j, (*chip, c), me, False).wait_recv()
                    copy(a, 4 + j, (*chip, c), sibling, False).start()

        def finish():
            for a in range(n_arr):
                copy(a, 0, sibling, me, False).wait_recv()
                for j, chip in enumerate(chips):
                    copy(a, 4 + j, (*chip, 1 - c), me, False).wait_recv()
                for cp in first(a):
                    cp.wait_send()
                for j, chip in enumerate(chips):
                    copy(a, 4 + j, (*chip, c), sibling, False).wait_send()
                mine(a).wait()

        return start, middle, finish


class _Scatter:
    def __init__(self, arrays):
        self.arrays = list(arrays)
        self.out_shapes = [jax.ShapeDtypeStruct(b.shape, b.dtype) for b in arrays]

    def phases(self, src, dst, send_sems, recv_sems, local_sems):
        n_arr = len(self.arrays)
        x, y, c = _mesh_pos()
        me = (x, y, c)
        peers = []
        for k in range(1, N_DEV):
            fx, fy, fc = (k >> 2) & 1, (k >> 1) & 1, k & 1
            peers.append((1 - x if fx else x, 1 - y if fy else y, 1 - c if fc else c))

        def copy(a, k, peer, sending):
            return pltpu.make_async_remote_copy(
                src_ref=src[a].at[_slot(peer)], dst_ref=dst[a].at[_slot(me) if sending else _slot(peer)],
                send_sem=send_sems.at[a * 7 + k], recv_sem=recv_sems.at[a * 7 + k], device_id=peer,
                device_id_type=MESH)

        def mine(a):
            return pltpu.make_async_copy(src[a].at[_slot(me)], dst[a].at[_slot(me)], local_sems.at[a])

        def start():
            for a in range(n_arr):
                mine(a).start()
                for k, peer in enumerate(peers):
                    copy(a, k, peer, True).start()

        def finish():
            for a in range(n_arr):
                for k, peer in enumerate(peers):
                    copy(a, k, peer, False).wait_recv()
                for k, peer in enumerate(peers):
                    copy(a, k, peer, True).wait_send()
                mine(a).wait()

        return start, None, finish


def _hosted_call(name, body, grid, in_specs, out_specs, out_shape, scratch_shapes, sem, operands, exch=None):
    if exch is None:
        return pl.pallas_call(body, grid=grid, in_specs=in_specs, out_specs=out_specs, out_shape=out_shape,
                              scratch_shapes=scratch_shapes, compiler_params=_params(sem), name=name)(*operands)
    n_in, n_out, n_scr, ne = len(in_specs), len(out_specs), len(scratch_shapes), len(exch.arrays)
    total = math.prod(grid)

    def wrapped(*refs):
        ins, ex_in = refs[:n_in], refs[n_in:n_in + ne]
        outs = refs[n_in + ne:n_in + ne + n_out]
        ex_out = refs[n_in + ne + n_out:n_in + 2 * ne + n_out]
        scr = refs[n_in + 2 * ne + n_out:n_in + 2 * ne + n_out + n_scr]
        step = 0
        for axis, size in enumerate(grid):
            step = step * size + pl.program_id(axis)
        start, middle, finish = exch.phases(ex_in, ex_out, *refs[-3:])
        pl.when(step == 0)(start)
        if middle is not None:
            pl.when(step == (total * HAND_ON_AT) // 8)(middle)
        body(*ins, *outs, *scr)
        pl.when(step == total - 1)(finish)

    hbm = pl.BlockSpec(memory_space=pl.ANY)
    sems = [pltpu.SemaphoreType.DMA((ne * 7,)), pltpu.SemaphoreType.DMA((ne * 7,)), pltpu.SemaphoreType.DMA((ne,))]
    return pl.pallas_call(
        wrapped, grid=grid, in_specs=[*in_specs, *[hbm] * ne], out_specs=[*out_specs, *[hbm] * ne],
        out_shape=[*out_shape, *exch.out_shapes], scratch_shapes=[*scratch_shapes, *sems],
        compiler_params=pltpu.CompilerParams(dimension_semantics=("arbitrary",) * len(grid),
                                             vmem_limit_bytes=VMEM_LIMIT, has_side_effects=True),
        name=name)(*operands, *exch.arrays)


def _flip_peers():
    x, y, c = _mesh_pos()
    peers = []
    for k in range(1, N_DEV):
        fx, fy, fc = (k >> 2) & 1, (k >> 1) & 1, k & 1
        peers.append((1 - x if fx else x, 1 - y if fy else y, 1 - c if fc else c))
    return (x, y, c), peers


def _split_scatter_copy(src, land, send_sems, recv_sems, k, me, peer, sending):
    return pltpu.make_async_remote_copy(
        src_ref=src.at[_slot(peer)], dst_ref=land.at[_slot(me) if sending else _slot(peer)],
        send_sem=send_sems.at[k], recv_sem=recv_sems.at[k], device_id=peer, device_id_type=MESH)


def _scatter_start(name, blocks):
    def body(src, land, send_sems, recv_sems, src_thru, land_thru, token):
        me, peers = _flip_peers()
        for k, peer in enumerate(peers):
            _split_scatter_copy(src, land, send_sems, recv_sems, k, me, peer, True).start()
        token[...] = jnp.zeros_like(token)

    hbm = pl.BlockSpec(memory_space=pltpu.HBM)
    sem = pl.BlockSpec(memory_space=pltpu.SEMAPHORE)
    return pl.pallas_call(
        body, name=name,
        out_shape=(pltpu.SemaphoreType.DMA((N_DEV - 1,)), pltpu.SemaphoreType.DMA((N_DEV - 1,)),
                   pltpu.HBM(blocks.shape, blocks.dtype), pltpu.HBM(blocks.shape, blocks.dtype),
                   jax.ShapeDtypeStruct((8, 128), F32)),
        in_specs=(hbm, hbm), out_specs=(sem, sem, hbm, hbm, pl.BlockSpec(memory_space=pltpu.VMEM)),
        input_output_aliases={0: 2, 1: 3},
        compiler_params=pltpu.CompilerParams(has_side_effects=pltpu.SideEffectType.DATAFLOW_SIDE_EFFECTING),
    )(pltpu.with_memory_space_constraint(blocks, pltpu.HBM),
      pltpu.with_memory_space_constraint(lax.empty(blocks.shape, blocks.dtype), pltpu.HBM))


def _scatter_wait(name, send_sems, recv_sems, src_thru, land_thru, after):
    def body(src, land, send_sems, recv_sems, *rest):
        me, peers = _flip_peers()
        for k, peer in enumerate(peers):
            _split_scatter_copy(src, land, send_sems, recv_sems, k, me, peer, True).wait_send()
            _split_scatter_copy(src, land, send_sems, recv_sems, k, me, peer, False).wait_recv()

    hbm = pl.BlockSpec(memory_space=pltpu.HBM)
    sem = pl.BlockSpec(memory_space=pltpu.SEMAPHORE)
    return pl.pallas_call(
        body, name=name,
        out_shape=(pltpu.HBM(src_thru.shape, src_thru.dtype), pltpu.HBM(land_thru.shape, land_thru.dtype)),
        in_specs=(hbm, hbm, sem, sem, *[pl.BlockSpec(memory_space=pl.ANY)] * len(after)), out_specs=(hbm, hbm),
        input_output_aliases={0: 0, 1: 1},
        compiler_params=pltpu.CompilerParams(has_side_effects=pltpu.SideEffectType.DATAFLOW_SIDE_EFFECTING),
    )(src_thru, land_thru, send_sems, recv_sems, *after)


def _exchange_alone(name, exch):
    def body():
        pass

    return _hosted_call(name, body, (1,), [], [], [], [], None, (), exch)


def _mm(name, grid, a, a_spec, b, b_spec, dims, outs, o_specs, acc_shape, extra=(), extra_specs=(), epi=None,
        exch=None):
    nk = grid[-1]
    n_extra, n_out = len(extra), len(outs)

    def body(*refs):
        a_ref, b_ref = refs[0], refs[1]
        ex = refs[2:2 + n_extra]
        o_refs = refs[2 + n_extra:2 + n_extra + n_out]

        def write(res):
            res = (res,) if epi is None else epi(res, *[e[...] for e in ex])
            for o, r in zip(o_refs, res):
                o[...] = r.astype(o.dtype)

        if nk == 1:
            write(_bdot(a_ref[...], b_ref[...], dims))
            return
        acc = refs[-1]
        k = pl.program_id(len(grid) - 1)

        @pl.when(k == 0)
        def _():
            acc[...] = _bdot(a_ref[...], b_ref[...], dims)

        @pl.when(jnp.logical_and(k > 0, k < nk - 1))
        def _():
            acc[...] += _bdot(a_ref[...], b_ref[...], dims)

        @pl.when(k == nk - 1)
        def _():
            write(acc[...] + _bdot(a_ref[...], b_ref[...], dims))

    sem = ("parallel",) * (len(grid) - 1) + ("arbitrary",)
    scratch = [] if nk == 1 else [pltpu.VMEM(acc_shape, F32)]
    return _hosted_call(name, body, grid, [a_spec, b_spec, *extra_specs], list(o_specs), list(outs),
                        scratch, sem, (a, b, *extra), exch)


def _add_epi(acc, add):
    return (acc + add,)


def _pack_epi(acc):
    return (_pack_pairs(acc),)


def _mm_nt(name, a, b, n_off, n, out_dtype=F32, tm=1024, tn=512, exch=None):
    m, kk = a.shape
    tm, tn = _t(m, tm), math.gcd(_t(n, tn), n_off)
    res = _mm(name, (m // tm, n // tn, 1),
              a, pl.BlockSpec((tm, kk), lambda i, j, k: (i, 0)),
              b, pl.BlockSpec((tn, kk), lambda i, j, k: (n_off // tn + j, 0)), NT,
              [jax.ShapeDtypeStruct((m, n), out_dtype)], [pl.BlockSpec((tm, tn), lambda i, j, k: (i, j))], (tm, tn),
              exch=exch)
    return res[0] if exch is None else res


def _mm_nn_blocked(name, a, b3, tm=1024, tn=512):
    m, kk = a.shape
    nb, _, cs = b3.shape
    tm, tn = _t(m, tm), _t(cs, tn)
    per = cs // tn
    return _mm(name, (m // tm, nb * per, 1),
               a, pl.BlockSpec((tm, kk), lambda i, j, k: (i, 0)),
               b3, pl.BlockSpec((None, kk, tn), lambda i, j, k: (j // per, 0, j % per)), NN,
               [jax.ShapeDtypeStruct((m, nb * cs), F32)], [pl.BlockSpec((tm, tn), lambda i, j, k: (i, j))],
               (tm, tn))[0]


def _mm_nt_blocked(name, a, b3, blk_off, add=None, tm=1024, tn=1024):
    m = a.shape[0]
    _, n, cs = b3.shape
    nk = a.shape[1] // cs
    tm, tn = _t(m, tm), _t(n, tn)
    extra, especs, epi = (), (), None
    if add is not None:
        extra, especs, epi = (add,), (pl.BlockSpec((tm, tn), lambda i, j, k: (i, j)),), _add_epi
    return _mm(name, (m // tm, n // tn, nk),
               a, pl.BlockSpec((tm, cs), lambda i, j, k: (i, k)),
               b3, pl.BlockSpec((None, tn, cs), lambda i, j, k: (blk_off + k, j, 0)), NT,
               [jax.ShapeDtypeStruct((m, n), F32)], [pl.BlockSpec((tm, tn), lambda i, j, k: (i, j))], (tm, tn),
               extra, especs, epi)[0]


def _mm_nn(name, a, b, k_off=0, kk=None, add=None, out_dtype=F32, tm=1024, tn=1024, tk=1024):
    m = a.shape[0]
    kk = a.shape[1] if kk is None else kk
    n = b.shape[1]
    tm, tn, tk = _t(m, tm), _t(n, tn), math.gcd(_t(kk, tk), k_off)
    extra, especs, epi = (), (), None
    if add is not None:
        extra, especs, epi = (add,), (pl.BlockSpec((tm, tn), lambda i, j, k: (i, j)),), _add_epi
    return _mm(name, (m // tm, n // tn, kk // tk),
               a, pl.BlockSpec((tm, tk), lambda i, j, k: (i, k)),
               b, pl.BlockSpec((tk, tn), lambda i, j, k: (k_off // tk + k, j)), NN,
               [jax.ShapeDtypeStruct((m, n), out_dtype)], [pl.BlockSpec((tm, tn), lambda i, j, k: (i, j))], (tm, tn),
               extra, especs, epi)[0]


TN_ACC_ELEMENTS = 1 << 20


def _mm_tn_packed(name, a, b, pw, tk=1024):
    kk, m = a.shape
    n = b.shape[1]
    tm, tk = _t(m, TN_ACC_ELEMENTS // pw), _t(kk, tk)
    return _mm(name, (m // tm, n // pw, kk // tk),
               a, pl.BlockSpec((tk, tm), lambda i, j, k: (k, i)),
               b, pl.BlockSpec((tk, pw), lambda i, j, k: (k, j)), TN,
               [jax.ShapeDtypeStruct((n // pw, m, pw // 2), F32)],
               [pl.BlockSpec((None, tm, pw // 2), lambda i, j, k: (j, i, 0))], (tm, pw), epi=_pack_epi)[0]


def _rms_fwd(name, x, g, tm=256):
    l, d = x.shape
    tm = _t(l, tm)

    def body(x_ref, g_ref, h_ref, r_ref):
        xv = x_ref[...]
        r = lax.rsqrt(jnp.mean(xv * xv, axis=-1, keepdims=True) + EPS)
        h_ref[...] = (xv * r * g_ref[...]).astype(BF)
        r_ref[...] = r

    return pl.pallas_call(
        body, grid=(l // tm,),
        in_specs=[pl.BlockSpec((tm, d), lambda i: (i, 0)), pl.BlockSpec((1, d), lambda i: (0, 0))],
        out_specs=[pl.BlockSpec((tm, d), lambda i: (i, 0)), pl.BlockSpec((tm, 1), lambda i: (i, 0))],
        out_shape=[jax.ShapeDtypeStruct((l, d), BF), jax.ShapeDtypeStruct((l, 1), F32)],
        compiler_params=_params(("parallel",)), name=name)(x, g)


def _rms_bwd(name, dh, x, r, g, dres, tm=256):
    l, d = x.shape
    tm = _t(l, tm)

    def body(dh_ref, x_ref, r_ref, g_ref, dres_ref, dx_ref, dxb_ref, dg_ref):
        i = pl.program_id(0)
        rr = r_ref[...]
        xhat = x_ref[...] * rr
        dhv = dh_ref[...]
        dxh = dhv * g_ref[...]
        dx = dres_ref[...] + rr * (dxh - xhat * jnp.mean(dxh * xhat, axis=-1, keepdims=True))
        dx_ref[...] = dx
        dxb_ref[...] = dx.astype(BF)

        @pl.when(i == 0)
        def _():
            dg_ref[...] = jnp.zeros_like(dg_ref)

        dg_ref[...] += jnp.sum(dhv * xhat, axis=0, keepdims=True)

    row = pl.BlockSpec((tm, d), lambda i: (i, 0))
    vec = pl.BlockSpec((1, d), lambda i: (0, 0))
    return pl.pallas_call(
        body, grid=(l // tm,), in_specs=[row, row, pl.BlockSpec((tm, 1), lambda i: (i, 0)), vec, row],
        out_specs=[row, row, vec],
        out_shape=[jax.ShapeDtypeStruct((l, d), F32), jax.ShapeDtypeStruct((l, d), BF),
                   jax.ShapeDtypeStruct((1, d), F32)],
        compiler_params=_params(("arbitrary",)), name=name)(dh, x, r, g, dres)


def _loss_head(x, g, target, tm=256):
    l, d = x.shape
    tm = _t(l, tm)

    def body(x_ref, g_ref, t_ref, dx_ref, dxb_ref, dg_ref, loss_ref):
        i = pl.program_id(0)
        xv = x_ref[...]
        gv = g_ref[...]
        r = lax.rsqrt(jnp.mean(xv * xv, axis=-1, keepdims=True) + EPS)
        xhat = xv * r
        e = xhat * gv - t_ref[...]
        dy = e * (1.0 / d)
        dxh = dy * gv
        dx = r * (dxh - xhat * jnp.mean(dxh * xhat, axis=-1, keepdims=True))
        dx_ref[...] = dx
        dxb_ref[...] = dx.astype(BF)

        @pl.when(i == 0)
        def _():
            dg_ref[...] = jnp.zeros_like(dg_ref)
            loss_ref[...] = jnp.zeros_like(loss_ref)

        dg_ref[...] += jnp.sum(dy * xhat, axis=0, keepdims=True)
        loss_ref[...] += 0.5 * jnp.sum(jnp.sum(e * e, axis=-1, keepdims=True) * (1.0 / d), axis=0, keepdims=True)

    row = pl.BlockSpec((tm, d), lambda i: (i, 0))
    vec = pl.BlockSpec((1, d), lambda i: (0, 0))
    return pl.pallas_call(
        body, grid=(l // tm,), in_specs=[row, vec, row],
        out_specs=[row, row, vec, pl.BlockSpec((1, 1), lambda i: (0, 0))],
        out_shape=[jax.ShapeDtypeStruct((l, d), F32), jax.ShapeDtypeStruct((l, d), BF),
                   jax.ShapeDtypeStruct((1, d), F32), jax.ShapeDtypeStruct((1, 1), F32)],
        compiler_params=_params(("arbitrary",)), name="loss_head")(x, g, target)


CONV_HALO = 8


def _conv_pre(x_ref, w_ref, b_ref, i, tr):
    r0 = pl.multiple_of(i * tr, tr)
    cur = x_ref[pl.ds(r0, tr), :]
    prev = x_ref[pl.ds(pl.multiple_of(jnp.maximum(r0 - CONV_HALO, 0), CONV_HALO), CONV_HALO), :]
    prev = jnp.where(i > 0, prev, 0.0)
    ext = jnp.concatenate([prev, cur], axis=0)
    taps = []
    for k in range(CONV_TAPS):
        s = CONV_TAPS - 1 - k
        taps.append(cur if s == 0 else pltpu.roll(ext, s, 0)[CONV_HALO:])
    pre = b_ref[...] + sum(w_ref[k:k + 1, :] * taps[k] for k in range(CONV_TAPS))
    return r0, pre, taps


def _conv_fwd(pm, col_off, conv_w, conv_b, cw=256, tr=512):
    l = pm.shape[0]
    c = conv_w.shape[1]
    cw, tr = _t(c, cw), _t(l, tr)
    assert col_off % cw == 0

    def body(x_ref, w_ref, b_ref, o_ref):
        def step(i, carry):
            r0, pre, _ = _conv_pre(x_ref, w_ref, b_ref, i, tr)
            o_ref[pl.ds(r0, tr), :] = _silu(pre)
            return carry
        lax.fori_loop(0, l // tr, step, 0)

    return pl.pallas_call(
        body, grid=(c // cw,),
        in_specs=[pl.BlockSpec((l, cw), lambda j: (0, col_off // cw + j)),
                  pl.BlockSpec((CONV_TAPS, cw), lambda j: (0, j)), pl.BlockSpec((1, cw), lambda j: (0, j))],
        out_specs=pl.BlockSpec((l, cw), lambda j: (0, j)), out_shape=jax.ShapeDtypeStruct((l, c), F32),
        compiler_params=_params(("parallel",)), name="conv_fwd")(pm, conv_w, conv_b)


def _conv_bwd(pm, col_off, conv_w, conv_b, dy, dproj, cw=256, tr=512):
    l = pm.shape[0]
    c = conv_w.shape[1]
    cw, tr = _t(c, cw), _t(l, tr)
    nt = l // tr

    def body(x_ref, w_ref, b_ref, dy_ref, _, dx_ref, dw_ref, db_ref, dpre_ref):
        def step1(i, carry):
            dws, db = carry
            r0, pre, taps = _conv_pre(x_ref, w_ref, b_ref, i, tr)
            dpre = dy_ref[pl.ds(r0, tr), :] * _dsilu(pre)
            dpre_ref[pl.ds(r0, tr), :] = dpre
            dws = tuple(dws[k] + jnp.sum(dpre * taps[k], axis=0, keepdims=True) for k in range(CONV_TAPS))
            return dws, db + jnp.sum(dpre, axis=0, keepdims=True)

        z = jnp.zeros((1, cw), F32)
        dws, db = lax.fori_loop(0, nt, step1, ((z,) * CONV_TAPS, z))
        for k in range(CONV_TAPS):
            dw_ref[k:k + 1, :] = dws[k]
        db_ref[...] = db

        def step2(i, carry):
            r0 = pl.multiple_of(i * tr, tr)
            cur = dpre_ref[pl.ds(r0, tr), :]
            nxt = dpre_ref[pl.ds(pl.multiple_of(jnp.minimum(r0 + tr, l - CONV_HALO), CONV_HALO), CONV_HALO), :]
            nxt = jnp.where(i < nt - 1, nxt, 0.0)
            ext = jnp.concatenate([cur, nxt], axis=0)
            acc = w_ref[CONV_TAPS - 1:CONV_TAPS, :] * cur
            for k in range(CONV_TAPS - 1):
                s = CONV_TAPS - 1 - k
                acc = acc + w_ref[k:k + 1, :] * pltpu.roll(ext, tr + CONV_HALO - s, 0)[:tr]
            dx_ref[pl.ds(r0, tr), :] = acc.astype(dx_ref.dtype)
            return carry
        lax.fori_loop(0, nt, step2, 0)

    col = pl.BlockSpec((l, cw), lambda j: (0, j))
    shifted = pl.BlockSpec((l, cw), lambda j: (0, col_off // cw + j))
    return pl.pallas_call(
        body, grid=(c // cw,),
        in_specs=[shifted, pl.BlockSpec((CONV_TAPS, cw), lambda j: (0, j)), pl.BlockSpec((1, cw), lambda j: (0, j)),
                  col, pl.BlockSpec(memory_space=pl.ANY)],
        out_specs=[shifted, pl.BlockSpec((CONV_TAPS, cw), lambda j: (0, j)), pl.BlockSpec((1, cw), lambda j: (0, j))],
        out_shape=[jax.ShapeDtypeStruct(dproj.shape, dproj.dtype), jax.ShapeDtypeStruct((CONV_TAPS, c), F32),
                   jax.ShapeDtypeStruct((1, c), F32)],
        scratch_shapes=[pltpu.VMEM((l, cw), F32)], input_output_aliases={4: 0},
        compiler_params=_params(("parallel",)), name="conv_bwd")(pm, conv_w, conv_b, dy, dproj)


def _split(x, pieces):
    out, rest = [], x
    for _ in range(pieces):
        piece = rest.astype(BF)
        out.append(piece)
        rest = rest - piece.astype(F32)
    return out


def _rows_times(xs, m_stack, pieces):
    x = xs[0] if len(xs) == 1 else jnp.concatenate(xs, axis=0)
    out = _bdot(jnp.concatenate(_split(x, pieces), axis=1), m_stack)
    sizes = [v.shape[0] for v in xs]
    offs = np.cumsum([0] + sizes)
    return [out[offs[i]:offs[i + 1]] for i in range(len(xs))]


def _times_rows(m_stack, x, pieces):
    return _bdot(m_stack, jnp.concatenate(_split(x, pieces), axis=0))


EXPAND_PIECES = 3
FOLD_PIECES = 2


def _ssd_consts(r, p, t):
    assert p == t, "heads expand to P lanes of the inputs and to T lanes of the decay matrices alike"
    rp = r * p
    tri = np.tril(np.ones((t, t), np.float32))
    ep = np.zeros((HEAD_LANES, rp), np.float32)
    ep[np.arange(rp) // p, np.arange(rp)] = 1.0
    itile = (np.arange(t)[:, None] == (np.arange(rp) % t)[None, :]).astype(np.float32)
    lmask = (np.arange(t)[:, None] >= (np.arange(rp) % t)[None, :]).astype(np.float32)
    bmask = ((np.arange(rp) // t)[:, None] == (np.arange(rp) // p)[None, :]).astype(np.float32)
    return [jnp.asarray(np.concatenate([ep] * EXPAND_PIECES, axis=0), BF),
            jnp.asarray(np.concatenate([ep.T] * FOLD_PIECES, axis=0), BF),
            jnp.asarray(np.concatenate([tri] * EXPAND_PIECES, axis=1), BF),
            jnp.asarray(np.concatenate([tri.T] * EXPAND_PIECES, axis=1), BF),
            jnp.asarray(ep.T.copy()), jnp.asarray(itile), jnp.asarray(lmask), jnp.asarray(bmask)]


def _ssd_common(xs, bm, pre_raw, dtb, alog, e_stack, tri_stack, itile, lmask, bmask, r, t):
    pre = pre_raw + dtb
    dt = jnp.maximum(pre, 0.0) + jnp.log(1.0 + jnp.exp(-jnp.abs(pre)))
    a = -jnp.exp(alog)
    cs = _times_rows(tri_stack, dt * a, EXPAND_PIECES)
    last = cs[t - 1:t, :]
    ecs, w = jnp.exp(cs), jnp.exp(last - cs)
    dtx, ecsx, wx, csx = _rows_times([dt, ecs, w, cs], e_stack, EXPAND_PIECES)
    csrow = jnp.sum(csx * itile, axis=0, keepdims=True)
    lx = jnp.exp(jnp.where(lmask > 0.0, csx - csrow, NEG_BIG))
    xdt = xs * dtx
    xblk = (jnp.concatenate([xdt] * r, axis=0) * bmask).astype(BF)
    btile = jnp.concatenate([bm] * r, axis=0).astype(BF)
    return pre, dt, a, cs, last, ecs, w, dtx, ecsx, wx, lx, xdt, xblk, btile


def _ssd_specs(l, g_n, r, p, n, t, conv, rev):
    nc = l // t
    rp = r * p
    cidx = (lambda c: nc - 1 - c) if rev else (lambda c: c)
    row_spec = lambda width: pl.BlockSpec((t, width), lambda c: (cidx(c), 0))
    dtr_spec = pl.BlockSpec((g_n, t, HEAD_LANES), lambda c: (0, cidx(c), 0))
    par_spec = pl.BlockSpec((g_n, 1, HEAD_LANES), lambda c: (0, 0, 0))
    dskx_spec = pl.BlockSpec((g_n, 1, rp), lambda c: (0, 0, 0))
    st_spec = pl.BlockSpec((None, g_n, rp, n), lambda c: (cidx(c), 0, 0, 0))
    return nc, row_spec, dtr_spec, par_spec, dskx_spec, st_spec


def _const_specs(consts):
    return [pl.BlockSpec(a.shape, lambda c: (0, 0)) for a in consts]


def _ssd_fwd(xbc, dtr, dtb, alog, dskx, dims, exch=None):
    h, g_n, r, p, n, t = dims
    l, conv = xbc.shape
    rp, hp = r * p, h * p
    nc, row_spec, dtr_spec, par_spec, dskx_spec, st_spec = _ssd_specs(l, g_n, r, p, n, t, conv, False)
    consts = _ssd_consts(r, p, t)

    def body(x_ref, dtr_ref, dtb_ref, alog_ref, dskx_ref,
             e_ref, et_ref, tri_ref, trit_ref, ept_ref, it_ref, lm_ref, bmk_ref, y_ref, st_ref, s_ref):
        @pl.when(pl.program_id(0) == 0)
        def _():
            s_ref[...] = jnp.zeros_like(s_ref)

        for g in range(g_n):
            xs = x_ref[:, g * rp:(g + 1) * rp]
            bm = x_ref[:, hp + g * n:hp + (g + 1) * n]
            cm = x_ref[:, hp + (g_n + g) * n:hp + (g_n + g + 1) * n]
            (_, _, _, _, last, _, _, _, ecsx, wx, lx, xdt, xblk, btile) = _ssd_common(
                xs, bm, dtr_ref[g], dtb_ref[g], alog_ref[g], e_ref[...], tri_ref[...], it_ref[...], lm_ref[...],
                bmk_ref[...], r, t)
            s_in = s_ref[g]
            st_ref[g] = s_in
            cbx = _bdot(cm, btile, NT)
            yd = _bdot(cbx * lx, xblk)
            yo = ecsx * _bdot(cm, s_in, NT)
            y_ref[:, g * rp:(g + 1) * rp] = yd + yo + dskx_ref[g] * xs
            elast = jnp.sum(ept_ref[...] * jnp.exp(last), axis=1, keepdims=True)
            s_ref[g] = elast * s_in + _bdot(xdt * wx, bm, TN)

    return _hosted_call(
        "ssd_fwd", body, (nc,),
        [row_spec(conv), dtr_spec, par_spec, par_spec, dskx_spec, *_const_specs(consts)],
        [row_spec(hp), st_spec],
        [jax.ShapeDtypeStruct((l, hp), F32), jax.ShapeDtypeStruct((nc, g_n, rp, n), F32)],
        [pltpu.VMEM((g_n, rp, n), F32)], ("arbitrary",),
        (xbc, dtr, dtb, alog, dskx, *consts), exch)


def _ssd_bwd(xbc, dtr, dtb, alog, dskx, states, dy, dims, exch=None):
    h, g_n, r, p, n, t = dims
    l, conv = xbc.shape
    rp, hp = r * p, h * p
    nc, row_spec, dtr_spec, par_spec, dskx_spec, st_spec = _ssd_specs(l, g_n, r, p, n, t, conv, True)
    consts = _ssd_consts(r, p, t)

    def fold_rows(v, rows):
        return sum(v[k * rows:(k + 1) * rows, :] for k in range(r))

    def body(x_ref, dtr_ref, dtb_ref, alog_ref, dskx_ref, st_ref, dy_ref,
             e_ref, et_ref, tri_ref, trit_ref, ept_ref, it_ref, lm_ref, bmk_ref,
             dx_ref, ddt_ref, dbias_ref, dalog_ref, dd_ref, ds_ref):
        @pl.when(pl.program_id(0) == 0)
        def _():
            ds_ref[...] = jnp.zeros_like(ds_ref)
            dbias_ref[...] = jnp.zeros_like(dbias_ref)
            dalog_ref[...] = jnp.zeros_like(dalog_ref)
            dd_ref[...] = jnp.zeros_like(dd_ref)

        ept, itile, bmask = ept_ref[...], it_ref[...], bmk_ref[...]
        last_row = lax.broadcasted_iota(jnp.int32, (t, HEAD_LANES), 0) == t - 1
        for g in range(g_n):
            xs = x_ref[:, g * rp:(g + 1) * rp]
            bm = x_ref[:, hp + g * n:hp + (g + 1) * n]
            cm = x_ref[:, hp + (g_n + g) * n:hp + (g_n + g + 1) * n]
            dy = dy_ref[:, g * rp:(g + 1) * rp]
            (pre, dt, a, cs, last, ecs, w, dtx, ecsx, wx, lx, xdt, xblk, btile) = _ssd_common(
                xs, bm, dtr_ref[g], dtb_ref[g], alog_ref[g], e_ref[...], tri_ref[...], itile, lm_ref[...], bmask, r, t)
            s_in = st_ref[g]
            ds_out = ds_ref[g]
            elast_row = jnp.exp(last)
            elast = jnp.sum(ept * elast_row, axis=1, keepdims=True)

            cbx = _bdot(cm, btile, NT)
            amat = cbx * lx
            da = _bdot(dy, xblk, NT)
            dxdt = fold_rows(_bdot(amat, dy, TN) * bmask, t)
            dcbx = da * lx
            q = da * amat
            dc = _bdot(dcbx, btile)
            db = fold_rows(_bdot(dcbx, cm, TN), t)

            g0 = _bdot(cm, s_in, NT)
            dg0 = dy * ecsx
            dc = dc + _bdot(dg0, s_in)

            z = _bdot(bm, ds_out, NT)
            dxdt = dxdt + z * wx
            db = db + _bdot(xdt * wx, ds_out)
            ds_ref[g] = elast * ds_out + _bdot(dg0, cm, TN)

            skip = jnp.sum(dy * xs, axis=0, keepdims=True)
            skip = jnp.concatenate([skip, jnp.zeros((15, rp), F32)], axis=0)
            f_q, f_yo, f_w, f_dt, f_skip = _rows_times(
                [q - itile * jnp.sum(q, axis=0, keepdims=True), dy * g0, z * xdt, dxdt * xs, skip],
                et_ref[...], FOLD_PIECES)
            dd_ref[g] += f_skip[0:1]
            dww = f_w * w
            dcs = f_q + f_yo * ecs - dww
            dlast = jnp.sum(jnp.sum(ds_out * s_in, axis=1, keepdims=True) * ept, axis=0, keepdims=True) * elast_row
            dlast = dlast + jnp.sum(dww, axis=0, keepdims=True)

            dcs = dcs + jnp.where(last_row, dlast, 0.0)
            dadt = _times_rows(trit_ref[...], dcs, EXPAND_PIECES)
            dalog_ref[g] += jnp.sum(dadt * dt, axis=0, keepdims=True) * a
            dpre = (dadt * a + f_dt) * (1.0 / (1.0 + jnp.exp(-pre)))
            ddt_ref[g] = dpre
            dbias_ref[g] += jnp.sum(dpre, axis=0, keepdims=True)
            dx_ref[:, g * rp:(g + 1) * rp] = dskx_ref[g] * dy + dxdt * dtx
            dx_ref[:, hp + g * n:hp + (g + 1) * n] = db
            dx_ref[:, hp + (g_n + g) * n:hp + (g_n + g + 1) * n] = dc

    par_shape = jax.ShapeDtypeStruct((g_n, 1, HEAD_LANES), F32)
    return _hosted_call(
        "ssd_bwd", body, (nc,),
        [row_spec(conv), dtr_spec, par_spec, par_spec, dskx_spec, st_spec, row_spec(hp), *_const_specs(consts)],
        [row_spec(conv), dtr_spec, par_spec, par_spec, par_spec],
        [jax.ShapeDtypeStruct((l, conv), F32), jax.ShapeDtypeStruct((g_n, l, HEAD_LANES), F32),
         par_shape, par_shape, par_shape],
        [pltpu.VMEM((g_n, rp, n), F32)], ("arbitrary",),
        (xbc, dtr, dtb, alog, dskx, states, dy, *consts), exch)


def _gate_norm_fwd(y, pm, g, tm=128):
    l, di = y.shape
    tm = _t(l, tm)

    def body(y_ref, z_ref, g_ref, o_ref, r_ref):
        v = y_ref[...] * _silu(z_ref[...])
        rr = lax.rsqrt(jnp.mean(v * v, axis=-1, keepdims=True) + EPS)
        o_ref[...] = (v * rr * g_ref[...]).astype(BF)
        r_ref[...] = rr

    row = pl.BlockSpec((tm, di), lambda i: (i, 0))
    return pl.pallas_call(
        body, grid=(l // tm,), in_specs=[row, row, pl.BlockSpec((1, di), lambda i: (0, 0))],
        out_specs=[row, pl.BlockSpec((tm, 1), lambda i: (i, 0))],
        out_shape=[jax.ShapeDtypeStruct((l, di), BF), jax.ShapeDtypeStruct((l, 1), F32)],
        compiler_params=_params(("parallel",)), name="gate_norm_fwd")(y, pm, g)


def _gate_norm_bwd(dyn, y, pm, rstd, g, tm=128):
    l, di = y.shape
    tm = _t(l, tm)

    def body(dyn_ref, y_ref, z_ref, r_ref, g_ref, dy_ref, dz_ref, dg_ref):
        i = pl.program_id(0)
        yv, zv, rr = y_ref[...], z_ref[...], r_ref[...]
        sz = _silu(zv)
        vhat = yv * sz * rr
        dn = dyn_ref[...]
        dvh = dn * g_ref[...]
        dv = rr * (dvh - vhat * jnp.mean(dvh * vhat, axis=-1, keepdims=True))
        dy_ref[...] = dv * sz
        dz_ref[...] = (dv * yv * _dsilu(zv)).astype(BF)

        @pl.when(i == 0)
        def _():
            dg_ref[...] = jnp.zeros_like(dg_ref)

        dg_ref[...] += jnp.sum(dn * vhat, axis=0, keepdims=True)

    row = pl.BlockSpec((tm, di), lambda i: (i, 0))
    vec = pl.BlockSpec((1, di), lambda i: (0, 0))
    return pl.pallas_call(
        body, grid=(l // tm,), in_specs=[row, row, row, pl.BlockSpec((tm, 1), lambda i: (i, 0)), vec],
        out_specs=[row, row, vec],
        out_shape=[jax.ShapeDtypeStruct((l, di), F32), jax.ShapeDtypeStruct((l, pm.shape[1]), BF),
                   jax.ShapeDtypeStruct((1, di), F32)],
        compiler_params=_params(("arbitrary",)), name="gate_norm_bwd")(dyn, y, pm, rstd, g)


POOL_HALO = 16


def _pool_mix(name, src, dp, backward, out_dtype, into=None, cw=256, tr=512):
    l = src.shape[0]
    gd = dp // len(POOL_WINDOWS)
    cwl, trl = _t(gd, cw), _t(l, tr)
    nt = l // trl

    def body(x_ref, *rest):
        o_ref = rest[-1]
        gi = pl.program_id(0)
        for wi, win in enumerate(POOL_WINDOWS):
            @pl.when(gi == wi)
            def _(win=win):
                def step(i, carry):
                    r0 = pl.multiple_of(i * trl, trl)
                    cur = x_ref[pl.ds(r0, trl), :]
                    trow = r0 + lax.broadcasted_iota(jnp.int32, (trl, 1), 0)
                    cnt = jnp.minimum(trow + 1, win).astype(F32)
                    if not backward:
                        halo = x_ref[pl.ds(pl.multiple_of(jnp.maximum(r0 - POOL_HALO, 0), POOL_HALO), POOL_HALO), :]
                        halo = jnp.where(i > 0, halo, 0.0)
                        s = jnp.concatenate([halo, cur], axis=0)
                        sh = 1
                        while sh < win:
                            s = s + pltpu.roll(s, sh, 0)
                            sh *= 2
                        res = s[POOL_HALO:] / cnt - cur
                    else:
                        halo = x_ref[pl.ds(pl.multiple_of(jnp.minimum(r0 + trl, l - POOL_HALO), POOL_HALO),
                                           POOL_HALO), :]
                        hrow = r0 + trl + lax.broadcasted_iota(jnp.int32, (POOL_HALO, 1), 0)
                        hcnt = jnp.minimum(hrow + 1, win).astype(F32)
                        halo = jnp.where(i < nt - 1, halo / hcnt, 0.0)
                        s = jnp.concatenate([cur / cnt, halo], axis=0)
                        sh = 1
                        while sh < win:
                            s = s + pltpu.roll(s, trl + POOL_HALO - sh, 0)
                            sh *= 2
                        res = s[:trl] - cur
                    o_ref[pl.ds(r0, trl), :] = res.astype(o_ref.dtype)
                    return carry
                lax.fori_loop(0, nt, step, 0)

    col = pl.BlockSpec((l, cwl), lambda g, j: (0, g * (gd // cwl) + j))
    if into is None:
        return pl.pallas_call(
            body, grid=(len(POOL_WINDOWS), gd // cwl), in_specs=[col], out_specs=col,
            out_shape=jax.ShapeDtypeStruct((l, dp), out_dtype),
            compiler_params=_params(("parallel", "parallel")), name=name)(src)
    return pl.pallas_call(
        body, grid=(len(POOL_WINDOWS), gd // cwl), in_specs=[col, pl.BlockSpec(memory_space=pl.ANY)], out_specs=col,
        out_shape=jax.ShapeDtypeStruct(into.shape, into.dtype), input_output_aliases={1: 0},
        compiler_params=_params(("parallel", "parallel")), name=name)(src, into)


def _pool_gate_bwd(dyp, mg, pu, scale, tm=128):
    l, dp = mg.shape
    tm = _t(l, tm)

    def body(dyp_ref, mg_ref, gate_ref, sc_ref, dmg_ref, dgate_ref, dsc_ref):
        i = pl.program_id(0)
        d, m, gt, sc = dyp_ref[...], mg_ref[...], gate_ref[...], sc_ref[...]
        sg = _silu(gt)
        dmg_ref[...] = (d * sc * sg).astype(BF)
        dgate_ref[...] = (d * m * sc * _dsilu(gt)).astype(BF)

        @pl.when(i == 0)
        def _():
            dsc_ref[...] = jnp.zeros_like(dsc_ref)

        dsc_ref[...] += jnp.sum(d * m * sg, axis=0, keepdims=True)

    row = pl.BlockSpec((tm, dp), lambda i: (i, 0))
    right = pl.BlockSpec((tm, dp), lambda i: (i, 1))
    vec = pl.BlockSpec((1, dp), lambda i: (0, 0))
    return pl.pallas_call(
        body, grid=(l // tm,), in_specs=[row, row, right, vec], out_specs=[row, right, vec],
        out_shape=[jax.ShapeDtypeStruct((l, dp), BF), jax.ShapeDtypeStruct((l, 2 * dp), BF),
                   jax.ShapeDtypeStruct((1, dp), F32)],
        compiler_params=_params(("arbitrary",)), name="pool_gate_bwd")(dyp, mg, pu, scale)


def _group_mm_fwd(mp, wg, pu, scale, tm=1024, tn=1024, tk=1024):
    l, dp = mp.shape
    ng, gd = wg.shape[0], wg.shape[1]
    tm, tn, tk = _t(l, tm), _t(gd, tn), _t(gd, tk)

    def epi(acc, gate, sc):
        return acc, acc * sc * _silu(gate)

    out = pl.BlockSpec((tm, tn), lambda g, i, j, k: (i, g * (gd // tn) + j))
    return _mm("group_mm_fwd", (ng, l // tm, gd // tn, gd // tk),
               mp, pl.BlockSpec((tm, tk), lambda g, i, j, k: (i, g * (gd // tk) + k)),
               wg, pl.BlockSpec((None, tk, tn), lambda g, i, j, k: (g, k, j)), NN,
               [jax.ShapeDtypeStruct((l, dp), F32), jax.ShapeDtypeStruct((l, dp), BF)], [out, out], (tm, tn),
               (pu, scale),
               (pl.BlockSpec((tm, tn), lambda g, i, j, k: (i, (dp + g * gd) // tn + j)),
                pl.BlockSpec((1, tn), lambda g, i, j, k: (0, g * (gd // tn) + j))), epi)


def _group_mm_bwd_data(dmg, wg, tm=1024, tn=1024, tk=1024):
    l, dp = dmg.shape
    ng, gd = wg.shape[0], wg.shape[1]
    tm, tn, tk = _t(l, tm), _t(gd, tn), _t(gd, tk)
    return _mm("group_mm_bwd_data", (ng, l // tm, gd // tn, gd // tk),
               dmg, pl.BlockSpec((tm, tk), lambda g, i, j, k: (i, g * (gd // tk) + k)),
               wg, pl.BlockSpec((None, tn, tk), lambda g, i, j, k: (g, j, k)), NT,
               [jax.ShapeDtypeStruct((l, dp), F32)],
               [pl.BlockSpec((tm, tn), lambda g, i, j, k: (i, g * (gd // tn) + j))], (tm, tn))[0]


def _group_mm_bwd_weight(mp, dmg, ng, tm=512, tk=1024):
    l, dp = mp.shape
    gd = dp // ng
    tm, tk = _t(gd, tm), _t(l, tk)
    return _mm("group_mm_bwd_weight", (ng, gd // tm, l // tk),
               mp, pl.BlockSpec((tk, tm), lambda g, i, k: (k, g * (gd // tm) + i)),
               dmg, pl.BlockSpec((tk, gd), lambda g, i, k: (k, g)), TN,
               [jax.ShapeDtypeStruct((ng, gd, gd // 2), F32)],
               [pl.BlockSpec((None, tm, gd // 2), lambda g, i, k: (g, i, 0))], (tm, gd), epi=_pack_epi)[0]


def _cast_bf16(name, w, tr=256):
    r, c = w.shape
    tr = _row_tile(r, tr, 16)

    def body(w_ref, o_ref):
        o_ref[...] = w_ref[...].astype(BF)

    blk = pl.BlockSpec((tr, c), lambda i: (i, 0))
    return pl.pallas_call(body, grid=(r // tr,), in_specs=[blk], out_specs=blk,
                          out_shape=jax.ShapeDtypeStruct((r, c), BF), compiler_params=_params(("parallel",)),
                          name=name)(w)


def _pack_rows(name, w, tr=256):
    r, c = w.shape
    tr = _row_tile(r, tr)

    def body(w_ref, o_ref):
        o_ref[...] = _pack_pairs(w_ref[...])

    return pl.pallas_call(body, grid=(r // tr,), in_specs=[pl.BlockSpec((tr, c), lambda i: (i, 0))],
                          out_specs=pl.BlockSpec((tr, c // 2), lambda i: (i, 0)),
                          out_shape=jax.ShapeDtypeStruct((r, c // 2), F32), compiler_params=_params(("parallel",)),
                          name=name)(w)


def _unpack_rows(name, w, tr=512):
    r, h = w.shape
    tr = _row_tile(r, tr, 16)

    def body(w_ref, o_ref):
        hi, lo = _unpack_pairs(w_ref[...])
        o_ref[:, :h] = hi.astype(BF)
        o_ref[:, h:] = lo.astype(BF)

    return pl.pallas_call(body, grid=(r // tr,), in_specs=[pl.BlockSpec((tr, h), lambda i: (i, 0))],
                          out_specs=pl.BlockSpec((tr, 2 * h), lambda i: (i, 0)),
                          out_shape=jax.ShapeDtypeStruct((r, 2 * h), BF), compiler_params=_params(("parallel",)),
                          name=name)(w)


def _reduce_packed(name, recv, tr=256):
    nd, r, h = recv.shape
    tr = _row_tile(r, tr)

    def body(p_ref, o_ref):
        hi, lo = _unpack_pairs(p_ref[0])
        for k in range(1, nd):
            a, b = _unpack_pairs(p_ref[k])
            hi, lo = hi + a, lo + b
        o_ref[:, :h] = hi
        o_ref[:, h:] = lo

    return pl.pallas_call(body, grid=(r // tr,), in_specs=[pl.BlockSpec((nd, tr, h), lambda i: (0, i, 0))],
                          out_specs=pl.BlockSpec((tr, 2 * h), lambda i: (i, 0)),
                          out_shape=jax.ShapeDtypeStruct((r, 2 * h), F32), compiler_params=_params(("parallel",)),
                          name=name)(recv)


def _adamw_math(w, g, m, v):
    m2 = ADAM_B1 * m + (1.0 - ADAM_B1) * g
    v2 = ADAM_B2 * v + (1.0 - ADAM_B2) * (g * g)
    m_hat = m2 / (1.0 - ADAM_B1 ** ADAM_STEP)
    v_hat = v2 / (1.0 - ADAM_B2 ** ADAM_STEP)
    delta = -ADAM_LR * (m_hat / (jnp.sqrt(v_hat) + ADAM_EPS) + ADAM_WD * w)
    return delta, m2, v2


def _adamw(name, w, g, m, v, tr=256):
    r, c = w.shape
    tr = _row_tile(r, tr)

    def body(w_ref, g_ref, m_ref, v_ref, d_ref, m2_ref, v2_ref):
        d, m2, v2 = _adamw_math(w_ref[...], g_ref[...], m_ref[...], v_ref[...])
        d_ref[...] = d
        m2_ref[...] = m2
        v2_ref[...] = v2

    blk = pl.BlockSpec((tr, c), lambda i: (i, 0))
    shp = jax.ShapeDtypeStruct((r, c), F32)
    return pl.pallas_call(body, grid=(r // tr,), in_specs=[blk] * 4, out_specs=[blk] * 3, out_shape=[shp] * 3,
                          compiler_params=_params(("parallel",)), name=name)(w, g, m, v)


def _sum_slots(name, a):
    nd, r, c = a.shape

    def body(a_ref, o_ref):
        s = a_ref[0]
        for k in range(1, nd):
            s = s + a_ref[k]
        o_ref[...] = s

    return pl.pallas_call(body, out_shape=jax.ShapeDtypeStruct((r, c), F32), name=name)(a)


def _head_rows(v, g_n, r):
    return jnp.pad(v.reshape(g_n, 1, r), ((0, 0), (0, 0), (0, HEAD_LANES - r)))


class _Later:
    def __init__(self, gather_pool_in, pool_in_of, gather_rest, rest_of, scatter_of, send_w_in):
        self.gather_pool_in, self.pool_in_of = gather_pool_in, pool_in_of
        self.gather_rest, self.rest_of, self.scatter_of = gather_rest, rest_of, scatter_of
        self.send_w_in = send_w_in


def _local_step(x, target, ln_g, final_g, wt_in, conv_w, conv_b, dt_bias, a_log, d_skip, norm_g, scale, later):
    l, d = x.shape
    di = norm_g.shape[1]
    h = dt_bias.shape[1]
    p = di // h
    conv = conv_b.shape[1]
    n, t = SSM_STATE, SSD_CHUNK
    g_n = (conv - di) // (2 * n)
    r = h // g_n
    nm = di + conv
    dp = scale.shape[1]
    ng = len(POOL_WINDOWS)
    dims = (h, g_n, r, p, n, t)
    pw = d

    h0, r0 = _rms_fwd("rms0_fwd", x, ln_g[0:1])
    pm = _mm_nt("in_proj_main", h0, wt_in, 0, nm, exch=later.gather_pool_in)
    if later.gather_pool_in is not None:
        pm, *arrived = pm
    else:
        arrived = None
    wp_in = later.pool_in_of(arrived)
    dtr = _mm_nt("in_proj_dt", h0, wt_in, nm, h, tn=h)
    xbc = _conv_fwd(pm, di, conv_w, conv_b)
    dtb, alog = _head_rows(dt_bias, g_n, r), _head_rows(a_log, g_n, r)
    dskx = jnp.repeat(d_skip.reshape(g_n, 1, r), p, axis=2)
    dtr_g = jnp.pad(jnp.transpose(dtr.reshape(l, g_n, r), (1, 0, 2)), ((0, 0), (0, 0), (0, HEAD_LANES - r)))
    y, states, *arrived = _ssd_fwd(xbc, dtr_g, dtb, alog, dskx, dims, exch=later.gather_rest)
    w_out, wg, wp_out = later.rest_of(arrived)
    yn, r_n = _gate_norm_fwd(y, pm, norm_g)
    x1 = _mm_nn("ssm_out_proj", yn, w_out, add=x)

    h1, r1 = _rms_fwd("rms1_fwd", x1, ln_g[1:2])
    pu = _mm_nn_blocked("pool_in_proj", h1, wp_in)
    mp = _pool_mix("pool_mix_fwd", pu, dp, False, BF)
    mg, yp = _group_mm_fwd(mp, wg, pu, scale)
    x2 = _mm_nn("pool_out_proj", yp, wp_out, add=x1)

    dx2, dx2_b, d_final_g, loss = _loss_head(x2, final_g, target)

    dyp = _mm_nt("pool_out_bwd_data", dx2_b, wp_out, 0, dp, tn=1024)
    gw_pout = _mm_tn_packed("pool_out_bwd_weight", yp, dx2_b, pw)
    dmg, dpu, d_scale = _pool_gate_bwd(dyp, mg, pu, scale)
    dmp = _group_mm_bwd_data(dmg, wg)
    gw_g = _group_mm_bwd_weight(mp, dmg, ng)
    dpu = _pool_mix("pool_mix_bwd", dmp, dp, True, BF, into=dpu)
    dh1 = _mm_nt_blocked("pool_in_bwd_data", dpu, wp_in, 0)
    gw_pin = _mm_tn_packed("pool_in_bwd_weight", h1, dpu, 2 * dp // N_DEV)
    dx1, dx1_b, d_ln1 = _rms_bwd("rms1_bwd", dh1, x1, r1, ln_g[1:2], dx2)

    dyn = _mm_nt("ssm_out_bwd_data", dx1_b, w_out, 0, di, tn=1024)
    gw_out = _mm_tn_packed("ssm_out_bwd_weight", yn, dx1_b, pw)
    dy, dproj, d_norm_g = _gate_norm_bwd(dyn, y, pm, r_n, norm_g)
    big = dict(w_out=gw_out, wp_in=gw_pin, wg=gw_g, wp_out=gw_pout)
    dxbc, ddt_g, dbias_g, dalog_g, dd_g, *received = _ssd_bwd(
        xbc, dtr_g, dtb, alog, dskx, states, dy, dims, exch=later.scatter_of(big))
    dproj, d_conv_w, d_conv_b = _conv_bwd(pm, di, conv_w, conv_b, dxbc, dproj)
    ddt = jnp.transpose(ddt_g[:, :, :r], (1, 0, 2)).reshape(l, h)
    gwt_in = jnp.concatenate([_mm_tn_packed("in_proj_bwd_weight", dproj, h0, pw),
                              _mm_tn_packed("in_proj_bwd_weight_dt", ddt, h0, pw)], axis=1)
    sent_w_in, token = later.send_w_in(gwt_in)
    dh0 = _mm_nn("in_proj_bwd_data_dt", ddt + token, wt_in, nm)
    dh0 = _mm_nn("in_proj_bwd_data", dproj, wt_in, 0, add=dh0)
    grad_x, _, d_ln0 = _rms_bwd("rms0_bwd", dh0, x, r0, ln_g[0:1], dx1)

    def heads(v):
        return v[:, 0, :r].reshape(1, h)

    small = dict(ln_g=jnp.concatenate([d_ln0, d_ln1], axis=0), final_g=d_final_g, conv_w=d_conv_w, conv_b=d_conv_b,
                 dt_bias=heads(dbias_g), a_log=heads(dalog_g), d_skip=heads(dd_g), norm_g=d_norm_g, scale=d_scale)
    big["wt_in"] = gwt_in
    return loss, grad_x, small, big, received, sent_w_in


SMALL_ORDER = ("ln_g", "final_g", "conv_w", "conv_b", "dt_bias", "a_log", "d_skip", "norm_g", "scale", "loss")


def _flatten_small(parts):
    flat = jnp.concatenate([parts[k].reshape(-1) for k in SMALL_ORDER])
    n = flat.shape[0]
    rows = -(-n // 1024) * 8
    return jnp.pad(flat, (0, rows * 128 - n)).reshape(rows, 128)


def _split_small(flat, shapes):
    flat = flat.reshape(-1)
    out, off = {}, 0
    for k in SMALL_ORDER:
        size = int(np.prod(shapes[k]))
        out[k] = flat[off:off + size].reshape(shapes[k])
        off += size
    return out


def kernel(x, ln_g, final_g, ssm_w_in, ssm_conv_w, ssm_conv_b, ssm_dt_bias, ssm_a_log, ssm_d, ssm_norm_g, ssm_w_out, pool_w_in, pool_w_group, pool_scale, pool_w_out, loss_target, m_ln_g, m_final_g, m_ssm_w_in, m_ssm_conv_w, m_ssm_conv_b, m_ssm_dt_bias, m_ssm_a_log, m_ssm_d, m_ssm_norm_g, m_ssm_w_out, m_pool_w_in, m_pool_w_group, m_pool_scale, m_pool_w_out, v_ln_g, v_final_g, v_ssm_w_in, v_ssm_conv_w, v_ssm_conv_b, v_ssm_dt_bias, v_ssm_a_log, v_ssm_d, v_ssm_norm_g, v_ssm_w_out, v_pool_w_in, v_pool_w_group, v_pool_scale, v_pool_w_out):
    l, d = x.shape[1], x.shape[2]
    me = 4 * lax.axis_index("x") + 2 * lax.axis_index("y") + lax.axis_index("c")
    ng, gds, gd = pool_w_group.shape[1], pool_w_group.shape[2], pool_w_group.shape[3]
    sin_s = ssm_w_in.shape[2]
    dp_s = pool_w_out.shape[1]
    conv_s = ssm_conv_w.shape[2]

    wt_in_s = _pack_rows("pack_w_in", jnp.transpose(ssm_w_in[0]))
    w_out_s = _cast_bf16("cast_w_out", ssm_w_out[0])
    wp_in_s = _cast_bf16("cast_pool_w_in", pool_w_in[0])
    wg_s = _cast_bf16("cast_pool_w_group", pool_w_group[0].reshape(ng * gds, gd))
    wp_out_s = _cast_bf16("cast_pool_w_out", pool_w_out[0])
    small_s = jnp.concatenate([ssm_conv_w[0].reshape(-1), pool_scale[0]])
    n_small = small_s.shape[0]
    small_s = jnp.pad(small_s, (0, -(-n_small // 1024) * 1024 - n_small)).reshape(-1, 128)
    wt_in_g, small_g = _exchange_alone("all_gather_w_in", _Gather([wt_in_s, small_s]))
    wt_in = _unpack_rows("unpack_w_in", wt_in_g.reshape(N_DEV * sin_s, d // 2))
    small_all = small_g.reshape(N_DEV, -1)[:, :n_small]
    conv_w = jnp.transpose(small_all[:, :CONV_TAPS * conv_s].reshape(N_DEV, CONV_TAPS, conv_s), (1, 0, 2))
    conv_w = conv_w.reshape(CONV_TAPS, -1)
    scale = small_all[:, CONV_TAPS * conv_s:].reshape(1, -1)

    def rest_of(arrived):
        w_out_g, wg_g, wp_out_g = arrived
        wg = jnp.transpose(wg_g.reshape(N_DEV, ng, gds, gd), (1, 0, 2, 3)).reshape(ng, gd, gd)
        return w_out_g.reshape(-1, d), wg, wp_out_g.reshape(-1, d)

    def rows_major(gp):
        q, _, hw = gp.shape
        return jnp.transpose(gp.reshape(q, N_DEV, -1, hw), (1, 0, 2, 3))

    def scatter_of(big):
        return _Scatter([rows_major(big["w_out"]), big["wp_in"][:, None],
                         jnp.transpose(big["wg"].reshape(ng, N_DEV, gds, gd // 2), (1, 0, 2, 3)),
                         rows_major(big["wp_out"])])

    def send_w_in(gwt_in):
        *sent, token = _scatter_start("scatter_w_in_start", rows_major(gwt_in))
        return sent, token[0, 0]

    later = _Later(_Gather([wp_in_s]), lambda arrived: arrived[0], _Gather([w_out_s, wg_s, wp_out_s]), rest_of,
                   scatter_of, send_w_in)

    loss, grad_x, small, big, recv, sent_w_in = _local_step(
        x[0], loss_target[0], ln_g, final_g.reshape(1, d), wt_in, conv_w, ssm_conv_b, ssm_dt_bias, ssm_a_log, ssm_d,
        ssm_norm_g, scale, later)

    def reduced(name, rv):
        q = rv.shape[1]
        cols = [_reduce_packed(f"{name}_{j}", rv[:, j]) for j in range(q)]
        return cols[0] if q == 1 else jnp.concatenate(cols, axis=1)

    g_w_out = reduced("reduce_w_out", recv[0])[None]
    g_wp_in = reduced("reduce_pool_w_in", recv[1])[None]
    g_wg = _reduce_packed("reduce_pool_w_group", recv[2].reshape(N_DEV, ng * gds, gd // 2)).reshape(1, ng, gds, gd)
    g_wp_out = reduced("reduce_pool_w_out", recv[3])[None]

    small["loss"] = loss
    shapes = {k: small[k].shape for k in SMALL_ORDER}
    gathered_small, = _exchange_alone("all_gather_small_grads", _Gather([_flatten_small(small)]))
    summed = _split_small(_sum_slots("sum_small_grads", gathered_small), shapes)
    g_conv_w = lax.dynamic_slice_in_dim(summed["conv_w"], me * conv_s, conv_s, axis=1)[None]
    g_scale = lax.dynamic_slice_in_dim(summed["scale"], me * dp_s, dp_s, axis=1)
    grads = dict(ln_g=summed["ln_g"], final_g=summed["final_g"].reshape(d), ssm_w_in=None, ssm_conv_w=g_conv_w,
                 ssm_conv_b=summed["conv_b"], ssm_dt_bias=summed["dt_bias"], ssm_a_log=summed["a_log"],
                 ssm_d=summed["d_skip"], ssm_norm_g=summed["norm_g"], ssm_w_out=g_w_out, pool_w_in=g_wp_in,
                 pool_w_group=g_wg, pool_scale=g_scale, pool_w_out=g_wp_out)

    weights = dict(ln_g=ln_g, final_g=final_g, ssm_w_in=ssm_w_in, ssm_conv_w=ssm_conv_w, ssm_conv_b=ssm_conv_b,
                   ssm_dt_bias=ssm_dt_bias, ssm_a_log=ssm_a_log, ssm_d=ssm_d, ssm_norm_g=ssm_norm_g,
                   ssm_w_out=ssm_w_out, pool_w_in=pool_w_in, pool_w_group=pool_w_group, pool_scale=pool_scale,
                   pool_w_out=pool_w_out)
    m_in = dict(ln_g=m_ln_g, final_g=m_final_g, ssm_w_in=m_ssm_w_in, ssm_conv_w=m_ssm_conv_w, ssm_conv_b=m_ssm_conv_b,
                ssm_dt_bias=m_ssm_dt_bias, ssm_a_log=m_ssm_a_log, ssm_d=m_ssm_d, ssm_norm_g=m_ssm_norm_g,
                ssm_w_out=m_ssm_w_out, pool_w_in=m_pool_w_in, pool_w_group=m_pool_w_group, pool_scale=m_pool_scale,
                pool_w_out=m_pool_w_out)
    v_in = dict(ln_g=v_ln_g, final_g=v_final_g, ssm_w_in=v_ssm_w_in, ssm_conv_w=v_ssm_conv_w, ssm_conv_b=v_ssm_conv_b,
                ssm_dt_bias=v_ssm_dt_bias, ssm_a_log=v_ssm_a_log, ssm_d=v_ssm_d, ssm_norm_g=v_ssm_norm_g,
                ssm_w_out=v_ssm_w_out, pool_w_in=v_pool_w_in, pool_w_group=v_pool_w_group, pool_scale=v_pool_scale,
                pool_w_out=v_pool_w_out)
    names = list(weights)
    big_names = ("ssm_w_out", "pool_w_in", "pool_w_group", "pool_w_out", "ssm_w_in")
    delta, new_m, new_v = {}, {}, {}

    def adamw_big(k):
        shp = weights[k].shape
        two_d = (-1, shp[-1])
        dk, mk, vk = _adamw(f"adamw_{k}", weights[k].reshape(two_d), grads[k].reshape(two_d), m_in[k].reshape(two_d),
                            v_in[k].reshape(two_d))
        delta[k], new_m[k], new_v[k] = dk.reshape(shp), mk.reshape(shp), vk.reshape(shp)

    for k in big_names[:-1]:
        adamw_big(k)
    small_names = [k for k in names if k not in big_names]

    def packed(tree):
        flat = jnp.concatenate([tree[k].reshape(-1) for k in small_names])
        nn = flat.shape[0]
        return jnp.pad(flat, (0, -(-nn // 1024) * 1024 - nn), constant_values=1.0).reshape(-1, 128)

    ds, ms, vs = _adamw("adamw_small", packed(weights), packed(grads), packed(m_in), packed(v_in))
    off = 0
    for k in small_names:
        shp = weights[k].shape
        size = int(np.prod(shp))
        for res, arr in ((delta, ds), (new_m, ms), (new_v, vs)):
            res[k] = arr.reshape(-1)[off:off + size].reshape(shp)
        off += size

    src_w_in, land_w_in = _scatter_wait("scatter_w_in_wait", *sent_w_in,
                                        after=(grad_x, ds, *[delta[k] for k in big_names[:-1]]))
    own = lax.dynamic_slice_in_dim(src_w_in, me, 1, axis=0)
    recv_w_in = lax.dynamic_update_slice_in_dim(land_w_in, own, me, axis=0)
    grads["ssm_w_in"] = jnp.transpose(reduced("reduce_w_in", recv_w_in))[None]
    adamw_big("ssm_w_in")

    return (summed["loss"].reshape(()), grad_x[None], *[grads[k] for k in names], *[delta[k] for k in names],
            *[new_m[k] for k in names], *[new_v[k] for k in names])
```

```python
import math

import jax
import jax.numpy as jnp
import numpy as np
from jax import lax
from jax.experimental import pallas as pl
from jax.experimental.pallas import tpu as pltpu

F32 = jnp.float32
BF = jnp.bfloat16
U32 = jnp.uint32

N_DEV = 8
EPS = 1e-6
SSD_CHUNK = 64
SSM_STATE = 128
CONV_TAPS = 4
HEAD_LANES = 128
POOL_WINDOWS = (2, 4, 8, 16)
ADAM_LR, ADAM_B1, ADAM_B2, ADAM_EPS, ADAM_WD, ADAM_STEP = 0.001, 0.9, 0.999, 1e-08, 0.01, 10
VMEM_LIMIT = 56 * 1024 * 1024
NEG_BIG = -1e30
HAND_ON_AT = 6

NN = ((1,), (0,))
NT = ((1,), (1,))
TN = ((0,), (0,))
MESH = pl.DeviceIdType.MESH


def _t(dim, pref):
    return pref if dim % pref == 0 else dim


def _row_tile(rows, pref, mult=8):
    best = rows
    for cand in range(mult, min(rows, pref) + 1, mult):
        if rows % cand == 0:
            best = cand
    return best


def _params(sem=None):
    return pltpu.CompilerParams(dimension_semantics=sem, vmem_limit_bytes=VMEM_LIMIT)


def _silu(x):
    return x * (1.0 / (1.0 + jnp.exp(-x)))


def _dsilu(x):
    s = 1.0 / (1.0 + jnp.exp(-x))
    return s * (1.0 + x * (1.0 - s))


def _bdot(a, b, dims=NN):
    return lax.dot_general(a.astype(BF), b.astype(BF), (dims, ((), ())), preferred_element_type=F32)


def _pack_pairs(x):
    h = x.shape[1] // 2
    hi = lax.bitcast_convert_type(x[:, :h].astype(jnp.bfloat16).astype(F32), U32)
    lo = lax.bitcast_convert_type(x[:, h:].astype(jnp.bfloat16).astype(F32), U32)
    return lax.bitcast_convert_type(hi | (lo >> 16), F32)


def _unpack_pairs(w):
    u = lax.bitcast_convert_type(w, U32)
    hi = lax.bitcast_convert_type(u & jnp.uint32(0xFFFF0000), F32)
    lo = lax.bitcast_convert_type(u << 16, F32)
    return hi, lo


def _mesh_pos():
    return lax.axis_index("x"), lax.axis_index("y"), lax.axis_index("c")


def _slot(pos):
    return 4 * pos[0] + 2 * pos[1] + pos[2]


class _Gather:
    def __init__(self, arrays):
        self.arrays = list(arrays)
        self.out_shapes = [jax.ShapeDtypeStruct((N_DEV, *s.shape), s.dtype) for s in arrays]

    def phases(self, src, dst, send_sems, recv_sems, local_sems):
        n_arr = len(self.arrays)
        x, y, c = _mesh_pos()
        me, sibling = (x, y, c), (x, y, 1 - c)
        chips = [(1 - x, y), (x, 1 - y), (1 - x, 1 - y)]

        def copy(a, k, block, to, from_src):
            return pltpu.make_async_remote_copy(
                src_ref=src[a] if from_src else dst[a].at[_slot(block)], dst_ref=dst[a].at[_slot(block)],
                send_sem=send_sems.at[a * 7 + k], recv_sem=recv_sems.at[a * 7 + k], device_id=to, device_id_type=MESH)

        def mine(a):
            return pltpu.make_async_copy(src[a], dst[a].at[_slot(me)], local_sems.at[a])

        def first(a):
            return [copy(a, 0, me, sibling, True)] + [copy(a, 1 + j, me, (*chip, c), True)
                                                     for j, chip in enumerate(chips)]

        def start():
            for a in range(n_arr):
                mine(a).start()
                for cp in first(a):
                    cp.start()

        def middle():
            for j, chip in enumerate(chips):
                for a in range(n_arr):
                    copy(a, 1 + j, (*chip, c), me, False).wait_recv()
                    copy(a, 4 + j, (*chip, c), sibling, False).start()

        def finish():
            for a in range(n_arr):
                copy(a, 0, sibling, me, False).wait_recv()
                for j, chip in enumerate(chips):
                    copy(a, 4 + j, (*chip, 1 - c), me, False).wait_recv()
                for cp in first(a):
                    cp.wait_send()
                for j, chip in enumerate(chips):
                    copy(a, 4 + j, (*chip, c), sibling, False).wait_send()
                mine(a).wait()

        return start, middle, finish


def _hosted_call(name, body, grid, in_specs, out_specs, out_shape, scratch_shapes, sem, operands, exch=None,
                 after=()):
    if after:
        n_lead = len(in_specs)
        inner = body

        def body(*refs):
            inner(*refs[:n_lead], *refs[n_lead + len(after):])

        in_specs = [*in_specs, *[pl.BlockSpec(memory_space=pl.ANY)] * len(after)]
        operands = (*operands, *after)
    if exch is None:
        return pl.pallas_call(body, grid=grid, in_specs=in_specs, out_specs=out_specs, out_shape=out_shape,
                              scratch_shapes=scratch_shapes, compiler_params=_params(sem), name=name)(*operands)
    n_in, n_out, n_scr, ne = len(in_specs), len(out_specs), len(scratch_shapes), len(exch.arrays)
    total = math.prod(grid)

    def wrapped(*refs):
        ins, ex_in = refs[:n_in], refs[n_in:n_in + ne]
        outs = refs[n_in + ne:n_in + ne + n_out]
        ex_out = refs[n_in + ne + n_out:n_in + 2 * ne + n_out]
        scr = refs[n_in + 2 * ne + n_out:n_in + 2 * ne + n_out + n_scr]
        step = 0
        for axis, size in enumerate(grid):
            step = step * size + pl.program_id(axis)
        start, middle, finish = exch.phases(ex_in, ex_out, *refs[-3:])
        pl.when(step == 0)(start)
        if middle is not None:
            pl.when(step == (total * HAND_ON_AT) // 8)(middle)
        body(*ins, *outs, *scr)
        pl.when(step == total - 1)(finish)

    hbm = pl.BlockSpec(memory_space=pl.ANY)
    sems = [pltpu.SemaphoreType.DMA((ne * 7,)), pltpu.SemaphoreType.DMA((ne * 7,)), pltpu.SemaphoreType.DMA((ne,))]
    return pl.pallas_call(
        wrapped, grid=grid, in_specs=[*in_specs, *[hbm] * ne], out_specs=[*out_specs, *[hbm] * ne],
        out_shape=[*out_shape, *exch.out_shapes], scratch_shapes=[*scratch_shapes, *sems],
        compiler_params=pltpu.CompilerParams(dimension_semantics=("arbitrary",) * len(grid),
                                             vmem_limit_bytes=VMEM_LIMIT, has_side_effects=True),
        name=name)(*operands, *exch.arrays)


def _flip_peers():
    x, y, c = _mesh_pos()
    peers = []
    for k in range(1, N_DEV):
        fx, fy, fc = (k >> 2) & 1, (k >> 1) & 1, k & 1
        peers.append((1 - x if fx else x, 1 - y if fy else y, 1 - c if fc else c))
    return (x, y, c), peers


def _split_scatter_copy(src, land, send_sems, recv_sems, k, me, peer, sending):
    return pltpu.make_async_remote_copy(
        src_ref=src.at[_slot(peer)], dst_ref=land.at[_slot(me) if sending else _slot(peer)],
        send_sem=send_sems.at[k], recv_sem=recv_sems.at[k], device_id=peer, device_id_type=MESH)


def _scatter_start(name, blocks):
    def body(src, land, send_sems, recv_sems, src_thru, land_thru, token):
        me, peers = _flip_peers()
        for k, peer in enumerate(peers):
            _split_scatter_copy(src, land, send_sems, recv_sems, k, me, peer, True).start()
        token[...] = jnp.zeros_like(token)

    hbm = pl.BlockSpec(memory_space=pltpu.HBM)
    sem = pl.BlockSpec(memory_space=pltpu.SEMAPHORE)
    return pl.pallas_call(
        body, name=name,
        out_shape=(pltpu.SemaphoreType.DMA((N_DEV - 1,)), pltpu.SemaphoreType.DMA((N_DEV - 1,)),
                   pltpu.HBM(blocks.shape, blocks.dtype), pltpu.HBM(blocks.shape, blocks.dtype),
                   jax.ShapeDtypeStruct((8, 128), F32)),
        in_specs=(hbm, hbm), out_specs=(sem, sem, hbm, hbm, pl.BlockSpec(memory_space=pltpu.VMEM)),
        input_output_aliases={0: 2, 1: 3},
        compiler_params=pltpu.CompilerParams(has_side_effects=pltpu.SideEffectType.DATAFLOW_SIDE_EFFECTING),
    )(pltpu.with_memory_space_constraint(blocks, pltpu.HBM),
      pltpu.with_memory_space_constraint(lax.empty(blocks.shape, blocks.dtype), pltpu.HBM))


def _scatter_wait(name, send_sems, recv_sems, src_thru, land_thru, after):
    def body(src, land, send_sems, recv_sems, *rest):
        me, peers = _flip_peers()
        for k, peer in enumerate(peers):
            _split_scatter_copy(src, land, send_sems, recv_sems, k, me, peer, True).wait_send()
            _split_scatter_copy(src, land, send_sems, recv_sems, k, me, peer, False).wait_recv()

    hbm = pl.BlockSpec(memory_space=pltpu.HBM)
    sem = pl.BlockSpec(memory_space=pltpu.SEMAPHORE)
    return pl.pallas_call(
        body, name=name,
        out_shape=(pltpu.HBM(src_thru.shape, src_thru.dtype), pltpu.HBM(land_thru.shape, land_thru.dtype)),
        in_specs=(hbm, hbm, sem, sem, *[pl.BlockSpec(memory_space=pl.ANY)] * len(after)), out_specs=(hbm, hbm),
        input_output_aliases={0: 0, 1: 1},
        compiler_params=pltpu.CompilerParams(has_side_effects=pltpu.SideEffectType.DATAFLOW_SIDE_EFFECTING),
    )(src_thru, land_thru, send_sems, recv_sems, *after)


def _exchange_alone(name, exch, after=()):
    def body():
        pass

    return _hosted_call(name, body, (1,), [], [], [], [], None, (), exch, after)


def _mm(name, grid, a, a_spec, b, b_spec, dims, outs, o_specs, acc_shape, extra=(), extra_specs=(), epi=None,
        exch=None, after=()):
    nk = grid[-1]
    n_extra, n_out = len(extra), len(outs)

    def body(*refs):
        a_ref, b_ref = refs[0], refs[1]
        ex = refs[2:2 + n_extra]
        o_refs = refs[2 + n_extra:2 + n_extra + n_out]

        def write(res):
            res = (res,) if epi is None else epi(res, *[e[...] for e in ex])
            for o, r in zip(o_refs, res):
                o[...] = r.astype(o.dtype)

        if nk == 1:
            write(_bdot(a_ref[...], b_ref[...], dims))
            return
        acc = refs[-1]
        k = pl.program_id(len(grid) - 1)

        @pl.when(k == 0)
        def _():
            acc[...] = _bdot(a_ref[...], b_ref[...], dims)

        @pl.when(jnp.logical_and(k > 0, k < nk - 1))
        def _():
            acc[...] += _bdot(a_ref[...], b_ref[...], dims)

        @pl.when(k == nk - 1)
        def _():
            write(acc[...] + _bdot(a_ref[...], b_ref[...], dims))

    sem = ("parallel",) * (len(grid) - 1) + ("arbitrary",)
    scratch = [] if nk == 1 else [pltpu.VMEM(acc_shape, F32)]
    return _hosted_call(name, body, grid, [a_spec, b_spec, *extra_specs], list(o_specs), list(outs),
                        scratch, sem, (a, b, *extra), exch, after)


def _add_epi(acc, add):
    return (acc + add,)


def _pack_epi(acc):
    return (_pack_pairs(acc),)


def _mm_nt(name, a, b, n_off, n, out_dtype=F32, tm=1024, tn=512, exch=None, after=()):
    m, kk = a.shape
    tm, tn = _t(m, tm), math.gcd(_t(n, tn), n_off)
    res = _mm(name, (m // tm, n // tn, 1),
              a, pl.BlockSpec((tm, kk), lambda i, j, k: (i, 0)),
              b, pl.BlockSpec((tn, kk), lambda i, j, k: (n_off // tn + j, 0)), NT,
              [jax.ShapeDtypeStruct((m, n), out_dtype)], [pl.BlockSpec((tm, tn), lambda i, j, k: (i, j))], (tm, tn),
              exch=exch, after=after)
    return res[0] if exch is None else res


def _mm_nn_blocked(name, a, b3, tm=1024, tn=512):
    m, kk = a.shape
    nb, _, cs = b3.shape
    tm, tn = _t(m, tm), _t(cs, tn)
    per = cs // tn
    return _mm(name, (m // tm, nb * per, 1),
               a, pl.BlockSpec((tm, kk), lambda i, j, k: (i, 0)),
               b3, pl.BlockSpec((None, kk, tn), lambda i, j, k: (j // per, 0, j % per)), NN,
               [jax.ShapeDtypeStruct((m, nb * cs), F32)], [pl.BlockSpec((tm, tn), lambda i, j, k: (i, j))],
               (tm, tn))[0]


def _mm_nt_blocked(name, a, b3, after=(), tm=1024, tn=1024):
    m = a.shape[0]
    nb, n, cs = b3.shape
    tm, tn = _t(m, tm), _t(n, tn)
    return _mm(name, (m // tm, n // tn, nb),
               a, pl.BlockSpec((tm, cs), lambda i, j, k: (i, k)),
               b3, pl.BlockSpec((None, tn, cs), lambda i, j, k: (k, j, 0)), NT,
               [jax.ShapeDtypeStruct((m, n), F32)], [pl.BlockSpec((tm, tn), lambda i, j, k: (i, j))], (tm, tn),
               after=after)[0]


def _mm_nn(name, a, b, k_off=0, kk=None, add=None, out_dtype=F32, tm=1024, tn=1024, tk=1024, after=()):
    m = a.shape[0]
    kk = a.shape[1] if kk is None else kk
    n = b.shape[1]
    tm, tn, tk = _t(m, tm), _t(n, tn), math.gcd(_t(kk, tk), k_off)
    extra, especs, epi = (), (), None
    if add is not None:
        extra, especs, epi = (add,), (pl.BlockSpec((tm, tn), lambda i, j, k: (i, j)),), _add_epi
    return _mm(name, (m // tm, n // tn, kk // tk),
               a, pl.BlockSpec((tm, tk), lambda i, j, k: (i, k)),
               b, pl.BlockSpec((tk, tn), lambda i, j, k: (k_off // tk + k, j)), NN,
               [jax.ShapeDtypeStruct((m, n), out_dtype)], [pl.BlockSpec((tm, tn), lambda i, j, k: (i, j))], (tm, tn),
               extra, especs, epi, after=after)[0]


TN_ACC_ELEMENTS = 1 << 20


def _mm_tn_packed(name, a, b, pw, tk=1024):
    kk, m = a.shape
    n = b.shape[1]
    tm, tk = _t(m, TN_ACC_ELEMENTS // pw), _t(kk, tk)
    return _mm(name, (m // tm, n // pw, kk // tk),
               a, pl.BlockSpec((tk, tm), lambda i, j, k: (k, i)),
               b, pl.BlockSpec((tk, pw), lambda i, j, k: (k, j)), TN,
               [jax.ShapeDtypeStruct((n // pw, m, pw // 2), F32)],
               [pl.BlockSpec((None, tm, pw // 2), lambda i, j, k: (j, i, 0))], (tm, pw), epi=_pack_epi)[0]


def _rms_fwd(name, x, g, tm=256):
    l, d = x.shape
    tm = _t(l, tm)

    def body(x_ref, g_ref, h_ref, r_ref):
        xv = x_ref[...]
        r = lax.rsqrt(jnp.mean(xv * xv, axis=-1, keepdims=True) + EPS)
        h_ref[...] = (xv * r * g_ref[...]).astype(BF)
        r_ref[...] = r

    return pl.pallas_call(
        body, grid=(l // tm,),
        in_specs=[pl.BlockSpec((tm, d), lambda i: (i, 0)), pl.BlockSpec((1, d), lambda i: (0, 0))],
        out_specs=[pl.BlockSpec((tm, d), lambda i: (i, 0)), pl.BlockSpec((tm, 1), lambda i: (i, 0))],
        out_shape=[jax.ShapeDtypeStruct((l, d), BF), jax.ShapeDtypeStruct((l, 1), F32)],
        compiler_params=_params(("parallel",)), name=name)(x, g)


def _rms_bwd(name, dh, x, r, g, dres, tm=256):
    l, d = x.shape
    tm = _t(l, tm)

    def body(dh_ref, x_ref, r_ref, g_ref, dres_ref, dx_ref, dxb_ref, dg_ref):
        i = pl.program_id(0)
        rr = r_ref[...]
        xhat = x_ref[...] * rr
        dhv = dh_ref[...]
        dxh = dhv * g_ref[...]
        dx = dres_ref[...] + rr * (dxh - xhat * jnp.mean(dxh * xhat, axis=-1, keepdims=True))
        dx_ref[...] = dx
        dxb_ref[...] = dx.astype(BF)

        @pl.when(i == 0)
        def _():
            dg_ref[...] = jnp.zeros_like(dg_ref)

        dg_ref[...] += jnp.sum(dhv * xhat, axis=0, keepdims=True)

    row = pl.BlockSpec((tm, d), lambda i: (i, 0))
    vec = pl.BlockSpec((1, d), lambda i: (0, 0))
    return pl.pallas_call(
        body, grid=(l // tm,), in_specs=[row, row, pl.BlockSpec((tm, 1), lambda i: (i, 0)), vec, row],
        out_specs=[row, row, vec],
        out_shape=[jax.ShapeDtypeStruct((l, d), F32), jax.ShapeDtypeStruct((l, d), BF),
                   jax.ShapeDtypeStruct((1, d), F32)],
        compiler_params=_params(("arbitrary",)), name=name)(dh, x, r, g, dres)


def _loss_head(x, g, target, tm=256):
    l, d = x.shape
    tm = _t(l, tm)

    def body(x_ref, g_ref, t_ref, dx_ref, dxb_ref, dg_ref, loss_ref):
        i = pl.program_id(0)
        xv = x_ref[...]
        gv = g_ref[...]
        r = lax.rsqrt(jnp.mean(xv * xv, axis=-1, keepdims=True) + EPS)
        xhat = xv * r
        e = xhat * gv - t_ref[...]
        dy = e * (1.0 / d)
        dxh = dy * gv
        dx = r * (dxh - xhat * jnp.mean(dxh * xhat, axis=-1, keepdims=True))
        dx_ref[...] = dx
        dxb_ref[...] = dx.astype(BF)

        @pl.when(i == 0)
        def _():
            dg_ref[...] = jnp.zeros_like(dg_ref)
            loss_ref[...] = jnp.zeros_like(loss_ref)

        dg_ref[...] += jnp.sum(dy * xhat, axis=0, keepdims=True)
        loss_ref[...] += 0.5 * jnp.sum(jnp.sum(e * e, axis=-1, keepdims=True) * (1.0 / d), axis=0, keepdims=True)

    row = pl.BlockSpec((tm, d), lambda i: (i, 0))
    vec = pl.BlockSpec((1, d), lambda i: (0, 0))
    return pl.pallas_call(
        body, grid=(l // tm,), in_specs=[row, vec, row],
        out_specs=[row, row, vec, pl.BlockSpec((1, 1), lambda i: (0, 0))],
        out_shape=[jax.ShapeDtypeStruct((l, d), F32), jax.ShapeDtypeStruct((l, d), BF),
                   jax.ShapeDtypeStruct((1, d), F32), jax.ShapeDtypeStruct((1, 1), F32)],
        compiler_params=_params(("arbitrary",)), name="loss_head")(x, g, target)


CONV_HALO = 8


def _conv_pre(x_ref, w_ref, b_ref, i, tr):
    r0 = pl.multiple_of(i * tr, tr)
    cur = x_ref[pl.ds(r0, tr), :]
    prev = x_ref[pl.ds(pl.multiple_of(jnp.maximum(r0 - CONV_HALO, 0), CONV_HALO), CONV_HALO), :]
    prev = jnp.where(i > 0, prev, 0.0)
    ext = jnp.concatenate([prev, cur], axis=0)
    taps = []
    for k in range(CONV_TAPS):
        s = CONV_TAPS - 1 - k
        taps.append(cur if s == 0 else pltpu.roll(ext, s, 0)[CONV_HALO:])
    pre = b_ref[...] + sum(w_ref[k:k + 1, :] * taps[k] for k in range(CONV_TAPS))
    return r0, pre, taps


def _conv_fwd(pm, col_off, conv_w, conv_b, cw=256, tr=512):
    l = pm.shape[0]
    c = conv_w.shape[1]
    cw, tr = _t(c, cw), _t(l, tr)
    assert col_off % cw == 0

    def body(x_ref, w_ref, b_ref, o_ref):
        def step(i, carry):
            r0, pre, _ = _conv_pre(x_ref, w_ref, b_ref, i, tr)
            o_ref[pl.ds(r0, tr), :] = _silu(pre)
            return carry
        lax.fori_loop(0, l // tr, step, 0)

    return pl.pallas_call(
        body, grid=(c // cw,),
        in_specs=[pl.BlockSpec((l, cw), lambda j: (0, col_off // cw + j)),
                  pl.BlockSpec((CONV_TAPS, cw), lambda j: (0, j)), pl.BlockSpec((1, cw), lambda j: (0, j))],
        out_specs=pl.BlockSpec((l, cw), lambda j: (0, j)), out_shape=jax.ShapeDtypeStruct((l, c), F32),
        compiler_params=_params(("parallel",)), name="conv_fwd")(pm, conv_w, conv_b)


def _conv_bwd(pm, col_off, conv_w, conv_b, dy, dproj, cw=256, tr=512):
    l = pm.shape[0]
    c = conv_w.shape[1]
    cw, tr = _t(c, cw), _t(l, tr)
    nt = l // tr

    def body(x_ref, w_ref, b_ref, dy_ref, _, dx_ref, dw_ref, db_ref, dpre_ref):
        def step1(i, carry):
            dws, db = carry
            r0, pre, taps = _conv_pre(x_ref, w_ref, b_ref, i, tr)
            dpre = dy_ref[pl.ds(r0, tr), :] * _dsilu(pre)
            dpre_ref[pl.ds(r0, tr), :] = dpre
            dws = tuple(dws[k] + jnp.sum(dpre * taps[k], axis=0, keepdims=True) for k in range(CONV_TAPS))
            return dws, db + jnp.sum(dpre, axis=0, keepdims=True)

        z = jnp.zeros((1, cw), F32)
        dws, db = lax.fori_loop(0, nt, step1, ((z,) * CONV_TAPS, z))
        for k in range(CONV_TAPS):
            dw_ref[k:k + 1, :] = dws[k]
        db_ref[...] = db

        def step2(i, carry):
            r0 = pl.multiple_of(i * tr, tr)
            cur = dpre_ref[pl.ds(r0, tr), :]
            nxt = dpre_ref[pl.ds(pl.multiple_of(jnp.minimum(r0 + tr, l - CONV_HALO), CONV_HALO), CONV_HALO), :]
            nxt = jnp.where(i < nt - 1, nxt, 0.0)
            ext = jnp.concatenate([cur, nxt], axis=0)
            acc = w_ref[CONV_TAPS - 1:CONV_TAPS, :] * cur
            for k in range(CONV_TAPS - 1):
                s = CONV_TAPS - 1 - k
                acc = acc + w_ref[k:k + 1, :] * pltpu.roll(ext, tr + CONV_HALO - s, 0)[:tr]
            dx_ref[pl.ds(r0, tr), :] = acc.astype(dx_ref.dtype)
            return carry
        lax.fori_loop(0, nt, step2, 0)

    col = pl.BlockSpec((l, cw), lambda j: (0, j))
    shifted = pl.BlockSpec((l, cw), lambda j: (0, col_off // cw + j))
    return pl.pallas_call(
        body, grid=(c // cw,),
        in_specs=[shifted, pl.BlockSpec((CONV_TAPS, cw), lambda j: (0, j)), pl.BlockSpec((1, cw), lambda j: (0, j)),
                  col, pl.BlockSpec(memory_space=pl.ANY)],
        out_specs=[shifted, pl.BlockSpec((CONV_TAPS, cw), lambda j: (0, j)), pl.BlockSpec((1, cw), lambda j: (0, j))],
        out_shape=[jax.ShapeDtypeStruct(dproj.shape, dproj.dtype), jax.ShapeDtypeStruct((CONV_TAPS, c), F32),
                   jax.ShapeDtypeStruct((1, c), F32)],
        scratch_shapes=[pltpu.VMEM((l, cw), F32)], input_output_aliases={4: 0},
        compiler_params=_params(("parallel",)), name="conv_bwd")(pm, conv_w, conv_b, dy, dproj)


def _split(x, pieces):
    out, rest = [], x
    for _ in range(pieces):
        piece = rest.astype(BF)
        out.append(piece)
        rest = rest - piece.astype(F32)
    return out


def _rows_times(xs, m_stack, pieces):
    x = xs[0] if len(xs) == 1 else jnp.concatenate(xs, axis=0)
    out = _bdot(jnp.concatenate(_split(x, pieces), axis=1), m_stack)
    sizes = [v.shape[0] for v in xs]
    offs = np.cumsum([0] + sizes)
    return [out[offs[i]:offs[i + 1]] for i in range(len(xs))]


def _times_rows(m_stack, x, pieces):
    return _bdot(m_stack, jnp.concatenate(_split(x, pieces), axis=0))


EXPAND_PIECES = 3
FOLD_PIECES = 2


def _ssd_consts(r, p, t):
    assert p == t, "heads expand to P lanes of the inputs and to T lanes of the decay matrices alike"
    rp = r * p
    tri = np.tril(np.ones((t, t), np.float32))
    ep = np.zeros((HEAD_LANES, rp), np.float32)
    ep[np.arange(rp) // p, np.arange(rp)] = 1.0
    itile = (np.arange(t)[:, None] == (np.arange(rp) % t)[None, :]).astype(np.float32)
    lmask = (np.arange(t)[:, None] >= (np.arange(rp) % t)[None, :]).astype(np.float32)
    bmask = ((np.arange(rp) // t)[:, None] == (np.arange(rp) // p)[None, :]).astype(np.float32)
    return [jnp.asarray(np.concatenate([ep] * EXPAND_PIECES, axis=0), BF),
            jnp.asarray(np.concatenate([ep.T] * FOLD_PIECES, axis=0), BF),
            jnp.asarray(np.concatenate([tri] * EXPAND_PIECES, axis=1), BF),
            jnp.asarray(np.concatenate([tri.T] * EXPAND_PIECES, axis=1), BF),
            jnp.asarray(ep.T.copy()), jnp.asarray(itile), jnp.asarray(lmask), jnp.asarray(bmask)]


def _ssd_common(xs, bm, pre_raw, dtb, alog, e_stack, tri_stack, itile, lmask, bmask, r, t):
    pre = pre_raw + dtb
    dt = jnp.maximum(pre, 0.0) + jnp.log(1.0 + jnp.exp(-jnp.abs(pre)))
    a = -jnp.exp(alog)
    cs = _times_rows(tri_stack, dt * a, EXPAND_PIECES)
    last = cs[t - 1:t, :]
    ecs, w = jnp.exp(cs), jnp.exp(last - cs)
    dtx, ecsx, wx, csx = _rows_times([dt, ecs, w, cs], e_stack, EXPAND_PIECES)
    csrow = jnp.sum(csx * itile, axis=0, keepdims=True)
    lx = jnp.exp(jnp.where(lmask > 0.0, csx - csrow, NEG_BIG))
    xdt = xs * dtx
    xblk = (jnp.concatenate([xdt] * r, axis=0) * bmask).astype(BF)
    btile = jnp.concatenate([bm] * r, axis=0).astype(BF)
    return pre, dt, a, cs, last, ecs, w, dtx, ecsx, wx, lx, xdt, xblk, btile


def _ssd_specs(l, g_n, r, p, n, t, conv, rev):
    nc = l // t
    rp = r * p
    cidx = (lambda c: nc - 1 - c) if rev else (lambda c: c)
    row_spec = lambda width: pl.BlockSpec((t, width), lambda c: (cidx(c), 0))
    dtr_spec = pl.BlockSpec((g_n, t, HEAD_LANES), lambda c: (0, cidx(c), 0))
    par_spec = pl.BlockSpec((g_n, 1, HEAD_LANES), lambda c: (0, 0, 0))
    dskx_spec = pl.BlockSpec((g_n, 1, rp), lambda c: (0, 0, 0))
    st_spec = pl.BlockSpec((None, g_n, rp, n), lambda c: (cidx(c), 0, 0, 0))
    return nc, row_spec, dtr_spec, par_spec, dskx_spec, st_spec


def _const_specs(consts):
    return [pl.BlockSpec(a.shape, lambda c: (0, 0)) for a in consts]


def _ssd_fwd(xbc, dtr, dtb, alog, dskx, dims, exch=None):
    h, g_n, r, p, n, t = dims
    l, conv = xbc.shape
    rp, hp = r * p, h * p
    nc, row_spec, dtr_spec, par_spec, dskx_spec, st_spec = _ssd_specs(l, g_n, r, p, n, t, conv, False)
    consts = _ssd_consts(r, p, t)

    def body(x_ref, dtr_ref, dtb_ref, alog_ref, dskx_ref,
             e_ref, et_ref, tri_ref, trit_ref, ept_ref, it_ref, lm_ref, bmk_ref, y_ref, st_ref, s_ref):
        @pl.when(pl.program_id(0) == 0)
        def _():
            s_ref[...] = jnp.zeros_like(s_ref)

        for g in range(g_n):
            xs = x_ref[:, g * rp:(g + 1) * rp]
            bm = x_ref[:, hp + g * n:hp + (g + 1) * n]
            cm = x_ref[:, hp + (g_n + g) * n:hp + (g_n + g + 1) * n]
            (_, _, _, _, last, _, _, _, ecsx, wx, lx, xdt, xblk, btile) = _ssd_common(
                xs, bm, dtr_ref[g], dtb_ref[g], alog_ref[g], e_ref[...], tri_ref[...], it_ref[...], lm_ref[...],
                bmk_ref[...], r, t)
            s_in = s_ref[g]
            st_ref[g] = s_in
            cbx = _bdot(cm, btile, NT)
            yd = _bdot(cbx * lx, xblk)
            yo = ecsx * _bdot(cm, s_in, NT)
            y_ref[:, g * rp:(g + 1) * rp] = yd + yo + dskx_ref[g] * xs
            elast = jnp.sum(ept_ref[...] * jnp.exp(last), axis=1, keepdims=True)
            s_ref[g] = elast * s_in + _bdot(xdt * wx, bm, TN)

    return _hosted_call(
        "ssd_fwd", body, (nc,),
        [row_spec(conv), dtr_spec, par_spec, par_spec, dskx_spec, *_const_specs(consts)],
        [row_spec(hp), st_spec],
        [jax.ShapeDtypeStruct((l, hp), F32), jax.ShapeDtypeStruct((nc, g_n, rp, n), F32)],
        [pltpu.VMEM((g_n, rp, n), F32)], ("arbitrary",),
        (xbc, dtr, dtb, alog, dskx, *consts), exch)


def _ssd_bwd(xbc, dtr, dtb, alog, dskx, states, dy, dims, after=()):
    h, g_n, r, p, n, t = dims
    l, conv = xbc.shape
    rp, hp = r * p, h * p
    nc, row_spec, dtr_spec, par_spec, dskx_spec, st_spec = _ssd_specs(l, g_n, r, p, n, t, conv, True)
    consts = _ssd_consts(r, p, t)

    def fold_rows(v, rows):
        return sum(v[k * rows:(k + 1) * rows, :] for k in range(r))

    def body(x_ref, dtr_ref, dtb_ref, alog_ref, dskx_ref, st_ref, dy_ref,
             e_ref, et_ref, tri_ref, trit_ref, ept_ref, it_ref, lm_ref, bmk_ref,
             dx_ref, ddt_ref, dbias_ref, dalog_ref, dd_ref, ds_ref):
        @pl.when(pl.program_id(0) == 0)
        def _():
            ds_ref[...] = jnp.zeros_like(ds_ref)
            dbias_ref[...] = jnp.zeros_like(dbias_ref)
            dalog_ref[...] = jnp.zeros_like(dalog_ref)
            dd_ref[...] = jnp.zeros_like(dd_ref)

        ept, itile, bmask = ept_ref[...], it_ref[...], bmk_ref[...]
        last_row = lax.broadcasted_iota(jnp.int32, (t, HEAD_LANES), 0) == t - 1
        for g in range(g_n):
            xs = x_ref[:, g * rp:(g + 1) * rp]
            bm = x_ref[:, hp + g * n:hp + (g + 1) * n]
            cm = x_ref[:, hp + (g_n + g) * n:hp + (g_n + g + 1) * n]
            dy = dy_ref[:, g * rp:(g + 1) * rp]
            (pre, dt, a, cs, last, ecs, w, dtx, ecsx, wx, lx, xdt, xblk, btile) = _ssd_common(
                xs, bm, dtr_ref[g], dtb_ref[g], alog_ref[g], e_ref[...], tri_ref[...], itile, lm_ref[...], bmask, r, t)
            s_in = st_ref[g]
            ds_out = ds_ref[g]
            elast_row = jnp.exp(last)
            elast = jnp.sum(ept * elast_row, axis=1, keepdims=True)

            cbx = _bdot(cm, btile, NT)
            amat = cbx * lx
            da = _bdot(dy, xblk, NT)
            dxdt = fold_rows(_bdot(amat, dy, TN) * bmask, t)
            dcbx = da * lx
            q = da * amat
            dc = _bdot(dcbx, btile)
            db = fold_rows(_bdot(dcbx, cm, TN), t)

            g0 = _bdot(cm, s_in, NT)
            dg0 = dy * ecsx
            dc = dc + _bdot(dg0, s_in)

            z = _bdot(bm, ds_out, NT)
            dxdt = dxdt + z * wx
            db = db + _bdot(xdt * wx, ds_out)
            ds_ref[g] = elast * ds_out + _bdot(dg0, cm, TN)

            skip = jnp.sum(dy * xs, axis=0, keepdims=True)
            skip = jnp.concatenate([skip, jnp.zeros((15, rp), F32)], axis=0)
            f_q, f_yo, f_w, f_dt, f_skip = _rows_times(
                [q - itile * jnp.sum(q, axis=0, keepdims=True), dy * g0, z * xdt, dxdt * xs, skip],
                et_ref[...], FOLD_PIECES)
            dd_ref[g] += f_skip[0:1]
            dww = f_w * w
            dcs = f_q + f_yo * ecs - dww
            dlast = jnp.sum(jnp.sum(ds_out * s_in, axis=1, keepdims=True) * ept, axis=0, keepdims=True) * elast_row
            dlast = dlast + jnp.sum(dww, axis=0, keepdims=True)

            dcs = dcs + jnp.where(last_row, dlast, 0.0)
            dadt = _times_rows(trit_ref[...], dcs, EXPAND_PIECES)
            dalog_ref[g] += jnp.sum(dadt * dt, axis=0, keepdims=True) * a
            dpre = (dadt * a + f_dt) * (1.0 / (1.0 + jnp.exp(-pre)))
            ddt_ref[g] = dpre
            dbias_ref[g] += jnp.sum(dpre, axis=0, keepdims=True)
            dx_ref[:, g * rp:(g + 1) * rp] = dskx_ref[g] * dy + dxdt * dtx
            dx_ref[:, hp + g * n:hp + (g + 1) * n] = db
            dx_ref[:, hp + (g_n + g) * n:hp + (g_n + g + 1) * n] = dc

    par_shape = jax.ShapeDtypeStruct((g_n, 1, HEAD_LANES), F32)
    return _hosted_call(
        "ssd_bwd", body, (nc,),
        [row_spec(conv), dtr_spec, par_spec, par_spec, dskx_spec, st_spec, row_spec(hp), *_const_specs(consts)],
        [row_spec(conv), dtr_spec, par_spec, par_spec, par_spec],
        [jax.ShapeDtypeStruct((l, conv), F32), jax.ShapeDtypeStruct((g_n, l, HEAD_LANES), F32),
         par_shape, par_shape, par_shape],
        [pltpu.VMEM((g_n, rp, n), F32)], ("arbitrary",),
        (xbc, dtr, dtb, alog, dskx, states, dy, *consts), after=after)


def _gate_norm_fwd(y, pm, g, tm=128):
    l, di = y.shape
    tm = _t(l, tm)

    def body(y_ref, z_ref, g_ref, o_ref, r_ref):
        v = y_ref[...] * _silu(z_ref[...])
        rr = lax.rsqrt(jnp.mean(v * v, axis=-1, keepdims=True) + EPS)
        o_ref[...] = (v * rr * g_ref[...]).astype(BF)
        r_ref[...] = rr

    row = pl.BlockSpec((tm, di), lambda i: (i, 0))
    return pl.pallas_call(
        body, grid=(l // tm,), in_specs=[row, row, pl.BlockSpec((1, di), lambda i: (0, 0))],
        out_specs=[row, pl.BlockSpec((tm, 1), lambda i: (i, 0))],
        out_shape=[jax.ShapeDtypeStruct((l, di), BF), jax.ShapeDtypeStruct((l, 1), F32)],
        compiler_params=_params(("parallel",)), name="gate_norm_fwd")(y, pm, g)


def _gate_norm_bwd(dyn, y, pm, rstd, g, tm=128):
    l, di = y.shape
    tm = _t(l, tm)

    def body(dyn_ref, y_ref, z_ref, r_ref, g_ref, dy_ref, dz_ref, dg_ref):
        i = pl.program_id(0)
        yv, zv, rr = y_ref[...], z_ref[...], r_ref[...]
        sz = _silu(zv)
        vhat = yv * sz * rr
        dn = dyn_ref[...]
        dvh = dn * g_ref[...]
        dv = rr * (dvh - vhat * jnp.mean(dvh * vhat, axis=-1, keepdims=True))
        dy_ref[...] = dv * sz
        dz_ref[...] = (dv * yv * _dsilu(zv)).astype(BF)

        @pl.when(i == 0)
        def _():
            dg_ref[...] = jnp.zeros_like(dg_ref)

        dg_ref[...] += jnp.sum(dn * vhat, axis=0, keepdims=True)

    row = pl.BlockSpec((tm, di), lambda i: (i, 0))
    vec = pl.BlockSpec((1, di), lambda i: (0, 0))
    return pl.pallas_call(
        body, grid=(l // tm,), in_specs=[row, row, row, pl.BlockSpec((tm, 1), lambda i: (i, 0)), vec],
        out_specs=[row, row, vec],
        out_shape=[jax.ShapeDtypeStruct((l, di), F32), jax.ShapeDtypeStruct((l, pm.shape[1]), BF),
                   jax.ShapeDtypeStruct((1, di), F32)],
        compiler_params=_params(("arbitrary",)), name="gate_norm_bwd")(dyn, y, pm, rstd, g)


POOL_HALO = 16


def _pool_mix(name, src, dp, backward, out_dtype, into=None, cw=256, tr=512):
    l = src.shape[0]
    gd = dp // len(POOL_WINDOWS)
    cwl, trl = _t(gd, cw), _t(l, tr)
    nt = l // trl

    def body(x_ref, *rest):
        o_ref = rest[-1]
        gi = pl.program_id(0)
        for wi, win in enumerate(POOL_WINDOWS):
            @pl.when(gi == wi)
            def _(win=win):
                def step(i, carry):
                    r0 = pl.multiple_of(i * trl, trl)
                    cur = x_ref[pl.ds(r0, trl), :]
                    trow = r0 + lax.broadcasted_iota(jnp.int32, (trl, 1), 0)
                    cnt = jnp.minimum(trow + 1, win).astype(F32)
                    if not backward:
                        halo = x_ref[pl.ds(pl.multiple_of(jnp.maximum(r0 - POOL_HALO, 0), POOL_HALO), POOL_HALO), :]
                        halo = jnp.where(i > 0, halo, 0.0)
                        s = jnp.concatenate([halo, cur], axis=0)
                        sh = 1
                        while sh < win:
                            s = s + pltpu.roll(s, sh, 0)
                            sh *= 2
                        res = s[POOL_HALO:] / cnt - cur
                    else:
                        halo = x_ref[pl.ds(pl.multiple_of(jnp.minimum(r0 + trl, l - POOL_HALO), POOL_HALO),
                                           POOL_HALO), :]
                        hrow = r0 + trl + lax.broadcasted_iota(jnp.int32, (POOL_HALO, 1), 0)
                        hcnt = jnp.minimum(hrow + 1, win).astype(F32)
                        halo = jnp.where(i < nt - 1, halo / hcnt, 0.0)
                        s = jnp.concatenate([cur / cnt, halo], axis=0)
                        sh = 1
                        while sh < win:
                            s = s + pltpu.roll(s, trl + POOL_HALO - sh, 0)
                            sh *= 2
                        res = s[:trl] - cur
                    o_ref[pl.ds(r0, trl), :] = res.astype(o_ref.dtype)
                    return carry
                lax.fori_loop(0, nt, step, 0)

    col = pl.BlockSpec((l, cwl), lambda g, j: (0, g * (gd // cwl) + j))
    if into is None:
        return pl.pallas_call(
            body, grid=(len(POOL_WINDOWS), gd // cwl), in_specs=[col], out_specs=col,
            out_shape=jax.ShapeDtypeStruct((l, dp), out_dtype),
            compiler_params=_params(("parallel", "parallel")), name=name)(src)
    return pl.pallas_call(
        body, grid=(len(POOL_WINDOWS), gd // cwl), in_specs=[col, pl.BlockSpec(memory_space=pl.ANY)], out_specs=col,
        out_shape=jax.ShapeDtypeStruct(into.shape, into.dtype), input_output_aliases={1: 0},
        compiler_params=_params(("parallel", "parallel")), name=name)(src, into)


def _pool_gate_bwd(dyp, mg, pu, scale, tm=128):
    l, dp = mg.shape
    tm = _t(l, tm)

    def body(dyp_ref, mg_ref, gate_ref, sc_ref, dmg_ref, dgate_ref, dsc_ref):
        i = pl.program_id(0)
        d, m, gt, sc = dyp_ref[...], mg_ref[...], gate_ref[...], sc_ref[...]
        sg = _silu(gt)
        dmg_ref[...] = (d * sc * sg).astype(BF)
        dgate_ref[...] = (d * m * sc * _dsilu(gt)).astype(BF)

        @pl.when(i == 0)
        def _():
            dsc_ref[...] = jnp.zeros_like(dsc_ref)

        dsc_ref[...] += jnp.sum(d * m * sg, axis=0, keepdims=True)

    row = pl.BlockSpec((tm, dp), lambda i: (i, 0))
    right = pl.BlockSpec((tm, dp), lambda i: (i, 1))
    vec = pl.BlockSpec((1, dp), lambda i: (0, 0))
    return pl.pallas_call(
        body, grid=(l // tm,), in_specs=[row, row, right, vec], out_specs=[row, right, vec],
        out_shape=[jax.ShapeDtypeStruct((l, dp), BF), jax.ShapeDtypeStruct((l, 2 * dp), BF),
                   jax.ShapeDtypeStruct((1, dp), F32)],
        compiler_params=_params(("arbitrary",)), name="pool_gate_bwd")(dyp, mg, pu, scale)


def _group_mm_fwd(mp, wg, pu, scale, tm=1024, tn=1024, tk=1024):
    l, dp = mp.shape
    ng, gd = wg.shape[0], wg.shape[1]
    tm, tn, tk = _t(l, tm), _t(gd, tn), _t(gd, tk)

    def epi(acc, gate, sc):
        return acc, acc * sc * _silu(gate)

    out = pl.BlockSpec((tm, tn), lambda g, i, j, k: (i, g * (gd // tn) + j))
    return _mm("group_mm_fwd", (ng, l // tm, gd // tn, gd // tk),
               mp, pl.BlockSpec((tm, tk), lambda g, i, j, k: (i, g * (gd // tk) + k)),
               wg, pl.BlockSpec((None, tk, tn), lambda g, i, j, k: (g, k, j)), NN,
               [jax.ShapeDtypeStruct((l, dp), F32), jax.ShapeDtypeStruct((l, dp), BF)], [out, out], (tm, tn),
               (pu, scale),
               (pl.BlockSpec((tm, tn), lambda g, i, j, k: (i, (dp + g * gd) // tn + j)),
                pl.BlockSpec((1, tn), lambda g, i, j, k: (0, g * (gd // tn) + j))), epi)


def _group_mm_bwd_data(dmg, wg, after=(), tm=1024, tn=1024, tk=1024):
    l, dp = dmg.shape
    ng, gd = wg.shape[0], wg.shape[1]
    tm, tn, tk = _t(l, tm), _t(gd, tn), _t(gd, tk)
    return _mm("group_mm_bwd_data", (ng, l // tm, gd // tn, gd // tk),
               dmg, pl.BlockSpec((tm, tk), lambda g, i, j, k: (i, g * (gd // tk) + k)),
               wg, pl.BlockSpec((None, tn, tk), lambda g, i, j, k: (g, j, k)), NT,
               [jax.ShapeDtypeStruct((l, dp), F32)],
               [pl.BlockSpec((tm, tn), lambda g, i, j, k: (i, g * (gd // tn) + j))], (tm, tn), after=after)[0]


def _group_mm_bwd_weight(mp, dmg, ng, tm=512, tk=1024):
    l, dp = mp.shape
    gd = dp // ng
    tm, tk = _t(gd, tm), _t(l, tk)
    return _mm("group_mm_bwd_weight", (ng, gd // tm, l // tk),
               mp, pl.BlockSpec((tk, tm), lambda g, i, k: (k, g * (gd // tm) + i)),
               dmg, pl.BlockSpec((tk, gd), lambda g, i, k: (k, g)), TN,
               [jax.ShapeDtypeStruct((ng, gd, gd // 2), F32)],
               [pl.BlockSpec((None, tm, gd // 2), lambda g, i, k: (g, i, 0))], (tm, gd), epi=_pack_epi)[0]


def _cast_bf16(name, w, tr=256):
    r, c = w.shape
    tr = _row_tile(r, tr, 16)

    def body(w_ref, o_ref):
        o_ref[...] = w_ref[...].astype(BF)

    blk = pl.BlockSpec((tr, c), lambda i: (i, 0))
    return pl.pallas_call(body, grid=(r // tr,), in_specs=[blk], out_specs=blk,
                          out_shape=jax.ShapeDtypeStruct((r, c), BF), compiler_params=_params(("parallel",)),
                          name=name)(w)


def _pack_rows(name, w, tr=256):
    r, c = w.shape
    tr = _row_tile(r, tr)

    def body(w_ref, o_ref):
        o_ref[...] = _pack_pairs(w_ref[...])

    return pl.pallas_call(body, grid=(r // tr,), in_specs=[pl.BlockSpec((tr, c), lambda i: (i, 0))],
                          out_specs=pl.BlockSpec((tr, c // 2), lambda i: (i, 0)),
                          out_shape=jax.ShapeDtypeStruct((r, c // 2), F32), compiler_params=_params(("parallel",)),
                          name=name)(w)


def _unpack_rows(name, w, tr=512):
    r, h = w.shape
    tr = _row_tile(r, tr, 16)

    def body(w_ref, o_ref):
        hi, lo = _unpack_pairs(w_ref[...])
        o_ref[:, :h] = hi.astype(BF)
        o_ref[:, h:] = lo.astype(BF)

    return pl.pallas_call(body, grid=(r // tr,), in_specs=[pl.BlockSpec((tr, h), lambda i: (i, 0))],
                          out_specs=pl.BlockSpec((tr, 2 * h), lambda i: (i, 0)),
                          out_shape=jax.ShapeDtypeStruct((r, 2 * h), BF), compiler_params=_params(("parallel",)),
                          name=name)(w)


def _reduce_packed(name, recv, tr=256):
    nd, r, h = recv.shape
    tr = _row_tile(r, tr)

    def body(p_ref, o_ref):
        hi, lo = _unpack_pairs(p_ref[0])
        for k in range(1, nd):
            a, b = _unpack_pairs(p_ref[k])
            hi, lo = hi + a, lo + b
        o_ref[:, :h] = hi
        o_ref[:, h:] = lo

    return pl.pallas_call(body, grid=(r // tr,), in_specs=[pl.BlockSpec((nd, tr, h), lambda i: (0, i, 0))],
                          out_specs=pl.BlockSpec((tr, 2 * h), lambda i: (i, 0)),
                          out_shape=jax.ShapeDtypeStruct((r, 2 * h), F32), compiler_params=_params(("parallel",)),
                          name=name)(recv)


def _adamw_math(w, g, m, v):
    m2 = ADAM_B1 * m + (1.0 - ADAM_B1) * g
    v2 = ADAM_B2 * v + (1.0 - ADAM_B2) * (g * g)
    m_hat = m2 / (1.0 - ADAM_B1 ** ADAM_STEP)
    v_hat = v2 / (1.0 - ADAM_B2 ** ADAM_STEP)
    delta = -ADAM_LR * (m_hat / (jnp.sqrt(v_hat) + ADAM_EPS) + ADAM_WD * w)
    return delta, m2, v2


def _adamw(name, w, g, m, v, tr=256):
    r, c = w.shape
    tr = _row_tile(r, tr)

    def body(w_ref, g_ref, m_ref, v_ref, d_ref, m2_ref, v2_ref):
        d, m2, v2 = _adamw_math(w_ref[...], g_ref[...], m_ref[...], v_ref[...])
        d_ref[...] = d
        m2_ref[...] = m2
        v2_ref[...] = v2

    blk = pl.BlockSpec((tr, c), lambda i: (i, 0))
    shp = jax.ShapeDtypeStruct((r, c), F32)
    return pl.pallas_call(body, grid=(r // tr,), in_specs=[blk] * 4, out_specs=[blk] * 3, out_shape=[shp] * 3,
                          compiler_params=_params(("parallel",)), name=name)(w, g, m, v)


def _sum_slots(name, a):
    nd, r, c = a.shape

    def body(a_ref, o_ref):
        s = a_ref[0]
        for k in range(1, nd):
            s = s + a_ref[k]
        o_ref[...] = s

    return pl.pallas_call(body, out_shape=jax.ShapeDtypeStruct((r, c), F32), name=name)(a)


def _head_rows(v, g_n, r):
    return jnp.pad(v.reshape(g_n, 1, r), ((0, 0), (0, 0), (0, HEAD_LANES - r)))


class _Later:
    def __init__(self, gather_pool_in, pool_in_of, gather_rest, rest_of, send):
        self.gather_pool_in, self.pool_in_of = gather_pool_in, pool_in_of
        self.gather_rest, self.rest_of, self.send = gather_rest, rest_of, send


def _local_step(x, target, ln_g, final_g, wt_in, conv_w, conv_b, dt_bias, a_log, d_skip, norm_g, scale, later):
    l, d = x.shape
    di = norm_g.shape[1]
    h = dt_bias.shape[1]
    p = di // h
    conv = conv_b.shape[1]
    n, t = SSM_STATE, SSD_CHUNK
    g_n = (conv - di) // (2 * n)
    r = h // g_n
    nm = di + conv
    dp = scale.shape[1]
    ng = len(POOL_WINDOWS)
    dims = (h, g_n, r, p, n, t)
    pw = d

    h0, r0 = _rms_fwd("rms0_fwd", x, ln_g[0:1])
    pm = _mm_nt("in_proj_main", h0, wt_in, 0, nm, exch=later.gather_pool_in)
    if later.gather_pool_in is not None:
        pm, *arrived = pm
    else:
        arrived = None
    wp_in = later.pool_in_of(arrived)
    dtr = _mm_nt("in_proj_dt", h0, wt_in, nm, h, tn=h)
    xbc = _conv_fwd(pm, di, conv_w, conv_b)
    dtb, alog = _head_rows(dt_bias, g_n, r), _head_rows(a_log, g_n, r)
    dskx = jnp.repeat(d_skip.reshape(g_n, 1, r), p, axis=2)
    dtr_g = jnp.pad(jnp.transpose(dtr.reshape(l, g_n, r), (1, 0, 2)), ((0, 0), (0, 0), (0, HEAD_LANES - r)))
    y, states, *arrived = _ssd_fwd(xbc, dtr_g, dtb, alog, dskx, dims, exch=later.gather_rest)
    w_out, wg, wp_out = later.rest_of(arrived)
    yn, r_n = _gate_norm_fwd(y, pm, norm_g)
    x1 = _mm_nn("ssm_out_proj", yn, w_out, add=x)

    h1, r1 = _rms_fwd("rms1_fwd", x1, ln_g[1:2])
    pu = _mm_nn_blocked("pool_in_proj", h1, wp_in)
    mp = _pool_mix("pool_mix_fwd", pu, dp, False, BF)
    mg, yp = _group_mm_fwd(mp, wg, pu, scale)
    x2 = _mm_nn("pool_out_proj", yp, wp_out, add=x1)

    dx2, dx2_b, d_final_g, loss = _loss_head(x2, final_g, target)

    dyp = _mm_nt("pool_out_bwd_data", dx2_b, wp_out, 0, dp, tn=1024)
    gw_pout = _mm_tn_packed("pool_out_bwd_weight", yp, dx2_b, pw)
    going = later.send("pool_w_out", gw_pout)
    dmg, dpu, d_scale = _pool_gate_bwd(dyp, mg, pu, scale)
    dmp = _group_mm_bwd_data(dmg, wg, after=going)
    gw_g = _group_mm_bwd_weight(mp, dmg, ng)
    going = later.send("pool_w_group", gw_g)
    dpu = _pool_mix("pool_mix_bwd", dmp, dp, True, BF, into=dpu)
    dh1 = _mm_nt_blocked("pool_in_bwd_data", dpu, wp_in, after=going)
    gw_pin = _mm_tn_packed("pool_in_bwd_weight", h1, dpu, 2 * dp // N_DEV)
    going = later.send("pool_w_in", gw_pin)
    dx1, dx1_b, d_ln1 = _rms_bwd("rms1_bwd", dh1, x1, r1, ln_g[1:2], dx2)

    dyn = _mm_nt("ssm_out_bwd_data", dx1_b, w_out, 0, di, tn=1024, after=going)
    gw_out = _mm_tn_packed("ssm_out_bwd_weight", yn, dx1_b, pw)
    going = later.send("ssm_w_out", gw_out)
    dy, dproj, d_norm_g = _gate_norm_bwd(dyn, y, pm, r_n, norm_g)
    dxbc, ddt_g, dbias_g, dalog_g, dd_g = _ssd_bwd(xbc, dtr_g, dtb, alog, dskx, states, dy, dims, after=going)
    dproj, d_conv_w, d_conv_b = _conv_bwd(pm, di, conv_w, conv_b, dxbc, dproj)
    ddt = jnp.transpose(ddt_g[:, :, :r], (1, 0, 2)).reshape(l, h)
    gwt_in = jnp.concatenate([_mm_tn_packed("in_proj_bwd_weight", dproj, h0, pw),
                              _mm_tn_packed("in_proj_bwd_weight_dt", ddt, h0, pw)], axis=1)
    going = later.send("ssm_w_in", gwt_in)
    dh0 = _mm_nn("in_proj_bwd_data_dt", ddt, wt_in, nm, after=going)
    dh0 = _mm_nn("in_proj_bwd_data", dproj, wt_in, 0, add=dh0)
    grad_x, _, d_ln0 = _rms_bwd("rms0_bwd", dh0, x, r0, ln_g[0:1], dx1)

    def heads(v):
        return v[:, 0, :r].reshape(1, h)

    small = dict(ln_g=jnp.concatenate([d_ln0, d_ln1], axis=0), final_g=d_final_g, conv_w=d_conv_w, conv_b=d_conv_b,
                 dt_bias=heads(dbias_g), a_log=heads(dalog_g), d_skip=heads(dd_g), norm_g=d_norm_g, scale=d_scale)
    big = dict(ssm_w_in=gwt_in, ssm_w_out=gw_out, pool_w_in=gw_pin, pool_w_group=gw_g, pool_w_out=gw_pout)
    return loss, grad_x, small, big


SMALL_ORDER = ("ln_g", "final_g", "conv_w", "conv_b", "dt_bias", "a_log", "d_skip", "norm_g", "scale", "loss")


def _flatten_small(parts):
    flat = jnp.concatenate([parts[k].reshape(-1) for k in SMALL_ORDER])
    n = flat.shape[0]
    rows = -(-n // 1024) * 8
    return jnp.pad(flat, (0, rows * 128 - n)).reshape(rows, 128)


def _split_small(flat, shapes):
    flat = flat.reshape(-1)
    out, off = {}, 0
    for k in SMALL_ORDER:
        size = int(np.prod(shapes[k]))
        out[k] = flat[off:off + size].reshape(shapes[k])
        off += size
    return out


def kernel(x, ln_g, final_g, ssm_w_in, ssm_conv_w, ssm_conv_b, ssm_dt_bias, ssm_a_log, ssm_d, ssm_norm_g, ssm_w_out, pool_w_in, pool_w_group, pool_scale, pool_w_out, loss_target, m_ln_g, m_final_g, m_ssm_w_in, m_ssm_conv_w, m_ssm_conv_b, m_ssm_dt_bias, m_ssm_a_log, m_ssm_d, m_ssm_norm_g, m_ssm_w_out, m_pool_w_in, m_pool_w_group, m_pool_scale, m_pool_w_out, v_ln_g, v_final_g, v_ssm_w_in, v_ssm_conv_w, v_ssm_conv_b, v_ssm_dt_bias, v_ssm_a_log, v_ssm_d, v_ssm_norm_g, v_ssm_w_out, v_pool_w_in, v_pool_w_group, v_pool_scale, v_pool_w_out):
    l, d = x.shape[1], x.shape[2]
    me = 4 * lax.axis_index("x") + 2 * lax.axis_index("y") + lax.axis_index("c")
    ng, gds, gd = pool_w_group.shape[1], pool_w_group.shape[2], pool_w_group.shape[3]
    sin_s = ssm_w_in.shape[2]
    dp_s = pool_w_out.shape[1]
    conv_s = ssm_conv_w.shape[2]

    wt_in_s = _pack_rows("pack_w_in", jnp.transpose(ssm_w_in[0]))
    w_out_s = _cast_bf16("cast_w_out", ssm_w_out[0])
    wp_in_s = _cast_bf16("cast_pool_w_in", pool_w_in[0])
    wg_s = _cast_bf16("cast_pool_w_group", pool_w_group[0].reshape(ng * gds, gd))
    wp_out_s = _cast_bf16("cast_pool_w_out", pool_w_out[0])
    small_s = jnp.concatenate([ssm_conv_w[0].reshape(-1), pool_scale[0]])
    n_small = small_s.shape[0]
    small_s = jnp.pad(small_s, (0, -(-n_small // 1024) * 1024 - n_small)).reshape(-1, 128)
    wt_in_g, small_g = _exchange_alone("all_gather_w_in", _Gather([wt_in_s, small_s]))
    wt_in = _unpack_rows("unpack_w_in", wt_in_g.reshape(N_DEV * sin_s, d // 2))
    small_all = small_g.reshape(N_DEV, -1)[:, :n_small]
    conv_w = jnp.transpose(small_all[:, :CONV_TAPS * conv_s].reshape(N_DEV, CONV_TAPS, conv_s), (1, 0, 2))
    conv_w = conv_w.reshape(CONV_TAPS, -1)
    scale = small_all[:, CONV_TAPS * conv_s:].reshape(1, -1)

    def rest_of(arrived):
        w_out_g, wg_g, wp_out_g = arrived
        wg = jnp.transpose(wg_g.reshape(N_DEV, ng, gds, gd), (1, 0, 2, 3)).reshape(ng, gd, gd)
        return w_out_g.reshape(-1, d), wg, wp_out_g.reshape(-1, d)

    def rows_major(gp):
        q, _, hw = gp.shape
        return jnp.transpose(gp.reshape(q, N_DEV, -1, hw), (1, 0, 2, 3))

    to_blocks = dict(ssm_w_in=rows_major, ssm_w_out=rows_major, pool_w_out=rows_major,
                     pool_w_in=lambda gp: gp[:, None],
                     pool_w_group=lambda gp: jnp.transpose(gp.reshape(ng, N_DEV, gds, gd // 2), (1, 0, 2, 3)))
    travelling = {}

    def send(name, gp):
        *travelling[name], token = _scatter_start(f"scatter_{name}_start", to_blocks[name](gp))
        return (token,)

    later = _Later(_Gather([wp_in_s]), lambda arrived: arrived[0], _Gather([w_out_s, wg_s, wp_out_s]), rest_of, send)

    loss, grad_x, small, _ = _local_step(
        x[0], loss_target[0], ln_g, final_g.reshape(1, d), wt_in, conv_w, ssm_conv_b, ssm_dt_bias, ssm_a_log, ssm_d,
        ssm_norm_g, scale, later)

    def arrived(name, after):
        src, land = _scatter_wait(f"scatter_{name}_wait", *travelling[name], after=after)
        own = lax.dynamic_slice_in_dim(src, me, 1, axis=0)
        return lax.dynamic_update_slice_in_dim(land, own, me, axis=0)

    def reduced(name, rv):
        q = rv.shape[1]
        cols = [_reduce_packed(f"{name}_{j}", rv[:, j]) for j in range(q)]
        return cols[0] if q == 1 else jnp.concatenate(cols, axis=1)

    grads = {}
    grads["pool_w_out"] = reduced("reduce_pool_w_out", arrived("pool_w_out", (grad_x,)))[None]
    grads["pool_w_group"] = _reduce_packed("reduce_pool_w_group", arrived("pool_w_group", (grad_x,)).reshape(
        N_DEV, ng * gds, gd // 2)).reshape(1, ng, gds, gd)
    grads["pool_w_in"] = reduced("reduce_pool_w_in", arrived("pool_w_in", (grad_x,)))[None]
    grads["ssm_w_out"] = reduced("reduce_w_out", arrived("ssm_w_out", (grad_x,)))[None]

    weights = dict(ln_g=ln_g, final_g=final_g, ssm_w_in=ssm_w_in, ssm_conv_w=ssm_conv_w, ssm_conv_b=ssm_conv_b,
                   ssm_dt_bias=ssm_dt_bias, ssm_a_log=ssm_a_log, ssm_d=ssm_d, ssm_norm_g=ssm_norm_g,
                   ssm_w_out=ssm_w_out, pool_w_in=pool_w_in, pool_w_group=pool_w_group, pool_scale=pool_scale,
                   pool_w_out=pool_w_out)
    m_in = dict(ln_g=m_ln_g, final_g=m_final_g, ssm_w_in=m_ssm_w_in, ssm_conv_w=m_ssm_conv_w, ssm_conv_b=m_ssm_conv_b,
                ssm_dt_bias=m_ssm_dt_bias, ssm_a_log=m_ssm_a_log, ssm_d=m_ssm_d, ssm_norm_g=m_ssm_norm_g,
                ssm_w_out=m_ssm_w_out, pool_w_in=m_pool_w_in, pool_w_group=m_pool_w_group, pool_scale=m_pool_scale,
                pool_w_out=m_pool_w_out)
    v_in = dict(ln_g=v_ln_g, final_g=v_final_g, ssm_w_in=v_ssm_w_in, ssm_conv_w=v_ssm_conv_w, ssm_conv_b=v_ssm_conv_b,
                ssm_dt_bias=v_ssm_dt_bias, ssm_a_log=v_ssm_a_log, ssm_d=v_ssm_d, ssm_norm_g=v_ssm_norm_g,
                ssm_w_out=v_ssm_w_out, pool_w_in=v_pool_w_in, pool_w_group=v_pool_w_group, pool_scale=v_pool_scale,
                pool_w_out=v_pool_w_out)
    names = list(weights)
    big_names = ("ssm_w_out", "pool_w_in", "pool_w_group", "pool_w_out", "ssm_w_in")
    delta, new_m, new_v = {}, {}, {}

    def adamw_big(k):
        shp = weights[k].shape
        two_d = (-1, shp[-1])
        dk, mk, vk = _adamw(f"adamw_{k}", weights[k].reshape(two_d), grads[k].reshape(two_d), m_in[k].reshape(two_d),
                            v_in[k].reshape(two_d))
        delta[k], new_m[k], new_v[k] = dk.reshape(shp), mk.reshape(shp), vk.reshape(shp)

    for k in big_names[:-1]:
        adamw_big(k)
    small_names = [k for k in names if k not in big_names]

    small["loss"] = loss
    shapes = {k: small[k].shape for k in SMALL_ORDER}
    gathered_small, = _exchange_alone("all_gather_small_grads", _Gather([_flatten_small(small)]),
                                      after=tuple(delta[k] for k in big_names[:-1]))
    summed = _split_small(_sum_slots("sum_small_grads", gathered_small), shapes)
    grads.update(ln_g=summed["ln_g"], final_g=summed["final_g"].reshape(d), ssm_conv_b=summed["conv_b"],
                 ssm_conv_w=lax.dynamic_slice_in_dim(summed["conv_w"], me * conv_s, conv_s, axis=1)[None],
                 ssm_dt_bias=summed["dt_bias"], ssm_a_log=summed["a_log"], ssm_d=summed["d_skip"],
                 ssm_norm_g=summed["norm_g"],
                 pool_scale=lax.dynamic_slice_in_dim(summed["scale"], me * dp_s, dp_s, axis=1))

    def packed(tree):
        flat = jnp.concatenate([tree[k].reshape(-1) for k in small_names])
        nn = flat.shape[0]
        return jnp.pad(flat, (0, -(-nn // 1024) * 1024 - nn), constant_values=1.0).reshape(-1, 128)

    ds, ms, vs = _adamw("adamw_small", packed(weights), packed(grads), packed(m_in), packed(v_in))
    off = 0
    for k in small_names:
        shp = weights[k].shape
        size = int(np.prod(shp))
        for res, arr in ((delta, ds), (new_m, ms), (new_v, vs)):
            res[k] = arr.reshape(-1)[off:off + size].reshape(shp)
        off += size

    grads["ssm_w_in"] = jnp.transpose(reduced("reduce_w_in", arrived("ssm_w_in", (ds,))))[None]
    adamw_big("ssm_w_in")

    return (summed["loss"].reshape(()), grad_x[None], *[grads[k] for k in names], *[delta[k] for k in names],
            *[new_m[k] for k in names], *[new_v[k] for k in names])
```

```python
import math

import jax
import jax.numpy as jnp
import numpy as np
from jax import lax
from jax.experimental import pallas as pl
from jax.experimental.pallas import tpu as pltpu

F32 = jnp.float32
BF = jnp.bfloat16
U32 = jnp.uint32

N_DEV = 8
EPS = 1e-6
SSD_CHUNK = 64
SSM_STATE = 128
CONV_TAPS = 4
HEAD_LANES = 128
POOL_WINDOWS = (2, 4, 8, 16)
ADAM_LR, ADAM_B1, ADAM_B2, ADAM_EPS, ADAM_WD, ADAM_STEP = 0.001, 0.9, 0.999, 1e-08, 0.01, 10
VMEM_LIMIT = 56 * 1024 * 1024
NEG_BIG = -1e30
HAND_ON_AT = 6

NN = ((1,), (0,))
NT = ((1,), (1,))
TN = ((0,), (0,))
MESH = pl.DeviceIdType.MESH


def _t(dim, pref):
    return pref if dim % pref == 0 else dim


def _row_tile(rows, pref, mult=8):
    best = rows
    for cand in range(mult, min(rows, pref) + 1, mult):
        if rows % cand == 0:
            best = cand
    return best


def _params(sem=None):
    return pltpu.CompilerParams(dimension_semantics=sem, vmem_limit_bytes=VMEM_LIMIT)


def _silu(x):
    return x * (1.0 / (1.0 + jnp.exp(-x)))


def _dsilu(x):
    s = 1.0 / (1.0 + jnp.exp(-x))
    return s * (1.0 + x * (1.0 - s))


def _bdot(a, b, dims=NN):
    return lax.dot_general(a.astype(BF), b.astype(BF), (dims, ((), ())), preferred_element_type=F32)


def _pack_pairs(x):
    h = x.shape[1] // 2
    hi = lax.bitcast_convert_type(x[:, :h].astype(jnp.bfloat16).astype(F32), U32)
    lo = lax.bitcast_convert_type(x[:, h:].astype(jnp.bfloat16).astype(F32), U32)
    return lax.bitcast_convert_type(hi | (lo >> 16), F32)


def _unpack_pairs(w):
    u = lax.bitcast_convert_type(w, U32)
    hi = lax.bitcast_convert_type(u & jnp.uint32(0xFFFF0000), F32)
    lo = lax.bitcast_convert_type(u << 16, F32)
    return hi, lo


def _mesh_pos():
    return lax.axis_index("x"), lax.axis_index("y"), lax.axis_index("c")


def _slot(pos):
    return 4 * pos[0] + 2 * pos[1] + pos[2]


class _Gather:
    def __init__(self, arrays):
        self.arrays = list(arrays)
        self.out_shapes = [jax.ShapeDtypeStruct((N_DEV, *s.shape), s.dtype) for s in arrays]

    def phases(self, src, dst, send_sems, recv_sems, local_sems):
        n_arr = len(self.arrays)
        x, y, c = _mesh_pos()
        me, sibling = (x, y, c), (x, y, 1 - c)
        chips = [(1 - x, y), (x, 1 - y), (1 - x, 1 - y)]

        def copy(a, k, block, to, from_src):
            return pltpu.make_async_remote_copy(
                src_ref=src[a] if from_src else dst[a].at[_slot(block)], dst_ref=dst[a].at[_slot(block)],
                send_sem=send_sems.at[a * 7 + k], recv_sem=recv_sems.at[a * 7 + k], device_id=to, device_id_type=MESH)

        def mine(a):
            return pltpu.make_async_copy(src[a], dst[a].at[_slot(me)], local_sems.at[a])

        def first(a):
            return [copy(a, 0, me, sibling, True)] + [copy(a, 1 + j, me, (*chip, c), True)
                                                     for j, chip in enumerate(chips)]

        def start():
            for a in range(n_arr):
                mine(a).start()
                for cp in first(a):
                    cp.start()

        def middle():
            for j, chip in enumerate(chips):
                for a in range(n_arr):
                    copy(a, 1 + j, (*chip, c), me, False).wait_recv()
                    copy(a, 4 + j, (*chip, c), sibling, False).start()

        def finish():
            for a in range(n_arr):
                copy(a, 0, sibling, me, False).wait_recv()
                for j, chip in enumerate(chips):
                    copy(a, 4 + j, (*chip, 1 - c), me, False).wait_recv()
                for cp in first(a):
                    cp.wait_send()
                for j, chip in enumerate(chips):
                    copy(a, 4 + j, (*chip, c), sibling, False).wait_send()
                mine(a).wait()

        return start, middle, finish


def _hosted_call(name, body, grid, in_specs, out_specs, out_shape, scratch_shapes, sem, operands, exch=None,
                 after=(), into=None):
    if into is not None:
        n_lead = len(in_specs)
        inner = body

        def body(*refs):
            inner(*refs[:n_lead], *refs[n_lead + 1:])

        return pl.pallas_call(
            body, grid=grid, in_specs=[*in_specs, pl.BlockSpec(memory_space=pl.ANY)], out_specs=out_specs,
            out_shape=out_shape, scratch_shapes=scratch_shapes, input_output_aliases={n_lead: 0},
            compiler_params=_params(sem), name=name)(*operands, into)
    if after:
        n_lead = len(in_specs)
        inner = body

        def body(*refs):
            inner(*refs[:n_lead], *refs[n_lead + len(after):])

        in_specs = [*in_specs, *[pl.BlockSpec(memory_space=pl.ANY)] * len(after)]
        operands = (*operands, *after)
    if exch is None:
        return pl.pallas_call(body, grid=grid, in_specs=in_specs, out_specs=out_specs, out_shape=out_shape,
                              scratch_shapes=scratch_shapes, compiler_params=_params(sem), name=name)(*operands)
    n_in, n_out, n_scr, ne = len(in_specs), len(out_specs), len(scratch_shapes), len(exch.arrays)
    total = math.prod(grid)

    def wrapped(*refs):
        ins, ex_in = refs[:n_in], refs[n_in:n_in + ne]
        outs = refs[n_in + ne:n_in + ne + n_out]
        ex_out = refs[n_in + ne + n_out:n_in + 2 * ne + n_out]
        scr = refs[n_in + 2 * ne + n_out:n_in + 2 * ne + n_out + n_scr]
        step = 0
        for axis, size in enumerate(grid):
            step = step * size + pl.program_id(axis)
        start, middle, finish = exch.phases(ex_in, ex_out, *refs[-3:])
        pl.when(step == 0)(start)
        if middle is not None:
            pl.when(step == (total * HAND_ON_AT) // 8)(middle)
        body(*ins, *outs, *scr)
        pl.when(step == total - 1)(finish)

    hbm = pl.BlockSpec(memory_space=pl.ANY)
    sems = [pltpu.SemaphoreType.DMA((ne * 7,)), pltpu.SemaphoreType.DMA((ne * 7,)), pltpu.SemaphoreType.DMA((ne,))]
    return pl.pallas_call(
        wrapped, grid=grid, in_specs=[*in_specs, *[hbm] * ne], out_specs=[*out_specs, *[hbm] * ne],
        out_shape=[*out_shape, *exch.out_shapes], scratch_shapes=[*scratch_shapes, *sems],
        compiler_params=pltpu.CompilerParams(dimension_semantics=("arbitrary",) * len(grid),
                                             vmem_limit_bytes=VMEM_LIMIT, has_side_effects=True),
        name=name)(*operands, *exch.arrays)


def _flip_peers():
    x, y, c = _mesh_pos()
    peers = []
    for k in range(1, N_DEV):
        fx, fy, fc = (k >> 2) & 1, (k >> 1) & 1, k & 1
        peers.append((1 - x if fx else x, 1 - y if fy else y, 1 - c if fc else c))
    return (x, y, c), peers


def _split_scatter_copy(src, land, send_sems, recv_sems, k, me, peer, sending):
    return pltpu.make_async_remote_copy(
        src_ref=src.at[_slot(peer)], dst_ref=land.at[_slot(me) if sending else _slot(peer)],
        send_sem=send_sems.at[k], recv_sem=recv_sems.at[k], device_id=peer, device_id_type=MESH)


def _scatter_start(name, blocks):
    def body(src, land, send_sems, recv_sems, src_thru, land_thru, token):
        me, peers = _flip_peers()
        for k, peer in enumerate(peers):
            _split_scatter_copy(src, land, send_sems, recv_sems, k, me, peer, True).start()
        token[...] = jnp.zeros_like(token)

    hbm = pl.BlockSpec(memory_space=pltpu.HBM)
    sem = pl.BlockSpec(memory_space=pltpu.SEMAPHORE)
    return pl.pallas_call(
        body, name=name,
        out_shape=(pltpu.SemaphoreType.DMA((N_DEV - 1,)), pltpu.SemaphoreType.DMA((N_DEV - 1,)),
                   pltpu.HBM(blocks.shape, blocks.dtype), pltpu.HBM(blocks.shape, blocks.dtype),
                   jax.ShapeDtypeStruct((8, 128), F32)),
        in_specs=(hbm, hbm), out_specs=(sem, sem, hbm, hbm, pl.BlockSpec(memory_space=pltpu.VMEM)),
        input_output_aliases={0: 2, 1: 3},
        compiler_params=pltpu.CompilerParams(has_side_effects=pltpu.SideEffectType.DATAFLOW_SIDE_EFFECTING),
    )(pltpu.with_memory_space_constraint(blocks, pltpu.HBM),
      pltpu.with_memory_space_constraint(lax.empty(blocks.shape, blocks.dtype), pltpu.HBM))


def _scatter_wait(name, send_sems, recv_sems, src_thru, land_thru, after):
    def body(src, land, send_sems, recv_sems, *rest):
        me, peers = _flip_peers()
        for k, peer in enumerate(peers):
            _split_scatter_copy(src, land, send_sems, recv_sems, k, me, peer, True).wait_send()
            _split_scatter_copy(src, land, send_sems, recv_sems, k, me, peer, False).wait_recv()

    hbm = pl.BlockSpec(memory_space=pltpu.HBM)
    sem = pl.BlockSpec(memory_space=pltpu.SEMAPHORE)
    return pl.pallas_call(
        body, name=name,
        out_shape=(pltpu.HBM(src_thru.shape, src_thru.dtype), pltpu.HBM(land_thru.shape, land_thru.dtype)),
        in_specs=(hbm, hbm, sem, sem, *[pl.BlockSpec(memory_space=pl.ANY)] * len(after)), out_specs=(hbm, hbm),
        input_output_aliases={0: 0, 1: 1},
        compiler_params=pltpu.CompilerParams(has_side_effects=pltpu.SideEffectType.DATAFLOW_SIDE_EFFECTING),
    )(src_thru, land_thru, send_sems, recv_sems, *after)


def _exchange_alone(name, exch, after=()):
    def body():
        pass

    return _hosted_call(name, body, (1,), [], [], [], [], None, (), exch, after)


def _mm(name, grid, a, a_spec, b, b_spec, dims, outs, o_specs, acc_shape, extra=(), extra_specs=(), epi=None,
        exch=None, after=(), into=None):
    nk = grid[-1]
    n_extra, n_out = len(extra), len(outs)

    def body(*refs):
        a_ref, b_ref = refs[0], refs[1]
        ex = refs[2:2 + n_extra]
        o_refs = refs[2 + n_extra:2 + n_extra + n_out]

        def write(res):
            res = (res,) if epi is None else epi(res, *[e[...] for e in ex])
            for o, r in zip(o_refs, res):
                o[...] = r.astype(o.dtype)

        if nk == 1:
            write(_bdot(a_ref[...], b_ref[...], dims))
            return
        acc = refs[-1]
        k = pl.program_id(len(grid) - 1)

        @pl.when(k == 0)
        def _():
            acc[...] = _bdot(a_ref[...], b_ref[...], dims)

        @pl.when(jnp.logical_and(k > 0, k < nk - 1))
        def _():
            acc[...] += _bdot(a_ref[...], b_ref[...], dims)

        @pl.when(k == nk - 1)
        def _():
            write(acc[...] + _bdot(a_ref[...], b_ref[...], dims))

    sem = ("parallel",) * (len(grid) - 1) + ("arbitrary",)
    scratch = [] if nk == 1 else [pltpu.VMEM(acc_shape, F32)]
    return _hosted_call(name, body, grid, [a_spec, b_spec, *extra_specs], list(o_specs), list(outs),
                        scratch, sem, (a, b, *extra), exch, after, into)


def _add_epi(acc, add):
    return (acc + add,)


def _pack_epi(acc):
    return (_pack_pairs(acc),)


def _mm_nt(name, a, b, n_off, n, out_dtype=F32, tm=1024, tn=512, exch=None, after=()):
    m, kk = a.shape
    tm, tn = _t(m, tm), math.gcd(_t(n, tn), n_off)
    res = _mm(name, (m // tm, n // tn, 1),
              a, pl.BlockSpec((tm, kk), lambda i, j, k: (i, 0)),
              b, pl.BlockSpec((tn, kk), lambda i, j, k: (n_off // tn + j, 0)), NT,
              [jax.ShapeDtypeStruct((m, n), out_dtype)], [pl.BlockSpec((tm, tn), lambda i, j, k: (i, j))], (tm, tn),
              exch=exch, after=after)
    return res[0] if exch is None else res


def _mm_nn_blocked(name, a, b3, tm=1024, tn=512):
    m, kk = a.shape
    nb, _, cs = b3.shape
    tm, tn = _t(m, tm), _t(cs, tn)
    per = cs // tn
    return _mm(name, (m // tm, nb * per, 1),
               a, pl.BlockSpec((tm, kk), lambda i, j, k: (i, 0)),
               b3, pl.BlockSpec((None, kk, tn), lambda i, j, k: (j // per, 0, j % per)), NN,
               [jax.ShapeDtypeStruct((m, nb * cs), F32)], [pl.BlockSpec((tm, tn), lambda i, j, k: (i, j))],
               (tm, tn))[0]


def _mm_nt_blocked(name, a, b3, after=(), tm=1024, tn=1024):
    m = a.shape[0]
    nb, n, cs = b3.shape
    tm, tn = _t(m, tm), _t(n, tn)
    return _mm(name, (m // tm, n // tn, nb),
               a, pl.BlockSpec((tm, cs), lambda i, j, k: (i, k)),
               b3, pl.BlockSpec((None, tn, cs), lambda i, j, k: (k, j, 0)), NT,
               [jax.ShapeDtypeStruct((m, n), F32)], [pl.BlockSpec((tm, tn), lambda i, j, k: (i, j))], (tm, tn),
               after=after)[0]


def _mm_nn(name, a, b, k_off=0, kk=None, add=None, out_dtype=F32, tm=1024, tn=1024, tk=1024, after=()):
    m = a.shape[0]
    kk = a.shape[1] if kk is None else kk
    n = b.shape[1]
    tm, tn, tk = _t(m, tm), _t(n, tn), math.gcd(_t(kk, tk), k_off)
    extra, especs, epi = (), (), None
    if add is not None:
        extra, especs, epi = (add,), (pl.BlockSpec((tm, tn), lambda i, j, k: (i, j)),), _add_epi
    return _mm(name, (m // tm, n // tn, kk // tk),
               a, pl.BlockSpec((tm, tk), lambda i, j, k: (i, k)),
               b, pl.BlockSpec((tk, tn), lambda i, j, k: (k_off // tk + k, j)), NN,
               [jax.ShapeDtypeStruct((m, n), out_dtype)], [pl.BlockSpec((tm, tn), lambda i, j, k: (i, j))], (tm, tn),
               extra, especs, epi, after=after)[0]


TN_ACC_ELEMENTS = 1 << 20


def _mm_tn_packed(name, a, b, pw, tk=1024, rows=None, into=None):
    kk, m = a.shape
    n = b.shape[1]
    tm, tk = _t(m, TN_ACC_ELEMENTS // pw), _t(kk, tk)
    rows = m if rows is None else rows
    row_blk = 0
    if into is not None:
        rows = into.shape[1]
        assert (rows - m) % tm == 0
        row_blk = (rows - m) // tm
    return _mm(name, (m // tm, n // pw, kk // tk),
               a, pl.BlockSpec((tk, tm), lambda i, j, k: (k, i)),
               b, pl.BlockSpec((tk, pw), lambda i, j, k: (k, j)), TN,
               [jax.ShapeDtypeStruct((n // pw, rows, pw // 2), F32)],
               [pl.BlockSpec((None, tm, pw // 2), lambda i, j, k: (j, row_blk + i, 0))], (tm, pw), epi=_pack_epi,
               into=into)[0]


def _rms_fwd(name, x, g, tm=256):
    l, d = x.shape
    tm = _t(l, tm)

    def body(x_ref, g_ref, h_ref, r_ref):
        xv = x_ref[...]
        r = lax.rsqrt(jnp.mean(xv * xv, axis=-1, keepdims=True) + EPS)
        h_ref[...] = (xv * r * g_ref[...]).astype(BF)
        r_ref[...] = r

    return pl.pallas_call(
        body, grid=(l // tm,),
        in_specs=[pl.BlockSpec((tm, d), lambda i: (i, 0)), pl.BlockSpec((1, d), lambda i: (0, 0))],
        out_specs=[pl.BlockSpec((tm, d), lambda i: (i, 0)), pl.BlockSpec((tm, 1), lambda i: (i, 0))],
        out_shape=[jax.ShapeDtypeStruct((l, d), BF), jax.ShapeDtypeStruct((l, 1), F32)],
        compiler_params=_params(("parallel",)), name=name)(x, g)


def _rms_bwd(name, dh, x, r, g, dres, tm=256):
    l, d = x.shape
    tm = _t(l, tm)

    def body(dh_ref, x_ref, r_ref, g_ref, dres_ref, dx_ref, dxb_ref, dg_ref):
        i = pl.program_id(0)
        rr = r_ref[...]
        xhat = x_ref[...] * rr
        dhv = dh_ref[...]
        dxh = dhv * g_ref[...]
        dx = dres_ref[...] + rr * (dxh - xhat * jnp.mean(dxh * xhat, axis=-1, keepdims=True))
        dx_ref[...] = dx
        dxb_ref[...] = dx.astype(BF)

        @pl.when(i == 0)
        def _():
            dg_ref[...] = jnp.zeros_like(dg_ref)

        dg_ref[...] += jnp.sum(dhv * xhat, axis=0, keepdims=True)

    row = pl.BlockSpec((tm, d), lambda i: (i, 0))
    vec = pl.BlockSpec((1, d), lambda i: (0, 0))
    return pl.pallas_call(
        body, grid=(l // tm,), in_specs=[row, row, pl.BlockSpec((tm, 1), lambda i: (i, 0)), vec, row],
        out_specs=[row, row, vec],
        out_shape=[jax.ShapeDtypeStruct((l, d), F32), jax.ShapeDtypeStruct((l, d), BF),
                   jax.ShapeDtypeStruct((1, d), F32)],
        compiler_params=_params(("arbitrary",)), name=name)(dh, x, r, g, dres)


def _loss_head(x, g, target, tm=256):
    l, d = x.shape
    tm = _t(l, tm)

    def body(x_ref, g_ref, t_ref, dx_ref, dxb_ref, dg_ref, loss_ref):
        i = pl.program_id(0)
        xv = x_ref[...]
        gv = g_ref[...]
        r = lax.rsqrt(jnp.mean(xv * xv, axis=-1, keepdims=True) + EPS)
        xhat = xv * r
        e = xhat * gv - t_ref[...]
        dy = e * (1.0 / d)
        dxh = dy * gv
        dx = r * (dxh - xhat * jnp.mean(dxh * xhat, axis=-1, keepdims=True))
        dx_ref[...] = dx
        dxb_ref[...] = dx.astype(BF)

        @pl.when(i == 0)
        def _():
            dg_ref[...] = jnp.zeros_like(dg_ref)
            loss_ref[...] = jnp.zeros_like(loss_ref)

        dg_ref[...] += jnp.sum(dy * xhat, axis=0, keepdims=True)
        loss_ref[...] += 0.5 * jnp.sum(jnp.sum(e * e, axis=-1, keepdims=True) * (1.0 / d), axis=0, keepdims=True)

    row = pl.BlockSpec((tm, d), lambda i: (i, 0))
    vec = pl.BlockSpec((1, d), lambda i: (0, 0))
    return pl.pallas_call(
        body, grid=(l // tm,), in_specs=[row, vec, row],
        out_specs=[row, row, vec, pl.BlockSpec((1, 1), lambda i: (0, 0))],
        out_shape=[jax.ShapeDtypeStruct((l, d), F32), jax.ShapeDtypeStruct((l, d), BF),
                   jax.ShapeDtypeStruct((1, d), F32), jax.ShapeDtypeStruct((1, 1), F32)],
        compiler_params=_params(("arbitrary",)), name="loss_head")(x, g, target)


CONV_HALO = 8


def _conv_pre(x_ref, w_ref, b_ref, i, tr):
    r0 = pl.multiple_of(i * tr, tr)
    cur = x_ref[pl.ds(r0, tr), :]
    prev = x_ref[pl.ds(pl.multiple_of(jnp.maximum(r0 - CONV_HALO, 0), CONV_HALO), CONV_HALO), :]
    prev = jnp.where(i > 0, prev, 0.0)
    ext = jnp.concatenate([prev, cur], axis=0)
    taps = []
    for k in range(CONV_TAPS):
        s = CONV_TAPS - 1 - k
        taps.append(cur if s == 0 else pltpu.roll(ext, s, 0)[CONV_HALO:])
    pre = b_ref[...] + sum(w_ref[k:k + 1, :] * taps[k] for k in range(CONV_TAPS))
    return r0, pre, taps


def _conv_fwd(pm, col_off, conv_w, conv_b, cw=256, tr=512):
    l = pm.shape[0]
    c = conv_w.shape[1]
    cw, tr = _t(c, cw), _t(l, tr)
    assert col_off % cw == 0

    def body(x_ref, w_ref, b_ref, o_ref):
        def step(i, carry):
            r0, pre, _ = _conv_pre(x_ref, w_ref, b_ref, i, tr)
            o_ref[pl.ds(r0, tr), :] = _silu(pre)
            return carry
        lax.fori_loop(0, l // tr, step, 0)

    return pl.pallas_call(
        body, grid=(c // cw,),
        in_specs=[pl.BlockSpec((l, cw), lambda j: (0, col_off // cw + j)),
                  pl.BlockSpec((CONV_TAPS, cw), lambda j: (0, j)), pl.BlockSpec((1, cw), lambda j: (0, j))],
        out_specs=pl.BlockSpec((l, cw), lambda j: (0, j)), out_shape=jax.ShapeDtypeStruct((l, c), F32),
        compiler_params=_params(("parallel",)), name="conv_fwd")(pm, conv_w, conv_b)


def _conv_bwd(pm, col_off, conv_w, conv_b, dy, dproj, cw=256, tr=512):
    l = pm.shape[0]
    c = conv_w.shape[1]
    cw, tr = _t(c, cw), _t(l, tr)
    nt = l // tr

    def body(x_ref, w_ref, b_ref, dy_ref, _, dx_ref, dw_ref, db_ref, dpre_ref):
        def step1(i, carry):
            dws, db = carry
            r0, pre, taps = _conv_pre(x_ref, w_ref, b_ref, i, tr)
            dpre = dy_ref[pl.ds(r0, tr), :] * _dsilu(pre)
            dpre_ref[pl.ds(r0, tr), :] = dpre
            dws = tuple(dws[k] + jnp.sum(dpre * taps[k], axis=0, keepdims=True) for k in range(CONV_TAPS))
            return dws, db + jnp.sum(dpre, axis=0, keepdims=True)

        z = jnp.zeros((1, cw), F32)
        dws, db = lax.fori_loop(0, nt, step1, ((z,) * CONV_TAPS, z))
        for k in range(CONV_TAPS):
            dw_ref[k:k + 1, :] = dws[k]
        db_ref[...] = db

        def step2(i, carry):
            r0 = pl.multiple_of(i * tr, tr)
            cur = dpre_ref[pl.ds(r0, tr), :]
            nxt = dpre_ref[pl.ds(pl.multiple_of(jnp.minimum(r0 + tr, l - CONV_HALO), CONV_HALO), CONV_HALO), :]
            nxt = jnp.where(i < nt - 1, nxt, 0.0)
            ext = jnp.concatenate([cur, nxt], axis=0)
            acc = w_ref[CONV_TAPS - 1:CONV_TAPS, :] * cur
            for k in range(CONV_TAPS - 1):
                s = CONV_TAPS - 1 - k
                acc = acc + w_ref[k:k + 1, :] * pltpu.roll(ext, tr + CONV_HALO - s, 0)[:tr]
            dx_ref[pl.ds(r0, tr), :] = acc.astype(dx_ref.dtype)
            return carry
        lax.fori_loop(0, nt, step2, 0)

    col = pl.BlockSpec((l, cw), lambda j: (0, j))
    shifted = pl.BlockSpec((l, cw), lambda j: (0, col_off // cw + j))
    return pl.pallas_call(
        body, grid=(c // cw,),
        in_specs=[shifted, pl.BlockSpec((CONV_TAPS, cw), lambda j: (0, j)), pl.BlockSpec((1, cw), lambda j: (0, j)),
                  col, pl.BlockSpec(memory_space=pl.ANY)],
        out_specs=[shifted, pl.BlockSpec((CONV_TAPS, cw), lambda j: (0, j)), pl.BlockSpec((1, cw), lambda j: (0, j))],
        out_shape=[jax.ShapeDtypeStruct(dproj.shape, dproj.dtype), jax.ShapeDtypeStruct((CONV_TAPS, c), F32),
                   jax.ShapeDtypeStruct((1, c), F32)],
        scratch_shapes=[pltpu.VMEM((l, cw), F32)], input_output_aliases={4: 0},
        compiler_params=_params(("parallel",)), name="conv_bwd")(pm, conv_w, conv_b, dy, dproj)


def _split(x, pieces):
    out, rest = [], x
    for _ in range(pieces):
        piece = rest.astype(BF)
        out.append(piece)
        rest = rest - piece.astype(F32)
    return out


def _rows_times(xs, m_stack, pieces):
    x = xs[0] if len(xs) == 1 else jnp.concatenate(xs, axis=0)
    out = _bdot(jnp.concatenate(_split(x, pieces), axis=1), m_stack)
    sizes = [v.shape[0] for v in xs]
    offs = np.cumsum([0] + sizes)
    return [out[offs[i]:offs[i + 1]] for i in range(len(xs))]


def _times_rows(m_stack, x, pieces):
    return _bdot(m_stack, jnp.concatenate(_split(x, pieces), axis=0))


EXPAND_PIECES = 3
FOLD_PIECES = 2


def _ssd_consts(r, p, t):
    assert p == t, "heads expand to P lanes of the inputs and to T lanes of the decay matrices alike"
    rp = r * p
    tri = np.tril(np.ones((t, t), np.float32))
    ep = np.zeros((HEAD_LANES, rp), np.float32)
    ep[np.arange(rp) // p, np.arange(rp)] = 1.0
    itile = (np.arange(t)[:, None] == (np.arange(rp) % t)[None, :]).astype(np.float32)
    lmask = (np.arange(t)[:, None] >= (np.arange(rp) % t)[None, :]).astype(np.float32)
    bmask = ((np.arange(rp) // t)[:, None] == (np.arange(rp) // p)[None, :]).astype(np.float32)
    return [jnp.asarray(np.concatenate([ep] * EXPAND_PIECES, axis=0), BF),
            jnp.asarray(np.concatenate([ep.T] * FOLD_PIECES, axis=0), BF),
            jnp.asarray(np.concatenate([tri] * EXPAND_PIECES, axis=1), BF),
            jnp.asarray(np.concatenate([tri.T] * EXPAND_PIECES, axis=1), BF),
            jnp.asarray(ep.T.copy()), jnp.asarray(itile), jnp.asarray(lmask), jnp.asarray(bmask)]


def _ssd_common(xs, bm, pre_raw, dtb, alog, e_stack, tri_stack, itile, lmask, bmask, r, t):
    pre = pre_raw + dtb
    dt = jnp.maximum(pre, 0.0) + jnp.log(1.0 + jnp.exp(-jnp.abs(pre)))
    a = -jnp.exp(alog)
    cs = _times_rows(tri_stack, dt * a, EXPAND_PIECES)
    last = cs[t - 1:t, :]
    dtx, csx = _rows_times([dt, cs], e_stack, EXPAND_PIECES)
    ecsx = jnp.exp(csx)
    wx = jnp.exp(csx[t - 1:t, :] - csx)
    csrow = jnp.sum(csx * itile, axis=0, keepdims=True)
    lx = jnp.exp(jnp.where(lmask > 0.0, csx - csrow, NEG_BIG))
    xdt = xs * dtx
    xblk = (jnp.concatenate([xdt] * r, axis=0) * bmask).astype(BF)
    btile = jnp.concatenate([bm] * r, axis=0).astype(BF)
    return pre, dt, a, last, dtx, ecsx, wx, lx, xdt, xblk, btile


def _ssd_specs(l, g_n, r, p, n, t, conv, rev):
    nc = l // t
    rp = r * p
    cidx = (lambda c: nc - 1 - c) if rev else (lambda c: c)
    row_spec = lambda width: pl.BlockSpec((t, width), lambda c: (cidx(c), 0))
    dtr_spec = pl.BlockSpec((g_n, t, HEAD_LANES), lambda c: (0, cidx(c), 0))
    par_spec = pl.BlockSpec((g_n, 1, HEAD_LANES), lambda c: (0, 0, 0))
    dskx_spec = pl.BlockSpec((g_n, 1, rp), lambda c: (0, 0, 0))
    st_spec = pl.BlockSpec((None, g_n, rp, n), lambda c: (cidx(c), 0, 0, 0))
    return nc, row_spec, dtr_spec, par_spec, dskx_spec, st_spec


def _const_specs(consts):
    return [pl.BlockSpec(a.shape, lambda c: (0, 0)) for a in consts]


def _ssd_fwd(xbc, dtr, dtb, alog, dskx, dims, exch=None):
    h, g_n, r, p, n, t = dims
    l, conv = xbc.shape
    rp, hp = r * p, h * p
    nc, row_spec, dtr_spec, par_spec, dskx_spec, st_spec = _ssd_specs(l, g_n, r, p, n, t, conv, False)
    consts = _ssd_consts(r, p, t)

    def body(x_ref, dtr_ref, dtb_ref, alog_ref, dskx_ref,
             e_ref, et_ref, tri_ref, trit_ref, ept_ref, it_ref, lm_ref, bmk_ref, y_ref, st_ref, s_ref):
        @pl.when(pl.program_id(0) == 0)
        def _():
            s_ref[...] = jnp.zeros_like(s_ref)

        for g in range(g_n):
            xs = x_ref[:, g * rp:(g + 1) * rp]
            bm = x_ref[:, hp + g * n:hp + (g + 1) * n]
            cm = x_ref[:, hp + (g_n + g) * n:hp + (g_n + g + 1) * n]
            (_, _, _, last, _, ecsx, wx, lx, xdt, xblk, btile) = _ssd_common(
                xs, bm, dtr_ref[g], dtb_ref[g], alog_ref[g], e_ref[...], tri_ref[...], it_ref[...], lm_ref[...],
                bmk_ref[...], r, t)
            s_in = s_ref[g]
            st_ref[g] = s_in
            cbx = _bdot(cm, btile, NT)
            yd = _bdot(cbx * lx, xblk)
            yo = ecsx * _bdot(cm, s_in, NT)
            y_ref[:, g * rp:(g + 1) * rp] = yd + yo + dskx_ref[g] * xs
            elast = jnp.sum(ept_ref[...] * jnp.exp(last), axis=1, keepdims=True)
            s_ref[g] = elast * s_in + _bdot(xdt * wx, bm, TN)

    return _hosted_call(
        "ssd_fwd", body, (nc,),
        [row_spec(conv), dtr_spec, par_spec, par_spec, dskx_spec, *_const_specs(consts)],
        [row_spec(hp), st_spec],
        [jax.ShapeDtypeStruct((l, hp), F32), jax.ShapeDtypeStruct((nc, g_n, rp, n), F32)],
        [pltpu.VMEM((g_n, rp, n), F32)], ("arbitrary",),
        (xbc, dtr, dtb, alog, dskx, *consts), exch)


def _ssd_bwd(xbc, dtr, dtb, alog, dskx, states, dy, dims, after=()):
    h, g_n, r, p, n, t = dims
    l, conv = xbc.shape
    rp, hp = r * p, h * p
    nc, row_spec, dtr_spec, par_spec, dskx_spec, st_spec = _ssd_specs(l, g_n, r, p, n, t, conv, True)
    consts = _ssd_consts(r, p, t)

    def fold_rows(v, rows):
        return sum(v[k * rows:(k + 1) * rows, :] for k in range(r))

    def body(x_ref, dtr_ref, dtb_ref, alog_ref, dskx_ref, st_ref, dy_ref,
             e_ref, et_ref, tri_ref, trit_ref, ept_ref, it_ref, lm_ref, bmk_ref,
             dx_ref, ddt_ref, dbias_ref, dalog_ref, dd_ref, ds_ref):
        @pl.when(pl.program_id(0) == 0)
        def _():
            ds_ref[...] = jnp.zeros_like(ds_ref)
            dbias_ref[...] = jnp.zeros_like(dbias_ref)
            dalog_ref[...] = jnp.zeros_like(dalog_ref)
            dd_ref[...] = jnp.zeros_like(dd_ref)

        ept, itile, bmask = ept_ref[...], it_ref[...], bmk_ref[...]
        last_row = lax.broadcasted_iota(jnp.int32, (t, HEAD_LANES), 0) == t - 1
        for g in range(g_n):
            xs = x_ref[:, g * rp:(g + 1) * rp]
            bm = x_ref[:, hp + g * n:hp + (g + 1) * n]
            cm = x_ref[:, hp + (g_n + g) * n:hp + (g_n + g + 1) * n]
            dy = dy_ref[:, g * rp:(g + 1) * rp]
            (pre, dt, a, last, dtx, ecsx, wx, lx, xdt, xblk, btile) = _ssd_common(
                xs, bm, dtr_ref[g], dtb_ref[g], alog_ref[g], e_ref[...], tri_ref[...], itile, lm_ref[...], bmask, r, t)
            s_in = st_ref[g]
            ds_out = ds_ref[g]
            elast_row = jnp.exp(last)
            elast = jnp.sum(ept * elast_row, axis=1, keepdims=True)

            cbx = _bdot(cm, btile, NT)
            amat = cbx * lx
            da = _bdot(dy, xblk, NT)
            dxdt = fold_rows(_bdot(amat, dy, TN) * bmask, t)
            dcbx = da * lx
            q = da * amat
            dc = _bdot(dcbx, btile)
            db = fold_rows(_bdot(dcbx, cm, TN), t)

            g0 = _bdot(cm, s_in, NT)
            dg0 = dy * ecsx
            dc = dc + _bdot(dg0, s_in)

            z = _bdot(bm, ds_out, NT)
            dxdt = dxdt + z * wx
            db = db + _bdot(xdt * wx, ds_out)
            ds_ref[g] = elast * ds_out + _bdot(dg0, cm, TN)

            dwx = z * xdt * wx
            rows = jnp.concatenate([jnp.sum(dy * xs, axis=0, keepdims=True), jnp.sum(dwx, axis=0, keepdims=True),
                                    jnp.zeros((14, rp), F32)], axis=0)
            f_cs, f_dt, f_rows = _rows_times(
                [q - itile * jnp.sum(q, axis=0, keepdims=True) + dy * g0 * ecsx - dwx, dxdt * xs, rows],
                et_ref[...], FOLD_PIECES)
            dd_ref[g] += f_rows[0:1]
            dlast = jnp.sum(jnp.sum(ds_out * s_in, axis=1, keepdims=True) * ept, axis=0, keepdims=True) * elast_row
            dlast = dlast + f_rows[1:2]

            dcs = f_cs + jnp.where(last_row, dlast, 0.0)
            dadt = _times_rows(trit_ref[...], dcs, EXPAND_PIECES)
            dalog_ref[g] += jnp.sum(dadt * dt, axis=0, keepdims=True) * a
            dpre = (dadt * a + f_dt) * (1.0 / (1.0 + jnp.exp(-pre)))
            ddt_ref[g] = dpre
            dbias_ref[g] += jnp.sum(dpre, axis=0, keepdims=True)
            dx_ref[:, g * rp:(g + 1) * rp] = dskx_ref[g] * dy + dxdt * dtx
            dx_ref[:, hp + g * n:hp + (g + 1) * n] = db
            dx_ref[:, hp + (g_n + g) * n:hp + (g_n + g + 1) * n] = dc

    par_shape = jax.ShapeDtypeStruct((g_n, 1, HEAD_LANES), F32)
    return _hosted_call(
        "ssd_bwd", body, (nc,),
        [row_spec(conv), dtr_spec, par_spec, par_spec, dskx_spec, st_spec, row_spec(hp), *_const_specs(consts)],
        [row_spec(conv), dtr_spec, par_spec, par_spec, par_spec],
        [jax.ShapeDtypeStruct((l, conv), F32), jax.ShapeDtypeStruct((g_n, l, HEAD_LANES), F32),
         par_shape, par_shape, par_shape],
        [pltpu.VMEM((g_n, rp, n), F32)], ("arbitrary",),
        (xbc, dtr, dtb, alog, dskx, states, dy, *consts), after=after)


def _gate_norm_fwd(y, pm, g, tm=128):
    l, di = y.shape
    tm = _t(l, tm)

    def body(y_ref, z_ref, g_ref, o_ref, r_ref):
        v = y_ref[...] * _silu(z_ref[...])
        rr = lax.rsqrt(jnp.mean(v * v, axis=-1, keepdims=True) + EPS)
        o_ref[...] = (v * rr * g_ref[...]).astype(BF)
        r_ref[...] = rr

    row = pl.BlockSpec((tm, di), lambda i: (i, 0))
    return pl.pallas_call(
        body, grid=(l // tm,), in_specs=[row, row, pl.BlockSpec((1, di), lambda i: (0, 0))],
        out_specs=[row, pl.BlockSpec((tm, 1), lambda i: (i, 0))],
        out_shape=[jax.ShapeDtypeStruct((l, di), BF), jax.ShapeDtypeStruct((l, 1), F32)],
        compiler_params=_params(("parallel",)), name="gate_norm_fwd")(y, pm, g)


def _gate_norm_bwd(dyn, y, pm, rstd, g, tm=128):
    l, di = y.shape
    tm = _t(l, tm)

    def body(dyn_ref, y_ref, z_ref, r_ref, g_ref, dy_ref, dz_ref, dg_ref):
        i = pl.program_id(0)
        yv, zv, rr = y_ref[...], z_ref[...], r_ref[...]
        sz = _silu(zv)
        vhat = yv * sz * rr
        dn = dyn_ref[...]
        dvh = dn * g_ref[...]
        dv = rr * (dvh - vhat * jnp.mean(dvh * vhat, axis=-1, keepdims=True))
        dy_ref[...] = dv * sz
        dz_ref[...] = (dv * yv * _dsilu(zv)).astype(BF)

        @pl.when(i == 0)
        def _():
            dg_ref[...] = jnp.zeros_like(dg_ref)

        dg_ref[...] += jnp.sum(dn * vhat, axis=0, keepdims=True)

    row = pl.BlockSpec((tm, di), lambda i: (i, 0))
    vec = pl.BlockSpec((1, di), lambda i: (0, 0))
    return pl.pallas_call(
        body, grid=(l // tm,), in_specs=[row, row, row, pl.BlockSpec((tm, 1), lambda i: (i, 0)), vec],
        out_specs=[row, row, vec],
        out_shape=[jax.ShapeDtypeStruct((l, di), F32), jax.ShapeDtypeStruct((l, pm.shape[1]), BF),
                   jax.ShapeDtypeStruct((1, di), F32)],
        compiler_params=_params(("arbitrary",)), name="gate_norm_bwd")(dyn, y, pm, rstd, g)


POOL_HALO = 16


def _pool_mix(name, src, dp, backward, out_dtype, into=None, cw=256, tr=512):
    l = src.shape[0]
    gd = dp // len(POOL_WINDOWS)
    cwl, trl = _t(gd, cw), _t(l, tr)
    nt = l // trl

    def body(x_ref, *rest):
        o_ref = rest[-1]
        gi = pl.program_id(0)
        for wi, win in enumerate(POOL_WINDOWS):
            @pl.when(gi == wi)
            def _(win=win):
                def step(i, carry):
                    r0 = pl.multiple_of(i * trl, trl)
                    cur = x_ref[pl.ds(r0, trl), :]
                    trow = r0 + lax.broadcasted_iota(jnp.int32, (trl, 1), 0)
                    cnt = jnp.minimum(trow + 1, win).astype(F32)
                    if not backward:
                        halo = x_ref[pl.ds(pl.multiple_of(jnp.maximum(r0 - POOL_HALO, 0), POOL_HALO), POOL_HALO), :]
                        halo = jnp.where(i > 0, halo, 0.0)
                        s = jnp.concatenate([halo, cur], axis=0)
                        sh = 1
                        while sh < win:
                            s = s + pltpu.roll(s, sh, 0)
                            sh *= 2
                        res = s[POOL_HALO:] / cnt - cur
                    else:
                        halo = x_ref[pl.ds(pl.multiple_of(jnp.minimum(r0 + trl, l - POOL_HALO), POOL_HALO),
                                           POOL_HALO), :]
                        hrow = r0 + trl + lax.broadcasted_iota(jnp.int32, (POOL_HALO, 1), 0)
                        hcnt = jnp.minimum(hrow + 1, win).astype(F32)
                        halo = jnp.where(i < nt - 1, halo / hcnt, 0.0)
                        s = jnp.concatenate([cur / cnt, halo], axis=0)
                        sh = 1
                        while sh < win:
                            s = s + pltpu.roll(s, trl + POOL_HALO - sh, 0)
                            sh *= 2
                        res = s[:trl] - cur
                    o_ref[pl.ds(r0, trl), :] = res.astype(o_ref.dtype)
                    return carry
                lax.fori_loop(0, nt, step, 0)

    col = pl.BlockSpec((l, cwl), lambda g, j: (0, g * (gd // cwl) + j))
    if into is None:
        return pl.pallas_call(
            body, grid=(len(POOL_WINDOWS), gd // cwl), in_specs=[col], out_specs=col,
            out_shape=jax.ShapeDtypeStruct((l, dp), out_dtype),
            compiler_params=_params(("parallel", "parallel")), name=name)(src)
    return pl.pallas_call(
        body, grid=(len(POOL_WINDOWS), gd // cwl), in_specs=[col, pl.BlockSpec(memory_space=pl.ANY)], out_specs=col,
        out_shape=jax.ShapeDtypeStruct(into.shape, into.dtype), input_output_aliases={1: 0},
        compiler_params=_params(("parallel", "parallel")), name=name)(src, into)


def _pool_gate_bwd(dyp, mg, pu, scale, tm=128):
    l, dp = mg.shape
    tm = _t(l, tm)

    def body(dyp_ref, mg_ref, gate_ref, sc_ref, dmg_ref, dgate_ref, dsc_ref):
        i = pl.program_id(0)
        d, m, gt, sc = dyp_ref[...], mg_ref[...], gate_ref[...], sc_ref[...]
        sg = _silu(gt)
        dmg_ref[...] = (d * sc * sg).astype(BF)
        dgate_ref[...] = (d * m * sc * _dsilu(gt)).astype(BF)

        @pl.when(i == 0)
        def _():
            dsc_ref[...] = jnp.zeros_like(dsc_ref)

        dsc_ref[...] += jnp.sum(d * m * sg, axis=0, keepdims=True)

    row = pl.BlockSpec((tm, dp), lambda i: (i, 0))
    right = pl.BlockSpec((tm, dp), lambda i: (i, 1))
    vec = pl.BlockSpec((1, dp), lambda i: (0, 0))
    return pl.pallas_call(
        body, grid=(l // tm,), in_specs=[row, row, right, vec], out_specs=[row, right, vec],
        out_shape=[jax.ShapeDtypeStruct((l, dp), BF), jax.ShapeDtypeStruct((l, 2 * dp), BF),
                   jax.ShapeDtypeStruct((1, dp), F32)],
        compiler_params=_params(("arbitrary",)), name="pool_gate_bwd")(dyp, mg, pu, scale)


def _group_mm_fwd(mp, wg, pu, scale, tm=1024, tn=1024, tk=1024):
    l, dp = mp.shape
    ng, gd = wg.shape[0], wg.shape[1]
    tm, tn, tk = _t(l, tm), _t(gd, tn), _t(gd, tk)

    def epi(acc, gate, sc):
        return acc, acc * sc * _silu(gate)

    out = pl.BlockSpec((tm, tn), lambda g, i, j, k: (i, g * (gd // tn) + j))
    return _mm("group_mm_fwd", (ng, l // tm, gd // tn, gd // tk),
               mp, pl.BlockSpec((tm, tk), lambda g, i, j, k: (i, g * (gd // tk) + k)),
               wg, pl.BlockSpec((None, tk, tn), lambda g, i, j, k: (g, k, j)), NN,
               [jax.ShapeDtypeStruct((l, dp), F32), jax.ShapeDtypeStruct((l, dp), BF)], [out, out], (tm, tn),
               (pu, scale),
               (pl.BlockSpec((tm, tn), lambda g, i, j, k: (i, (dp + g * gd) // tn + j)),
                pl.BlockSpec((1, tn), lambda g, i, j, k: (0, g * (gd // tn) + j))), epi)


def _group_mm_bwd_data(dmg, wg, after=(), tm=1024, tn=1024, tk=1024):
    l, dp = dmg.shape
    ng, gd = wg.shape[0], wg.shape[1]
    tm, tn, tk = _t(l, tm), _t(gd, tn), _t(gd, tk)
    return _mm("group_mm_bwd_data", (ng, l // tm, gd // tn, gd // tk),
               dmg, pl.BlockSpec((tm, tk), lambda g, i, j, k: (i, g * (gd // tk) + k)),
               wg, pl.BlockSpec((None, tn, tk), lambda g, i, j, k: (g, j, k)), NT,
               [jax.ShapeDtypeStruct((l, dp), F32)],
               [pl.BlockSpec((tm, tn), lambda g, i, j, k: (i, g * (gd // tn) + j))], (tm, tn), after=after)[0]


def _group_mm_bwd_weight(mp, dmg, ng, tm=512, tk=1024):
    l, dp = mp.shape
    gd = dp // ng
    tm, tk = _t(gd, tm), _t(l, tk)
    return _mm("group_mm_bwd_weight", (ng, gd // tm, l // tk),
               mp, pl.BlockSpec((tk, tm), lambda g, i, k: (k, g * (gd // tm) + i)),
               dmg, pl.BlockSpec((tk, gd), lambda g, i, k: (k, g)), TN,
               [jax.ShapeDtypeStruct((ng, gd, gd // 2), F32)],
               [pl.BlockSpec((None, tm, gd // 2), lambda g, i, k: (g, i, 0))], (tm, gd), epi=_pack_epi)[0]


def _cast_bf16(name, w, tr=256):
    r, c = w.shape
    tr = _row_tile(r, tr, 16)

    def body(w_ref, o_ref):
        o_ref[...] = w_ref[...].astype(BF)

    blk = pl.BlockSpec((tr, c), lambda i: (i, 0))
    return pl.pallas_call(body, grid=(r // tr,), in_specs=[blk], out_specs=blk,
                          out_shape=jax.ShapeDtypeStruct((r, c), BF), compiler_params=_params(("parallel",)),
                          name=name)(w)


def _pack_rows(name, w, tr=256):
    r, c = w.shape
    tr = _row_tile(r, tr)

    def body(w_ref, o_ref):
        o_ref[...] = _pack_pairs(w_ref[...])

    return pl.pallas_call(body, grid=(r // tr,), in_specs=[pl.BlockSpec((tr, c), lambda i: (i, 0))],
                          out_specs=pl.BlockSpec((tr, c // 2), lambda i: (i, 0)),
                          out_shape=jax.ShapeDtypeStruct((r, c // 2), F32), compiler_params=_params(("parallel",)),
                          name=name)(w)


def _unpack_rows(name, w, tr=512):
    r, h = w.shape
    tr = _row_tile(r, tr, 16)

    def body(w_ref, o_ref):
        hi, lo = _unpack_pairs(w_ref[...])
        o_ref[:, :h] = hi.astype(BF)
        o_ref[:, h:] = lo.astype(BF)

    return pl.pallas_call(body, grid=(r // tr,), in_specs=[pl.BlockSpec((tr, h), lambda i: (i, 0))],
                          out_specs=pl.BlockSpec((tr, 2 * h), lambda i: (i, 0)),
                          out_shape=jax.ShapeDtypeStruct((r, 2 * h), BF), compiler_params=_params(("parallel",)),
                          name=name)(w)


def _reduce_packed(name, recv, tr=256):
    nd, r, h = recv.shape
    tr = _row_tile(r, tr)

    def body(p_ref, o_ref):
        hi, lo = _unpack_pairs(p_ref[0])
        for k in range(1, nd):
            a, b = _unpack_pairs(p_ref[k])
            hi, lo = hi + a, lo + b
        o_ref[:, :h] = hi
        o_ref[:, h:] = lo

    return pl.pallas_call(body, grid=(r // tr,), in_specs=[pl.BlockSpec((nd, tr, h), lambda i: (0, i, 0))],
                          out_specs=pl.BlockSpec((tr, 2 * h), lambda i: (i, 0)),
                          out_shape=jax.ShapeDtypeStruct((r, 2 * h), F32), compiler_params=_params(("parallel",)),
                          name=name)(recv)


def _adamw_math(w, g, m, v):
    m2 = ADAM_B1 * m + (1.0 - ADAM_B1) * g
    v2 = ADAM_B2 * v + (1.0 - ADAM_B2) * (g * g)
    m_hat = m2 / (1.0 - ADAM_B1 ** ADAM_STEP)
    v_hat = v2 / (1.0 - ADAM_B2 ** ADAM_STEP)
    delta = -ADAM_LR * (m_hat / (jnp.sqrt(v_hat) + ADAM_EPS) + ADAM_WD * w)
    return delta, m2, v2


def _adamw(name, w, g, m, v, tr=256):
    r, c = w.shape
    tr = _row_tile(r, tr)

    def body(w_ref, g_ref, m_ref, v_ref, d_ref, m2_ref, v2_ref):
        d, m2, v2 = _adamw_math(w_ref[...], g_ref[...], m_ref[...], v_ref[...])
        d_ref[...] = d
        m2_ref[...] = m2
        v2_ref[...] = v2

    blk = pl.BlockSpec((tr, c), lambda i: (i, 0))
    shp = jax.ShapeDtypeStruct((r, c), F32)
    return pl.pallas_call(body, grid=(r // tr,), in_specs=[blk] * 4, out_specs=[blk] * 3, out_shape=[shp] * 3,
                          compiler_params=_params(("parallel",)), name=name)(w, g, m, v)


def _sum_slots(name, a):
    nd, r, c = a.shape

    def body(a_ref, o_ref):
        s = a_ref[0]
        for k in range(1, nd):
            s = s + a_ref[k]
        o_ref[...] = s

    return pl.pallas_call(body, out_shape=jax.ShapeDtypeStruct((r, c), F32), name=name)(a)


def _head_rows(v, g_n, r):
    return jnp.pad(v.reshape(g_n, 1, r), ((0, 0), (0, 0), (0, HEAD_LANES - r)))


class _Later:
    def __init__(self, gather_pool_in, pool_in_of, gather_rest, rest_of, send):
        self.gather_pool_in, self.pool_in_of = gather_pool_in, pool_in_of
        self.gather_rest, self.rest_of, self.send = gather_rest, rest_of, send


def _local_step(x, target, ln_g, final_g, wt_in, conv_w, conv_b, dt_bias, a_log, d_skip, norm_g, scale, later):
    l, d = x.shape
    di = norm_g.shape[1]
    h = dt_bias.shape[1]
    p = di // h
    conv = conv_b.shape[1]
    n, t = SSM_STATE, SSD_CHUNK
    g_n = (conv - di) // (2 * n)
    r = h // g_n
    nm = di + conv
    dp = scale.shape[1]
    ng = len(POOL_WINDOWS)
    dims = (h, g_n, r, p, n, t)
    pw = d

    h0, r0 = _rms_fwd("rms0_fwd", x, ln_g[0:1])
    pm = _mm_nt("in_proj_main", h0, wt_in, 0, nm, exch=later.gather_pool_in)
    if later.gather_pool_in is not None:
        pm, *arrived = pm
    else:
        arrived = None
    wp_in = later.pool_in_of(arrived)
    dtr = _mm_nt("in_proj_dt", h0, wt_in, nm, h, tn=h)
    xbc = _conv_fwd(pm, di, conv_w, conv_b)
    dtb, alog = _head_rows(dt_bias, g_n, r), _head_rows(a_log, g_n, r)
    dskx = jnp.repeat(d_skip.reshape(g_n, 1, r), p, axis=2)
    dtr_g = jnp.pad(jnp.transpose(dtr.reshape(l, g_n, r), (1, 0, 2)), ((0, 0), (0, 0), (0, HEAD_LANES - r)))
    y, states, *arrived = _ssd_fwd(xbc, dtr_g, dtb, alog, dskx, dims, exch=later.gather_rest)
    w_out, wg, wp_out = later.rest_of(arrived)
    yn, r_n = _gate_norm_fwd(y, pm, norm_g)
    x1 = _mm_nn("ssm_out_proj", yn, w_out, add=x)

    h1, r1 = _rms_fwd("rms1_fwd", x1, ln_g[1:2])
    pu = _mm_nn_blocked("pool_in_proj", h1, wp_in)
    mp = _pool_mix("pool_mix_fwd", pu, dp, False, BF)
    mg, yp = _group_mm_fwd(mp, wg, pu, scale)
    x2 = _mm_nn("pool_out_proj", yp, wp_out, add=x1)

    dx2, dx2_b, d_final_g, loss = _loss_head(x2, final_g, target)

    dyp = _mm_nt("pool_out_bwd_data", dx2_b, wp_out, 0, dp, tn=1024)
    gw_pout = _mm_tn_packed("pool_out_bwd_weight", yp, dx2_b, pw)
    going = later.send("pool_w_out", gw_pout)
    dmg, dpu, d_scale = _pool_gate_bwd(dyp, mg, pu, scale)
    dmp = _group_mm_bwd_data(dmg, wg, after=going)
    gw_g = _group_mm_bwd_weight(mp, dmg, ng)
    going = later.send("pool_w_group", gw_g)
    dpu = _pool_mix("pool_mix_bwd", dmp, dp, True, BF, into=dpu)
    dh1 = _mm_nt_blocked("pool_in_bwd_data", dpu, wp_in, after=going)
    gw_pin = _mm_tn_packed("pool_in_bwd_weight", h1, dpu, 2 * dp // N_DEV)
    going = later.send("pool_w_in", gw_pin)
    dx1, dx1_b, d_ln1 = _rms_bwd("rms1_bwd", dh1, x1, r1, ln_g[1:2], dx2)

    dyn = _mm_nt("ssm_out_bwd_data", dx1_b, w_out, 0, di, tn=1024, after=going)
    gw_out = _mm_tn_packed("ssm_out_bwd_weight", yn, dx1_b, pw)
    going = later.send("ssm_w_out", gw_out)
    dy, dproj, d_norm_g = _gate_norm_bwd(dyn, y, pm, r_n, norm_g)
    dxbc, ddt_g, dbias_g, dalog_g, dd_g = _ssd_bwd(xbc, dtr_g, dtb, alog, dskx, states, dy, dims, after=going)
    dproj, d_conv_w, d_conv_b = _conv_bwd(pm, di, conv_w, conv_b, dxbc, dproj)
    ddt = jnp.transpose(ddt_g[:, :, :r], (1, 0, 2)).reshape(l, h)
    gwt_in = _mm_tn_packed("in_proj_bwd_weight", dproj, h0, pw, rows=nm + h)
    gwt_in = _mm_tn_packed("in_proj_bwd_weight_dt", ddt, h0, pw, into=gwt_in)
    going = later.send("ssm_w_in", gwt_in)
    dh0 = _mm_nn("in_proj_bwd_data_dt", ddt, wt_in, nm, after=going)
    dh0 = _mm_nn("in_proj_bwd_data", dproj, wt_in, 0, add=dh0)
    grad_x, _, d_ln0 = _rms_bwd("rms0_bwd", dh0, x, r0, ln_g[0:1], dx1)

    def heads(v):
        return v[:, 0, :r].reshape(1, h)

    small = dict(ln_g=jnp.concatenate([d_ln0, d_ln1], axis=0), final_g=d_final_g, conv_w=d_conv_w, conv_b=d_conv_b,
                 dt_bias=heads(dbias_g), a_log=heads(dalog_g), d_skip=heads(dd_g), norm_g=d_norm_g, scale=d_scale)
    big = dict(ssm_w_in=gwt_in, ssm_w_out=gw_out, pool_w_in=gw_pin, pool_w_group=gw_g, pool_w_out=gw_pout)
    return loss, grad_x, small, big


SMALL_ORDER = ("ln_g", "final_g", "conv_w", "conv_b", "dt_bias", "a_log", "d_skip", "norm_g", "scale", "loss")


def _flatten_small(parts):
    flat = jnp.concatenate([parts[k].reshape(-1) for k in SMALL_ORDER])
    n = flat.shape[0]
    rows = -(-n // 1024) * 8
    return jnp.pad(flat, (0, rows * 128 - n)).reshape(rows, 128)


def _split_small(flat, shapes):
    flat = flat.reshape(-1)
    out, off = {}, 0
    for k in SMALL_ORDER:
        size = int(np.prod(shapes[k]))
        out[k] = flat[off:off + size].reshape(shapes[k])
        off += size
    return out


def kernel(x, ln_g, final_g, ssm_w_in, ssm_conv_w, ssm_conv_b, ssm_dt_bias, ssm_a_log, ssm_d, ssm_norm_g, ssm_w_out, pool_w_in, pool_w_group, pool_scale, pool_w_out, loss_target, m_ln_g, m_final_g, m_ssm_w_in, m_ssm_conv_w, m_ssm_conv_b, m_ssm_dt_bias, m_ssm_a_log, m_ssm_d, m_ssm_norm_g, m_ssm_w_out, m_pool_w_in, m_pool_w_group, m_pool_scale, m_pool_w_out, v_ln_g, v_final_g, v_ssm_w_in, v_ssm_conv_w, v_ssm_conv_b, v_ssm_dt_bias, v_ssm_a_log, v_ssm_d, v_ssm_norm_g, v_ssm_w_out, v_pool_w_in, v_pool_w_group, v_pool_scale, v_pool_w_out):
    l, d = x.shape[1], x.shape[2]
    me = 4 * lax.axis_index("x") + 2 * lax.axis_index("y") + lax.axis_index("c")
    ng, gds, gd = pool_w_group.shape[1], pool_w_group.shape[2], pool_w_group.shape[3]
    sin_s = ssm_w_in.shape[2]
    dp_s = pool_w_out.shape[1]
    conv_s = ssm_conv_w.shape[2]

    wt_in_s = _pack_rows("pack_w_in", jnp.transpose(ssm_w_in[0]))
    w_out_s = _cast_bf16("cast_w_out", ssm_w_out[0])
    wp_in_s = _cast_bf16("cast_pool_w_in", pool_w_in[0])
    wg_s = _cast_bf16("cast_pool_w_group", pool_w_group[0].reshape(ng * gds, gd))
    wp_out_s = _cast_bf16("cast_pool_w_out", pool_w_out[0])
    small_s = jnp.concatenate([ssm_conv_w[0].reshape(-1), pool_scale[0]])
    n_small = small_s.shape[0]
    small_s = jnp.pad(small_s, (0, -(-n_small // 1024) * 1024 - n_small)).reshape(-1, 128)
    wt_in_g, small_g = _exchange_alone("all_gather_w_in", _Gather([wt_in_s, small_s]))
    wt_in = _unpack_rows("unpack_w_in", wt_in_g.reshape(N_DEV * sin_s, d // 2))
    small_all = small_g.reshape(N_DEV, -1)[:, :n_small]
    conv_w = jnp.transpose(small_all[:, :CONV_TAPS * conv_s].reshape(N_DEV, CONV_TAPS, conv_s), (1, 0, 2))
    conv_w = conv_w.reshape(CONV_TAPS, -1)
    scale = small_all[:, CONV_TAPS * conv_s:].reshape(1, -1)

    def rest_of(arrived):
        w_out_g, wg_g, wp_out_g = arrived
        wg = jnp.transpose(wg_g.reshape(N_DEV, ng, gds, gd), (1, 0, 2, 3)).reshape(ng, gd, gd)
        return w_out_g.reshape(-1, d), wg, wp_out_g.reshape(-1, d)

    def rows_major(gp):
        q, _, hw = gp.shape
        return jnp.transpose(gp.reshape(q, N_DEV, -1, hw), (1, 0, 2, 3))

    to_blocks = dict(ssm_w_in=rows_major, ssm_w_out=rows_major, pool_w_out=rows_major,
                     pool_w_in=lambda gp: gp[:, None],
                     pool_w_group=lambda gp: jnp.transpose(gp.reshape(ng, N_DEV, gds, gd // 2), (1, 0, 2, 3)))
    travelling = {}

    def send(name, gp):
        *travelling[name], token = _scatter_start(f"scatter_{name}_start", to_blocks[name](gp))
        return (token,)

    later = _Later(_Gather([wp_in_s]), lambda arrived: arrived[0], _Gather([w_out_s, wg_s, wp_out_s]), rest_of, send)

    loss, grad_x, small, _ = _local_step(
        x[0], loss_target[0], ln_g, final_g.reshape(1, d), wt_in, conv_w, ssm_conv_b, ssm_dt_bias, ssm_a_log, ssm_d,
        ssm_norm_g, scale, later)

    def arrived(name, after):
        src, land = _scatter_wait(f"scatter_{name}_wait", *travelling[name], after=after)
        own = lax.dynamic_slice_in_dim(src, me, 1, axis=0)
        return lax.dynamic_update_slice_in_dim(land, own, me, axis=0)

    def reduced(name, rv):
        q = rv.shape[1]
        cols = [_reduce_packed(f"{name}_{j}", rv[:, j]) for j in range(q)]
        return cols[0] if q == 1 else jnp.concatenate(cols, axis=1)

    grads = {}
    grads["pool_w_out"] = reduced("reduce_pool_w_out", arrived("pool_w_out", (grad_x,)))[None]
    grads["pool_w_group"] = _reduce_packed("reduce_pool_w_group", arrived("pool_w_group", (grad_x,)).reshape(
        N_DEV, ng * gds, gd // 2)).reshape(1, ng, gds, gd)
    grads["pool_w_in"] = reduced("reduce_pool_w_in", arrived("pool_w_in", (grad_x,)))[None]
    grads["ssm_w_out"] = reduced("reduce_w_out", arrived("ssm_w_out", (grad_x,)))[None]

    weights = dict(ln_g=ln_g, final_g=final_g, ssm_w_in=ssm_w_in, ssm_conv_w=ssm_conv_w, ssm_conv_b=ssm_conv_b,
                   ssm_dt_bias=ssm_dt_bias, ssm_a_log=ssm_a_log, ssm_d=ssm_d, ssm_norm_g=ssm_norm_g,
                   ssm_w_out=ssm_w_out, pool_w_in=pool_w_in, pool_w_group=pool_w_group, pool_scale=pool_scale,
                   pool_w_out=pool_w_out)
    m_in = dict(ln_g=m_ln_g, final_g=m_final_g, ssm_w_in=m_ssm_w_in, ssm_conv_w=m_ssm_conv_w, ssm_conv_b=m_ssm_conv_b,
                ssm_dt_bias=m_ssm_dt_bias, ssm_a_log=m_ssm_a_log, ssm_d=m_ssm_d, ssm_norm_g=m_ssm_norm_g,
                ssm_w_out=m_ssm_w_out, pool_w_in=m_pool_w_in, pool_w_group=m_pool_w_group, pool_scale=m_pool_scale,
                pool_w_out=m_pool_w_out)
    v_in = dict(ln_g=v_ln_g, final_g=v_final_g, ssm_w_in=v_ssm_w_in, ssm_conv_w=v_ssm_conv_w, ssm_conv_b=v_ssm_conv_b,
                ssm_dt_bias=v_ssm_dt_bias, ssm_a_log=v_ssm_a_log, ssm_d=v_ssm_d, ssm_norm_g=v_ssm_norm_g,
                ssm_w_out=v_ssm_w_out, pool_w_in=v_pool_w_in, pool_w_group=v_pool_w_group, pool_scale=v_pool_scale,
                pool_w_out=v_pool_w_out)
    names = list(weights)
    big_names = ("ssm_w_out", "pool_w_in", "pool_w_group", "pool_w_out", "ssm_w_in")
    delta, new_m, new_v = {}, {}, {}

    def adamw_big(k):
        shp = weights[k].shape
        two_d = (-1, shp[-1])
        dk, mk, vk = _adamw(f"adamw_{k}", weights[k].reshape(two_d), grads[k].reshape(two_d), m_in[k].reshape(two_d),
                            v_in[k].reshape(two_d))
        delta[k], new_m[k], new_v[k] = dk.reshape(shp), mk.reshape(shp), vk.reshape(shp)

    for k in big_names[:-1]:
        adamw_big(k)
    small_names = [k for k in names if k not in big_names]

    small["loss"] = loss
    shapes = {k: small[k].shape for k in SMALL_ORDER}
    gathered_small, = _exchange_alone("all_gather_small_grads", _Gather([_flatten_small(small)]),
                                      after=tuple(delta[k] for k in big_names[:-1]))
    summed = _split_small(_sum_slots("sum_small_grads", gathered_small), shapes)
    grads.update(ln_g=summed["ln_g"], final_g=summed["final_g"].reshape(d), ssm_conv_b=summed["conv_b"],
                 ssm_conv_w=lax.dynamic_slice_in_dim(summed["conv_w"], me * conv_s, conv_s, axis=1)[None],
                 ssm_dt_bias=summed["dt_bias"], ssm_a_log=summed["a_log"], ssm_d=summed["d_skip"],
                 ssm_norm_g=summed["norm_g"],
                 pool_scale=lax.dynamic_slice_in_dim(summed["scale"], me * dp_s, dp_s, axis=1))

    def packed(tree):
        flat = jnp.concatenate([tree[k].reshape(-1) for k in small_names])
        nn = flat.shape[0]
        return jnp.pad(flat, (0, -(-nn // 1024) * 1024 - nn), constant_values=1.0).reshape(-1, 128)

    ds, ms, vs = _adamw("adamw_small", packed(weights), packed(grads), packed(m_in), packed(v_in))
    off = 0
    for k in small_names:
        shp = weights[k].shape
        size = int(np.prod(shp))
        for res, arr in ((delta, ds), (new_m, ms), (new_v, vs)):
            res[k] = arr.reshape(-1)[off:off + size].reshape(shp)
        off += size

    k = "ssm_w_in"
    gt = reduced("reduce_w_in", arrived(k, (ds,)))
    dk, mk, vk = _adamw("adamw_ssm_w_in", jnp.transpose(weights[k][0]), gt, jnp.transpose(m_in[k][0]),
                        jnp.transpose(v_in[k][0]))
    grads[k], delta[k], new_m[k], new_v[k] = (jnp.transpose(t)[None] for t in (gt, dk, mk, vk))

    return (summed["loss"].reshape(()), grad_x[None], *[grads[k] for k in names], *[delta[k] for k in names],
            *[new_m[k] for k in names], *[new_v[k] for k in names])
```

```python
import math

import jax
import jax.numpy as jnp
import numpy as np
from jax import lax
from jax.experimental import pallas as pl
from jax.experimental.pallas import tpu as pltpu

F32 = jnp.float32
BF = jnp.bfloat16
U32 = jnp.uint32

N_DEV = 8
EPS = 1e-6
SSD_CHUNK = 64
SSM_STATE = 128
CONV_TAPS = 4
HEAD_LANES = 128
POOL_WINDOWS = (2, 4, 8, 16)
ADAM_LR, ADAM_B1, ADAM_B2, ADAM_EPS, ADAM_WD, ADAM_STEP = 0.001, 0.9, 0.999, 1e-08, 0.01, 10
VMEM_LIMIT = 56 * 1024 * 1024
NEG_BIG = -1e30
HAND_ON_AT = 6

NN = ((1,), (0,))
NT = ((1,), (1,))
TN = ((0,), (0,))
MESH = pl.DeviceIdType.MESH


def _t(dim, pref):
    return pref if dim % pref == 0 else dim


def _row_tile(rows, pref, mult=8):
    best = rows
    for cand in range(mult, min(rows, pref) + 1, mult):
        if rows % cand == 0:
            best = cand
    return best


def _params(sem=None):
    return pltpu.CompilerParams(dimension_semantics=sem, vmem_limit_bytes=VMEM_LIMIT)


def _silu(x):
    return x * (1.0 / (1.0 + jnp.exp(-x)))


def _dsilu(x):
    s = 1.0 / (1.0 + jnp.exp(-x))
    return s * (1.0 + x * (1.0 - s))


def _bdot(a, b, dims=NN):
    return lax.dot_general(a.astype(BF), b.astype(BF), (dims, ((), ())), preferred_element_type=F32)


def _pack_pairs(x):
    h = x.shape[1] // 2
    hi = lax.bitcast_convert_type(x[:, :h].astype(jnp.bfloat16).astype(F32), U32)
    lo = lax.bitcast_convert_type(x[:, h:].astype(jnp.bfloat16).astype(F32), U32)
    return lax.bitcast_convert_type(hi | (lo >> 16), F32)


def _unpack_pairs(w):
    u = lax.bitcast_convert_type(w, U32)
    hi = lax.bitcast_convert_type(u & jnp.uint32(0xFFFF0000), F32)
    lo = lax.bitcast_convert_type(u << 16, F32)
    return hi, lo


def _mesh_pos():
    return lax.axis_index("x"), lax.axis_index("y"), lax.axis_index("c")


def _slot(pos):
    return 4 * pos[0] + 2 * pos[1] + pos[2]


class _Gather:
    def __init__(self, arrays):
        self.arrays = list(arrays)
        self.out_shapes = [jax.ShapeDtypeStruct((N_DEV, *s.shape), s.dtype) for s in arrays]

    def phases(self, src, dst, send_sems, recv_sems, local_sems):
        n_arr = len(self.arrays)
        x, y, c = _mesh_pos()
        me, sibling = (x, y, c), (x, y, 1 - c)
        chips = [(1 - x, y), (x, 1 - y), (1 - x, 1 - y)]

        def copy(a, k, block, to, from_src):
            return pltpu.make_async_remote_copy(
                src_ref=src[a] if from_src else dst[a].at[_slot(block)], dst_ref=dst[a].at[_slot(block)],
                send_sem=send_sems.at[a * 7 + k], recv_sem=recv_sems.at[a * 7 + k], device_id=to, device_id_type=MESH)

        def mine(a):
            return pltpu.make_async_copy(src[a], dst[a].at[_slot(me)], local_sems.at[a])

        def first(a):
            return [copy(a, 0, me, sibling, True)] + [copy(a, 1 + j, me, (*chip, c), True)
                                                     for j, chip in enumerate(chips)]

        def start():
            for a in range(n_arr):
                mine(a).start()
                for cp in first(a):
                    cp.start()

        def middle():
            for j, chip in enumerate(chips):
                for a in range(n_arr):
                    copy(a, 1 + j, (*chip, c), me, False).wait_recv()
                    copy(a, 4 + j, (*chip, c), sibling, False).start()

        def finish():
            for a in range(n_arr):
                copy(a, 0, sibling, me, False).wait_recv()
                for j, chip in enumerate(chips):
                    copy(a, 4 + j, (*chip, 1 - c), me, False).wait_recv()
                for cp in first(a):
                    cp.wait_send()
                for j, chip in enumerate(chips):
                    copy(a, 4 + j, (*chip, c), sibling, False).wait_send()
                mine(a).wait()

        return start, middle, finish


def _hosted_call(name, body, grid, in_specs, out_specs, out_shape, scratch_shapes, sem, operands, exch=None,
                 after=(), into=None):
    if into is not None:
        n_lead = len(in_specs)
        inner = body

        def body(*refs):
            inner(*refs[:n_lead], *refs[n_lead + 1:])

        return pl.pallas_call(
            body, grid=grid, in_specs=[*in_specs, pl.BlockSpec(memory_space=pl.ANY)], out_specs=out_specs,
            out_shape=out_shape, scratch_shapes=scratch_shapes, input_output_aliases={n_lead: 0},
            compiler_params=_params(sem), name=name)(*operands, into)
    if after:
        n_lead = len(in_specs)
        inner = body

        def body(*refs):
            inner(*refs[:n_lead], *refs[n_lead + len(after):])

        in_specs = [*in_specs, *[pl.BlockSpec(memory_space=pl.ANY)] * len(after)]
        operands = (*operands, *after)
    if exch is None:
        return pl.pallas_call(body, grid=grid, in_specs=in_specs, out_specs=out_specs, out_shape=out_shape,
                              scratch_shapes=scratch_shapes, compiler_params=_params(sem), name=name)(*operands)
    n_in, n_out, n_scr, ne = len(in_specs), len(out_specs), len(scratch_shapes), len(exch.arrays)
    total = math.prod(grid)

    def wrapped(*refs):
        ins, ex_in = refs[:n_in], refs[n_in:n_in + ne]
        outs = refs[n_in + ne:n_in + ne + n_out]
        ex_out = refs[n_in + ne + n_out:n_in + 2 * ne + n_out]
        scr = refs[n_in + 2 * ne + n_out:n_in + 2 * ne + n_out + n_scr]
        step = 0
        for axis, size in enumerate(grid):
            step = step * size + pl.program_id(axis)
        start, middle, finish = exch.phases(ex_in, ex_out, *refs[-3:])
        pl.when(step == 0)(start)
        if middle is not None:
            pl.when(step == (total * HAND_ON_AT) // 8)(middle)
        body(*ins, *outs, *scr)
        pl.when(step == total - 1)(finish)

    hbm = pl.BlockSpec(memory_space=pl.ANY)
    sems = [pltpu.SemaphoreType.DMA((ne * 7,)), pltpu.SemaphoreType.DMA((ne * 7,)), pltpu.SemaphoreType.DMA((ne,))]
    return pl.pallas_call(
        wrapped, grid=grid, in_specs=[*in_specs, *[hbm] * ne], out_specs=[*out_specs, *[hbm] * ne],
        out_shape=[*out_shape, *exch.out_shapes], scratch_shapes=[*scratch_shapes, *sems],
        compiler_params=pltpu.CompilerParams(dimension_semantics=("arbitrary",) * len(grid),
                                             vmem_limit_bytes=VMEM_LIMIT, has_side_effects=True),
        name=name)(*operands, *exch.arrays)


def _flip_peers():
    x, y, c = _mesh_pos()
    peers = []
    for k in range(1, N_DEV):
        fx, fy, fc = (k >> 2) & 1, (k >> 1) & 1, k & 1
        peers.append((1 - x if fx else x, 1 - y if fy else y, 1 - c if fc else c))
    return (x, y, c), peers


def _split_scatter_copy(src, land, send_sems, recv_sems, k, me, peer, sending):
    return pltpu.make_async_remote_copy(
        src_ref=src.at[_slot(peer)], dst_ref=land.at[_slot(me) if sending else _slot(peer)],
        send_sem=send_sems.at[k], recv_sem=recv_sems.at[k], device_id=peer, device_id_type=MESH)


def _scatter_start(name, blocks):
    def body(src, land, send_sems, recv_sems, src_thru, land_thru, token):
        me, peers = _flip_peers()
        for k, peer in enumerate(peers):
            _split_scatter_copy(src, land, send_sems, recv_sems, k, me, peer, True).start()
        token[...] = jnp.zeros_like(token)

    hbm = pl.BlockSpec(memory_space=pltpu.HBM)
    sem = pl.BlockSpec(memory_space=pltpu.SEMAPHORE)
    return pl.pallas_call(
        body, name=name,
        out_shape=(pltpu.SemaphoreType.DMA((N_DEV - 1,)), pltpu.SemaphoreType.DMA((N_DEV - 1,)),
                   pltpu.HBM(blocks.shape, blocks.dtype), pltpu.HBM(blocks.shape, blocks.dtype),
                   jax.ShapeDtypeStruct((8, 128), F32)),
        in_specs=(hbm, hbm), out_specs=(sem, sem, hbm, hbm, pl.BlockSpec(memory_space=pltpu.VMEM)),
        input_output_aliases={0: 2, 1: 3},
        compiler_params=pltpu.CompilerParams(has_side_effects=pltpu.SideEffectType.DATAFLOW_SIDE_EFFECTING),
    )(pltpu.with_memory_space_constraint(blocks, pltpu.HBM),
      pltpu.with_memory_space_constraint(lax.empty(blocks.shape, blocks.dtype), pltpu.HBM))


def _scatter_wait(name, send_sems, recv_sems, src_thru, land_thru, after):
    def body(src, land, send_sems, recv_sems, *rest):
        me, peers = _flip_peers()
        for k, peer in enumerate(peers):
            _split_scatter_copy(src, land, send_sems, recv_sems, k, me, peer, True).wait_send()
            _split_scatter_copy(src, land, send_sems, recv_sems, k, me, peer, False).wait_recv()

    hbm = pl.BlockSpec(memory_space=pltpu.HBM)
    sem = pl.BlockSpec(memory_space=pltpu.SEMAPHORE)
    return pl.pallas_call(
        body, name=name,
        out_shape=(pltpu.HBM(src_thru.shape, src_thru.dtype), pltpu.HBM(land_thru.shape, land_thru.dtype)),
        in_specs=(hbm, hbm, sem, sem, *[pl.BlockSpec(memory_space=pl.ANY)] * len(after)), out_specs=(hbm, hbm),
        input_output_aliases={0: 0, 1: 1},
        compiler_params=pltpu.CompilerParams(has_side_effects=pltpu.SideEffectType.DATAFLOW_SIDE_EFFECTING),
    )(src_thru, land_thru, send_sems, recv_sems, *after)


def _exchange_alone(name, exch, after=()):
    def body():
        pass

    return _hosted_call(name, body, (1,), [], [], [], [], None, (), exch, after)


def _mm(name, grid, a, a_spec, b, b_spec, dims, outs, o_specs, acc_shape, extra=(), extra_specs=(), epi=None,
        exch=None, after=(), into=None, summed=None):
    nk = grid[-1]
    n_extra, n_out = len(extra), len(outs)

    def body(*refs):
        a_ref, b_ref = refs[0], refs[1]
        ex = refs[2:2 + n_extra]
        o_refs = refs[2 + n_extra:2 + n_extra + n_out]

        def write(res):
            res = (res,) if epi is None else epi(res, *[e[...] for e in ex])
            for idx, (o, r) in enumerate(zip(o_refs, res)):
                if summed is not None and idx == summed[0]:
                    first = pl.program_id(summed[1]) == 0

                    @pl.when(first)
                    def _(o=o, r=r):
                        o[...] = r.astype(o.dtype)

                    @pl.when(jnp.logical_not(first))
                    def _(o=o, r=r):
                        o[...] += r.astype(o.dtype)
                else:
                    o[...] = r.astype(o.dtype)

        if nk == 1:
            write(_bdot(a_ref[...], b_ref[...], dims))
            return
        acc = refs[-1]
        k = pl.program_id(len(grid) - 1)

        @pl.when(k == 0)
        def _():
            acc[...] = _bdot(a_ref[...], b_ref[...], dims)

        @pl.when(jnp.logical_and(k > 0, k < nk - 1))
        def _():
            acc[...] += _bdot(a_ref[...], b_ref[...], dims)

        @pl.when(k == nk - 1)
        def _():
            write(acc[...] + _bdot(a_ref[...], b_ref[...], dims))

    sem = ("parallel",) * (len(grid) - 1) + ("arbitrary",)
    if summed is not None:
        sem = ("arbitrary",) * len(grid)
    scratch = [] if nk == 1 else [pltpu.VMEM(acc_shape, F32)]
    return _hosted_call(name, body, grid, [a_spec, b_spec, *extra_specs], list(o_specs), list(outs),
                        scratch, sem, (a, b, *extra), exch, after, into)


def _add_epi(acc, add):
    return (acc + add,)


def _pack_epi(acc):
    return (_pack_pairs(acc),)


def _mm_nt(name, a, b, n_off, n, out_dtype=F32, tm=1024, tn=512, exch=None, after=()):
    m, kk = a.shape
    tm, tn = _t(m, tm), math.gcd(_t(n, tn), n_off)
    res = _mm(name, (m // tm, n // tn, 1),
              a, pl.BlockSpec((tm, kk), lambda i, j, k: (i, 0)),
              b, pl.BlockSpec((tn, kk), lambda i, j, k: (n_off // tn + j, 0)), NT,
              [jax.ShapeDtypeStruct((m, n), out_dtype)], [pl.BlockSpec((tm, tn), lambda i, j, k: (i, j))], (tm, tn),
              exch=exch, after=after)
    return res[0] if exch is None else res


def _mm_nn_blocked(name, a, b3, tm=1024, tn=512):
    m, kk = a.shape
    nb, _, cs = b3.shape
    tm, tn = _t(m, tm), _t(cs, tn)
    per = cs // tn
    return _mm(name, (m // tm, nb * per, 1),
               a, pl.BlockSpec((tm, kk), lambda i, j, k: (i, 0)),
               b3, pl.BlockSpec((None, kk, tn), lambda i, j, k: (j // per, 0, j % per)), NN,
               [jax.ShapeDtypeStruct((m, nb * cs), F32)], [pl.BlockSpec((tm, tn), lambda i, j, k: (i, j))],
               (tm, tn))[0]


def _mm_nt_blocked(name, a, b3, after=(), tm=1024, tn=1024):
    m = a.shape[0]
    nb, n, cs = b3.shape
    tm, tn = _t(m, tm), _t(n, tn)
    return _mm(name, (m // tm, n // tn, nb),
               a, pl.BlockSpec((tm, cs), lambda i, j, k: (i, k)),
               b3, pl.BlockSpec((None, tn, cs), lambda i, j, k: (k, j, 0)), NT,
               [jax.ShapeDtypeStruct((m, n), F32)], [pl.BlockSpec((tm, tn), lambda i, j, k: (i, j))], (tm, tn),
               after=after)[0]


def _mm_nn(name, a, b, k_off=0, kk=None, add=None, out_dtype=F32, tm=1024, tn=1024, tk=1024, after=()):
    m = a.shape[0]
    kk = a.shape[1] if kk is None else kk
    n = b.shape[1]
    tm, tn, tk = _t(m, tm), _t(n, tn), math.gcd(_t(kk, tk), k_off)
    extra, especs, epi = (), (), None
    if add is not None:
        extra, especs, epi = (add,), (pl.BlockSpec((tm, tn), lambda i, j, k: (i, j)),), _add_epi
    return _mm(name, (m // tm, n // tn, kk // tk),
               a, pl.BlockSpec((tm, tk), lambda i, j, k: (i, k)),
               b, pl.BlockSpec((tk, tn), lambda i, j, k: (k_off // tk + k, j)), NN,
               [jax.ShapeDtypeStruct((m, n), out_dtype)], [pl.BlockSpec((tm, tn), lambda i, j, k: (i, j))], (tm, tn),
               extra, especs, epi, after=after)[0]


TN_ACC_ELEMENTS = 1 << 20


def _mm_tn_packed(name, a, b, pw, tk=1024, rows=None, into=None):
    kk, m = a.shape
    n = b.shape[1]
    tm, tk = _t(m, TN_ACC_ELEMENTS // pw), _t(kk, tk)
    rows = m if rows is None else rows
    row_blk = 0
    if into is not None:
        rows = into.shape[1]
        assert (rows - m) % tm == 0
        row_blk = (rows - m) // tm
    return _mm(name, (m // tm, n // pw, kk // tk),
               a, pl.BlockSpec((tk, tm), lambda i, j, k: (k, i)),
               b, pl.BlockSpec((tk, pw), lambda i, j, k: (k, j)), TN,
               [jax.ShapeDtypeStruct((n // pw, rows, pw // 2), F32)],
               [pl.BlockSpec((None, tm, pw // 2), lambda i, j, k: (j, row_blk + i, 0))], (tm, pw), epi=_pack_epi,
               into=into)[0]


def _rms_fwd(name, x, g, tm=256):
    l, d = x.shape
    tm = _t(l, tm)

    def body(x_ref, g_ref, h_ref, r_ref):
        xv = x_ref[...]
        r = lax.rsqrt(jnp.mean(xv * xv, axis=-1, keepdims=True) + EPS)
        h_ref[...] = (xv * r * g_ref[...]).astype(BF)
        r_ref[...] = r

    return pl.pallas_call(
        body, grid=(l // tm,),
        in_specs=[pl.BlockSpec((tm, d), lambda i: (i, 0)), pl.BlockSpec((1, d), lambda i: (0, 0))],
        out_specs=[pl.BlockSpec((tm, d), lambda i: (i, 0)), pl.BlockSpec((tm, 1), lambda i: (i, 0))],
        out_shape=[jax.ShapeDtypeStruct((l, d), BF), jax.ShapeDtypeStruct((l, 1), F32)],
        compiler_params=_params(("parallel",)), name=name)(x, g)


def _rms_bwd(name, dh, x, r, g, dres, tm=256):
    l, d = x.shape
    tm = _t(l, tm)

    def body(dh_ref, x_ref, r_ref, g_ref, dres_ref, dx_ref, dxb_ref, dg_ref):
        i = pl.program_id(0)
        rr = r_ref[...]
        xhat = x_ref[...] * rr
        dhv = dh_ref[...]
        dxh = dhv * g_ref[...]
        dx = dres_ref[...] + rr * (dxh - xhat * jnp.mean(dxh * xhat, axis=-1, keepdims=True))
        dx_ref[...] = dx
        dxb_ref[...] = dx.astype(BF)

        @pl.when(i == 0)
        def _():
            dg_ref[...] = jnp.zeros_like(dg_ref)

        dg_ref[...] += jnp.sum(dhv * xhat, axis=0, keepdims=True)

    row = pl.BlockSpec((tm, d), lambda i: (i, 0))
    vec = pl.BlockSpec((1, d), lambda i: (0, 0))
    return pl.pallas_call(
        body, grid=(l // tm,), in_specs=[row, row, pl.BlockSpec((tm, 1), lambda i: (i, 0)), vec, row],
        out_specs=[row, row, vec],
        out_shape=[jax.ShapeDtypeStruct((l, d), F32), jax.ShapeDtypeStruct((l, d), BF),
                   jax.ShapeDtypeStruct((1, d), F32)],
        compiler_params=_params(("arbitrary",)), name=name)(dh, x, r, g, dres)


def _loss_head(x, g, target, tm=256):
    l, d = x.shape
    tm = _t(l, tm)

    def body(x_ref, g_ref, t_ref, dx_ref, dxb_ref, dg_ref, loss_ref):
        i = pl.program_id(0)
        xv = x_ref[...]
        gv = g_ref[...]
        r = lax.rsqrt(jnp.mean(xv * xv, axis=-1, keepdims=True) + EPS)
        xhat = xv * r
        e = xhat * gv - t_ref[...]
        dy = e * (1.0 / d)
        dxh = dy * gv
        dx = r * (dxh - xhat * jnp.mean(dxh * xhat, axis=-1, keepdims=True))
        dx_ref[...] = dx
        dxb_ref[...] = dx.astype(BF)

        @pl.when(i == 0)
        def _():
            dg_ref[...] = jnp.zeros_like(dg_ref)
            loss_ref[...] = jnp.zeros_like(loss_ref)

        dg_ref[...] += jnp.sum(dy * xhat, axis=0, keepdims=True)
        loss_ref[...] += 0.5 * jnp.sum(jnp.sum(e * e, axis=-1, keepdims=True) * (1.0 / d), axis=0, keepdims=True)

    row = pl.BlockSpec((tm, d), lambda i: (i, 0))
    vec = pl.BlockSpec((1, d), lambda i: (0, 0))
    return pl.pallas_call(
        body, grid=(l // tm,), in_specs=[row, vec, row],
        out_specs=[row, row, vec, pl.BlockSpec((1, 1), lambda i: (0, 0))],
        out_shape=[jax.ShapeDtypeStruct((l, d), F32), jax.ShapeDtypeStruct((l, d), BF),
                   jax.ShapeDtypeStruct((1, d), F32), jax.ShapeDtypeStruct((1, 1), F32)],
        compiler_params=_params(("arbitrary",)), name="loss_head")(x, g, target)


CONV_HALO = 8


def _conv_pre(x_ref, w_ref, b_ref, i, tr):
    r0 = pl.multiple_of(i * tr, tr)
    cur = x_ref[pl.ds(r0, tr), :]
    prev = x_ref[pl.ds(pl.multiple_of(jnp.maximum(r0 - CONV_HALO, 0), CONV_HALO), CONV_HALO), :]
    prev = jnp.where(i > 0, prev, 0.0)
    ext = jnp.concatenate([prev, cur], axis=0)
    taps = []
    for k in range(CONV_TAPS):
        s = CONV_TAPS - 1 - k
        taps.append(cur if s == 0 else pltpu.roll(ext, s, 0)[CONV_HALO:])
    pre = b_ref[...] + sum(w_ref[k:k + 1, :] * taps[k] for k in range(CONV_TAPS))
    return r0, pre, taps


def _conv_fwd(pm, col_off, conv_w, conv_b, cw=256, tr=512):
    l = pm.shape[0]
    c = conv_w.shape[1]
    cw, tr = _t(c, cw), _t(l, tr)
    assert col_off % cw == 0

    def body(x_ref, w_ref, b_ref, o_ref):
        def step(i, carry):
            r0, pre, _ = _conv_pre(x_ref, w_ref, b_ref, i, tr)
            o_ref[pl.ds(r0, tr), :] = _silu(pre)
            return carry
        lax.fori_loop(0, l // tr, step, 0)

    return pl.pallas_call(
        body, grid=(c // cw,),
        in_specs=[pl.BlockSpec((l, cw), lambda j: (0, col_off // cw + j)),
                  pl.BlockSpec((CONV_TAPS, cw), lambda j: (0, j)), pl.BlockSpec((1, cw), lambda j: (0, j))],
        out_specs=pl.BlockSpec((l, cw), lambda j: (0, j)), out_shape=jax.ShapeDtypeStruct((l, c), F32),
        compiler_params=_params(("parallel",)), name="conv_fwd")(pm, conv_w, conv_b)


def _conv_bwd(pm, col_off, conv_w, conv_b, dy, dproj, cw=256, tr=512):
    l = pm.shape[0]
    c = conv_w.shape[1]
    cw, tr = _t(c, cw), _t(l, tr)
    nt = l // tr

    def body(x_ref, w_ref, b_ref, dy_ref, _, dx_ref, dw_ref, db_ref, dpre_ref):
        def step1(i, carry):
            dws, db = carry
            r0, pre, taps = _conv_pre(x_ref, w_ref, b_ref, i, tr)
            dpre = dy_ref[pl.ds(r0, tr), :] * _dsilu(pre)
            dpre_ref[pl.ds(r0, tr), :] = dpre
            dws = tuple(dws[k] + jnp.sum(dpre * taps[k], axis=0, keepdims=True) for k in range(CONV_TAPS))
            return dws, db + jnp.sum(dpre, axis=0, keepdims=True)

        z = jnp.zeros((1, cw), F32)
        dws, db = lax.fori_loop(0, nt, step1, ((z,) * CONV_TAPS, z))
        for k in range(CONV_TAPS):
            dw_ref[k:k + 1, :] = dws[k]
        db_ref[...] = db

        def step2(i, carry):
            r0 = pl.multiple_of(i * tr, tr)
            cur = dpre_ref[pl.ds(r0, tr), :]
            nxt = dpre_ref[pl.ds(pl.multiple_of(jnp.minimum(r0 + tr, l - CONV_HALO), CONV_HALO), CONV_HALO), :]
            nxt = jnp.where(i < nt - 1, nxt, 0.0)
            ext = jnp.concatenate([cur, nxt], axis=0)
            acc = w_ref[CONV_TAPS - 1:CONV_TAPS, :] * cur
            for k in range(CONV_TAPS - 1):
                s = CONV_TAPS - 1 - k
                acc = acc + w_ref[k:k + 1, :] * pltpu.roll(ext, tr + CONV_HALO - s, 0)[:tr]
            dx_ref[pl.ds(r0, tr), :] = acc.astype(dx_ref.dtype)
            return carry
        lax.fori_loop(0, nt, step2, 0)

    col = pl.BlockSpec((l, cw), lambda j: (0, j))
    shifted = pl.BlockSpec((l, cw), lambda j: (0, col_off // cw + j))
    return pl.pallas_call(
        body, grid=(c // cw,),
        in_specs=[shifted, pl.BlockSpec((CONV_TAPS, cw), lambda j: (0, j)), pl.BlockSpec((1, cw), lambda j: (0, j)),
                  col, pl.BlockSpec(memory_space=pl.ANY)],
        out_specs=[shifted, pl.BlockSpec((CONV_TAPS, cw), lambda j: (0, j)), pl.BlockSpec((1, cw), lambda j: (0, j))],
        out_shape=[jax.ShapeDtypeStruct(dproj.shape, dproj.dtype), jax.ShapeDtypeStruct((CONV_TAPS, c), F32),
                   jax.ShapeDtypeStruct((1, c), F32)],
        scratch_shapes=[pltpu.VMEM((l, cw), F32)], input_output_aliases={4: 0},
        compiler_params=_params(("parallel",)), name="conv_bwd")(pm, conv_w, conv_b, dy, dproj)


def _split(x, pieces):
    out, rest = [], x
    for _ in range(pieces):
        piece = rest.astype(BF)
        out.append(piece)
        rest = rest - piece.astype(F32)
    return out


def _rows_times(xs, m_stack, pieces):
    x = xs[0] if len(xs) == 1 else jnp.concatenate(xs, axis=0)
    out = _bdot(jnp.concatenate(_split(x, pieces), axis=1), m_stack)
    sizes = [v.shape[0] for v in xs]
    offs = np.cumsum([0] + sizes)
    return [out[offs[i]:offs[i + 1]] for i in range(len(xs))]


def _times_rows(m_stack, x, pieces):
    return _bdot(m_stack, jnp.concatenate(_split(x, pieces), axis=0))


EXPAND_PIECES = 3
FOLD_PIECES = 2


def _ssd_consts(r, p, t):
    assert p == t, "heads expand to P lanes of the inputs and to T lanes of the decay matrices alike"
    rp = r * p
    tri = np.tril(np.ones((t, t), np.float32))
    ep = np.zeros((HEAD_LANES, rp), np.float32)
    ep[np.arange(rp) // p, np.arange(rp)] = 1.0
    itile = (np.arange(t)[:, None] == (np.arange(rp) % t)[None, :]).astype(np.float32)
    lmask = (np.arange(t)[:, None] >= (np.arange(rp) % t)[None, :]).astype(np.float32)
    bmask = ((np.arange(rp) // t)[:, None] == (np.arange(rp) // p)[None, :]).astype(np.float32)
    return [jnp.asarray(np.concatenate([ep] * EXPAND_PIECES, axis=0), BF),
            jnp.asarray(np.concatenate([ep.T] * FOLD_PIECES, axis=0), BF),
            jnp.asarray(np.concatenate([tri] * EXPAND_PIECES, axis=1), BF),
            jnp.asarray(np.concatenate([tri.T] * EXPAND_PIECES, axis=1), BF),
            jnp.asarray(ep.T.copy()), jnp.asarray(itile), jnp.asarray(lmask), jnp.asarray(bmask)]


def _ssd_common(xs, bm, pre_raw, dtb, alog, e_stack, tri_stack, itile, lmask, bmask, r, t):
    pre = pre_raw + dtb
    dt = jnp.maximum(pre, 0.0) + jnp.log(1.0 + jnp.exp(-jnp.abs(pre)))
    a = -jnp.exp(alog)
    cs = _times_rows(tri_stack, dt * a, EXPAND_PIECES)
    last = cs[t - 1:t, :]
    dtx, csx = _rows_times([dt, cs], e_stack, EXPAND_PIECES)
    ecsx = jnp.exp(csx)
    wx = jnp.exp(csx[t - 1:t, :] - csx)
    csrow = jnp.sum(csx * itile, axis=0, keepdims=True)
    lx = jnp.exp(jnp.where(lmask > 0.0, csx - csrow, NEG_BIG))
    xdt = xs * dtx
    xblk = (jnp.concatenate([xdt] * r, axis=0) * bmask).astype(BF)
    btile = jnp.concatenate([bm] * r, axis=0).astype(BF)
    return pre, dt, a, last, dtx, ecsx, wx, lx, xdt, xblk, btile


def _ssd_specs(l, g_n, r, p, n, t, conv, rev):
    nc = l // t
    rp = r * p
    cidx = (lambda c: nc - 1 - c) if rev else (lambda c: c)
    row_spec = lambda width: pl.BlockSpec((t, width), lambda c: (cidx(c), 0))
    dtr_spec = pl.BlockSpec((g_n, t, HEAD_LANES), lambda c: (0, cidx(c), 0))
    par_spec = pl.BlockSpec((g_n, 1, HEAD_LANES), lambda c: (0, 0, 0))
    dskx_spec = pl.BlockSpec((g_n, 1, rp), lambda c: (0, 0, 0))
    st_spec = pl.BlockSpec((None, g_n, rp, n), lambda c: (cidx(c), 0, 0, 0))
    return nc, row_spec, dtr_spec, par_spec, dskx_spec, st_spec


def _const_specs(consts):
    return [pl.BlockSpec(a.shape, lambda c: (0, 0)) for a in consts]


def _ssd_fwd(xbc, dtr, dtb, alog, dskx, pm, norm_g, dims, exch=None):
    h, g_n, r, p, n, t = dims
    l, conv = xbc.shape
    rp, hp = r * p, h * p
    nc, row_spec, dtr_spec, par_spec, dskx_spec, st_spec = _ssd_specs(l, g_n, r, p, n, t, conv, False)
    consts = _ssd_consts(r, p, t)

    def body(x_ref, dtr_ref, dtb_ref, alog_ref, dskx_ref, z_ref, ng_ref,
             e_ref, et_ref, tri_ref, trit_ref, ept_ref, it_ref, lm_ref, bmk_ref,
             y_ref, st_ref, yn_ref, rn_ref, s_ref):
        @pl.when(pl.program_id(0) == 0)
        def _():
            s_ref[...] = jnp.zeros_like(s_ref)

        for g in range(g_n):
            xs = x_ref[:, g * rp:(g + 1) * rp]
            bm = x_ref[:, hp + g * n:hp + (g + 1) * n]
            cm = x_ref[:, hp + (g_n + g) * n:hp + (g_n + g + 1) * n]
            (_, _, _, last, _, ecsx, wx, lx, xdt, xblk, btile) = _ssd_common(
                xs, bm, dtr_ref[g], dtb_ref[g], alog_ref[g], e_ref[...], tri_ref[...], it_ref[...], lm_ref[...],
                bmk_ref[...], r, t)
            s_in = s_ref[g]
            st_ref[g] = s_in
            cbx = _bdot(cm, btile, NT)
            yd = _bdot(cbx * lx, xblk)
            yo = ecsx * _bdot(cm, s_in, NT)
            y_ref[:, g * rp:(g + 1) * rp] = yd + yo + dskx_ref[g] * xs
            elast = jnp.sum(ept_ref[...] * jnp.exp(last), axis=1, keepdims=True)
            s_ref[g] = elast * s_in + _bdot(xdt * wx, bm, TN)

        v = y_ref[...] * _silu(z_ref[...])
        rr = lax.rsqrt(jnp.mean(v * v, axis=-1, keepdims=True) + EPS)
        yn_ref[...] = (v * rr * ng_ref[...]).astype(BF)
        rn_ref[...] = rr

    return _hosted_call(
        "ssd_fwd", body, (nc,),
        [row_spec(conv), dtr_spec, par_spec, par_spec, dskx_spec, row_spec(hp),
         pl.BlockSpec((1, hp), lambda c: (0, 0)), *_const_specs(consts)],
        [row_spec(hp), st_spec, row_spec(hp), row_spec(1)],
        [jax.ShapeDtypeStruct((l, hp), F32), jax.ShapeDtypeStruct((nc, g_n, rp, n), F32),
         jax.ShapeDtypeStruct((l, hp), BF), jax.ShapeDtypeStruct((l, 1), F32)],
        [pltpu.VMEM((g_n, rp, n), F32)], ("arbitrary",),
        (xbc, dtr, dtb, alog, dskx, pm, norm_g, *consts), exch)


def _ssd_bwd(xbc, dtr, dtb, alog, dskx, states, dyn, y, pm, rstd, norm_g, dims, after=()):
    h, g_n, r, p, n, t = dims
    l, conv = xbc.shape
    rp, hp = r * p, h * p
    nc, row_spec, dtr_spec, par_spec, dskx_spec, st_spec = _ssd_specs(l, g_n, r, p, n, t, conv, True)
    consts = _ssd_consts(r, p, t)

    def fold_rows(v, rows):
        return sum(v[k * rows:(k + 1) * rows, :] for k in range(r))

    def body(x_ref, dtr_ref, dtb_ref, alog_ref, dskx_ref, st_ref, dyn_ref, y_ref, z_ref, rn_ref, ng_ref,
             e_ref, et_ref, tri_ref, trit_ref, ept_ref, it_ref, lm_ref, bmk_ref,
             dx_ref, ddt_ref, dbias_ref, dalog_ref, dd_ref, dz_ref, dng_ref, ds_ref, dy_ref):
        @pl.when(pl.program_id(0) == 0)
        def _():
            ds_ref[...] = jnp.zeros_like(ds_ref)
            dbias_ref[...] = jnp.zeros_like(dbias_ref)
            dalog_ref[...] = jnp.zeros_like(dalog_ref)
            dd_ref[...] = jnp.zeros_like(dd_ref)
            dng_ref[...] = jnp.zeros_like(dng_ref)

        yv, zv, rr, dn = y_ref[...], z_ref[...], rn_ref[...], dyn_ref[...]
        sz = _silu(zv)
        vhat = yv * sz * rr
        dvh = dn * ng_ref[...]
        dv = rr * (dvh - vhat * jnp.mean(dvh * vhat, axis=-1, keepdims=True))
        dy_ref[...] = dv * sz
        dz_ref[...] = (dv * yv * _dsilu(zv)).astype(BF)
        dng_ref[...] += jnp.sum(dn * vhat, axis=0, keepdims=True)

        ept, itile, bmask = ept_ref[...], it_ref[...], bmk_ref[...]
        last_row = lax.broadcasted_iota(jnp.int32, (t, HEAD_LANES), 0) == t - 1
        for g in range(g_n):
            xs = x_ref[:, g * rp:(g + 1) * rp]
            bm = x_ref[:, hp + g * n:hp + (g + 1) * n]
            cm = x_ref[:, hp + (g_n + g) * n:hp + (g_n + g + 1) * n]
            dy = dy_ref[:, g * rp:(g + 1) * rp]
            (pre, dt, a, last, dtx, ecsx, wx, lx, xdt, xblk, btile) = _ssd_common(
                xs, bm, dtr_ref[g], dtb_ref[g], alog_ref[g], e_ref[...], tri_ref[...], itile, lm_ref[...], bmask, r, t)
            s_in = st_ref[g]
            ds_out = ds_ref[g]
            elast_row = jnp.exp(last)
            elast = jnp.sum(ept * elast_row, axis=1, keepdims=True)

            cbx = _bdot(cm, btile, NT)
            amat = cbx * lx
            da = _bdot(dy, xblk, NT)
            dxdt = fold_rows(_bdot(amat, dy, TN) * bmask, t)
            dcbx = da * lx
            q = da * amat
            dc = _bdot(dcbx, btile)
            db = fold_rows(_bdot(dcbx, cm, TN), t)

            g0 = _bdot(cm, s_in, NT)
            dg0 = dy * ecsx
            dc = dc + _bdot(dg0, s_in)

            z = _bdot(bm, ds_out, NT)
            dxdt = dxdt + z * wx
            db = db + _bdot(xdt * wx, ds_out)
            ds_ref[g] = elast * ds_out + _bdot(dg0, cm, TN)

            dwx = z * xdt * wx
            rows = jnp.concatenate([jnp.sum(dy * xs, axis=0, keepdims=True), jnp.sum(dwx, axis=0, keepdims=True),
                                    jnp.zeros((14, rp), F32)], axis=0)
            f_cs, f_dt, f_rows = _rows_times(
                [q - itile * jnp.sum(q, axis=0, keepdims=True) + dy * g0 * ecsx - dwx, dxdt * xs, rows],
                et_ref[...], FOLD_PIECES)
            dd_ref[g] += f_rows[0:1]
            dlast = jnp.sum(jnp.sum(ds_out * s_in, axis=1, keepdims=True) * ept, axis=0, keepdims=True) * elast_row
            dlast = dlast + f_rows[1:2]

            dcs = f_cs + jnp.where(last_row, dlast, 0.0)
            dadt = _times_rows(trit_ref[...], dcs, EXPAND_PIECES)
            dalog_ref[g] += jnp.sum(dadt * dt, axis=0, keepdims=True) * a
            dpre = (dadt * a + f_dt) * (1.0 / (1.0 + jnp.exp(-pre)))
            ddt_ref[g] = dpre
            dbias_ref[g] += jnp.sum(dpre, axis=0, keepdims=True)
            dx_ref[:, g * rp:(g + 1) * rp] = dskx_ref[g] * dy + dxdt * dtx
            dx_ref[:, hp + g * n:hp + (g + 1) * n] = db
            dx_ref[:, hp + (g_n + g) * n:hp + (g_n + g + 1) * n] = dc

    par_shape = jax.ShapeDtypeStruct((g_n, 1, HEAD_LANES), F32)
    vec = pl.BlockSpec((1, hp), lambda c: (0, 0))
    return _hosted_call(
        "ssd_bwd", body, (nc,),
        [row_spec(conv), dtr_spec, par_spec, par_spec, dskx_spec, st_spec, row_spec(hp), row_spec(hp), row_spec(hp),
         row_spec(1), vec, *_const_specs(consts)],
        [row_spec(conv), dtr_spec, par_spec, par_spec, par_spec, row_spec(hp), vec],
        [jax.ShapeDtypeStruct((l, conv), F32), jax.ShapeDtypeStruct((g_n, l, HEAD_LANES), F32),
         par_shape, par_shape, par_shape, jax.ShapeDtypeStruct((l, pm.shape[1]), BF),
         jax.ShapeDtypeStruct((1, hp), F32)],
        [pltpu.VMEM((g_n, rp, n), F32), pltpu.VMEM((t, hp), F32)], ("arbitrary",),
        (xbc, dtr, dtb, alog, dskx, states, dyn, y, pm, rstd, norm_g, *consts), after=after)


POOL_HALO = 16


def _pool_mix(name, src, dp, backward, out_dtype, into=None, cw=256, tr=512):
    l = src.shape[0]
    gd = dp // len(POOL_WINDOWS)
    cwl, trl = _t(gd, cw), _t(l, tr)
    nt = l // trl

    def body(x_ref, *rest):
        o_ref = rest[-1]
        gi = pl.program_id(0)
        for wi, win in enumerate(POOL_WINDOWS):
            @pl.when(gi == wi)
            def _(win=win):
                def step(i, carry):
                    r0 = pl.multiple_of(i * trl, trl)
                    cur = x_ref[pl.ds(r0, trl), :]
                    trow = r0 + lax.broadcasted_iota(jnp.int32, (trl, 1), 0)
                    cnt = jnp.minimum(trow + 1, win).astype(F32)
                    if not backward:
                        halo = x_ref[pl.ds(pl.multiple_of(jnp.maximum(r0 - POOL_HALO, 0), POOL_HALO), POOL_HALO), :]
                        halo = jnp.where(i > 0, halo, 0.0)
                        s = jnp.concatenate([halo, cur], axis=0)
                        sh = 1
                        while sh < win:
                            s = s + pltpu.roll(s, sh, 0)
                            sh *= 2
                        res = s[POOL_HALO:] / cnt - cur
                    else:
                        halo = x_ref[pl.ds(pl.multiple_of(jnp.minimum(r0 + trl, l - POOL_HALO), POOL_HALO),
                                           POOL_HALO), :]
                        hrow = r0 + trl + lax.broadcasted_iota(jnp.int32, (POOL_HALO, 1), 0)
                        hcnt = jnp.minimum(hrow + 1, win).astype(F32)
                        halo = jnp.where(i < nt - 1, halo / hcnt, 0.0)
                        s = jnp.concatenate([cur / cnt, halo], axis=0)
                        sh = 1
                        while sh < win:
                            s = s + pltpu.roll(s, trl + POOL_HALO - sh, 0)
                            sh *= 2
                        res = s[:trl] - cur
                    o_ref[pl.ds(r0, trl), :] = res.astype(o_ref.dtype)
                    return carry
                lax.fori_loop(0, nt, step, 0)

    col = pl.BlockSpec((l, cwl), lambda g, j: (0, g * (gd // cwl) + j))
    if into is None:
        return pl.pallas_call(
            body, grid=(len(POOL_WINDOWS), gd // cwl), in_specs=[col], out_specs=col,
            out_shape=jax.ShapeDtypeStruct((l, dp), out_dtype),
            compiler_params=_params(("parallel", "parallel")), name=name)(src)
    return pl.pallas_call(
        body, grid=(len(POOL_WINDOWS), gd // cwl), in_specs=[col, pl.BlockSpec(memory_space=pl.ANY)], out_specs=col,
        out_shape=jax.ShapeDtypeStruct(into.shape, into.dtype), input_output_aliases={1: 0},
        compiler_params=_params(("parallel", "parallel")), name=name)(src, into)


def _pool_out_bwd_data(dx, wp_out, mg, pu, scale, tm=1024, tn=512):
    l, d = dx.shape
    dp = wp_out.shape[0]
    tm, tn = _t(l, tm), _t(dp, tn)

    def epi(dyp, m, gt, sc):
        sg = _silu(gt)
        return dyp * sc * sg, dyp * m * sc * _dsilu(gt), jnp.sum(dyp * m * sg, axis=0, keepdims=True)

    tile = pl.BlockSpec((tm, tn), lambda j, i, k: (i, j))
    right = pl.BlockSpec((tm, tn), lambda j, i, k: (i, dp // tn + j))
    vec = pl.BlockSpec((1, tn), lambda j, i, k: (0, j))
    return _mm("pool_out_bwd_data", (dp // tn, l // tm, 1),
               dx, pl.BlockSpec((tm, d), lambda j, i, k: (i, 0)),
               wp_out, pl.BlockSpec((tn, d), lambda j, i, k: (j, 0)), NT,
               [jax.ShapeDtypeStruct((l, dp), BF), jax.ShapeDtypeStruct((l, 2 * dp), BF),
                jax.ShapeDtypeStruct((1, dp), F32)], [tile, right, vec], (tm, tn),
               (mg, pu, scale), (tile, right, vec), epi, summed=(2, 1))


def _group_mm_fwd(mp, wg, pu, scale, tm=1024, tn=1024, tk=1024):
    l, dp = mp.shape
    ng, gd = wg.shape[0], wg.shape[1]
    tm, tn, tk = _t(l, tm), _t(gd, tn), _t(gd, tk)

    def epi(acc, gate, sc):
        return acc, acc * sc * _silu(gate)

    out = pl.BlockSpec((tm, tn), lambda g, i, j, k: (i, g * (gd // tn) + j))
    return _mm("group_mm_fwd", (ng, l // tm, gd // tn, gd // tk),
               mp, pl.BlockSpec((tm, tk), lambda g, i, j, k: (i, g * (gd // tk) + k)),
               wg, pl.BlockSpec((None, tk, tn), lambda g, i, j, k: (g, k, j)), NN,
               [jax.ShapeDtypeStruct((l, dp), F32), jax.ShapeDtypeStruct((l, dp), BF)], [out, out], (tm, tn),
               (pu, scale),
               (pl.BlockSpec((tm, tn), lambda g, i, j, k: (i, (dp + g * gd) // tn + j)),
                pl.BlockSpec((1, tn), lambda g, i, j, k: (0, g * (gd // tn) + j))), epi)


def _group_mm_bwd_data(dmg, wg, after=(), tm=1024, tn=1024, tk=1024):
    l, dp = dmg.shape
    ng, gd = wg.shape[0], wg.shape[1]
    tm, tn, tk = _t(l, tm), _t(gd, tn), _t(gd, tk)
    return _mm("group_mm_bwd_data", (ng, l // tm, gd // tn, gd // tk),
               dmg, pl.BlockSpec((tm, tk), lambda g, i, j, k: (i, g * (gd // tk) + k)),
               wg, pl.BlockSpec((None, tn, tk), lambda g, i, j, k: (g, j, k)), NT,
               [jax.ShapeDtypeStruct((l, dp), F32)],
               [pl.BlockSpec((tm, tn), lambda g, i, j, k: (i, g * (gd // tn) + j))], (tm, tn), after=after)[0]


def _group_mm_bwd_weight(mp, dmg, ng, tm=512, tk=1024):
    l, dp = mp.shape
    gd = dp // ng
    tm, tk = _t(gd, tm), _t(l, tk)
    return _mm("group_mm_bwd_weight", (ng, gd // tm, l // tk),
               mp, pl.BlockSpec((tk, tm), lambda g, i, k: (k, g * (gd // tm) + i)),
               dmg, pl.BlockSpec((tk, gd), lambda g, i, k: (k, g)), TN,
               [jax.ShapeDtypeStruct((ng, gd, gd // 2), F32)],
               [pl.BlockSpec((None, tm, gd // 2), lambda g, i, k: (g, i, 0))], (tm, gd), epi=_pack_epi)[0]


def _cast_bf16(name, w, tr=256):
    r, c = w.shape
    tr = _row_tile(r, tr, 16)

    def body(w_ref, o_ref):
        o_ref[...] = w_ref[...].astype(BF)

    blk = pl.BlockSpec((tr, c), lambda i: (i, 0))
    return pl.pallas_call(body, grid=(r // tr,), in_specs=[blk], out_specs=blk,
                          out_shape=jax.ShapeDtypeStruct((r, c), BF), compiler_params=_params(("parallel",)),
                          name=name)(w)


def _pack_rows(name, w, tr=256):
    r, c = w.shape
    tr = _row_tile(r, tr)

    def body(w_ref, o_ref):
        o_ref[...] = _pack_pairs(w_ref[...])

    return pl.pallas_call(body, grid=(r // tr,), in_specs=[pl.BlockSpec((tr, c), lambda i: (i, 0))],
                          out_specs=pl.BlockSpec((tr, c // 2), lambda i: (i, 0)),
                          out_shape=jax.ShapeDtypeStruct((r, c // 2), F32), compiler_params=_params(("parallel",)),
                          name=name)(w)


def _unpack_rows(name, w, tr=512):
    r, h = w.shape
    tr = _row_tile(r, tr, 16)

    def body(w_ref, o_ref):
        hi, lo = _unpack_pairs(w_ref[...])
        o_ref[:, :h] = hi.astype(BF)
        o_ref[:, h:] = lo.astype(BF)

    return pl.pallas_call(body, grid=(r // tr,), in_specs=[pl.BlockSpec((tr, h), lambda i: (i, 0))],
                          out_specs=pl.BlockSpec((tr, 2 * h), lambda i: (i, 0)),
                          out_shape=jax.ShapeDtypeStruct((r, 2 * h), BF), compiler_params=_params(("parallel",)),
                          name=name)(w)


def _reduce_packed(name, recv, tr=256):
    nd, r, h = recv.shape
    tr = _row_tile(r, tr)

    def body(p_ref, o_ref):
        hi, lo = _unpack_pairs(p_ref[0])
        for k in range(1, nd):
            a, b = _unpack_pairs(p_ref[k])
            hi, lo = hi + a, lo + b
        o_ref[:, :h] = hi
        o_ref[:, h:] = lo

    return pl.pallas_call(body, grid=(r // tr,), in_specs=[pl.BlockSpec((nd, tr, h), lambda i: (0, i, 0))],
                          out_specs=pl.BlockSpec((tr, 2 * h), lambda i: (i, 0)),
                          out_shape=jax.ShapeDtypeStruct((r, 2 * h), F32), compiler_params=_params(("parallel",)),
                          name=name)(recv)


def _adamw_math(w, g, m, v):
    m2 = ADAM_B1 * m + (1.0 - ADAM_B1) * g
    v2 = ADAM_B2 * v + (1.0 - ADAM_B2) * (g * g)
    m_hat = m2 / (1.0 - ADAM_B1 ** ADAM_STEP)
    v_hat = v2 / (1.0 - ADAM_B2 ** ADAM_STEP)
    delta = -ADAM_LR * (m_hat / (jnp.sqrt(v_hat) + ADAM_EPS) + ADAM_WD * w)
    return delta, m2, v2


def _adamw(name, w, g, m, v, tr=256):
    r, c = w.shape
    tr = _row_tile(r, tr)

    def body(w_ref, g_ref, m_ref, v_ref, d_ref, m2_ref, v2_ref):
        d, m2, v2 = _adamw_math(w_ref[...], g_ref[...], m_ref[...], v_ref[...])
        d_ref[...] = d
        m2_ref[...] = m2
        v2_ref[...] = v2

    blk = pl.BlockSpec((tr, c), lambda i: (i, 0))
    shp = jax.ShapeDtypeStruct((r, c), F32)
    return pl.pallas_call(body, grid=(r // tr,), in_specs=[blk] * 4, out_specs=[blk] * 3, out_shape=[shp] * 3,
                          compiler_params=_params(("parallel",)), name=name)(w, g, m, v)


def _sum_slots(name, a):
    nd, r, c = a.shape

    def body(a_ref, o_ref):
        s = a_ref[0]
        for k in range(1, nd):
            s = s + a_ref[k]
        o_ref[...] = s

    return pl.pallas_call(body, out_shape=jax.ShapeDtypeStruct((r, c), F32), name=name)(a)


def _head_rows(v, g_n, r):
    return jnp.pad(v.reshape(g_n, 1, r), ((0, 0), (0, 0), (0, HEAD_LANES - r)))


class _Later:
    def __init__(self, gather_pool_in, pool_in_of, gather_rest, rest_of, send):
        self.gather_pool_in, self.pool_in_of = gather_pool_in, pool_in_of
        self.gather_rest, self.rest_of, self.send = gather_rest, rest_of, send


def _local_step(x, target, ln_g, final_g, wt_in, conv_w, conv_b, dt_bias, a_log, d_skip, norm_g, scale, later):
    l, d = x.shape
    di = norm_g.shape[1]
    h = dt_bias.shape[1]
    p = di // h
    conv = conv_b.shape[1]
    n, t = SSM_STATE, SSD_CHUNK
    g_n = (conv - di) // (2 * n)
    r = h // g_n
    nm = di + conv
    dp = scale.shape[1]
    ng = len(POOL_WINDOWS)
    dims = (h, g_n, r, p, n, t)
    pw = d

    h0, r0 = _rms_fwd("rms0_fwd", x, ln_g[0:1])
    pm = _mm_nt("in_proj_main", h0, wt_in, 0, nm, exch=later.gather_pool_in)
    if later.gather_pool_in is not None:
        pm, *arrived = pm
    else:
        arrived = None
    wp_in = later.pool_in_of(arrived)
    dtr = _mm_nt("in_proj_dt", h0, wt_in, nm, h, tn=h)
    xbc = _conv_fwd(pm, di, conv_w, conv_b)
    dtb, alog = _head_rows(dt_bias, g_n, r), _head_rows(a_log, g_n, r)
    dskx = jnp.repeat(d_skip.reshape(g_n, 1, r), p, axis=2)
    dtr_g = jnp.pad(jnp.transpose(dtr.reshape(l, g_n, r), (1, 0, 2)), ((0, 0), (0, 0), (0, HEAD_LANES - r)))
    y, states, yn, r_n, *arrived = _ssd_fwd(xbc, dtr_g, dtb, alog, dskx, pm, norm_g, dims, exch=later.gather_rest)
    w_out, wg, wp_out = later.rest_of(arrived)
    x1 = _mm_nn("ssm_out_proj", yn, w_out, add=x)

    h1, r1 = _rms_fwd("rms1_fwd", x1, ln_g[1:2])
    pu = _mm_nn_blocked("pool_in_proj", h1, wp_in)
    mp = _pool_mix("pool_mix_fwd", pu, dp, False, BF)
    mg, yp = _group_mm_fwd(mp, wg, pu, scale)
    x2 = _mm_nn("pool_out_proj", yp, wp_out, add=x1)

    dx2, dx2_b, d_final_g, loss = _loss_head(x2, final_g, target)

    gw_pout = _mm_tn_packed("pool_out_bwd_weight", yp, dx2_b, pw)
    going = later.send("pool_w_out", gw_pout)
    dmg, dpu, d_scale = _pool_out_bwd_data(dx2_b, wp_out, mg, pu, scale)
    dmp = _group_mm_bwd_data(dmg, wg, after=going)
    gw_g = _group_mm_bwd_weight(mp, dmg, ng)
    going = later.send("pool_w_group", gw_g)
    dpu = _pool_mix("pool_mix_bwd", dmp, dp, True, BF, into=dpu)
    dh1 = _mm_nt_blocked("pool_in_bwd_data", dpu, wp_in, after=going)
    gw_pin = _mm_tn_packed("pool_in_bwd_weight", h1, dpu, 2 * dp // N_DEV)
    going = later.send("pool_w_in", gw_pin)
    dx1, dx1_b, d_ln1 = _rms_bwd("rms1_bwd", dh1, x1, r1, ln_g[1:2], dx2)

    dyn = _mm_nt("ssm_out_bwd_data", dx1_b, w_out, 0, di, tn=1024, after=going)
    gw_out = _mm_tn_packed("ssm_out_bwd_weight", yn, dx1_b, pw)
    going = later.send("ssm_w_out", gw_out)
    dxbc, ddt_g, dbias_g, dalog_g, dd_g, dproj, d_norm_g = _ssd_bwd(
        xbc, dtr_g, dtb, alog, dskx, states, dyn, y, pm, r_n, norm_g, dims, after=going)
    dproj, d_conv_w, d_conv_b = _conv_bwd(pm, di, conv_w, conv_b, dxbc, dproj)
    ddt = jnp.transpose(ddt_g[:, :, :r], (1, 0, 2)).reshape(l, h)
    gwt_in = _mm_tn_packed("in_proj_bwd_weight", dproj, h0, pw, rows=nm + h)
    gwt_in = _mm_tn_packed("in_proj_bwd_weight_dt", ddt, h0, pw, into=gwt_in)
    going = later.send("ssm_w_in", gwt_in)
    dh0 = _mm_nn("in_proj_bwd_data_dt", ddt, wt_in, nm, after=going)
    dh0 = _mm_nn("in_proj_bwd_data", dproj, wt_in, 0, add=dh0)
    grad_x, _, d_ln0 = _rms_bwd("rms0_bwd", dh0, x, r0, ln_g[0:1], dx1)

    def heads(v):
        return v[:, 0, :r].reshape(1, h)

    small = dict(ln_g=jnp.concatenate([d_ln0, d_ln1], axis=0), final_g=d_final_g, conv_w=d_conv_w, conv_b=d_conv_b,
                 dt_bias=heads(dbias_g), a_log=heads(dalog_g), d_skip=heads(dd_g), norm_g=d_norm_g, scale=d_scale)
    big = dict(ssm_w_in=gwt_in, ssm_w_out=gw_out, pool_w_in=gw_pin, pool_w_group=gw_g, pool_w_out=gw_pout)
    return loss, grad_x, small, big


SMALL_ORDER = ("ln_g", "final_g", "conv_w", "conv_b", "dt_bias", "a_log", "d_skip", "norm_g", "scale", "loss")


def _flatten_small(parts):
    flat = jnp.concatenate([parts[k].reshape(-1) for k in SMALL_ORDER])
    n = flat.shape[0]
    rows = -(-n // 1024) * 8
    return jnp.pad(flat, (0, rows * 128 - n)).reshape(rows, 128)


def _split_small(flat, shapes):
    flat = flat.reshape(-1)
    out, off = {}, 0
    for k in SMALL_ORDER:
        size = int(np.prod(shapes[k]))
        out[k] = flat[off:off + size].reshape(shapes[k])
        off += size
    return out


def kernel(x, ln_g, final_g, ssm_w_in, ssm_conv_w, ssm_conv_b, ssm_dt_bias, ssm_a_log, ssm_d, ssm_norm_g, ssm_w_out, pool_w_in, pool_w_group, pool_scale, pool_w_out, loss_target, m_ln_g, m_final_g, m_ssm_w_in, m_ssm_conv_w, m_ssm_conv_b, m_ssm_dt_bias, m_ssm_a_log, m_ssm_d, m_ssm_norm_g, m_ssm_w_out, m_pool_w_in, m_pool_w_group, m_pool_scale, m_pool_w_out, v_ln_g, v_final_g, v_ssm_w_in, v_ssm_conv_w, v_ssm_conv_b, v_ssm_dt_bias, v_ssm_a_log, v_ssm_d, v_ssm_norm_g, v_ssm_w_out, v_pool_w_in, v_pool_w_group, v_pool_scale, v_pool_w_out):
    l, d = x.shape[1], x.shape[2]
    me = 4 * lax.axis_index("x") + 2 * lax.axis_index("y") + lax.axis_index("c")
    ng, gds, gd = pool_w_group.shape[1], pool_w_group.shape[2], pool_w_group.shape[3]
    sin_s = ssm_w_in.shape[2]
    dp_s = pool_w_out.shape[1]
    conv_s = ssm_conv_w.shape[2]

    wt_in_s = _pack_rows("pack_w_in", jnp.transpose(ssm_w_in[0]))
    w_out_s = _cast_bf16("cast_w_out", ssm_w_out[0])
    wp_in_s = _cast_bf16("cast_pool_w_in", pool_w_in[0])
    wg_s = _cast_bf16("cast_pool_w_group", pool_w_group[0].reshape(ng * gds, gd))
    wp_out_s = _cast_bf16("cast_pool_w_out", pool_w_out[0])
    small_s = jnp.concatenate([ssm_conv_w[0].reshape(-1), pool_scale[0]])
    n_small = small_s.shape[0]
    small_s = jnp.pad(small_s, (0, -(-n_small // 1024) * 1024 - n_small)).reshape(-1, 128)
    wt_in_g, small_g = _exchange_alone("all_gather_w_in", _Gather([wt_in_s, small_s]))
    wt_in = _unpack_rows("unpack_w_in", wt_in_g.reshape(N_DEV * sin_s, d // 2))
    small_all = small_g.reshape(N_DEV, -1)[:, :n_small]
    conv_w = jnp.transpose(small_all[:, :CONV_TAPS * conv_s].reshape(N_DEV, CONV_TAPS, conv_s), (1, 0, 2))
    conv_w = conv_w.reshape(CONV_TAPS, -1)
    scale = small_all[:, CONV_TAPS * conv_s:].reshape(1, -1)

    def rest_of(arrived):
        w_out_g, wg_g, wp_out_g = arrived
        wg = jnp.transpose(wg_g.reshape(N_DEV, ng, gds, gd), (1, 0, 2, 3)).reshape(ng, gd, gd)
        return w_out_g.reshape(-1, d), wg, wp_out_g.reshape(-1, d)

    def rows_major(gp):
        q, _, hw = gp.shape
        return jnp.transpose(gp.reshape(q, N_DEV, -1, hw), (1, 0, 2, 3))

    to_blocks = dict(ssm_w_in=rows_major, ssm_w_out=rows_major, pool_w_out=rows_major,
                     pool_w_in=lambda gp: gp[:, None],
                     pool_w_group=lambda gp: jnp.transpose(gp.reshape(ng, N_DEV, gds, gd // 2), (1, 0, 2, 3)))
    travelling = {}

    def send(name, gp):
        *travelling[name], token = _scatter_start(f"scatter_{name}_start", to_blocks[name](gp))
        return (token,)

    later = _Later(_Gather([wp_in_s]), lambda arrived: arrived[0], _Gather([w_out_s, wg_s, wp_out_s]), rest_of, send)

    loss, grad_x, small, _ = _local_step(
        x[0], loss_target[0], ln_g, final_g.reshape(1, d), wt_in, conv_w, ssm_conv_b, ssm_dt_bias, ssm_a_log, ssm_d,
        ssm_norm_g, scale, later)

    def arrived(name, after):
        src, land = _scatter_wait(f"scatter_{name}_wait", *travelling[name], after=after)
        own = lax.dynamic_slice_in_dim(src, me, 1, axis=0)
        return lax.dynamic_update_slice_in_dim(land, own, me, axis=0)

    def reduced(name, rv):
        q = rv.shape[1]
        cols = [_reduce_packed(f"{name}_{j}", rv[:, j]) for j in range(q)]
        return cols[0] if q == 1 else jnp.concatenate(cols, axis=1)

    grads = {}
    grads["pool_w_out"] = reduced("reduce_pool_w_out", arrived("pool_w_out", (grad_x,)))[None]
    grads["pool_w_group"] = _reduce_packed("reduce_pool_w_group", arrived("pool_w_group", (grad_x,)).reshape(
        N_DEV, ng * gds, gd // 2)).reshape(1, ng, gds, gd)
    grads["pool_w_in"] = reduced("reduce_pool_w_in", arrived("pool_w_in", (grad_x,)))[None]
    grads["ssm_w_out"] = reduced("reduce_w_out", arrived("ssm_w_out", (grad_x,)))[None]

    weights = dict(ln_g=ln_g, final_g=final_g, ssm_w_in=ssm_w_in, ssm_conv_w=ssm_conv_w, ssm_conv_b=ssm_conv_b,
                   ssm_dt_bias=ssm_dt_bias, ssm_a_log=ssm_a_log, ssm_d=ssm_d, ssm_norm_g=ssm_norm_g,
                   ssm_w_out=ssm_w_out, pool_w_in=pool_w_in, pool_w_group=pool_w_group, pool_scale=pool_scale,
                   pool_w_out=pool_w_out)
    m_in = dict(ln_g=m_ln_g, final_g=m_final_g, ssm_w_in=m_ssm_w_in, ssm_conv_w=m_ssm_conv_w, ssm_conv_b=m_ssm_conv_b,
                ssm_dt_bias=m_ssm_dt_bias, ssm_a_log=m_ssm_a_log, ssm_d=m_ssm_d, ssm_norm_g=m_ssm_norm_g,
                ssm_w_out=m_ssm_w_out, pool_w_in=m_pool_w_in, pool_w_group=m_pool_w_group, pool_scale=m_pool_scale,
                pool_w_out=m_pool_w_out)
    v_in = dict(ln_g=v_ln_g, final_g=v_final_g, ssm_w_in=v_ssm_w_in, ssm_conv_w=v_ssm_conv_w, ssm_conv_b=v_ssm_conv_b,
                ssm_dt_bias=v_ssm_dt_bias, ssm_a_log=v_ssm_a_log, ssm_d=v_ssm_d, ssm_norm_g=v_ssm_norm_g,
                ssm_w_out=v_ssm_w_out, pool_w_in=v_pool_w_in, pool_w_group=v_pool_w_group, pool_scale=v_pool_scale,
                pool_w_out=v_pool_w_out)
    names = list(weights)
    big_names = ("ssm_w_out", "pool_w_in", "pool_w_group", "pool_w_out", "ssm_w_in")
    delta, new_m, new_v = {}, {}, {}

    def adamw_big(k):
        shp = weights[k].shape
        two_d = (-1, shp[-1])
        dk, mk, vk = _adamw(f"adamw_{k}", weights[k].reshape(two_d), grads[k].reshape(two_d), m_in[k].reshape(two_d),
                            v_in[k].reshape(two_d))
        delta[k], new_m[k], new_v[k] = dk.reshape(shp), mk.reshape(shp), vk.reshape(shp)

    for k in big_names[:-1]:
        adamw_big(k)
    small_names = [k for k in names if k not in big_names]

    small["loss"] = loss
    shapes = {k: small[k].shape for k in SMALL_ORDER}
    gathered_small, = _exchange_alone("all_gather_small_grads", _Gather([_flatten_small(small)]),
                                      after=tuple(delta[k] for k in big_names[:-1]))
    summed = _split_small(_sum_slots("sum_small_grads", gathered_small), shapes)
    grads.update(ln_g=summed["ln_g"], final_g=summed["final_g"].reshape(d), ssm_conv_b=summed["conv_b"],
                 ssm_conv_w=lax.dynamic_slice_in_dim(summed["conv_w"], me * conv_s, conv_s, axis=1)[None],
                 ssm_dt_bias=summed["dt_bias"], ssm_a_log=summed["a_log"], ssm_d=summed["d_skip"],
                 ssm_norm_g=summed["norm_g"],
                 pool_scale=lax.dynamic_slice_in_dim(summed["scale"], me * dp_s, dp_s, axis=1))

    def packed(tree):
        flat = jnp.concatenate([tree[k].reshape(-1) for k in small_names])
        nn = flat.shape[0]
        return jnp.pad(flat, (0, -(-nn // 1024) * 1024 - nn), constant_values=1.0).reshape(-1, 128)

    ds, ms, vs = _adamw("adamw_small", packed(weights), packed(grads), packed(m_in), packed(v_in))
    off = 0
    for k in small_names:
        shp = weights[k].shape
        size = int(np.prod(shp))
        for res, arr in ((delta, ds), (new_m, ms), (new_v, vs)):
            res[k] = arr.reshape(-1)[off:off + size].reshape(shp)
        off += size

    k = "ssm_w_in"
    gt = reduced("reduce_w_in", arrived(k, (ds,)))
    dk, mk, vk = _adamw("adamw_ssm_w_in", jnp.transpose(weights[k][0]), gt, jnp.transpose(m_in[k][0]),
                        jnp.transpose(v_in[k][0]))
    grads[k], delta[k], new_m[k], new_v[k] = (jnp.transpose(t)[None] for t in (gt, dk, mk, vk))

    return (summed["loss"].reshape(()), grad_x[None], *[grads[k] for k in names], *[delta[k] for k in names],
            *[new_m[k] for k in names], *[new_v[k] for k in names])
```

```python
import math

import jax
import jax.numpy as jnp
import numpy as np
from jax import lax
from jax.experimental import pallas as pl
from jax.experimental.pallas import tpu as pltpu

F32 = jnp.float32
BF = jnp.bfloat16
U32 = jnp.uint32

N_DEV = 8
EPS = 1e-6
SSD_CHUNK = 64
SSM_STATE = 128
CONV_TAPS = 4
HEAD_LANES = 128
POOL_WINDOWS = (2, 4, 8, 16)
ADAM_LR, ADAM_B1, ADAM_B2, ADAM_EPS, ADAM_WD, ADAM_STEP = 0.001, 0.9, 0.999, 1e-08, 0.01, 10
VMEM_LIMIT = 56 * 1024 * 1024
NEG_BIG = -1e30
HAND_ON_AT = 6

NN = ((1,), (0,))
NT = ((1,), (1,))
TN = ((0,), (0,))
MESH = pl.DeviceIdType.MESH


def _t(dim, pref):
    return pref if dim % pref == 0 else dim


def _row_tile(rows, pref, mult=8):
    best = rows
    for cand in range(mult, min(rows, pref) + 1, mult):
        if rows % cand == 0:
            best = cand
    return best


def _params(sem=None):
    return pltpu.CompilerParams(dimension_semantics=sem, vmem_limit_bytes=VMEM_LIMIT)


def _silu(x):
    return x * (1.0 / (1.0 + jnp.exp(-x)))


def _dsilu(x):
    s = 1.0 / (1.0 + jnp.exp(-x))
    return s * (1.0 + x * (1.0 - s))


def _bdot(a, b, dims=NN):
    return lax.dot_general(a.astype(BF), b.astype(BF), (dims, ((), ())), preferred_element_type=F32)


def _pack_pairs(x):
    h = x.shape[1] // 2
    hi = lax.bitcast_convert_type(x[:, :h].astype(jnp.bfloat16).astype(F32), U32)
    lo = lax.bitcast_convert_type(x[:, h:].astype(jnp.bfloat16).astype(F32), U32)
    return lax.bitcast_convert_type(hi | (lo >> 16), F32)


def _unpack_pairs(w):
    u = lax.bitcast_convert_type(w, U32)
    hi = lax.bitcast_convert_type(u & jnp.uint32(0xFFFF0000), F32)
    lo = lax.bitcast_convert_type(u << 16, F32)
    return hi, lo


def _mesh_pos():
    return lax.axis_index("x"), lax.axis_index("y"), lax.axis_index("c")


def _slot(pos):
    return 4 * pos[0] + 2 * pos[1] + pos[2]


class _Gather:
    def __init__(self, arrays):
        self.arrays = list(arrays)
        self.out_shapes = [jax.ShapeDtypeStruct((N_DEV, *s.shape), s.dtype) for s in arrays]

    def phases(self, src, dst, send_sems, recv_sems, local_sems):
        n_arr = len(self.arrays)
        x, y, c = _mesh_pos()
        me, sibling = (x, y, c), (x, y, 1 - c)
        chips = [(1 - x, y), (x, 1 - y), (1 - x, 1 - y)]

        def copy(a, k, block, to, from_src):
            return pltpu.make_async_remote_copy(
                src_ref=src[a] if from_src else dst[a].at[_slot(block)], dst_ref=dst[a].at[_slot(block)],
                send_sem=send_sems.at[a * 7 + k], recv_sem=recv_sems.at[a * 7 + k], device_id=to, device_id_type=MESH)

        def mine(a):
            return pltpu.make_async_copy(src[a], dst[a].at[_slot(me)], local_sems.at[a])

        def first(a):
            return [copy(a, 0, me, sibling, True)] + [copy(a, 1 + j, me, (*chip, c), True)
                                                     for j, chip in enumerate(chips)]

        def start():
            for a in range(n_arr):
                mine(a).start()
                for cp in first(a):
                    cp.start()

        def middle():
            for j, chip in enumerate(chips):
                for a in range(n_arr):
                    copy(a, 1 + j, (*chip, c), me, False).wait_recv()
                    copy(a, 4 + j, (*chip, c), sibling, False).start()

        def finish():
            for a in range(n_arr):
                copy(a, 0, sibling, me, False).wait_recv()
                for j, chip in enumerate(chips):
                    copy(a, 4 + j, (*chip, 1 - c), me, False).wait_recv()
                for cp in first(a):
                    cp.wait_send()
                for j, chip in enumerate(chips):
                    copy(a, 4 + j, (*chip, c), sibling, False).wait_send()
                mine(a).wait()

        return start, middle, finish


def _hosted_call(name, body, grid, in_specs, out_specs, out_shape, scratch_shapes, sem, operands, exch=None,
                 after=(), into=None):
    if into is not None:
        n_lead = len(in_specs)
        inner = body

        def body(*refs):
            inner(*refs[:n_lead], *refs[n_lead + 1:])

        return pl.pallas_call(
            body, grid=grid, in_specs=[*in_specs, pl.BlockSpec(memory_space=pl.ANY)], out_specs=out_specs,
            out_shape=out_shape, scratch_shapes=scratch_shapes, input_output_aliases={n_lead: 0},
            compiler_params=_params(sem), name=name)(*operands, into)
    if after:
        n_lead = len(in_specs)
        inner = body

        def body(*refs):
            inner(*refs[:n_lead], *refs[n_lead + len(after):])

        in_specs = [*in_specs, *[pl.BlockSpec(memory_space=pl.ANY)] * len(after)]
        operands = (*operands, *after)
    if exch is None:
        return pl.pallas_call(body, grid=grid, in_specs=in_specs, out_specs=out_specs, out_shape=out_shape,
                              scratch_shapes=scratch_shapes, compiler_params=_params(sem), name=name)(*operands)
    n_in, n_out, n_scr, ne = len(in_specs), len(out_specs), len(scratch_shapes), len(exch.arrays)
    total = math.prod(grid)

    def wrapped(*refs):
        ins, ex_in = refs[:n_in], refs[n_in:n_in + ne]
        outs = refs[n_in + ne:n_in + ne + n_out]
        ex_out = refs[n_in + ne + n_out:n_in + 2 * ne + n_out]
        scr = refs[n_in + 2 * ne + n_out:n_in + 2 * ne + n_out + n_scr]
        step = 0
        for axis, size in enumerate(grid):
            step = step * size + pl.program_id(axis)
        start, middle, finish = exch.phases(ex_in, ex_out, *refs[-3:])
        pl.when(step == 0)(start)
        if middle is not None:
            pl.when(step == (total * HAND_ON_AT) // 8)(middle)
        body(*ins, *outs, *scr)
        pl.when(step == total - 1)(finish)

    hbm = pl.BlockSpec(memory_space=pl.ANY)
    sems = [pltpu.SemaphoreType.DMA((ne * 7,)), pltpu.SemaphoreType.DMA((ne * 7,)), pltpu.SemaphoreType.DMA((ne,))]
    return pl.pallas_call(
        wrapped, grid=grid, in_specs=[*in_specs, *[hbm] * ne], out_specs=[*out_specs, *[hbm] * ne],
        out_shape=[*out_shape, *exch.out_shapes], scratch_shapes=[*scratch_shapes, *sems],
        compiler_params=pltpu.CompilerParams(dimension_semantics=("arbitrary",) * len(grid),
                                             vmem_limit_bytes=VMEM_LIMIT, has_side_effects=True),
        name=name)(*operands, *exch.arrays)


def _flip_peers():
    x, y, c = _mesh_pos()
    peers = []
    for k in range(1, N_DEV):
        fx, fy, fc = (k >> 2) & 1, (k >> 1) & 1, k & 1
        peers.append((1 - x if fx else x, 1 - y if fy else y, 1 - c if fc else c))
    return (x, y, c), peers


def _split_scatter_copy(src, land, send_sems, recv_sems, k, me, peer, sending):
    return pltpu.make_async_remote_copy(
        src_ref=src.at[_slot(peer)], dst_ref=land.at[_slot(me) if sending else _slot(peer)],
        send_sem=send_sems.at[k], recv_sem=recv_sems.at[k], device_id=peer, device_id_type=MESH)


def _scatter_start(name, blocks):
    def body(src, land, send_sems, recv_sems, src_thru, land_thru, token):
        me, peers = _flip_peers()
        for k, peer in enumerate(peers):
            _split_scatter_copy(src, land, send_sems, recv_sems, k, me, peer, True).start()
        token[...] = jnp.zeros_like(token)

    hbm = pl.BlockSpec(memory_space=pltpu.HBM)
    sem = pl.BlockSpec(memory_space=pltpu.SEMAPHORE)
    return pl.pallas_call(
        body, name=name,
        out_shape=(pltpu.SemaphoreType.DMA((N_DEV - 1,)), pltpu.SemaphoreType.DMA((N_DEV - 1,)),
                   pltpu.HBM(blocks.shape, blocks.dtype), pltpu.HBM(blocks.shape, blocks.dtype),
                   jax.ShapeDtypeStruct((8, 128), F32)),
        in_specs=(hbm, hbm), out_specs=(sem, sem, hbm, hbm, pl.BlockSpec(memory_space=pltpu.VMEM)),
        input_output_aliases={0: 2, 1: 3},
        compiler_params=pltpu.CompilerParams(has_side_effects=pltpu.SideEffectType.DATAFLOW_SIDE_EFFECTING),
    )(pltpu.with_memory_space_constraint(blocks, pltpu.HBM),
      pltpu.with_memory_space_constraint(lax.empty(blocks.shape, blocks.dtype), pltpu.HBM))


def _scatter_wait(name, send_sems, recv_sems, src_thru, land_thru, after):
    def body(src, land, send_sems, recv_sems, *rest):
        me, peers = _flip_peers()
        for k, peer in enumerate(peers):
            _split_scatter_copy(src, land, send_sems, recv_sems, k, me, peer, True).wait_send()
            _split_scatter_copy(src, land, send_sems, recv_sems, k, me, peer, False).wait_recv()

    hbm = pl.BlockSpec(memory_space=pltpu.HBM)
    sem = pl.BlockSpec(memory_space=pltpu.SEMAPHORE)
    return pl.pallas_call(
        body, name=name,
        out_shape=(pltpu.HBM(src_thru.shape, src_thru.dtype), pltpu.HBM(land_thru.shape, land_thru.dtype)),
        in_specs=(hbm, hbm, sem, sem, *[pl.BlockSpec(memory_space=pl.ANY)] * len(after)), out_specs=(hbm, hbm),
        input_output_aliases={0: 0, 1: 1},
        compiler_params=pltpu.CompilerParams(has_side_effects=pltpu.SideEffectType.DATAFLOW_SIDE_EFFECTING),
    )(src_thru, land_thru, send_sems, recv_sems, *after)


def _exchange_alone(name, exch, after=()):
    def body():
        pass

    return _hosted_call(name, body, (1,), [], [], [], [], None, (), exch, after)


def _mm(name, grid, a, a_spec, b, b_spec, dims, outs, o_specs, acc_shape, extra=(), extra_specs=(), epi=None,
        exch=None, after=(), into=None, summed=None):
    nk = grid[-1]
    n_extra, n_out = len(extra), len(outs)

    def body(*refs):
        a_ref, b_ref = refs[0], refs[1]
        ex = refs[2:2 + n_extra]
        o_refs = refs[2 + n_extra:2 + n_extra + n_out]

        def write(res):
            res = (res,) if epi is None else epi(res, *[e[...] for e in ex])
            for idx, (o, r) in enumerate(zip(o_refs, res)):
                if summed is not None and idx == summed[0]:
                    first = pl.program_id(summed[1]) == 0

                    @pl.when(first)
                    def _(o=o, r=r):
                        o[...] = r.astype(o.dtype)

                    @pl.when(jnp.logical_not(first))
                    def _(o=o, r=r):
                        o[...] += r.astype(o.dtype)
                else:
                    o[...] = r.astype(o.dtype)

        if nk == 1:
            write(_bdot(a_ref[...], b_ref[...], dims))
            return
        acc = refs[-1]
        k = pl.program_id(len(grid) - 1)

        @pl.when(k == 0)
        def _():
            acc[...] = _bdot(a_ref[...], b_ref[...], dims)

        @pl.when(jnp.logical_and(k > 0, k < nk - 1))
        def _():
            acc[...] += _bdot(a_ref[...], b_ref[...], dims)

        @pl.when(k == nk - 1)
        def _():
            write(acc[...] + _bdot(a_ref[...], b_ref[...], dims))

    sem = ("parallel",) * (len(grid) - 1) + ("arbitrary",)
    if summed is not None:
        sem = ("arbitrary",) * len(grid)
    scratch = [] if nk == 1 else [pltpu.VMEM(acc_shape, F32)]
    return _hosted_call(name, body, grid, [a_spec, b_spec, *extra_specs], list(o_specs), list(outs),
                        scratch, sem, (a, b, *extra), exch, after, into)


def _add_epi(acc, add):
    return (acc + add,)


def _pack_epi(acc):
    return (_pack_pairs(acc),)


def _mm_nt(name, a, b, n_off, n, out_dtype=F32, tm=1024, tn=512, exch=None, after=()):
    m, kk = a.shape
    tm, tn = _t(m, tm), math.gcd(_t(n, tn), n_off)
    res = _mm(name, (m // tm, n // tn, 1),
              a, pl.BlockSpec((tm, kk), lambda i, j, k: (i, 0)),
              b, pl.BlockSpec((tn, kk), lambda i, j, k: (n_off // tn + j, 0)), NT,
              [jax.ShapeDtypeStruct((m, n), out_dtype)], [pl.BlockSpec((tm, tn), lambda i, j, k: (i, j))], (tm, tn),
              exch=exch, after=after)
    return res[0] if exch is None else res


def _mm_nt_half(name, a, b2, q, n_off, n, add=None, tm=1024, tn=512, exch=None):
    m = a.shape[0]
    kh = b2.shape[2]
    tm, tn = _t(m, tm), math.gcd(_t(n, tn), n_off)
    extra, especs, epi = (), (), None
    if add is not None:
        extra, especs, epi = (add,), (pl.BlockSpec((tm, tn), lambda i, j, k: (i, j)),), _add_epi
    res = _mm(name, (m // tm, n // tn, 1),
              a, pl.BlockSpec((tm, kh), lambda i, j, k: (i, q)),
              b2, pl.BlockSpec((None, tn, kh), lambda i, j, k: (q, n_off // tn + j, 0)), NT,
              [jax.ShapeDtypeStruct((m, n), F32)], [pl.BlockSpec((tm, tn), lambda i, j, k: (i, j))], (tm, tn),
              extra, especs, epi, exch=exch)
    return res[0] if exch is None else res


def _mm_nn_halves(name, a, b2, k_off=0, add=None, after=(), tm=1024, tk=1024):
    m, kk = a.shape
    half = b2.shape[2]
    tm, tk = _t(m, tm), math.gcd(_t(kk, tk), k_off)
    extra, especs, epi = (), (), None
    if add is not None:
        extra, especs, epi = (add,), (pl.BlockSpec((tm, half), lambda i, j, k: (i, j)),), _add_epi
    return _mm(name, (m // tm, 2, kk // tk),
               a, pl.BlockSpec((tm, tk), lambda i, j, k: (i, k)),
               b2, pl.BlockSpec((None, tk, half), lambda i, j, k: (j, k_off // tk + k, 0)), NN,
               [jax.ShapeDtypeStruct((m, 2 * half), F32)], [pl.BlockSpec((tm, half), lambda i, j, k: (i, j))],
               (tm, half), extra, especs, epi, after=after)[0]


def _mm_nn_blocked(name, a, b3, tm=1024, tn=512):
    m, kk = a.shape
    nb, _, cs = b3.shape
    tm, tn = _t(m, tm), _t(cs, tn)
    per = cs // tn
    return _mm(name, (m // tm, nb * per, 1),
               a, pl.BlockSpec((tm, kk), lambda i, j, k: (i, 0)),
               b3, pl.BlockSpec((None, kk, tn), lambda i, j, k: (j // per, 0, j % per)), NN,
               [jax.ShapeDtypeStruct((m, nb * cs), F32)], [pl.BlockSpec((tm, tn), lambda i, j, k: (i, j))],
               (tm, tn))[0]


def _mm_nt_blocked(name, a, b3, after=(), tm=1024, tn=1024):
    m = a.shape[0]
    nb, n, cs = b3.shape
    tm, tn = _t(m, tm), _t(n, tn)
    return _mm(name, (m // tm, n // tn, nb),
               a, pl.BlockSpec((tm, cs), lambda i, j, k: (i, k)),
               b3, pl.BlockSpec((None, tn, cs), lambda i, j, k: (k, j, 0)), NT,
               [jax.ShapeDtypeStruct((m, n), F32)], [pl.BlockSpec((tm, tn), lambda i, j, k: (i, j))], (tm, tn),
               after=after)[0]


def _mm_nn(name, a, b, k_off=0, kk=None, add=None, out_dtype=F32, tm=1024, tn=1024, tk=1024, after=()):
    m = a.shape[0]
    kk = a.shape[1] if kk is None else kk
    n = b.shape[1]
    tm, tn, tk = _t(m, tm), _t(n, tn), math.gcd(_t(kk, tk), k_off)
    extra, especs, epi = (), (), None
    if add is not None:
        extra, especs, epi = (add,), (pl.BlockSpec((tm, tn), lambda i, j, k: (i, j)),), _add_epi
    return _mm(name, (m // tm, n // tn, kk // tk),
               a, pl.BlockSpec((tm, tk), lambda i, j, k: (i, k)),
               b, pl.BlockSpec((tk, tn), lambda i, j, k: (k_off // tk + k, j)), NN,
               [jax.ShapeDtypeStruct((m, n), out_dtype)], [pl.BlockSpec((tm, tn), lambda i, j, k: (i, j))], (tm, tn),
               extra, especs, epi, after=after)[0]


TN_ACC_ELEMENTS = 1 << 20


def _mm_tn_packed(name, a, b, pw, tk=1024, rows=None, into=None):
    kk, m = a.shape
    n = b.shape[1]
    tm, tk = _t(m, TN_ACC_ELEMENTS // pw), _t(kk, tk)
    rows = m if rows is None else rows
    row_blk = 0
    if into is not None:
        rows = into.shape[1]
        assert (rows - m) % tm == 0
        row_blk = (rows - m) // tm
    return _mm(name, (m // tm, n // pw, kk // tk),
               a, pl.BlockSpec((tk, tm), lambda i, j, k: (k, i)),
               b, pl.BlockSpec((tk, pw), lambda i, j, k: (k, j)), TN,
               [jax.ShapeDtypeStruct((n // pw, rows, pw // 2), F32)],
               [pl.BlockSpec((None, tm, pw // 2), lambda i, j, k: (j, row_blk + i, 0))], (tm, pw), epi=_pack_epi,
               into=into)[0]


def _rms_fwd(name, x, g, tm=256):
    l, d = x.shape
    tm = _t(l, tm)

    def body(x_ref, g_ref, h_ref, r_ref):
        xv = x_ref[...]
        r = lax.rsqrt(jnp.mean(xv * xv, axis=-1, keepdims=True) + EPS)
        h_ref[...] = (xv * r * g_ref[...]).astype(BF)
        r_ref[...] = r

    return pl.pallas_call(
        body, grid=(l // tm,),
        in_specs=[pl.BlockSpec((tm, d), lambda i: (i, 0)), pl.BlockSpec((1, d), lambda i: (0, 0))],
        out_specs=[pl.BlockSpec((tm, d), lambda i: (i, 0)), pl.BlockSpec((tm, 1), lambda i: (i, 0))],
        out_shape=[jax.ShapeDtypeStruct((l, d), BF), jax.ShapeDtypeStruct((l, 1), F32)],
        compiler_params=_params(("parallel",)), name=name)(x, g)


def _rms_bwd(name, dh, x, r, g, dres, tm=256):
    l, d = x.shape
    tm = _t(l, tm)

    def body(dh_ref, x_ref, r_ref, g_ref, dres_ref, dx_ref, dxb_ref, dg_ref):
        i = pl.program_id(0)
        rr = r_ref[...]
        xhat = x_ref[...] * rr
        dhv = dh_ref[...]
        dxh = dhv * g_ref[...]
        dx = dres_ref[...] + rr * (dxh - xhat * jnp.mean(dxh * xhat, axis=-1, keepdims=True))
        dx_ref[...] = dx
        dxb_ref[...] = dx.astype(BF)

        @pl.when(i == 0)
        def _():
            dg_ref[...] = jnp.zeros_like(dg_ref)

        dg_ref[...] += jnp.sum(dhv * xhat, axis=0, keepdims=True)

    row = pl.BlockSpec((tm, d), lambda i: (i, 0))
    vec = pl.BlockSpec((1, d), lambda i: (0, 0))
    return pl.pallas_call(
        body, grid=(l // tm,), in_specs=[row, row, pl.BlockSpec((tm, 1), lambda i: (i, 0)), vec, row],
        out_specs=[row, row, vec],
        out_shape=[jax.ShapeDtypeStruct((l, d), F32), jax.ShapeDtypeStruct((l, d), BF),
                   jax.ShapeDtypeStruct((1, d), F32)],
        compiler_params=_params(("arbitrary",)), name=name)(dh, x, r, g, dres)


def _loss_head(x, g, target, tm=256):
    l, d = x.shape
    tm = _t(l, tm)

    def body(x_ref, g_ref, t_ref, dx_ref, dxb_ref, dg_ref, loss_ref):
        i = pl.program_id(0)
        xv = x_ref[...]
        gv = g_ref[...]
        r = lax.rsqrt(jnp.mean(xv * xv, axis=-1, keepdims=True) + EPS)
        xhat = xv * r
        e = xhat * gv - t_ref[...]
        dy = e * (1.0 / d)
        dxh = dy * gv
        dx = r * (dxh - xhat * jnp.mean(dxh * xhat, axis=-1, keepdims=True))
        dx_ref[...] = dx
        dxb_ref[...] = dx.astype(BF)

        @pl.when(i == 0)
        def _():
            dg_ref[...] = jnp.zeros_like(dg_ref)
            loss_ref[...] = jnp.zeros_like(loss_ref)

        dg_ref[...] += jnp.sum(dy * xhat, axis=0, keepdims=True)
        loss_ref[...] += 0.5 * jnp.sum(jnp.sum(e * e, axis=-1, keepdims=True) * (1.0 / d), axis=0, keepdims=True)

    row = pl.BlockSpec((tm, d), lambda i: (i, 0))
    vec = pl.BlockSpec((1, d), lambda i: (0, 0))
    return pl.pallas_call(
        body, grid=(l // tm,), in_specs=[row, vec, row],
        out_specs=[row, row, vec, pl.BlockSpec((1, 1), lambda i: (0, 0))],
        out_shape=[jax.ShapeDtypeStruct((l, d), F32), jax.ShapeDtypeStruct((l, d), BF),
                   jax.ShapeDtypeStruct((1, d), F32), jax.ShapeDtypeStruct((1, 1), F32)],
        compiler_params=_params(("arbitrary",)), name="loss_head")(x, g, target)


CONV_HALO = 8


def _conv_pre(x_ref, w_ref, b_ref, i, tr):
    r0 = pl.multiple_of(i * tr, tr)
    cur = x_ref[pl.ds(r0, tr), :]
    prev = x_ref[pl.ds(pl.multiple_of(jnp.maximum(r0 - CONV_HALO, 0), CONV_HALO), CONV_HALO), :]
    prev = jnp.where(i > 0, prev, 0.0)
    ext = jnp.concatenate([prev, cur], axis=0)
    taps = []
    for k in range(CONV_TAPS):
        s = CONV_TAPS - 1 - k
        taps.append(cur if s == 0 else pltpu.roll(ext, s, 0)[CONV_HALO:])
    pre = b_ref[...] + sum(w_ref[k:k + 1, :] * taps[k] for k in range(CONV_TAPS))
    return r0, pre, taps


def _conv_fwd(pm, col_off, conv_w, conv_b, cw=256, tr=512, exch=None):
    l = pm.shape[0]
    c = conv_w.shape[1]
    cw, tr = _t(c, cw), _t(l, tr)
    assert col_off % cw == 0

    def body(x_ref, w_ref, b_ref, o_ref):
        def step(i, carry):
            r0, pre, _ = _conv_pre(x_ref, w_ref, b_ref, i, tr)
            o_ref[pl.ds(r0, tr), :] = _silu(pre)
            return carry
        lax.fori_loop(0, l // tr, step, 0)

    res = _hosted_call(
        "conv_fwd", body, (c // cw,),
        [pl.BlockSpec((l, cw), lambda j: (0, col_off // cw + j)),
         pl.BlockSpec((CONV_TAPS, cw), lambda j: (0, j)), pl.BlockSpec((1, cw), lambda j: (0, j))],
        [pl.BlockSpec((l, cw), lambda j: (0, j))], [jax.ShapeDtypeStruct((l, c), F32)], [], ("parallel",),
        (pm, conv_w, conv_b), exch)
    return res[0] if exch is None else res


def _conv_bwd(pm, col_off, conv_w, conv_b, dy, dproj, cw=256, tr=512):
    l = pm.shape[0]
    c = conv_w.shape[1]
    cw, tr = _t(c, cw), _t(l, tr)
    nt = l // tr

    def body(x_ref, w_ref, b_ref, dy_ref, _, dx_ref, dw_ref, db_ref, dpre_ref):
        def step1(i, carry):
            dws, db = carry
            r0, pre, taps = _conv_pre(x_ref, w_ref, b_ref, i, tr)
            dpre = dy_ref[pl.ds(r0, tr), :] * _dsilu(pre)
            dpre_ref[pl.ds(r0, tr), :] = dpre
            dws = tuple(dws[k] + jnp.sum(dpre * taps[k], axis=0, keepdims=True) for k in range(CONV_TAPS))
            return dws, db + jnp.sum(dpre, axis=0, keepdims=True)

        z = jnp.zeros((1, cw), F32)
        dws, db = lax.fori_loop(0, nt, step1, ((z,) * CONV_TAPS, z))
        for k in range(CONV_TAPS):
            dw_ref[k:k + 1, :] = dws[k]
        db_ref[...] = db

        def step2(i, carry):
            r0 = pl.multiple_of(i * tr, tr)
            cur = dpre_ref[pl.ds(r0, tr), :]
            nxt = dpre_ref[pl.ds(pl.multiple_of(jnp.minimum(r0 + tr, l - CONV_HALO), CONV_HALO), CONV_HALO), :]
            nxt = jnp.where(i < nt - 1, nxt, 0.0)
            ext = jnp.concatenate([cur, nxt], axis=0)
            acc = w_ref[CONV_TAPS - 1:CONV_TAPS, :] * cur
            for k in range(CONV_TAPS - 1):
                s = CONV_TAPS - 1 - k
                acc = acc + w_ref[k:k + 1, :] * pltpu.roll(ext, tr + CONV_HALO - s, 0)[:tr]
            dx_ref[pl.ds(r0, tr), :] = acc.astype(dx_ref.dtype)
            return carry
        lax.fori_loop(0, nt, step2, 0)

    col = pl.BlockSpec((l, cw), lambda j: (0, j))
    shifted = pl.BlockSpec((l, cw), lambda j: (0, col_off // cw + j))
    return pl.pallas_call(
        body, grid=(c // cw,),
        in_specs=[shifted, pl.BlockSpec((CONV_TAPS, cw), lambda j: (0, j)), pl.BlockSpec((1, cw), lambda j: (0, j)),
                  col, pl.BlockSpec(memory_space=pl.ANY)],
        out_specs=[shifted, pl.BlockSpec((CONV_TAPS, cw), lambda j: (0, j)), pl.BlockSpec((1, cw), lambda j: (0, j))],
        out_shape=[jax.ShapeDtypeStruct(dproj.shape, dproj.dtype), jax.ShapeDtypeStruct((CONV_TAPS, c), F32),
                   jax.ShapeDtypeStruct((1, c), F32)],
        scratch_shapes=[pltpu.VMEM((l, cw), F32)], input_output_aliases={4: 0},
        compiler_params=_params(("parallel",)), name="conv_bwd")(pm, conv_w, conv_b, dy, dproj)


def _split(x, pieces):
    out, rest = [], x
    for _ in range(pieces):
        piece = rest.astype(BF)
        out.append(piece)
        rest = rest - piece.astype(F32)
    return out


def _rows_times(xs, m_stack, pieces):
    x = xs[0] if len(xs) == 1 else jnp.concatenate(xs, axis=0)
    out = _bdot(jnp.concatenate(_split(x, pieces), axis=1), m_stack)
    sizes = [v.shape[0] for v in xs]
    offs = np.cumsum([0] + sizes)
    return [out[offs[i]:offs[i + 1]] for i in range(len(xs))]


def _times_rows(m_stack, x, pieces):
    return _bdot(m_stack, jnp.concatenate(_split(x, pieces), axis=0))


EXPAND_PIECES = 3
FOLD_PIECES = 2


def _ssd_consts(r, p, t):
    assert p == t, "heads expand to P lanes of the inputs and to T lanes of the decay matrices alike"
    rp = r * p
    tri = np.tril(np.ones((t, t), np.float32))
    ep = np.zeros((HEAD_LANES, rp), np.float32)
    ep[np.arange(rp) // p, np.arange(rp)] = 1.0
    itile = (np.arange(t)[:, None] == (np.arange(rp) % t)[None, :]).astype(np.float32)
    lmask = (np.arange(t)[:, None] >= (np.arange(rp) % t)[None, :]).astype(np.float32)
    bmask = ((np.arange(rp) // t)[:, None] == (np.arange(rp) // p)[None, :]).astype(np.float32)
    return [jnp.asarray(np.concatenate([ep] * EXPAND_PIECES, axis=0), BF),
            jnp.asarray(np.concatenate([ep.T] * FOLD_PIECES, axis=0), BF),
            jnp.asarray(np.concatenate([tri] * EXPAND_PIECES, axis=1), BF),
            jnp.asarray(np.concatenate([tri.T] * EXPAND_PIECES, axis=1), BF),
            jnp.asarray(ep.T.copy()), jnp.asarray(itile), jnp.asarray(lmask), jnp.asarray(bmask)]


def _ssd_common(xs, bm, pre_raw, dtb, alog, e_stack, tri_stack, itile, lmask, bmask, r, t):
    pre = pre_raw + dtb
    dt = jnp.maximum(pre, 0.0) + jnp.log(1.0 + jnp.exp(-jnp.abs(pre)))
    a = -jnp.exp(alog)
    cs = _times_rows(tri_stack, dt * a, EXPAND_PIECES)
    last = cs[t - 1:t, :]
    dtx, csx = _rows_times([dt, cs], e_stack, EXPAND_PIECES)
    ecsx = jnp.exp(csx)
    wx = jnp.exp(csx[t - 1:t, :] - csx)
    csrow = jnp.sum(csx * itile, axis=0, keepdims=True)
    lx = jnp.exp(jnp.where(lmask > 0.0, csx - csrow, NEG_BIG))
    xdt = xs * dtx
    xblk = (jnp.concatenate([xdt] * r, axis=0) * bmask).astype(BF)
    btile = jnp.concatenate([bm] * r, axis=0).astype(BF)
    return pre, dt, a, last, dtx, ecsx, wx, lx, xdt, xblk, btile


def _ssd_specs(l, g_n, r, p, n, t, conv, rev):
    nc = l // t
    rp = r * p
    cidx = (lambda c: nc - 1 - c) if rev else (lambda c: c)
    row_spec = lambda width: pl.BlockSpec((t, width), lambda c: (cidx(c), 0))
    dtr_spec = pl.BlockSpec((g_n, t, HEAD_LANES), lambda c: (0, cidx(c), 0))
    par_spec = pl.BlockSpec((g_n, 1, HEAD_LANES), lambda c: (0, 0, 0))
    dskx_spec = pl.BlockSpec((g_n, 1, rp), lambda c: (0, 0, 0))
    st_spec = pl.BlockSpec((None, g_n, rp, n), lambda c: (cidx(c), 0, 0, 0))
    return nc, row_spec, dtr_spec, par_spec, dskx_spec, st_spec


def _const_specs(consts):
    return [pl.BlockSpec(a.shape, lambda c: (0, 0)) for a in consts]


def _ssd_fwd(xbc, dtr, dtb, alog, dskx, pm, norm_g, dims, exch=None):
    h, g_n, r, p, n, t = dims
    l, conv = xbc.shape
    rp, hp = r * p, h * p
    nc, row_spec, dtr_spec, par_spec, dskx_spec, st_spec = _ssd_specs(l, g_n, r, p, n, t, conv, False)
    consts = _ssd_consts(r, p, t)

    def body(x_ref, dtr_ref, dtb_ref, alog_ref, dskx_ref, z_ref, ng_ref,
             e_ref, et_ref, tri_ref, trit_ref, ept_ref, it_ref, lm_ref, bmk_ref,
             y_ref, st_ref, yn_ref, rn_ref, s_ref):
        @pl.when(pl.program_id(0) == 0)
        def _():
            s_ref[...] = jnp.zeros_like(s_ref)

        for g in range(g_n):
            xs = x_ref[:, g * rp:(g + 1) * rp]
            bm = x_ref[:, hp + g * n:hp + (g + 1) * n]
            cm = x_ref[:, hp + (g_n + g) * n:hp + (g_n + g + 1) * n]
            (_, _, _, last, _, ecsx, wx, lx, xdt, xblk, btile) = _ssd_common(
                xs, bm, dtr_ref[g], dtb_ref[g], alog_ref[g], e_ref[...], tri_ref[...], it_ref[...], lm_ref[...],
                bmk_ref[...], r, t)
            s_in = s_ref[g]
            st_ref[g] = s_in
            cbx = _bdot(cm, btile, NT)
            yd = _bdot(cbx * lx, xblk)
            yo = ecsx * _bdot(cm, s_in, NT)
            y_ref[:, g * rp:(g + 1) * rp] = yd + yo + dskx_ref[g] * xs
            elast = jnp.sum(ept_ref[...] * jnp.exp(last), axis=1, keepdims=True)
            s_ref[g] = elast * s_in + _bdot(xdt * wx, bm, TN)

        v = y_ref[...] * _silu(z_ref[...])
        rr = lax.rsqrt(jnp.mean(v * v, axis=-1, keepdims=True) + EPS)
        yn_ref[...] = (v * rr * ng_ref[...]).astype(BF)
        rn_ref[...] = rr

    return _hosted_call(
        "ssd_fwd", body, (nc,),
        [row_spec(conv), dtr_spec, par_spec, par_spec, dskx_spec, row_spec(hp),
         pl.BlockSpec((1, hp), lambda c: (0, 0)), *_const_specs(consts)],
        [row_spec(hp), st_spec, row_spec(hp), row_spec(1)],
        [jax.ShapeDtypeStruct((l, hp), F32), jax.ShapeDtypeStruct((nc, g_n, rp, n), F32),
         jax.ShapeDtypeStruct((l, hp), BF), jax.ShapeDtypeStruct((l, 1), F32)],
        [pltpu.VMEM((g_n, rp, n), F32)], ("arbitrary",),
        (xbc, dtr, dtb, alog, dskx, pm, norm_g, *consts), exch)


def _ssd_bwd(xbc, dtr, dtb, alog, dskx, states, dyn, y, pm, rstd, norm_g, dims, after=()):
    h, g_n, r, p, n, t = dims
    l, conv = xbc.shape
    rp, hp = r * p, h * p
    nc, row_spec, dtr_spec, par_spec, dskx_spec, st_spec = _ssd_specs(l, g_n, r, p, n, t, conv, True)
    consts = _ssd_consts(r, p, t)

    def fold_rows(v, rows):
        return sum(v[k * rows:(k + 1) * rows, :] for k in range(r))

    def body(x_ref, dtr_ref, dtb_ref, alog_ref, dskx_ref, st_ref, dyn_ref, y_ref, z_ref, rn_ref, ng_ref,
             e_ref, et_ref, tri_ref, trit_ref, ept_ref, it_ref, lm_ref, bmk_ref,
             dx_ref, ddt_ref, dbias_ref, dalog_ref, dd_ref, dz_ref, dng_ref, ds_ref, dy_ref):
        @pl.when(pl.program_id(0) == 0)
        def _():
            ds_ref[...] = jnp.zeros_like(ds_ref)
            dbias_ref[...] = jnp.zeros_like(dbias_ref)
            dalog_ref[...] = jnp.zeros_like(dalog_ref)
            dd_ref[...] = jnp.zeros_like(dd_ref)
            dng_ref[...] = jnp.zeros_like(dng_ref)

        yv, zv, rr, dn = y_ref[...], z_ref[...], rn_ref[...], dyn_ref[...]
        sz = _silu(zv)
        vhat = yv * sz * rr
        dvh = dn * ng_ref[...]
        dv = rr * (dvh - vhat * jnp.mean(dvh * vhat, axis=-1, keepdims=True))
        dy_ref[...] = dv * sz
        dz_ref[...] = (dv * yv * _dsilu(zv)).astype(BF)
        dng_ref[...] += jnp.sum(dn * vhat, axis=0, keepdims=True)

        ept, itile, bmask = ept_ref[...], it_ref[...], bmk_ref[...]
        last_row = lax.broadcasted_iota(jnp.int32, (t, HEAD_LANES), 0) == t - 1
        for g in range(g_n):
            xs = x_ref[:, g * rp:(g + 1) * rp]
            bm = x_ref[:, hp + g * n:hp + (g + 1) * n]
            cm = x_ref[:, hp + (g_n + g) * n:hp + (g_n + g + 1) * n]
            dy = dy_ref[:, g * rp:(g + 1) * rp]
            (pre, dt, a, last, dtx, ecsx, wx, lx, xdt, xblk, btile) = _ssd_common(
                xs, bm, dtr_ref[g], dtb_ref[g], alog_ref[g], e_ref[...], tri_ref[...], itile, lm_ref[...], bmask, r, t)
            s_in = st_ref[g]
            ds_out = ds_ref[g]
            elast_row = jnp.exp(last)
            elast = jnp.sum(ept * elast_row, axis=1, keepdims=True)

            cbx = _bdot(cm, btile, NT)
            amat = cbx * lx
            da = _bdot(dy, xblk, NT)
            dxdt = fold_rows(_bdot(amat, dy, TN) * bmask, t)
            dcbx = da * lx
            q = da * amat
            dc = _bdot(dcbx, btile)
            db = fold_rows(_bdot(dcbx, cm, TN), t)

            g0 = _bdot(cm, s_in, NT)
            dg0 = dy * ecsx
            dc = dc + _bdot(dg0, s_in)

            z = _bdot(bm, ds_out, NT)
            dxdt = dxdt + z * wx
            db = db + _bdot(xdt * wx, ds_out)
            ds_ref[g] = elast * ds_out + _bdot(dg0, cm, TN)

            dwx = z * xdt * wx
            rows = jnp.concatenate([jnp.sum(dy * xs, axis=0, keepdims=True), jnp.sum(dwx, axis=0, keepdims=True),
                                    jnp.zeros((14, rp), F32)], axis=0)
            f_cs, f_dt, f_rows = _rows_times(
                [q - itile * jnp.sum(q, axis=0, keepdims=True) + dy * g0 * ecsx - dwx, dxdt * xs, rows],
                et_ref[...], FOLD_PIECES)
            dd_ref[g] += f_rows[0:1]
            dlast = jnp.sum(jnp.sum(ds_out * s_in, axis=1, keepdims=True) * ept, axis=0, keepdims=True) * elast_row
            dlast = dlast + f_rows[1:2]

            dcs = f_cs + jnp.where(last_row, dlast, 0.0)
            dadt = _times_rows(trit_ref[...], dcs, EXPAND_PIECES)
            dalog_ref[g] += jnp.sum(dadt * dt, axis=0, keepdims=True) * a
            dpre = (dadt * a + f_dt) * (1.0 / (1.0 + jnp.exp(-pre)))
            ddt_ref[g] = dpre
            dbias_ref[g] += jnp.sum(dpre, axis=0, keepdims=True)
            dx_ref[:, g * rp:(g + 1) * rp] = dskx_ref[g] * dy + dxdt * dtx
            dx_ref[:, hp + g * n:hp + (g + 1) * n] = db
            dx_ref[:, hp + (g_n + g) * n:hp + (g_n + g + 1) * n] = dc

    par_shape = jax.ShapeDtypeStruct((g_n, 1, HEAD_LANES), F32)
    vec = pl.BlockSpec((1, hp), lambda c: (0, 0))
    return _hosted_call(
        "ssd_bwd", body, (nc,),
        [row_spec(conv), dtr_spec, par_spec, par_spec, dskx_spec, st_spec, row_spec(hp), row_spec(hp), row_spec(hp),
         row_spec(1), vec, *_const_specs(consts)],
        [row_spec(conv), dtr_spec, par_spec, par_spec, par_spec, row_spec(hp), vec],
        [jax.ShapeDtypeStruct((l, conv), F32), jax.ShapeDtypeStruct((g_n, l, HEAD_LANES), F32),
         par_shape, par_shape, par_shape, jax.ShapeDtypeStruct((l, pm.shape[1]), BF),
         jax.ShapeDtypeStruct((1, hp), F32)],
        [pltpu.VMEM((g_n, rp, n), F32), pltpu.VMEM((t, hp), F32)], ("arbitrary",),
        (xbc, dtr, dtb, alog, dskx, states, dyn, y, pm, rstd, norm_g, *consts), after=after)


POOL_HALO = 16


def _pool_mix(name, src, dp, backward, out_dtype, into=None, cw=256, tr=512):
    l = src.shape[0]
    gd = dp // len(POOL_WINDOWS)
    cwl, trl = _t(gd, cw), _t(l, tr)
    nt = l // trl

    def body(x_ref, *rest):
        o_ref = rest[-1]
        gi = pl.program_id(0)
        for wi, win in enumerate(POOL_WINDOWS):
            @pl.when(gi == wi)
            def _(win=win):
                def step(i, carry):
                    r0 = pl.multiple_of(i * trl, trl)
                    cur = x_ref[pl.ds(r0, trl), :]
                    trow = r0 + lax.broadcasted_iota(jnp.int32, (trl, 1), 0)
                    cnt = jnp.minimum(trow + 1, win).astype(F32)
                    if not backward:
                        halo = x_ref[pl.ds(pl.multiple_of(jnp.maximum(r0 - POOL_HALO, 0), POOL_HALO), POOL_HALO), :]
                        halo = jnp.where(i > 0, halo, 0.0)
                        s = jnp.concatenate([halo, cur], axis=0)
                        sh = 1
                        while sh < win:
                            s = s + pltpu.roll(s, sh, 0)
                            sh *= 2
                        res = s[POOL_HALO:] / cnt - cur
                    else:
                        halo = x_ref[pl.ds(pl.multiple_of(jnp.minimum(r0 + trl, l - POOL_HALO), POOL_HALO),
                                           POOL_HALO), :]
                        hrow = r0 + trl + lax.broadcasted_iota(jnp.int32, (POOL_HALO, 1), 0)
                        hcnt = jnp.minimum(hrow + 1, win).astype(F32)
                        halo = jnp.where(i < nt - 1, halo / hcnt, 0.0)
                        s = jnp.concatenate([cur / cnt, halo], axis=0)
                        sh = 1
                        while sh < win:
                            s = s + pltpu.roll(s, trl + POOL_HALO - sh, 0)
                            sh *= 2
                        res = s[:trl] - cur
                    o_ref[pl.ds(r0, trl), :] = res.astype(o_ref.dtype)
                    return carry
                lax.fori_loop(0, nt, step, 0)

    col = pl.BlockSpec((l, cwl), lambda g, j: (0, g * (gd // cwl) + j))
    if into is None:
        return pl.pallas_call(
            body, grid=(len(POOL_WINDOWS), gd // cwl), in_specs=[col], out_specs=col,
            out_shape=jax.ShapeDtypeStruct((l, dp), out_dtype),
            compiler_params=_params(("parallel", "parallel")), name=name)(src)
    return pl.pallas_call(
        body, grid=(len(POOL_WINDOWS), gd // cwl), in_specs=[col, pl.BlockSpec(memory_space=pl.ANY)], out_specs=col,
        out_shape=jax.ShapeDtypeStruct(into.shape, into.dtype), input_output_aliases={1: 0},
        compiler_params=_params(("parallel", "parallel")), name=name)(src, into)


def _pool_out_bwd_data(dx, wp_out, mg, pu, scale, tm=1024, tn=512):
    l, d = dx.shape
    dp = wp_out.shape[0]
    tm, tn = _t(l, tm), _t(dp, tn)

    def epi(dyp, m, gt, sc):
        sg = _silu(gt)
        return dyp * sc * sg, dyp * m * sc * _dsilu(gt), jnp.sum(dyp * m * sg, axis=0, keepdims=True)

    tile = pl.BlockSpec((tm, tn), lambda j, i, k: (i, j))
    right = pl.BlockSpec((tm, tn), lambda j, i, k: (i, dp // tn + j))
    vec = pl.BlockSpec((1, tn), lambda j, i, k: (0, j))
    return _mm("pool_out_bwd_data", (dp // tn, l // tm, 1),
               dx, pl.BlockSpec((tm, d), lambda j, i, k: (i, 0)),
               wp_out, pl.BlockSpec((tn, d), lambda j, i, k: (j, 0)), NT,
               [jax.ShapeDtypeStruct((l, dp), BF), jax.ShapeDtypeStruct((l, 2 * dp), BF),
                jax.ShapeDtypeStruct((1, dp), F32)], [tile, right, vec], (tm, tn),
               (mg, pu, scale), (tile, right, vec), epi, summed=(2, 1))


def _group_mm_fwd(mp, wg, pu, scale, tm=1024, tn=1024, tk=1024):
    l, dp = mp.shape
    ng, gd = wg.shape[0], wg.shape[1]
    tm, tn, tk = _t(l, tm), _t(gd, tn), _t(gd, tk)

    def epi(acc, gate, sc):
        return acc, acc * sc * _silu(gate)

    out = pl.BlockSpec((tm, tn), lambda g, i, j, k: (i, g * (gd // tn) + j))
    return _mm("group_mm_fwd", (ng, l // tm, gd // tn, gd // tk),
               mp, pl.BlockSpec((tm, tk), lambda g, i, j, k: (i, g * (gd // tk) + k)),
               wg, pl.BlockSpec((None, tk, tn), lambda g, i, j, k: (g, k, j)), NN,
               [jax.ShapeDtypeStruct((l, dp), F32), jax.ShapeDtypeStruct((l, dp), BF)], [out, out], (tm, tn),
               (pu, scale),
               (pl.BlockSpec((tm, tn), lambda g, i, j, k: (i, (dp + g * gd) // tn + j)),
                pl.BlockSpec((1, tn), lambda g, i, j, k: (0, g * (gd // tn) + j))), epi)


def _group_mm_bwd_data(dmg, wg, after=(), tm=1024, tn=1024, tk=1024):
    l, dp = dmg.shape
    ng, gd = wg.shape[0], wg.shape[1]
    tm, tn, tk = _t(l, tm), _t(gd, tn), _t(gd, tk)
    return _mm("group_mm_bwd_data", (ng, l // tm, gd // tn, gd // tk),
               dmg, pl.BlockSpec((tm, tk), lambda g, i, j, k: (i, g * (gd // tk) + k)),
               wg, pl.BlockSpec((None, tn, tk), lambda g, i, j, k: (g, j, k)), NT,
               [jax.ShapeDtypeStruct((l, dp), F32)],
               [pl.BlockSpec((tm, tn), lambda g, i, j, k: (i, g * (gd // tn) + j))], (tm, tn), after=after)[0]


def _group_mm_bwd_weight(mp, dmg, ng, tm=512, tk=1024):
    l, dp = mp.shape
    gd = dp // ng
    tm, tk = _t(gd, tm), _t(l, tk)
    return _mm("group_mm_bwd_weight", (ng, gd // tm, l // tk),
               mp, pl.BlockSpec((tk, tm), lambda g, i, k: (k, g * (gd // tm) + i)),
               dmg, pl.BlockSpec((tk, gd), lambda g, i, k: (k, g)), TN,
               [jax.ShapeDtypeStruct((ng, gd, gd // 2), F32)],
               [pl.BlockSpec((None, tm, gd // 2), lambda g, i, k: (g, i, 0))], (tm, gd), epi=_pack_epi)[0]


def _cast_bf16(name, w, tr=256):
    r, c = w.shape
    tr = _row_tile(r, tr, 16)

    def body(w_ref, o_ref):
        o_ref[...] = w_ref[...].astype(BF)

    blk = pl.BlockSpec((tr, c), lambda i: (i, 0))
    return pl.pallas_call(body, grid=(r // tr,), in_specs=[blk], out_specs=blk,
                          out_shape=jax.ShapeDtypeStruct((r, c), BF), compiler_params=_params(("parallel",)),
                          name=name)(w)


def _pack_rows(name, w, tr=256):
    r, c = w.shape
    tr = _row_tile(r, tr)

    def body(w_ref, o_ref):
        o_ref[...] = _pack_pairs(w_ref[...])

    return pl.pallas_call(body, grid=(r // tr, 2), in_specs=[pl.BlockSpec((tr, c // 2), lambda i, q: (i, q))],
                          out_specs=pl.BlockSpec((None, tr, c // 4), lambda i, q: (q, i, 0)),
                          out_shape=jax.ShapeDtypeStruct((2, r, c // 4), F32),
                          compiler_params=_params(("parallel", "parallel")), name=name)(w)


def _unpack_rows(name, w, q, into=None, tr=512):
    r, h = w.shape
    tr = _row_tile(r, tr, 16)

    def body(w_ref, *rest):
        o_ref = rest[-1]
        hi, lo = _unpack_pairs(w_ref[...])
        o_ref[:, :h] = hi.astype(BF)
        o_ref[:, h:] = lo.astype(BF)

    words = pl.BlockSpec((tr, h), lambda i: (i, 0))
    out = pl.BlockSpec((None, tr, 2 * h), lambda i: (q, i, 0))
    shape = jax.ShapeDtypeStruct((2, r, 2 * h), BF)
    if into is None:
        return pl.pallas_call(body, grid=(r // tr,), in_specs=[words], out_specs=out, out_shape=shape,
                              compiler_params=_params(("parallel",)), name=name)(w)
    return pl.pallas_call(body, grid=(r // tr,), in_specs=[words, pl.BlockSpec(memory_space=pl.ANY)], out_specs=out,
                          out_shape=shape, input_output_aliases={1: 0}, compiler_params=_params(("parallel",)),
                          name=name)(w, into)


def _reduce_packed(name, recv, tr=256):
    nd, r, h = recv.shape
    tr = _row_tile(r, tr)

    def body(p_ref, o_ref):
        hi, lo = _unpack_pairs(p_ref[0])
        for k in range(1, nd):
            a, b = _unpack_pairs(p_ref[k])
            hi, lo = hi + a, lo + b
        o_ref[:, :h] = hi
        o_ref[:, h:] = lo

    return pl.pallas_call(body, grid=(r // tr,), in_specs=[pl.BlockSpec((nd, tr, h), lambda i: (0, i, 0))],
                          out_specs=pl.BlockSpec((tr, 2 * h), lambda i: (i, 0)),
                          out_shape=jax.ShapeDtypeStruct((r, 2 * h), F32), compiler_params=_params(("parallel",)),
                          name=name)(recv)


def _adamw_math(w, g, m, v):
    m2 = ADAM_B1 * m + (1.0 - ADAM_B1) * g
    v2 = ADAM_B2 * v + (1.0 - ADAM_B2) * (g * g)
    m_hat = m2 / (1.0 - ADAM_B1 ** ADAM_STEP)
    v_hat = v2 / (1.0 - ADAM_B2 ** ADAM_STEP)
    delta = -ADAM_LR * (m_hat / (jnp.sqrt(v_hat) + ADAM_EPS) + ADAM_WD * w)
    return delta, m2, v2


def _adamw(name, w, g, m, v, tr=256):
    r, c = w.shape
    tr = _row_tile(r, tr)

    def body(w_ref, g_ref, m_ref, v_ref, d_ref, m2_ref, v2_ref):
        d, m2, v2 = _adamw_math(w_ref[...], g_ref[...], m_ref[...], v_ref[...])
        d_ref[...] = d
        m2_ref[...] = m2
        v2_ref[...] = v2

    blk = pl.BlockSpec((tr, c), lambda i: (i, 0))
    shp = jax.ShapeDtypeStruct((r, c), F32)
    return pl.pallas_call(body, grid=(r // tr,), in_specs=[blk] * 4, out_specs=[blk] * 3, out_shape=[shp] * 3,
                          compiler_params=_params(("parallel",)), name=name)(w, g, m, v)


def _sum_slots(name, a):
    nd, r, c = a.shape

    def body(a_ref, o_ref):
        s = a_ref[0]
        for k in range(1, nd):
            s = s + a_ref[k]
        o_ref[...] = s

    return pl.pallas_call(body, out_shape=jax.ShapeDtypeStruct((r, c), F32), name=name)(a)


def _head_rows(v, g_n, r):
    return jnp.pad(v.reshape(g_n, 1, r), ((0, 0), (0, 0), (0, HEAD_LANES - r)))


class _Later:
    def __init__(self, hosted, second_half, weights_of, send):
        self.hosted, self.second_half, self.weights_of, self.send = hosted, second_half, weights_of, send


def _listed(res):
    return res if isinstance(res, (list, tuple)) else [res]


def _local_step(x, target, ln_g, final_g, wt_in, conv_w, conv_b, dt_bias, a_log, d_skip, norm_g, scale, later):
    l, d = x.shape
    di = norm_g.shape[1]
    h = dt_bias.shape[1]
    p = di // h
    conv = conv_b.shape[1]
    n, t = SSM_STATE, SSD_CHUNK
    g_n = (conv - di) // (2 * n)
    r = h // g_n
    nm = di + conv
    dp = scale.shape[1]
    ng = len(POOL_WINDOWS)
    dims = (h, g_n, r, p, n, t)
    pw = d

    h0, r0 = _rms_fwd("rms0_fwd", x, ln_g[0:1])
    hosted = later.hosted
    pm, *arrived_a = _listed(_mm_nt_half("in_proj_a", h0, wt_in, 0, 0, nm, exch=hosted.get("in_proj_a")))
    dtr = _mm_nt_half("in_proj_dt_a", h0, wt_in, 0, nm, h, tn=h)
    wt_in = later.second_half(wt_in, arrived_a)
    pm, *arrived_b = _listed(_mm_nt_half("in_proj_b", h0, wt_in, 1, 0, nm, add=pm, exch=hosted.get("in_proj_b")))
    dtr = _mm_nt_half("in_proj_dt_b", h0, wt_in, 1, nm, h, add=dtr, tn=h)
    xbc, *arrived_c = _listed(_conv_fwd(pm, di, conv_w, conv_b, exch=hosted.get("conv")))
    dtb, alog = _head_rows(dt_bias, g_n, r), _head_rows(a_log, g_n, r)
    dskx = jnp.repeat(d_skip.reshape(g_n, 1, r), p, axis=2)
    dtr_g = jnp.pad(jnp.transpose(dtr.reshape(l, g_n, r), (1, 0, 2)), ((0, 0), (0, 0), (0, HEAD_LANES - r)))
    y, states, yn, r_n, *arrived_s = _ssd_fwd(xbc, dtr_g, dtb, alog, dskx, pm, norm_g, dims, exch=hosted.get("ssd"))
    w_out, wg, wp_in, wp_out = later.weights_of(arrived_b, arrived_c, arrived_s)
    x1 = _mm_nn("ssm_out_proj", yn, w_out, add=x)

    h1, r1 = _rms_fwd("rms1_fwd", x1, ln_g[1:2])
    pu = _mm_nn_blocked("pool_in_proj", h1, wp_in)
    mp = _pool_mix("pool_mix_fwd", pu, dp, False, BF)
    mg, yp = _group_mm_fwd(mp, wg, pu, scale)
    x2 = _mm_nn("pool_out_proj", yp, wp_out, add=x1)

    dx2, dx2_b, d_final_g, loss = _loss_head(x2, final_g, target)

    gw_pout = _mm_tn_packed("pool_out_bwd_weight", yp, dx2_b, pw)
    going = later.send("pool_w_out", gw_pout)
    dmg, dpu, d_scale = _pool_out_bwd_data(dx2_b, wp_out, mg, pu, scale)
    dmp = _group_mm_bwd_data(dmg, wg, after=going)
    gw_g = _group_mm_bwd_weight(mp, dmg, ng)
    going = later.send("pool_w_group", gw_g)
    dpu = _pool_mix("pool_mix_bwd", dmp, dp, True, BF, into=dpu)
    dh1 = _mm_nt_blocked("pool_in_bwd_data", dpu, wp_in, after=going)
    gw_pin = _mm_tn_packed("pool_in_bwd_weight", h1, dpu, 2 * dp // N_DEV)
    going = later.send("pool_w_in", gw_pin)
    dx1, dx1_b, d_ln1 = _rms_bwd("rms1_bwd", dh1, x1, r1, ln_g[1:2], dx2)

    dyn = _mm_nt("ssm_out_bwd_data", dx1_b, w_out, 0, di, tn=1024, after=going)
    gw_out = _mm_tn_packed("ssm_out_bwd_weight", yn, dx1_b, pw)
    going = later.send("ssm_w_out", gw_out)
    dxbc, ddt_g, dbias_g, dalog_g, dd_g, dproj, d_norm_g = _ssd_bwd(
        xbc, dtr_g, dtb, alog, dskx, states, dyn, y, pm, r_n, norm_g, dims, after=going)
    dproj, d_conv_w, d_conv_b = _conv_bwd(pm, di, conv_w, conv_b, dxbc, dproj)
    ddt = jnp.transpose(ddt_g[:, :, :r], (1, 0, 2)).reshape(l, h)
    gwt_in = _mm_tn_packed("in_proj_bwd_weight", dproj, h0, pw, rows=nm + h)
    gwt_in = _mm_tn_packed("in_proj_bwd_weight_dt", ddt, h0, pw, into=gwt_in)
    going = later.send("ssm_w_in", gwt_in)
    dh0 = _mm_nn_halves("in_proj_bwd_data_dt", ddt, wt_in, nm, after=going)
    dh0 = _mm_nn_halves("in_proj_bwd_data", dproj, wt_in, 0, add=dh0)
    grad_x, _, d_ln0 = _rms_bwd("rms0_bwd", dh0, x, r0, ln_g[0:1], dx1)

    def heads(v):
        return v[:, 0, :r].reshape(1, h)

    small = dict(ln_g=jnp.concatenate([d_ln0, d_ln1], axis=0), final_g=d_final_g, conv_w=d_conv_w, conv_b=d_conv_b,
                 dt_bias=heads(dbias_g), a_log=heads(dalog_g), d_skip=heads(dd_g), norm_g=d_norm_g, scale=d_scale)
    big = dict(ssm_w_in=gwt_in, ssm_w_out=gw_out, pool_w_in=gw_pin, pool_w_group=gw_g, pool_w_out=gw_pout)
    return loss, grad_x, small, big


SMALL_ORDER = ("ln_g", "final_g", "conv_w", "conv_b", "dt_bias", "a_log", "d_skip", "norm_g", "scale", "loss")


def _flatten_small(parts):
    flat = jnp.concatenate([parts[k].reshape(-1) for k in SMALL_ORDER])
    n = flat.shape[0]
    rows = -(-n // 1024) * 8
    return jnp.pad(flat, (0, rows * 128 - n)).reshape(rows, 128)


def _split_small(flat, shapes):
    flat = flat.reshape(-1)
    out, off = {}, 0
    for k in SMALL_ORDER:
        size = int(np.prod(shapes[k]))
        out[k] = flat[off:off + size].reshape(shapes[k])
        off += size
    return out


def kernel(x, ln_g, final_g, ssm_w_in, ssm_conv_w, ssm_conv_b, ssm_dt_bias, ssm_a_log, ssm_d, ssm_norm_g, ssm_w_out, pool_w_in, pool_w_group, pool_scale, pool_w_out, loss_target, m_ln_g, m_final_g, m_ssm_w_in, m_ssm_conv_w, m_ssm_conv_b, m_ssm_dt_bias, m_ssm_a_log, m_ssm_d, m_ssm_norm_g, m_ssm_w_out, m_pool_w_in, m_pool_w_group, m_pool_scale, m_pool_w_out, v_ln_g, v_final_g, v_ssm_w_in, v_ssm_conv_w, v_ssm_conv_b, v_ssm_dt_bias, v_ssm_a_log, v_ssm_d, v_ssm_norm_g, v_ssm_w_out, v_pool_w_in, v_pool_w_group, v_pool_scale, v_pool_w_out):
    l, d = x.shape[1], x.shape[2]
    me = 4 * lax.axis_index("x") + 2 * lax.axis_index("y") + lax.axis_index("c")
    ng, gds, gd = pool_w_group.shape[1], pool_w_group.shape[2], pool_w_group.shape[3]
    sin_s = ssm_w_in.shape[2]
    dp_s = pool_w_out.shape[1]
    conv_s = ssm_conv_w.shape[2]

    wt_in_s = _pack_rows("pack_w_in", jnp.transpose(ssm_w_in[0]))
    w_out_s = _cast_bf16("cast_w_out", ssm_w_out[0])
    wp_in_s = _cast_bf16("cast_pool_w_in", pool_w_in[0])
    wg_s = _cast_bf16("cast_pool_w_group", pool_w_group[0].reshape(ng * gds, gd))
    wp_out_s = _cast_bf16("cast_pool_w_out", pool_w_out[0])
    small_s = jnp.concatenate([ssm_conv_w[0].reshape(-1), pool_scale[0]])
    n_small = small_s.shape[0]
    small_s = jnp.pad(small_s, (0, -(-n_small // 1024) * 1024 - n_small)).reshape(-1, 128)
    wt_in_g, small_g = _exchange_alone("all_gather_w_in", _Gather([wt_in_s[0], small_s]))
    wt_in = _unpack_rows("unpack_w_in_a", wt_in_g.reshape(N_DEV * sin_s, d // 4), 0)
    small_all = small_g.reshape(N_DEV, -1)[:, :n_small]
    conv_w = jnp.transpose(small_all[:, :CONV_TAPS * conv_s].reshape(N_DEV, CONV_TAPS, conv_s), (1, 0, 2))
    conv_w = conv_w.reshape(CONV_TAPS, -1)
    scale = small_all[:, CONV_TAPS * conv_s:].reshape(1, -1)
    hosted = dict(in_proj_a=_Gather([wt_in_s[1]]), in_proj_b=_Gather([w_out_s]), conv=_Gather([wg_s]),
                  ssd=_Gather([wp_in_s, wp_out_s]))

    def second_half(first, arrived):
        return _unpack_rows("unpack_w_in_b", arrived[0].reshape(N_DEV * sin_s, d // 4), 1, into=first)

    def weights_of(arrived_b, arrived_c, arrived_s):
        wg = jnp.transpose(arrived_c[0].reshape(N_DEV, ng, gds, gd), (1, 0, 2, 3)).reshape(ng, gd, gd)
        return arrived_b[0].reshape(-1, d), wg, arrived_s[0], arrived_s[1].reshape(-1, d)

    def rows_major(gp):
        q, _, hw = gp.shape
        return jnp.transpose(gp.reshape(q, N_DEV, -1, hw), (1, 0, 2, 3))

    to_blocks = dict(ssm_w_in=rows_major, ssm_w_out=rows_major, pool_w_out=rows_major,
                     pool_w_in=lambda gp: gp[:, None],
                     pool_w_group=lambda gp: jnp.transpose(gp.reshape(ng, N_DEV, gds, gd // 2), (1, 0, 2, 3)))
    travelling = {}

    def send(name, gp):
        *travelling[name], token = _scatter_start(f"scatter_{name}_start", to_blocks[name](gp))
        return (token,)

    later = _Later(hosted, second_half, weights_of, send)

    loss, grad_x, small, _ = _local_step(
        x[0], loss_target[0], ln_g, final_g.reshape(1, d), wt_in, conv_w, ssm_conv_b, ssm_dt_bias, ssm_a_log, ssm_d,
        ssm_norm_g, scale, later)

    def arrived(name, after):
        src, land = _scatter_wait(f"scatter_{name}_wait", *travelling[name], after=after)
        own = lax.dynamic_slice_in_dim(src, me, 1, axis=0)
        return lax.dynamic_update_slice_in_dim(land, own, me, axis=0)

    def reduced(name, rv):
        q = rv.shape[1]
        cols = [_reduce_packed(f"{name}_{j}", rv[:, j]) for j in range(q)]
        return cols[0] if q == 1 else jnp.concatenate(cols, axis=1)

    grads = {}
    grads["pool_w_out"] = reduced("reduce_pool_w_out", arrived("pool_w_out", (grad_x,)))[None]
    grads["pool_w_group"] = _reduce_packed("reduce_pool_w_group", arrived("pool_w_group", (grad_x,)).reshape(
        N_DEV, ng * gds, gd // 2)).reshape(1, ng, gds, gd)
    grads["pool_w_in"] = reduced("reduce_pool_w_in", arrived("pool_w_in", (grad_x,)))[None]
    grads["ssm_w_out"] = reduced("reduce_w_out", arrived("ssm_w_out", (grad_x,)))[None]

    weights = dict(ln_g=ln_g, final_g=final_g, ssm_w_in=ssm_w_in, ssm_conv_w=ssm_conv_w, ssm_conv_b=ssm_conv_b,
                   ssm_dt_bias=ssm_dt_bias, ssm_a_log=ssm_a_log, ssm_d=ssm_d, ssm_norm_g=ssm_norm_g,
                   ssm_w_out=ssm_w_out, pool_w_in=pool_w_in, pool_w_group=pool_w_group, pool_scale=pool_scale,
                   pool_w_out=pool_w_out)
    m_in = dict(ln_g=m_ln_g, final_g=m_final_g, ssm_w_in=m_ssm_w_in, ssm_conv_w=m_ssm_conv_w, ssm_conv_b=m_ssm_conv_b,
                ssm_dt_bias=m_ssm_dt_bias, ssm_a_log=m_ssm_a_log, ssm_d=m_ssm_d, ssm_norm_g=m_ssm_norm_g,
                ssm_w_out=m_ssm_w_out, pool_w_in=m_pool_w_in, pool_w_group=m_pool_w_group, pool_scale=m_pool_scale,
                pool_w_out=m_pool_w_out)
    v_in = dict(ln_g=v_ln_g, final_g=v_final_g, ssm_w_in=v_ssm_w_in, ssm_conv_w=v_ssm_conv_w, ssm_conv_b=v_ssm_conv_b,
                ssm_dt_bias=v_ssm_dt_bias, ssm_a_log=v_ssm_a_log, ssm_d=v_ssm_d, ssm_norm_g=v_ssm_norm_g,
                ssm_w_out=v_ssm_w_out, pool_w_in=v_pool_w_in, pool_w_group=v_pool_w_group, pool_scale=v_pool_scale,
                pool_w_out=v_pool_w_out)
    names = list(weights)
    big_names = ("ssm_w_out", "pool_w_in", "pool_w_group", "pool_w_out", "ssm_w_in")
    delta, new_m, new_v = {}, {}, {}

    def adamw_big(k):
        shp = weights[k].shape
        two_d = (-1, shp[-1])
        dk, mk, vk = _adamw(f"adamw_{k}", weights[k].reshape(two_d), grads[k].reshape(two_d), m_in[k].reshape(two_d),
                            v_in[k].reshape(two_d))
        delta[k], new_m[k], new_v[k] = dk.reshape(shp), mk.reshape(shp), vk.reshape(shp)

    for k in big_names[:-1]:
        adamw_big(k)
    small_names = [k for k in names if k not in big_names]

    small["loss"] = loss
    shapes = {k: small[k].shape for k in SMALL_ORDER}
    gathered_small, = _exchange_alone("all_gather_small_grads", _Gather([_flatten_small(small)]),
                                      after=tuple(delta[k] for k in big_names[:-1]))
    summed = _split_small(_sum_slots("sum_small_grads", gathered_small), shapes)
    grads.update(ln_g=summed["ln_g"], final_g=summed["final_g"].reshape(d), ssm_conv_b=summed["conv_b"],
                 ssm_conv_w=lax.dynamic_slice_in_dim(summed["conv_w"], me * conv_s, conv_s, axis=1)[None],
                 ssm_dt_bias=summed["dt_bias"], ssm_a_log=summed["a_log"], ssm_d=summed["d_skip"],
                 ssm_norm_g=summed["norm_g"],
                 pool_scale=lax.dynamic_slice_in_dim(summed["scale"], me * dp_s, dp_s, axis=1))

    def packed(tree):
        flat = jnp.concatenate([tree[k].reshape(-1) for k in small_names])
        nn = flat.shape[0]
        return jnp.pad(flat, (0, -(-nn // 1024) * 1024 - nn), constant_values=1.0).reshape(-1, 128)

    ds, ms, vs = _adamw("adamw_small", packed(weights), packed(grads), packed(m_in), packed(v_in))
    off = 0
    for k in small_names:
        shp = weights[k].shape
        size = int(np.prod(shp))
        for res, arr in ((delta, ds), (new_m, ms), (new_v, vs)):
            res[k] = arr.reshape(-1)[off:off + size].reshape(shp)
        off += size

    k = "ssm_w_in"
    gt = reduced("reduce_w_in", arrived(k, (ds,)))
    dk, mk, vk = _adamw("adamw_ssm_w_in", jnp.transpose(weights[k][0]), gt, jnp.transpose(m_in[k][0]),
                        jnp.transpose(v_in[k][0]))
    grads[k], delta[k], new_m[k], new_v[k] = (jnp.transpose(t)[None] for t in (gt, dk, mk, vk))

    return (summed["loss"].reshape(()), grad_x[None], *[grads[k] for k in names], *[delta[k] for k in names],
            *[new_m[k] for k in names], *[new_v[k] for k in names])
```

```python
import math

import jax
import jax.numpy as jnp
import numpy as np
from jax import lax
from jax.experimental import pallas as pl
from jax.experimental.pallas import tpu as pltpu

F32 = jnp.float32
BF = jnp.bfloat16
U32 = jnp.uint32

N_DEV = 8
EPS = 1e-6
SSD_CHUNK = 64
SSM_STATE = 128
CONV_TAPS = 4
HEAD_LANES = 128
POOL_WINDOWS = (2, 4, 8, 16)
ADAM_LR, ADAM_B1, ADAM_B2, ADAM_EPS, ADAM_WD, ADAM_STEP = 0.001, 0.9, 0.999, 1e-08, 0.01, 10
VMEM_LIMIT = 56 * 1024 * 1024
NEG_BIG = -1e30
HAND_ON_AT = 6

NN = ((1,), (0,))
NT = ((1,), (1,))
TN = ((0,), (0,))
MESH = pl.DeviceIdType.MESH


def _t(dim, pref):
    return pref if dim % pref == 0 else dim


def _row_tile(rows, pref, mult=8):
    best = rows
    for cand in range(mult, min(rows, pref) + 1, mult):
        if rows % cand == 0:
            best = cand
    return best


def _params(sem=None):
    return pltpu.CompilerParams(dimension_semantics=sem, vmem_limit_bytes=VMEM_LIMIT)


def _silu(x):
    return x * (1.0 / (1.0 + jnp.exp(-x)))


def _dsilu(x):
    s = 1.0 / (1.0 + jnp.exp(-x))
    return s * (1.0 + x * (1.0 - s))


def _bdot(a, b, dims=NN):
    return lax.dot_general(a.astype(BF), b.astype(BF), (dims, ((), ())), preferred_element_type=F32)


def _pack_pairs(x):
    h = x.shape[1] // 2
    hi = lax.bitcast_convert_type(x[:, :h].astype(jnp.bfloat16).astype(F32), U32)
    lo = lax.bitcast_convert_type(x[:, h:].astype(jnp.bfloat16).astype(F32), U32)
    return lax.bitcast_convert_type(hi | (lo >> 16), F32)


def _unpack_pairs(w):
    u = lax.bitcast_convert_type(w, U32)
    hi = lax.bitcast_convert_type(u & jnp.uint32(0xFFFF0000), F32)
    lo = lax.bitcast_convert_type(u << 16, F32)
    return hi, lo


def _mesh_pos():
    return lax.axis_index("x"), lax.axis_index("y"), lax.axis_index("c")


def _slot(pos):
    return 4 * pos[0] + 2 * pos[1] + pos[2]


class _Gather:
    def __init__(self, arrays):
        self.arrays = list(arrays)
        self.out_shapes = [jax.ShapeDtypeStruct((N_DEV, *s.shape), s.dtype) for s in arrays]

    def phases(self, src, dst, send_sems, recv_sems, local_sems):
        n_arr = len(self.arrays)
        x, y, c = _mesh_pos()
        me, sibling = (x, y, c), (x, y, 1 - c)
        chips = [(1 - x, y), (x, 1 - y), (1 - x, 1 - y)]

        def copy(a, k, block, to, from_src):
            return pltpu.make_async_remote_copy(
                src_ref=src[a] if from_src else dst[a].at[_slot(block)], dst_ref=dst[a].at[_slot(block)],
                send_sem=send_sems.at[a * 7 + k], recv_sem=recv_sems.at[a * 7 + k], device_id=to, device_id_type=MESH)

        def mine(a):
            return pltpu.make_async_copy(src[a], dst[a].at[_slot(me)], local_sems.at[a])

        def first(a):
            return [copy(a, 0, me, sibling, True)] + [copy(a, 1 + j, me, (*chip, c), True)
                                                     for j, chip in enumerate(chips)]

        def start():
            for a in range(n_arr):
                mine(a).start()
                for cp in first(a):
                    cp.start()

        def middle():
            for j, chip in enumerate(chips):
                for a in range(n_arr):
                    copy(a, 1 + j, (*chip, c), me, False).wait_recv()
                    copy(a, 4 + j, (*chip, c), sibling, False).start()

        def finish():
            for a in range(n_arr):
                copy(a, 0, sibling, me, False).wait_recv()
                for j, chip in enumerate(chips):
                    copy(a, 4 + j, (*chip, 1 - c), me, False).wait_recv()
                for cp in first(a):
                    cp.wait_send()
                for j, chip in enumerate(chips):
                    copy(a, 4 + j, (*chip, c), sibling, False).wait_send()
                mine(a).wait()

        return start, middle, finish


def _hosted_call(name, body, grid, in_specs, out_specs, out_shape, scratch_shapes, sem, operands, exch=None,
                 after=(), into=None):
    if into is not None:
        n_lead = len(in_specs)
        inner = body

        def body(*refs):
            inner(*refs[:n_lead], *refs[n_lead + 1:])

        return pl.pallas_call(
            body, grid=grid, in_specs=[*in_specs, pl.BlockSpec(memory_space=pl.ANY)], out_specs=out_specs,
            out_shape=out_shape, scratch_shapes=scratch_shapes, input_output_aliases={n_lead: 0},
            compiler_params=_params(sem), name=name)(*operands, into)
    if after:
        n_lead = len(in_specs)
        inner = body

        def body(*refs):
            inner(*refs[:n_lead], *refs[n_lead + len(after):])

        in_specs = [*in_specs, *[pl.BlockSpec(memory_space=pl.ANY)] * len(after)]
        operands = (*operands, *after)
    if exch is None:
        return pl.pallas_call(body, grid=grid, in_specs=in_specs, out_specs=out_specs, out_shape=out_shape,
                              scratch_shapes=scratch_shapes, compiler_params=_params(sem), name=name)(*operands)
    n_in, n_out, n_scr, ne = len(in_specs), len(out_specs), len(scratch_shapes), len(exch.arrays)
    total = math.prod(grid)

    def wrapped(*refs):
        ins, ex_in = refs[:n_in], refs[n_in:n_in + ne]
        outs = refs[n_in + ne:n_in + ne + n_out]
        ex_out = refs[n_in + ne + n_out:n_in + 2 * ne + n_out]
        scr = refs[n_in + 2 * ne + n_out:n_in + 2 * ne + n_out + n_scr]
        step = 0
        for axis, size in enumerate(grid):
            step = step * size + pl.program_id(axis)
        start, middle, finish = exch.phases(ex_in, ex_out, *refs[-3:])
        pl.when(step == 0)(start)
        if middle is not None:
            pl.when(step == (total * HAND_ON_AT) // 8)(middle)
        body(*ins, *outs, *scr)
        pl.when(step == total - 1)(finish)

    hbm = pl.BlockSpec(memory_space=pl.ANY)
    sems = [pltpu.SemaphoreType.DMA((ne * 7,)), pltpu.SemaphoreType.DMA((ne * 7,)), pltpu.SemaphoreType.DMA((ne,))]
    return pl.pallas_call(
        wrapped, grid=grid, in_specs=[*in_specs, *[hbm] * ne], out_specs=[*out_specs, *[hbm] * ne],
        out_shape=[*out_shape, *exch.out_shapes], scratch_shapes=[*scratch_shapes, *sems],
        compiler_params=pltpu.CompilerParams(dimension_semantics=("arbitrary",) * len(grid),
                                             vmem_limit_bytes=VMEM_LIMIT, has_side_effects=True),
        name=name)(*operands, *exch.arrays)


def _flip_peers():
    x, y, c = _mesh_pos()
    peers = []
    for k in range(1, N_DEV):
        fx, fy, fc = (k >> 2) & 1, (k >> 1) & 1, k & 1
        peers.append((1 - x if fx else x, 1 - y if fy else y, 1 - c if fc else c))
    return (x, y, c), peers


def _split_scatter_copy(src, land, send_sems, recv_sems, k, me, peer, sending):
    return pltpu.make_async_remote_copy(
        src_ref=src.at[_slot(peer)], dst_ref=land.at[_slot(me) if sending else _slot(peer)],
        send_sem=send_sems.at[k], recv_sem=recv_sems.at[k], device_id=peer, device_id_type=MESH)


def _scatter_start(name, blocks):
    def body(src, land, send_sems, recv_sems, src_thru, land_thru, token):
        me, peers = _flip_peers()
        for k, peer in enumerate(peers):
            _split_scatter_copy(src, land, send_sems, recv_sems, k, me, peer, True).start()
        token[...] = jnp.zeros_like(token)

    hbm = pl.BlockSpec(memory_space=pltpu.HBM)
    sem = pl.BlockSpec(memory_space=pltpu.SEMAPHORE)
    return pl.pallas_call(
        body, name=name,
        out_shape=(pltpu.SemaphoreType.DMA((N_DEV - 1,)), pltpu.SemaphoreType.DMA((N_DEV - 1,)),
                   pltpu.HBM(blocks.shape, blocks.dtype), pltpu.HBM(blocks.shape, blocks.dtype),
                   jax.ShapeDtypeStruct((8, 128), F32)),
        in_specs=(hbm, hbm), out_specs=(sem, sem, hbm, hbm, pl.BlockSpec(memory_space=pltpu.VMEM)),
        input_output_aliases={0: 2, 1: 3},
        compiler_params=pltpu.CompilerParams(has_side_effects=pltpu.SideEffectType.DATAFLOW_SIDE_EFFECTING),
    )(pltpu.with_memory_space_constraint(blocks, pltpu.HBM),
      pltpu.with_memory_space_constraint(lax.empty(blocks.shape, blocks.dtype), pltpu.HBM))


def _scatter_wait(name, send_sems, recv_sems, src_thru, land_thru, after):
    def body(src, land, send_sems, recv_sems, *rest):
        me, peers = _flip_peers()
        for k, peer in enumerate(peers):
            _split_scatter_copy(src, land, send_sems, recv_sems, k, me, peer, True).wait_send()
            _split_scatter_copy(src, land, send_sems, recv_sems, k, me, peer, False).wait_recv()

    hbm = pl.BlockSpec(memory_space=pltpu.HBM)
    sem = pl.BlockSpec(memory_space=pltpu.SEMAPHORE)
    return pl.pallas_call(
        body, name=name,
        out_shape=(pltpu.HBM(src_thru.shape, src_thru.dtype), pltpu.HBM(land_thru.shape, land_thru.dtype)),
        in_specs=(hbm, hbm, sem, sem, *[pl.BlockSpec(memory_space=pl.ANY)] * len(after)), out_specs=(hbm, hbm),
        input_output_aliases={0: 0, 1: 1},
        compiler_params=pltpu.CompilerParams(has_side_effects=pltpu.SideEffectType.DATAFLOW_SIDE_EFFECTING),
    )(src_thru, land_thru, send_sems, recv_sems, *after)


def _exchange_alone(name, exch, after=()):
    def body():
        pass

    return _hosted_call(name, body, (1,), [], [], [], [], None, (), exch, after)


def _mm(name, grid, a, a_spec, b, b_spec, dims, outs, o_specs, acc_shape, extra=(), extra_specs=(), epi=None,
        exch=None, after=(), into=None, summed=None):
    nk = grid[-1]
    n_extra, n_out = len(extra), len(outs)

    def body(*refs):
        a_ref, b_ref = refs[0], refs[1]
        ex = refs[2:2 + n_extra]
        o_refs = refs[2 + n_extra:2 + n_extra + n_out]

        def write(res):
            res = (res,) if epi is None else epi(res, *[e[...] for e in ex])
            for idx, (o, r) in enumerate(zip(o_refs, res)):
                if summed is not None and idx == summed[0]:
                    first = pl.program_id(summed[1]) == 0

                    @pl.when(first)
                    def _(o=o, r=r):
                        o[...] = r.astype(o.dtype)

                    @pl.when(jnp.logical_not(first))
                    def _(o=o, r=r):
                        o[...] += r.astype(o.dtype)
                else:
                    o[...] = r.astype(o.dtype)

        if nk == 1:
            write(_bdot(a_ref[...], b_ref[...], dims))
            return
        acc = refs[-1]
        k = pl.program_id(len(grid) - 1)

        @pl.when(k == 0)
        def _():
            acc[...] = _bdot(a_ref[...], b_ref[...], dims)

        @pl.when(jnp.logical_and(k > 0, k < nk - 1))
        def _():
            acc[...] += _bdot(a_ref[...], b_ref[...], dims)

        @pl.when(k == nk - 1)
        def _():
            write(acc[...] + _bdot(a_ref[...], b_ref[...], dims))

    sem = ("parallel",) * (len(grid) - 1) + ("arbitrary",)
    if summed is not None:
        sem = ("arbitrary",) * len(grid)
    scratch = [] if nk == 1 else [pltpu.VMEM(acc_shape, F32)]
    return _hosted_call(name, body, grid, [a_spec, b_spec, *extra_specs], list(o_specs), list(outs),
                        scratch, sem, (a, b, *extra), exch, after, into)


def _add_epi(acc, add):
    return (acc + add,)


def _pack_epi(acc):
    return (_pack_pairs(acc),)


def _mm_nt(name, a, b, n_off, n, out_dtype=F32, tm=1024, tn=512, exch=None, after=()):
    m, kk = a.shape
    tm, tn = _t(m, tm), math.gcd(_t(n, tn), n_off)
    res = _mm(name, (m // tm, n // tn, 1),
              a, pl.BlockSpec((tm, kk), lambda i, j, k: (i, 0)),
              b, pl.BlockSpec((tn, kk), lambda i, j, k: (n_off // tn + j, 0)), NT,
              [jax.ShapeDtypeStruct((m, n), out_dtype)], [pl.BlockSpec((tm, tn), lambda i, j, k: (i, j))], (tm, tn),
              exch=exch, after=after)
    return res[0] if exch is None else res


def _mm_nt_halves(name, a, b2, n_off, n, tm=1024, tn=512, exch=None):
    m = a.shape[0]
    kh = b2.shape[2]
    tm, tn = _t(m, tm), math.gcd(_t(n, tn), n_off)

    def body(a_ref, b_ref, o_ref):
        o_ref[...] = _bdot(a_ref[:, :kh], b_ref[0], NT) + _bdot(a_ref[:, kh:], b_ref[1], NT)

    res = _hosted_call(name, body, (m // tm, n // tn),
                       [pl.BlockSpec((tm, 2 * kh), lambda i, j: (i, 0)),
                        pl.BlockSpec((2, tn, kh), lambda i, j: (0, n_off // tn + j, 0))],
                       [pl.BlockSpec((tm, tn), lambda i, j: (i, j))], [jax.ShapeDtypeStruct((m, n), F32)], [],
                       ("parallel", "parallel"), (a, b2), exch)
    return res[0] if exch is None else res


def _mm_nn_halves(name, a, b2, k_off=0, add=None, after=(), tm=1024, tk=1024):
    m, kk = a.shape
    half = b2.shape[2]
    tm, tk = _t(m, tm), math.gcd(_t(kk, tk), k_off)
    extra, especs, epi = (), (), None
    if add is not None:
        extra, especs, epi = (add,), (pl.BlockSpec((tm, half), lambda i, j, k: (i, j)),), _add_epi
    return _mm(name, (m // tm, 2, kk // tk),
               a, pl.BlockSpec((tm, tk), lambda i, j, k: (i, k)),
               b2, pl.BlockSpec((None, tk, half), lambda i, j, k: (j, k_off // tk + k, 0)), NN,
               [jax.ShapeDtypeStruct((m, 2 * half), F32)], [pl.BlockSpec((tm, half), lambda i, j, k: (i, j))],
               (tm, half), extra, especs, epi, after=after)[0]


def _mm_nn_blocked(name, a, b3, tm=1024, tn=512, exch=None):
    m, kk = a.shape
    nb, _, cs = b3.shape
    tm, tn = _t(m, tm), _t(cs, tn)
    per = cs // tn
    res = _mm(name, (m // tm, nb * per, 1),
              a, pl.BlockSpec((tm, kk), lambda i, j, k: (i, 0)),
              b3, pl.BlockSpec((None, kk, tn), lambda i, j, k: (j // per, 0, j % per)), NN,
              [jax.ShapeDtypeStruct((m, nb * cs), F32)], [pl.BlockSpec((tm, tn), lambda i, j, k: (i, j))],
              (tm, tn), exch=exch)
    return res[0] if exch is None else res


def _mm_nt_blocked(name, a, b3, after=(), tm=1024, tn=1024):
    m = a.shape[0]
    nb, n, cs = b3.shape
    tm, tn = _t(m, tm), _t(n, tn)
    return _mm(name, (m // tm, n // tn, nb),
               a, pl.BlockSpec((tm, cs), lambda i, j, k: (i, k)),
               b3, pl.BlockSpec((None, tn, cs), lambda i, j, k: (k, j, 0)), NT,
               [jax.ShapeDtypeStruct((m, n), F32)], [pl.BlockSpec((tm, tn), lambda i, j, k: (i, j))], (tm, tn),
               after=after)[0]


def _mm_nn(name, a, b, k_off=0, kk=None, add=None, out_dtype=F32, tm=1024, tn=1024, tk=1024, after=()):
    m = a.shape[0]
    kk = a.shape[1] if kk is None else kk
    n = b.shape[1]
    tm, tn, tk = _t(m, tm), _t(n, tn), math.gcd(_t(kk, tk), k_off)
    extra, especs, epi = (), (), None
    if add is not None:
        extra, especs, epi = (add,), (pl.BlockSpec((tm, tn), lambda i, j, k: (i, j)),), _add_epi
    return _mm(name, (m // tm, n // tn, kk // tk),
               a, pl.BlockSpec((tm, tk), lambda i, j, k: (i, k)),
               b, pl.BlockSpec((tk, tn), lambda i, j, k: (k_off // tk + k, j)), NN,
               [jax.ShapeDtypeStruct((m, n), out_dtype)], [pl.BlockSpec((tm, tn), lambda i, j, k: (i, j))], (tm, tn),
               extra, especs, epi, after=after)[0]


TN_ACC_ELEMENTS = 1 << 20


def _mm_tn_packed(name, a, b, pw, tk=1024, rows=None, into=None, b_blk=None, after=()):
    kk, m = a.shape
    nq = b.shape[1] // pw if b_blk is None else 1
    first = 0 if b_blk is None else b_blk
    tm, tk = _t(m, TN_ACC_ELEMENTS // pw), _t(kk, tk)
    rows = m if rows is None else rows
    row_blk = 0
    if into is not None:
        rows = into.shape[1]
        assert (rows - m) % tm == 0
        row_blk = (rows - m) // tm
    return _mm(name, (m // tm, nq, kk // tk),
               a, pl.BlockSpec((tk, tm), lambda i, j, k: (k, i)),
               b, pl.BlockSpec((tk, pw), lambda i, j, k: (k, first + j)), TN,
               [jax.ShapeDtypeStruct((nq, rows, pw // 2), F32)],
               [pl.BlockSpec((None, tm, pw // 2), lambda i, j, k: (j, row_blk + i, 0))], (tm, pw), epi=_pack_epi,
               into=into, after=after)[0]


def _rms_fwd(name, x, g, tm=256):
    l, d = x.shape
    tm = _t(l, tm)

    def body(x_ref, g_ref, h_ref, r_ref):
        xv = x_ref[...]
        r = lax.rsqrt(jnp.mean(xv * xv, axis=-1, keepdims=True) + EPS)
        h_ref[...] = (xv * r * g_ref[...]).astype(BF)
        r_ref[...] = r

    return pl.pallas_call(
        body, grid=(l // tm,),
        in_specs=[pl.BlockSpec((tm, d), lambda i: (i, 0)), pl.BlockSpec((1, d), lambda i: (0, 0))],
        out_specs=[pl.BlockSpec((tm, d), lambda i: (i, 0)), pl.BlockSpec((tm, 1), lambda i: (i, 0))],
        out_shape=[jax.ShapeDtypeStruct((l, d), BF), jax.ShapeDtypeStruct((l, 1), F32)],
        compiler_params=_params(("parallel",)), name=name)(x, g)


def _rms_bwd(name, dh, x, r, g, dres, tm=256):
    l, d = x.shape
    tm = _t(l, tm)

    def body(dh_ref, x_ref, r_ref, g_ref, dres_ref, dx_ref, dxb_ref, dg_ref):
        i = pl.program_id(0)
        rr = r_ref[...]
        xhat = x_ref[...] * rr
        dhv = dh_ref[...]
        dxh = dhv * g_ref[...]
        dx = dres_ref[...] + rr * (dxh - xhat * jnp.mean(dxh * xhat, axis=-1, keepdims=True))
        dx_ref[...] = dx
        dxb_ref[...] = dx.astype(BF)

        @pl.when(i == 0)
        def _():
            dg_ref[...] = jnp.zeros_like(dg_ref)

        dg_ref[...] += jnp.sum(dhv * xhat, axis=0, keepdims=True)

    row = pl.BlockSpec((tm, d), lambda i: (i, 0))
    vec = pl.BlockSpec((1, d), lambda i: (0, 0))
    return pl.pallas_call(
        body, grid=(l // tm,), in_specs=[row, row, pl.BlockSpec((tm, 1), lambda i: (i, 0)), vec, row],
        out_specs=[row, row, vec],
        out_shape=[jax.ShapeDtypeStruct((l, d), F32), jax.ShapeDtypeStruct((l, d), BF),
                   jax.ShapeDtypeStruct((1, d), F32)],
        compiler_params=_params(("arbitrary",)), name=name)(dh, x, r, g, dres)


def _loss_head(x, g, target, tm=256):
    l, d = x.shape
    tm = _t(l, tm)

    def body(x_ref, g_ref, t_ref, dx_ref, dxb_ref, dg_ref, loss_ref):
        i = pl.program_id(0)
        xv = x_ref[...]
        gv = g_ref[...]
        r = lax.rsqrt(jnp.mean(xv * xv, axis=-1, keepdims=True) + EPS)
        xhat = xv * r
        e = xhat * gv - t_ref[...]
        dy = e * (1.0 / d)
        dxh = dy * gv
        dx = r * (dxh - xhat * jnp.mean(dxh * xhat, axis=-1, keepdims=True))
        dx_ref[...] = dx
        dxb_ref[...] = dx.astype(BF)

        @pl.when(i == 0)
        def _():
            dg_ref[...] = jnp.zeros_like(dg_ref)
            loss_ref[...] = jnp.zeros_like(loss_ref)

        dg_ref[...] += jnp.sum(dy * xhat, axis=0, keepdims=True)
        loss_ref[...] += 0.5 * jnp.sum(jnp.sum(e * e, axis=-1, keepdims=True) * (1.0 / d), axis=0, keepdims=True)

    row = pl.BlockSpec((tm, d), lambda i: (i, 0))
    vec = pl.BlockSpec((1, d), lambda i: (0, 0))
    return pl.pallas_call(
        body, grid=(l // tm,), in_specs=[row, vec, row],
        out_specs=[row, row, vec, pl.BlockSpec((1, 1), lambda i: (0, 0))],
        out_shape=[jax.ShapeDtypeStruct((l, d), F32), jax.ShapeDtypeStruct((l, d), BF),
                   jax.ShapeDtypeStruct((1, d), F32), jax.ShapeDtypeStruct((1, 1), F32)],
        compiler_params=_params(("arbitrary",)), name="loss_head")(x, g, target)


CONV_HALO = 8


def _conv_pre(x_ref, w_ref, b_ref, i, tr):
    r0 = pl.multiple_of(i * tr, tr)
    cur = x_ref[pl.ds(r0, tr), :]
    prev = x_ref[pl.ds(pl.multiple_of(jnp.maximum(r0 - CONV_HALO, 0), CONV_HALO), CONV_HALO), :]
    prev = jnp.where(i > 0, prev, 0.0)
    ext = jnp.concatenate([prev, cur], axis=0)
    taps = []
    for k in range(CONV_TAPS):
        s = CONV_TAPS - 1 - k
        taps.append(cur if s == 0 else pltpu.roll(ext, s, 0)[CONV_HALO:])
    pre = b_ref[...] + sum(w_ref[k:k + 1, :] * taps[k] for k in range(CONV_TAPS))
    return r0, pre, taps


def _conv_fwd(pm, col_off, conv_w, conv_b, cw=256, tr=512, exch=None):
    l = pm.shape[0]
    c = conv_w.shape[1]
    cw, tr = _t(c, cw), _t(l, tr)
    assert col_off % cw == 0

    def body(x_ref, w_ref, b_ref, o_ref):
        def step(i, carry):
            r0, pre, _ = _conv_pre(x_ref, w_ref, b_ref, i, tr)
            o_ref[pl.ds(r0, tr), :] = _silu(pre)
            return carry
        lax.fori_loop(0, l // tr, step, 0)

    res = _hosted_call(
        "conv_fwd", body, (c // cw,),
        [pl.BlockSpec((l, cw), lambda j: (0, col_off // cw + j)),
         pl.BlockSpec((CONV_TAPS, cw), lambda j: (0, j)), pl.BlockSpec((1, cw), lambda j: (0, j))],
        [pl.BlockSpec((l, cw), lambda j: (0, j))], [jax.ShapeDtypeStruct((l, c), F32)], [], ("parallel",),
        (pm, conv_w, conv_b), exch)
    return res[0] if exch is None else res


def _conv_bwd(pm, col_off, conv_w, conv_b, dy, dproj, cw=256, tr=512):
    l = pm.shape[0]
    c = conv_w.shape[1]
    cw, tr = _t(c, cw), _t(l, tr)
    nt = l // tr

    def body(x_ref, w_ref, b_ref, dy_ref, _, dx_ref, dw_ref, db_ref, dpre_ref):
        def step1(i, carry):
            dws, db = carry
            r0, pre, taps = _conv_pre(x_ref, w_ref, b_ref, i, tr)
            dpre = dy_ref[pl.ds(r0, tr), :] * _dsilu(pre)
            dpre_ref[pl.ds(r0, tr), :] = dpre
            dws = tuple(dws[k] + jnp.sum(dpre * taps[k], axis=0, keepdims=True) for k in range(CONV_TAPS))
            return dws, db + jnp.sum(dpre, axis=0, keepdims=True)

        z = jnp.zeros((1, cw), F32)
        dws, db = lax.fori_loop(0, nt, step1, ((z,) * CONV_TAPS, z))
        for k in range(CONV_TAPS):
            dw_ref[k:k + 1, :] = dws[k]
        db_ref[...] = db

        def step2(i, carry):
            r0 = pl.multiple_of(i * tr, tr)
            cur = dpre_ref[pl.ds(r0, tr), :]
            nxt = dpre_ref[pl.ds(pl.multiple_of(jnp.minimum(r0 + tr, l - CONV_HALO), CONV_HALO), CONV_HALO), :]
            nxt = jnp.where(i < nt - 1, nxt, 0.0)
            ext = jnp.concatenate([cur, nxt], axis=0)
            acc = w_ref[CONV_TAPS - 1:CONV_TAPS, :] * cur
            for k in range(CONV_TAPS - 1):
                s = CONV_TAPS - 1 - k
                acc = acc + w_ref[k:k + 1, :] * pltpu.roll(ext, tr + CONV_HALO - s, 0)[:tr]
            dx_ref[pl.ds(r0, tr), :] = acc.astype(dx_ref.dtype)
            return carry
        lax.fori_loop(0, nt, step2, 0)

    col = pl.BlockSpec((l, cw), lambda j: (0, j))
    shifted = pl.BlockSpec((l, cw), lambda j: (0, col_off // cw + j))
    return pl.pallas_call(
        body, grid=(c // cw,),
        in_specs=[shifted, pl.BlockSpec((CONV_TAPS, cw), lambda j: (0, j)), pl.BlockSpec((1, cw), lambda j: (0, j)),
                  col, pl.BlockSpec(memory_space=pl.ANY)],
        out_specs=[shifted, pl.BlockSpec((CONV_TAPS, cw), lambda j: (0, j)), pl.BlockSpec((1, cw), lambda j: (0, j))],
        out_shape=[jax.ShapeDtypeStruct(dproj.shape, dproj.dtype), jax.ShapeDtypeStruct((CONV_TAPS, c), F32),
                   jax.ShapeDtypeStruct((1, c), F32)],
        scratch_shapes=[pltpu.VMEM((l, cw), F32)], input_output_aliases={4: 0},
        compiler_params=_params(("parallel",)), name="conv_bwd")(pm, conv_w, conv_b, dy, dproj)


def _split(x, pieces):
    out, rest = [], x
    for _ in range(pieces):
        piece = rest.astype(BF)
        out.append(piece)
        rest = rest - piece.astype(F32)
    return out


def _rows_times(xs, m_stack, pieces):
    x = xs[0] if len(xs) == 1 else jnp.concatenate(xs, axis=0)
    out = _bdot(jnp.concatenate(_split(x, pieces), axis=1), m_stack)
    sizes = [v.shape[0] for v in xs]
    offs = np.cumsum([0] + sizes)
    return [out[offs[i]:offs[i + 1]] for i in range(len(xs))]


def _times_rows(m_stack, x, pieces):
    return _bdot(m_stack, jnp.concatenate(_split(x, pieces), axis=0))


EXPAND_PIECES = 3
FOLD_PIECES = 2


def _ssd_consts(r, p, t):
    assert p == t, "heads expand to P lanes of the inputs and to T lanes of the decay matrices alike"
    rp = r * p
    tri = np.tril(np.ones((t, t), np.float32))
    ep = np.zeros((HEAD_LANES, rp), np.float32)
    ep[np.arange(rp) // p, np.arange(rp)] = 1.0
    itile = (np.arange(t)[:, None] == (np.arange(rp) % t)[None, :]).astype(np.float32)
    lmask = (np.arange(t)[:, None] >= (np.arange(rp) % t)[None, :]).astype(np.float32)
    bmask = ((np.arange(rp) // t)[:, None] == (np.arange(rp) // p)[None, :]).astype(np.float32)
    return [jnp.asarray(np.concatenate([ep] * EXPAND_PIECES, axis=0), BF),
            jnp.asarray(np.concatenate([ep.T] * FOLD_PIECES, axis=0), BF),
            jnp.asarray(np.concatenate([tri] * EXPAND_PIECES, axis=1), BF),
            jnp.asarray(np.concatenate([tri.T] * EXPAND_PIECES, axis=1), BF),
            jnp.asarray(ep.T.copy()), jnp.asarray(itile), jnp.asarray(lmask), jnp.asarray(bmask)]


def _ssd_common(xs, bm, pre_raw, dtb, alog, e_stack, tri_stack, itile, lmask, bmask, r, t):
    pre = pre_raw + dtb
    dt = jnp.maximum(pre, 0.0) + jnp.log(1.0 + jnp.exp(-jnp.abs(pre)))
    a = -jnp.exp(alog)
    cs = _times_rows(tri_stack, dt * a, EXPAND_PIECES)
    last = cs[t - 1:t, :]
    dtx, csx = _rows_times([dt, cs], e_stack, EXPAND_PIECES)
    ecsx = jnp.exp(csx)
    wx = jnp.exp(csx[t - 1:t, :] - csx)
    csrow = jnp.sum(csx * itile, axis=0, keepdims=True)
    lx = jnp.exp(jnp.where(lmask > 0.0, csx - csrow, NEG_BIG))
    xdt = xs * dtx
    xblk = (jnp.concatenate([xdt] * r, axis=0) * bmask).astype(BF)
    btile = jnp.concatenate([bm] * r, axis=0).astype(BF)
    return pre, dt, a, last, dtx, ecsx, wx, lx, xdt, xblk, btile


def _ssd_specs(l, g_n, r, p, n, t, conv, rev):
    nc = l // t
    rp = r * p
    cidx = (lambda c: nc - 1 - c) if rev else (lambda c: c)
    row_spec = lambda width: pl.BlockSpec((t, width), lambda c: (cidx(c), 0))
    dtr_spec = pl.BlockSpec((g_n, t, HEAD_LANES), lambda c: (0, cidx(c), 0))
    par_spec = pl.BlockSpec((g_n, 1, HEAD_LANES), lambda c: (0, 0, 0))
    dskx_spec = pl.BlockSpec((g_n, 1, rp), lambda c: (0, 0, 0))
    st_spec = pl.BlockSpec((None, g_n, rp, n), lambda c: (cidx(c), 0, 0, 0))
    return nc, row_spec, dtr_spec, par_spec, dskx_spec, st_spec


def _const_specs(consts):
    return [pl.BlockSpec(a.shape, lambda c: (0, 0)) for a in consts]


def _ssd_fwd(xbc, dtr, dtb, alog, dskx, pm, norm_g, dims, exch=None):
    h, g_n, r, p, n, t = dims
    l, conv = xbc.shape
    rp, hp = r * p, h * p
    nc, row_spec, dtr_spec, par_spec, dskx_spec, st_spec = _ssd_specs(l, g_n, r, p, n, t, conv, False)
    consts = _ssd_consts(r, p, t)

    def body(x_ref, dtr_ref, dtb_ref, alog_ref, dskx_ref, z_ref, ng_ref,
             e_ref, et_ref, tri_ref, trit_ref, ept_ref, it_ref, lm_ref, bmk_ref,
             y_ref, st_ref, yn_ref, rn_ref, s_ref):
        @pl.when(pl.program_id(0) == 0)
        def _():
            s_ref[...] = jnp.zeros_like(s_ref)

        for g in range(g_n):
            xs = x_ref[:, g * rp:(g + 1) * rp]
            bm = x_ref[:, hp + g * n:hp + (g + 1) * n]
            cm = x_ref[:, hp + (g_n + g) * n:hp + (g_n + g + 1) * n]
            (_, _, _, last, _, ecsx, wx, lx, xdt, xblk, btile) = _ssd_common(
                xs, bm, dtr_ref[g], dtb_ref[g], alog_ref[g], e_ref[...], tri_ref[...], it_ref[...], lm_ref[...],
                bmk_ref[...], r, t)
            s_in = s_ref[g]
            st_ref[g] = s_in
            cbx = _bdot(cm, btile, NT)
            yd = _bdot(cbx * lx, xblk)
            yo = ecsx * _bdot(cm, s_in, NT)
            y_ref[:, g * rp:(g + 1) * rp] = yd + yo + dskx_ref[g] * xs
            elast = jnp.sum(ept_ref[...] * jnp.exp(last), axis=1, keepdims=True)
            s_ref[g] = elast * s_in + _bdot(xdt * wx, bm, TN)

        v = y_ref[...] * _silu(z_ref[...])
        rr = lax.rsqrt(jnp.mean(v * v, axis=-1, keepdims=True) + EPS)
        yn_ref[...] = (v * rr * ng_ref[...]).astype(BF)
        rn_ref[...] = rr

    return _hosted_call(
        "ssd_fwd", body, (nc,),
        [row_spec(conv), dtr_spec, par_spec, par_spec, dskx_spec, row_spec(hp),
         pl.BlockSpec((1, hp), lambda c: (0, 0)), *_const_specs(consts)],
        [row_spec(hp), st_spec, row_spec(hp), row_spec(1)],
        [jax.ShapeDtypeStruct((l, hp), F32), jax.ShapeDtypeStruct((nc, g_n, rp, n), F32),
         jax.ShapeDtypeStruct((l, hp), BF), jax.ShapeDtypeStruct((l, 1), F32)],
        [pltpu.VMEM((g_n, rp, n), F32)], ("arbitrary",),
        (xbc, dtr, dtb, alog, dskx, pm, norm_g, *consts), exch)


def _ssd_bwd(xbc, dtr, dtb, alog, dskx, states, dyn, y, pm, rstd, norm_g, dims, after=()):
    h, g_n, r, p, n, t = dims
    l, conv = xbc.shape
    rp, hp = r * p, h * p
    nc, row_spec, dtr_spec, par_spec, dskx_spec, st_spec = _ssd_specs(l, g_n, r, p, n, t, conv, True)
    consts = _ssd_consts(r, p, t)

    def fold_rows(v, rows):
        return sum(v[k * rows:(k + 1) * rows, :] for k in range(r))

    def body(x_ref, dtr_ref, dtb_ref, alog_ref, dskx_ref, st_ref, dyn_ref, y_ref, z_ref, rn_ref, ng_ref,
             e_ref, et_ref, tri_ref, trit_ref, ept_ref, it_ref, lm_ref, bmk_ref,
             dx_ref, ddt_ref, dbias_ref, dalog_ref, dd_ref, dz_ref, dng_ref, ds_ref, dy_ref):
        @pl.when(pl.program_id(0) == 0)
        def _():
            ds_ref[...] = jnp.zeros_like(ds_ref)
            dbias_ref[...] = jnp.zeros_like(dbias_ref)
            dalog_ref[...] = jnp.zeros_like(dalog_ref)
            dd_ref[...] = jnp.zeros_like(dd_ref)
            dng_ref[...] = jnp.zeros_like(dng_ref)

        yv, zv, rr, dn = y_ref[...], z_ref[...], rn_ref[...], dyn_ref[...]
        sz = _silu(zv)
        vhat = yv * sz * rr
        dvh = dn * ng_ref[...]
        dv = rr * (dvh - vhat * jnp.mean(dvh * vhat, axis=-1, keepdims=True))
        dy_ref[...] = dv * sz
        dz_ref[...] = (dv * yv * _dsilu(zv)).astype(BF)
        dng_ref[...] += jnp.sum(dn * vhat, axis=0, keepdims=True)

        ept, itile, bmask = ept_ref[...], it_ref[...], bmk_ref[...]
        last_row = lax.broadcasted_iota(jnp.int32, (t, HEAD_LANES), 0) == t - 1
        for g in range(g_n):
            xs = x_ref[:, g * rp:(g + 1) * rp]
            bm = x_ref[:, hp + g * n:hp + (g + 1) * n]
            cm = x_ref[:, hp + (g_n + g) * n:hp + (g_n + g + 1) * n]
            dy = dy_ref[:, g * rp:(g + 1) * rp]
            (pre, dt, a, last, dtx, ecsx, wx, lx, xdt, xblk, btile) = _ssd_common(
                xs, bm, dtr_ref[g], dtb_ref[g], alog_ref[g], e_ref[...], tri_ref[...], itile, lm_ref[...], bmask, r, t)
            s_in = st_ref[g]
            ds_out = ds_ref[g]
            elast_row = jnp.exp(last)
            elast = jnp.sum(ept * elast_row, axis=1, keepdims=True)

            cbx = _bdot(cm, btile, NT)
            amat = cbx * lx
            da = _bdot(dy, xblk, NT)
            dxdt = fold_rows(_bdot(amat, dy, TN) * bmask, t)
            dcbx = da * lx
            q = da * amat
            dc = _bdot(dcbx, btile)
            db = fold_rows(_bdot(dcbx, cm, TN), t)

            g0 = _bdot(cm, s_in, NT)
            dg0 = dy * ecsx
            dc = dc + _bdot(dg0, s_in)

            z = _bdot(bm, ds_out, NT)
            dxdt = dxdt + z * wx
            db = db + _bdot(xdt * wx, ds_out)
            ds_ref[g] = elast * ds_out + _bdot(dg0, cm, TN)

            dwx = z * xdt * wx
            rows = jnp.concatenate([jnp.sum(dy * xs, axis=0, keepdims=True), jnp.sum(dwx, axis=0, keepdims=True),
                                    jnp.zeros((14, rp), F32)], axis=0)
            f_cs, f_dt, f_rows = _rows_times(
                [q - itile * jnp.sum(q, axis=0, keepdims=True) + dy * g0 * ecsx - dwx, dxdt * xs, rows],
                et_ref[...], FOLD_PIECES)
            dd_ref[g] += f_rows[0:1]
            dlast = jnp.sum(jnp.sum(ds_out * s_in, axis=1, keepdims=True) * ept, axis=0, keepdims=True) * elast_row
            dlast = dlast + f_rows[1:2]

            dcs = f_cs + jnp.where(last_row, dlast, 0.0)
            dadt = _times_rows(trit_ref[...], dcs, EXPAND_PIECES)
            dalog_ref[g] += jnp.sum(dadt * dt, axis=0, keepdims=True) * a
            dpre = (dadt * a + f_dt) * (1.0 / (1.0 + jnp.exp(-pre)))
            ddt_ref[g] = dpre
            dbias_ref[g] += jnp.sum(dpre, axis=0, keepdims=True)
            dx_ref[:, g * rp:(g + 1) * rp] = dskx_ref[g] * dy + dxdt * dtx
            dx_ref[:, hp + g * n:hp + (g + 1) * n] = db
            dx_ref[:, hp + (g_n + g) * n:hp + (g_n + g + 1) * n] = dc

    par_shape = jax.ShapeDtypeStruct((g_n, 1, HEAD_LANES), F32)
    vec = pl.BlockSpec((1, hp), lambda c: (0, 0))
    return _hosted_call(
        "ssd_bwd", body, (nc,),
        [row_spec(conv), dtr_spec, par_spec, par_spec, dskx_spec, st_spec, row_spec(hp), row_spec(hp), row_spec(hp),
         row_spec(1), vec, *_const_specs(consts)],
        [row_spec(conv), dtr_spec, par_spec, par_spec, par_spec, row_spec(hp), vec],
        [jax.ShapeDtypeStruct((l, conv), F32), jax.ShapeDtypeStruct((g_n, l, HEAD_LANES), F32),
         par_shape, par_shape, par_shape, jax.ShapeDtypeStruct((l, pm.shape[1]), BF),
         jax.ShapeDtypeStruct((1, hp), F32)],
        [pltpu.VMEM((g_n, rp, n), F32), pltpu.VMEM((t, hp), F32)], ("arbitrary",),
        (xbc, dtr, dtb, alog, dskx, states, dyn, y, pm, rstd, norm_g, *consts), after=after)


POOL_HALO = 16


def _pool_mix(name, src, dp, backward, out_dtype, into=None, cw=256, tr=512):
    l = src.shape[0]
    gd = dp // len(POOL_WINDOWS)
    cwl, trl = _t(gd, cw), _t(l, tr)
    nt = l // trl

    def body(x_ref, *rest):
        o_ref = rest[-1]
        gi = pl.program_id(0)
        for wi, win in enumerate(POOL_WINDOWS):
            @pl.when(gi == wi)
            def _(win=win):
                def step(i, carry):
                    r0 = pl.multiple_of(i * trl, trl)
                    cur = x_ref[pl.ds(r0, trl), :]
                    trow = r0 + lax.broadcasted_iota(jnp.int32, (trl, 1), 0)
                    cnt = jnp.minimum(trow + 1, win).astype(F32)
                    if not backward:
                        halo = x_ref[pl.ds(pl.multiple_of(jnp.maximum(r0 - POOL_HALO, 0), POOL_HALO), POOL_HALO), :]
                        halo = jnp.where(i > 0, halo, 0.0)
                        s = jnp.concatenate([halo, cur], axis=0)
                        sh = 1
                        while sh < win:
                            s = s + pltpu.roll(s, sh, 0)
                            sh *= 2
                        res = s[POOL_HALO:] / cnt - cur
                    else:
                        halo = x_ref[pl.ds(pl.multiple_of(jnp.minimum(r0 + trl, l - POOL_HALO), POOL_HALO),
                                           POOL_HALO), :]
                        hrow = r0 + trl + lax.broadcasted_iota(jnp.int32, (POOL_HALO, 1), 0)
                        hcnt = jnp.minimum(hrow + 1, win).astype(F32)
                        halo = jnp.where(i < nt - 1, halo / hcnt, 0.0)
                        s = jnp.concatenate([cur / cnt, halo], axis=0)
                        sh = 1
                        while sh < win:
                            s = s + pltpu.roll(s, trl + POOL_HALO - sh, 0)
                            sh *= 2
                        res = s[:trl] - cur
                    o_ref[pl.ds(r0, trl), :] = res.astype(o_ref.dtype)
                    return carry
                lax.fori_loop(0, nt, step, 0)

    col = pl.BlockSpec((l, cwl), lambda g, j: (0, g * (gd // cwl) + j))
    if into is None:
        return pl.pallas_call(
            body, grid=(len(POOL_WINDOWS), gd // cwl), in_specs=[col], out_specs=col,
            out_shape=jax.ShapeDtypeStruct((l, dp), out_dtype),
            compiler_params=_params(("parallel", "parallel")), name=name)(src)
    return pl.pallas_call(
        body, grid=(len(POOL_WINDOWS), gd // cwl), in_specs=[col, pl.BlockSpec(memory_space=pl.ANY)], out_specs=col,
        out_shape=jax.ShapeDtypeStruct(into.shape, into.dtype), input_output_aliases={1: 0},
        compiler_params=_params(("parallel", "parallel")), name=name)(src, into)


def _pool_out_bwd_data(dx, wp_out, mg, pu, scale, tm=1024, tn=512):
    l, d = dx.shape
    dp = wp_out.shape[0]
    tm, tn = _t(l, tm), _t(dp, tn)

    def epi(dyp, m, gt, sc):
        sg = _silu(gt)
        return dyp * sc * sg, dyp * m * sc * _dsilu(gt), jnp.sum(dyp * m * sg, axis=0, keepdims=True)

    tile = pl.BlockSpec((tm, tn), lambda j, i, k: (i, j))
    right = pl.BlockSpec((tm, tn), lambda j, i, k: (i, dp // tn + j))
    vec = pl.BlockSpec((1, tn), lambda j, i, k: (0, j))
    return _mm("pool_out_bwd_data", (dp // tn, l // tm, 1),
               dx, pl.BlockSpec((tm, d), lambda j, i, k: (i, 0)),
               wp_out, pl.BlockSpec((tn, d), lambda j, i, k: (j, 0)), NT,
               [jax.ShapeDtypeStruct((l, dp), BF), jax.ShapeDtypeStruct((l, 2 * dp), BF),
                jax.ShapeDtypeStruct((1, dp), F32)], [tile, right, vec], (tm, tn),
               (mg, pu, scale), (tile, right, vec), epi, summed=(2, 1))


def _group_mm_fwd(mp, wg, pu, scale, tm=1024, tn=1024, tk=1024):
    l, dp = mp.shape
    ng, gd = wg.shape[0], wg.shape[1]
    tm, tn, tk = _t(l, tm), _t(gd, tn), _t(gd, tk)

    def epi(acc, gate, sc):
        return acc, acc * sc * _silu(gate)

    out = pl.BlockSpec((tm, tn), lambda g, i, j, k: (i, g * (gd // tn) + j))
    return _mm("group_mm_fwd", (ng, l // tm, gd // tn, gd // tk),
               mp, pl.BlockSpec((tm, tk), lambda g, i, j, k: (i, g * (gd // tk) + k)),
               wg, pl.BlockSpec((None, tk, tn), lambda g, i, j, k: (g, k, j)), NN,
               [jax.ShapeDtypeStruct((l, dp), F32), jax.ShapeDtypeStruct((l, dp), BF)], [out, out], (tm, tn),
               (pu, scale),
               (pl.BlockSpec((tm, tn), lambda g, i, j, k: (i, (dp + g * gd) // tn + j)),
                pl.BlockSpec((1, tn), lambda g, i, j, k: (0, g * (gd // tn) + j))), epi)


def _group_mm_bwd_data(dmg, wg, after=(), tm=1024, tn=1024, tk=1024):
    l, dp = dmg.shape
    ng, gd = wg.shape[0], wg.shape[1]
    tm, tn, tk = _t(l, tm), _t(gd, tn), _t(gd, tk)
    return _mm("group_mm_bwd_data", (ng, l // tm, gd // tn, gd // tk),
               dmg, pl.BlockSpec((tm, tk), lambda g, i, j, k: (i, g * (gd // tk) + k)),
               wg, pl.BlockSpec((None, tn, tk), lambda g, i, j, k: (g, j, k)), NT,
               [jax.ShapeDtypeStruct((l, dp), F32)],
               [pl.BlockSpec((tm, tn), lambda g, i, j, k: (i, g * (gd // tn) + j))], (tm, tn), after=after)[0]


def _group_mm_bwd_weight(mp, dmg, ng, tm=512, tk=1024):
    l, dp = mp.shape
    gd = dp // ng
    tm, tk = _t(gd, tm), _t(l, tk)
    return _mm("group_mm_bwd_weight", (ng, gd // tm, l // tk),
               mp, pl.BlockSpec((tk, tm), lambda g, i, k: (k, g * (gd // tm) + i)),
               dmg, pl.BlockSpec((tk, gd), lambda g, i, k: (k, g)), TN,
               [jax.ShapeDtypeStruct((ng, gd, gd // 2), F32)],
               [pl.BlockSpec((None, tm, gd // 2), lambda g, i, k: (g, i, 0))], (tm, gd), epi=_pack_epi)[0]


def _cast_bf16(name, w, tr=256):
    r, c = w.shape
    tr = _row_tile(r, tr, 16)

    def body(w_ref, o_ref):
        o_ref[...] = w_ref[...].astype(BF)

    blk = pl.BlockSpec((tr, c), lambda i: (i, 0))
    return pl.pallas_call(body, grid=(r // tr,), in_specs=[blk], out_specs=blk,
                          out_shape=jax.ShapeDtypeStruct((r, c), BF), compiler_params=_params(("parallel",)),
                          name=name)(w)


def _pack_rows(name, w, tr=256):
    r, c = w.shape
    tr = _row_tile(r, tr)

    def body(w_ref, o_ref):
        o_ref[...] = _pack_pairs(w_ref[...])

    return pl.pallas_call(body, grid=(r // tr, 2), in_specs=[pl.BlockSpec((tr, c // 2), lambda i, q: (i, q))],
                          out_specs=pl.BlockSpec((None, tr, c // 4), lambda i, q: (q, i, 0)),
                          out_shape=jax.ShapeDtypeStruct((2, r, c // 4), F32),
                          compiler_params=_params(("parallel", "parallel")), name=name)(w)


def _unpack_rows(name, w0, w1, tr=512):
    r, h = w0.shape
    tr = _row_tile(r, tr, 16)

    def body(w0_ref, w1_ref, o_ref):
        for q, w_ref in enumerate((w0_ref, w1_ref)):
            hi, lo = _unpack_pairs(w_ref[...])
            o_ref[q, :, :h] = hi.astype(BF)
            o_ref[q, :, h:] = lo.astype(BF)

    words = pl.BlockSpec((tr, h), lambda i: (i, 0))
    return pl.pallas_call(body, grid=(r // tr,), in_specs=[words, words],
                          out_specs=pl.BlockSpec((2, tr, 2 * h), lambda i: (0, i, 0)),
                          out_shape=jax.ShapeDtypeStruct((2, r, 2 * h), BF), compiler_params=_params(("parallel",)),
                          name=name)(w0, w1)


def _reduce_packed(name, recv, tr=256):
    nd, r, h = recv.shape
    tr = _row_tile(r, tr)

    def body(p_ref, o_ref):
        hi, lo = _unpack_pairs(p_ref[0])
        for k in range(1, nd):
            a, b = _unpack_pairs(p_ref[k])
            hi, lo = hi + a, lo + b
        o_ref[:, :h] = hi
        o_ref[:, h:] = lo

    return pl.pallas_call(body, grid=(r // tr,), in_specs=[pl.BlockSpec((nd, tr, h), lambda i: (0, i, 0))],
                          out_specs=pl.BlockSpec((tr, 2 * h), lambda i: (i, 0)),
                          out_shape=jax.ShapeDtypeStruct((r, 2 * h), F32), compiler_params=_params(("parallel",)),
                          name=name)(recv)


def _adamw_math(w, g, m, v):
    m2 = ADAM_B1 * m + (1.0 - ADAM_B1) * g
    v2 = ADAM_B2 * v + (1.0 - ADAM_B2) * (g * g)
    m_hat = m2 / (1.0 - ADAM_B1 ** ADAM_STEP)
    v_hat = v2 / (1.0 - ADAM_B2 ** ADAM_STEP)
    delta = -ADAM_LR * (m_hat / (jnp.sqrt(v_hat) + ADAM_EPS) + ADAM_WD * w)
    return delta, m2, v2


def _adamw(name, w, g, m, v, tr=256):
    r, c = w.shape
    tr = _row_tile(r, tr)

    def body(w_ref, g_ref, m_ref, v_ref, d_ref, m2_ref, v2_ref):
        d, m2, v2 = _adamw_math(w_ref[...], g_ref[...], m_ref[...], v_ref[...])
        d_ref[...] = d
        m2_ref[...] = m2
        v2_ref[...] = v2

    blk = pl.BlockSpec((tr, c), lambda i: (i, 0))
    shp = jax.ShapeDtypeStruct((r, c), F32)
    return pl.pallas_call(body, grid=(r // tr,), in_specs=[blk] * 4, out_specs=[blk] * 3, out_shape=[shp] * 3,
                          compiler_params=_params(("parallel",)), name=name)(w, g, m, v)


def _sum_slots(name, a):
    nd, r, c = a.shape

    def body(a_ref, o_ref):
        s = a_ref[0]
        for k in range(1, nd):
            s = s + a_ref[k]
        o_ref[...] = s

    return pl.pallas_call(body, out_shape=jax.ShapeDtypeStruct((r, c), F32), name=name)(a)


def _head_rows(v, g_n, r):
    return jnp.pad(v.reshape(g_n, 1, r), ((0, 0), (0, 0), (0, HEAD_LANES - r)))


class _Later:
    def __init__(self, hosted, weights_of, pool_w_out_of, send):
        self.hosted, self.weights_of, self.pool_w_out_of, self.send = hosted, weights_of, pool_w_out_of, send


def _listed(res):
    return res if isinstance(res, (list, tuple)) else [res]


def _local_step(x, target, ln_g, final_g, wt_in, conv_w, conv_b, dt_bias, a_log, d_skip, norm_g, scale, later):
    l, d = x.shape
    di = norm_g.shape[1]
    h = dt_bias.shape[1]
    p = di // h
    conv = conv_b.shape[1]
    n, t = SSM_STATE, SSD_CHUNK
    g_n = (conv - di) // (2 * n)
    r = h // g_n
    nm = di + conv
    dp = scale.shape[1]
    ng = len(POOL_WINDOWS)
    dims = (h, g_n, r, p, n, t)
    pw = d

    h0, r0 = _rms_fwd("rms0_fwd", x, ln_g[0:1])
    hosted = later.hosted
    pm, *arrived_i = _listed(_mm_nt_halves("in_proj_main", h0, wt_in, 0, nm, exch=hosted.get("in_proj")))
    dtr = _mm_nt_halves("in_proj_dt", h0, wt_in, nm, h, tn=h)
    xbc = _conv_fwd(pm, di, conv_w, conv_b)
    dtb, alog = _head_rows(dt_bias, g_n, r), _head_rows(a_log, g_n, r)
    dskx = jnp.repeat(d_skip.reshape(g_n, 1, r), p, axis=2)
    dtr_g = jnp.pad(jnp.transpose(dtr.reshape(l, g_n, r), (1, 0, 2)), ((0, 0), (0, 0), (0, HEAD_LANES - r)))
    y, states, yn, r_n, *arrived_s = _ssd_fwd(xbc, dtr_g, dtb, alog, dskx, pm, norm_g, dims, exch=hosted.get("ssd"))
    w_out, wg, wp_in = later.weights_of(arrived_i, arrived_s)
    x1 = _mm_nn("ssm_out_proj", yn, w_out, add=x)

    h1, r1 = _rms_fwd("rms1_fwd", x1, ln_g[1:2])
    pu, *arrived_p = _listed(_mm_nn_blocked("pool_in_proj", h1, wp_in, exch=hosted.get("pool_in_proj")))
    wp_out = later.pool_w_out_of(arrived_p)
    mp = _pool_mix("pool_mix_fwd", pu, dp, False, BF)
    mg, yp = _group_mm_fwd(mp, wg, pu, scale)
    x2 = _mm_nn("pool_out_proj", yp, wp_out, add=x1)

    dx2, dx2_b, d_final_g, loss = _loss_head(x2, final_g, target)

    gw_pout = _mm_tn_packed("pool_out_bwd_weight", yp, dx2_b, pw)
    going = later.send("pool_w_out", gw_pout)
    dmg, dpu, d_scale = _pool_out_bwd_data(dx2_b, wp_out, mg, pu, scale)
    dmp = _group_mm_bwd_data(dmg, wg, after=going)
    gw_g = _group_mm_bwd_weight(mp, dmg, ng)
    going = later.send("pool_w_group", gw_g)
    dpu = _pool_mix("pool_mix_bwd", dmp, dp, True, BF, into=dpu)
    dh1 = _mm_nt_blocked("pool_in_bwd_data", dpu, wp_in, after=going)
    gw_pin = _mm_tn_packed("pool_in_bwd_weight", h1, dpu, 2 * dp // N_DEV)
    going = later.send("pool_w_in", gw_pin)
    dx1, dx1_b, d_ln1 = _rms_bwd("rms1_bwd", dh1, x1, r1, ln_g[1:2], dx2)

    dyn = _mm_nt("ssm_out_bwd_data", dx1_b, w_out, 0, di, tn=1024, after=going)
    gw_out = _mm_tn_packed("ssm_out_bwd_weight", yn, dx1_b, pw)
    going = later.send("ssm_w_out", gw_out)
    dxbc, ddt_g, dbias_g, dalog_g, dd_g, dproj, d_norm_g = _ssd_bwd(
        xbc, dtr_g, dtb, alog, dskx, states, dyn, y, pm, r_n, norm_g, dims, after=going)
    dproj, d_conv_w, d_conv_b = _conv_bwd(pm, di, conv_w, conv_b, dxbc, dproj)
    ddt = jnp.transpose(ddt_g[:, :, :r], (1, 0, 2)).reshape(l, h)
    gwt_in, going = [], ()
    for q in range(2):
        gq = _mm_tn_packed(f"in_proj_bwd_weight_{q}", dproj, h0, d // 2, rows=nm + h, b_blk=q, after=going)
        gq = _mm_tn_packed(f"in_proj_bwd_weight_dt_{q}", ddt, h0, d // 2, into=gq, b_blk=q)
        going = later.send(f"ssm_w_in_{q}", gq)
        gwt_in.append(gq)
    gwt_in = jnp.concatenate(gwt_in, axis=0)
    dh0 = _mm_nn_halves("in_proj_bwd_data_dt", ddt, wt_in, nm, after=going)
    dh0 = _mm_nn_halves("in_proj_bwd_data", dproj, wt_in, 0, add=dh0)
    grad_x, _, d_ln0 = _rms_bwd("rms0_bwd", dh0, x, r0, ln_g[0:1], dx1)

    def heads(v):
        return v[:, 0, :r].reshape(1, h)

    small = dict(ln_g=jnp.concatenate([d_ln0, d_ln1], axis=0), final_g=d_final_g, conv_w=d_conv_w, conv_b=d_conv_b,
                 dt_bias=heads(dbias_g), a_log=heads(dalog_g), d_skip=heads(dd_g), norm_g=d_norm_g, scale=d_scale)
    big = dict(ssm_w_in=gwt_in, ssm_w_out=gw_out, pool_w_in=gw_pin, pool_w_group=gw_g, pool_w_out=gw_pout)
    return loss, grad_x, small, big


SMALL_ORDER = ("ln_g", "final_g", "conv_w", "conv_b", "dt_bias", "a_log", "d_skip", "norm_g", "scale", "loss")


def _flatten_small(parts):
    flat = jnp.concatenate([parts[k].reshape(-1) for k in SMALL_ORDER])
    n = flat.shape[0]
    rows = -(-n // 1024) * 8
    return jnp.pad(flat, (0, rows * 128 - n)).reshape(rows, 128)


def _split_small(flat, shapes):
    flat = flat.reshape(-1)
    out, off = {}, 0
    for k in SMALL_ORDER:
        size = int(np.prod(shapes[k]))
        out[k] = flat[off:off + size].reshape(shapes[k])
        off += size
    return out


def kernel(x, ln_g, final_g, ssm_w_in, ssm_conv_w, ssm_conv_b, ssm_dt_bias, ssm_a_log, ssm_d, ssm_norm_g, ssm_w_out, pool_w_in, pool_w_group, pool_scale, pool_w_out, loss_target, m_ln_g, m_final_g, m_ssm_w_in, m_ssm_conv_w, m_ssm_conv_b, m_ssm_dt_bias, m_ssm_a_log, m_ssm_d, m_ssm_norm_g, m_ssm_w_out, m_pool_w_in, m_pool_w_group, m_pool_scale, m_pool_w_out, v_ln_g, v_final_g, v_ssm_w_in, v_ssm_conv_w, v_ssm_conv_b, v_ssm_dt_bias, v_ssm_a_log, v_ssm_d, v_ssm_norm_g, v_ssm_w_out, v_pool_w_in, v_pool_w_group, v_pool_scale, v_pool_w_out):
    l, d = x.shape[1], x.shape[2]
    me = 4 * lax.axis_index("x") + 2 * lax.axis_index("y") + lax.axis_index("c")
    ng, gds, gd = pool_w_group.shape[1], pool_w_group.shape[2], pool_w_group.shape[3]
    sin_s = ssm_w_in.shape[2]
    dp_s = pool_w_out.shape[1]
    conv_s = ssm_conv_w.shape[2]

    wt_in_s = _pack_rows("pack_w_in", jnp.transpose(ssm_w_in[0]))
    w_out_s = _cast_bf16("cast_w_out", ssm_w_out[0])
    wp_in_s = _cast_bf16("cast_pool_w_in", pool_w_in[0])
    wg_s = _cast_bf16("cast_pool_w_group", pool_w_group[0].reshape(ng * gds, gd))
    wp_out_s = _cast_bf16("cast_pool_w_out", pool_w_out[0])
    small_s = jnp.concatenate([ssm_conv_w[0].reshape(-1), pool_scale[0]])
    n_small = small_s.shape[0]
    small_s = jnp.pad(small_s, (0, -(-n_small // 1024) * 1024 - n_small)).reshape(-1, 128)
    wt_in_g0, wt_in_g1, small_g = _exchange_alone("all_gather_w_in", _Gather([wt_in_s[0], wt_in_s[1], small_s]))
    wt_in = _unpack_rows("unpack_w_in", wt_in_g0.reshape(N_DEV * sin_s, d // 4),
                         wt_in_g1.reshape(N_DEV * sin_s, d // 4))
    small_all = small_g.reshape(N_DEV, -1)[:, :n_small]
    conv_w = jnp.transpose(small_all[:, :CONV_TAPS * conv_s].reshape(N_DEV, CONV_TAPS, conv_s), (1, 0, 2))
    conv_w = conv_w.reshape(CONV_TAPS, -1)
    scale = small_all[:, CONV_TAPS * conv_s:].reshape(1, -1)
    hosted = dict(in_proj=_Gather([w_out_s, wg_s]), ssd=_Gather([wp_in_s]), pool_in_proj=_Gather([wp_out_s]))

    def weights_of(arrived_i, arrived_s):
        wg = jnp.transpose(arrived_i[1].reshape(N_DEV, ng, gds, gd), (1, 0, 2, 3)).reshape(ng, gd, gd)
        return arrived_i[0].reshape(-1, d), wg, arrived_s[0]

    def rows_major(gp):
        q, _, hw = gp.shape
        return jnp.transpose(gp.reshape(q, N_DEV, -1, hw), (1, 0, 2, 3))

    to_blocks = dict(ssm_w_in_0=rows_major, ssm_w_in_1=rows_major, ssm_w_out=rows_major, pool_w_out=rows_major,
                     pool_w_in=lambda gp: gp[:, None],
                     pool_w_group=lambda gp: jnp.transpose(gp.reshape(ng, N_DEV, gds, gd // 2), (1, 0, 2, 3)))
    travelling = {}

    def send(name, gp):
        *travelling[name], token = _scatter_start(f"scatter_{name}_start", to_blocks[name](gp))
        return (token,)

    later = _Later(hosted, weights_of, lambda arrived_p: arrived_p[0].reshape(-1, d), send)

    loss, grad_x, small, _ = _local_step(
        x[0], loss_target[0], ln_g, final_g.reshape(1, d), wt_in, conv_w, ssm_conv_b, ssm_dt_bias, ssm_a_log, ssm_d,
        ssm_norm_g, scale, later)

    def arrived(name, after):
        src, land = _scatter_wait(f"scatter_{name}_wait", *travelling[name], after=after)
        own = lax.dynamic_slice_in_dim(src, me, 1, axis=0)
        return lax.dynamic_update_slice_in_dim(land, own, me, axis=0)

    def reduced(name, rv):
        q = rv.shape[1]
        cols = [_reduce_packed(f"{name}_{j}", rv[:, j]) for j in range(q)]
        return cols[0] if q == 1 else jnp.concatenate(cols, axis=1)

    grads = {}
    grads["pool_w_out"] = reduced("reduce_pool_w_out", arrived("pool_w_out", (grad_x,)))[None]
    grads["pool_w_group"] = _reduce_packed("reduce_pool_w_group", arrived("pool_w_group", (grad_x,)).reshape(
        N_DEV, ng * gds, gd // 2)).reshape(1, ng, gds, gd)
    grads["pool_w_in"] = reduced("reduce_pool_w_in", arrived("pool_w_in", (grad_x,)))[None]
    grads["ssm_w_out"] = reduced("reduce_w_out", arrived("ssm_w_out", (grad_x,)))[None]

    weights = dict(ln_g=ln_g, final_g=final_g, ssm_w_in=ssm_w_in, ssm_conv_w=ssm_conv_w, ssm_conv_b=ssm_conv_b,
                   ssm_dt_bias=ssm_dt_bias, ssm_a_log=ssm_a_log, ssm_d=ssm_d, ssm_norm_g=ssm_norm_g,
                   ssm_w_out=ssm_w_out, pool_w_in=pool_w_in, pool_w_group=pool_w_group, pool_scale=pool_scale,
                   pool_w_out=pool_w_out)
    m_in = dict(ln_g=m_ln_g, final_g=m_final_g, ssm_w_in=m_ssm_w_in, ssm_conv_w=m_ssm_conv_w, ssm_conv_b=m_ssm_conv_b,
                ssm_dt_bias=m_ssm_dt_bias, ssm_a_log=m_ssm_a_log, ssm_d=m_ssm_d, ssm_norm_g=m_ssm_norm_g,
                ssm_w_out=m_ssm_w_out, pool_w_in=m_pool_w_in, pool_w_group=m_pool_w_group, pool_scale=m_pool_scale,
                pool_w_out=m_pool_w_out)
    v_in = dict(ln_g=v_ln_g, final_g=v_final_g, ssm_w_in=v_ssm_w_in, ssm_conv_w=v_ssm_conv_w, ssm_conv_b=v_ssm_conv_b,
                ssm_dt_bias=v_ssm_dt_bias, ssm_a_log=v_ssm_a_log, ssm_d=v_ssm_d, ssm_norm_g=v_ssm_norm_g,
                ssm_w_out=v_ssm_w_out, pool_w_in=v_pool_w_in, pool_w_group=v_pool_w_group, pool_scale=v_pool_scale,
                pool_w_out=v_pool_w_out)
    names = list(weights)
    big_names = ("ssm_w_out", "pool_w_in", "pool_w_group", "pool_w_out", "ssm_w_in")
    delta, new_m, new_v = {}, {}, {}

    def adamw_big(k):
        shp = weights[k].shape
        two_d = (-1, shp[-1])
        dk, mk, vk = _adamw(f"adamw_{k}", weights[k].reshape(two_d), grads[k].reshape(two_d), m_in[k].reshape(two_d),
                            v_in[k].reshape(two_d))
        delta[k], new_m[k], new_v[k] = dk.reshape(shp), mk.reshape(shp), vk.reshape(shp)

    for k in big_names[:-1]:
        adamw_big(k)
    small_names = [k for k in names if k not in big_names]

    small["loss"] = loss
    shapes = {k: small[k].shape for k in SMALL_ORDER}
    gathered_small, = _exchange_alone("all_gather_small_grads", _Gather([_flatten_small(small)]),
                                      after=tuple(delta[k] for k in big_names[:-1]))
    summed = _split_small(_sum_slots("sum_small_grads", gathered_small), shapes)
    grads.update(ln_g=summed["ln_g"], final_g=summed["final_g"].reshape(d), ssm_conv_b=summed["conv_b"],
                 ssm_conv_w=lax.dynamic_slice_in_dim(summed["conv_w"], me * conv_s, conv_s, axis=1)[None],
                 ssm_dt_bias=summed["dt_bias"], ssm_a_log=summed["a_log"], ssm_d=summed["d_skip"],
                 ssm_norm_g=summed["norm_g"],
                 pool_scale=lax.dynamic_slice_in_dim(summed["scale"], me * dp_s, dp_s, axis=1))

    def packed(tree):
        flat = jnp.concatenate([tree[k].reshape(-1) for k in small_names])
        nn = flat.shape[0]
        return jnp.pad(flat, (0, -(-nn // 1024) * 1024 - nn), constant_values=1.0).reshape(-1, 128)

    ds, ms, vs = _adamw("adamw_small", packed(weights), packed(grads), packed(m_in), packed(v_in))
    off = 0
    for k in small_names:
        shp = weights[k].shape
        size = int(np.prod(shp))
        for res, arr in ((delta, ds), (new_m, ms), (new_v, vs)):
            res[k] = arr.reshape(-1)[off:off + size].reshape(shp)
        off += size

    k = "ssm_w_in"
    gt = jnp.concatenate([reduced(f"reduce_w_in_{q}", arrived(f"ssm_w_in_{q}", (ds,))) for q in range(2)], axis=1)
    dk, mk, vk = _adamw("adamw_ssm_w_in", jnp.transpose(weights[k][0]), gt, jnp.transpose(m_in[k][0]),
                        jnp.transpose(v_in[k][0]))
    grads[k], delta[k], new_m[k], new_v[k] = (jnp.transpose(t)[None] for t in (gt, dk, mk, vk))

    return (summed["loss"].reshape(()), grad_x[None], *[grads[k] for k in names], *[delta[k] for k in names],
            *[new_m[k] for k in names], *[new_v[k] for k in names])
```

```python
import math

import jax
import jax.numpy as jnp
import numpy as np
from jax import lax
from jax.experimental import pallas as pl
from jax.experimental.pallas import tpu as pltpu

F32 = jnp.float32
BF = jnp.bfloat16
U32 = jnp.uint32

N_DEV = 8
EPS = 1e-6
SSD_CHUNK = 64
SSM_STATE = 128
CONV_TAPS = 4
HEAD_LANES = 128
POOL_WINDOWS = (2, 4, 8, 16)
ADAM_LR, ADAM_B1, ADAM_B2, ADAM_EPS, ADAM_WD, ADAM_STEP = 0.001, 0.9, 0.999, 1e-08, 0.01, 10
VMEM_LIMIT = 56 * 1024 * 1024
NEG_BIG = -1e30
HAND_ON_AT = 6

NN = ((1,), (0,))
NT = ((1,), (1,))
TN = ((0,), (0,))
MESH = pl.DeviceIdType.MESH


def _t(dim, pref):
    return pref if dim % pref == 0 else dim


def _row_tile(rows, pref, mult=8):
    best = rows
    for cand in range(mult, min(rows, pref) + 1, mult):
        if rows % cand == 0:
            best = cand
    return best


def _params(sem=None):
    return pltpu.CompilerParams(dimension_semantics=sem, vmem_limit_bytes=VMEM_LIMIT)


def _silu(x):
    return x * (1.0 / (1.0 + jnp.exp(-x)))


def _dsilu(x):
    s = 1.0 / (1.0 + jnp.exp(-x))
    return s * (1.0 + x * (1.0 - s))


def _bdot(a, b, dims=NN):
    return lax.dot_general(a.astype(BF), b.astype(BF), (dims, ((), ())), preferred_element_type=F32)


def _pack_pairs(x):
    h = x.shape[1] // 2
    hi = lax.bitcast_convert_type(x[:, :h].astype(jnp.bfloat16).astype(F32), U32)
    lo = lax.bitcast_convert_type(x[:, h:].astype(jnp.bfloat16).astype(F32), U32)
    return lax.bitcast_convert_type(hi | (lo >> 16), F32)


def _unpack_pairs(w):
    u = lax.bitcast_convert_type(w, U32)
    hi = lax.bitcast_convert_type(u & jnp.uint32(0xFFFF0000), F32)
    lo = lax.bitcast_convert_type(u << 16, F32)
    return hi, lo


def _mesh_pos():
    return lax.axis_index("x"), lax.axis_index("y"), lax.axis_index("c")


def _slot(pos):
    return 4 * pos[0] + 2 * pos[1] + pos[2]


class _Gather:
    def __init__(self, arrays):
        self.arrays = list(arrays)
        self.out_shapes = [jax.ShapeDtypeStruct((N_DEV, *s.shape), s.dtype) for s in arrays]

    def phases(self, src, dst, send_sems, recv_sems, local_sems):
        n_arr = len(self.arrays)
        x, y, c = _mesh_pos()
        me, sibling = (x, y, c), (x, y, 1 - c)
        chips = [(1 - x, y), (x, 1 - y), (1 - x, 1 - y)]

        def copy(a, k, block, to, from_src):
            return pltpu.make_async_remote_copy(
                src_ref=src[a] if from_src else dst[a].at[_slot(block)], dst_ref=dst[a].at[_slot(block)],
                send_sem=send_sems.at[a * 7 + k], recv_sem=recv_sems.at[a * 7 + k], device_id=to, device_id_type=MESH)

        def mine(a):
            return pltpu.make_async_copy(src[a], dst[a].at[_slot(me)], local_sems.at[a])

        def first(a):
            return [copy(a, 0, me, sibling, True)] + [copy(a, 1 + j, me, (*chip, c), True)
                                                     for j, chip in enumerate(chips)]

        def start():
            for a in range(n_arr):
                mine(a).start()
                for cp in first(a):
                    cp.start()

        def middle():
            for j, chip in enumerate(chips):
                for a in range(n_arr):
                    copy(a, 1 + j, (*chip, c), me, False).wait_recv()
                    copy(a, 4 + j, (*chip, c), sibling, False).start()

        def finish():
            for a in range(n_arr):
                copy(a, 0, sibling, me, False).wait_recv()
                for j, chip in enumerate(chips):
                    copy(a, 4 + j, (*chip, 1 - c), me, False).wait_recv()
                for cp in first(a):
                    cp.wait_send()
                for j, chip in enumerate(chips):
                    copy(a, 4 + j, (*chip, c), sibling, False).wait_send()
                mine(a).wait()

        return start, middle, finish


def _hosted_call(name, body, grid, in_specs, out_specs, out_shape, scratch_shapes, sem, operands, exch=None,
                 after=(), into=None):
    if into is not None:
        n_lead = len(in_specs)
        inner = body

        def body(*refs):
            inner(*refs[:n_lead], *refs[n_lead + 1:])

        return pl.pallas_call(
            body, grid=grid, in_specs=[*in_specs, pl.BlockSpec(memory_space=pl.ANY)], out_specs=out_specs,
            out_shape=out_shape, scratch_shapes=scratch_shapes, input_output_aliases={n_lead: 0},
            compiler_params=_params(sem), name=name)(*operands, into)
    if after:
        n_lead = len(in_specs)
        inner = body

        def body(*refs):
            inner(*refs[:n_lead], *refs[n_lead + len(after):])

        in_specs = [*in_specs, *[pl.BlockSpec(memory_space=pl.ANY)] * len(after)]
        operands = (*operands, *after)
    if exch is None:
        return pl.pallas_call(body, grid=grid, in_specs=in_specs, out_specs=out_specs, out_shape=out_shape,
                              scratch_shapes=scratch_shapes, compiler_params=_params(sem), name=name)(*operands)
    n_in, n_out, n_scr, ne = len(in_specs), len(out_specs), len(scratch_shapes), len(exch.arrays)
    total = math.prod(grid)

    def wrapped(*refs):
        ins, ex_in = refs[:n_in], refs[n_in:n_in + ne]
        outs = refs[n_in + ne:n_in + ne + n_out]
        ex_out = refs[n_in + ne + n_out:n_in + 2 * ne + n_out]
        scr = refs[n_in + 2 * ne + n_out:n_in + 2 * ne + n_out + n_scr]
        step = 0
        for axis, size in enumerate(grid):
            step = step * size + pl.program_id(axis)
        start, middle, finish = exch.phases(ex_in, ex_out, *refs[-3:])
        pl.when(step == 0)(start)
        if middle is not None:
            pl.when(step == (total * HAND_ON_AT) // 8)(middle)
        body(*ins, *outs, *scr)
        pl.when(step == total - 1)(finish)

    hbm = pl.BlockSpec(memory_space=pl.ANY)
    sems = [pltpu.SemaphoreType.DMA((ne * 7,)), pltpu.SemaphoreType.DMA((ne * 7,)), pltpu.SemaphoreType.DMA((ne,))]
    return pl.pallas_call(
        wrapped, grid=grid, in_specs=[*in_specs, *[hbm] * ne], out_specs=[*out_specs, *[hbm] * ne],
        out_shape=[*out_shape, *exch.out_shapes], scratch_shapes=[*scratch_shapes, *sems],
        compiler_params=pltpu.CompilerParams(dimension_semantics=("arbitrary",) * len(grid),
                                             vmem_limit_bytes=VMEM_LIMIT, has_side_effects=True),
        name=name)(*operands, *exch.arrays)


def _flip_peers():
    x, y, c = _mesh_pos()
    peers = []
    for k in range(1, N_DEV):
        fx, fy, fc = (k >> 2) & 1, (k >> 1) & 1, k & 1
        peers.append((1 - x if fx else x, 1 - y if fy else y, 1 - c if fc else c))
    return (x, y, c), peers


def _split_scatter_copy(src, land, send_sems, recv_sems, k, me, peer, sending):
    return pltpu.make_async_remote_copy(
        src_ref=src.at[_slot(peer)], dst_ref=land.at[_slot(me) if sending else _slot(peer)],
        send_sem=send_sems.at[k], recv_sem=recv_sems.at[k], device_id=peer, device_id_type=MESH)


def _scatter_start(name, blocks):
    def body(src, land, send_sems, recv_sems, src_thru, land_thru, token):
        me, peers = _flip_peers()
        for k, peer in enumerate(peers):
            _split_scatter_copy(src, land, send_sems, recv_sems, k, me, peer, True).start()
        token[...] = jnp.zeros_like(token)

    hbm = pl.BlockSpec(memory_space=pltpu.HBM)
    sem = pl.BlockSpec(memory_space=pltpu.SEMAPHORE)
    return pl.pallas_call(
        body, name=name,
        out_shape=(pltpu.SemaphoreType.DMA((N_DEV - 1,)), pltpu.SemaphoreType.DMA((N_DEV - 1,)),
                   pltpu.HBM(blocks.shape, blocks.dtype), pltpu.HBM(blocks.shape, blocks.dtype),
                   jax.ShapeDtypeStruct((8, 128), F32)),
        in_specs=(hbm, hbm), out_specs=(sem, sem, hbm, hbm, pl.BlockSpec(memory_space=pltpu.VMEM)),
        input_output_aliases={0: 2, 1: 3},
        compiler_params=pltpu.CompilerParams(has_side_effects=pltpu.SideEffectType.DATAFLOW_SIDE_EFFECTING),
    )(pltpu.with_memory_space_constraint(blocks, pltpu.HBM),
      pltpu.with_memory_space_constraint(lax.empty(blocks.shape, blocks.dtype), pltpu.HBM))


def _scatter_wait(name, send_sems, recv_sems, src_thru, land_thru, after):
    def body(src, land, send_sems, recv_sems, *rest):
        me, peers = _flip_peers()
        for k, peer in enumerate(peers):
            _split_scatter_copy(src, land, send_sems, recv_sems, k, me, peer, True).wait_send()
            _split_scatter_copy(src, land, send_sems, recv_sems, k, me, peer, False).wait_recv()

    hbm = pl.BlockSpec(memory_space=pltpu.HBM)
    sem = pl.BlockSpec(memory_space=pltpu.SEMAPHORE)
    return pl.pallas_call(
        body, name=name,
        out_shape=(pltpu.HBM(src_thru.shape, src_thru.dtype), pltpu.HBM(land_thru.shape, land_thru.dtype)),
        in_specs=(hbm, hbm, sem, sem, *[pl.BlockSpec(memory_space=pl.ANY)] * len(after)), out_specs=(hbm, hbm),
        input_output_aliases={0: 0, 1: 1},
        compiler_params=pltpu.CompilerParams(has_side_effects=pltpu.SideEffectType.DATAFLOW_SIDE_EFFECTING),
    )(src_thru, land_thru, send_sems, recv_sems, *after)


def _exchange_alone(name, exch, after=()):
    def body():
        pass

    return _hosted_call(name, body, (1,), [], [], [], [], None, (), exch, after)


def _mm(name, grid, a, a_spec, b, b_spec, dims, outs, o_specs, acc_shape, extra=(), extra_specs=(), epi=None,
        exch=None, after=(), into=None, summed=None):
    nk = grid[-1]
    n_extra, n_out = len(extra), len(outs)

    def body(*refs):
        a_ref, b_ref = refs[0], refs[1]
        ex = refs[2:2 + n_extra]
        o_refs = refs[2 + n_extra:2 + n_extra + n_out]

        def write(res):
            res = (res,) if epi is None else epi(res, *[e[...] for e in ex])
            for idx, (o, r) in enumerate(zip(o_refs, res)):
                if summed is not None and idx == summed[0]:
                    first = pl.program_id(summed[1]) == 0

                    @pl.when(first)
                    def _(o=o, r=r):
                        o[...] = r.astype(o.dtype)

                    @pl.when(jnp.logical_not(first))
                    def _(o=o, r=r):
                        o[...] += r.astype(o.dtype)
                else:
                    o[...] = r.astype(o.dtype)

        if nk == 1:
            write(_bdot(a_ref[...], b_ref[...], dims))
            return
        acc = refs[-1]
        k = pl.program_id(len(grid) - 1)

        @pl.when(k == 0)
        def _():
            acc[...] = _bdot(a_ref[...], b_ref[...], dims)

        @pl.when(jnp.logical_and(k > 0, k < nk - 1))
        def _():
            acc[...] += _bdot(a_ref[...], b_ref[...], dims)

        @pl.when(k == nk - 1)
        def _():
            write(acc[...] + _bdot(a_ref[...], b_ref[...], dims))

    sem = ("parallel",) * (len(grid) - 1) + ("arbitrary",)
    if summed is not None:
        sem = ("arbitrary",) * len(grid)
    scratch = [] if nk == 1 else [pltpu.VMEM(acc_shape, F32)]
    return _hosted_call(name, body, grid, [a_spec, b_spec, *extra_specs], list(o_specs), list(outs),
                        scratch, sem, (a, b, *extra), exch, after, into)


def _add_epi(acc, add):
    return (acc + add,)


def _pack_epi(acc):
    return (_pack_pairs(acc),)


def _mm_nt(name, a, b, n_off, n, out_dtype=F32, tm=1024, tn=512, exch=None, after=()):
    m, kk = a.shape
    tm, tn = _t(m, tm), math.gcd(_t(n, tn), n_off)
    res = _mm(name, (m // tm, n // tn, 1),
              a, pl.BlockSpec((tm, kk), lambda i, j, k: (i, 0)),
              b, pl.BlockSpec((tn, kk), lambda i, j, k: (n_off // tn + j, 0)), NT,
              [jax.ShapeDtypeStruct((m, n), out_dtype)], [pl.BlockSpec((tm, tn), lambda i, j, k: (i, j))], (tm, tn),
              exch=exch, after=after)
    return res[0] if exch is None else res


def _mm_nt_halves(name, a, b2, n_off, n, tm=1024, tn=512, exch=None):
    m = a.shape[0]
    kh = b2.shape[2]
    tm, tn = _t(m, tm), math.gcd(_t(n, tn), n_off)

    def body(a_ref, b_ref, o_ref):
        o_ref[...] = _bdot(a_ref[:, :kh], b_ref[0], NT) + _bdot(a_ref[:, kh:], b_ref[1], NT)

    res = _hosted_call(name, body, (m // tm, n // tn),
                       [pl.BlockSpec((tm, 2 * kh), lambda i, j: (i, 0)),
                        pl.BlockSpec((2, tn, kh), lambda i, j: (0, n_off // tn + j, 0))],
                       [pl.BlockSpec((tm, tn), lambda i, j: (i, j))], [jax.ShapeDtypeStruct((m, n), F32)], [],
                       ("parallel", "parallel"), (a, b2), exch)
    return res[0] if exch is None else res


def _mm_nn_halves(name, a, b2, k_off=0, add=None, after=(), tm=1024, tk=1024):
    m, kk = a.shape
    half = b2.shape[2]
    tm, tk = _t(m, tm), math.gcd(_t(kk, tk), k_off)
    extra, especs, epi = (), (), None
    if add is not None:
        extra, especs, epi = (add,), (pl.BlockSpec((tm, half), lambda i, j, k: (i, j)),), _add_epi
    return _mm(name, (m // tm, 2, kk // tk),
               a, pl.BlockSpec((tm, tk), lambda i, j, k: (i, k)),
               b2, pl.BlockSpec((None, tk, half), lambda i, j, k: (j, k_off // tk + k, 0)), NN,
               [jax.ShapeDtypeStruct((m, 2 * half), F32)], [pl.BlockSpec((tm, half), lambda i, j, k: (i, j))],
               (tm, half), extra, especs, epi, after=after)[0]


def _mm_nn_blocked(name, a, b3, tm=1024, tn=512, exch=None):
    m, kk = a.shape
    nb, _, cs = b3.shape
    tm, tn = _t(m, tm), _t(cs, tn)
    per = cs // tn
    res = _mm(name, (m // tm, nb * per, 1),
              a, pl.BlockSpec((tm, kk), lambda i, j, k: (i, 0)),
              b3, pl.BlockSpec((None, kk, tn), lambda i, j, k: (j // per, 0, j % per)), NN,
              [jax.ShapeDtypeStruct((m, nb * cs), F32)], [pl.BlockSpec((tm, tn), lambda i, j, k: (i, j))],
              (tm, tn), exch=exch)
    return res[0] if exch is None else res


def _mm_nt_blocked(name, a, b3, after=(), tm=1024, tn=1024):
    m = a.shape[0]
    nb, n, cs = b3.shape
    tm, tn = _t(m, tm), _t(n, tn)
    return _mm(name, (m // tm, n // tn, nb),
               a, pl.BlockSpec((tm, cs), lambda i, j, k: (i, k)),
               b3, pl.BlockSpec((None, tn, cs), lambda i, j, k: (k, j, 0)), NT,
               [jax.ShapeDtypeStruct((m, n), F32)], [pl.BlockSpec((tm, tn), lambda i, j, k: (i, j))], (tm, tn),
               after=after)[0]


def _mm_nn(name, a, b, k_off=0, kk=None, add=None, out_dtype=F32, tm=1024, tn=1024, tk=1024, after=()):
    m = a.shape[0]
    kk = a.shape[1] if kk is None else kk
    n = b.shape[1]
    tm, tn, tk = _t(m, tm), _t(n, tn), math.gcd(_t(kk, tk), k_off)
    extra, especs, epi = (), (), None
    if add is not None:
        extra, especs, epi = (add,), (pl.BlockSpec((tm, tn), lambda i, j, k: (i, j)),), _add_epi
    return _mm(name, (m // tm, n // tn, kk // tk),
               a, pl.BlockSpec((tm, tk), lambda i, j, k: (i, k)),
               b, pl.BlockSpec((tk, tn), lambda i, j, k: (k_off // tk + k, j)), NN,
               [jax.ShapeDtypeStruct((m, n), out_dtype)], [pl.BlockSpec((tm, tn), lambda i, j, k: (i, j))], (tm, tn),
               extra, especs, epi, after=after)[0]


TN_ACC_ELEMENTS = 1 << 20


def _mm_tn_packed(name, a, b, pw, tk=1024, rows=None, into=None, b_blk=None, after=()):
    kk, m = a.shape
    nq = b.shape[1] // pw if b_blk is None else 1
    first = 0 if b_blk is None else b_blk
    tm, tk = _t(m, TN_ACC_ELEMENTS // pw), _t(kk, tk)
    rows = m if rows is None else rows
    row_blk = 0
    if into is not None:
        rows = into.shape[1]
        assert (rows - m) % tm == 0
        row_blk = (rows - m) // tm
    return _mm(name, (m // tm, nq, kk // tk),
               a, pl.BlockSpec((tk, tm), lambda i, j, k: (k, i)),
               b, pl.BlockSpec((tk, pw), lambda i, j, k: (k, first + j)), TN,
               [jax.ShapeDtypeStruct((nq, rows, pw // 2), F32)],
               [pl.BlockSpec((None, tm, pw // 2), lambda i, j, k: (j, row_blk + i, 0))], (tm, pw), epi=_pack_epi,
               into=into, after=after)[0]


def _rms_fwd(name, x, g, tm=256, exch=None):
    l, d = x.shape
    tm = _t(l, tm)

    def body(x_ref, g_ref, h_ref, r_ref):
        xv = x_ref[...]
        r = lax.rsqrt(jnp.mean(xv * xv, axis=-1, keepdims=True) + EPS)
        h_ref[...] = (xv * r * g_ref[...]).astype(BF)
        r_ref[...] = r

    return _hosted_call(
        name, body, (l // tm,),
        [pl.BlockSpec((tm, d), lambda i: (i, 0)), pl.BlockSpec((1, d), lambda i: (0, 0))],
        [pl.BlockSpec((tm, d), lambda i: (i, 0)), pl.BlockSpec((tm, 1), lambda i: (i, 0))],
        [jax.ShapeDtypeStruct((l, d), BF), jax.ShapeDtypeStruct((l, 1), F32)], [], ("parallel",), (x, g), exch)


def _rms_bwd(name, dh, x, r, g, dres, tm=256):
    l, d = x.shape
    tm = _t(l, tm)

    def body(dh_ref, x_ref, r_ref, g_ref, dres_ref, dx_ref, dxb_ref, dg_ref):
        i = pl.program_id(0)
        rr = r_ref[...]
        xhat = x_ref[...] * rr
        dhv = dh_ref[...]
        dxh = dhv * g_ref[...]
        dx = dres_ref[...] + rr * (dxh - xhat * jnp.mean(dxh * xhat, axis=-1, keepdims=True))
        dx_ref[...] = dx
        dxb_ref[...] = dx.astype(BF)

        @pl.when(i == 0)
        def _():
            dg_ref[...] = jnp.zeros_like(dg_ref)

        dg_ref[...] += jnp.sum(dhv * xhat, axis=0, keepdims=True)

    row = pl.BlockSpec((tm, d), lambda i: (i, 0))
    vec = pl.BlockSpec((1, d), lambda i: (0, 0))
    return pl.pallas_call(
        body, grid=(l // tm,), in_specs=[row, row, pl.BlockSpec((tm, 1), lambda i: (i, 0)), vec, row],
        out_specs=[row, row, vec],
        out_shape=[jax.ShapeDtypeStruct((l, d), F32), jax.ShapeDtypeStruct((l, d), BF),
                   jax.ShapeDtypeStruct((1, d), F32)],
        compiler_params=_params(("arbitrary",)), name=name)(dh, x, r, g, dres)


def _loss_head(x, g, target, tm=256):
    l, d = x.shape
    tm = _t(l, tm)

    def body(x_ref, g_ref, t_ref, dx_ref, dxb_ref, dg_ref, loss_ref):
        i = pl.program_id(0)
        xv = x_ref[...]
        gv = g_ref[...]
        r = lax.rsqrt(jnp.mean(xv * xv, axis=-1, keepdims=True) + EPS)
        xhat = xv * r
        e = xhat * gv - t_ref[...]
        dy = e * (1.0 / d)
        dxh = dy * gv
        dx = r * (dxh - xhat * jnp.mean(dxh * xhat, axis=-1, keepdims=True))
        dx_ref[...] = dx
        dxb_ref[...] = dx.astype(BF)

        @pl.when(i == 0)
        def _():
            dg_ref[...] = jnp.zeros_like(dg_ref)
            loss_ref[...] = jnp.zeros_like(loss_ref)

        dg_ref[...] += jnp.sum(dy * xhat, axis=0, keepdims=True)
        loss_ref[...] += 0.5 * jnp.sum(jnp.sum(e * e, axis=-1, keepdims=True) * (1.0 / d), axis=0, keepdims=True)

    row = pl.BlockSpec((tm, d), lambda i: (i, 0))
    vec = pl.BlockSpec((1, d), lambda i: (0, 0))
    return pl.pallas_call(
        body, grid=(l // tm,), in_specs=[row, vec, row],
        out_specs=[row, row, vec, pl.BlockSpec((1, 1), lambda i: (0, 0))],
        out_shape=[jax.ShapeDtypeStruct((l, d), F32), jax.ShapeDtypeStruct((l, d), BF),
                   jax.ShapeDtypeStruct((1, d), F32), jax.ShapeDtypeStruct((1, 1), F32)],
        compiler_params=_params(("arbitrary",)), name="loss_head")(x, g, target)


CONV_HALO = 8


def _conv_pre(x_ref, w_ref, b_ref, i, tr):
    r0 = pl.multiple_of(i * tr, tr)
    cur = x_ref[pl.ds(r0, tr), :]
    prev = x_ref[pl.ds(pl.multiple_of(jnp.maximum(r0 - CONV_HALO, 0), CONV_HALO), CONV_HALO), :]
    prev = jnp.where(i > 0, prev, 0.0)
    ext = jnp.concatenate([prev, cur], axis=0)
    taps = []
    for k in range(CONV_TAPS):
        s = CONV_TAPS - 1 - k
        taps.append(cur if s == 0 else pltpu.roll(ext, s, 0)[CONV_HALO:])
    pre = b_ref[...] + sum(w_ref[k:k + 1, :] * taps[k] for k in range(CONV_TAPS))
    return r0, pre, taps


def _conv_fwd(pm, col_off, conv_w, conv_b, cw=256, tr=512):
    l = pm.shape[0]
    c = conv_w.shape[1]
    cw, tr = _t(c, cw), _t(l, tr)
    assert col_off % cw == 0

    def body(x_ref, w_ref, b_ref, o_ref):
        def step(i, carry):
            r0, pre, _ = _conv_pre(x_ref, w_ref, b_ref, i, tr)
            o_ref[pl.ds(r0, tr), :] = _silu(pre)
            return carry
        lax.fori_loop(0, l // tr, step, 0)

    return pl.pallas_call(
        body, grid=(c // cw,),
        in_specs=[pl.BlockSpec((l, cw), lambda j: (0, col_off // cw + j)),
                  pl.BlockSpec((CONV_TAPS, cw), lambda j: (0, j)), pl.BlockSpec((1, cw), lambda j: (0, j))],
        out_specs=pl.BlockSpec((l, cw), lambda j: (0, j)), out_shape=jax.ShapeDtypeStruct((l, c), F32),
        compiler_params=_params(("parallel",)), name="conv_fwd")(pm, conv_w, conv_b)


def _conv_bwd(pm, col_off, conv_w, conv_b, dy, dproj, cw=256, tr=512):
    l = pm.shape[0]
    c = conv_w.shape[1]
    cw, tr = _t(c, cw), _t(l, tr)
    nt = l // tr

    def body(x_ref, w_ref, b_ref, dy_ref, _, dx_ref, dw_ref, db_ref, dpre_ref):
        def step1(i, carry):
            dws, db = carry
            r0, pre, taps = _conv_pre(x_ref, w_ref, b_ref, i, tr)
            dpre = dy_ref[pl.ds(r0, tr), :] * _dsilu(pre)
            dpre_ref[pl.ds(r0, tr), :] = dpre
            dws = tuple(dws[k] + jnp.sum(dpre * taps[k], axis=0, keepdims=True) for k in range(CONV_TAPS))
            return dws, db + jnp.sum(dpre, axis=0, keepdims=True)

        z = jnp.zeros((1, cw), F32)
        dws, db = lax.fori_loop(0, nt, step1, ((z,) * CONV_TAPS, z))
        for k in range(CONV_TAPS):
            dw_ref[k:k + 1, :] = dws[k]
        db_ref[...] = db

        def step2(i, carry):
            r0 = pl.multiple_of(i * tr, tr)
            cur = dpre_ref[pl.ds(r0, tr), :]
            nxt = dpre_ref[pl.ds(pl.multiple_of(jnp.minimum(r0 + tr, l - CONV_HALO), CONV_HALO), CONV_HALO), :]
            nxt = jnp.where(i < nt - 1, nxt, 0.0)
            ext = jnp.concatenate([cur, nxt], axis=0)
            acc = w_ref[CONV_TAPS - 1:CONV_TAPS, :] * cur
            for k in range(CONV_TAPS - 1):
                s = CONV_TAPS - 1 - k
                acc = acc + w_ref[k:k + 1, :] * pltpu.roll(ext, tr + CONV_HALO - s, 0)[:tr]
            dx_ref[pl.ds(r0, tr), :] = acc.astype(dx_ref.dtype)
            return carry
        lax.fori_loop(0, nt, step2, 0)

    col = pl.BlockSpec((l, cw), lambda j: (0, j))
    shifted = pl.BlockSpec((l, cw), lambda j: (0, col_off // cw + j))
    return pl.pallas_call(
        body, grid=(c // cw,),
        in_specs=[shifted, pl.BlockSpec((CONV_TAPS, cw), lambda j: (0, j)), pl.BlockSpec((1, cw), lambda j: (0, j)),
                  col, pl.BlockSpec(memory_space=pl.ANY)],
        out_specs=[shifted, pl.BlockSpec((CONV_TAPS, cw), lambda j: (0, j)), pl.BlockSpec((1, cw), lambda j: (0, j))],
        out_shape=[jax.ShapeDtypeStruct(dproj.shape, dproj.dtype), jax.ShapeDtypeStruct((CONV_TAPS, c), F32),
                   jax.ShapeDtypeStruct((1, c), F32)],
        scratch_shapes=[pltpu.VMEM((l, cw), F32)], input_output_aliases={4: 0},
        compiler_params=_params(("parallel",)), name="conv_bwd")(pm, conv_w, conv_b, dy, dproj)


def _split(x, pieces):
    out, rest = [], x
    for _ in range(pieces):
        piece = rest.astype(BF)
        out.append(piece)
        rest = rest - piece.astype(F32)
    return out


def _rows_times(xs, m_stack, pieces):
    x = xs[0] if len(xs) == 1 else jnp.concatenate(xs, axis=0)
    out = _bdot(jnp.concatenate(_split(x, pieces), axis=1), m_stack)
    sizes = [v.shape[0] for v in xs]
    offs = np.cumsum([0] + sizes)
    return [out[offs[i]:offs[i + 1]] for i in range(len(xs))]


def _times_rows(m_stack, x, pieces):
    return _bdot(m_stack, jnp.concatenate(_split(x, pieces), axis=0))


EXPAND_PIECES = 3
FOLD_PIECES = 2


def _ssd_consts(r, p, t):
    assert p == t, "heads expand to P lanes of the inputs and to T lanes of the decay matrices alike"
    rp = r * p
    tri = np.tril(np.ones((t, t), np.float32))
    ep = np.zeros((HEAD_LANES, rp), np.float32)
    ep[np.arange(rp) // p, np.arange(rp)] = 1.0
    itile = (np.arange(t)[:, None] == (np.arange(rp) % t)[None, :]).astype(np.float32)
    lmask = (np.arange(t)[:, None] >= (np.arange(rp) % t)[None, :]).astype(np.float32)
    bmask = ((np.arange(rp) // t)[:, None] == (np.arange(rp) // p)[None, :]).astype(np.float32)
    return [jnp.asarray(np.concatenate([ep] * EXPAND_PIECES, axis=0), BF),
            jnp.asarray(np.concatenate([ep.T] * FOLD_PIECES, axis=0), BF),
            jnp.asarray(np.concatenate([tri] * EXPAND_PIECES, axis=1), BF),
            jnp.asarray(np.concatenate([tri.T] * EXPAND_PIECES, axis=1), BF),
            jnp.asarray(ep.T.copy()), jnp.asarray(itile), jnp.asarray(lmask), jnp.asarray(bmask),
            jnp.asarray(bmask, BF)]


def _ssd_common(xs, bm, pre_raw, dtb, alog, e_stack, tri_stack, itile, lmask, bmask, r, t):
    pre = pre_raw + dtb
    dt = jnp.maximum(pre, 0.0) + jnp.log(1.0 + jnp.exp(-jnp.abs(pre)))
    a = -jnp.exp(alog)
    cs = _times_rows(tri_stack, dt * a, EXPAND_PIECES)
    last = cs[t - 1:t, :]
    dtx, csx = _rows_times([dt, cs], e_stack, EXPAND_PIECES)
    ecsx = jnp.exp(csx)
    wx = jnp.exp(csx[t - 1:t, :] - csx)
    csrow = jnp.sum(csx * itile, axis=0, keepdims=True)
    lx = jnp.exp(jnp.where(lmask > 0.0, csx - csrow, NEG_BIG))
    xdt = xs * dtx
    xblk = jnp.concatenate([xdt.astype(BF)] * r, axis=0) * bmask
    btile = jnp.concatenate([bm.astype(BF)] * r, axis=0)
    return pre, dt, a, last, dtx, ecsx, wx, lx, xdt, xblk, btile


def _ssd_specs(l, g_n, r, p, n, t, conv, rev):
    nc = l // t
    rp = r * p
    cidx = (lambda c: nc - 1 - c) if rev else (lambda c: c)
    row_spec = lambda width: pl.BlockSpec((t, width), lambda c: (cidx(c), 0))
    dtr_spec = pl.BlockSpec((g_n, t, HEAD_LANES), lambda c: (0, cidx(c), 0))
    par_spec = pl.BlockSpec((g_n, 1, HEAD_LANES), lambda c: (0, 0, 0))
    dskx_spec = pl.BlockSpec((g_n, 1, rp), lambda c: (0, 0, 0))
    st_spec = pl.BlockSpec((None, g_n, rp, n), lambda c: (cidx(c), 0, 0, 0))
    return nc, row_spec, dtr_spec, par_spec, dskx_spec, st_spec


def _const_specs(consts):
    return [pl.BlockSpec(a.shape, lambda c: (0, 0)) for a in consts]


def _ssd_fwd(xbc, dtr, dtb, alog, dskx, pm, norm_g, dims, exch=None):
    h, g_n, r, p, n, t = dims
    l, conv = xbc.shape
    rp, hp = r * p, h * p
    nc, row_spec, dtr_spec, par_spec, dskx_spec, st_spec = _ssd_specs(l, g_n, r, p, n, t, conv, False)
    consts = _ssd_consts(r, p, t)

    def body(x_ref, dtr_ref, dtb_ref, alog_ref, dskx_ref, z_ref, ng_ref,
             e_ref, et_ref, tri_ref, trit_ref, ept_ref, it_ref, lm_ref, bmk_ref, bmkb_ref,
             y_ref, st_ref, yn_ref, rn_ref, s_ref):
        @pl.when(pl.program_id(0) == 0)
        def _():
            s_ref[...] = jnp.zeros_like(s_ref)

        for g in range(g_n):
            xs = x_ref[:, g * rp:(g + 1) * rp]
            bm = x_ref[:, hp + g * n:hp + (g + 1) * n]
            cm = x_ref[:, hp + (g_n + g) * n:hp + (g_n + g + 1) * n]
            (_, _, _, last, _, ecsx, wx, lx, xdt, xblk, btile) = _ssd_common(
                xs, bm, dtr_ref[g], dtb_ref[g], alog_ref[g], e_ref[...], tri_ref[...], it_ref[...], lm_ref[...],
                bmkb_ref[...], r, t)
            s_in = s_ref[g]
            st_ref[g] = s_in
            cbx = _bdot(cm, btile, NT)
            yd = _bdot(cbx * lx, xblk)
            yo = ecsx * _bdot(cm, s_in, NT)
            y_ref[:, g * rp:(g + 1) * rp] = yd + yo + dskx_ref[g] * xs
            elast = jnp.sum(ept_ref[...] * jnp.exp(last), axis=1, keepdims=True)
            s_ref[g] = elast * s_in + _bdot(xdt * wx, bm, TN)

        v = y_ref[...] * _silu(z_ref[...])
        rr = lax.rsqrt(jnp.mean(v * v, axis=-1, keepdims=True) + EPS)
        yn_ref[...] = (v * rr * ng_ref[...]).astype(BF)
        rn_ref[...] = rr

    return _hosted_call(
        "ssd_fwd", body, (nc,),
        [row_spec(conv), dtr_spec, par_spec, par_spec, dskx_spec, row_spec(hp),
         pl.BlockSpec((1, hp), lambda c: (0, 0)), *_const_specs(consts)],
        [row_spec(hp), st_spec, row_spec(hp), row_spec(1)],
        [jax.ShapeDtypeStruct((l, hp), F32), jax.ShapeDtypeStruct((nc, g_n, rp, n), F32),
         jax.ShapeDtypeStruct((l, hp), BF), jax.ShapeDtypeStruct((l, 1), F32)],
        [pltpu.VMEM((g_n, rp, n), F32)], ("arbitrary",),
        (xbc, dtr, dtb, alog, dskx, pm, norm_g, *consts), exch)


def _ssd_bwd(xbc, dtr, dtb, alog, dskx, states, dyn, y, pm, rstd, norm_g, dims, after=()):
    h, g_n, r, p, n, t = dims
    l, conv = xbc.shape
    rp, hp = r * p, h * p
    nc, row_spec, dtr_spec, par_spec, dskx_spec, st_spec = _ssd_specs(l, g_n, r, p, n, t, conv, True)
    consts = _ssd_consts(r, p, t)

    def fold_rows(v, rows):
        return sum(v[k * rows:(k + 1) * rows, :] for k in range(r))

    def body(x_ref, dtr_ref, dtb_ref, alog_ref, dskx_ref, st_ref, dyn_ref, y_ref, z_ref, rn_ref, ng_ref,
             e_ref, et_ref, tri_ref, trit_ref, ept_ref, it_ref, lm_ref, bmk_ref, bmkb_ref,
             dx_ref, ddt_ref, dbias_ref, dalog_ref, dd_ref, dz_ref, dng_ref, ds_ref, dy_ref):
        @pl.when(pl.program_id(0) == 0)
        def _():
            ds_ref[...] = jnp.zeros_like(ds_ref)
            dbias_ref[...] = jnp.zeros_like(dbias_ref)
            dalog_ref[...] = jnp.zeros_like(dalog_ref)
            dd_ref[...] = jnp.zeros_like(dd_ref)
            dng_ref[...] = jnp.zeros_like(dng_ref)

        yv, zv, rr, dn = y_ref[...], z_ref[...], rn_ref[...], dyn_ref[...]
        sz = _silu(zv)
        vhat = yv * sz * rr
        dvh = dn * ng_ref[...]
        dv = rr * (dvh - vhat * jnp.mean(dvh * vhat, axis=-1, keepdims=True))
        dy_ref[...] = dv * sz
        dz_ref[...] = (dv * yv * _dsilu(zv)).astype(BF)
        dng_ref[...] += jnp.sum(dn * vhat, axis=0, keepdims=True)

        ept, itile, bmask = ept_ref[...], it_ref[...], bmk_ref[...]
        last_row = lax.broadcasted_iota(jnp.int32, (t, HEAD_LANES), 0) == t - 1
        for g in range(g_n):
            xs = x_ref[:, g * rp:(g + 1) * rp]
            bm = x_ref[:, hp + g * n:hp + (g + 1) * n]
            cm = x_ref[:, hp + (g_n + g) * n:hp + (g_n + g + 1) * n]
            dy = dy_ref[:, g * rp:(g + 1) * rp]
            (pre, dt, a, last, dtx, ecsx, wx, lx, xdt, xblk, btile) = _ssd_common(
                xs, bm, dtr_ref[g], dtb_ref[g], alog_ref[g], e_ref[...], tri_ref[...], itile, lm_ref[...],
                bmkb_ref[...], r, t)
            s_in = st_ref[g]
            ds_out = ds_ref[g]
            elast_row = jnp.exp(last)
            elast = jnp.sum(ept * elast_row, axis=1, keepdims=True)

            cbx = _bdot(cm, btile, NT)
            amat = cbx * lx
            da = _bdot(dy, xblk, NT)
            dxdt = fold_rows(_bdot(amat, dy, TN) * bmask, t)
            dcbx = da * lx
            q = da * amat
            dc = _bdot(dcbx, btile)
            db = fold_rows(_bdot(dcbx, cm, TN), t)

            g0 = _bdot(cm, s_in, NT)
            dg0 = dy * ecsx
            dc = dc + _bdot(dg0, s_in)

            z = _bdot(bm, ds_out, NT)
            dxdt = dxdt + z * wx
            db = db + _bdot(xdt * wx, ds_out)
            ds_ref[g] = elast * ds_out + _bdot(dg0, cm, TN)

            dwx = z * xdt * wx
            rows = jnp.concatenate([jnp.sum(dy * xs, axis=0, keepdims=True), jnp.sum(dwx, axis=0, keepdims=True),
                                    jnp.zeros((14, rp), F32)], axis=0)
            f_cs, f_dt, f_rows = _rows_times(
                [q - itile * jnp.sum(q, axis=0, keepdims=True) + dy * g0 * ecsx - dwx, dxdt * xs, rows],
                et_ref[...], FOLD_PIECES)
            dd_ref[g] += f_rows[0:1]
            dlast = jnp.sum(jnp.sum(ds_out * s_in, axis=1, keepdims=True) * ept, axis=0, keepdims=True) * elast_row
            dlast = dlast + f_rows[1:2]

            dcs = f_cs + jnp.where(last_row, dlast, 0.0)
            dadt = _times_rows(trit_ref[...], dcs, EXPAND_PIECES)
            dalog_ref[g] += jnp.sum(dadt * dt, axis=0, keepdims=True) * a
            dpre = (dadt * a + f_dt) * (1.0 / (1.0 + jnp.exp(-pre)))
            ddt_ref[g] = dpre
            dbias_ref[g] += jnp.sum(dpre, axis=0, keepdims=True)
            dx_ref[:, g * rp:(g + 1) * rp] = dskx_ref[g] * dy + dxdt * dtx
            dx_ref[:, hp + g * n:hp + (g + 1) * n] = db
            dx_ref[:, hp + (g_n + g) * n:hp + (g_n + g + 1) * n] = dc

    par_shape = jax.ShapeDtypeStruct((g_n, 1, HEAD_LANES), F32)
    vec = pl.BlockSpec((1, hp), lambda c: (0, 0))
    return _hosted_call(
        "ssd_bwd", body, (nc,),
        [row_spec(conv), dtr_spec, par_spec, par_spec, dskx_spec, st_spec, row_spec(hp), row_spec(hp), row_spec(hp),
         row_spec(1), vec, *_const_specs(consts)],
        [row_spec(conv), dtr_spec, par_spec, par_spec, par_spec, row_spec(hp), vec],
        [jax.ShapeDtypeStruct((l, conv), F32), jax.ShapeDtypeStruct((g_n, l, HEAD_LANES), F32),
         par_shape, par_shape, par_shape, jax.ShapeDtypeStruct((l, pm.shape[1]), BF),
         jax.ShapeDtypeStruct((1, hp), F32)],
        [pltpu.VMEM((g_n, rp, n), F32), pltpu.VMEM((t, hp), F32)], ("arbitrary",),
        (xbc, dtr, dtb, alog, dskx, states, dyn, y, pm, rstd, norm_g, *consts), after=after)


POOL_HALO = 16


def _pool_mix(name, src, dp, backward, out_dtype, into=None, cw=256, tr=512):
    l = src.shape[0]
    gd = dp // len(POOL_WINDOWS)
    cwl, trl = _t(gd, cw), _t(l, tr)
    nt = l // trl

    def body(x_ref, *rest):
        o_ref = rest[-1]
        gi = pl.program_id(0)
        for wi, win in enumerate(POOL_WINDOWS):
            @pl.when(gi == wi)
            def _(win=win):
                def step(i, carry):
                    r0 = pl.multiple_of(i * trl, trl)
                    cur = x_ref[pl.ds(r0, trl), :]
                    trow = r0 + lax.broadcasted_iota(jnp.int32, (trl, 1), 0)
                    cnt = jnp.minimum(trow + 1, win).astype(F32)
                    if not backward:
                        halo = x_ref[pl.ds(pl.multiple_of(jnp.maximum(r0 - POOL_HALO, 0), POOL_HALO), POOL_HALO), :]
                        halo = jnp.where(i > 0, halo, 0.0)
                        s = jnp.concatenate([halo, cur], axis=0)
                        sh = 1
                        while sh < win:
                            s = s + pltpu.roll(s, sh, 0)
                            sh *= 2
                        res = s[POOL_HALO:] / cnt - cur
                    else:
                        halo = x_ref[pl.ds(pl.multiple_of(jnp.minimum(r0 + trl, l - POOL_HALO), POOL_HALO),
                                           POOL_HALO), :]
                        hrow = r0 + trl + lax.broadcasted_iota(jnp.int32, (POOL_HALO, 1), 0)
                        hcnt = jnp.minimum(hrow + 1, win).astype(F32)
                        halo = jnp.where(i < nt - 1, halo / hcnt, 0.0)
                        s = jnp.concatenate([cur / cnt, halo], axis=0)
                        sh = 1
                        while sh < win:
                            s = s + pltpu.roll(s, trl + POOL_HALO - sh, 0)
                            sh *= 2
                        res = s[:trl] - cur
                    o_ref[pl.ds(r0, trl), :] = res.astype(o_ref.dtype)
                    return carry
                lax.fori_loop(0, nt, step, 0)

    col = pl.BlockSpec((l, cwl), lambda g, j: (0, g * (gd // cwl) + j))
    if into is None:
        return pl.pallas_call(
            body, grid=(len(POOL_WINDOWS), gd // cwl), in_specs=[col], out_specs=col,
            out_shape=jax.ShapeDtypeStruct((l, dp), out_dtype),
            compiler_params=_params(("parallel", "parallel")), name=name)(src)
    return pl.pallas_call(
        body, grid=(len(POOL_WINDOWS), gd // cwl), in_specs=[col, pl.BlockSpec(memory_space=pl.ANY)], out_specs=col,
        out_shape=jax.ShapeDtypeStruct(into.shape, into.dtype), input_output_aliases={1: 0},
        compiler_params=_params(("parallel", "parallel")), name=name)(src, into)


def _pool_out_bwd_data(dx, wp_out, mg, pu, scale, tm=1024, tn=512):
    l, d = dx.shape
    dp = wp_out.shape[0]
    tm, tn = _t(l, tm), _t(dp, tn)

    def epi(dyp, m, gt, sc):
        sg = _silu(gt)
        return dyp * sc * sg, dyp * m * sc * _dsilu(gt), jnp.sum(dyp * m * sg, axis=0, keepdims=True)

    tile = pl.BlockSpec((tm, tn), lambda j, i, k: (i, j))
    right = pl.BlockSpec((tm, tn), lambda j, i, k: (i, dp // tn + j))
    vec = pl.BlockSpec((1, tn), lambda j, i, k: (0, j))
    return _mm("pool_out_bwd_data", (dp // tn, l // tm, 1),
               dx, pl.BlockSpec((tm, d), lambda j, i, k: (i, 0)),
               wp_out, pl.BlockSpec((tn, d), lambda j, i, k: (j, 0)), NT,
               [jax.ShapeDtypeStruct((l, dp), BF), jax.ShapeDtypeStruct((l, 2 * dp), BF),
                jax.ShapeDtypeStruct((1, dp), F32)], [tile, right, vec], (tm, tn),
               (mg, pu, scale), (tile, right, vec), epi, summed=(2, 1))


def _group_mm_fwd(mp, wg, pu, scale, tm=1024, tn=1024, tk=1024):
    l, dp = mp.shape
    ng, gd = wg.shape[0], wg.shape[1]
    tm, tn, tk = _t(l, tm), _t(gd, tn), _t(gd, tk)

    def epi(acc, gate, sc):
        return acc, acc * sc * _silu(gate)

    out = pl.BlockSpec((tm, tn), lambda g, i, j, k: (i, g * (gd // tn) + j))
    return _mm("group_mm_fwd", (ng, l // tm, gd // tn, gd // tk),
               mp, pl.BlockSpec((tm, tk), lambda g, i, j, k: (i, g * (gd // tk) + k)),
               wg, pl.BlockSpec((None, tk, tn), lambda g, i, j, k: (g, k, j)), NN,
               [jax.ShapeDtypeStruct((l, dp), F32), jax.ShapeDtypeStruct((l, dp), BF)], [out, out], (tm, tn),
               (pu, scale),
               (pl.BlockSpec((tm, tn), lambda g, i, j, k: (i, (dp + g * gd) // tn + j)),
                pl.BlockSpec((1, tn), lambda g, i, j, k: (0, g * (gd // tn) + j))), epi)


def _group_mm_bwd_data(dmg, wg, after=(), tm=1024, tn=1024, tk=1024):
    l, dp = dmg.shape
    ng, gd = wg.shape[0], wg.shape[1]
    tm, tn, tk = _t(l, tm), _t(gd, tn), _t(gd, tk)
    return _mm("group_mm_bwd_data", (ng, l // tm, gd // tn, gd // tk),
               dmg, pl.BlockSpec((tm, tk), lambda g, i, j, k: (i, g * (gd // tk) + k)),
               wg, pl.BlockSpec((None, tn, tk), lambda g, i, j, k: (g, j, k)), NT,
               [jax.ShapeDtypeStruct((l, dp), F32)],
               [pl.BlockSpec((tm, tn), lambda g, i, j, k: (i, g * (gd // tn) + j))], (tm, tn), after=after)[0]


def _group_mm_bwd_weight(mp, dmg, ng, tm=512, tk=1024):
    l, dp = mp.shape
    gd = dp // ng
    tm, tk = _t(gd, tm), _t(l, tk)
    return _mm("group_mm_bwd_weight", (ng, gd // tm, l // tk),
               mp, pl.BlockSpec((tk, tm), lambda g, i, k: (k, g * (gd // tm) + i)),
               dmg, pl.BlockSpec((tk, gd), lambda g, i, k: (k, g)), TN,
               [jax.ShapeDtypeStruct((ng, gd, gd // 2), F32)],
               [pl.BlockSpec((None, tm, gd // 2), lambda g, i, k: (g, i, 0))], (tm, gd), epi=_pack_epi)[0]


def _cast_bf16(name, w, tr=256):
    r, c = w.shape
    tr = _row_tile(r, tr, 16)

    def body(w_ref, o_ref):
        o_ref[...] = w_ref[...].astype(BF)

    blk = pl.BlockSpec((tr, c), lambda i: (i, 0))
    return pl.pallas_call(body, grid=(r // tr,), in_specs=[blk], out_specs=blk,
                          out_shape=jax.ShapeDtypeStruct((r, c), BF), compiler_params=_params(("parallel",)),
                          name=name)(w)


def _pack_rows(name, w, tr=256):
    r, c = w.shape
    tr = _row_tile(r, tr)

    def body(w_ref, o_ref):
        o_ref[...] = _pack_pairs(w_ref[...])

    return pl.pallas_call(body, grid=(r // tr, 2), in_specs=[pl.BlockSpec((tr, c // 2), lambda i, q: (i, q))],
                          out_specs=pl.BlockSpec((None, tr, c // 4), lambda i, q: (q, i, 0)),
                          out_shape=jax.ShapeDtypeStruct((2, r, c // 4), F32),
                          compiler_params=_params(("parallel", "parallel")), name=name)(w)


def _unpack_rows(name, w0, w1, tr=512):
    r, h = w0.shape
    tr = _row_tile(r, tr, 16)

    def body(w0_ref, w1_ref, o_ref):
        for q, w_ref in enumerate((w0_ref, w1_ref)):
            hi, lo = _unpack_pairs(w_ref[...])
            o_ref[q, :, :h] = hi.astype(BF)
            o_ref[q, :, h:] = lo.astype(BF)

    words = pl.BlockSpec((tr, h), lambda i: (i, 0))
    return pl.pallas_call(body, grid=(r // tr,), in_specs=[words, words],
                          out_specs=pl.BlockSpec((2, tr, 2 * h), lambda i: (0, i, 0)),
                          out_shape=jax.ShapeDtypeStruct((2, r, 2 * h), BF), compiler_params=_params(("parallel",)),
                          name=name)(w0, w1)


def _reduce_packed(name, recv, tr=256):
    nd, r, h = recv.shape
    tr = _row_tile(r, tr)

    def body(p_ref, o_ref):
        hi, lo = _unpack_pairs(p_ref[0])
        for k in range(1, nd):
            a, b = _unpack_pairs(p_ref[k])
            hi, lo = hi + a, lo + b
        o_ref[:, :h] = hi
        o_ref[:, h:] = lo

    return pl.pallas_call(body, grid=(r // tr,), in_specs=[pl.BlockSpec((nd, tr, h), lambda i: (0, i, 0))],
                          out_specs=pl.BlockSpec((tr, 2 * h), lambda i: (i, 0)),
                          out_shape=jax.ShapeDtypeStruct((r, 2 * h), F32), compiler_params=_params(("parallel",)),
                          name=name)(recv)


def _adamw_math(w, g, m, v):
    m2 = ADAM_B1 * m + (1.0 - ADAM_B1) * g
    v2 = ADAM_B2 * v + (1.0 - ADAM_B2) * (g * g)
    m_hat = m2 / (1.0 - ADAM_B1 ** ADAM_STEP)
    v_hat = v2 / (1.0 - ADAM_B2 ** ADAM_STEP)
    delta = -ADAM_LR * (m_hat / (jnp.sqrt(v_hat) + ADAM_EPS) + ADAM_WD * w)
    return delta, m2, v2


def _adamw(name, w, g, m, v, tr=256):
    r, c = w.shape
    tr = _row_tile(r, tr)

    def body(w_ref, g_ref, m_ref, v_ref, d_ref, m2_ref, v2_ref):
        d, m2, v2 = _adamw_math(w_ref[...], g_ref[...], m_ref[...], v_ref[...])
        d_ref[...] = d
        m2_ref[...] = m2
        v2_ref[...] = v2

    blk = pl.BlockSpec((tr, c), lambda i: (i, 0))
    shp = jax.ShapeDtypeStruct((r, c), F32)
    return pl.pallas_call(body, grid=(r // tr,), in_specs=[blk] * 4, out_specs=[blk] * 3, out_shape=[shp] * 3,
                          compiler_params=_params(("parallel",)), name=name)(w, g, m, v)


def _sum_slots(name, a):
    nd, r, c = a.shape

    def body(a_ref, o_ref):
        s = a_ref[0]
        for k in range(1, nd):
            s = s + a_ref[k]
        o_ref[...] = s

    return pl.pallas_call(body, out_shape=jax.ShapeDtypeStruct((r, c), F32), name=name)(a)


def _head_rows(v, g_n, r):
    return jnp.pad(v.reshape(g_n, 1, r), ((0, 0), (0, 0), (0, HEAD_LANES - r)))


class _Later:
    def __init__(self, hosted, first_of, weights_of, pool_w_out_of, send):
        self.hosted, self.first_of, self.weights_of = hosted, first_of, weights_of
        self.pool_w_out_of, self.send = pool_w_out_of, send


def _listed(res):
    return res if isinstance(res, (list, tuple)) else [res]


def _local_step(x, target, ln_g, final_g, conv_b, dt_bias, a_log, d_skip, norm_g, later):
    l, d = x.shape
    di = norm_g.shape[1]
    h = dt_bias.shape[1]
    p = di // h
    conv = conv_b.shape[1]
    n, t = SSM_STATE, SSD_CHUNK
    g_n = (conv - di) // (2 * n)
    r = h // g_n
    nm = di + conv
    ng = len(POOL_WINDOWS)
    dims = (h, g_n, r, p, n, t)
    pw = d

    hosted = later.hosted
    h0, r0, *arrived_0 = _rms_fwd("rms0_fwd", x, ln_g[0:1], exch=hosted.get("rms0"))
    wt_in, conv_w, scale = later.first_of(arrived_0)
    dp = scale.shape[1]
    pm, *arrived_i = _listed(_mm_nt_halves("in_proj_main", h0, wt_in, 0, nm, exch=hosted.get("in_proj")))
    dtr = _mm_nt_halves("in_proj_dt", h0, wt_in, nm, h, tn=h)
    xbc = _conv_fwd(pm, di, conv_w, conv_b)
    dtb, alog = _head_rows(dt_bias, g_n, r), _head_rows(a_log, g_n, r)
    dskx = jnp.repeat(d_skip.reshape(g_n, 1, r), p, axis=2)
    dtr_g = jnp.pad(jnp.transpose(dtr.reshape(l, g_n, r), (1, 0, 2)), ((0, 0), (0, 0), (0, HEAD_LANES - r)))
    y, states, yn, r_n, *arrived_s = _ssd_fwd(xbc, dtr_g, dtb, alog, dskx, pm, norm_g, dims, exch=hosted.get("ssd"))
    w_out, wg, wp_in = later.weights_of(arrived_i, arrived_s)
    x1 = _mm_nn("ssm_out_proj", yn, w_out, add=x)

    h1, r1 = _rms_fwd("rms1_fwd", x1, ln_g[1:2])
    pu, *arrived_p = _listed(_mm_nn_blocked("pool_in_proj", h1, wp_in, exch=hosted.get("pool_in_proj")))
    wp_out = later.pool_w_out_of(arrived_p)
    mp = _pool_mix("pool_mix_fwd", pu, dp, False, BF)
    mg, yp = _group_mm_fwd(mp, wg, pu, scale)
    x2 = _mm_nn("pool_out_proj", yp, wp_out, add=x1)

    dx2, dx2_b, d_final_g, loss = _loss_head(x2, final_g, target)

    gw_pout = _mm_tn_packed("pool_out_bwd_weight", yp, dx2_b, pw)
    going = later.send("pool_w_out", gw_pout)
    dmg, dpu, d_scale = _pool_out_bwd_data(dx2_b, wp_out, mg, pu, scale)
    dmp = _group_mm_bwd_data(dmg, wg, after=going)
    gw_g = _group_mm_bwd_weight(mp, dmg, ng)
    going = later.send("pool_w_group", gw_g)
    dpu = _pool_mix("pool_mix_bwd", dmp, dp, True, BF, into=dpu)
    dh1 = _mm_nt_blocked("pool_in_bwd_data", dpu, wp_in, after=going)
    gw_pin = _mm_tn_packed("pool_in_bwd_weight", h1, dpu, 2 * dp // N_DEV)
    going = later.send("pool_w_in", gw_pin)
    dx1, dx1_b, d_ln1 = _rms_bwd("rms1_bwd", dh1, x1, r1, ln_g[1:2], dx2)

    dyn = _mm_nt("ssm_out_bwd_data", dx1_b, w_out, 0, di, tn=1024, after=going)
    gw_out = _mm_tn_packed("ssm_out_bwd_weight", yn, dx1_b, pw)
    going = later.send("ssm_w_out", gw_out)
    dxbc, ddt_g, dbias_g, dalog_g, dd_g, dproj, d_norm_g = _ssd_bwd(
        xbc, dtr_g, dtb, alog, dskx, states, dyn, y, pm, r_n, norm_g, dims, after=going)
    dproj, d_conv_w, d_conv_b = _conv_bwd(pm, di, conv_w, conv_b, dxbc, dproj)
    ddt = jnp.transpose(ddt_g[:, :, :r], (1, 0, 2)).reshape(l, h)
    gwt_in, going = [], ()
    for q in range(2):
        gq = _mm_tn_packed(f"in_proj_bwd_weight_{q}", dproj, h0, d // 2, rows=nm + h, b_blk=q, after=going)
        gq = _mm_tn_packed(f"in_proj_bwd_weight_dt_{q}", ddt, h0, d // 2, into=gq, b_blk=q)
        going = later.send(f"ssm_w_in_{q}", gq)
        gwt_in.append(gq)
    gwt_in = jnp.concatenate(gwt_in, axis=0)
    dh0 = _mm_nn_halves("in_proj_bwd_data_dt", ddt, wt_in, nm, after=going)
    dh0 = _mm_nn_halves("in_proj_bwd_data", dproj, wt_in, 0, add=dh0)
    grad_x, _, d_ln0 = _rms_bwd("rms0_bwd", dh0, x, r0, ln_g[0:1], dx1)

    def heads(v):
        return v[:, 0, :r].reshape(1, h)

    small = dict(ln_g=jnp.concatenate([d_ln0, d_ln1], axis=0), final_g=d_final_g, conv_w=d_conv_w, conv_b=d_conv_b,
                 dt_bias=heads(dbias_g), a_log=heads(dalog_g), d_skip=heads(dd_g), norm_g=d_norm_g, scale=d_scale)
    big = dict(ssm_w_in=gwt_in, ssm_w_out=gw_out, pool_w_in=gw_pin, pool_w_group=gw_g, pool_w_out=gw_pout)
    return loss, grad_x, small, big


SMALL_ORDER = ("ln_g", "final_g", "conv_w", "conv_b", "dt_bias", "a_log", "d_skip", "norm_g", "scale", "loss")


def _flatten_small(parts):
    flat = jnp.concatenate([parts[k].reshape(-1) for k in SMALL_ORDER])
    n = flat.shape[0]
    rows = -(-n // 1024) * 8
    return jnp.pad(flat, (0, rows * 128 - n)).reshape(rows, 128)


def _split_small(flat, shapes):
    flat = flat.reshape(-1)
    out, off = {}, 0
    for k in SMALL_ORDER:
        size = int(np.prod(shapes[k]))
        out[k] = flat[off:off + size].reshape(shapes[k])
        off += size
    return out


def kernel(x, ln_g, final_g, ssm_w_in, ssm_conv_w, ssm_conv_b, ssm_dt_bias, ssm_a_log, ssm_d, ssm_norm_g, ssm_w_out, pool_w_in, pool_w_group, pool_scale, pool_w_out, loss_target, m_ln_g, m_final_g, m_ssm_w_in, m_ssm_conv_w, m_ssm_conv_b, m_ssm_dt_bias, m_ssm_a_log, m_ssm_d, m_ssm_norm_g, m_ssm_w_out, m_pool_w_in, m_pool_w_group, m_pool_scale, m_pool_w_out, v_ln_g, v_final_g, v_ssm_w_in, v_ssm_conv_w, v_ssm_conv_b, v_ssm_dt_bias, v_ssm_a_log, v_ssm_d, v_ssm_norm_g, v_ssm_w_out, v_pool_w_in, v_pool_w_group, v_pool_scale, v_pool_w_out):
    l, d = x.shape[1], x.shape[2]
    me = 4 * lax.axis_index("x") + 2 * lax.axis_index("y") + lax.axis_index("c")
    ng, gds, gd = pool_w_group.shape[1], pool_w_group.shape[2], pool_w_group.shape[3]
    sin_s = ssm_w_in.shape[2]
    dp_s = pool_w_out.shape[1]
    conv_s = ssm_conv_w.shape[2]

    wt_in_s = _pack_rows("pack_w_in", jnp.transpose(ssm_w_in[0]))
    w_out_s = _cast_bf16("cast_w_out", ssm_w_out[0])
    wp_in_s = _cast_bf16("cast_pool_w_in", pool_w_in[0])
    wg_s = _cast_bf16("cast_pool_w_group", pool_w_group[0].reshape(ng * gds, gd))
    wp_out_s = _cast_bf16("cast_pool_w_out", pool_w_out[0])
    small_s = jnp.concatenate([ssm_conv_w[0].reshape(-1), pool_scale[0]])
    n_small = small_s.shape[0]
    small_s = jnp.pad(small_s, (0, -(-n_small // 1024) * 1024 - n_small)).reshape(-1, 128)
    hosted = dict(rms0=_Gather([wt_in_s[0], wt_in_s[1], small_s]), in_proj=_Gather([w_out_s, wg_s]),
                  ssd=_Gather([wp_in_s]), pool_in_proj=_Gather([wp_out_s]))

    def first_of(arrived_0):
        wt_in_g0, wt_in_g1, small_g = arrived_0
        wt_in = _unpack_rows("unpack_w_in", wt_in_g0.reshape(N_DEV * sin_s, d // 4),
                             wt_in_g1.reshape(N_DEV * sin_s, d // 4))
        small_all = small_g.reshape(N_DEV, -1)[:, :n_small]
        conv_w = jnp.transpose(small_all[:, :CONV_TAPS * conv_s].reshape(N_DEV, CONV_TAPS, conv_s), (1, 0, 2))
        return wt_in, conv_w.reshape(CONV_TAPS, -1), small_all[:, CONV_TAPS * conv_s:].reshape(1, -1)

    def weights_of(arrived_i, arrived_s):
        wg = jnp.transpose(arrived_i[1].reshape(N_DEV, ng, gds, gd), (1, 0, 2, 3)).reshape(ng, gd, gd)
        return arrived_i[0].reshape(-1, d), wg, arrived_s[0]

    def rows_major(gp):
        q, _, hw = gp.shape
        return jnp.transpose(gp.reshape(q, N_DEV, -1, hw), (1, 0, 2, 3))

    to_blocks = dict(ssm_w_in_0=rows_major, ssm_w_in_1=rows_major, ssm_w_out=rows_major, pool_w_out=rows_major,
                     pool_w_in=lambda gp: gp[:, None],
                     pool_w_group=lambda gp: jnp.transpose(gp.reshape(ng, N_DEV, gds, gd // 2), (1, 0, 2, 3)))
    travelling = {}

    def send(name, gp):
        *travelling[name], token = _scatter_start(f"scatter_{name}_start", to_blocks[name](gp))
        return (token,)

    later = _Later(hosted, first_of, weights_of, lambda arrived_p: arrived_p[0].reshape(-1, d), send)

    loss, grad_x, small, _ = _local_step(
        x[0], loss_target[0], ln_g, final_g.reshape(1, d), ssm_conv_b, ssm_dt_bias, ssm_a_log, ssm_d, ssm_norm_g, later)

    def arrived(name, after):
        src, land = _scatter_wait(f"scatter_{name}_wait", *travelling[name], after=after)
        own = lax.dynamic_slice_in_dim(src, me, 1, axis=0)
        return lax.dynamic_update_slice_in_dim(land, own, me, axis=0)

    def reduced(name, rv):
        q = rv.shape[1]
        cols = [_reduce_packed(f"{name}_{j}", rv[:, j]) for j in range(q)]
        return cols[0] if q == 1 else jnp.concatenate(cols, axis=1)

    grads = {}
    grads["pool_w_out"] = reduced("reduce_pool_w_out", arrived("pool_w_out", (grad_x,)))[None]
    grads["pool_w_group"] = _reduce_packed("reduce_pool_w_group", arrived("pool_w_group", (grad_x,)).reshape(
        N_DEV, ng * gds, gd // 2)).reshape(1, ng, gds, gd)
    grads["pool_w_in"] = reduced("reduce_pool_w_in", arrived("pool_w_in", (grad_x,)))[None]
    grads["ssm_w_out"] = reduced("reduce_w_out", arrived("ssm_w_out", (grad_x,)))[None]

    weights = dict(ln_g=ln_g, final_g=final_g, ssm_w_in=ssm_w_in, ssm_conv_w=ssm_conv_w, ssm_conv_b=ssm_conv_b,
                   ssm_dt_bias=ssm_dt_bias, ssm_a_log=ssm_a_log, ssm_d=ssm_d, ssm_norm_g=ssm_norm_g,
                   ssm_w_out=ssm_w_out, pool_w_in=pool_w_in, pool_w_group=pool_w_group, pool_scale=pool_scale,
                   pool_w_out=pool_w_out)
    m_in = dict(ln_g=m_ln_g, final_g=m_final_g, ssm_w_in=m_ssm_w_in, ssm_conv_w=m_ssm_conv_w, ssm_conv_b=m_ssm_conv_b,
                ssm_dt_bias=m_ssm_dt_bias, ssm_a_log=m_ssm_a_log, ssm_d=m_ssm_d, ssm_norm_g=m_ssm_norm_g,
                ssm_w_out=m_ssm_w_out, pool_w_in=m_pool_w_in, pool_w_group=m_pool_w_group, pool_scale=m_pool_scale,
                pool_w_out=m_pool_w_out)
    v_in = dict(ln_g=v_ln_g, final_g=v_final_g, ssm_w_in=v_ssm_w_in, ssm_conv_w=v_ssm_conv_w, ssm_conv_b=v_ssm_conv_b,
                ssm_dt_bias=v_ssm_dt_bias, ssm_a_log=v_ssm_a_log, ssm_d=v_ssm_d, ssm_norm_g=v_ssm_norm_g,
                ssm_w_out=v_ssm_w_out, pool_w_in=v_pool_w_in, pool_w_group=v_pool_w_group, pool_scale=v_pool_scale,
                pool_w_out=v_pool_w_out)
    names = list(weights)
    big_names = ("ssm_w_out", "pool_w_in", "pool_w_group", "pool_w_out", "ssm_w_in")
    delta, new_m, new_v = {}, {}, {}

    def adamw_big(k):
        shp = weights[k].shape
        two_d = (-1, shp[-1])
        dk, mk, vk = _adamw(f"adamw_{k}", weights[k].reshape(two_d), grads[k].reshape(two_d), m_in[k].reshape(two_d),
                            v_in[k].reshape(two_d))
        delta[k], new_m[k], new_v[k] = dk.reshape(shp), mk.reshape(shp), vk.reshape(shp)

    for k in big_names[:-1]:
        adamw_big(k)
    small_names = [k for k in names if k not in big_names]

    small["loss"] = loss
    shapes = {k: small[k].shape for k in SMALL_ORDER}
    gathered_small, = _exchange_alone("all_gather_small_grads", _Gather([_flatten_small(small)]),
                                      after=tuple(delta[k] for k in big_names[:-1]))
    summed = _split_small(_sum_slots("sum_small_grads", gathered_small), shapes)
    grads.update(ln_g=summed["ln_g"], final_g=summed["final_g"].reshape(d), ssm_conv_b=summed["conv_b"],
                 ssm_conv_w=lax.dynamic_slice_in_dim(summed["conv_w"], me * conv_s, conv_s, axis=1)[None],
                 ssm_dt_bias=summed["dt_bias"], ssm_a_log=summed["a_log"], ssm_d=summed["d_skip"],
                 ssm_norm_g=summed["norm_g"],
                 pool_scale=lax.dynamic_slice_in_dim(summed["scale"], me * dp_s, dp_s, axis=1))

    def packed(tree):
        flat = jnp.concatenate([tree[k].reshape(-1) for k in small_names])
        nn = flat.shape[0]
        return jnp.pad(flat, (0, -(-nn // 1024) * 1024 - nn), constant_values=1.0).reshape(-1, 128)

    ds, ms, vs = _adamw("adamw_small", packed(weights), packed(grads), packed(m_in), packed(v_in))
    off = 0
    for k in small_names:
        shp = weights[k].shape
        size = int(np.prod(shp))
        for res, arr in ((delta, ds), (new_m, ms), (new_v, vs)):
            res[k] = arr.reshape(-1)[off:off + size].reshape(shp)
        off += size

    k = "ssm_w_in"
    gt = jnp.concatenate([reduced(f"reduce_w_in_{q}", arrived(f"ssm_w_in_{q}", (ds,))) for q in range(2)], axis=1)
    dk, mk, vk = _adamw("adamw_ssm_w_in", jnp.transpose(weights[k][0]), gt, jnp.transpose(m_in[k][0]),
                        jnp.transpose(v_in[k][0]))
    grads[k], delta[k], new_m[k], new_v[k] = (jnp.transpose(t)[None] for t in (gt, dk, mk, vk))

    return (summed["loss"].reshape(()), grad_x[None], *[grads[k] for k in names], *[delta[k] for k in names],
            *[new_m[k] for k in names], *[new_v[k] for k in names])
```

```python
import math

import jax
import jax.numpy as jnp
import numpy as np
from jax import lax
from jax.experimental import pallas as pl
from jax.experimental.pallas import tpu as pltpu

F32 = jnp.float32
BF = jnp.bfloat16
U32 = jnp.uint32

N_DEV = 8
EPS = 1e-6
SSD_CHUNK = 64
SSM_STATE = 128
CONV_TAPS = 4
HEAD_LANES = 128
POOL_WINDOWS = (2, 4, 8, 16)
ADAM_LR, ADAM_B1, ADAM_B2, ADAM_EPS, ADAM_WD, ADAM_STEP = 0.001, 0.9, 0.999, 1e-08, 0.01, 10
VMEM_LIMIT = 56 * 1024 * 1024
NEG_BIG = -1e30
HAND_ON_AT = 6

NN = ((1,), (0,))
NT = ((1,), (1,))
TN = ((0,), (0,))
MESH = pl.DeviceIdType.MESH


def _t(dim, pref):
    return pref if dim % pref == 0 else dim


def _row_tile(rows, pref, mult=8):
    best = rows
    for cand in range(mult, min(rows, pref) + 1, mult):
        if rows % cand == 0:
            best = cand
    return best


def _params(sem=None):
    return pltpu.CompilerParams(dimension_semantics=sem, vmem_limit_bytes=VMEM_LIMIT)


def _silu(x):
    return x * (1.0 / (1.0 + jnp.exp(-x)))


def _dsilu(x):
    s = 1.0 / (1.0 + jnp.exp(-x))
    return s * (1.0 + x * (1.0 - s))


def _bdot(a, b, dims=NN):
    return lax.dot_general(a.astype(BF), b.astype(BF), (dims, ((), ())), preferred_element_type=F32)


def _pack_pairs(x):
    h = x.shape[1] // 2
    hi = lax.bitcast_convert_type(x[:, :h].astype(jnp.bfloat16).astype(F32), U32)
    lo = lax.bitcast_convert_type(x[:, h:].astype(jnp.bfloat16).astype(F32), U32)
    return lax.bitcast_convert_type(hi | (lo >> 16), F32)


def _unpack_pairs(w):
    u = lax.bitcast_convert_type(w, U32)
    hi = lax.bitcast_convert_type(u & jnp.uint32(0xFFFF0000), F32)
    lo = lax.bitcast_convert_type(u << 16, F32)
    return hi, lo


def _mesh_pos():
    return lax.axis_index("x"), lax.axis_index("y"), lax.axis_index("c")


def _slot(pos):
    return 4 * pos[0] + 2 * pos[1] + pos[2]


class _Gather:
    def __init__(self, arrays):
        self.arrays = list(arrays)
        self.out_shapes = [jax.ShapeDtypeStruct((N_DEV, *s.shape), s.dtype) for s in arrays]

    def phases(self, src, dst, send_sems, recv_sems, local_sems):
        n_arr = len(self.arrays)
        x, y, c = _mesh_pos()
        me, sibling = (x, y, c), (x, y, 1 - c)
        chips = [(1 - x, y), (x, 1 - y), (1 - x, 1 - y)]

        def copy(a, k, block, to, from_src):
            return pltpu.make_async_remote_copy(
                src_ref=src[a] if from_src else dst[a].at[_slot(block)], dst_ref=dst[a].at[_slot(block)],
                send_sem=send_sems.at[a * 7 + k], recv_sem=recv_sems.at[a * 7 + k], device_id=to, device_id_type=MESH)

        def mine(a):
            return pltpu.make_async_copy(src[a], dst[a].at[_slot(me)], local_sems.at[a])

        def first(a):
            return [copy(a, 0, me, sibling, True)] + [copy(a, 1 + j, me, (*chip, c), True)
                                                     for j, chip in enumerate(chips)]

        def start():
            for a in range(n_arr):
                mine(a).start()
                for cp in first(a):
                    cp.start()

        def middle():
            for j, chip in enumerate(chips):
                for a in range(n_arr):
                    copy(a, 1 + j, (*chip, c), me, False).wait_recv()
                    copy(a, 4 + j, (*chip, c), sibling, False).start()

        def finish():
            for a in range(n_arr):
                copy(a, 0, sibling, me, False).wait_recv()
                for j, chip in enumerate(chips):
                    copy(a, 4 + j, (*chip, 1 - c), me, False).wait_recv()
                for cp in first(a):
                    cp.wait_send()
                for j, chip in enumerate(chips):
                    copy(a, 4 + j, (*chip, c), sibling, False).wait_send()
                mine(a).wait()

        return start, middle, finish


def _hosted_call(name, body, grid, in_specs, out_specs, out_shape, scratch_shapes, sem, operands, exch=None,
                 after=(), into=None):
    if into is not None:
        n_lead = len(in_specs)
        inner = body

        def body(*refs):
            inner(*refs[:n_lead], *refs[n_lead + 1:])

        return pl.pallas_call(
            body, grid=grid, in_specs=[*in_specs, pl.BlockSpec(memory_space=pl.ANY)], out_specs=out_specs,
            out_shape=out_shape, scratch_shapes=scratch_shapes, input_output_aliases={n_lead: 0},
            compiler_params=_params(sem), name=name)(*operands, into)
    if after:
        n_lead = len(in_specs)
        inner = body

        def body(*refs):
            inner(*refs[:n_lead], *refs[n_lead + len(after):])

        in_specs = [*in_specs, *[pl.BlockSpec(memory_space=pl.ANY)] * len(after)]
        operands = (*operands, *after)
    if exch is None:
        return pl.pallas_call(body, grid=grid, in_specs=in_specs, out_specs=out_specs, out_shape=out_shape,
                              scratch_shapes=scratch_shapes, compiler_params=_params(sem), name=name)(*operands)
    n_in, n_out, n_scr, ne = len(in_specs), len(out_specs), len(scratch_shapes), len(exch.arrays)
    total = math.prod(grid)

    def wrapped(*refs):
        ins, ex_in = refs[:n_in], refs[n_in:n_in + ne]
        outs = refs[n_in + ne:n_in + ne + n_out]
        ex_out = refs[n_in + ne + n_out:n_in + 2 * ne + n_out]
        scr = refs[n_in + 2 * ne + n_out:n_in + 2 * ne + n_out + n_scr]
        step = 0
        for axis, size in enumerate(grid):
            step = step * size + pl.program_id(axis)
        start, middle, finish = exch.phases(ex_in, ex_out, *refs[-3:])
        pl.when(step == 0)(start)
        if middle is not None:
            pl.when(step == (total * HAND_ON_AT) // 8)(middle)
        body(*ins, *outs, *scr)
        pl.when(step == total - 1)(finish)

    hbm = pl.BlockSpec(memory_space=pl.ANY)
    sems = [pltpu.SemaphoreType.DMA((ne * 7,)), pltpu.SemaphoreType.DMA((ne * 7,)), pltpu.SemaphoreType.DMA((ne,))]
    return pl.pallas_call(
        wrapped, grid=grid, in_specs=[*in_specs, *[hbm] * ne], out_specs=[*out_specs, *[hbm] * ne],
        out_shape=[*out_shape, *exch.out_shapes], scratch_shapes=[*scratch_shapes, *sems],
        compiler_params=pltpu.CompilerParams(dimension_semantics=("arbitrary",) * len(grid),
                                             vmem_limit_bytes=VMEM_LIMIT, has_side_effects=True),
        name=name)(*operands, *exch.arrays)


def _flip_peers():
    x, y, c = _mesh_pos()
    peers = []
    for k in range(1, N_DEV):
        fx, fy, fc = (k >> 2) & 1, (k >> 1) & 1, k & 1
        peers.append((1 - x if fx else x, 1 - y if fy else y, 1 - c if fc else c))
    return (x, y, c), peers


def _split_scatter_copy(src, land, send_sems, recv_sems, k, me, peer, sending):
    return pltpu.make_async_remote_copy(
        src_ref=src.at[_slot(peer)], dst_ref=land.at[_slot(me) if sending else _slot(peer)],
        send_sem=send_sems.at[k], recv_sem=recv_sems.at[k], device_id=peer, device_id_type=MESH)


def _scatter_start(name, blocks):
    def body(src, land, send_sems, recv_sems, src_thru, land_thru, token):
        me, peers = _flip_peers()
        for k, peer in enumerate(peers):
            _split_scatter_copy(src, land, send_sems, recv_sems, k, me, peer, True).start()
        token[...] = jnp.zeros_like(token)

    hbm = pl.BlockSpec(memory_space=pltpu.HBM)
    sem = pl.BlockSpec(memory_space=pltpu.SEMAPHORE)
    return pl.pallas_call(
        body, name=name,
        out_shape=(pltpu.SemaphoreType.DMA((N_DEV - 1,)), pltpu.SemaphoreType.DMA((N_DEV - 1,)),
                   pltpu.HBM(blocks.shape, blocks.dtype), pltpu.HBM(blocks.shape, blocks.dtype),
                   jax.ShapeDtypeStruct((8, 128), F32)),
        in_specs=(hbm, hbm), out_specs=(sem, sem, hbm, hbm, pl.BlockSpec(memory_space=pltpu.VMEM)),
        input_output_aliases={0: 2, 1: 3},
        compiler_params=pltpu.CompilerParams(has_side_effects=pltpu.SideEffectType.DATAFLOW_SIDE_EFFECTING),
    )(pltpu.with_memory_space_constraint(blocks, pltpu.HBM),
      pltpu.with_memory_space_constraint(lax.empty(blocks.shape, blocks.dtype), pltpu.HBM))


def _scatter_wait(name, send_sems, recv_sems, src_thru, land_thru, after):
    def body(src, land, send_sems, recv_sems, *rest):
        me, peers = _flip_peers()
        for k, peer in enumerate(peers):
            _split_scatter_copy(src, land, send_sems, recv_sems, k, me, peer, True).wait_send()
            _split_scatter_copy(src, land, send_sems, recv_sems, k, me, peer, False).wait_recv()

    hbm = pl.BlockSpec(memory_space=pltpu.HBM)
    sem = pl.BlockSpec(memory_space=pltpu.SEMAPHORE)
    return pl.pallas_call(
        body, name=name,
        out_shape=(pltpu.HBM(src_thru.shape, src_thru.dtype), pltpu.HBM(land_thru.shape, land_thru.dtype)),
        in_specs=(hbm, hbm, sem, sem, *[pl.BlockSpec(memory_space=pl.ANY)] * len(after)), out_specs=(hbm, hbm),
        input_output_aliases={0: 0, 1: 1},
        compiler_params=pltpu.CompilerParams(has_side_effects=pltpu.SideEffectType.DATAFLOW_SIDE_EFFECTING),
    )(src_thru, land_thru, send_sems, recv_sems, *after)


def _exchange_alone(name, exch, after=()):
    def body():
        pass

    return _hosted_call(name, body, (1,), [], [], [], [], None, (), exch, after)


def _mm(name, grid, a, a_spec, b, b_spec, dims, outs, o_specs, acc_shape, extra=(), extra_specs=(), epi=None,
        exch=None, after=(), into=None, summed=None):
    nk = grid[-1]
    n_extra, n_out = len(extra), len(outs)

    def body(*refs):
        a_ref, b_ref = refs[0], refs[1]
        ex = refs[2:2 + n_extra]
        o_refs = refs[2 + n_extra:2 + n_extra + n_out]

        def write(res):
            res = (res,) if epi is None else epi(res, *[e[...] for e in ex])
            for idx, (o, r) in enumerate(zip(o_refs, res)):
                if summed is not None and idx == summed[0]:
                    first = pl.program_id(summed[1]) == 0

                    @pl.when(first)
                    def _(o=o, r=r):
                        o[...] = r.astype(o.dtype)

                    @pl.when(jnp.logical_not(first))
                    def _(o=o, r=r):
                        o[...] += r.astype(o.dtype)
                else:
                    o[...] = r.astype(o.dtype)

        if nk == 1:
            write(_bdot(a_ref[...], b_ref[...], dims))
            return
        acc = refs[-1]
        k = pl.program_id(len(grid) - 1)

        @pl.when(k == 0)
        def _():
            acc[...] = _bdot(a_ref[...], b_ref[...], dims)

        @pl.when(jnp.logical_and(k > 0, k < nk - 1))
        def _():
            acc[...] += _bdot(a_ref[...], b_ref[...], dims)

        @pl.when(k == nk - 1)
        def _():
            write(acc[...] + _bdot(a_ref[...], b_ref[...], dims))

    sem = ("parallel",) * (len(grid) - 1) + ("arbitrary",)
    if summed is not None:
        sem = ("arbitrary",) * len(grid)
    scratch = [] if nk == 1 else [pltpu.VMEM(acc_shape, F32)]
    return _hosted_call(name, body, grid, [a_spec, b_spec, *extra_specs], list(o_specs), list(outs),
                        scratch, sem, (a, b, *extra), exch, after, into)


def _add_epi(acc, add):
    return (acc + add,)


def _pack_epi(acc):
    return (_pack_pairs(acc),)


def _mm_nt(name, a, b, n_off, n, out_dtype=F32, tm=1024, tn=512, exch=None, after=()):
    m, kk = a.shape
    tm, tn = _t(m, tm), math.gcd(_t(n, tn), n_off)
    res = _mm(name, (m // tm, n // tn, 1),
              a, pl.BlockSpec((tm, kk), lambda i, j, k: (i, 0)),
              b, pl.BlockSpec((tn, kk), lambda i, j, k: (n_off // tn + j, 0)), NT,
              [jax.ShapeDtypeStruct((m, n), out_dtype)], [pl.BlockSpec((tm, tn), lambda i, j, k: (i, j))], (tm, tn),
              exch=exch, after=after)
    return res[0] if exch is None else res


def _mm_nt_halves(name, a, b2, n_off, n, tm=1024, tn=512, exch=None):
    m = a.shape[0]
    kh = b2.shape[2]
    tm, tn = _t(m, tm), math.gcd(_t(n, tn), n_off)

    def body(a_ref, b_ref, o_ref):
        o_ref[...] = _bdot(a_ref[:, :kh], b_ref[0], NT) + _bdot(a_ref[:, kh:], b_ref[1], NT)

    res = _hosted_call(name, body, (m // tm, n // tn),
                       [pl.BlockSpec((tm, 2 * kh), lambda i, j: (i, 0)),
                        pl.BlockSpec((2, tn, kh), lambda i, j: (0, n_off // tn + j, 0))],
                       [pl.BlockSpec((tm, tn), lambda i, j: (i, j))], [jax.ShapeDtypeStruct((m, n), F32)], [],
                       ("parallel", "parallel"), (a, b2), exch)
    return res[0] if exch is None else res


def _mm_nn_halves(name, a, b2, k_off=0, add=None, after=(), tm=1024, tk=1024):
    m, kk = a.shape
    half = b2.shape[2]
    tm, tk = _t(m, tm), math.gcd(_t(kk, tk), k_off)
    extra, especs, epi = (), (), None
    if add is not None:
        extra, especs, epi = (add,), (pl.BlockSpec((tm, half), lambda i, j, k: (i, j)),), _add_epi
    return _mm(name, (m // tm, 2, kk // tk),
               a, pl.BlockSpec((tm, tk), lambda i, j, k: (i, k)),
               b2, pl.BlockSpec((None, tk, half), lambda i, j, k: (j, k_off // tk + k, 0)), NN,
               [jax.ShapeDtypeStruct((m, 2 * half), F32)], [pl.BlockSpec((tm, half), lambda i, j, k: (i, j))],
               (tm, half), extra, especs, epi, after=after)[0]


def _mm_nn_blocked(name, a, b3, tm=1024, tn=512, exch=None):
    m, kk = a.shape
    nb, _, cs = b3.shape
    tm, tn = _t(m, tm), _t(cs, tn)
    per = cs // tn
    res = _mm(name, (m // tm, nb * per, 1),
              a, pl.BlockSpec((tm, kk), lambda i, j, k: (i, 0)),
              b3, pl.BlockSpec((None, kk, tn), lambda i, j, k: (j // per, 0, j % per)), NN,
              [jax.ShapeDtypeStruct((m, nb * cs), F32)], [pl.BlockSpec((tm, tn), lambda i, j, k: (i, j))],
              (tm, tn), exch=exch)
    return res[0] if exch is None else res


def _mm_nt_blocked(name, a, b3, after=(), tm=1024, tn=1024):
    m = a.shape[0]
    nb, n, cs = b3.shape
    tm, tn = _t(m, tm), _t(n, tn)
    return _mm(name, (m // tm, n // tn, nb),
               a, pl.BlockSpec((tm, cs), lambda i, j, k: (i, k)),
               b3, pl.BlockSpec((None, tn, cs), lambda i, j, k: (k, j, 0)), NT,
               [jax.ShapeDtypeStruct((m, n), F32)], [pl.BlockSpec((tm, tn), lambda i, j, k: (i, j))], (tm, tn),
               after=after)[0]


def _mm_nn(name, a, b, k_off=0, kk=None, add=None, out_dtype=F32, tm=1024, tn=1024, tk=1024, after=()):
    m = a.shape[0]
    kk = a.shape[1] if kk is None else kk
    n = b.shape[1]
    tm, tn, tk = _t(m, tm), _t(n, tn), math.gcd(_t(kk, tk), k_off)
    extra, especs, epi = (), (), None
    if add is not None:
        extra, especs, epi = (add,), (pl.BlockSpec((tm, tn), lambda i, j, k: (i, j)),), _add_epi
    return _mm(name, (m // tm, n // tn, kk // tk),
               a, pl.BlockSpec((tm, tk), lambda i, j, k: (i, k)),
               b, pl.BlockSpec((tk, tn), lambda i, j, k: (k_off // tk + k, j)), NN,
               [jax.ShapeDtypeStruct((m, n), out_dtype)], [pl.BlockSpec((tm, tn), lambda i, j, k: (i, j))], (tm, tn),
               extra, especs, epi, after=after)[0]


TN_ACC_ELEMENTS = 1 << 20


def _mm_tn_packed(name, a, b, pw, tk=1024, rows=None, into=None, b_blk=None, after=()):
    kk, m = a.shape
    nq = b.shape[1] // pw if b_blk is None else 1
    first = 0 if b_blk is None else b_blk
    tm, tk = _t(m, TN_ACC_ELEMENTS // pw), _t(kk, tk)
    rows = m if rows is None else rows
    row_blk = 0
    if into is not None:
        rows = into.shape[1]
        assert (rows - m) % tm == 0
        row_blk = (rows - m) // tm
    return _mm(name, (m // tm, nq, kk // tk),
               a, pl.BlockSpec((tk, tm), lambda i, j, k: (k, i)),
               b, pl.BlockSpec((tk, pw), lambda i, j, k: (k, first + j)), TN,
               [jax.ShapeDtypeStruct((nq, rows, pw // 2), F32)],
               [pl.BlockSpec((None, tm, pw // 2), lambda i, j, k: (j, row_blk + i, 0))], (tm, pw), epi=_pack_epi,
               into=into, after=after)[0]


def _rms_fwd(name, x, g, tm=256, exch=None):
    l, d = x.shape
    tm = _t(l, tm)

    def body(x_ref, g_ref, h_ref, r_ref):
        xv = x_ref[...]
        r = lax.rsqrt(jnp.mean(xv * xv, axis=-1, keepdims=True) + EPS)
        h_ref[...] = (xv * r * g_ref[...]).astype(BF)
        r_ref[...] = r

    return _hosted_call(
        name, body, (l // tm,),
        [pl.BlockSpec((tm, d), lambda i: (i, 0)), pl.BlockSpec((1, d), lambda i: (0, 0))],
        [pl.BlockSpec((tm, d), lambda i: (i, 0)), pl.BlockSpec((tm, 1), lambda i: (i, 0))],
        [jax.ShapeDtypeStruct((l, d), BF), jax.ShapeDtypeStruct((l, 1), F32)], [], ("parallel",), (x, g), exch)


def _rms_bwd(name, dh, x, r, g, dres, tm=256):
    l, d = x.shape
    tm = _t(l, tm)

    def body(dh_ref, x_ref, r_ref, g_ref, dres_ref, dx_ref, dxb_ref, dg_ref):
        i = pl.program_id(0)
        rr = r_ref[...]
        xhat = x_ref[...] * rr
        dhv = dh_ref[...]
        dxh = dhv * g_ref[...]
        dx = dres_ref[...] + rr * (dxh - xhat * jnp.mean(dxh * xhat, axis=-1, keepdims=True))
        dx_ref[...] = dx
        dxb_ref[...] = dx.astype(BF)

        @pl.when(i == 0)
        def _():
            dg_ref[...] = jnp.zeros_like(dg_ref)

        dg_ref[...] += jnp.sum(dhv * xhat, axis=0, keepdims=True)

    row = pl.BlockSpec((tm, d), lambda i: (i, 0))
    vec = pl.BlockSpec((1, d), lambda i: (0, 0))
    return pl.pallas_call(
        body, grid=(l // tm,), in_specs=[row, row, pl.BlockSpec((tm, 1), lambda i: (i, 0)), vec, row],
        out_specs=[row, row, vec],
        out_shape=[jax.ShapeDtypeStruct((l, d), F32), jax.ShapeDtypeStruct((l, d), BF),
                   jax.ShapeDtypeStruct((1, d), F32)],
        compiler_params=_params(("arbitrary",)), name=name)(dh, x, r, g, dres)


def _loss_head(x, g, target, tm=256):
    l, d = x.shape
    tm = _t(l, tm)

    def body(x_ref, g_ref, t_ref, dx_ref, dxb_ref, dg_ref, loss_ref):
        i = pl.program_id(0)
        xv = x_ref[...]
        gv = g_ref[...]
        r = lax.rsqrt(jnp.mean(xv * xv, axis=-1, keepdims=True) + EPS)
        xhat = xv * r
        e = xhat * gv - t_ref[...]
        dy = e * (1.0 / d)
        dxh = dy * gv
        dx = r * (dxh - xhat * jnp.mean(dxh * xhat, axis=-1, keepdims=True))
        dx_ref[...] = dx
        dxb_ref[...] = dx.astype(BF)

        @pl.when(i == 0)
        def _():
            dg_ref[...] = jnp.zeros_like(dg_ref)
            loss_ref[...] = jnp.zeros_like(loss_ref)

        dg_ref[...] += jnp.sum(dy * xhat, axis=0, keepdims=True)
        loss_ref[...] += 0.5 * jnp.sum(jnp.sum(e * e, axis=-1, keepdims=True) * (1.0 / d), axis=0, keepdims=True)

    row = pl.BlockSpec((tm, d), lambda i: (i, 0))
    vec = pl.BlockSpec((1, d), lambda i: (0, 0))
    return pl.pallas_call(
        body, grid=(l // tm,), in_specs=[row, vec, row],
        out_specs=[row, row, vec, pl.BlockSpec((1, 1), lambda i: (0, 0))],
        out_shape=[jax.ShapeDtypeStruct((l, d), F32), jax.ShapeDtypeStruct((l, d), BF),
                   jax.ShapeDtypeStruct((1, d), F32), jax.ShapeDtypeStruct((1, 1), F32)],
        compiler_params=_params(("arbitrary",)), name="loss_head")(x, g, target)


CONV_HALO = 8


def _conv_pre(x_ref, w_ref, b_ref, i, tr):
    r0 = pl.multiple_of(i * tr, tr)
    cur = x_ref[pl.ds(r0, tr), :]
    prev = x_ref[pl.ds(pl.multiple_of(jnp.maximum(r0 - CONV_HALO, 0), CONV_HALO), CONV_HALO), :]
    prev = jnp.where(i > 0, prev, 0.0)
    ext = jnp.concatenate([prev, cur], axis=0)
    taps = []
    for k in range(CONV_TAPS):
        s = CONV_TAPS - 1 - k
        taps.append(cur if s == 0 else pltpu.roll(ext, s, 0)[CONV_HALO:])
    pre = b_ref[...] + sum(w_ref[k:k + 1, :] * taps[k] for k in range(CONV_TAPS))
    return r0, pre, taps


def _conv_fwd(pm, col_off, conv_w, conv_b, cw=256, tr=512):
    l = pm.shape[0]
    c = conv_w.shape[1]
    cw, tr = _t(c, cw), _t(l, tr)
    assert col_off % cw == 0

    def body(x_ref, w_ref, b_ref, o_ref):
        def step(i, carry):
            r0, pre, _ = _conv_pre(x_ref, w_ref, b_ref, i, tr)
            o_ref[pl.ds(r0, tr), :] = _silu(pre)
            return carry
        lax.fori_loop(0, l // tr, step, 0)

    return pl.pallas_call(
        body, grid=(c // cw,),
        in_specs=[pl.BlockSpec((l, cw), lambda j: (0, col_off // cw + j)),
                  pl.BlockSpec((CONV_TAPS, cw), lambda j: (0, j)), pl.BlockSpec((1, cw), lambda j: (0, j))],
        out_specs=pl.BlockSpec((l, cw), lambda j: (0, j)), out_shape=jax.ShapeDtypeStruct((l, c), F32),
        compiler_params=_params(("parallel",)), name="conv_fwd")(pm, conv_w, conv_b)


def _conv_bwd(pm, col_off, conv_w, conv_b, dy, dproj, cw=256, tr=512):
    l = pm.shape[0]
    c = conv_w.shape[1]
    cw, tr = _t(c, cw), _t(l, tr)
    nt = l // tr

    def body(x_ref, w_ref, b_ref, dy_ref, _, dx_ref, dw_ref, db_ref, dpre_ref):
        def step1(i, carry):
            dws, db = carry
            r0, pre, taps = _conv_pre(x_ref, w_ref, b_ref, i, tr)
            dpre = dy_ref[pl.ds(r0, tr), :] * _dsilu(pre)
            dpre_ref[pl.ds(r0, tr), :] = dpre
            dws = tuple(dws[k] + jnp.sum(dpre * taps[k], axis=0, keepdims=True) for k in range(CONV_TAPS))
            return dws, db + jnp.sum(dpre, axis=0, keepdims=True)

        z = jnp.zeros((1, cw), F32)
        dws, db = lax.fori_loop(0, nt, step1, ((z,) * CONV_TAPS, z))
        for k in range(CONV_TAPS):
            dw_ref[k:k + 1, :] = dws[k]
        db_ref[...] = db

        def step2(i, carry):
            r0 = pl.multiple_of(i * tr, tr)
            cur = dpre_ref[pl.ds(r0, tr), :]
            nxt = dpre_ref[pl.ds(pl.multiple_of(jnp.minimum(r0 + tr, l - CONV_HALO), CONV_HALO), CONV_HALO), :]
            nxt = jnp.where(i < nt - 1, nxt, 0.0)
            ext = jnp.concatenate([cur, nxt], axis=0)
            acc = w_ref[CONV_TAPS - 1:CONV_TAPS, :] * cur
            for k in range(CONV_TAPS - 1):
                s = CONV_TAPS - 1 - k
                acc = acc + w_ref[k:k + 1, :] * pltpu.roll(ext, tr + CONV_HALO - s, 0)[:tr]
            dx_ref[pl.ds(r0, tr), :] = acc.astype(dx_ref.dtype)
            return carry
        lax.fori_loop(0, nt, step2, 0)

    col = pl.BlockSpec((l, cw), lambda j: (0, j))
    shifted = pl.BlockSpec((l, cw), lambda j: (0, col_off // cw + j))
    return pl.pallas_call(
        body, grid=(c // cw,),
        in_specs=[shifted, pl.BlockSpec((CONV_TAPS, cw), lambda j: (0, j)), pl.BlockSpec((1, cw), lambda j: (0, j)),
                  col, pl.BlockSpec(memory_space=pl.ANY)],
        out_specs=[shifted, pl.BlockSpec((CONV_TAPS, cw), lambda j: (0, j)), pl.BlockSpec((1, cw), lambda j: (0, j))],
        out_shape=[jax.ShapeDtypeStruct(dproj.shape, dproj.dtype), jax.ShapeDtypeStruct((CONV_TAPS, c), F32),
                   jax.ShapeDtypeStruct((1, c), F32)],
        scratch_shapes=[pltpu.VMEM((l, cw), F32)], input_output_aliases={4: 0},
        compiler_params=_params(("parallel",)), name="conv_bwd")(pm, conv_w, conv_b, dy, dproj)


def _split(x, pieces):
    out, rest = [], x
    for _ in range(pieces):
        piece = rest.astype(BF)
        out.append(piece)
        rest = rest - piece.astype(F32)
    return out


def _rows_times(xs, m_stack, pieces):
    x = xs[0] if len(xs) == 1 else jnp.concatenate(xs, axis=0)
    out = _bdot(jnp.concatenate(_split(x, pieces), axis=1), m_stack)
    sizes = [v.shape[0] for v in xs]
    offs = np.cumsum([0] + sizes)
    return [out[offs[i]:offs[i + 1]] for i in range(len(xs))]


def _times_rows(m_stack, x, pieces):
    return _bdot(m_stack, jnp.concatenate(_split(x, pieces), axis=0))


EXPAND_PIECES = 3
FOLD_PIECES = 2


def _ssd_consts(r, p, t):
    assert p == t, "heads expand to P lanes of the inputs and to T lanes of the decay matrices alike"
    rp = r * p
    tri = np.tril(np.ones((t, t), np.float32))
    ep = np.zeros((HEAD_LANES, rp), np.float32)
    ep[np.arange(rp) // p, np.arange(rp)] = 1.0
    itile = (np.arange(t)[:, None] == (np.arange(rp) % t)[None, :]).astype(np.float32)
    lmask = (np.arange(t)[:, None] >= (np.arange(rp) % t)[None, :]).astype(np.float32)
    bmask = ((np.arange(rp) // t)[:, None] == (np.arange(rp) // p)[None, :]).astype(np.float32)
    return [jnp.asarray(np.concatenate([ep] * EXPAND_PIECES, axis=0), BF),
            jnp.asarray(np.concatenate([ep.T] * FOLD_PIECES, axis=0), BF),
            jnp.asarray(np.concatenate([tri] * EXPAND_PIECES, axis=1), BF),
            jnp.asarray(np.concatenate([tri.T] * EXPAND_PIECES, axis=1), BF),
            jnp.asarray(ep.T.copy()), jnp.asarray(itile), jnp.asarray(lmask), jnp.asarray(bmask),
            jnp.asarray(bmask, BF)]


def _ssd_heads(dtr_ref, dtb_ref, alog_ref, e_stack, tri_stack, g_n, t):
    pre = [dtr_ref[g] + dtb_ref[g] for g in range(g_n)]
    dt = [jnp.maximum(v, 0.0) + jnp.log(1.0 + jnp.exp(-jnp.abs(v))) for v in pre]
    a = [-jnp.exp(alog_ref[g]) for g in range(g_n)]
    adt = jnp.concatenate([dt[g] * a[g] for g in range(g_n)], axis=1)
    cs = _times_rows(tri_stack, adt, EXPAND_PIECES)
    cs = [cs[:, g * HEAD_LANES:(g + 1) * HEAD_LANES] for g in range(g_n)]
    at_lanes = _rows_times([v for g in range(g_n) for v in (dt[g], cs[g])], e_stack, EXPAND_PIECES)
    return [(pre[g], dt[g], a[g], cs[g][t - 1:t, :], at_lanes[2 * g], at_lanes[2 * g + 1]) for g in range(g_n)]


def _ssd_common(xs, bm, dtx, csx, itile, lmask, bmask, r, t):
    ecsx = jnp.exp(csx)
    wx = jnp.exp(csx[t - 1:t, :] - csx)
    csrow = jnp.sum(csx * itile, axis=0, keepdims=True)
    lx = jnp.exp(jnp.where(lmask > 0.0, csx - csrow, NEG_BIG))
    xdt = xs * dtx
    xblk = jnp.concatenate([xdt.astype(BF)] * r, axis=0) * bmask
    btile = jnp.concatenate([bm.astype(BF)] * r, axis=0)
    return ecsx, wx, lx, xdt, xblk, btile


def _ssd_specs(l, g_n, r, p, n, t, conv, rev):
    nc = l // t
    rp = r * p
    cidx = (lambda c: nc - 1 - c) if rev else (lambda c: c)
    row_spec = lambda width: pl.BlockSpec((t, width), lambda c: (cidx(c), 0))
    dtr_spec = pl.BlockSpec((g_n, t, HEAD_LANES), lambda c: (0, cidx(c), 0))
    par_spec = pl.BlockSpec((g_n, 1, HEAD_LANES), lambda c: (0, 0, 0))
    dskx_spec = pl.BlockSpec((g_n, 1, rp), lambda c: (0, 0, 0))
    st_spec = pl.BlockSpec((None, g_n, rp, n), lambda c: (cidx(c), 0, 0, 0))
    return nc, row_spec, dtr_spec, par_spec, dskx_spec, st_spec


def _const_specs(consts):
    return [pl.BlockSpec(a.shape, lambda c: (0, 0)) for a in consts]


def _ssd_fwd(xbc, dtr, dtb, alog, dskx, pm, norm_g, dims, exch=None):
    h, g_n, r, p, n, t = dims
    l, conv = xbc.shape
    rp, hp = r * p, h * p
    nc, row_spec, dtr_spec, par_spec, dskx_spec, st_spec = _ssd_specs(l, g_n, r, p, n, t, conv, False)
    consts = _ssd_consts(r, p, t)

    def body(x_ref, dtr_ref, dtb_ref, alog_ref, dskx_ref, z_ref, ng_ref,
             e_ref, et_ref, tri_ref, trit_ref, ept_ref, it_ref, lm_ref, bmk_ref, bmkb_ref,
             y_ref, st_ref, yn_ref, rn_ref, s_ref):
        @pl.when(pl.program_id(0) == 0)
        def _():
            s_ref[...] = jnp.zeros_like(s_ref)

        heads = _ssd_heads(dtr_ref, dtb_ref, alog_ref, e_ref[...], tri_ref[...], g_n, t)
        for g in range(g_n):
            xs = x_ref[:, g * rp:(g + 1) * rp]
            bm = x_ref[:, hp + g * n:hp + (g + 1) * n]
            cm = x_ref[:, hp + (g_n + g) * n:hp + (g_n + g + 1) * n]
            _, _, _, last, dtx, csx = heads[g]
            ecsx, wx, lx, xdt, xblk, btile = _ssd_common(xs, bm, dtx, csx, it_ref[...], lm_ref[...], bmkb_ref[...],
                                                         r, t)
            s_in = s_ref[g]
            st_ref[g] = s_in
            cbx = _bdot(cm, btile, NT)
            yd = _bdot(cbx * lx, xblk)
            yo = ecsx * _bdot(cm, s_in, NT)
            y_ref[:, g * rp:(g + 1) * rp] = yd + yo + dskx_ref[g] * xs
            elast = jnp.sum(ept_ref[...] * jnp.exp(last), axis=1, keepdims=True)
            s_ref[g] = elast * s_in + _bdot(xdt * wx, bm, TN)

        v = y_ref[...] * _silu(z_ref[...])
        rr = lax.rsqrt(jnp.mean(v * v, axis=-1, keepdims=True) + EPS)
        yn_ref[...] = (v * rr * ng_ref[...]).astype(BF)
        rn_ref[...] = rr

    return _hosted_call(
        "ssd_fwd", body, (nc,),
        [row_spec(conv), dtr_spec, par_spec, par_spec, dskx_spec, row_spec(hp),
         pl.BlockSpec((1, hp), lambda c: (0, 0)), *_const_specs(consts)],
        [row_spec(hp), st_spec, row_spec(hp), row_spec(1)],
        [jax.ShapeDtypeStruct((l, hp), F32), jax.ShapeDtypeStruct((nc, g_n, rp, n), F32),
         jax.ShapeDtypeStruct((l, hp), BF), jax.ShapeDtypeStruct((l, 1), F32)],
        [pltpu.VMEM((g_n, rp, n), F32)], ("arbitrary",),
        (xbc, dtr, dtb, alog, dskx, pm, norm_g, *consts), exch)


def _ssd_bwd(xbc, dtr, dtb, alog, dskx, states, dyn, y, pm, rstd, norm_g, dims, after=()):
    h, g_n, r, p, n, t = dims
    l, conv = xbc.shape
    rp, hp = r * p, h * p
    nc, row_spec, dtr_spec, par_spec, dskx_spec, st_spec = _ssd_specs(l, g_n, r, p, n, t, conv, True)
    consts = _ssd_consts(r, p, t)

    def fold_rows(v, rows):
        return sum(v[k * rows:(k + 1) * rows, :] for k in range(r))

    def body(x_ref, dtr_ref, dtb_ref, alog_ref, dskx_ref, st_ref, dyn_ref, y_ref, z_ref, rn_ref, ng_ref,
             e_ref, et_ref, tri_ref, trit_ref, ept_ref, it_ref, lm_ref, bmk_ref, bmkb_ref,
             dx_ref, ddt_ref, dbias_ref, dalog_ref, dd_ref, dz_ref, dng_ref, ds_ref, dy_ref):
        @pl.when(pl.program_id(0) == 0)
        def _():
            ds_ref[...] = jnp.zeros_like(ds_ref)
            dbias_ref[...] = jnp.zeros_like(dbias_ref)
            dalog_ref[...] = jnp.zeros_like(dalog_ref)
            dd_ref[...] = jnp.zeros_like(dd_ref)
            dng_ref[...] = jnp.zeros_like(dng_ref)

        yv, zv, rr, dn = y_ref[...], z_ref[...], rn_ref[...], dyn_ref[...]
        sz = _silu(zv)
        vhat = yv * sz * rr
        dvh = dn * ng_ref[...]
        dv = rr * (dvh - vhat * jnp.mean(dvh * vhat, axis=-1, keepdims=True))
        dy_ref[...] = dv * sz
        dz_ref[...] = (dv * yv * _dsilu(zv)).astype(BF)
        dng_ref[...] += jnp.sum(dn * vhat, axis=0, keepdims=True)

        ept, itile, bmask = ept_ref[...], it_ref[...], bmk_ref[...]
        last_row = lax.broadcasted_iota(jnp.int32, (t, HEAD_LANES), 0) == t - 1
        heads = _ssd_heads(dtr_ref, dtb_ref, alog_ref, e_ref[...], tri_ref[...], g_n, t)
        to_fold, dlast_state = [], []
        for g in range(g_n):
            xs = x_ref[:, g * rp:(g + 1) * rp]
            bm = x_ref[:, hp + g * n:hp + (g + 1) * n]
            cm = x_ref[:, hp + (g_n + g) * n:hp + (g_n + g + 1) * n]
            dy = dy_ref[:, g * rp:(g + 1) * rp]
            pre, dt, a, last, dtx, csx = heads[g]
            ecsx, wx, lx, xdt, xblk, btile = _ssd_common(xs, bm, dtx, csx, itile, lm_ref[...], bmkb_ref[...], r, t)
            s_in = st_ref[g]
            ds_out = ds_ref[g]
            elast_row = jnp.exp(last)
            elast = jnp.sum(ept * elast_row, axis=1, keepdims=True)

            cbx = _bdot(cm, btile, NT)
            amat = cbx * lx
            da = _bdot(dy, xblk, NT)
            dxdt = fold_rows(_bdot(amat, dy, TN) * bmask, t)
            dcbx = da * lx
            q = da * amat
            dc = _bdot(dcbx, btile)
            db = fold_rows(_bdot(dcbx, cm, TN), t)

            g0 = _bdot(cm, s_in, NT)
            dg0 = dy * ecsx
            dc = dc + _bdot(dg0, s_in)

            z = _bdot(bm, ds_out, NT)
            dxdt = dxdt + z * wx
            db = db + _bdot(xdt * wx, ds_out)
            ds_ref[g] = elast * ds_out + _bdot(dg0, cm, TN)

            dx_ref[:, g * rp:(g + 1) * rp] = dskx_ref[g] * dy + dxdt * dtx
            dx_ref[:, hp + g * n:hp + (g + 1) * n] = db
            dx_ref[:, hp + (g_n + g) * n:hp + (g_n + g + 1) * n] = dc

            dwx = z * xdt * wx
            rows = jnp.concatenate([jnp.sum(dy * xs, axis=0, keepdims=True), jnp.sum(dwx, axis=0, keepdims=True),
                                    jnp.zeros((14, rp), F32)], axis=0)
            to_fold += [q - itile * jnp.sum(q, axis=0, keepdims=True) + dy * g0 * ecsx - dwx, dxdt * xs, rows]
            dlast_state.append(
                jnp.sum(jnp.sum(ds_out * s_in, axis=1, keepdims=True) * ept, axis=0, keepdims=True) * elast_row)

        folded = _rows_times(to_fold, et_ref[...], FOLD_PIECES)
        dcs = []
        for g in range(g_n):
            f_cs, f_rows = folded[3 * g], folded[3 * g + 2]
            dd_ref[g] += f_rows[0:1]
            dcs.append(f_cs + jnp.where(last_row, dlast_state[g] + f_rows[1:2], 0.0))
        dadt_all = _times_rows(trit_ref[...], jnp.concatenate(dcs, axis=1), EXPAND_PIECES)
        for g in range(g_n):
            pre, dt, a = heads[g][:3]
            dadt = dadt_all[:, g * HEAD_LANES:(g + 1) * HEAD_LANES]
            dalog_ref[g] += jnp.sum(dadt * dt, axis=0, keepdims=True) * a
            dpre = (dadt * a + folded[3 * g + 1]) * (1.0 / (1.0 + jnp.exp(-pre)))
            ddt_ref[g] = dpre
            dbias_ref[g] += jnp.sum(dpre, axis=0, keepdims=True)

    par_shape = jax.ShapeDtypeStruct((g_n, 1, HEAD_LANES), F32)
    vec = pl.BlockSpec((1, hp), lambda c: (0, 0))
    return _hosted_call(
        "ssd_bwd", body, (nc,),
        [row_spec(conv), dtr_spec, par_spec, par_spec, dskx_spec, st_spec, row_spec(hp), row_spec(hp), row_spec(hp),
         row_spec(1), vec, *_const_specs(consts)],
        [row_spec(conv), dtr_spec, par_spec, par_spec, par_spec, row_spec(hp), vec],
        [jax.ShapeDtypeStruct((l, conv), F32), jax.ShapeDtypeStruct((g_n, l, HEAD_LANES), F32),
         par_shape, par_shape, par_shape, jax.ShapeDtypeStruct((l, pm.shape[1]), BF),
         jax.ShapeDtypeStruct((1, hp), F32)],
        [pltpu.VMEM((g_n, rp, n), F32), pltpu.VMEM((t, hp), F32)], ("arbitrary",),
        (xbc, dtr, dtb, alog, dskx, states, dyn, y, pm, rstd, norm_g, *consts), after=after)


POOL_HALO = 16


def _pool_mix(name, src, dp, backward, out_dtype, into=None, cw=256, tr=512):
    l = src.shape[0]
    gd = dp // len(POOL_WINDOWS)
    cwl, trl = _t(gd, cw), _t(l, tr)
    nt = l // trl

    def body(x_ref, *rest):
        o_ref = rest[-1]
        gi = pl.program_id(0)
        for wi, win in enumerate(POOL_WINDOWS):
            @pl.when(gi == wi)
            def _(win=win):
                def step(i, carry):
                    r0 = pl.multiple_of(i * trl, trl)
                    cur = x_ref[pl.ds(r0, trl), :]
                    trow = r0 + lax.broadcasted_iota(jnp.int32, (trl, 1), 0)
                    cnt = jnp.minimum(trow + 1, win).astype(F32)
                    if not backward:
                        halo = x_ref[pl.ds(pl.multiple_of(jnp.maximum(r0 - POOL_HALO, 0), POOL_HALO), POOL_HALO), :]
                        halo = jnp.where(i > 0, halo, 0.0)
                        s = jnp.concatenate([halo, cur], axis=0)
                        sh = 1
                        while sh < win:
                            s = s + pltpu.roll(s, sh, 0)
                            sh *= 2
                        res = s[POOL_HALO:] / cnt - cur
                    else:
                        halo = x_ref[pl.ds(pl.multiple_of(jnp.minimum(r0 + trl, l - POOL_HALO), POOL_HALO),
                                           POOL_HALO), :]
                        hrow = r0 + trl + lax.broadcasted_iota(jnp.int32, (POOL_HALO, 1), 0)
                        hcnt = jnp.minimum(hrow + 1, win).astype(F32)
                        halo = jnp.where(i < nt - 1, halo / hcnt, 0.0)
                        s = jnp.concatenate([cur / cnt, halo], axis=0)
                        sh = 1
                        while sh < win:
                            s = s + pltpu.roll(s, trl + POOL_HALO - sh, 0)
                            sh *= 2
                        res = s[:trl] - cur
                    o_ref[pl.ds(r0, trl), :] = res.astype(o_ref.dtype)
                    return carry
                lax.fori_loop(0, nt, step, 0)

    col = pl.BlockSpec((l, cwl), lambda g, j: (0, g * (gd // cwl) + j))
    if into is None:
        return pl.pallas_call(
            body, grid=(len(POOL_WINDOWS), gd // cwl), in_specs=[col], out_specs=col,
            out_shape=jax.ShapeDtypeStruct((l, dp), out_dtype),
            compiler_params=_params(("parallel", "parallel")), name=name)(src)
    return pl.pallas_call(
        body, grid=(len(POOL_WINDOWS), gd // cwl), in_specs=[col, pl.BlockSpec(memory_space=pl.ANY)], out_specs=col,
        out_shape=jax.ShapeDtypeStruct(into.shape, into.dtype), input_output_aliases={1: 0},
        compiler_params=_params(("parallel", "parallel")), name=name)(src, into)


def _pool_out_bwd_data(dx, wp_out, mg, pu, scale, tm=1024, tn=512):
    l, d = dx.shape
    dp = wp_out.shape[0]
    tm, tn = _t(l, tm), _t(dp, tn)

    def epi(dyp, m, gt, sc):
        sg = _silu(gt)
        return dyp * sc * sg, dyp * m * sc * _dsilu(gt), jnp.sum(dyp * m * sg, axis=0, keepdims=True)

    tile = pl.BlockSpec((tm, tn), lambda j, i, k: (i, j))
    right = pl.BlockSpec((tm, tn), lambda j, i, k: (i, dp // tn + j))
    vec = pl.BlockSpec((1, tn), lambda j, i, k: (0, j))
    return _mm("pool_out_bwd_data", (dp // tn, l // tm, 1),
               dx, pl.BlockSpec((tm, d), lambda j, i, k: (i, 0)),
               wp_out, pl.BlockSpec((tn, d), lambda j, i, k: (j, 0)), NT,
               [jax.ShapeDtypeStruct((l, dp), BF), jax.ShapeDtypeStruct((l, 2 * dp), BF),
                jax.ShapeDtypeStruct((1, dp), F32)], [tile, right, vec], (tm, tn),
               (mg, pu, scale), (tile, right, vec), epi, summed=(2, 1))


def _group_mm_fwd(mp, wg, pu, scale, tm=1024, tn=1024, tk=1024):
    l, dp = mp.shape
    ng, gd = wg.shape[0], wg.shape[1]
    tm, tn, tk = _t(l, tm), _t(gd, tn), _t(gd, tk)

    def epi(acc, gate, sc):
        return acc, acc * sc * _silu(gate)

    out = pl.BlockSpec((tm, tn), lambda g, i, j, k: (i, g * (gd // tn) + j))
    return _mm("group_mm_fwd", (ng, l // tm, gd // tn, gd // tk),
               mp, pl.BlockSpec((tm, tk), lambda g, i, j, k: (i, g * (gd // tk) + k)),
               wg, pl.BlockSpec((None, tk, tn), lambda g, i, j, k: (g, k, j)), NN,
               [jax.ShapeDtypeStruct((l, dp), F32), jax.ShapeDtypeStruct((l, dp), BF)], [out, out], (tm, tn),
               (pu, scale),
               (pl.BlockSpec((tm, tn), lambda g, i, j, k: (i, (dp + g * gd) // tn + j)),
                pl.BlockSpec((1, tn), lambda g, i, j, k: (0, g * (gd // tn) + j))), epi)


def _group_mm_bwd_data(dmg, wg, after=(), tm=1024, tn=1024, tk=1024):
    l, dp = dmg.shape
    ng, gd = wg.shape[0], wg.shape[1]
    tm, tn, tk = _t(l, tm), _t(gd, tn), _t(gd, tk)
    return _mm("group_mm_bwd_data", (ng, l // tm, gd // tn, gd // tk),
               dmg, pl.BlockSpec((tm, tk), lambda g, i, j, k: (i, g * (gd // tk) + k)),
               wg, pl.BlockSpec((None, tn, tk), lambda g, i, j, k: (g, j, k)), NT,
               [jax.ShapeDtypeStruct((l, dp), F32)],
               [pl.BlockSpec((tm, tn), lambda g, i, j, k: (i, g * (gd // tn) + j))], (tm, tn), after=after)[0]


def _group_mm_bwd_weight(mp, dmg, ng, tm=512, tk=1024):
    l, dp = mp.shape
    gd = dp // ng
    tm, tk = _t(gd, tm), _t(l, tk)
    return _mm("group_mm_bwd_weight", (ng, gd // tm, l // tk),
               mp, pl.BlockSpec((tk, tm), lambda g, i, k: (k, g * (gd // tm) + i)),
               dmg, pl.BlockSpec((tk, gd), lambda g, i, k: (k, g)), TN,
               [jax.ShapeDtypeStruct((ng, gd, gd // 2), F32)],
               [pl.BlockSpec((None, tm, gd // 2), lambda g, i, k: (g, i, 0))], (tm, gd), epi=_pack_epi)[0]


def _cast_bf16(name, w, tr=256):
    r, c = w.shape
    tr = _row_tile(r, tr, 16)

    def body(w_ref, o_ref):
        o_ref[...] = w_ref[...].astype(BF)

    blk = pl.BlockSpec((tr, c), lambda i: (i, 0))
    return pl.pallas_call(body, grid=(r // tr,), in_specs=[blk], out_specs=blk,
                          out_shape=jax.ShapeDtypeStruct((r, c), BF), compiler_params=_params(("parallel",)),
                          name=name)(w)


def _pack_rows(name, w, tr=256):
    r, c = w.shape
    tr = _row_tile(r, tr)

    def body(w_ref, o_ref):
        o_ref[...] = _pack_pairs(w_ref[...])

    return pl.pallas_call(body, grid=(r // tr, 2), in_specs=[pl.BlockSpec((tr, c // 2), lambda i, q: (i, q))],
                          out_specs=pl.BlockSpec((None, tr, c // 4), lambda i, q: (q, i, 0)),
                          out_shape=jax.ShapeDtypeStruct((2, r, c // 4), F32),
                          compiler_params=_params(("parallel", "parallel")), name=name)(w)


def _unpack_rows(name, w0, w1, tr=512):
    r, h = w0.shape
    tr = _row_tile(r, tr, 16)

    def body(w0_ref, w1_ref, o_ref):
        for q, w_ref in enumerate((w0_ref, w1_ref)):
            hi, lo = _unpack_pairs(w_ref[...])
            o_ref[q, :, :h] = hi.astype(BF)
            o_ref[q, :, h:] = lo.astype(BF)

    words = pl.BlockSpec((tr, h), lambda i: (i, 0))
    return pl.pallas_call(body, grid=(r // tr,), in_specs=[words, words],
                          out_specs=pl.BlockSpec((2, tr, 2 * h), lambda i: (0, i, 0)),
                          out_shape=jax.ShapeDtypeStruct((2, r, 2 * h), BF), compiler_params=_params(("parallel",)),
                          name=name)(w0, w1)


def _reduce_packed(name, recv, tr=256):
    nd, r, h = recv.shape
    tr = _row_tile(r, tr)

    def body(p_ref, o_ref):
        hi, lo = _unpack_pairs(p_ref[0])
        for k in range(1, nd):
            a, b = _unpack_pairs(p_ref[k])
            hi, lo = hi + a, lo + b
        o_ref[:, :h] = hi
        o_ref[:, h:] = lo

    return pl.pallas_call(body, grid=(r // tr,), in_specs=[pl.BlockSpec((nd, tr, h), lambda i: (0, i, 0))],
                          out_specs=pl.BlockSpec((tr, 2 * h), lambda i: (i, 0)),
                          out_shape=jax.ShapeDtypeStruct((r, 2 * h), F32), compiler_params=_params(("parallel",)),
                          name=name)(recv)


def _adamw_math(w, g, m, v):
    m2 = ADAM_B1 * m + (1.0 - ADAM_B1) * g
    v2 = ADAM_B2 * v + (1.0 - ADAM_B2) * (g * g)
    m_hat = m2 / (1.0 - ADAM_B1 ** ADAM_STEP)
    v_hat = v2 / (1.0 - ADAM_B2 ** ADAM_STEP)
    delta = -ADAM_LR * (m_hat / (jnp.sqrt(v_hat) + ADAM_EPS) + ADAM_WD * w)
    return delta, m2, v2


def _adamw(name, w, g, m, v, tr=256):
    r, c = w.shape
    tr = _row_tile(r, tr)

    def body(w_ref, g_ref, m_ref, v_ref, d_ref, m2_ref, v2_ref):
        d, m2, v2 = _adamw_math(w_ref[...], g_ref[...], m_ref[...], v_ref[...])
        d_ref[...] = d
        m2_ref[...] = m2
        v2_ref[...] = v2

    blk = pl.BlockSpec((tr, c), lambda i: (i, 0))
    shp = jax.ShapeDtypeStruct((r, c), F32)
    return pl.pallas_call(body, grid=(r // tr,), in_specs=[blk] * 4, out_specs=[blk] * 3, out_shape=[shp] * 3,
                          compiler_params=_params(("parallel",)), name=name)(w, g, m, v)


def _sum_slots(name, a):
    nd, r, c = a.shape

    def body(a_ref, o_ref):
        s = a_ref[0]
        for k in range(1, nd):
            s = s + a_ref[k]
        o_ref[...] = s

    return pl.pallas_call(body, out_shape=jax.ShapeDtypeStruct((r, c), F32), name=name)(a)


def _head_rows(v, g_n, r):
    return jnp.pad(v.reshape(g_n, 1, r), ((0, 0), (0, 0), (0, HEAD_LANES - r)))


class _Later:
    def __init__(self, hosted, first_of, weights_of, pool_w_out_of, send):
        self.hosted, self.first_of, self.weights_of = hosted, first_of, weights_of
        self.pool_w_out_of, self.send = pool_w_out_of, send


def _listed(res):
    return res if isinstance(res, (list, tuple)) else [res]


def _local_step(x, target, ln_g, final_g, conv_b, dt_bias, a_log, d_skip, norm_g, later):
    l, d = x.shape
    di = norm_g.shape[1]
    h = dt_bias.shape[1]
    p = di // h
    conv = conv_b.shape[1]
    n, t = SSM_STATE, SSD_CHUNK
    g_n = (conv - di) // (2 * n)
    r = h // g_n
    nm = di + conv
    ng = len(POOL_WINDOWS)
    dims = (h, g_n, r, p, n, t)
    pw = d

    hosted = later.hosted
    h0, r0, *arrived_0 = _rms_fwd("rms0_fwd", x, ln_g[0:1], exch=hosted.get("rms0"))
    wt_in, conv_w, scale = later.first_of(arrived_0)
    dp = scale.shape[1]
    pm, *arrived_i = _listed(_mm_nt_halves("in_proj_main", h0, wt_in, 0, nm, exch=hosted.get("in_proj")))
    dtr = _mm_nt_halves("in_proj_dt", h0, wt_in, nm, h, tn=h)
    xbc = _conv_fwd(pm, di, conv_w, conv_b)
    dtb, alog = _head_rows(dt_bias, g_n, r), _head_rows(a_log, g_n, r)
    dskx = jnp.repeat(d_skip.reshape(g_n, 1, r), p, axis=2)
    dtr_g = jnp.pad(jnp.transpose(dtr.reshape(l, g_n, r), (1, 0, 2)), ((0, 0), (0, 0), (0, HEAD_LANES - r)))
    y, states, yn, r_n, *arrived_s = _ssd_fwd(xbc, dtr_g, dtb, alog, dskx, pm, norm_g, dims, exch=hosted.get("ssd"))
    w_out, wg, wp_in = later.weights_of(arrived_i, arrived_s)
    x1 = _mm_nn("ssm_out_proj", yn, w_out, add=x)

    h1, r1 = _rms_fwd("rms1_fwd", x1, ln_g[1:2])
    pu, *arrived_p = _listed(_mm_nn_blocked("pool_in_proj", h1, wp_in, exch=hosted.get("pool_in_proj")))
    wp_out = later.pool_w_out_of(arrived_p)
    mp = _pool_mix("pool_mix_fwd", pu, dp, False, BF)
    mg, yp = _group_mm_fwd(mp, wg, pu, scale)
    x2 = _mm_nn("pool_out_proj", yp, wp_out, add=x1)

    dx2, dx2_b, d_final_g, loss = _loss_head(x2, final_g, target)

    gw_pout = _mm_tn_packed("pool_out_bwd_weight", yp, dx2_b, pw)
    going = later.send("pool_w_out", gw_pout)
    dmg, dpu, d_scale = _pool_out_bwd_data(dx2_b, wp_out, mg, pu, scale)
    dmp = _group_mm_bwd_data(dmg, wg, after=going)
    gw_g = _group_mm_bwd_weight(mp, dmg, ng)
    going = later.send("pool_w_group", gw_g)
    dpu = _pool_mix("pool_mix_bwd", dmp, dp, True, BF, into=dpu)
    dh1 = _mm_nt_blocked("pool_in_bwd_data", dpu, wp_in, after=going)
    gw_pin = _mm_tn_packed("pool_in_bwd_weight", h1, dpu, 2 * dp // N_DEV)
    going = later.send("pool_w_in", gw_pin)
    dx1, dx1_b, d_ln1 = _rms_bwd("rms1_bwd", dh1, x1, r1, ln_g[1:2], dx2)

    dyn = _mm_nt("ssm_out_bwd_data", dx1_b, w_out, 0, di, tn=1024, after=going)
    gw_out = _mm_tn_packed("ssm_out_bwd_weight", yn, dx1_b, pw)
    going = later.send("ssm_w_out", gw_out)
    dxbc, ddt_g, dbias_g, dalog_g, dd_g, dproj, d_norm_g = _ssd_bwd(
        xbc, dtr_g, dtb, alog, dskx, states, dyn, y, pm, r_n, norm_g, dims, after=going)
    dproj, d_conv_w, d_conv_b = _conv_bwd(pm, di, conv_w, conv_b, dxbc, dproj)
    ddt = jnp.transpose(ddt_g[:, :, :r], (1, 0, 2)).reshape(l, h)
    gwt_in, going = [], ()
    for q in range(2):
        gq = _mm_tn_packed(f"in_proj_bwd_weight_{q}", dproj, h0, d // 2, rows=nm + h, b_blk=q, after=going)
        gq = _mm_tn_packed(f"in_proj_bwd_weight_dt_{q}", ddt, h0, d // 2, into=gq, b_blk=q)
        going = later.send(f"ssm_w_in_{q}", gq)
        gwt_in.append(gq)
    gwt_in = jnp.concatenate(gwt_in, axis=0)
    dh0 = _mm_nn_halves("in_proj_bwd_data_dt", ddt, wt_in, nm, after=going)
    dh0 = _mm_nn_halves("in_proj_bwd_data", dproj, wt_in, 0, add=dh0)
    grad_x, _, d_ln0 = _rms_bwd("rms0_bwd", dh0, x, r0, ln_g[0:1], dx1)

    def heads(v):
        return v[:, 0, :r].reshape(1, h)

    small = dict(ln_g=jnp.concatenate([d_ln0, d_ln1], axis=0), final_g=d_final_g, conv_w=d_conv_w, conv_b=d_conv_b,
                 dt_bias=heads(dbias_g), a_log=heads(dalog_g), d_skip=heads(dd_g), norm_g=d_norm_g, scale=d_scale)
    big = dict(ssm_w_in=gwt_in, ssm_w_out=gw_out, pool_w_in=gw_pin, pool_w_group=gw_g, pool_w_out=gw_pout)
    return loss, grad_x, small, big


SMALL_ORDER = ("ln_g", "final_g", "conv_w", "conv_b", "dt_bias", "a_log", "d_skip", "norm_g", "scale", "loss")


def _flatten_small(parts):
    flat = jnp.concatenate([parts[k].reshape(-1) for k in SMALL_ORDER])
    n = flat.shape[0]
    rows = -(-n // 1024) * 8
    return jnp.pad(flat, (0, rows * 128 - n)).reshape(rows, 128)


def _split_small(flat, shapes):
    flat = flat.reshape(-1)
    out, off = {}, 0
    for k in SMALL_ORDER:
        size = int(np.prod(shapes[k]))
        out[k] = flat[off:off + size].reshape(shapes[k])
        off += size
    return out


def kernel(x, ln_g, final_g, ssm_w_in, ssm_conv_w, ssm_conv_b, ssm_dt_bias, ssm_a_log, ssm_d, ssm_norm_g, ssm_w_out, pool_w_in, pool_w_group, pool_scale, pool_w_out, loss_target, m_ln_g, m_final_g, m_ssm_w_in, m_ssm_conv_w, m_ssm_conv_b, m_ssm_dt_bias, m_ssm_a_log, m_ssm_d, m_ssm_norm_g, m_ssm_w_out, m_pool_w_in, m_pool_w_group, m_pool_scale, m_pool_w_out, v_ln_g, v_final_g, v_ssm_w_in, v_ssm_conv_w, v_ssm_conv_b, v_ssm_dt_bias, v_ssm_a_log, v_ssm_d, v_ssm_norm_g, v_ssm_w_out, v_pool_w_in, v_pool_w_group, v_pool_scale, v_pool_w_out):
    l, d = x.shape[1], x.shape[2]
    me = 4 * lax.axis_index("x") + 2 * lax.axis_index("y") + lax.axis_index("c")
    ng, gds, gd = pool_w_group.shape[1], pool_w_group.shape[2], pool_w_group.shape[3]
    sin_s = ssm_w_in.shape[2]
    dp_s = pool_w_out.shape[1]
    conv_s = ssm_conv_w.shape[2]

    wt_in_s = _pack_rows("pack_w_in", jnp.transpose(ssm_w_in[0]))
    w_out_s = _cast_bf16("cast_w_out", ssm_w_out[0])
    wp_in_s = _cast_bf16("cast_pool_w_in", pool_w_in[0])
    wg_s = _cast_bf16("cast_pool_w_group", pool_w_group[0].reshape(ng * gds, gd))
    wp_out_s = _cast_bf16("cast_pool_w_out", pool_w_out[0])
    small_s = jnp.concatenate([ssm_conv_w[0].reshape(-1), pool_scale[0]])
    n_small = small_s.shape[0]
    small_s = jnp.pad(small_s, (0, -(-n_small // 1024) * 1024 - n_small)).reshape(-1, 128)
    hosted = dict(rms0=_Gather([wt_in_s[0], wt_in_s[1], small_s]), in_proj=_Gather([w_out_s, wg_s]),
                  ssd=_Gather([wp_in_s]), pool_in_proj=_Gather([wp_out_s]))

    def first_of(arrived_0):
        wt_in_g0, wt_in_g1, small_g = arrived_0
        wt_in = _unpack_rows("unpack_w_in", wt_in_g0.reshape(N_DEV * sin_s, d // 4),
                             wt_in_g1.reshape(N_DEV * sin_s, d // 4))
        small_all = small_g.reshape(N_DEV, -1)[:, :n_small]
        conv_w = jnp.transpose(small_all[:, :CONV_TAPS * conv_s].reshape(N_DEV, CONV_TAPS, conv_s), (1, 0, 2))
        return wt_in, conv_w.reshape(CONV_TAPS, -1), small_all[:, CONV_TAPS * conv_s:].reshape(1, -1)

    def weights_of(arrived_i, arrived_s):
        wg = jnp.transpose(arrived_i[1].reshape(N_DEV, ng, gds, gd), (1, 0, 2, 3)).reshape(ng, gd, gd)
        return arrived_i[0].reshape(-1, d), wg, arrived_s[0]

    def rows_major(gp):
        q, _, hw = gp.shape
        return jnp.transpose(gp.reshape(q, N_DEV, -1, hw), (1, 0, 2, 3))

    to_blocks = dict(ssm_w_in_0=rows_major, ssm_w_in_1=rows_major, ssm_w_out=rows_major, pool_w_out=rows_major,
                     pool_w_in=lambda gp: gp[:, None],
                     pool_w_group=lambda gp: jnp.transpose(gp.reshape(ng, N_DEV, gds, gd // 2), (1, 0, 2, 3)))
    travelling = {}

    def send(name, gp):
        *travelling[name], token = _scatter_start(f"scatter_{name}_start", to_blocks[name](gp))
        return (token,)

    later = _Later(hosted, first_of, weights_of, lambda arrived_p: arrived_p[0].reshape(-1, d), send)

    loss, grad_x, small, _ = _local_step(
        x[0], loss_target[0], ln_g, final_g.reshape(1, d), ssm_conv_b, ssm_dt_bias, ssm_a_log, ssm_d, ssm_norm_g, later)

    def arrived(name, after):
        src, land = _scatter_wait(f"scatter_{name}_wait", *travelling[name], after=after)
        own = lax.dynamic_slice_in_dim(src, me, 1, axis=0)
        return lax.dynamic_update_slice_in_dim(land, own, me, axis=0)

    def reduced(name, rv):
        q = rv.shape[1]
        cols = [_reduce_packed(f"{name}_{j}", rv[:, j]) for j in range(q)]
        return cols[0] if q == 1 else jnp.concatenate(cols, axis=1)

    grads = {}
    grads["pool_w_out"] = reduced("reduce_pool_w_out", arrived("pool_w_out", (grad_x,)))[None]
    grads["pool_w_group"] = _reduce_packed("reduce_pool_w_group", arrived("pool_w_group", (grad_x,)).reshape(
        N_DEV, ng * gds, gd // 2)).reshape(1, ng, gds, gd)
    grads["pool_w_in"] = reduced("reduce_pool_w_in", arrived("pool_w_in", (grad_x,)))[None]
    grads["ssm_w_out"] = reduced("reduce_w_out", arrived("ssm_w_out", (grad_x,)))[None]

    weights = dict(ln_g=ln_g, final_g=final_g, ssm_w_in=ssm_w_in, ssm_conv_w=ssm_conv_w, ssm_conv_b=ssm_conv_b,
                   ssm_dt_bias=ssm_dt_bias, ssm_a_log=ssm_a_log, ssm_d=ssm_d, ssm_norm_g=ssm_norm_g,
                   ssm_w_out=ssm_w_out, pool_w_in=pool_w_in, pool_w_group=pool_w_group, pool_scale=pool_scale,
                   pool_w_out=pool_w_out)
    m_in = dict(ln_g=m_ln_g, final_g=m_final_g, ssm_w_in=m_ssm_w_in, ssm_conv_w=m_ssm_conv_w, ssm_conv_b=m_ssm_conv_b,
                ssm_dt_bias=m_ssm_dt_bias, ssm_a_log=m_ssm_a_log, ssm_d=m_ssm_d, ssm_norm_g=m_ssm_norm_g,
                ssm_w_out=m_ssm_w_out, pool_w_in=m_pool_w_in, pool_w_group=m_pool_w_group, pool_scale=m_pool_scale,
                pool_w_out=m_pool_w_out)
    v_in = dict(ln_g=v_ln_g, final_g=v_final_g, ssm_w_in=v_ssm_w_in, ssm_conv_w=v_ssm_conv_w, ssm_conv_b=v_ssm_conv_b,
                ssm_dt_bias=v_ssm_dt_bias, ssm_a_log=v_ssm_a_log, ssm_d=v_ssm_d, ssm_norm_g=v_ssm_norm_g,
                ssm_w_out=v_ssm_w_out, pool_w_in=v_pool_w_in, pool_w_group=v_pool_w_group, pool_scale=v_pool_scale,
                pool_w_out=v_pool_w_out)
    names = list(weights)
    big_names = ("ssm_w_out", "pool_w_in", "pool_w_group", "pool_w_out", "ssm_w_in")
    delta, new_m, new_v = {}, {}, {}

    def adamw_big(k):
        shp = weights[k].shape
        two_d = (-1, shp[-1])
        dk, mk, vk = _adamw(f"adamw_{k}", weights[k].reshape(two_d), grads[k].reshape(two_d), m_in[k].reshape(two_d),
                            v_in[k].reshape(two_d))
        delta[k], new_m[k], new_v[k] = dk.reshape(shp), mk.reshape(shp), vk.reshape(shp)

    for k in big_names[:-1]:
        adamw_big(k)
    small_names = [k for k in names if k not in big_names]

    small["loss"] = loss
    shapes = {k: small[k].shape for k in SMALL_ORDER}
    gathered_small, = _exchange_alone("all_gather_small_grads", _Gather([_flatten_small(small)]),
                                      after=tuple(delta[k] for k in big_names[:-1]))
    summed = _split_small(_sum_slots("sum_small_grads", gathered_small), shapes)
    grads.update(ln_g=summed["ln_g"], final_g=summed["final_g"].reshape(d), ssm_conv_b=summed["conv_b"],
                 ssm_conv_w=lax.dynamic_slice_in_dim(summed["conv_w"], me * conv_s, conv_s, axis=1)[None],
                 ssm_dt_bias=summed["dt_bias"], ssm_a_log=summed["a_log"], ssm_d=summed["d_skip"],
                 ssm_norm_g=summed["norm_g"],
                 pool_scale=lax.dynamic_slice_in_dim(summed["scale"], me * dp_s, dp_s, axis=1))

    def packed(tree):
        flat = jnp.concatenate([tree[k].reshape(-1) for k in small_names])
        nn = flat.shape[0]
        return jnp.pad(flat, (0, -(-nn // 1024) * 1024 - nn), constant_values=1.0).reshape(-1, 128)

    ds, ms, vs = _adamw("adamw_small", packed(weights), packed(grads), packed(m_in), packed(v_in))
    off = 0
    for k in small_names:
        shp = weights[k].shape
        size = int(np.prod(shp))
        for res, arr in ((delta, ds), (new_m, ms), (new_v, vs)):
            res[k] = arr.reshape(-1)[off:off + size].reshape(shp)
        off += size

    k = "ssm_w_in"
    gt = jnp.concatenate([reduced(f"reduce_w_in_{q}", arrived(f"ssm_w_in_{q}", (ds,))) for q in range(2)], axis=1)
    dk, mk, vk = _adamw("adamw_ssm_w_in", jnp.transpose(weights[k][0]), gt, jnp.transpose(m_in[k][0]),
                        jnp.transpose(v_in[k][0]))
    grads[k], delta[k], new_m[k], new_v[k] = (jnp.transpose(t)[None] for t in (gt, dk, mk, vk))

    return (summed["loss"].reshape(()), grad_x[None], *[grads[k] for k in names], *[delta[k] for k in names],
            *[new_m[k] for k in names], *[new_v[k] for k in names])
```

```python
import math

import jax
import jax.numpy as jnp
import numpy as np
from jax import lax
from jax.experimental import pallas as pl
from jax.experimental.pallas import tpu as pltpu

F32 = jnp.float32
BF = jnp.bfloat16
U32 = jnp.uint32

N_DEV = 8
EPS = 1e-6
SSD_CHUNK = 64
SSM_STATE = 128
CONV_TAPS = 4
HEAD_LANES = 128
POOL_WINDOWS = (2, 4, 8, 16)
ADAM_LR, ADAM_B1, ADAM_B2, ADAM_EPS, ADAM_WD, ADAM_STEP = 0.001, 0.9, 0.999, 1e-08, 0.01, 10
VMEM_LIMIT = 56 * 1024 * 1024
NEG_BIG = -1e30
HAND_ON_AT = 6

NN = ((1,), (0,))
NT = ((1,), (1,))
TN = ((0,), (0,))
MESH = pl.DeviceIdType.MESH


def _t(dim, pref):
    return pref if dim % pref == 0 else dim


def _row_tile(rows, pref, mult=8):
    best = rows
    for cand in range(mult, min(rows, pref) + 1, mult):
        if rows % cand == 0:
            best = cand
    return best


def _params(sem=None):
    return pltpu.CompilerParams(dimension_semantics=sem, vmem_limit_bytes=VMEM_LIMIT)


def _silu(x):
    return x * (1.0 / (1.0 + jnp.exp(-x)))


def _dsilu(x):
    s = 1.0 / (1.0 + jnp.exp(-x))
    return s * (1.0 + x * (1.0 - s))


def _bdot(a, b, dims=NN):
    return lax.dot_general(a.astype(BF), b.astype(BF), (dims, ((), ())), preferred_element_type=F32)


def _pack_pairs(x):
    h = x.shape[1] // 2
    hi = lax.bitcast_convert_type(x[:, :h].astype(jnp.bfloat16).astype(F32), U32)
    lo = lax.bitcast_convert_type(x[:, h:].astype(jnp.bfloat16).astype(F32), U32)
    return lax.bitcast_convert_type(hi | (lo >> 16), F32)


def _unpack_pairs(w):
    u = lax.bitcast_convert_type(w, U32)
    hi = lax.bitcast_convert_type(u & jnp.uint32(0xFFFF0000), F32)
    lo = lax.bitcast_convert_type(u << 16, F32)
    return hi, lo


def _mesh_pos():
    return lax.axis_index("x"), lax.axis_index("y"), lax.axis_index("c")


def _slot(pos):
    return 4 * pos[0] + 2 * pos[1] + pos[2]


class _Gather:
    def __init__(self, arrays):
        self.arrays = list(arrays)
        self.out_shapes = [jax.ShapeDtypeStruct((N_DEV, *s.shape), s.dtype) for s in arrays]

    def phases(self, src, dst, send_sems, recv_sems, local_sems):
        n_arr = len(self.arrays)
        x, y, c = _mesh_pos()
        me, sibling = (x, y, c), (x, y, 1 - c)
        chips = [(1 - x, y), (x, 1 - y), (1 - x, 1 - y)]

        def copy(a, k, block, to, from_src):
            return pltpu.make_async_remote_copy(
                src_ref=src[a] if from_src else dst[a].at[_slot(block)], dst_ref=dst[a].at[_slot(block)],
                send_sem=send_sems.at[a * 7 + k], recv_sem=recv_sems.at[a * 7 + k], device_id=to, device_id_type=MESH)

        def mine(a):
            return pltpu.make_async_copy(src[a], dst[a].at[_slot(me)], local_sems.at[a])

        def first(a):
            return [copy(a, 0, me, sibling, True)] + [copy(a, 1 + j, me, (*chip, c), True)
                                                     for j, chip in enumerate(chips)]

        def start():
            for a in range(n_arr):
                mine(a).start()
                for cp in first(a):
                    cp.start()

        def middle():
            for j, chip in enumerate(chips):
                for a in range(n_arr):
                    copy(a, 1 + j, (*chip, c), me, False).wait_recv()
                    copy(a, 4 + j, (*chip, c), sibling, False).start()

        def finish():
            for a in range(n_arr):
                copy(a, 0, sibling, me, False).wait_recv()
                for j, chip in enumerate(chips):
                    copy(a, 4 + j, (*chip, 1 - c), me, False).wait_recv()
                for cp in first(a):
                    cp.wait_send()
                for j, chip in enumerate(chips):
                    copy(a, 4 + j, (*chip, c), sibling, False).wait_send()
                mine(a).wait()

        return start, middle, finish


def _hosted_call(name, body, grid, in_specs, out_specs, out_shape, scratch_shapes, sem, operands, exch=None,
                 after=(), into=None):
    if into is not None:
        n_lead = len(in_specs)
        inner = body

        def body(*refs):
            inner(*refs[:n_lead], *refs[n_lead + 1:])

        return pl.pallas_call(
            body, grid=grid, in_specs=[*in_specs, pl.BlockSpec(memory_space=pl.ANY)], out_specs=out_specs,
            out_shape=out_shape, scratch_shapes=scratch_shapes, input_output_aliases={n_lead: 0},
            compiler_params=_params(sem), name=name)(*operands, into)
    if after:
        n_lead = len(in_specs)
        inner = body

        def body(*refs):
            inner(*refs[:n_lead], *refs[n_lead + len(after):])

        in_specs = [*in_specs, *[pl.BlockSpec(memory_space=pl.ANY)] * len(after)]
        operands = (*operands, *after)
    if exch is None:
        return pl.pallas_call(body, grid=grid, in_specs=in_specs, out_specs=out_specs, out_shape=out_shape,
                              scratch_shapes=scratch_shapes, compiler_params=_params(sem), name=name)(*operands)
    n_in, n_out, n_scr, ne = len(in_specs), len(out_specs), len(scratch_shapes), len(exch.arrays)
    total = math.prod(grid)

    def wrapped(*refs):
        ins, ex_in = refs[:n_in], refs[n_in:n_in + ne]
        outs = refs[n_in + ne:n_in + ne + n_out]
        ex_out = refs[n_in + ne + n_out:n_in + 2 * ne + n_out]
        scr = refs[n_in + 2 * ne + n_out:n_in + 2 * ne + n_out + n_scr]
        step = 0
        for axis, size in enumerate(grid):
            step = step * size + pl.program_id(axis)
        start, middle, finish = exch.phases(ex_in, ex_out, *refs[-3:])
        pl.when(step == 0)(start)
        if middle is not None:
            pl.when(step == (total * HAND_ON_AT) // 8)(middle)
        body(*ins, *outs, *scr)
        pl.when(step == total - 1)(finish)

    hbm = pl.BlockSpec(memory_space=pl.ANY)
    sems = [pltpu.SemaphoreType.DMA((ne * 7,)), pltpu.SemaphoreType.DMA((ne * 7,)), pltpu.SemaphoreType.DMA((ne,))]
    return pl.pallas_call(
        wrapped, grid=grid, in_specs=[*in_specs, *[hbm] * ne], out_specs=[*out_specs, *[hbm] * ne],
        out_shape=[*out_shape, *exch.out_shapes], scratch_shapes=[*scratch_shapes, *sems],
        compiler_params=pltpu.CompilerParams(dimension_semantics=("arbitrary",) * len(grid),
                                             vmem_limit_bytes=VMEM_LIMIT, has_side_effects=True),
        name=name)(*operands, *exch.arrays)


def _flip_peers():
    x, y, c = _mesh_pos()
    peers = []
    for k in range(1, N_DEV):
        fx, fy, fc = (k >> 2) & 1, (k >> 1) & 1, k & 1
        peers.append((1 - x if fx else x, 1 - y if fy else y, 1 - c if fc else c))
    return (x, y, c), peers


def _split_scatter_copy(src, land, send_sems, recv_sems, k, me, peer, sending):
    return pltpu.make_async_remote_copy(
        src_ref=src.at[_slot(peer)], dst_ref=land.at[_slot(me) if sending else _slot(peer)],
        send_sem=send_sems.at[k], recv_sem=recv_sems.at[k], device_id=peer, device_id_type=MESH)


def _scatter_start(name, blocks):
    def body(src, land, send_sems, recv_sems, src_thru, land_thru, token):
        me, peers = _flip_peers()
        for k, peer in enumerate(peers):
            _split_scatter_copy(src, land, send_sems, recv_sems, k, me, peer, True).start()
        token[...] = jnp.zeros_like(token)

    hbm = pl.BlockSpec(memory_space=pltpu.HBM)
    sem = pl.BlockSpec(memory_space=pltpu.SEMAPHORE)
    return pl.pallas_call(
        body, name=name,
        out_shape=(pltpu.SemaphoreType.DMA((N_DEV - 1,)), pltpu.SemaphoreType.DMA((N_DEV - 1,)),
                   pltpu.HBM(blocks.shape, blocks.dtype), pltpu.HBM(blocks.shape, blocks.dtype),
                   jax.ShapeDtypeStruct((8, 128), F32)),
        in_specs=(hbm, hbm), out_specs=(sem, sem, hbm, hbm, pl.BlockSpec(memory_space=pltpu.VMEM)),
        input_output_aliases={0: 2, 1: 3},
        compiler_params=pltpu.CompilerParams(has_side_effects=pltpu.SideEffectType.DATAFLOW_SIDE_EFFECTING),
    )(pltpu.with_memory_space_constraint(blocks, pltpu.HBM),
      pltpu.with_memory_space_constraint(lax.empty(blocks.shape, blocks.dtype), pltpu.HBM))


def _scatter_wait(name, send_sems, recv_sems, src_thru, land_thru, after):
    def body(src, land, send_sems, recv_sems, *rest):
        me, peers = _flip_peers()
        for k, peer in enumerate(peers):
            _split_scatter_copy(src, land, send_sems, recv_sems, k, me, peer, True).wait_send()
            _split_scatter_copy(src, land, send_sems, recv_sems, k, me, peer, False).wait_recv()

    hbm = pl.BlockSpec(memory_space=pltpu.HBM)
    sem = pl.BlockSpec(memory_space=pltpu.SEMAPHORE)
    return pl.pallas_call(
        body, name=name,
        out_shape=(pltpu.HBM(src_thru.shape, src_thru.dtype), pltpu.HBM(land_thru.shape, land_thru.dtype)),
        in_specs=(hbm, hbm, sem, sem, *[pl.BlockSpec(memory_space=pl.ANY)] * len(after)), out_specs=(hbm, hbm),
        input_output_aliases={0: 0, 1: 1},
        compiler_params=pltpu.CompilerParams(has_side_effects=pltpu.SideEffectType.DATAFLOW_SIDE_EFFECTING),
    )(src_thru, land_thru, send_sems, recv_sems, *after)


def _exchange_alone(name, exch, after=()):
    def body():
        pass

    return _hosted_call(name, body, (1,), [], [], [], [], None, (), exch, after)


def _mm(name, grid, a, a_spec, b, b_spec, dims, outs, o_specs, acc_shape, extra=(), extra_specs=(), epi=None,
        exch=None, after=(), into=None, summed=None):
    nk = grid[-1]
    n_extra, n_out = len(extra), len(outs)

    def body(*refs):
        a_ref, b_ref = refs[0], refs[1]
        ex = refs[2:2 + n_extra]
        o_refs = refs[2 + n_extra:2 + n_extra + n_out]

        def write(res):
            res = (res,) if epi is None else epi(res, *[e[...] for e in ex])
            for idx, (o, r) in enumerate(zip(o_refs, res)):
                if summed is not None and idx == summed[0]:
                    first = pl.program_id(summed[1]) == 0

                    @pl.when(first)
                    def _(o=o, r=r):
                        o[...] = r.astype(o.dtype)

                    @pl.when(jnp.logical_not(first))
                    def _(o=o, r=r):
                        o[...] += r.astype(o.dtype)
                else:
                    o[...] = r.astype(o.dtype)

        if nk == 1:
            write(_bdot(a_ref[...], b_ref[...], dims))
            return
        acc = refs[-1]
        k = pl.program_id(len(grid) - 1)

        @pl.when(k == 0)
        def _():
            acc[...] = _bdot(a_ref[...], b_ref[...], dims)

        @pl.when(jnp.logical_and(k > 0, k < nk - 1))
        def _():
            acc[...] += _bdot(a_ref[...], b_ref[...], dims)

        @pl.when(k == nk - 1)
        def _():
            write(acc[...] + _bdot(a_ref[...], b_ref[...], dims))

    sem = ("parallel",) * (len(grid) - 1) + ("arbitrary",)
    if summed is not None:
        sem = ("arbitrary",) * len(grid)
    scratch = [] if nk == 1 else [pltpu.VMEM(acc_shape, F32)]
    return _hosted_call(name, body, grid, [a_spec, b_spec, *extra_specs], list(o_specs), list(outs),
                        scratch, sem, (a, b, *extra), exch, after, into)


def _add_epi(acc, add):
    return (acc + add,)


def _pack_epi(acc):
    return (_pack_pairs(acc),)


def _mm_nt(name, a, b, n_off, n, out_dtype=F32, tm=1024, tn=512, exch=None, after=()):
    m, kk = a.shape
    tm, tn = _t(m, tm), math.gcd(_t(n, tn), n_off)
    res = _mm(name, (m // tm, n // tn, 1),
              a, pl.BlockSpec((tm, kk), lambda i, j, k: (i, 0)),
              b, pl.BlockSpec((tn, kk), lambda i, j, k: (n_off // tn + j, 0)), NT,
              [jax.ShapeDtypeStruct((m, n), out_dtype)], [pl.BlockSpec((tm, tn), lambda i, j, k: (i, j))], (tm, tn),
              exch=exch, after=after)
    return res[0] if exch is None else res


def _mm_nt_halves(name, a, b2, n_off, n, tm=1024, tn=512, exch=None):
    m = a.shape[0]
    kh = b2.shape[2]
    tm, tn = _t(m, tm), math.gcd(_t(n, tn), n_off)

    def body(a_ref, b_ref, o_ref):
        o_ref[...] = _bdot(a_ref[:, :kh], b_ref[0], NT) + _bdot(a_ref[:, kh:], b_ref[1], NT)

    res = _hosted_call(name, body, (m // tm, n // tn),
                       [pl.BlockSpec((tm, 2 * kh), lambda i, j: (i, 0)),
                        pl.BlockSpec((2, tn, kh), lambda i, j: (0, n_off // tn + j, 0))],
                       [pl.BlockSpec((tm, tn), lambda i, j: (i, j))], [jax.ShapeDtypeStruct((m, n), F32)], [],
                       ("parallel", "parallel"), (a, b2), exch)
    return res[0] if exch is None else res


def _mm_nn_halves(name, a, b2, k_off=0, add=None, after=(), tm=1024, tk=1024):
    m, kk = a.shape
    half = b2.shape[2]
    tm, tk = _t(m, tm), math.gcd(_t(kk, tk), k_off)
    extra, especs, epi = (), (), None
    if add is not None:
        extra, especs, epi = (add,), (pl.BlockSpec((tm, half), lambda i, j, k: (i, j)),), _add_epi
    return _mm(name, (m // tm, 2, kk // tk),
               a, pl.BlockSpec((tm, tk), lambda i, j, k: (i, k)),
               b2, pl.BlockSpec((None, tk, half), lambda i, j, k: (j, k_off // tk + k, 0)), NN,
               [jax.ShapeDtypeStruct((m, 2 * half), F32)], [pl.BlockSpec((tm, half), lambda i, j, k: (i, j))],
               (tm, half), extra, especs, epi, after=after)[0]


def _mm_nn_blocked(name, a, b3, tm=1024, tn=512, exch=None):
    m, kk = a.shape
    nb, _, cs = b3.shape
    tm, tn = _t(m, tm), _t(cs, tn)
    per = cs // tn
    res = _mm(name, (m // tm, nb * per, 1),
              a, pl.BlockSpec((tm, kk), lambda i, j, k: (i, 0)),
              b3, pl.BlockSpec((None, kk, tn), lambda i, j, k: (j // per, 0, j % per)), NN,
              [jax.ShapeDtypeStruct((m, nb * cs), F32)], [pl.BlockSpec((tm, tn), lambda i, j, k: (i, j))],
              (tm, tn), exch=exch)
    return res[0] if exch is None else res


def _mm_nt_blocked(name, a, b3, after=(), tm=1024, tn=1024):
    m = a.shape[0]
    nb, n, cs = b3.shape
    tm, tn = _t(m, tm), _t(n, tn)
    return _mm(name, (m // tm, n // tn, nb),
               a, pl.BlockSpec((tm, cs), lambda i, j, k: (i, k)),
               b3, pl.BlockSpec((None, tn, cs), lambda i, j, k: (k, j, 0)), NT,
               [jax.ShapeDtypeStruct((m, n), F32)], [pl.BlockSpec((tm, tn), lambda i, j, k: (i, j))], (tm, tn),
               after=after)[0]


def _mm_nn(name, a, b, k_off=0, kk=None, add=None, out_dtype=F32, tm=1024, tn=1024, tk=1024, after=()):
    m = a.shape[0]
    kk = a.shape[1] if kk is None else kk
    n = b.shape[1]
    tm, tn, tk = _t(m, tm), _t(n, tn), math.gcd(_t(kk, tk), k_off)
    extra, especs, epi = (), (), None
    if add is not None:
        extra, especs, epi = (add,), (pl.BlockSpec((tm, tn), lambda i, j, k: (i, j)),), _add_epi
    return _mm(name, (m // tm, n // tn, kk // tk),
               a, pl.BlockSpec((tm, tk), lambda i, j, k: (i, k)),
               b, pl.BlockSpec((tk, tn), lambda i, j, k: (k_off // tk + k, j)), NN,
               [jax.ShapeDtypeStruct((m, n), out_dtype)], [pl.BlockSpec((tm, tn), lambda i, j, k: (i, j))], (tm, tn),
               extra, especs, epi, after=after)[0]


TN_ACC_ELEMENTS = 1 << 20


def _mm_tn_packed(name, a, b, pw, tk=1024, rows=None, into=None, b_blk=None, after=()):
    kk, m = a.shape
    nq = b.shape[1] // pw if b_blk is None else 1
    first = 0 if b_blk is None else b_blk
    tm, tk = _t(m, TN_ACC_ELEMENTS // pw), _t(kk, tk)
    rows = m if rows is None else rows
    row_blk = 0
    if into is not None:
        rows = into.shape[1]
        assert (rows - m) % tm == 0
        row_blk = (rows - m) // tm
    return _mm(name, (m // tm, nq, kk // tk),
               a, pl.BlockSpec((tk, tm), lambda i, j, k: (k, i)),
               b, pl.BlockSpec((tk, pw), lambda i, j, k: (k, first + j)), TN,
               [jax.ShapeDtypeStruct((nq, rows, pw // 2), F32)],
               [pl.BlockSpec((None, tm, pw // 2), lambda i, j, k: (j, row_blk + i, 0))], (tm, pw), epi=_pack_epi,
               into=into, after=after)[0]


def _rms_fwd(name, x, g, tm=256, exch=None):
    l, d = x.shape
    tm = _t(l, tm)

    def body(x_ref, g_ref, h_ref, r_ref):
        xv = x_ref[...]
        r = lax.rsqrt(jnp.mean(xv * xv, axis=-1, keepdims=True) + EPS)
        h_ref[...] = (xv * r * g_ref[...]).astype(BF)
        r_ref[...] = r

    return _hosted_call(
        name, body, (l // tm,),
        [pl.BlockSpec((tm, d), lambda i: (i, 0)), pl.BlockSpec((1, d), lambda i: (0, 0))],
        [pl.BlockSpec((tm, d), lambda i: (i, 0)), pl.BlockSpec((tm, 1), lambda i: (i, 0))],
        [jax.ShapeDtypeStruct((l, d), BF), jax.ShapeDtypeStruct((l, 1), F32)], [], ("parallel",), (x, g), exch)


def _rms_bwd(name, dh, x, r, g, dres, tm=256):
    l, d = x.shape
    tm = _t(l, tm)

    def body(dh_ref, x_ref, r_ref, g_ref, dres_ref, dx_ref, dxb_ref, dg_ref):
        i = pl.program_id(0)
        rr = r_ref[...]
        xhat = x_ref[...] * rr
        dhv = dh_ref[...]
        dxh = dhv * g_ref[...]
        dx = dres_ref[...] + rr * (dxh - xhat * jnp.mean(dxh * xhat, axis=-1, keepdims=True))
        dx_ref[...] = dx
        dxb_ref[...] = dx.astype(BF)

        @pl.when(i == 0)
        def _():
            dg_ref[...] = jnp.zeros_like(dg_ref)

        dg_ref[...] += jnp.sum(dhv * xhat, axis=0, keepdims=True)

    row = pl.BlockSpec((tm, d), lambda i: (i, 0))
    vec = pl.BlockSpec((1, d), lambda i: (0, 0))
    return pl.pallas_call(
        body, grid=(l // tm,), in_specs=[row, row, pl.BlockSpec((tm, 1), lambda i: (i, 0)), vec, row],
        out_specs=[row, row, vec],
        out_shape=[jax.ShapeDtypeStruct((l, d), F32), jax.ShapeDtypeStruct((l, d), BF),
                   jax.ShapeDtypeStruct((1, d), F32)],
        compiler_params=_params(("arbitrary",)), name=name)(dh, x, r, g, dres)


def _loss_head(x, g, target, tm=256):
    l, d = x.shape
    tm = _t(l, tm)

    def body(x_ref, g_ref, t_ref, dx_ref, dxb_ref, dg_ref, loss_ref):
        i = pl.program_id(0)
        xv = x_ref[...]
        gv = g_ref[...]
        r = lax.rsqrt(jnp.mean(xv * xv, axis=-1, keepdims=True) + EPS)
        xhat = xv * r
        e = xhat * gv - t_ref[...]
        dy = e * (1.0 / d)
        dxh = dy * gv
        dx = r * (dxh - xhat * jnp.mean(dxh * xhat, axis=-1, keepdims=True))
        dx_ref[...] = dx
        dxb_ref[...] = dx.astype(BF)

        @pl.when(i == 0)
        def _():
            dg_ref[...] = jnp.zeros_like(dg_ref)
            loss_ref[...] = jnp.zeros_like(loss_ref)

        dg_ref[...] += jnp.sum(dy * xhat, axis=0, keepdims=True)
        loss_ref[...] += 0.5 * jnp.sum(jnp.sum(e * e, axis=-1, keepdims=True) * (1.0 / d), axis=0, keepdims=True)

    row = pl.BlockSpec((tm, d), lambda i: (i, 0))
    vec = pl.BlockSpec((1, d), lambda i: (0, 0))
    return pl.pallas_call(
        body, grid=(l // tm,), in_specs=[row, vec, row],
        out_specs=[row, row, vec, pl.BlockSpec((1, 1), lambda i: (0, 0))],
        out_shape=[jax.ShapeDtypeStruct((l, d), F32), jax.ShapeDtypeStruct((l, d), BF),
                   jax.ShapeDtypeStruct((1, d), F32), jax.ShapeDtypeStruct((1, 1), F32)],
        compiler_params=_params(("arbitrary",)), name="loss_head")(x, g, target)


CONV_HALO = 8


def _conv_pre(x_ref, w_ref, b_ref, i, tr):
    r0 = pl.multiple_of(i * tr, tr)
    cur = x_ref[pl.ds(r0, tr), :]
    prev = x_ref[pl.ds(pl.multiple_of(jnp.maximum(r0 - CONV_HALO, 0), CONV_HALO), CONV_HALO), :]
    prev = jnp.where(i > 0, prev, 0.0)
    ext = jnp.concatenate([prev, cur], axis=0)
    taps = []
    for k in range(CONV_TAPS):
        s = CONV_TAPS - 1 - k
        taps.append(cur if s == 0 else pltpu.roll(ext, s, 0)[CONV_HALO:])
    pre = b_ref[...] + sum(w_ref[k:k + 1, :] * taps[k] for k in range(CONV_TAPS))
    return r0, pre, taps


def _conv_fwd(pm, col_off, conv_w, conv_b, cw=256, tr=512):
    l = pm.shape[0]
    c = conv_w.shape[1]
    cw, tr = _t(c, cw), _t(l, tr)
    assert col_off % cw == 0

    def body(x_ref, w_ref, b_ref, o_ref):
        def step(i, carry):
            r0, pre, _ = _conv_pre(x_ref, w_ref, b_ref, i, tr)
            o_ref[pl.ds(r0, tr), :] = _silu(pre)
            return carry
        lax.fori_loop(0, l // tr, step, 0)

    return pl.pallas_call(
        body, grid=(c // cw,),
        in_specs=[pl.BlockSpec((l, cw), lambda j: (0, col_off // cw + j)),
                  pl.BlockSpec((CONV_TAPS, cw), lambda j: (0, j)), pl.BlockSpec((1, cw), lambda j: (0, j))],
        out_specs=pl.BlockSpec((l, cw), lambda j: (0, j)), out_shape=jax.ShapeDtypeStruct((l, c), F32),
        compiler_params=_params(("parallel",)), name="conv_fwd")(pm, conv_w, conv_b)


def _conv_bwd(pm, col_off, conv_w, conv_b, dy, dproj, cw=256, tr=512):
    l = pm.shape[0]
    c = conv_w.shape[1]
    cw, tr = _t(c, cw), _t(l, tr)
    nt = l // tr

    def body(x_ref, w_ref, b_ref, dy_ref, _, dx_ref, dw_ref, db_ref, dpre_ref):
        def step1(i, carry):
            dws, db = carry
            r0, pre, taps = _conv_pre(x_ref, w_ref, b_ref, i, tr)
            dpre = dy_ref[pl.ds(r0, tr), :] * _dsilu(pre)
            dpre_ref[pl.ds(r0, tr), :] = dpre
            dws = tuple(dws[k] + jnp.sum(dpre * taps[k], axis=0, keepdims=True) for k in range(CONV_TAPS))
            return dws, db + jnp.sum(dpre, axis=0, keepdims=True)

        z = jnp.zeros((1, cw), F32)
        dws, db = lax.fori_loop(0, nt, step1, ((z,) * CONV_TAPS, z))
        for k in range(CONV_TAPS):
            dw_ref[k:k + 1, :] = dws[k]
        db_ref[...] = db

        def step2(i, carry):
            r0 = pl.multiple_of(i * tr, tr)
            cur = dpre_ref[pl.ds(r0, tr), :]
            nxt = dpre_ref[pl.ds(pl.multiple_of(jnp.minimum(r0 + tr, l - CONV_HALO), CONV_HALO), CONV_HALO), :]
            nxt = jnp.where(i < nt - 1, nxt, 0.0)
            ext = jnp.concatenate([cur, nxt], axis=0)
            acc = w_ref[CONV_TAPS - 1:CONV_TAPS, :] * cur
            for k in range(CONV_TAPS - 1):
                s = CONV_TAPS - 1 - k
                acc = acc + w_ref[k:k + 1, :] * pltpu.roll(ext, tr + CONV_HALO - s, 0)[:tr]
            dx_ref[pl.ds(r0, tr), :] = acc.astype(dx_ref.dtype)
            return carry
        lax.fori_loop(0, nt, step2, 0)

    col = pl.BlockSpec((l, cw), lambda j: (0, j))
    shifted = pl.BlockSpec((l, cw), lambda j: (0, col_off // cw + j))
    return pl.pallas_call(
        body, grid=(c // cw,),
        in_specs=[shifted, pl.BlockSpec((CONV_TAPS, cw), lambda j: (0, j)), pl.BlockSpec((1, cw), lambda j: (0, j)),
                  col, pl.BlockSpec(memory_space=pl.ANY)],
        out_specs=[shifted, pl.BlockSpec((CONV_TAPS, cw), lambda j: (0, j)), pl.BlockSpec((1, cw), lambda j: (0, j))],
        out_shape=[jax.ShapeDtypeStruct(dproj.shape, dproj.dtype), jax.ShapeDtypeStruct((CONV_TAPS, c), F32),
                   jax.ShapeDtypeStruct((1, c), F32)],
        scratch_shapes=[pltpu.VMEM((l, cw), F32)], input_output_aliases={4: 0},
        compiler_params=_params(("parallel",)), name="conv_bwd")(pm, conv_w, conv_b, dy, dproj)


def _split(x, pieces):
    out, rest = [], x
    for _ in range(pieces):
        piece = rest.astype(BF)
        out.append(piece)
        rest = rest - piece.astype(F32)
    return out


def _rows_times(xs, m_stack, pieces):
    x = xs[0] if len(xs) == 1 else jnp.concatenate(xs, axis=0)
    out = _bdot(jnp.concatenate(_split(x, pieces), axis=1), m_stack)
    sizes = [v.shape[0] for v in xs]
    offs = np.cumsum([0] + sizes)
    return [out[offs[i]:offs[i + 1]] for i in range(len(xs))]


def _times_rows(m_stack, x, pieces):
    return _bdot(m_stack, jnp.concatenate(_split(x, pieces), axis=0))


EXPAND_PIECES = 3
FOLD_PIECES = 2


def _ssd_consts(r, p, t):
    assert p == t, "heads expand to P lanes of the inputs and to T lanes of the decay matrices alike"
    assert LANES % p == 0 and (r * p) % LANES == 0, "whole heads per lane tile, whole lane tiles per group"
    rp = r * p
    tri = np.tril(np.ones((t, t), np.float32))
    ep = np.zeros((HEAD_LANES, rp), np.float32)
    ep[np.arange(rp) // p, np.arange(rp)] = 1.0
    itile = (np.arange(t)[:, None] == (np.arange(rp) % t)[None, :]).astype(np.float32)
    lmask = (np.arange(t)[:, None] >= (np.arange(rp) % t)[None, :]).astype(np.float32)
    return [jnp.asarray(np.concatenate([ep] * EXPAND_PIECES, axis=0), BF),
            jnp.asarray(np.concatenate([ep.T] * FOLD_PIECES, axis=0), BF),
            jnp.asarray(np.concatenate([tri] * EXPAND_PIECES, axis=1), BF),
            jnp.asarray(np.concatenate([tri.T] * EXPAND_PIECES, axis=1), BF),
            jnp.asarray(itile), jnp.asarray(lmask)]


def _ssd_heads(dtr_ref, dtb_ref, alog_ref, e_stack, tri_stack, g_n, t):
    pre = [dtr_ref[g] + dtb_ref[g] for g in range(g_n)]
    dt = [jnp.maximum(v, 0.0) + jnp.log(1.0 + jnp.exp(-jnp.abs(v))) for v in pre]
    a = [-jnp.exp(alog_ref[g]) for g in range(g_n)]
    adt = jnp.concatenate([dt[g] * a[g] for g in range(g_n)], axis=1)
    cs = _times_rows(tri_stack, adt, EXPAND_PIECES)
    cs = [cs[:, g * HEAD_LANES:(g + 1) * HEAD_LANES] for g in range(g_n)]
    at_lanes = _rows_times([v for g in range(g_n) for v in (dt[g], cs[g])], e_stack, EXPAND_PIECES)
    return [(pre[g], dt[g], a[g], cs[g][t - 1:t, :], at_lanes[2 * g], at_lanes[2 * g + 1]) for g in range(g_n)]


LANES = 128


def _head_lanes(k, p, rows):
    lane = lax.broadcasted_iota(jnp.int32, (rows, LANES), 1)
    j = k % (LANES // p)
    return jnp.logical_and(lane >= j * p, lane < (j + 1) * p)


def _block_diagonal(v, r, p):
    rows = []
    for k in range(r):
        c = k * p // LANES
        tiles = [jnp.zeros((v.shape[0], LANES), v.dtype)] * (r * p // LANES)
        tiles[c] = jnp.where(_head_lanes(k, p, v.shape[0]), v[:, c * LANES:(c + 1) * LANES], jnp.zeros((), v.dtype))
        rows.append(jnp.concatenate(tiles, axis=1))
    return jnp.concatenate(rows, axis=0)


def _diagonal_blocks(m, r, p):
    t = m.shape[0] // r
    tiles = []
    for c in range(r * p // LANES):
        heads = range(c * LANES // p, (c + 1) * LANES // p)
        acc = None
        for k in heads:
            blk = m[k * t:(k + 1) * t, c * LANES:(c + 1) * LANES]
            acc = blk if acc is None else jnp.where(_head_lanes(k, p, t), blk, acc)
        tiles.append(acc)
    return jnp.concatenate(tiles, axis=1)


def _ssd_common(xs, bm, dtx, csx, itile, lmask, r, t):
    ecsx = jnp.exp(csx)
    lastx = csx[t - 1:t, :]
    wx = jnp.exp(lastx - csx)
    elastx = jnp.exp(lastx)
    csrow = jnp.sum(csx * itile, axis=0, keepdims=True)
    lx = jnp.exp(jnp.where(lmask > 0.0, csx - csrow, NEG_BIG))
    xdt = xs * dtx
    xblk = _block_diagonal(xdt.astype(BF), r, t)
    btile = jnp.concatenate([bm.astype(BF)] * r, axis=0)
    return ecsx, wx, elastx, lx, xdt, xblk, btile


def _ssd_specs(l, g_n, r, p, n, t, conv, rev):
    nc = l // t
    rp = r * p
    cidx = (lambda c: nc - 1 - c) if rev else (lambda c: c)
    row_spec = lambda width: pl.BlockSpec((t, width), lambda c: (cidx(c), 0))
    dtr_spec = pl.BlockSpec((g_n, t, HEAD_LANES), lambda c: (0, cidx(c), 0))
    par_spec = pl.BlockSpec((g_n, 1, HEAD_LANES), lambda c: (0, 0, 0))
    dskx_spec = pl.BlockSpec((g_n, 1, rp), lambda c: (0, 0, 0))
    st_spec = pl.BlockSpec((None, g_n, n, rp), lambda c: (cidx(c), 0, 0, 0))
    return nc, row_spec, dtr_spec, par_spec, dskx_spec, st_spec


def _const_specs(consts):
    return [pl.BlockSpec(a.shape, lambda c: (0, 0)) for a in consts]


def _ssd_fwd(xbc, dtr, dtb, alog, dskx, pm, norm_g, dims, exch=None):
    h, g_n, r, p, n, t = dims
    l, conv = xbc.shape
    rp, hp = r * p, h * p
    nc, row_spec, dtr_spec, par_spec, dskx_spec, st_spec = _ssd_specs(l, g_n, r, p, n, t, conv, False)
    consts = _ssd_consts(r, p, t)

    def body(x_ref, dtr_ref, dtb_ref, alog_ref, dskx_ref, z_ref, ng_ref,
             e_ref, et_ref, tri_ref, trit_ref, it_ref, lm_ref,
             y_ref, st_ref, yn_ref, rn_ref, s_ref):
        @pl.when(pl.program_id(0) == 0)
        def _():
            s_ref[...] = jnp.zeros_like(s_ref)

        heads = _ssd_heads(dtr_ref, dtb_ref, alog_ref, e_ref[...], tri_ref[...], g_n, t)
        for g in range(g_n):
            xs = x_ref[:, g * rp:(g + 1) * rp]
            bm = x_ref[:, hp + g * n:hp + (g + 1) * n]
            cm = x_ref[:, hp + (g_n + g) * n:hp + (g_n + g + 1) * n]
            dtx, csx = heads[g][4:]
            ecsx, wx, elastx, lx, xdt, xblk, btile = _ssd_common(xs, bm, dtx, csx, it_ref[...], lm_ref[...], r, t)
            s_in = s_ref[g]
            st_ref[g] = s_in
            cbx = _bdot(cm, btile, NT)
            yd = _bdot(cbx * lx, xblk)
            yo = ecsx * _bdot(cm, s_in)
            y_ref[:, g * rp:(g + 1) * rp] = yd + yo + dskx_ref[g] * xs
            s_ref[g] = elastx * s_in + _bdot(bm, xdt * wx, TN)

        v = y_ref[...] * _silu(z_ref[...])
        rr = lax.rsqrt(jnp.mean(v * v, axis=-1, keepdims=True) + EPS)
        yn_ref[...] = (v * rr * ng_ref[...]).astype(BF)
        rn_ref[...] = rr

    return _hosted_call(
        "ssd_fwd", body, (nc,),
        [row_spec(conv), dtr_spec, par_spec, par_spec, dskx_spec, row_spec(hp),
         pl.BlockSpec((1, hp), lambda c: (0, 0)), *_const_specs(consts)],
        [row_spec(hp), st_spec, row_spec(hp), row_spec(1)],
        [jax.ShapeDtypeStruct((l, hp), F32), jax.ShapeDtypeStruct((nc, g_n, n, rp), F32),
         jax.ShapeDtypeStruct((l, hp), BF), jax.ShapeDtypeStruct((l, 1), F32)],
        [pltpu.VMEM((g_n, n, rp), F32)], ("arbitrary",),
        (xbc, dtr, dtb, alog, dskx, pm, norm_g, *consts), exch)


def _ssd_bwd(xbc, dtr, dtb, alog, dskx, states, dyn, y, pm, rstd, norm_g, dims, after=()):
    h, g_n, r, p, n, t = dims
    l, conv = xbc.shape
    rp, hp = r * p, h * p
    nc, row_spec, dtr_spec, par_spec, dskx_spec, st_spec = _ssd_specs(l, g_n, r, p, n, t, conv, True)
    consts = _ssd_consts(r, p, t)

    def fold_rows(v, rows):
        return sum(v[k * rows:(k + 1) * rows, :] for k in range(r))

    def body(x_ref, dtr_ref, dtb_ref, alog_ref, dskx_ref, st_ref, dyn_ref, y_ref, z_ref, rn_ref, ng_ref,
             e_ref, et_ref, tri_ref, trit_ref, it_ref, lm_ref,
             dx_ref, ddt_ref, dbias_ref, dalog_ref, dd_ref, dz_ref, dng_ref, ds_ref, dy_ref):
        @pl.when(pl.program_id(0) == 0)
        def _():
            ds_ref[...] = jnp.zeros_like(ds_ref)
            dbias_ref[...] = jnp.zeros_like(dbias_ref)
            dalog_ref[...] = jnp.zeros_like(dalog_ref)
            dd_ref[...] = jnp.zeros_like(dd_ref)
            dng_ref[...] = jnp.zeros_like(dng_ref)

        yv, zv, rr, dn = y_ref[...], z_ref[...], rn_ref[...], dyn_ref[...]
        sz = _silu(zv)
        vhat = yv * sz * rr
        dvh = dn * ng_ref[...]
        dv = rr * (dvh - vhat * jnp.mean(dvh * vhat, axis=-1, keepdims=True))
        dy_ref[...] = dv * sz
        dz_ref[...] = (dv * yv * _dsilu(zv)).astype(BF)
        dng_ref[...] += jnp.sum(dn * vhat, axis=0, keepdims=True)

        itile = it_ref[...]
        last_row = lax.broadcasted_iota(jnp.int32, (t, HEAD_LANES), 0) == t - 1
        heads = _ssd_heads(dtr_ref, dtb_ref, alog_ref, e_ref[...], tri_ref[...], g_n, t)
        to_fold = []
        for g in range(g_n):
            xs = x_ref[:, g * rp:(g + 1) * rp]
            bm = x_ref[:, hp + g * n:hp + (g + 1) * n]
            cm = x_ref[:, hp + (g_n + g) * n:hp + (g_n + g + 1) * n]
            dy = dy_ref[:, g * rp:(g + 1) * rp]
            dtx, csx = heads[g][4:]
            ecsx, wx, elastx, lx, xdt, xblk, btile = _ssd_common(xs, bm, dtx, csx, itile, lm_ref[...], r, t)
            s_in = st_ref[g]
            ds_out = ds_ref[g]

            cbx = _bdot(cm, btile, NT)
            amat = cbx * lx
            da = _bdot(dy, xblk, NT)
            dxdt = _diagonal_blocks(_bdot(amat, dy, TN), r, t)
            dcbx = da * lx
            q = da * amat
            dc = _bdot(dcbx, btile)
            db = fold_rows(_bdot(dcbx, cm, TN), t)

            g0 = _bdot(cm, s_in)
            dg0 = dy * ecsx
            dc = dc + _bdot(dg0, s_in, NT)

            z = _bdot(bm, ds_out)
            dxdt = dxdt + z * wx
            db = db + _bdot(xdt * wx, ds_out, NT)
            ds_ref[g] = elastx * ds_out + _bdot(cm, dg0, TN)

            dx_ref[:, g * rp:(g + 1) * rp] = dskx_ref[g] * dy + dxdt * dtx
            dx_ref[:, hp + g * n:hp + (g + 1) * n] = db
            dx_ref[:, hp + (g_n + g) * n:hp + (g_n + g + 1) * n] = dc

            dwx = z * xdt * wx
            dlastx = jnp.sum(dwx, axis=0, keepdims=True) + jnp.sum(ds_out * s_in, axis=0, keepdims=True) * elastx
            rows = jnp.concatenate([jnp.sum(dy * xs, axis=0, keepdims=True), dlastx, jnp.zeros((14, rp), F32)], axis=0)
            to_fold += [q - itile * jnp.sum(q, axis=0, keepdims=True) + dy * g0 * ecsx - dwx, dxdt * xs, rows]

        folded = _rows_times(to_fold, et_ref[...], FOLD_PIECES)
        dcs = []
        for g in range(g_n):
            f_cs, f_rows = folded[3 * g], folded[3 * g + 2]
            dd_ref[g] += f_rows[0:1]
            dcs.append(f_cs + jnp.where(last_row, f_rows[1:2], 0.0))
        dadt_all = _times_rows(trit_ref[...], jnp.concatenate(dcs, axis=1), EXPAND_PIECES)
        for g in range(g_n):
            pre, dt, a = heads[g][:3]
            dadt = dadt_all[:, g * HEAD_LANES:(g + 1) * HEAD_LANES]
            dalog_ref[g] += jnp.sum(dadt * dt, axis=0, keepdims=True) * a
            dpre = (dadt * a + folded[3 * g + 1]) * (1.0 / (1.0 + jnp.exp(-pre)))
            ddt_ref[g] = dpre
            dbias_ref[g] += jnp.sum(dpre, axis=0, keepdims=True)

    par_shape = jax.ShapeDtypeStruct((g_n, 1, HEAD_LANES), F32)
    vec = pl.BlockSpec((1, hp), lambda c: (0, 0))
    return _hosted_call(
        "ssd_bwd", body, (nc,),
        [row_spec(conv), dtr_spec, par_spec, par_spec, dskx_spec, st_spec, row_spec(hp), row_spec(hp), row_spec(hp),
         row_spec(1), vec, *_const_specs(consts)],
        [row_spec(conv), dtr_spec, par_spec, par_spec, par_spec, row_spec(hp), vec],
        [jax.ShapeDtypeStruct((l, conv), F32), jax.ShapeDtypeStruct((g_n, l, HEAD_LANES), F32),
         par_shape, par_shape, par_shape, jax.ShapeDtypeStruct((l, pm.shape[1]), BF),
         jax.ShapeDtypeStruct((1, hp), F32)],
        [pltpu.VMEM((g_n, n, rp), F32), pltpu.VMEM((t, hp), F32)], ("arbitrary",),
        (xbc, dtr, dtb, alog, dskx, states, dyn, y, pm, rstd, norm_g, *consts), after=after)


POOL_HALO = 16


def _pool_mix(name, src, dp, backward, out_dtype, into=None, cw=256, tr=512):
    l = src.shape[0]
    gd = dp // len(POOL_WINDOWS)
    cwl, trl = _t(gd, cw), _t(l, tr)
    nt = l // trl

    def body(x_ref, *rest):
        o_ref = rest[-1]
        gi = pl.program_id(0)
        for wi, win in enumerate(POOL_WINDOWS):
            @pl.when(gi == wi)
            def _(win=win):
                def step(i, carry):
                    r0 = pl.multiple_of(i * trl, trl)
                    cur = x_ref[pl.ds(r0, trl), :]
                    trow = r0 + lax.broadcasted_iota(jnp.int32, (trl, 1), 0)
                    cnt = jnp.minimum(trow + 1, win).astype(F32)
                    if not backward:
                        halo = x_ref[pl.ds(pl.multiple_of(jnp.maximum(r0 - POOL_HALO, 0), POOL_HALO), POOL_HALO), :]
                        halo = jnp.where(i > 0, halo, 0.0)
                        s = jnp.concatenate([halo, cur], axis=0)
                        sh = 1
                        while sh < win:
                            s = s + pltpu.roll(s, sh, 0)
                            sh *= 2
                        res = s[POOL_HALO:] / cnt - cur
                    else:
                        halo = x_ref[pl.ds(pl.multiple_of(jnp.minimum(r0 + trl, l - POOL_HALO), POOL_HALO),
                                           POOL_HALO), :]
                        hrow = r0 + trl + lax.broadcasted_iota(jnp.int32, (POOL_HALO, 1), 0)
                        hcnt = jnp.minimum(hrow + 1, win).astype(F32)
                        halo = jnp.where(i < nt - 1, halo / hcnt, 0.0)
                        s = jnp.concatenate([cur / cnt, halo], axis=0)
                        sh = 1
                        while sh < win:
                            s = s + pltpu.roll(s, trl + POOL_HALO - sh, 0)
                            sh *= 2
                        res = s[:trl] - cur
                    o_ref[pl.ds(r0, trl), :] = res.astype(o_ref.dtype)
                    return carry
                lax.fori_loop(0, nt, step, 0)

    col = pl.BlockSpec((l, cwl), lambda g, j: (0, g * (gd // cwl) + j))
    if into is None:
        return pl.pallas_call(
            body, grid=(len(POOL_WINDOWS), gd // cwl), in_specs=[col], out_specs=col,
            out_shape=jax.ShapeDtypeStruct((l, dp), out_dtype),
            compiler_params=_params(("parallel", "parallel")), name=name)(src)
    return pl.pallas_call(
        body, grid=(len(POOL_WINDOWS), gd // cwl), in_specs=[col, pl.BlockSpec(memory_space=pl.ANY)], out_specs=col,
        out_shape=jax.ShapeDtypeStruct(into.shape, into.dtype), input_output_aliases={1: 0},
        compiler_params=_params(("parallel", "parallel")), name=name)(src, into)


def _pool_out_bwd_data(dx, wp_out, mg, pu, scale, tm=1024, tn=512):
    l, d = dx.shape
    dp = wp_out.shape[0]
    tm, tn = _t(l, tm), _t(dp, tn)

    def epi(dyp, m, gt, sc):
        sg = _silu(gt)
        return dyp * sc * sg, dyp * m * sc * _dsilu(gt), jnp.sum(dyp * m * sg, axis=0, keepdims=True)

    tile = pl.BlockSpec((tm, tn), lambda j, i, k: (i, j))
    right = pl.BlockSpec((tm, tn), lambda j, i, k: (i, dp // tn + j))
    vec = pl.BlockSpec((1, tn), lambda j, i, k: (0, j))
    return _mm("pool_out_bwd_data", (dp // tn, l // tm, 1),
               dx, pl.BlockSpec((tm, d), lambda j, i, k: (i, 0)),
               wp_out, pl.BlockSpec((tn, d), lambda j, i, k: (j, 0)), NT,
               [jax.ShapeDtypeStruct((l, dp), BF), jax.ShapeDtypeStruct((l, 2 * dp), BF),
                jax.ShapeDtypeStruct((1, dp), F32)], [tile, right, vec], (tm, tn),
               (mg, pu, scale), (tile, right, vec), epi, summed=(2, 1))


def _group_mm_fwd(mp, wg, pu, scale, tm=1024, tn=1024, tk=1024):
    l, dp = mp.shape
    ng, gd = wg.shape[0], wg.shape[1]
    tm, tn, tk = _t(l, tm), _t(gd, tn), _t(gd, tk)

    def epi(acc, gate, sc):
        return acc, acc * sc * _silu(gate)

    out = pl.BlockSpec((tm, tn), lambda g, i, j, k: (i, g * (gd // tn) + j))
    return _mm("group_mm_fwd", (ng, l // tm, gd // tn, gd // tk),
               mp, pl.BlockSpec((tm, tk), lambda g, i, j, k: (i, g * (gd // tk) + k)),
               wg, pl.BlockSpec((None, tk, tn), lambda g, i, j, k: (g, k, j)), NN,
               [jax.ShapeDtypeStruct((l, dp), F32), jax.ShapeDtypeStruct((l, dp), BF)], [out, out], (tm, tn),
               (pu, scale),
               (pl.BlockSpec((tm, tn), lambda g, i, j, k: (i, (dp + g * gd) // tn + j)),
                pl.BlockSpec((1, tn), lambda g, i, j, k: (0, g * (gd // tn) + j))), epi)


def _group_mm_bwd_data(dmg, wg, after=(), tm=1024, tn=1024, tk=1024):
    l, dp = dmg.shape
    ng, gd = wg.shape[0], wg.shape[1]
    tm, tn, tk = _t(l, tm), _t(gd, tn), _t(gd, tk)
    return _mm("group_mm_bwd_data", (ng, l // tm, gd // tn, gd // tk),
               dmg, pl.BlockSpec((tm, tk), lambda g, i, j, k: (i, g * (gd // tk) + k)),
               wg, pl.BlockSpec((None, tn, tk), lambda g, i, j, k: (g, j, k)), NT,
               [jax.ShapeDtypeStruct((l, dp), F32)],
               [pl.BlockSpec((tm, tn), lambda g, i, j, k: (i, g * (gd // tn) + j))], (tm, tn), after=after)[0]


def _group_mm_bwd_weight(mp, dmg, ng, tm=512, tk=1024):
    l, dp = mp.shape
    gd = dp // ng
    tm, tk = _t(gd, tm), _t(l, tk)
    return _mm("group_mm_bwd_weight", (ng, gd // tm, l // tk),
               mp, pl.BlockSpec((tk, tm), lambda g, i, k: (k, g * (gd // tm) + i)),
               dmg, pl.BlockSpec((tk, gd), lambda g, i, k: (k, g)), TN,
               [jax.ShapeDtypeStruct((ng, gd, gd // 2), F32)],
               [pl.BlockSpec((None, tm, gd // 2), lambda g, i, k: (g, i, 0))], (tm, gd), epi=_pack_epi)[0]


def _cast_bf16(name, w, tr=256):
    r, c = w.shape
    tr = _row_tile(r, tr, 16)

    def body(w_ref, o_ref):
        o_ref[...] = w_ref[...].astype(BF)

    blk = pl.BlockSpec((tr, c), lambda i: (i, 0))
    return pl.pallas_call(body, grid=(r // tr,), in_specs=[blk], out_specs=blk,
                          out_shape=jax.ShapeDtypeStruct((r, c), BF), compiler_params=_params(("parallel",)),
                          name=name)(w)


def _pack_rows(name, w, tr=256):
    r, c = w.shape
    tr = _row_tile(r, tr)

    def body(w_ref, o_ref):
        o_ref[...] = _pack_pairs(w_ref[...])

    return pl.pallas_call(body, grid=(r // tr, 2), in_specs=[pl.BlockSpec((tr, c // 2), lambda i, q: (i, q))],
                          out_specs=pl.BlockSpec((None, tr, c // 4), lambda i, q: (q, i, 0)),
                          out_shape=jax.ShapeDtypeStruct((2, r, c // 4), F32),
                          compiler_params=_params(("parallel", "parallel")), name=name)(w)


def _unpack_rows(name, w0, w1, tr=512):
    r, h = w0.shape
    tr = _row_tile(r, tr, 16)

    def body(w0_ref, w1_ref, o_ref):
        for q, w_ref in enumerate((w0_ref, w1_ref)):
            hi, lo = _unpack_pairs(w_ref[...])
            o_ref[q, :, :h] = hi.astype(BF)
            o_ref[q, :, h:] = lo.astype(BF)

    words = pl.BlockSpec((tr, h), lambda i: (i, 0))
    return pl.pallas_call(body, grid=(r // tr,), in_specs=[words, words],
                          out_specs=pl.BlockSpec((2, tr, 2 * h), lambda i: (0, i, 0)),
                          out_shape=jax.ShapeDtypeStruct((2, r, 2 * h), BF), compiler_params=_params(("parallel",)),
                          name=name)(w0, w1)


def _reduce_packed(name, recv, tr=256):
    nd, r, h = recv.shape
    tr = _row_tile(r, tr)

    def body(p_ref, o_ref):
        hi, lo = _unpack_pairs(p_ref[0])
        for k in range(1, nd):
            a, b = _unpack_pairs(p_ref[k])
            hi, lo = hi + a, lo + b
        o_ref[:, :h] = hi
        o_ref[:, h:] = lo

    return pl.pallas_call(body, grid=(r // tr,), in_specs=[pl.BlockSpec((nd, tr, h), lambda i: (0, i, 0))],
                          out_specs=pl.BlockSpec((tr, 2 * h), lambda i: (i, 0)),
                          out_shape=jax.ShapeDtypeStruct((r, 2 * h), F32), compiler_params=_params(("parallel",)),
                          name=name)(recv)


def _adamw_math(w, g, m, v):
    m2 = ADAM_B1 * m + (1.0 - ADAM_B1) * g
    v2 = ADAM_B2 * v + (1.0 - ADAM_B2) * (g * g)
    m_hat = m2 / (1.0 - ADAM_B1 ** ADAM_STEP)
    v_hat = v2 / (1.0 - ADAM_B2 ** ADAM_STEP)
    delta = -ADAM_LR * (m_hat / (jnp.sqrt(v_hat) + ADAM_EPS) + ADAM_WD * w)
    return delta, m2, v2


def _adamw(name, w, g, m, v, tr=256):
    r, c = w.shape
    tr = _row_tile(r, tr)

    def body(w_ref, g_ref, m_ref, v_ref, d_ref, m2_ref, v2_ref):
        d, m2, v2 = _adamw_math(w_ref[...], g_ref[...], m_ref[...], v_ref[...])
        d_ref[...] = d
        m2_ref[...] = m2
        v2_ref[...] = v2

    blk = pl.BlockSpec((tr, c), lambda i: (i, 0))
    shp = jax.ShapeDtypeStruct((r, c), F32)
    return pl.pallas_call(body, grid=(r // tr,), in_specs=[blk] * 4, out_specs=[blk] * 3, out_shape=[shp] * 3,
                          compiler_params=_params(("parallel",)), name=name)(w, g, m, v)


def _sum_slots(name, a):
    nd, r, c = a.shape

    def body(a_ref, o_ref):
        s = a_ref[0]
        for k in range(1, nd):
            s = s + a_ref[k]
        o_ref[...] = s

    return pl.pallas_call(body, out_shape=jax.ShapeDtypeStruct((r, c), F32), name=name)(a)


def _head_rows(v, g_n, r):
    return jnp.pad(v.reshape(g_n, 1, r), ((0, 0), (0, 0), (0, HEAD_LANES - r)))


class _Later:
    def __init__(self, hosted, first_of, weights_of, pool_w_out_of, send):
        self.hosted, self.first_of, self.weights_of = hosted, first_of, weights_of
        self.pool_w_out_of, self.send = pool_w_out_of, send


def _listed(res):
    return res if isinstance(res, (list, tuple)) else [res]


def _local_step(x, target, ln_g, final_g, conv_b, dt_bias, a_log, d_skip, norm_g, later):
    l, d = x.shape
    di = norm_g.shape[1]
    h = dt_bias.shape[1]
    p = di // h
    conv = conv_b.shape[1]
    n, t = SSM_STATE, SSD_CHUNK
    g_n = (conv - di) // (2 * n)
    r = h // g_n
    nm = di + conv
    ng = len(POOL_WINDOWS)
    dims = (h, g_n, r, p, n, t)
    pw = d

    hosted = later.hosted
    h0, r0, *arrived_0 = _rms_fwd("rms0_fwd", x, ln_g[0:1], exch=hosted.get("rms0"))
    wt_in, conv_w, scale = later.first_of(arrived_0)
    dp = scale.shape[1]
    pm, *arrived_i = _listed(_mm_nt_halves("in_proj_main", h0, wt_in, 0, nm, exch=hosted.get("in_proj")))
    dtr = _mm_nt_halves("in_proj_dt", h0, wt_in, nm, h, tn=h)
    xbc = _conv_fwd(pm, di, conv_w, conv_b)
    dtb, alog = _head_rows(dt_bias, g_n, r), _head_rows(a_log, g_n, r)
    dskx = jnp.repeat(d_skip.reshape(g_n, 1, r), p, axis=2)
    dtr_g = jnp.pad(jnp.transpose(dtr.reshape(l, g_n, r), (1, 0, 2)), ((0, 0), (0, 0), (0, HEAD_LANES - r)))
    y, states, yn, r_n, *arrived_s = _ssd_fwd(xbc, dtr_g, dtb, alog, dskx, pm, norm_g, dims, exch=hosted.get("ssd"))
    w_out, wg, wp_in = later.weights_of(arrived_i, arrived_s)
    x1 = _mm_nn("ssm_out_proj", yn, w_out, add=x)

    h1, r1 = _rms_fwd("rms1_fwd", x1, ln_g[1:2])
    pu, *arrived_p = _listed(_mm_nn_blocked("pool_in_proj", h1, wp_in, exch=hosted.get("pool_in_proj")))
    wp_out = later.pool_w_out_of(arrived_p)
    mp = _pool_mix("pool_mix_fwd", pu, dp, False, BF)
    mg, yp = _group_mm_fwd(mp, wg, pu, scale)
    x2 = _mm_nn("pool_out_proj", yp, wp_out, add=x1)

    dx2, dx2_b, d_final_g, loss = _loss_head(x2, final_g, target)

    gw_pout = _mm_tn_packed("pool_out_bwd_weight", yp, dx2_b, pw)
    going = later.send("pool_w_out", gw_pout)
    dmg, dpu, d_scale = _pool_out_bwd_data(dx2_b, wp_out, mg, pu, scale)
    dmp = _group_mm_bwd_data(dmg, wg, after=going)
    gw_g = _group_mm_bwd_weight(mp, dmg, ng)
    going = later.send("pool_w_group", gw_g)
    dpu = _pool_mix("pool_mix_bwd", dmp, dp, True, BF, into=dpu)
    dh1 = _mm_nt_blocked("pool_in_bwd_data", dpu, wp_in, after=going)
    gw_pin = _mm_tn_packed("pool_in_bwd_weight", h1, dpu, 2 * dp // N_DEV)
    going = later.send("pool_w_in", gw_pin)
    dx1, dx1_b, d_ln1 = _rms_bwd("rms1_bwd", dh1, x1, r1, ln_g[1:2], dx2)

    dyn = _mm_nt("ssm_out_bwd_data", dx1_b, w_out, 0, di, tn=1024, after=going)
    gw_out = _mm_tn_packed("ssm_out_bwd_weight", yn, dx1_b, pw)
    going = later.send("ssm_w_out", gw_out)
    dxbc, ddt_g, dbias_g, dalog_g, dd_g, dproj, d_norm_g = _ssd_bwd(
        xbc, dtr_g, dtb, alog, dskx, states, dyn, y, pm, r_n, norm_g, dims, after=going)
    dproj, d_conv_w, d_conv_b = _conv_bwd(pm, di, conv_w, conv_b, dxbc, dproj)
    ddt = jnp.transpose(ddt_g[:, :, :r], (1, 0, 2)).reshape(l, h)
    gwt_in, going = [], ()
    for q in range(2):
        gq = _mm_tn_packed(f"in_proj_bwd_weight_{q}", dproj, h0, d // 2, rows=nm + h, b_blk=q, after=going)
        gq = _mm_tn_packed(f"in_proj_bwd_weight_dt_{q}", ddt, h0, d // 2, into=gq, b_blk=q)
        going = later.send(f"ssm_w_in_{q}", gq)
        gwt_in.append(gq)
    gwt_in = jnp.concatenate(gwt_in, axis=0)
    dh0 = _mm_nn_halves("in_proj_bwd_data_dt", ddt, wt_in, nm, after=going)
    dh0 = _mm_nn_halves("in_proj_bwd_data", dproj, wt_in, 0, add=dh0)
    grad_x, _, d_ln0 = _rms_bwd("rms0_bwd", dh0, x, r0, ln_g[0:1], dx1)

    def heads(v):
        return v[:, 0, :r].reshape(1, h)

    small = dict(ln_g=jnp.concatenate([d_ln0, d_ln1], axis=0), final_g=d_final_g, conv_w=d_conv_w, conv_b=d_conv_b,
                 dt_bias=heads(dbias_g), a_log=heads(dalog_g), d_skip=heads(dd_g), norm_g=d_norm_g, scale=d_scale)
    big = dict(ssm_w_in=gwt_in, ssm_w_out=gw_out, pool_w_in=gw_pin, pool_w_group=gw_g, pool_w_out=gw_pout)
    return loss, grad_x, small, big


SMALL_ORDER = ("ln_g", "final_g", "conv_w", "conv_b", "dt_bias", "a_log", "d_skip", "norm_g", "scale", "loss")


def _flatten_small(parts):
    flat = jnp.concatenate([parts[k].reshape(-1) for k in SMALL_ORDER])
    n = flat.shape[0]
    rows = -(-n // 1024) * 8
    return jnp.pad(flat, (0, rows * 128 - n)).reshape(rows, 128)


def _split_small(flat, shapes):
    flat = flat.reshape(-1)
    out, off = {}, 0
    for k in SMALL_ORDER:
        size = int(np.prod(shapes[k]))
        out[k] = flat[off:off + size].reshape(shapes[k])
        off += size
    return out


def kernel(x, ln_g, final_g, ssm_w_in, ssm_conv_w, ssm_conv_b, ssm_dt_bias, ssm_a_log, ssm_d, ssm_norm_g, ssm_w_out, pool_w_in, pool_w_group, pool_scale, pool_w_out, loss_target, m_ln_g, m_final_g, m_ssm_w_in, m_ssm_conv_w, m_ssm_conv_b, m_ssm_dt_bias, m_ssm_a_log, m_ssm_d, m_ssm_norm_g, m_ssm_w_out, m_pool_w_in, m_pool_w_group, m_pool_scale, m_pool_w_out, v_ln_g, v_final_g, v_ssm_w_in, v_ssm_conv_w, v_ssm_conv_b, v_ssm_dt_bias, v_ssm_a_log, v_ssm_d, v_ssm_norm_g, v_ssm_w_out, v_pool_w_in, v_pool_w_group, v_pool_scale, v_pool_w_out):
    l, d = x.shape[1], x.shape[2]
    me = 4 * lax.axis_index("x") + 2 * lax.axis_index("y") + lax.axis_index("c")
    ng, gds, gd = pool_w_group.shape[1], pool_w_group.shape[2], pool_w_group.shape[3]
    sin_s = ssm_w_in.shape[2]
    dp_s = pool_w_out.shape[1]
    conv_s = ssm_conv_w.shape[2]

    wt_in_s = _pack_rows("pack_w_in", jnp.transpose(ssm_w_in[0]))
    w_out_s = _cast_bf16("cast_w_out", ssm_w_out[0])
    wp_in_s = _cast_bf16("cast_pool_w_in", pool_w_in[0])
    wg_s = _cast_bf16("cast_pool_w_group", pool_w_group[0].reshape(ng * gds, gd))
    wp_out_s = _cast_bf16("cast_pool_w_out", pool_w_out[0])
    small_s = jnp.concatenate([ssm_conv_w[0].reshape(-1), pool_scale[0]])
    n_small = small_s.shape[0]
    small_s = jnp.pad(small_s, (0, -(-n_small // 1024) * 1024 - n_small)).reshape(-1, 128)
    hosted = dict(rms0=_Gather([wt_in_s[0], wt_in_s[1], small_s]), in_proj=_Gather([w_out_s, wg_s]),
                  ssd=_Gather([wp_in_s]), pool_in_proj=_Gather([wp_out_s]))

    def first_of(arrived_0):
        wt_in_g0, wt_in_g1, small_g = arrived_0
        wt_in = _unpack_rows("unpack_w_in", wt_in_g0.reshape(N_DEV * sin_s, d // 4),
                             wt_in_g1.reshape(N_DEV * sin_s, d // 4))
        small_all = small_g.reshape(N_DEV, -1)[:, :n_small]
        conv_w = jnp.transpose(small_all[:, :CONV_TAPS * conv_s].reshape(N_DEV, CONV_TAPS, conv_s), (1, 0, 2))
        return wt_in, conv_w.reshape(CONV_TAPS, -1), small_all[:, CONV_TAPS * conv_s:].reshape(1, -1)

    def weights_of(arrived_i, arrived_s):
        wg = jnp.transpose(arrived_i[1].reshape(N_DEV, ng, gds, gd), (1, 0, 2, 3)).reshape(ng, gd, gd)
        return arrived_i[0].reshape(-1, d), wg, arrived_s[0]

    def rows_major(gp):
        q, _, hw = gp.shape
        return jnp.transpose(gp.reshape(q, N_DEV, -1, hw), (1, 0, 2, 3))

    to_blocks = dict(ssm_w_in_0=rows_major, ssm_w_in_1=rows_major, ssm_w_out=rows_major, pool_w_out=rows_major,
                     pool_w_in=lambda gp: gp[:, None],
                     pool_w_group=lambda gp: jnp.transpose(gp.reshape(ng, N_DEV, gds, gd // 2), (1, 0, 2, 3)))
    travelling = {}

    def send(name, gp):
        *travelling[name], token = _scatter_start(f"scatter_{name}_start", to_blocks[name](gp))
        return (token,)

    later = _Later(hosted, first_of, weights_of, lambda arrived_p: arrived_p[0].reshape(-1, d), send)

    loss, grad_x, small, _ = _local_step(
        x[0], loss_target[0], ln_g, final_g.reshape(1, d), ssm_conv_b, ssm_dt_bias, ssm_a_log, ssm_d, ssm_norm_g, later)

    def arrived(name, after):
        src, land = _scatter_wait(f"scatter_{name}_wait", *travelling[name], after=after)
        own = lax.dynamic_slice_in_dim(src, me, 1, axis=0)
        return lax.dynamic_update_slice_in_dim(land, own, me, axis=0)

    def reduced(name, rv):
        q = rv.shape[1]
        cols = [_reduce_packed(f"{name}_{j}", rv[:, j]) for j in range(q)]
        return cols[0] if q == 1 else jnp.concatenate(cols, axis=1)

    grads = {}
    grads["pool_w_out"] = reduced("reduce_pool_w_out", arrived("pool_w_out", (grad_x,)))[None]
    grads["pool_w_group"] = _reduce_packed("reduce_pool_w_group", arrived("pool_w_group", (grad_x,)).reshape(
        N_DEV, ng * gds, gd // 2)).reshape(1, ng, gds, gd)
    grads["pool_w_in"] = reduced("reduce_pool_w_in", arrived("pool_w_in", (grad_x,)))[None]
    grads["ssm_w_out"] = reduced("reduce_w_out", arrived("ssm_w_out", (grad_x,)))[None]

    weights = dict(ln_g=ln_g, final_g=final_g, ssm_w_in=ssm_w_in, ssm_conv_w=ssm_conv_w, ssm_conv_b=ssm_conv_b,
                   ssm_dt_bias=ssm_dt_bias, ssm_a_log=ssm_a_log, ssm_d=ssm_d, ssm_norm_g=ssm_norm_g,
                   ssm_w_out=ssm_w_out, pool_w_in=pool_w_in, pool_w_group=pool_w_group, pool_scale=pool_scale,
                   pool_w_out=pool_w_out)
    m_in = dict(ln_g=m_ln_g, final_g=m_final_g, ssm_w_in=m_ssm_w_in, ssm_conv_w=m_ssm_conv_w, ssm_conv_b=m_ssm_conv_b,
                ssm_dt_bias=m_ssm_dt_bias, ssm_a_log=m_ssm_a_log, ssm_d=m_ssm_d, ssm_norm_g=m_ssm_norm_g,
                ssm_w_out=m_ssm_w_out, pool_w_in=m_pool_w_in, pool_w_group=m_pool_w_group, pool_scale=m_pool_scale,
                pool_w_out=m_pool_w_out)
    v_in = dict(ln_g=v_ln_g, final_g=v_final_g, ssm_w_in=v_ssm_w_in, ssm_conv_w=v_ssm_conv_w, ssm_conv_b=v_ssm_conv_b,
                ssm_dt_bias=v_ssm_dt_bias, ssm_a_log=v_ssm_a_log, ssm_d=v_ssm_d, ssm_norm_g=v_ssm_norm_g,
                ssm_w_out=v_ssm_w_out, pool_w_in=v_pool_w_in, pool_w_group=v_pool_w_group, pool_scale=v_pool_scale,
                pool_w_out=v_pool_w_out)
    names = list(weights)
    big_names = ("ssm_w_out", "pool_w_in", "pool_w_group", "pool_w_out", "ssm_w_in")
    delta, new_m, new_v = {}, {}, {}

    def adamw_big(k):
        shp = weights[k].shape
        two_d = (-1, shp[-1])
        dk, mk, vk = _adamw(f"adamw_{k}", weights[k].reshape(two_d), grads[k].reshape(two_d), m_in[k].reshape(two_d),
                            v_in[k].reshape(two_d))
        delta[k], new_m[k], new_v[k] = dk.reshape(shp), mk.reshape(shp), vk.reshape(shp)

    for k in big_names[:-1]:
        adamw_big(k)
    small_names = [k for k in names if k not in big_names]

    small["loss"] = loss
    shapes = {k: small[k].shape for k in SMALL_ORDER}
    gathered_small, = _exchange_alone("all_gather_small_grads", _Gather([_flatten_small(small)]),
                                      after=tuple(delta[k] for k in big_names[:-1]))
    summed = _split_small(_sum_slots("sum_small_grads", gathered_small), shapes)
    grads.update(ln_g=summed["ln_g"], final_g=summed["final_g"].reshape(d), ssm_conv_b=summed["conv_b"],
                 ssm_conv_w=lax.dynamic_slice_in_dim(summed["conv_w"], me * conv_s, conv_s, axis=1)[None],
                 ssm_dt_bias=summed["dt_bias"], ssm_a_log=summed["a_log"], ssm_d=summed["d_skip"],
                 ssm_norm_g=summed["norm_g"],
                 pool_scale=lax.dynamic_slice_in_dim(summed["scale"], me * dp_s, dp_s, axis=1))

    def packed(tree):
        flat = jnp.concatenate([tree[k].reshape(-1) for k in small_names])
        nn = flat.shape[0]
        return jnp.pad(flat, (0, -(-nn // 1024) * 1024 - nn), constant_values=1.0).reshape(-1, 128)

    ds, ms, vs = _adamw("adamw_small", packed(weights), packed(grads), packed(m_in), packed(v_in))
    off = 0
    for k in small_names:
        shp = weights[k].shape
        size = int(np.prod(shp))
        for res, arr in ((delta, ds), (new_m, ms), (new_v, vs)):
            res[k] = arr.reshape(-1)[off:off + size].reshape(shp)
        off += size

    k = "ssm_w_in"
    gt = jnp.concatenate([reduced(f"reduce_w_in_{q}", arrived(f"ssm_w_in_{q}", (ds,))) for q in range(2)], axis=1)
    dk, mk, vk = _adamw("adamw_ssm_w_in", jnp.transpose(weights[k][0]), gt, jnp.transpose(m_in[k][0]),
                        jnp.transpose(v_in[k][0]))
    grads[k], delta[k], new_m[k], new_v[k] = (jnp.transpose(t)[None] for t in (gt, dk, mk, vk))

    return (summed["loss"].reshape(()), grad_x[None], *[grads[k] for k in names], *[delta[k] for k in names],
            *[new_m[k] for k in names], *[new_v[k] for k in names])
```

```python
import math

import jax
import jax.numpy as jnp
import numpy as np
from jax import lax
from jax.experimental import pallas as pl
from jax.experimental.pallas import tpu as pltpu

F32 = jnp.float32
BF = jnp.bfloat16
U32 = jnp.uint32

N_DEV = 8
EPS = 1e-6
SSD_CHUNK = 64
SSM_STATE = 128
CONV_TAPS = 4
HEAD_LANES = 128
POOL_WINDOWS = (2, 4, 8, 16)
ADAM_LR, ADAM_B1, ADAM_B2, ADAM_EPS, ADAM_WD, ADAM_STEP = 0.001, 0.9, 0.999, 1e-08, 0.01, 10
VMEM_LIMIT = 56 * 1024 * 1024
NEG_BIG = -1e30
HAND_ON_AT = 6

NN = ((1,), (0,))
NT = ((1,), (1,))
TN = ((0,), (0,))
MESH = pl.DeviceIdType.MESH


def _t(dim, pref):
    return pref if dim % pref == 0 else dim


def _row_tile(rows, pref, mult=8):
    best = rows
    for cand in range(mult, min(rows, pref) + 1, mult):
        if rows % cand == 0:
            best = cand
    return best


def _params(sem=None):
    return pltpu.CompilerParams(dimension_semantics=sem, vmem_limit_bytes=VMEM_LIMIT)


def _silu(x):
    return x * (1.0 / (1.0 + jnp.exp(-x)))


def _dsilu(x):
    s = 1.0 / (1.0 + jnp.exp(-x))
    return s * (1.0 + x * (1.0 - s))


def _bdot(a, b, dims=NN):
    return lax.dot_general(a.astype(BF), b.astype(BF), (dims, ((), ())), preferred_element_type=F32)


def _pack_pairs(x):
    h = x.shape[1] // 2
    hi = lax.bitcast_convert_type(x[:, :h].astype(jnp.bfloat16).astype(F32), U32)
    lo = lax.bitcast_convert_type(x[:, h:].astype(jnp.bfloat16).astype(F32), U32)
    return lax.bitcast_convert_type(hi | (lo >> 16), F32)


def _unpack_pairs(w):
    u = lax.bitcast_convert_type(w, U32)
    hi = lax.bitcast_convert_type(u & jnp.uint32(0xFFFF0000), F32)
    lo = lax.bitcast_convert_type(u << 16, F32)
    return hi, lo


def _mesh_pos():
    return lax.axis_index("x"), lax.axis_index("y"), lax.axis_index("c")


def _slot(pos):
    return 4 * pos[0] + 2 * pos[1] + pos[2]


class _Gather:
    def __init__(self, arrays):
        self.arrays = list(arrays)
        self.out_shapes = [jax.ShapeDtypeStruct((N_DEV, *s.shape), s.dtype) for s in arrays]

    def phases(self, src, dst, send_sems, recv_sems, local_sems):
        n_arr = len(self.arrays)
        x, y, c = _mesh_pos()
        me, sibling = (x, y, c), (x, y, 1 - c)
        chips = [(1 - x, y), (x, 1 - y), (1 - x, 1 - y)]

        def copy(a, k, block, to, from_src):
            return pltpu.make_async_remote_copy(
                src_ref=src[a] if from_src else dst[a].at[_slot(block)], dst_ref=dst[a].at[_slot(block)],
                send_sem=send_sems.at[a * 7 + k], recv_sem=recv_sems.at[a * 7 + k], device_id=to, device_id_type=MESH)

        def mine(a):
            return pltpu.make_async_copy(src[a], dst[a].at[_slot(me)], local_sems.at[a])

        def first(a):
            return [copy(a, 0, me, sibling, True)] + [copy(a, 1 + j, me, (*chip, c), True)
                                                     for j, chip in enumerate(chips)]

        def start():
            for a in range(n_arr):
                mine(a).start()
                for cp in first(a):
                    cp.start()

        def middle():
            for j, chip in enumerate(chips):
                for a in range(n_arr):
                    copy(a, 1 + j, (*chip, c), me, False).wait_recv()
                    copy(a, 4 + j, (*chip, c), sibling, False).start()

        def finish():
            for a in range(n_arr):
                copy(a, 0, sibling, me, False).wait_recv()
                for j, chip in enumerate(chips):
                    copy(a, 4 + j, (*chip, 1 - c), me, False).wait_recv()
                for cp in first(a):
                    cp.wait_send()
                for j, chip in enumerate(chips):
                    copy(a, 4 + j, (*chip, c), sibling, False).wait_send()
                mine(a).wait()

        return start, middle, finish


def _hosted_call(name, body, grid, in_specs, out_specs, out_shape, scratch_shapes, sem, operands, exch=None,
                 after=(), into=None):
    if into is not None:
        n_lead = len(in_specs)
        inner = body

        def body(*refs):
            inner(*refs[:n_lead], *refs[n_lead + 1:])

        return pl.pallas_call(
            body, grid=grid, in_specs=[*in_specs, pl.BlockSpec(memory_space=pl.ANY)], out_specs=out_specs,
            out_shape=out_shape, scratch_shapes=scratch_shapes, input_output_aliases={n_lead: 0},
            compiler_params=_params(sem), name=name)(*operands, into)
    if after:
        n_lead = len(in_specs)
        inner = body

        def body(*refs):
            inner(*refs[:n_lead], *refs[n_lead + len(after):])

        in_specs = [*in_specs, *[pl.BlockSpec(memory_space=pl.ANY)] * len(after)]
        operands = (*operands, *after)
    if exch is None:
        return pl.pallas_call(body, grid=grid, in_specs=in_specs, out_specs=out_specs, out_shape=out_shape,
                              scratch_shapes=scratch_shapes, compiler_params=_params(sem), name=name)(*operands)
    n_in, n_out, n_scr, ne = len(in_specs), len(out_specs), len(scratch_shapes), len(exch.arrays)
    total = math.prod(grid)

    def wrapped(*refs):
        ins, ex_in = refs[:n_in], refs[n_in:n_in + ne]
        outs = refs[n_in + ne:n_in + ne + n_out]
        ex_out = refs[n_in + ne + n_out:n_in + 2 * ne + n_out]
        scr = refs[n_in + 2 * ne + n_out:n_in + 2 * ne + n_out + n_scr]
        step = 0
        for axis, size in enumerate(grid):
            step = step * size + pl.program_id(axis)
        start, middle, finish = exch.phases(ex_in, ex_out, *refs[-3:])
        pl.when(step == 0)(start)
        if middle is not None:
            pl.when(step == (total * HAND_ON_AT) // 8)(middle)
        body(*ins, *outs, *scr)
        pl.when(step == total - 1)(finish)

    hbm = pl.BlockSpec(memory_space=pl.ANY)
    sems = [pltpu.SemaphoreType.DMA((ne * 7,)), pltpu.SemaphoreType.DMA((ne * 7,)), pltpu.SemaphoreType.DMA((ne,))]
    return pl.pallas_call(
        wrapped, grid=grid, in_specs=[*in_specs, *[hbm] * ne], out_specs=[*out_specs, *[hbm] * ne],
        out_shape=[*out_shape, *exch.out_shapes], scratch_shapes=[*scratch_shapes, *sems],
        compiler_params=pltpu.CompilerParams(dimension_semantics=("arbitrary",) * len(grid),
                                             vmem_limit_bytes=VMEM_LIMIT, has_side_effects=True),
        name=name)(*operands, *exch.arrays)


def _flip_peers():
    x, y, c = _mesh_pos()
    peers = []
    for k in range(1, N_DEV):
        fx, fy, fc = (k >> 2) & 1, (k >> 1) & 1, k & 1
        peers.append((1 - x if fx else x, 1 - y if fy else y, 1 - c if fc else c))
    return (x, y, c), peers


def _split_scatter_copy(src, land, send_sems, recv_sems, k, me, peer, sending):
    return pltpu.make_async_remote_copy(
        src_ref=src.at[_slot(peer)], dst_ref=land.at[_slot(me) if sending else _slot(peer)],
        send_sem=send_sems.at[k], recv_sem=recv_sems.at[k], device_id=peer, device_id_type=MESH)


def _scatter_start(name, blocks):
    def body(src, land, send_sems, recv_sems, src_thru, land_thru, token):
        me, peers = _flip_peers()
        for k, peer in enumerate(peers):
            _split_scatter_copy(src, land, send_sems, recv_sems, k, me, peer, True).start()
        token[...] = jnp.zeros_like(token)

    hbm = pl.BlockSpec(memory_space=pltpu.HBM)
    sem = pl.BlockSpec(memory_space=pltpu.SEMAPHORE)
    return pl.pallas_call(
        body, name=name,
        out_shape=(pltpu.SemaphoreType.DMA((N_DEV - 1,)), pltpu.SemaphoreType.DMA((N_DEV - 1,)),
                   pltpu.HBM(blocks.shape, blocks.dtype), pltpu.HBM(blocks.shape, blocks.dtype),
                   jax.ShapeDtypeStruct((8, 128), F32)),
        in_specs=(hbm, hbm), out_specs=(sem, sem, hbm, hbm, pl.BlockSpec(memory_space=pltpu.VMEM)),
        input_output_aliases={0: 2, 1: 3},
        compiler_params=pltpu.CompilerParams(has_side_effects=pltpu.SideEffectType.DATAFLOW_SIDE_EFFECTING),
    )(pltpu.with_memory_space_constraint(blocks, pltpu.HBM),
      pltpu.with_memory_space_constraint(lax.empty(blocks.shape, blocks.dtype), pltpu.HBM))


def _scatter_wait(name, send_sems, recv_sems, src_thru, land_thru, after):
    def body(src, land, send_sems, recv_sems, *rest):
        me, peers = _flip_peers()
        for k, peer in enumerate(peers):
            _split_scatter_copy(src, land, send_sems, recv_sems, k, me, peer, True).wait_send()
            _split_scatter_copy(src, land, send_sems, recv_sems, k, me, peer, False).wait_recv()

    hbm = pl.BlockSpec(memory_space=pltpu.HBM)
    sem = pl.BlockSpec(memory_space=pltpu.SEMAPHORE)
    return pl.pallas_call(
        body, name=name,
        out_shape=(pltpu.HBM(src_thru.shape, src_thru.dtype), pltpu.HBM(land_thru.shape, land_thru.dtype)),
        in_specs=(hbm, hbm, sem, sem, *[pl.BlockSpec(memory_space=pl.ANY)] * len(after)), out_specs=(hbm, hbm),
        input_output_aliases={0: 0, 1: 1},
        compiler_params=pltpu.CompilerParams(has_side_effects=pltpu.SideEffectType.DATAFLOW_SIDE_EFFECTING),
    )(src_thru, land_thru, send_sems, recv_sems, *after)


def _exchange_alone(name, exch, after=()):
    def body():
        pass

    return _hosted_call(name, body, (1,), [], [], [], [], None, (), exch, after)


def _mm(name, grid, a, a_spec, b, b_spec, dims, outs, o_specs, acc_shape, extra=(), extra_specs=(), epi=None,
        exch=None, after=(), into=None, summed=None):
    nk = grid[-1]
    n_extra, n_out = len(extra), len(outs)

    def body(*refs):
        a_ref, b_ref = refs[0], refs[1]
        ex = refs[2:2 + n_extra]
        o_refs = refs[2 + n_extra:2 + n_extra + n_out]

        def write(res):
            res = (res,) if epi is None else epi(res, *[e[...] for e in ex])
            for idx, (o, r) in enumerate(zip(o_refs, res)):
                if summed is not None and idx == summed[0]:
                    first = pl.program_id(summed[1]) == 0

                    @pl.when(first)
                    def _(o=o, r=r):
                        o[...] = r.astype(o.dtype)

                    @pl.when(jnp.logical_not(first))
                    def _(o=o, r=r):
                        o[...] += r.astype(o.dtype)
                else:
                    o[...] = r.astype(o.dtype)

        if nk == 1:
            write(_bdot(a_ref[...], b_ref[...], dims))
            return
        acc = refs[-1]
        k = pl.program_id(len(grid) - 1)

        @pl.when(k == 0)
        def _():
            acc[...] = _bdot(a_ref[...], b_ref[...], dims)

        @pl.when(jnp.logical_and(k > 0, k < nk - 1))
        def _():
            acc[...] += _bdot(a_ref[...], b_ref[...], dims)

        @pl.when(k == nk - 1)
        def _():
            write(acc[...] + _bdot(a_ref[...], b_ref[...], dims))

    sem = ("parallel",) * (len(grid) - 1) + ("arbitrary",)
    if summed is not None:
        sem = ("arbitrary",) * len(grid)
    scratch = [] if nk == 1 else [pltpu.VMEM(acc_shape, F32)]
    return _hosted_call(name, body, grid, [a_spec, b_spec, *extra_specs], list(o_specs), list(outs),
                        scratch, sem, (a, b, *extra), exch, after, into)


def _add_epi(acc, add):
    return (acc + add,)


def _pack_epi(acc):
    return (_pack_pairs(acc),)


def _mm_nt(name, a, b, n_off, n, out_dtype=F32, tm=1024, tn=512, exch=None, after=()):
    m, kk = a.shape
    tm, tn = _t(m, tm), math.gcd(_t(n, tn), n_off)
    res = _mm(name, (m // tm, n // tn, 1),
              a, pl.BlockSpec((tm, kk), lambda i, j, k: (i, 0)),
              b, pl.BlockSpec((tn, kk), lambda i, j, k: (n_off // tn + j, 0)), NT,
              [jax.ShapeDtypeStruct((m, n), out_dtype)], [pl.BlockSpec((tm, tn), lambda i, j, k: (i, j))], (tm, tn),
              exch=exch, after=after)
    return res[0] if exch is None else res


def _mm_nt_halves(name, a, b2, n_off, n, tm=1024, tn=512, exch=None):
    m = a.shape[0]
    kh = b2.shape[2]
    tm, tn = _t(m, tm), math.gcd(_t(n, tn), n_off)

    def body(a_ref, b_ref, o_ref):
        o_ref[...] = _bdot(a_ref[:, :kh], b_ref[0], NT) + _bdot(a_ref[:, kh:], b_ref[1], NT)

    res = _hosted_call(name, body, (m // tm, n // tn),
                       [pl.BlockSpec((tm, 2 * kh), lambda i, j: (i, 0)),
                        pl.BlockSpec((2, tn, kh), lambda i, j: (0, n_off // tn + j, 0))],
                       [pl.BlockSpec((tm, tn), lambda i, j: (i, j))], [jax.ShapeDtypeStruct((m, n), F32)], [],
                       ("parallel", "parallel"), (a, b2), exch)
    return res[0] if exch is None else res


def _mm_nn_halves(name, a, b2, k_off=0, add=None, after=(), tm=1024, tk=1024):
    m, kk = a.shape
    half = b2.shape[2]
    tm, tk = _t(m, tm), math.gcd(_t(kk, tk), k_off)
    extra, especs, epi = (), (), None
    if add is not None:
        extra, especs, epi = (add,), (pl.BlockSpec((tm, half), lambda i, j, k: (i, j)),), _add_epi
    return _mm(name, (m // tm, 2, kk // tk),
               a, pl.BlockSpec((tm, tk), lambda i, j, k: (i, k)),
               b2, pl.BlockSpec((None, tk, half), lambda i, j, k: (j, k_off // tk + k, 0)), NN,
               [jax.ShapeDtypeStruct((m, 2 * half), F32)], [pl.BlockSpec((tm, half), lambda i, j, k: (i, j))],
               (tm, half), extra, especs, epi, after=after)[0]


def _mm_nn_blocked(name, a, b3, tm=1024, tn=512, exch=None):
    m, kk = a.shape
    nb, _, cs = b3.shape
    tm, tn = _t(m, tm), _t(cs, tn)
    per = cs // tn
    res = _mm(name, (m // tm, nb * per, 1),
              a, pl.BlockSpec((tm, kk), lambda i, j, k: (i, 0)),
              b3, pl.BlockSpec((None, kk, tn), lambda i, j, k: (j // per, 0, j % per)), NN,
              [jax.ShapeDtypeStruct((m, nb * cs), F32)], [pl.BlockSpec((tm, tn), lambda i, j, k: (i, j))],
              (tm, tn), exch=exch)
    return res[0] if exch is None else res


def _mm_nt_blocked(name, a, b3, after=(), tm=1024, tn=1024):
    m = a.shape[0]
    nb, n, cs = b3.shape
    tm, tn = _t(m, tm), _t(n, tn)
    return _mm(name, (m // tm, n // tn, nb),
               a, pl.BlockSpec((tm, cs), lambda i, j, k: (i, k)),
               b3, pl.BlockSpec((None, tn, cs), lambda i, j, k: (k, j, 0)), NT,
               [jax.ShapeDtypeStruct((m, n), F32)], [pl.BlockSpec((tm, tn), lambda i, j, k: (i, j))], (tm, tn),
               after=after)[0]


def _mm_nn(name, a, b, k_off=0, kk=None, add=None, out_dtype=F32, tm=1024, tn=1024, tk=1024, after=()):
    m = a.shape[0]
    kk = a.shape[1] if kk is None else kk
    n = b.shape[1]
    tm, tn, tk = _t(m, tm), _t(n, tn), math.gcd(_t(kk, tk), k_off)
    extra, especs, epi = (), (), None
    if add is not None:
        extra, especs, epi = (add,), (pl.BlockSpec((tm, tn), lambda i, j, k: (i, j)),), _add_epi
    return _mm(name, (m // tm, n // tn, kk // tk),
               a, pl.BlockSpec((tm, tk), lambda i, j, k: (i, k)),
               b, pl.BlockSpec((tk, tn), lambda i, j, k: (k_off // tk + k, j)), NN,
               [jax.ShapeDtypeStruct((m, n), out_dtype)], [pl.BlockSpec((tm, tn), lambda i, j, k: (i, j))], (tm, tn),
               extra, especs, epi, after=after)[0]


TN_ACC_ELEMENTS = 1 << 20


def _mm_tn_packed(name, a, b, pw, tk=1024, rows=None, into=None, b_blk=None, after=()):
    kk, m = a.shape
    nq = b.shape[1] // pw if b_blk is None else 1
    first = 0 if b_blk is None else b_blk
    tm, tk = _t(m, TN_ACC_ELEMENTS // pw), _t(kk, tk)
    rows = m if rows is None else rows
    row_blk = 0
    if into is not None:
        rows = into.shape[1]
        assert (rows - m) % tm == 0
        row_blk = (rows - m) // tm
    return _mm(name, (m // tm, nq, kk // tk),
               a, pl.BlockSpec((tk, tm), lambda i, j, k: (k, i)),
               b, pl.BlockSpec((tk, pw), lambda i, j, k: (k, first + j)), TN,
               [jax.ShapeDtypeStruct((nq, rows, pw // 2), F32)],
               [pl.BlockSpec((None, tm, pw // 2), lambda i, j, k: (j, row_blk + i, 0))], (tm, pw), epi=_pack_epi,
               into=into, after=after)[0]


def _rms_fwd(name, x, g, tm=256, exch=None):
    l, d = x.shape
    tm = _t(l, tm)

    def body(x_ref, g_ref, h_ref, r_ref):
        xv = x_ref[...]
        r = lax.rsqrt(jnp.mean(xv * xv, axis=-1, keepdims=True) + EPS)
        h_ref[...] = (xv * r * g_ref[...]).astype(BF)
        r_ref[...] = r

    return _hosted_call(
        name, body, (l // tm,),
        [pl.BlockSpec((tm, d), lambda i: (i, 0)), pl.BlockSpec((1, d), lambda i: (0, 0))],
        [pl.BlockSpec((tm, d), lambda i: (i, 0)), pl.BlockSpec((tm, 1), lambda i: (i, 0))],
        [jax.ShapeDtypeStruct((l, d), BF), jax.ShapeDtypeStruct((l, 1), F32)], [], ("parallel",), (x, g), exch)


def _rms_bwd(name, dh, x, r, g, dres, tm=256):
    l, d = x.shape
    tm = _t(l, tm)

    def body(dh_ref, x_ref, r_ref, g_ref, dres_ref, dx_ref, dxb_ref, dg_ref):
        i = pl.program_id(0)
        rr = r_ref[...]
        xhat = x_ref[...] * rr
        dhv = dh_ref[...]
        dxh = dhv * g_ref[...]
        dx = dres_ref[...] + rr * (dxh - xhat * jnp.mean(dxh * xhat, axis=-1, keepdims=True))
        dx_ref[...] = dx
        dxb_ref[...] = dx.astype(BF)

        @pl.when(i == 0)
        def _():
            dg_ref[...] = jnp.zeros_like(dg_ref)

        dg_ref[...] += jnp.sum(dhv * xhat, axis=0, keepdims=True)

    row = pl.BlockSpec((tm, d), lambda i: (i, 0))
    vec = pl.BlockSpec((1, d), lambda i: (0, 0))
    return pl.pallas_call(
        body, grid=(l // tm,), in_specs=[row, row, pl.BlockSpec((tm, 1), lambda i: (i, 0)), vec, row],
        out_specs=[row, row, vec],
        out_shape=[jax.ShapeDtypeStruct((l, d), F32), jax.ShapeDtypeStruct((l, d), BF),
                   jax.ShapeDtypeStruct((1, d), F32)],
        compiler_params=_params(("arbitrary",)), name=name)(dh, x, r, g, dres)


def _loss_head(x, g, target, tm=256):
    l, d = x.shape
    tm = _t(l, tm)

    def body(x_ref, g_ref, t_ref, dx_ref, dxb_ref, dg_ref, loss_ref):
        i = pl.program_id(0)
        xv = x_ref[...]
        gv = g_ref[...]
        r = lax.rsqrt(jnp.mean(xv * xv, axis=-1, keepdims=True) + EPS)
        xhat = xv * r
        e = xhat * gv - t_ref[...]
        dy = e * (1.0 / d)
        dxh = dy * gv
        dx = r * (dxh - xhat * jnp.mean(dxh * xhat, axis=-1, keepdims=True))
        dx_ref[...] = dx
        dxb_ref[...] = dx.astype(BF)

        @pl.when(i == 0)
        def _():
            dg_ref[...] = jnp.zeros_like(dg_ref)
            loss_ref[...] = jnp.zeros_like(loss_ref)

        dg_ref[...] += jnp.sum(dy * xhat, axis=0, keepdims=True)
        loss_ref[...] += 0.5 * jnp.sum(jnp.sum(e * e, axis=-1, keepdims=True) * (1.0 / d), axis=0, keepdims=True)

    row = pl.BlockSpec((tm, d), lambda i: (i, 0))
    vec = pl.BlockSpec((1, d), lambda i: (0, 0))
    return pl.pallas_call(
        body, grid=(l // tm,), in_specs=[row, vec, row],
        out_specs=[row, row, vec, pl.BlockSpec((1, 1), lambda i: (0, 0))],
        out_shape=[jax.ShapeDtypeStruct((l, d), F32), jax.ShapeDtypeStruct((l, d), BF),
                   jax.ShapeDtypeStruct((1, d), F32), jax.ShapeDtypeStruct((1, 1), F32)],
        compiler_params=_params(("arbitrary",)), name="loss_head")(x, g, target)


CONV_HALO = 8


def _conv_pre(x_ref, w_ref, b_ref, i, tr):
    r0 = pl.multiple_of(i * tr, tr)
    cur = x_ref[pl.ds(r0, tr), :]
    prev = x_ref[pl.ds(pl.multiple_of(jnp.maximum(r0 - CONV_HALO, 0), CONV_HALO), CONV_HALO), :]
    prev = jnp.where(i > 0, prev, 0.0)
    ext = jnp.concatenate([prev, cur], axis=0)
    taps = []
    for k in range(CONV_TAPS):
        s = CONV_TAPS - 1 - k
        taps.append(cur if s == 0 else pltpu.roll(ext, s, 0)[CONV_HALO:])
    pre = b_ref[...] + sum(w_ref[k:k + 1, :] * taps[k] for k in range(CONV_TAPS))
    return r0, pre, taps


def _conv_fwd(pm, col_off, conv_w, conv_b, cw=256, tr=512):
    l = pm.shape[0]
    c = conv_w.shape[1]
    cw, tr = _t(c, cw), _t(l, tr)
    assert col_off % cw == 0

    def body(x_ref, w_ref, b_ref, o_ref):
        def step(i, carry):
            r0, pre, _ = _conv_pre(x_ref, w_ref, b_ref, i, tr)
            o_ref[pl.ds(r0, tr), :] = _silu(pre)
            return carry
        lax.fori_loop(0, l // tr, step, 0)

    return pl.pallas_call(
        body, grid=(c // cw,),
        in_specs=[pl.BlockSpec((l, cw), lambda j: (0, col_off // cw + j)),
                  pl.BlockSpec((CONV_TAPS, cw), lambda j: (0, j)), pl.BlockSpec((1, cw), lambda j: (0, j))],
        out_specs=pl.BlockSpec((l, cw), lambda j: (0, j)), out_shape=jax.ShapeDtypeStruct((l, c), F32),
        compiler_params=_params(("parallel",)), name="conv_fwd")(pm, conv_w, conv_b)


def _conv_bwd(pm, col_off, conv_w, conv_b, dy, dproj, cw=256, tr=512):
    l = pm.shape[0]
    c = conv_w.shape[1]
    cw, tr = _t(c, cw), _t(l, tr)
    nt = l // tr

    def body(x_ref, w_ref, b_ref, dy_ref, _, dx_ref, dw_ref, db_ref, dpre_ref):
        def step1(i, carry):
            dws, db = carry
            r0, pre, taps = _conv_pre(x_ref, w_ref, b_ref, i, tr)
            dpre = dy_ref[pl.ds(r0, tr), :] * _dsilu(pre)
            dpre_ref[pl.ds(r0, tr), :] = dpre
            dws = tuple(dws[k] + jnp.sum(dpre * taps[k], axis=0, keepdims=True) for k in range(CONV_TAPS))
            return dws, db + jnp.sum(dpre, axis=0, keepdims=True)

        z = jnp.zeros((1, cw), F32)
        dws, db = lax.fori_loop(0, nt, step1, ((z,) * CONV_TAPS, z))
        for k in range(CONV_TAPS):
            dw_ref[k:k + 1, :] = dws[k]
        db_ref[...] = db

        def step2(i, carry):
            r0 = pl.multiple_of(i * tr, tr)
            cur = dpre_ref[pl.ds(r0, tr), :]
            nxt = dpre_ref[pl.ds(pl.multiple_of(jnp.minimum(r0 + tr, l - CONV_HALO), CONV_HALO), CONV_HALO), :]
            nxt = jnp.where(i < nt - 1, nxt, 0.0)
            ext = jnp.concatenate([cur, nxt], axis=0)
            acc = w_ref[CONV_TAPS - 1:CONV_TAPS, :] * cur
            for k in range(CONV_TAPS - 1):
                s = CONV_TAPS - 1 - k
                acc = acc + w_ref[k:k + 1, :] * pltpu.roll(ext, tr + CONV_HALO - s, 0)[:tr]
            dx_ref[pl.ds(r0, tr), :] = acc.astype(dx_ref.dtype)
            return carry
        lax.fori_loop(0, nt, step2, 0)

    col = pl.BlockSpec((l, cw), lambda j: (0, j))
    shifted = pl.BlockSpec((l, cw), lambda j: (0, col_off // cw + j))
    return pl.pallas_call(
        body, grid=(c // cw,),
        in_specs=[shifted, pl.BlockSpec((CONV_TAPS, cw), lambda j: (0, j)), pl.BlockSpec((1, cw), lambda j: (0, j)),
                  col, pl.BlockSpec(memory_space=pl.ANY)],
        out_specs=[shifted, pl.BlockSpec((CONV_TAPS, cw), lambda j: (0, j)), pl.BlockSpec((1, cw), lambda j: (0, j))],
        out_shape=[jax.ShapeDtypeStruct(dproj.shape, dproj.dtype), jax.ShapeDtypeStruct((CONV_TAPS, c), F32),
                   jax.ShapeDtypeStruct((1, c), F32)],
        scratch_shapes=[pltpu.VMEM((l, cw), F32)], input_output_aliases={4: 0},
        compiler_params=_params(("parallel",)), name="conv_bwd")(pm, conv_w, conv_b, dy, dproj)


def _split(x, pieces):
    out, rest = [], x
    for _ in range(pieces):
        piece = rest.astype(BF)
        out.append(piece)
        rest = rest - piece.astype(F32)
    return out


def _rows_times(xs, m_stack, pieces):
    x = xs[0] if len(xs) == 1 else jnp.concatenate(xs, axis=0)
    out = _bdot(jnp.concatenate(_split(x, pieces), axis=1), m_stack)
    sizes = [v.shape[0] for v in xs]
    offs = np.cumsum([0] + sizes)
    return [out[offs[i]:offs[i + 1]] for i in range(len(xs))]


def _times_rows(m_stack, x, pieces):
    return _bdot(m_stack, jnp.concatenate(_split(x, pieces), axis=0))


EXPAND_PIECES = 3
FOLD_PIECES = 2


def _ssd_consts(r, p, t):
    assert p == t, "heads expand to P lanes of the inputs and to T lanes of the decay matrices alike"
    assert LANES % p == 0 and (r * p) % LANES == 0, "whole heads per lane tile, whole lane tiles per group"
    rp = r * p
    tri = np.tril(np.ones((t, t), np.float32))
    ep = np.zeros((HEAD_LANES, rp), np.float32)
    ep[np.arange(rp) // p, np.arange(rp)] = 1.0
    itile = (np.arange(t)[:, None] == (np.arange(rp) % t)[None, :]).astype(np.float32)
    lmask = (np.arange(t)[:, None] >= (np.arange(rp) % t)[None, :]).astype(np.float32)
    return [jnp.asarray(np.concatenate([ep] * EXPAND_PIECES, axis=0), BF),
            jnp.asarray(np.concatenate([ep.T] * FOLD_PIECES, axis=0), BF),
            jnp.asarray(np.concatenate([tri] * EXPAND_PIECES, axis=1), BF),
            jnp.asarray(np.concatenate([tri.T] * EXPAND_PIECES, axis=1), BF),
            jnp.asarray(itile), jnp.asarray(lmask)]


def _ssd_heads(dtr_ref, dtb_ref, alog_ref, e_stack, tri_stack, g_n, t):
    pre = [dtr_ref[g] + dtb_ref[g] for g in range(g_n)]
    dt = [jnp.maximum(v, 0.0) + jnp.log(1.0 + jnp.exp(-jnp.abs(v))) for v in pre]
    a = [-jnp.exp(alog_ref[g]) for g in range(g_n)]
    adt = jnp.concatenate([dt[g] * a[g] for g in range(g_n)], axis=1)
    cs = _times_rows(tri_stack, adt, EXPAND_PIECES)
    cs = [cs[:, g * HEAD_LANES:(g + 1) * HEAD_LANES] for g in range(g_n)]
    at_lanes = _rows_times([v for g in range(g_n) for v in (dt[g], cs[g])], e_stack, EXPAND_PIECES)
    return [(pre[g], dt[g], a[g], cs[g][t - 1:t, :], at_lanes[2 * g], at_lanes[2 * g + 1]) for g in range(g_n)]


LANES = 128


def _head_lanes(k, p, rows):
    lane = lax.broadcasted_iota(jnp.int32, (rows, LANES), 1)
    j = k % (LANES // p)
    return jnp.logical_and(lane >= j * p, lane < (j + 1) * p)


def _block_diagonal(v, r, p):
    rows = []
    for k in range(r):
        c = k * p // LANES
        tiles = [jnp.zeros((v.shape[0], LANES), v.dtype)] * (r * p // LANES)
        tiles[c] = jnp.where(_head_lanes(k, p, v.shape[0]), v[:, c * LANES:(c + 1) * LANES], jnp.zeros((), v.dtype))
        rows.append(jnp.concatenate(tiles, axis=1))
    return jnp.concatenate(rows, axis=0)


def _diagonal_blocks(m, r, p):
    t = m.shape[0] // r
    tiles = []
    for c in range(r * p // LANES):
        heads = range(c * LANES // p, (c + 1) * LANES // p)
        acc = None
        for k in heads:
            blk = m[k * t:(k + 1) * t, c * LANES:(c + 1) * LANES]
            acc = blk if acc is None else jnp.where(_head_lanes(k, p, t), blk, acc)
        tiles.append(acc)
    return jnp.concatenate(tiles, axis=1)


def _ssd_common(xs, bm, dtx, csx, itile, lmask, r, t):
    ecsx = jnp.exp(csx)
    lastx = csx[t - 1:t, :]
    wx = jnp.exp(lastx - csx)
    elastx = jnp.exp(lastx)
    csrow = jnp.sum(csx * itile, axis=0, keepdims=True)
    lx = jnp.exp(jnp.where(lmask > 0.0, csx - csrow, NEG_BIG))
    xdt = xs * dtx
    xblk = _block_diagonal(xdt.astype(BF), r, t)
    btile = jnp.concatenate([bm.astype(BF)] * r, axis=0)
    return ecsx, wx, elastx, lx, xdt, xblk, btile


def _ssd_specs(l, g_n, r, p, n, t, conv, rev):
    nc = l // t
    rp = r * p
    cidx = (lambda c: nc - 1 - c) if rev else (lambda c: c)
    row_spec = lambda width: pl.BlockSpec((t, width), lambda c: (cidx(c), 0))
    dtr_spec = pl.BlockSpec((g_n, t, HEAD_LANES), lambda c: (0, cidx(c), 0))
    par_spec = pl.BlockSpec((g_n, 1, HEAD_LANES), lambda c: (0, 0, 0))
    dskx_spec = pl.BlockSpec((g_n, 1, rp), lambda c: (0, 0, 0))
    st_spec = pl.BlockSpec((None, g_n, n, rp), lambda c: (cidx(c), 0, 0, 0))
    return nc, row_spec, dtr_spec, par_spec, dskx_spec, st_spec


def _const_specs(consts):
    return [pl.BlockSpec(a.shape, lambda c: (0, 0)) for a in consts]


def _ssd_fwd(xbc, dtr, dtb, alog, dskx, pm, norm_g, dims, exch=None):
    h, g_n, r, p, n, t = dims
    l, conv = xbc.shape
    rp, hp = r * p, h * p
    nc, row_spec, dtr_spec, par_spec, dskx_spec, st_spec = _ssd_specs(l, g_n, r, p, n, t, conv, False)
    consts = _ssd_consts(r, p, t)

    def body(x_ref, dtr_ref, dtb_ref, alog_ref, dskx_ref, z_ref, ng_ref,
             e_ref, et_ref, tri_ref, trit_ref, it_ref, lm_ref,
             y_ref, st_ref, yn_ref, rn_ref, s_ref):
        @pl.when(pl.program_id(0) == 0)
        def _():
            s_ref[...] = jnp.zeros_like(s_ref)

        heads = _ssd_heads(dtr_ref, dtb_ref, alog_ref, e_ref[...], tri_ref[...], g_n, t)
        for g in range(g_n):
            xs = x_ref[:, g * rp:(g + 1) * rp]
            bm = x_ref[:, hp + g * n:hp + (g + 1) * n]
            cm = x_ref[:, hp + (g_n + g) * n:hp + (g_n + g + 1) * n]
            dtx, csx = heads[g][4:]
            ecsx, wx, elastx, lx, xdt, xblk, btile = _ssd_common(xs, bm, dtx, csx, it_ref[...], lm_ref[...], r, t)
            s_in = s_ref[g]
            st_ref[g] = s_in
            cbx = _bdot(cm, btile, NT)
            yd = _bdot(cbx * lx, xblk)
            yo = ecsx * _bdot(cm, s_in)
            y_ref[:, g * rp:(g + 1) * rp] = yd + yo + dskx_ref[g] * xs
            s_ref[g] = elastx * s_in + _bdot(bm, xdt * wx, TN)

        v = y_ref[...] * _silu(z_ref[...])
        rr = lax.rsqrt(jnp.mean(v * v, axis=-1, keepdims=True) + EPS)
        yn_ref[...] = (v * rr * ng_ref[...]).astype(BF)
        rn_ref[...] = rr

    return _hosted_call(
        "ssd_fwd", body, (nc,),
        [row_spec(conv), dtr_spec, par_spec, par_spec, dskx_spec, row_spec(hp),
         pl.BlockSpec((1, hp), lambda c: (0, 0)), *_const_specs(consts)],
        [row_spec(hp), st_spec, row_spec(hp), row_spec(1)],
        [jax.ShapeDtypeStruct((l, hp), F32), jax.ShapeDtypeStruct((nc, g_n, n, rp), F32),
         jax.ShapeDtypeStruct((l, hp), BF), jax.ShapeDtypeStruct((l, 1), F32)],
        [pltpu.VMEM((g_n, n, rp), F32)], ("arbitrary",),
        (xbc, dtr, dtb, alog, dskx, pm, norm_g, *consts), exch)


def _ssd_bwd(xbc, dtr, dtb, alog, dskx, states, dyn, y, pm, rstd, norm_g, dims, after=()):
    h, g_n, r, p, n, t = dims
    l, conv = xbc.shape
    rp, hp = r * p, h * p
    nc, row_spec, dtr_spec, par_spec, dskx_spec, st_spec = _ssd_specs(l, g_n, r, p, n, t, conv, True)
    consts = _ssd_consts(r, p, t)

    def fold_rows(v, rows):
        return sum(v[k * rows:(k + 1) * rows, :] for k in range(r))

    def body(x_ref, dtr_ref, dtb_ref, alog_ref, dskx_ref, st_ref, dyn_ref, y_ref, z_ref, rn_ref, ng_ref,
             e_ref, et_ref, tri_ref, trit_ref, it_ref, lm_ref,
             dx_ref, ddt_ref, dbias_ref, dalog_ref, dd_ref, dz_ref, dng_ref, ds_ref, dy_ref):
        @pl.when(pl.program_id(0) == 0)
        def _():
            ds_ref[...] = jnp.zeros_like(ds_ref)
            dbias_ref[...] = jnp.zeros_like(dbias_ref)
            dalog_ref[...] = jnp.zeros_like(dalog_ref)
            dd_ref[...] = jnp.zeros_like(dd_ref)
            dng_ref[...] = jnp.zeros_like(dng_ref)

        yv, zv, rr, dn = y_ref[...], z_ref[...], rn_ref[...], dyn_ref[...]
        sz = _silu(zv)
        vhat = yv * sz * rr
        dvh = dn * ng_ref[...]
        dv = rr * (dvh - vhat * jnp.mean(dvh * vhat, axis=-1, keepdims=True))
        dy_ref[...] = dv * sz
        dz_ref[...] = (dv * yv * _dsilu(zv)).astype(BF)
        dng_ref[...] += jnp.sum(dn * vhat, axis=0, keepdims=True)

        itile = it_ref[...]
        last_row = lax.broadcasted_iota(jnp.int32, (t, HEAD_LANES), 0) == t - 1
        heads = _ssd_heads(dtr_ref, dtb_ref, alog_ref, e_ref[...], tri_ref[...], g_n, t)
        to_fold = []
        for g in range(g_n):
            xs = x_ref[:, g * rp:(g + 1) * rp]
            bm = x_ref[:, hp + g * n:hp + (g + 1) * n]
            cm = x_ref[:, hp + (g_n + g) * n:hp + (g_n + g + 1) * n]
            dy = dy_ref[:, g * rp:(g + 1) * rp]
            dtx, csx = heads[g][4:]
            ecsx, wx, elastx, lx, xdt, xblk, btile = _ssd_common(xs, bm, dtx, csx, itile, lm_ref[...], r, t)
            s_in = st_ref[g]
            ds_out = ds_ref[g]

            cbx = _bdot(cm, btile, NT)
            amat = cbx * lx
            da = _bdot(dy, xblk, NT)
            dxdt = _diagonal_blocks(_bdot(amat, dy, TN), r, t)
            dcbx = da * lx
            q = da * amat
            dc = _bdot(dcbx, btile)
            db = fold_rows(_bdot(dcbx, cm, TN), t)

            g0 = _bdot(cm, s_in)
            dg0 = dy * ecsx
            dc = dc + _bdot(dg0, s_in, NT)

            z = _bdot(bm, ds_out)
            dxdt = dxdt + z * wx
            db = db + _bdot(xdt * wx, ds_out, NT)
            ds_ref[g] = elastx * ds_out + _bdot(cm, dg0, TN)

            dx_ref[:, g * rp:(g + 1) * rp] = dskx_ref[g] * dy + dxdt * dtx
            dx_ref[:, hp + g * n:hp + (g + 1) * n] = db
            dx_ref[:, hp + (g_n + g) * n:hp + (g_n + g + 1) * n] = dc

            dwx = z * xdt * wx
            dlastx = jnp.sum(dwx, axis=0, keepdims=True) + jnp.sum(ds_out * s_in, axis=0, keepdims=True) * elastx
            rows = jnp.concatenate([jnp.sum(dy * xs, axis=0, keepdims=True), dlastx, jnp.zeros((14, rp), F32)], axis=0)
            to_fold += [q - itile * jnp.sum(q, axis=0, keepdims=True) + dy * g0 * ecsx - dwx, dxdt * xs, rows]

        folded = _rows_times(to_fold, et_ref[...], FOLD_PIECES)
        dcs = []
        for g in range(g_n):
            f_cs, f_rows = folded[3 * g], folded[3 * g + 2]
            dd_ref[g] += f_rows[0:1]
            dcs.append(f_cs + jnp.where(last_row, f_rows[1:2], 0.0))
        dadt_all = _times_rows(trit_ref[...], jnp.concatenate(dcs, axis=1), EXPAND_PIECES)
        for g in range(g_n):
            pre, dt, a = heads[g][:3]
            dadt = dadt_all[:, g * HEAD_LANES:(g + 1) * HEAD_LANES]
            dalog_ref[g] += jnp.sum(dadt * dt, axis=0, keepdims=True) * a
            dpre = (dadt * a + folded[3 * g + 1]) * (1.0 / (1.0 + jnp.exp(-pre)))
            ddt_ref[g] = dpre
            dbias_ref[g] += jnp.sum(dpre, axis=0, keepdims=True)

    par_shape = jax.ShapeDtypeStruct((g_n, 1, HEAD_LANES), F32)
    vec = pl.BlockSpec((1, hp), lambda c: (0, 0))
    return _hosted_call(
        "ssd_bwd", body, (nc,),
        [row_spec(conv), dtr_spec, par_spec, par_spec, dskx_spec, st_spec, row_spec(hp), row_spec(hp), row_spec(hp),
         row_spec(1), vec, *_const_specs(consts)],
        [row_spec(conv), dtr_spec, par_spec, par_spec, par_spec, row_spec(hp), vec],
        [jax.ShapeDtypeStruct((l, conv), F32), jax.ShapeDtypeStruct((g_n, l, HEAD_LANES), F32),
         par_shape, par_shape, par_shape, jax.ShapeDtypeStruct((l, pm.shape[1]), BF),
         jax.ShapeDtypeStruct((1, hp), F32)],
        [pltpu.VMEM((g_n, n, rp), F32), pltpu.VMEM((t, hp), F32)], ("arbitrary",),
        (xbc, dtr, dtb, alog, dskx, states, dyn, y, pm, rstd, norm_g, *consts), after=after)


POOL_HALO = 16


def _pool_mix(name, src, dp, backward, out_dtype, into=None, cw=256, tr=512):
    l = src.shape[0]
    gd = dp // len(POOL_WINDOWS)
    cwl, trl = _t(gd, cw), _t(l, tr)
    nt = l // trl

    def body(x_ref, *rest):
        o_ref = rest[-1]
        gi = pl.program_id(0)
        for wi, win in enumerate(POOL_WINDOWS):
            @pl.when(gi == wi)
            def _(win=win):
                def step(i, carry):
                    r0 = pl.multiple_of(i * trl, trl)
                    cur = x_ref[pl.ds(r0, trl), :]
                    trow = r0 + lax.broadcasted_iota(jnp.int32, (trl, 1), 0)
                    cnt = jnp.minimum(trow + 1, win).astype(F32)
                    if not backward:
                        halo = x_ref[pl.ds(pl.multiple_of(jnp.maximum(r0 - POOL_HALO, 0), POOL_HALO), POOL_HALO), :]
                        halo = jnp.where(i > 0, halo, 0.0)
                        s = jnp.concatenate([halo, cur], axis=0)
                        sh = 1
                        while sh < win:
                            s = s + pltpu.roll(s, sh, 0)
                            sh *= 2
                        res = s[POOL_HALO:] / cnt - cur
                    else:
                        halo = x_ref[pl.ds(pl.multiple_of(jnp.minimum(r0 + trl, l - POOL_HALO), POOL_HALO),
                                           POOL_HALO), :]
                        hrow = r0 + trl + lax.broadcasted_iota(jnp.int32, (POOL_HALO, 1), 0)
                        hcnt = jnp.minimum(hrow + 1, win).astype(F32)
                        halo = jnp.where(i < nt - 1, halo / hcnt, 0.0)
                        s = jnp.concatenate([cur / cnt, halo], axis=0)
                        sh = 1
                        while sh < win:
                            s = s + pltpu.roll(s, trl + POOL_HALO - sh, 0)
                            sh *= 2
                        res = s[:trl] - cur
                    o_ref[pl.ds(r0, trl), :] = res.astype(o_ref.dtype)
                    return carry
                lax.fori_loop(0, nt, step, 0)

    col = pl.BlockSpec((l, cwl), lambda g, j: (0, g * (gd // cwl) + j))
    if into is None:
        return pl.pallas_call(
            body, grid=(len(POOL_WINDOWS), gd // cwl), in_specs=[col], out_specs=col,
            out_shape=jax.ShapeDtypeStruct((l, dp), out_dtype),
            compiler_params=_params(("parallel", "parallel")), name=name)(src)
    return pl.pallas_call(
        body, grid=(len(POOL_WINDOWS), gd // cwl), in_specs=[col, pl.BlockSpec(memory_space=pl.ANY)], out_specs=col,
        out_shape=jax.ShapeDtypeStruct(into.shape, into.dtype), input_output_aliases={1: 0},
        compiler_params=_params(("parallel", "parallel")), name=name)(src, into)


def _pool_out_bwd_data(dx, wp_out, mg, pu, scale, tm=1024, tn=512):
    l, d = dx.shape
    dp = wp_out.shape[0]
    tm, tn = _t(l, tm), _t(dp, tn)

    def epi(dyp, m, gt, sc):
        sg = _silu(gt)
        return dyp * sc * sg, dyp * m * sc * _dsilu(gt), jnp.sum(dyp * m * sg, axis=0, keepdims=True)

    tile = pl.BlockSpec((tm, tn), lambda j, i, k: (i, j))
    right = pl.BlockSpec((tm, tn), lambda j, i, k: (i, dp // tn + j))
    vec = pl.BlockSpec((1, tn), lambda j, i, k: (0, j))
    return _mm("pool_out_bwd_data", (dp // tn, l // tm, 1),
               dx, pl.BlockSpec((tm, d), lambda j, i, k: (i, 0)),
               wp_out, pl.BlockSpec((tn, d), lambda j, i, k: (j, 0)), NT,
               [jax.ShapeDtypeStruct((l, dp), BF), jax.ShapeDtypeStruct((l, 2 * dp), BF),
                jax.ShapeDtypeStruct((1, dp), F32)], [tile, right, vec], (tm, tn),
               (mg, pu, scale), (tile, right, vec), epi, summed=(2, 1))


def _group_mm_fwd(mp, wg, pu, scale, tm=1024, tn=1024, tk=1024):
    l, dp = mp.shape
    ng, gd = wg.shape[0], wg.shape[1]
    tm, tn, tk = _t(l, tm), _t(gd, tn), _t(gd, tk)

    def epi(acc, gate, sc):
        return acc, acc * sc * _silu(gate)

    out = pl.BlockSpec((tm, tn), lambda g, i, j, k: (i, g * (gd // tn) + j))
    return _mm("group_mm_fwd", (ng, l // tm, gd // tn, gd // tk),
               mp, pl.BlockSpec((tm, tk), lambda g, i, j, k: (i, g * (gd // tk) + k)),
               wg, pl.BlockSpec((None, tk, tn), lambda g, i, j, k: (g, k, j)), NN,
               [jax.ShapeDtypeStruct((l, dp), F32), jax.ShapeDtypeStruct((l, dp), BF)], [out, out], (tm, tn),
               (pu, scale),
               (pl.BlockSpec((tm, tn), lambda g, i, j, k: (i, (dp + g * gd) // tn + j)),
                pl.BlockSpec((1, tn), lambda g, i, j, k: (0, g * (gd // tn) + j))), epi)


def _group_mm_bwd_data(dmg, wg, after=(), tm=1024, tn=1024, tk=1024):
    l, dp = dmg.shape
    ng, gd = wg.shape[0], wg.shape[1]
    tm, tn, tk = _t(l, tm), _t(gd, tn), _t(gd, tk)
    return _mm("group_mm_bwd_data", (ng, l // tm, gd // tn, gd // tk),
               dmg, pl.BlockSpec((tm, tk), lambda g, i, j, k: (i, g * (gd // tk) + k)),
               wg, pl.BlockSpec((None, tn, tk), lambda g, i, j, k: (g, j, k)), NT,
               [jax.ShapeDtypeStruct((l, dp), F32)],
               [pl.BlockSpec((tm, tn), lambda g, i, j, k: (i, g * (gd // tn) + j))], (tm, tn), after=after)[0]


def _group_mm_bwd_weight(mp, dmg, ng, tm=512, tk=1024):
    l, dp = mp.shape
    gd = dp // ng
    tm, tk = _t(gd, tm), _t(l, tk)
    return _mm("group_mm_bwd_weight", (ng, gd // tm, l // tk),
               mp, pl.BlockSpec((tk, tm), lambda g, i, k: (k, g * (gd // tm) + i)),
               dmg, pl.BlockSpec((tk, gd), lambda g, i, k: (k, g)), TN,
               [jax.ShapeDtypeStruct((ng, gd, gd // 2), F32)],
               [pl.BlockSpec((None, tm, gd // 2), lambda g, i, k: (g, i, 0))], (tm, gd), epi=_pack_epi)[0]


def _cast_bf16(name, w, tr=256):
    r, c = w.shape
    tr = _row_tile(r, tr, 16)

    def body(w_ref, o_ref):
        o_ref[...] = w_ref[...].astype(BF)

    blk = pl.BlockSpec((tr, c), lambda i: (i, 0))
    return pl.pallas_call(body, grid=(r // tr,), in_specs=[blk], out_specs=blk,
                          out_shape=jax.ShapeDtypeStruct((r, c), BF), compiler_params=_params(("parallel",)),
                          name=name)(w)


def _pack_rows(name, w, tr=256):
    r, c = w.shape
    tr = _row_tile(r, tr)

    def body(w_ref, o_ref):
        o_ref[...] = _pack_pairs(w_ref[...])

    return pl.pallas_call(body, grid=(r // tr, 2), in_specs=[pl.BlockSpec((tr, c // 2), lambda i, q: (i, q))],
                          out_specs=pl.BlockSpec((None, tr, c // 4), lambda i, q: (q, i, 0)),
                          out_shape=jax.ShapeDtypeStruct((2, r, c // 4), F32),
                          compiler_params=_params(("parallel", "parallel")), name=name)(w)


def _unpack_rows(name, w0, w1, tr=512):
    r, h = w0.shape
    tr = _row_tile(r, tr, 16)

    def body(w0_ref, w1_ref, o_ref):
        for q, w_ref in enumerate((w0_ref, w1_ref)):
            hi, lo = _unpack_pairs(w_ref[...])
            o_ref[q, :, :h] = hi.astype(BF)
            o_ref[q, :, h:] = lo.astype(BF)

    words = pl.BlockSpec((tr, h), lambda i: (i, 0))
    return pl.pallas_call(body, grid=(r // tr,), in_specs=[words, words],
                          out_specs=pl.BlockSpec((2, tr, 2 * h), lambda i: (0, i, 0)),
                          out_shape=jax.ShapeDtypeStruct((2, r, 2 * h), BF), compiler_params=_params(("parallel",)),
                          name=name)(w0, w1)


def _reduce_packed(name, recv, tr=256):
    nd, r, h = recv.shape
    tr = _row_tile(r, tr)

    def body(p_ref, o_ref):
        hi, lo = _unpack_pairs(p_ref[0])
        for k in range(1, nd):
            a, b = _unpack_pairs(p_ref[k])
            hi, lo = hi + a, lo + b
        o_ref[:, :h] = hi
        o_ref[:, h:] = lo

    return pl.pallas_call(body, grid=(r // tr,), in_specs=[pl.BlockSpec((nd, tr, h), lambda i: (0, i, 0))],
                          out_specs=pl.BlockSpec((tr, 2 * h), lambda i: (i, 0)),
                          out_shape=jax.ShapeDtypeStruct((r, 2 * h), F32), compiler_params=_params(("parallel",)),
                          name=name)(recv)


def _adamw_math(w, g, m, v):
    m2 = ADAM_B1 * m + (1.0 - ADAM_B1) * g
    v2 = ADAM_B2 * v + (1.0 - ADAM_B2) * (g * g)
    m_hat = m2 / (1.0 - ADAM_B1 ** ADAM_STEP)
    v_hat = v2 / (1.0 - ADAM_B2 ** ADAM_STEP)
    delta = -ADAM_LR * (m_hat / (jnp.sqrt(v_hat) + ADAM_EPS) + ADAM_WD * w)
    return delta, m2, v2


def _adamw(name, w, g, m, v, tr=256):
    r, c = w.shape
    tr = _row_tile(r, tr)

    def body(w_ref, g_ref, m_ref, v_ref, d_ref, m2_ref, v2_ref):
        d, m2, v2 = _adamw_math(w_ref[...], g_ref[...], m_ref[...], v_ref[...])
        d_ref[...] = d
        m2_ref[...] = m2
        v2_ref[...] = v2

    blk = pl.BlockSpec((tr, c), lambda i: (i, 0))
    shp = jax.ShapeDtypeStruct((r, c), F32)
    return pl.pallas_call(body, grid=(r // tr,), in_specs=[blk] * 4, out_specs=[blk] * 3, out_shape=[shp] * 3,
                          compiler_params=_params(("parallel",)), name=name)(w, g, m, v)


def _sum_slots(name, a):
    nd, r, c = a.shape

    def body(a_ref, o_ref):
        s = a_ref[0]
        for k in range(1, nd):
            s = s + a_ref[k]
        o_ref[...] = s

    return pl.pallas_call(body, out_shape=jax.ShapeDtypeStruct((r, c), F32), name=name)(a)


def _head_rows(v, g_n, r):
    return jnp.pad(v.reshape(g_n, 1, r), ((0, 0), (0, 0), (0, HEAD_LANES - r)))


class _Later:
    def __init__(self, hosted, first_of, weights_of, pool_w_out_of, send):
        self.hosted, self.first_of, self.weights_of = hosted, first_of, weights_of
        self.pool_w_out_of, self.send = pool_w_out_of, send


def _listed(res):
    return res if isinstance(res, (list, tuple)) else [res]


def _local_step(x, target, ln_g, final_g, conv_b, dt_bias, a_log, d_skip, norm_g, later):
    l, d = x.shape
    di = norm_g.shape[1]
    h = dt_bias.shape[1]
    p = di // h
    conv = conv_b.shape[1]
    n, t = SSM_STATE, SSD_CHUNK
    g_n = (conv - di) // (2 * n)
    r = h // g_n
    nm = di + conv
    ng = len(POOL_WINDOWS)
    dims = (h, g_n, r, p, n, t)
    pw = d

    hosted = later.hosted
    h0, r0, *arrived_0 = _rms_fwd("rms0_fwd", x, ln_g[0:1], exch=hosted.get("rms0"))
    wt_in, conv_w, scale = later.first_of(arrived_0)
    dp = scale.shape[1]
    pm, *arrived_i = _listed(_mm_nt_halves("in_proj_main", h0, wt_in, 0, nm, exch=hosted.get("in_proj")))
    dtr = _mm_nt_halves("in_proj_dt", h0, wt_in, nm, h, tn=h)
    xbc = _conv_fwd(pm, di, conv_w, conv_b)
    dtb, alog = _head_rows(dt_bias, g_n, r), _head_rows(a_log, g_n, r)
    dskx = jnp.repeat(d_skip.reshape(g_n, 1, r), p, axis=2)
    dtr_g = jnp.pad(jnp.transpose(dtr.reshape(l, g_n, r), (1, 0, 2)), ((0, 0), (0, 0), (0, HEAD_LANES - r)))
    y, states, yn, r_n, *arrived_s = _ssd_fwd(xbc, dtr_g, dtb, alog, dskx, pm, norm_g, dims, exch=hosted.get("ssd"))
    w_out, wg, wp_in = later.weights_of(arrived_i, arrived_s)
    x1 = _mm_nn("ssm_out_proj", yn, w_out, add=x)

    h1, r1 = _rms_fwd("rms1_fwd", x1, ln_g[1:2])
    pu, *arrived_p = _listed(_mm_nn_blocked("pool_in_proj", h1, wp_in, exch=hosted.get("pool_in_proj")))
    wp_out = later.pool_w_out_of(arrived_p)
    mp = _pool_mix("pool_mix_fwd", pu, dp, False, BF)
    mg, yp = _group_mm_fwd(mp, wg, pu, scale)
    x2 = _mm_nn("pool_out_proj", yp, wp_out, add=x1)

    dx2, dx2_b, d_final_g, loss = _loss_head(x2, final_g, target)

    gw_pout = _mm_tn_packed("pool_out_bwd_weight", yp, dx2_b, pw)
    going = later.send("pool_w_out", gw_pout)
    dmg, dpu, d_scale = _pool_out_bwd_data(dx2_b, wp_out, mg, pu, scale)
    dmp = _group_mm_bwd_data(dmg, wg, after=going)
    gw_g = _group_mm_bwd_weight(mp, dmg, ng)
    going = later.send("pool_w_group", gw_g)
    dpu = _pool_mix("pool_mix_bwd", dmp, dp, True, BF, into=dpu)
    dh1 = _mm_nt_blocked("pool_in_bwd_data", dpu, wp_in, after=going)
    gw_pin = _mm_tn_packed("pool_in_bwd_weight", h1, dpu, 2 * dp // N_DEV)
    going = later.send("pool_w_in", gw_pin)
    dx1, dx1_b, d_ln1 = _rms_bwd("rms1_bwd", dh1, x1, r1, ln_g[1:2], dx2)

    dyn = _mm_nt("ssm_out_bwd_data", dx1_b, w_out, 0, di, tn=1024, after=going)
    gw_out = _mm_tn_packed("ssm_out_bwd_weight", yn, dx1_b, pw)
    going = later.send("ssm_w_out", gw_out)
    dxbc, ddt_g, dbias_g, dalog_g, dd_g, dproj, d_norm_g = _ssd_bwd(
        xbc, dtr_g, dtb, alog, dskx, states, dyn, y, pm, r_n, norm_g, dims, after=going)
    dproj, d_conv_w, d_conv_b = _conv_bwd(pm, di, conv_w, conv_b, dxbc, dproj)
    ddt = jnp.transpose(ddt_g[:, :, :r], (1, 0, 2)).reshape(l, h)
    gwt_in, going = [], ()
    for q in range(2):
        gq = _mm_tn_packed(f"in_proj_bwd_weight_{q}", dproj, h0, d // 2, rows=nm + h, b_blk=q, after=going)
        gq = _mm_tn_packed(f"in_proj_bwd_weight_dt_{q}", ddt, h0, d // 2, into=gq, b_blk=q)
        going = later.send(f"ssm_w_in_{q}", gq)
        gwt_in.append(gq)
    gwt_in = jnp.concatenate(gwt_in, axis=0)
    dh0 = _mm_nn_halves("in_proj_bwd_data_dt", ddt, wt_in, nm, after=going)
    dh0 = _mm_nn_halves("in_proj_bwd_data", dproj, wt_in, 0, add=dh0)
    grad_x, _, d_ln0 = _rms_bwd("rms0_bwd", dh0, x, r0, ln_g[0:1], dx1)

    def heads(v):
        return v[:, 0, :r].reshape(1, h)

    small = dict(ln_g=jnp.concatenate([d_ln0, d_ln1], axis=0), final_g=d_final_g, conv_w=d_conv_w, conv_b=d_conv_b,
                 dt_bias=heads(dbias_g), a_log=heads(dalog_g), d_skip=heads(dd_g), norm_g=d_norm_g, scale=d_scale)
    big = dict(ssm_w_in=gwt_in, ssm_w_out=gw_out, pool_w_in=gw_pin, pool_w_group=gw_g, pool_w_out=gw_pout)
    return loss, grad_x, small, big


SMALL_ORDER = ("ln_g", "final_g", "conv_w", "conv_b", "dt_bias", "a_log", "d_skip", "norm_g", "scale", "loss")


def _flatten_small(parts):
    flat = jnp.concatenate([parts[k].reshape(-1) for k in SMALL_ORDER])
    n = flat.shape[0]
    rows = -(-n // 1024) * 8
    return jnp.pad(flat, (0, rows * 128 - n)).reshape(rows, 128)


def _split_small(flat, shapes):
    flat = flat.reshape(-1)
    out, off = {}, 0
    for k in SMALL_ORDER:
        size = int(np.prod(shapes[k]))
        out[k] = flat[off:off + size].reshape(shapes[k])
        off += size
    return out


def kernel(x, ln_g, final_g, ssm_w_in, ssm_conv_w, ssm_conv_b, ssm_dt_bias, ssm_a_log, ssm_d, ssm_norm_g, ssm_w_out, pool_w_in, pool_w_group, pool_scale, pool_w_out, loss_target, m_ln_g, m_final_g, m_ssm_w_in, m_ssm_conv_w, m_ssm_conv_b, m_ssm_dt_bias, m_ssm_a_log, m_ssm_d, m_ssm_norm_g, m_ssm_w_out, m_pool_w_in, m_pool_w_group, m_pool_scale, m_pool_w_out, v_ln_g, v_final_g, v_ssm_w_in, v_ssm_conv_w, v_ssm_conv_b, v_ssm_dt_bias, v_ssm_a_log, v_ssm_d, v_ssm_norm_g, v_ssm_w_out, v_pool_w_in, v_pool_w_group, v_pool_scale, v_pool_w_out):
    l, d = x.shape[1], x.shape[2]
    me = 4 * lax.axis_index("x") + 2 * lax.axis_index("y") + lax.axis_index("c")
    ng, gds, gd = pool_w_group.shape[1], pool_w_group.shape[2], pool_w_group.shape[3]
    sin_s = ssm_w_in.shape[2]
    dp_s = pool_w_out.shape[1]
    conv_s = ssm_conv_w.shape[2]

    wt_in_s = _pack_rows("pack_w_in", jnp.transpose(ssm_w_in[0]))
    w_out_s = _cast_bf16("cast_w_out", ssm_w_out[0])
    wp_in_s = _cast_bf16("cast_pool_w_in", pool_w_in[0])
    wg_s = _cast_bf16("cast_pool_w_group", pool_w_group[0].reshape(ng * gds, gd))
    wp_out_s = _cast_bf16("cast_pool_w_out", pool_w_out[0])
    small_s = jnp.concatenate([ssm_conv_w[0].reshape(-1), pool_scale[0]])
    n_small = small_s.shape[0]
    small_s = jnp.pad(small_s, (0, -(-n_small // 1024) * 1024 - n_small)).reshape(-1, 128)
    hosted = dict(rms0=_Gather([wt_in_s[0], wt_in_s[1], small_s]), in_proj=_Gather([wp_in_s]),
                  ssd=_Gather([w_out_s, wg_s]), pool_in_proj=_Gather([wp_out_s]))

    def first_of(arrived_0):
        wt_in_g0, wt_in_g1, small_g = arrived_0
        wt_in = _unpack_rows("unpack_w_in", wt_in_g0.reshape(N_DEV * sin_s, d // 4),
                             wt_in_g1.reshape(N_DEV * sin_s, d // 4))
        small_all = small_g.reshape(N_DEV, -1)[:, :n_small]
        conv_w = jnp.transpose(small_all[:, :CONV_TAPS * conv_s].reshape(N_DEV, CONV_TAPS, conv_s), (1, 0, 2))
        return wt_in, conv_w.reshape(CONV_TAPS, -1), small_all[:, CONV_TAPS * conv_s:].reshape(1, -1)

    def weights_of(arrived_i, arrived_s):
        wg = jnp.transpose(arrived_s[1].reshape(N_DEV, ng, gds, gd), (1, 0, 2, 3)).reshape(ng, gd, gd)
        return arrived_s[0].reshape(-1, d), wg, arrived_i[0]

    def rows_major(gp):
        q, _, hw = gp.shape
        return jnp.transpose(gp.reshape(q, N_DEV, -1, hw), (1, 0, 2, 3))

    to_blocks = dict(ssm_w_in_0=rows_major, ssm_w_in_1=rows_major, ssm_w_out=rows_major, pool_w_out=rows_major,
                     pool_w_in=lambda gp: gp[:, None],
                     pool_w_group=lambda gp: jnp.transpose(gp.reshape(ng, N_DEV, gds, gd // 2), (1, 0, 2, 3)))
    travelling = {}

    def send(name, gp):
        *travelling[name], token = _scatter_start(f"scatter_{name}_start", to_blocks[name](gp))
        return (token,)

    later = _Later(hosted, first_of, weights_of, lambda arrived_p: arrived_p[0].reshape(-1, d), send)

    loss, grad_x, small, _ = _local_step(
        x[0], loss_target[0], ln_g, final_g.reshape(1, d), ssm_conv_b, ssm_dt_bias, ssm_a_log, ssm_d, ssm_norm_g, later)

    def arrived(name, after):
        src, land = _scatter_wait(f"scatter_{name}_wait", *travelling[name], after=after)
        own = lax.dynamic_slice_in_dim(src, me, 1, axis=0)
        return lax.dynamic_update_slice_in_dim(land, own, me, axis=0)

    def reduced(name, rv):
        q = rv.shape[1]
        cols = [_reduce_packed(f"{name}_{j}", rv[:, j]) for j in range(q)]
        return cols[0] if q == 1 else jnp.concatenate(cols, axis=1)

    grads = {}
    grads["pool_w_out"] = reduced("reduce_pool_w_out", arrived("pool_w_out", (grad_x,)))[None]
    grads["pool_w_group"] = _reduce_packed("reduce_pool_w_group", arrived("pool_w_group", (grad_x,)).reshape(
        N_DEV, ng * gds, gd // 2)).reshape(1, ng, gds, gd)
    grads["pool_w_in"] = reduced("reduce_pool_w_in", arrived("pool_w_in", (grad_x,)))[None]
    grads["ssm_w_out"] = reduced("reduce_w_out", arrived("ssm_w_out", (grad_x,)))[None]

    weights = dict(ln_g=ln_g, final_g=final_g, ssm_w_in=ssm_w_in, ssm_conv_w=ssm_conv_w, ssm_conv_b=ssm_conv_b,
                   ssm_dt_bias=ssm_dt_bias, ssm_a_log=ssm_a_log, ssm_d=ssm_d, ssm_norm_g=ssm_norm_g,
                   ssm_w_out=ssm_w_out, pool_w_in=pool_w_in, pool_w_group=pool_w_group, pool_scale=pool_scale,
                   pool_w_out=pool_w_out)
    m_in = dict(ln_g=m_ln_g, final_g=m_final_g, ssm_w_in=m_ssm_w_in, ssm_conv_w=m_ssm_conv_w, ssm_conv_b=m_ssm_conv_b,
                ssm_dt_bias=m_ssm_dt_bias, ssm_a_log=m_ssm_a_log, ssm_d=m_ssm_d, ssm_norm_g=m_ssm_norm_g,
                ssm_w_out=m_ssm_w_out, pool_w_in=m_pool_w_in, pool_w_group=m_pool_w_group, pool_scale=m_pool_scale,
                pool_w_out=m_pool_w_out)
    v_in = dict(ln_g=v_ln_g, final_g=v_final_g, ssm_w_in=v_ssm_w_in, ssm_conv_w=v_ssm_conv_w, ssm_conv_b=v_ssm_conv_b,
                ssm_dt_bias=v_ssm_dt_bias, ssm_a_log=v_ssm_a_log, ssm_d=v_ssm_d, ssm_norm_g=v_ssm_norm_g,
                ssm_w_out=v_ssm_w_out, pool_w_in=v_pool_w_in, pool_w_group=v_pool_w_group, pool_scale=v_pool_scale,
                pool_w_out=v_pool_w_out)
    names = list(weights)
    big_names = ("ssm_w_out", "pool_w_in", "pool_w_group", "pool_w_out", "ssm_w_in")
    delta, new_m, new_v = {}, {}, {}

    def adamw_big(k):
        shp = weights[k].shape
        two_d = (-1, shp[-1])
        dk, mk, vk = _adamw(f"adamw_{k}", weights[k].reshape(two_d), grads[k].reshape(two_d), m_in[k].reshape(two_d),
                            v_in[k].reshape(two_d))
        delta[k], new_m[k], new_v[k] = dk.reshape(shp), mk.reshape(shp), vk.reshape(shp)

    for k in big_names[:-1]:
        adamw_big(k)
    small_names = [k for k in names if k not in big_names]

    small["loss"] = loss
    shapes = {k: small[k].shape for k in SMALL_ORDER}
    gathered_small, = _exchange_alone("all_gather_small_grads", _Gather([_flatten_small(small)]),
                                      after=tuple(delta[k] for k in big_names[:-1]))
    summed = _split_small(_sum_slots("sum_small_grads", gathered_small), shapes)
    grads.update(ln_g=summed["ln_g"], final_g=summed["final_g"].reshape(d), ssm_conv_b=summed["conv_b"],
                 ssm_conv_w=lax.dynamic_slice_in_dim(summed["conv_w"], me * conv_s, conv_s, axis=1)[None],
                 ssm_dt_bias=summed["dt_bias"], ssm_a_log=summed["a_log"], ssm_d=summed["d_skip"],
                 ssm_norm_g=summed["norm_g"],
                 pool_scale=lax.dynamic_slice_in_dim(summed["scale"], me * dp_s, dp_s, axis=1))

    def packed(tree):
        flat = jnp.concatenate([tree[k].reshape(-1) for k in small_names])
        nn = flat.shape[0]
        return jnp.pad(flat, (0, -(-nn // 1024) * 1024 - nn), constant_values=1.0).reshape(-1, 128)

    ds, ms, vs = _adamw("adamw_small", packed(weights), packed(grads), packed(m_in), packed(v_in))
    off = 0
    for k in small_names:
        shp = weights[k].shape
        size = int(np.prod(shp))
        for res, arr in ((delta, ds), (new_m, ms), (new_v, vs)):
            res[k] = arr.reshape(-1)[off:off + size].reshape(shp)
        off += size

    k = "ssm_w_in"
    gt = jnp.concatenate([reduced(f"reduce_w_in_{q}", arrived(f"ssm_w_in_{q}", (ds,))) for q in range(2)], axis=1)
    dk, mk, vk = _adamw("adamw_ssm_w_in", jnp.transpose(weights[k][0]), gt, jnp.transpose(m_in[k][0]),
                        jnp.transpose(v_in[k][0]))
    grads[k], delta[k], new_m[k], new_v[k] = (jnp.transpose(t)[None] for t in (gt, dk, mk, vk))

    return (summed["loss"].reshape(()), grad_x[None], *[grads[k] for k in names], *[delta[k] for k in names],
            *[new_m[k] for k in names], *[new_v[k] for k in names])
```

```python
import math

import jax
import jax.numpy as jnp
import numpy as np
from jax import lax
from jax.experimental import pallas as pl
from jax.experimental.pallas import tpu as pltpu

F32 = jnp.float32
BF = jnp.bfloat16
U32 = jnp.uint32

N_DEV = 8
EPS = 1e-6
SSD_CHUNK = 64
SSM_STATE = 128
CONV_TAPS = 4
HEAD_LANES = 128
POOL_WINDOWS = (2, 4, 8, 16)
ADAM_LR, ADAM_B1, ADAM_B2, ADAM_EPS, ADAM_WD, ADAM_STEP = 0.001, 0.9, 0.999, 1e-08, 0.01, 10
VMEM_LIMIT = 56 * 1024 * 1024
NEG_BIG = -1e30
HAND_ON_AT = 7

NN = ((1,), (0,))
NT = ((1,), (1,))
TN = ((0,), (0,))
MESH = pl.DeviceIdType.MESH


def _t(dim, pref):
    return pref if dim % pref == 0 else dim


def _row_tile(rows, pref, mult=8):
    best = rows
    for cand in range(mult, min(rows, pref) + 1, mult):
        if rows % cand == 0:
            best = cand
    return best


def _params(sem=None):
    return pltpu.CompilerParams(dimension_semantics=sem, vmem_limit_bytes=VMEM_LIMIT)


def _silu(x):
    return x * (1.0 / (1.0 + jnp.exp(-x)))


def _dsilu(x):
    s = 1.0 / (1.0 + jnp.exp(-x))
    return s * (1.0 + x * (1.0 - s))


def _bdot(a, b, dims=NN):
    return lax.dot_general(a.astype(BF), b.astype(BF), (dims, ((), ())), preferred_element_type=F32)


def _pack_pairs(x):
    h = x.shape[1] // 2
    hi = lax.bitcast_convert_type(x[:, :h].astype(jnp.bfloat16).astype(F32), U32)
    lo = lax.bitcast_convert_type(x[:, h:].astype(jnp.bfloat16).astype(F32), U32)
    return lax.bitcast_convert_type(hi | (lo >> 16), F32)


def _unpack_pairs(w):
    u = lax.bitcast_convert_type(w, U32)
    hi = lax.bitcast_convert_type(u & jnp.uint32(0xFFFF0000), F32)
    lo = lax.bitcast_convert_type(u << 16, F32)
    return hi, lo


def _mesh_pos():
    return lax.axis_index("x"), lax.axis_index("y"), lax.axis_index("c")


def _slot(pos):
    return 4 * pos[0] + 2 * pos[1] + pos[2]


class _Gather:
    def __init__(self, arrays):
        self.arrays = list(arrays)
        self.out_shapes = [jax.ShapeDtypeStruct((N_DEV, *s.shape), s.dtype) for s in arrays]

    def phases(self, src, dst, send_sems, recv_sems, local_sems):
        n_arr = len(self.arrays)
        x, y, c = _mesh_pos()
        me, sibling = (x, y, c), (x, y, 1 - c)
        chips = [(1 - x, y), (x, 1 - y), (1 - x, 1 - y)]

        def copy(a, k, block, to, from_src):
            return pltpu.make_async_remote_copy(
                src_ref=src[a] if from_src else dst[a].at[_slot(block)], dst_ref=dst[a].at[_slot(block)],
                send_sem=send_sems.at[a * 7 + k], recv_sem=recv_sems.at[a * 7 + k], device_id=to, device_id_type=MESH)

        def mine(a):
            return pltpu.make_async_copy(src[a], dst[a].at[_slot(me)], local_sems.at[a])

        def first(a):
            return [copy(a, 0, me, sibling, True)] + [copy(a, 1 + j, me, (*chip, c), True)
                                                     for j, chip in enumerate(chips)]

        def start():
            for a in range(n_arr):
                mine(a).start()
                for cp in first(a):
                    cp.start()

        def middle():
            for j, chip in enumerate(chips):
                for a in range(n_arr):
                    copy(a, 1 + j, (*chip, c), me, False).wait_recv()
                    copy(a, 4 + j, (*chip, c), sibling, False).start()

        def finish():
            for a in range(n_arr):
                copy(a, 0, sibling, me, False).wait_recv()
                for j, chip in enumerate(chips):
                    copy(a, 4 + j, (*chip, 1 - c), me, False).wait_recv()
                for cp in first(a):
                    cp.wait_send()
                for j, chip in enumerate(chips):
                    copy(a, 4 + j, (*chip, c), sibling, False).wait_send()
                mine(a).wait()

        return start, middle, finish


def _hosted_call(name, body, grid, in_specs, out_specs, out_shape, scratch_shapes, sem, operands, exch=None,
                 after=(), into=None):
    if into is not None:
        n_lead = len(in_specs)
        inner = body

        def body(*refs):
            inner(*refs[:n_lead], *refs[n_lead + 1:])

        return pl.pallas_call(
            body, grid=grid, in_specs=[*in_specs, pl.BlockSpec(memory_space=pl.ANY)], out_specs=out_specs,
            out_shape=out_shape, scratch_shapes=scratch_shapes, input_output_aliases={n_lead: 0},
            compiler_params=_params(sem), name=name)(*operands, into)
    if after:
        n_lead = len(in_specs)
        inner = body

        def body(*refs):
            inner(*refs[:n_lead], *refs[n_lead + len(after):])

        in_specs = [*in_specs, *[pl.BlockSpec(memory_space=pl.ANY)] * len(after)]
        operands = (*operands, *after)
    if exch is None:
        return pl.pallas_call(body, grid=grid, in_specs=in_specs, out_specs=out_specs, out_shape=out_shape,
                              scratch_shapes=scratch_shapes, compiler_params=_params(sem), name=name)(*operands)
    n_in, n_out, n_scr, ne = len(in_specs), len(out_specs), len(scratch_shapes), len(exch.arrays)
    total = math.prod(grid)

    def wrapped(*refs):
        ins, ex_in = refs[:n_in], refs[n_in:n_in + ne]
        outs = refs[n_in + ne:n_in + ne + n_out]
        ex_out = refs[n_in + ne + n_out:n_in + 2 * ne + n_out]
        scr = refs[n_in + 2 * ne + n_out:n_in + 2 * ne + n_out + n_scr]
        step = 0
        for axis, size in enumerate(grid):
            step = step * size + pl.program_id(axis)
        start, middle, finish = exch.phases(ex_in, ex_out, *refs[-3:])
        pl.when(step == 0)(start)
        if middle is not None:
            pl.when(step == (total * HAND_ON_AT) // 8)(middle)
        body(*ins, *outs, *scr)
        pl.when(step == total - 1)(finish)

    hbm = pl.BlockSpec(memory_space=pl.ANY)
    sems = [pltpu.SemaphoreType.DMA((ne * 7,)), pltpu.SemaphoreType.DMA((ne * 7,)), pltpu.SemaphoreType.DMA((ne,))]
    return pl.pallas_call(
        wrapped, grid=grid, in_specs=[*in_specs, *[hbm] * ne], out_specs=[*out_specs, *[hbm] * ne],
        out_shape=[*out_shape, *exch.out_shapes], scratch_shapes=[*scratch_shapes, *sems],
        compiler_params=pltpu.CompilerParams(dimension_semantics=("arbitrary",) * len(grid),
                                             vmem_limit_bytes=VMEM_LIMIT, has_side_effects=True),
        name=name)(*operands, *exch.arrays)


def _flip_peers():
    x, y, c = _mesh_pos()
    peers = []
    for k in range(1, N_DEV):
        fx, fy, fc = (k >> 2) & 1, (k >> 1) & 1, k & 1
        peers.append((1 - x if fx else x, 1 - y if fy else y, 1 - c if fc else c))
    return (x, y, c), peers


def _split_scatter_copy(src, land, send_sems, recv_sems, k, me, peer, sending):
    return pltpu.make_async_remote_copy(
        src_ref=src.at[_slot(peer)], dst_ref=land.at[_slot(me) if sending else _slot(peer)],
        send_sem=send_sems.at[k], recv_sem=recv_sems.at[k], device_id=peer, device_id_type=MESH)


def _scatter_start(name, blocks):
    def body(src, land, send_sems, recv_sems, src_thru, land_thru, token):
        me, peers = _flip_peers()
        for k, peer in enumerate(peers):
            _split_scatter_copy(src, land, send_sems, recv_sems, k, me, peer, True).start()
        token[...] = jnp.zeros_like(token)

    hbm = pl.BlockSpec(memory_space=pltpu.HBM)
    sem = pl.BlockSpec(memory_space=pltpu.SEMAPHORE)
    return pl.pallas_call(
        body, name=name,
        out_shape=(pltpu.SemaphoreType.DMA((N_DEV - 1,)), pltpu.SemaphoreType.DMA((N_DEV - 1,)),
                   pltpu.HBM(blocks.shape, blocks.dtype), pltpu.HBM(blocks.shape, blocks.dtype),
                   jax.ShapeDtypeStruct((8, 128), F32)),
        in_specs=(hbm, hbm), out_specs=(sem, sem, hbm, hbm, pl.BlockSpec(memory_space=pltpu.VMEM)),
        input_output_aliases={0: 2, 1: 3},
        compiler_params=pltpu.CompilerParams(has_side_effects=pltpu.SideEffectType.DATAFLOW_SIDE_EFFECTING),
    )(pltpu.with_memory_space_constraint(blocks, pltpu.HBM),
      pltpu.with_memory_space_constraint(lax.empty(blocks.shape, blocks.dtype), pltpu.HBM))


def _scatter_wait(name, send_sems, recv_sems, src_thru, land_thru, after):
    def body(src, land, send_sems, recv_sems, *rest):
        me, peers = _flip_peers()
        for k, peer in enumerate(peers):
            _split_scatter_copy(src, land, send_sems, recv_sems, k, me, peer, True).wait_send()
            _split_scatter_copy(src, land, send_sems, recv_sems, k, me, peer, False).wait_recv()

    hbm = pl.BlockSpec(memory_space=pltpu.HBM)
    sem = pl.BlockSpec(memory_space=pltpu.SEMAPHORE)
    return pl.pallas_call(
        body, name=name,
        out_shape=(pltpu.HBM(src_thru.shape, src_thru.dtype), pltpu.HBM(land_thru.shape, land_thru.dtype)),
        in_specs=(hbm, hbm, sem, sem, *[pl.BlockSpec(memory_space=pl.ANY)] * len(after)), out_specs=(hbm, hbm),
        input_output_aliases={0: 0, 1: 1},
        compiler_params=pltpu.CompilerParams(has_side_effects=pltpu.SideEffectType.DATAFLOW_SIDE_EFFECTING),
    )(src_thru, land_thru, send_sems, recv_sems, *after)


def _exchange_alone(name, exch, after=()):
    def body():
        pass

    return _hosted_call(name, body, (1,), [], [], [], [], None, (), exch, after)


def _mm(name, grid, a, a_spec, b, b_spec, dims, outs, o_specs, acc_shape, extra=(), extra_specs=(), epi=None,
        exch=None, after=(), into=None, summed=None):
    nk = grid[-1]
    n_extra, n_out = len(extra), len(outs)

    def body(*refs):
        a_ref, b_ref = refs[0], refs[1]
        ex = refs[2:2 + n_extra]
        o_refs = refs[2 + n_extra:2 + n_extra + n_out]

        def write(res):
            res = (res,) if epi is None else epi(res, *[e[...] for e in ex])
            for idx, (o, r) in enumerate(zip(o_refs, res)):
                if summed is not None and idx == summed[0]:
                    first = pl.program_id(summed[1]) == 0

                    @pl.when(first)
                    def _(o=o, r=r):
                        o[...] = r.astype(o.dtype)

                    @pl.when(jnp.logical_not(first))
                    def _(o=o, r=r):
                        o[...] += r.astype(o.dtype)
                else:
                    o[...] = r.astype(o.dtype)

        if nk == 1:
            write(_bdot(a_ref[...], b_ref[...], dims))
            return
        acc = refs[-1]
        k = pl.program_id(len(grid) - 1)

        @pl.when(k == 0)
        def _():
            acc[...] = _bdot(a_ref[...], b_ref[...], dims)

        @pl.when(jnp.logical_and(k > 0, k < nk - 1))
        def _():
            acc[...] += _bdot(a_ref[...], b_ref[...], dims)

        @pl.when(k == nk - 1)
        def _():
            write(acc[...] + _bdot(a_ref[...], b_ref[...], dims))

    sem = ("parallel",) * (len(grid) - 1) + ("arbitrary",)
    if summed is not None:
        sem = ("arbitrary",) * len(grid)
    scratch = [] if nk == 1 else [pltpu.VMEM(acc_shape, F32)]
    return _hosted_call(name, body, grid, [a_spec, b_spec, *extra_specs], list(o_specs), list(outs),
                        scratch, sem, (a, b, *extra), exch, after, into)


def _add_epi(acc, add):
    return (acc + add,)


def _pack_epi(acc):
    return (_pack_pairs(acc),)


def _mm_nt(name, a, b, n_off, n, out_dtype=F32, tm=1024, tn=512, exch=None, after=()):
    m, kk = a.shape
    tm, tn = _t(m, tm), math.gcd(_t(n, tn), n_off)
    res = _mm(name, (m // tm, n // tn, 1),
              a, pl.BlockSpec((tm, kk), lambda i, j, k: (i, 0)),
              b, pl.BlockSpec((tn, kk), lambda i, j, k: (n_off // tn + j, 0)), NT,
              [jax.ShapeDtypeStruct((m, n), out_dtype)], [pl.BlockSpec((tm, tn), lambda i, j, k: (i, j))], (tm, tn),
              exch=exch, after=after)
    return res[0] if exch is None else res


def _mm_nt_halves(name, a, b2, n_off, n, tm=1024, tn=512, exch=None):
    m = a.shape[0]
    kh = b2.shape[2]
    tm, tn = _t(m, tm), math.gcd(_t(n, tn), n_off)

    def body(a_ref, b_ref, o_ref):
        o_ref[...] = _bdot(a_ref[:, :kh], b_ref[0], NT) + _bdot(a_ref[:, kh:], b_ref[1], NT)

    res = _hosted_call(name, body, (m // tm, n // tn),
                       [pl.BlockSpec((tm, 2 * kh), lambda i, j: (i, 0)),
                        pl.BlockSpec((2, tn, kh), lambda i, j: (0, n_off // tn + j, 0))],
                       [pl.BlockSpec((tm, tn), lambda i, j: (i, j))], [jax.ShapeDtypeStruct((m, n), F32)], [],
                       ("parallel", "parallel"), (a, b2), exch)
    return res[0] if exch is None else res


def _mm_nn_halves(name, a, b2, k_off=0, add=None, after=(), tm=1024, tk=1024):
    m, kk = a.shape
    half = b2.shape[2]
    tm, tk = _t(m, tm), math.gcd(_t(kk, tk), k_off)
    extra, especs, epi = (), (), None
    if add is not None:
        extra, especs, epi = (add,), (pl.BlockSpec((tm, half), lambda i, j, k: (i, j)),), _add_epi
    return _mm(name, (m // tm, 2, kk // tk),
               a, pl.BlockSpec((tm, tk), lambda i, j, k: (i, k)),
               b2, pl.BlockSpec((None, tk, half), lambda i, j, k: (j, k_off // tk + k, 0)), NN,
               [jax.ShapeDtypeStruct((m, 2 * half), F32)], [pl.BlockSpec((tm, half), lambda i, j, k: (i, j))],
               (tm, half), extra, especs, epi, after=after)[0]


def _mm_nn_blocked(name, a, b3, tm=1024, tn=512, exch=None):
    m, kk = a.shape
    nb, _, cs = b3.shape
    tm, tn = _t(m, tm), _t(cs, tn)
    per = cs // tn
    res = _mm(name, (m // tm, nb * per, 1),
              a, pl.BlockSpec((tm, kk), lambda i, j, k: (i, 0)),
              b3, pl.BlockSpec((None, kk, tn), lambda i, j, k: (j // per, 0, j % per)), NN,
              [jax.ShapeDtypeStruct((m, nb * cs), F32)], [pl.BlockSpec((tm, tn), lambda i, j, k: (i, j))],
              (tm, tn), exch=exch)
    return res[0] if exch is None else res


def _mm_nt_blocked(name, a, b3, after=(), tm=1024, tn=1024):
    m = a.shape[0]
    nb, n, cs = b3.shape
    tm, tn = _t(m, tm), _t(n, tn)
    return _mm(name, (m // tm, n // tn, nb),
               a, pl.BlockSpec((tm, cs), lambda i, j, k: (i, k)),
               b3, pl.BlockSpec((None, tn, cs), lambda i, j, k: (k, j, 0)), NT,
               [jax.ShapeDtypeStruct((m, n), F32)], [pl.BlockSpec((tm, tn), lambda i, j, k: (i, j))], (tm, tn),
               after=after)[0]


def _mm_nn(name, a, b, k_off=0, kk=None, add=None, out_dtype=F32, tm=1024, tn=1024, tk=1024, after=()):
    m = a.shape[0]
    kk = a.shape[1] if kk is None else kk
    n = b.shape[1]
    tm, tn, tk = _t(m, tm), _t(n, tn), math.gcd(_t(kk, tk), k_off)
    extra, especs, epi = (), (), None
    if add is not None:
        extra, especs, epi = (add,), (pl.BlockSpec((tm, tn), lambda i, j, k: (i, j)),), _add_epi
    return _mm(name, (m // tm, n // tn, kk // tk),
               a, pl.BlockSpec((tm, tk), lambda i, j, k: (i, k)),
               b, pl.BlockSpec((tk, tn), lambda i, j, k: (k_off // tk + k, j)), NN,
               [jax.ShapeDtypeStruct((m, n), out_dtype)], [pl.BlockSpec((tm, tn), lambda i, j, k: (i, j))], (tm, tn),
               extra, especs, epi, after=after)[0]


TN_ACC_ELEMENTS = 1 << 20


def _mm_tn_packed(name, a, b, pw, tk=1024, rows=None, into=None, b_blk=None, after=()):
    kk, m = a.shape
    nq = b.shape[1] // pw if b_blk is None else 1
    first = 0 if b_blk is None else b_blk
    tm, tk = _t(m, TN_ACC_ELEMENTS // pw), _t(kk, tk)
    rows = m if rows is None else rows
    row_blk = 0
    if into is not None:
        rows = into.shape[1]
        assert (rows - m) % tm == 0
        row_blk = (rows - m) // tm
    return _mm(name, (m // tm, nq, kk // tk),
               a, pl.BlockSpec((tk, tm), lambda i, j, k: (k, i)),
               b, pl.BlockSpec((tk, pw), lambda i, j, k: (k, first + j)), TN,
               [jax.ShapeDtypeStruct((nq, rows, pw // 2), F32)],
               [pl.BlockSpec((None, tm, pw // 2), lambda i, j, k: (j, row_blk + i, 0))], (tm, pw), epi=_pack_epi,
               into=into, after=after)[0]


def _rms_fwd(name, x, g, tm=256, exch=None):
    l, d = x.shape
    tm = _t(l, tm)

    def body(x_ref, g_ref, h_ref, r_ref):
        xv = x_ref[...]
        r = lax.rsqrt(jnp.mean(xv * xv, axis=-1, keepdims=True) + EPS)
        h_ref[...] = (xv * r * g_ref[...]).astype(BF)
        r_ref[...] = r

    return _hosted_call(
        name, body, (l // tm,),
        [pl.BlockSpec((tm, d), lambda i: (i, 0)), pl.BlockSpec((1, d), lambda i: (0, 0))],
        [pl.BlockSpec((tm, d), lambda i: (i, 0)), pl.BlockSpec((tm, 1), lambda i: (i, 0))],
        [jax.ShapeDtypeStruct((l, d), BF), jax.ShapeDtypeStruct((l, 1), F32)], [], ("parallel",), (x, g), exch)


def _rms_bwd(name, dh, x, r, g, dres, tm=256):
    l, d = x.shape
    tm = _t(l, tm)

    def body(dh_ref, x_ref, r_ref, g_ref, dres_ref, dx_ref, dxb_ref, dg_ref):
        i = pl.program_id(0)
        rr = r_ref[...]
        xhat = x_ref[...] * rr
        dhv = dh_ref[...]
        dxh = dhv * g_ref[...]
        dx = dres_ref[...] + rr * (dxh - xhat * jnp.mean(dxh * xhat, axis=-1, keepdims=True))
        dx_ref[...] = dx
        dxb_ref[...] = dx.astype(BF)

        @pl.when(i == 0)
        def _():
            dg_ref[...] = jnp.zeros_like(dg_ref)

        dg_ref[...] += jnp.sum(dhv * xhat, axis=0, keepdims=True)

    row = pl.BlockSpec((tm, d), lambda i: (i, 0))
    vec = pl.BlockSpec((1, d), lambda i: (0, 0))
    return pl.pallas_call(
        body, grid=(l // tm,), in_specs=[row, row, pl.BlockSpec((tm, 1), lambda i: (i, 0)), vec, row],
        out_specs=[row, row, vec],
        out_shape=[jax.ShapeDtypeStruct((l, d), F32), jax.ShapeDtypeStruct((l, d), BF),
                   jax.ShapeDtypeStruct((1, d), F32)],
        compiler_params=_params(("arbitrary",)), name=name)(dh, x, r, g, dres)


def _loss_head(x, g, target, tm=256):
    l, d = x.shape
    tm = _t(l, tm)

    def body(x_ref, g_ref, t_ref, dx_ref, dxb_ref, dg_ref, loss_ref):
        i = pl.program_id(0)
        xv = x_ref[...]
        gv = g_ref[...]
        r = lax.rsqrt(jnp.mean(xv * xv, axis=-1, keepdims=True) + EPS)
        xhat = xv * r
        e = xhat * gv - t_ref[...]
        dy = e * (1.0 / d)
        dxh = dy * gv
        dx = r * (dxh - xhat * jnp.mean(dxh * xhat, axis=-1, keepdims=True))
        dx_ref[...] = dx
        dxb_ref[...] = dx.astype(BF)

        @pl.when(i == 0)
        def _():
            dg_ref[...] = jnp.zeros_like(dg_ref)
            loss_ref[...] = jnp.zeros_like(loss_ref)

        dg_ref[...] += jnp.sum(dy * xhat, axis=0, keepdims=True)
        loss_ref[...] += 0.5 * jnp.sum(jnp.sum(e * e, axis=-1, keepdims=True) * (1.0 / d), axis=0, keepdims=True)

    row = pl.BlockSpec((tm, d), lambda i: (i, 0))
    vec = pl.BlockSpec((1, d), lambda i: (0, 0))
    return pl.pallas_call(
        body, grid=(l // tm,), in_specs=[row, vec, row],
        out_specs=[row, row, vec, pl.BlockSpec((1, 1), lambda i: (0, 0))],
        out_shape=[jax.ShapeDtypeStruct((l, d), F32), jax.ShapeDtypeStruct((l, d), BF),
                   jax.ShapeDtypeStruct((1, d), F32), jax.ShapeDtypeStruct((1, 1), F32)],
        compiler_params=_params(("arbitrary",)), name="loss_head")(x, g, target)


CONV_HALO = 8


def _conv_pre(x_ref, w_ref, b_ref, i, tr):
    r0 = pl.multiple_of(i * tr, tr)
    cur = x_ref[pl.ds(r0, tr), :]
    prev = x_ref[pl.ds(pl.multiple_of(jnp.maximum(r0 - CONV_HALO, 0), CONV_HALO), CONV_HALO), :]
    prev = jnp.where(i > 0, prev, 0.0)
    ext = jnp.concatenate([prev, cur], axis=0)
    taps = []
    for k in range(CONV_TAPS):
        s = CONV_TAPS - 1 - k
        taps.append(cur if s == 0 else pltpu.roll(ext, s, 0)[CONV_HALO:])
    pre = b_ref[...] + sum(w_ref[k:k + 1, :] * taps[k] for k in range(CONV_TAPS))
    return r0, pre, taps


def _conv_fwd(pm, col_off, conv_w, conv_b, cw=256, tr=512):
    l = pm.shape[0]
    c = conv_w.shape[1]
    cw, tr = _t(c, cw), _t(l, tr)
    assert col_off % cw == 0

    def body(x_ref, w_ref, b_ref, o_ref):
        def step(i, carry):
            r0, pre, _ = _conv_pre(x_ref, w_ref, b_ref, i, tr)
            o_ref[pl.ds(r0, tr), :] = _silu(pre)
            return carry
        lax.fori_loop(0, l // tr, step, 0)

    return pl.pallas_call(
        body, grid=(c // cw,),
        in_specs=[pl.BlockSpec((l, cw), lambda j: (0, col_off // cw + j)),
                  pl.BlockSpec((CONV_TAPS, cw), lambda j: (0, j)), pl.BlockSpec((1, cw), lambda j: (0, j))],
        out_specs=pl.BlockSpec((l, cw), lambda j: (0, j)), out_shape=jax.ShapeDtypeStruct((l, c), F32),
        compiler_params=_params(("parallel",)), name="conv_fwd")(pm, conv_w, conv_b)


def _conv_bwd(pm, col_off, conv_w, conv_b, dy, dproj, cw=256, tr=512):
    l = pm.shape[0]
    c = conv_w.shape[1]
    cw, tr = _t(c, cw), _t(l, tr)
    nt = l // tr

    def body(x_ref, w_ref, b_ref, dy_ref, _, dx_ref, dw_ref, db_ref, dpre_ref):
        def step1(i, carry):
            dws, db = carry
            r0, pre, taps = _conv_pre(x_ref, w_ref, b_ref, i, tr)
            dpre = dy_ref[pl.ds(r0, tr), :] * _dsilu(pre)
            dpre_ref[pl.ds(r0, tr), :] = dpre
            dws = tuple(dws[k] + jnp.sum(dpre * taps[k], axis=0, keepdims=True) for k in range(CONV_TAPS))
            return dws, db + jnp.sum(dpre, axis=0, keepdims=True)

        z = jnp.zeros((1, cw), F32)
        dws, db = lax.fori_loop(0, nt, step1, ((z,) * CONV_TAPS, z))
        for k in range(CONV_TAPS):
            dw_ref[k:k + 1, :] = dws[k]
        db_ref[...] = db

        def step2(i, carry):
            r0 = pl.multiple_of(i * tr, tr)
            cur = dpre_ref[pl.ds(r0, tr), :]
            nxt = dpre_ref[pl.ds(pl.multiple_of(jnp.minimum(r0 + tr, l - CONV_HALO), CONV_HALO), CONV_HALO), :]
            nxt = jnp.where(i < nt - 1, nxt, 0.0)
            ext = jnp.concatenate([cur, nxt], axis=0)
            acc = w_ref[CONV_TAPS - 1:CONV_TAPS, :] * cur
            for k in range(CONV_TAPS - 1):
                s = CONV_TAPS - 1 - k
                acc = acc + w_ref[k:k + 1, :] * pltpu.roll(ext, tr + CONV_HALO - s, 0)[:tr]
            dx_ref[pl.ds(r0, tr), :] = acc.astype(dx_ref.dtype)
            return carry
        lax.fori_loop(0, nt, step2, 0)

    col = pl.BlockSpec((l, cw), lambda j: (0, j))
    shifted = pl.BlockSpec((l, cw), lambda j: (0, col_off // cw + j))
    return pl.pallas_call(
        body, grid=(c // cw,),
        in_specs=[shifted, pl.BlockSpec((CONV_TAPS, cw), lambda j: (0, j)), pl.BlockSpec((1, cw), lambda j: (0, j)),
                  col, pl.BlockSpec(memory_space=pl.ANY)],
        out_specs=[shifted, pl.BlockSpec((CONV_TAPS, cw), lambda j: (0, j)), pl.BlockSpec((1, cw), lambda j: (0, j))],
        out_shape=[jax.ShapeDtypeStruct(dproj.shape, dproj.dtype), jax.ShapeDtypeStruct((CONV_TAPS, c), F32),
                   jax.ShapeDtypeStruct((1, c), F32)],
        scratch_shapes=[pltpu.VMEM((l, cw), F32)], input_output_aliases={4: 0},
        compiler_params=_params(("parallel",)), name="conv_bwd")(pm, conv_w, conv_b, dy, dproj)


def _split(x, pieces):
    out, rest = [], x
    for _ in range(pieces):
        piece = rest.astype(BF)
        out.append(piece)
        rest = rest - piece.astype(F32)
    return out


def _rows_times(xs, m_stack, pieces):
    x = xs[0] if len(xs) == 1 else jnp.concatenate(xs, axis=0)
    out = _bdot(jnp.concatenate(_split(x, pieces), axis=1), m_stack)
    sizes = [v.shape[0] for v in xs]
    offs = np.cumsum([0] + sizes)
    return [out[offs[i]:offs[i + 1]] for i in range(len(xs))]


def _times_rows(m_stack, x, pieces):
    return _bdot(m_stack, jnp.concatenate(_split(x, pieces), axis=0))


EXPAND_PIECES = 3
FOLD_PIECES = 2


def _ssd_consts(r, p, t):
    assert p == t, "heads expand to P lanes of the inputs and to T lanes of the decay matrices alike"
    assert LANES % p == 0 and (r * p) % LANES == 0, "whole heads per lane tile, whole lane tiles per group"
    rp = r * p
    tri = np.tril(np.ones((t, t), np.float32))
    ep = np.zeros((HEAD_LANES, rp), np.float32)
    ep[np.arange(rp) // p, np.arange(rp)] = 1.0
    itile = (np.arange(t)[:, None] == (np.arange(rp) % t)[None, :]).astype(np.float32)
    lmask = (np.arange(t)[:, None] >= (np.arange(rp) % t)[None, :]).astype(np.float32)
    return [jnp.asarray(np.concatenate([ep] * EXPAND_PIECES, axis=0), BF),
            jnp.asarray(np.concatenate([ep.T] * FOLD_PIECES, axis=0), BF),
            jnp.asarray(np.concatenate([tri] * EXPAND_PIECES, axis=1), BF),
            jnp.asarray(np.concatenate([tri.T] * EXPAND_PIECES, axis=1), BF),
            jnp.asarray(itile), jnp.asarray(lmask)]


def _ssd_heads(dtr_ref, dtb_ref, alog_ref, e_stack, tri_stack, g_n, t):
    pre = [dtr_ref[g] + dtb_ref[g] for g in range(g_n)]
    dt = [jnp.maximum(v, 0.0) + jnp.log(1.0 + jnp.exp(-jnp.abs(v))) for v in pre]
    a = [-jnp.exp(alog_ref[g]) for g in range(g_n)]
    adt = jnp.concatenate([dt[g] * a[g] for g in range(g_n)], axis=1)
    cs = _times_rows(tri_stack, adt, EXPAND_PIECES)
    cs = [cs[:, g * HEAD_LANES:(g + 1) * HEAD_LANES] for g in range(g_n)]
    at_lanes = _rows_times([v for g in range(g_n) for v in (dt[g], cs[g])], e_stack, EXPAND_PIECES)
    return [(pre[g], dt[g], a[g], cs[g][t - 1:t, :], at_lanes[2 * g], at_lanes[2 * g + 1]) for g in range(g_n)]


LANES = 128


def _head_lanes(k, p, rows):
    lane = lax.broadcasted_iota(jnp.int32, (rows, LANES), 1)
    j = k % (LANES // p)
    return jnp.logical_and(lane >= j * p, lane < (j + 1) * p)


def _block_diagonal(v, r, p):
    rows = []
    for k in range(r):
        c = k * p // LANES
        tiles = [jnp.zeros((v.shape[0], LANES), v.dtype)] * (r * p // LANES)
        tiles[c] = jnp.where(_head_lanes(k, p, v.shape[0]), v[:, c * LANES:(c + 1) * LANES], jnp.zeros((), v.dtype))
        rows.append(jnp.concatenate(tiles, axis=1))
    return jnp.concatenate(rows, axis=0)


def _diagonal_blocks(m, r, p):
    t = m.shape[0] // r
    tiles = []
    for c in range(r * p // LANES):
        heads = range(c * LANES // p, (c + 1) * LANES // p)
        acc = None
        for k in heads:
            blk = m[k * t:(k + 1) * t, c * LANES:(c + 1) * LANES]
            acc = blk if acc is None else jnp.where(_head_lanes(k, p, t), blk, acc)
        tiles.append(acc)
    return jnp.concatenate(tiles, axis=1)


def _ssd_common(xs, bm, dtx, csx, itile, lmask, r, t):
    ecsx = jnp.exp(csx)
    lastx = csx[t - 1:t, :]
    wx = jnp.exp(lastx - csx)
    elastx = jnp.exp(lastx)
    csrow = jnp.sum(csx * itile, axis=0, keepdims=True)
    lx = jnp.exp(jnp.where(lmask > 0.0, csx - csrow, NEG_BIG))
    xdt = xs * dtx
    xblk = _block_diagonal(xdt.astype(BF), r, t)
    btile = jnp.concatenate([bm.astype(BF)] * r, axis=0)
    return ecsx, wx, elastx, lx, xdt, xblk, btile


def _ssd_specs(l, g_n, r, p, n, t, conv, rev):
    nc = l // t
    rp = r * p
    cidx = (lambda c: nc - 1 - c) if rev else (lambda c: c)
    row_spec = lambda width: pl.BlockSpec((t, width), lambda c: (cidx(c), 0))
    dtr_spec = pl.BlockSpec((g_n, t, HEAD_LANES), lambda c: (0, cidx(c), 0))
    par_spec = pl.BlockSpec((g_n, 1, HEAD_LANES), lambda c: (0, 0, 0))
    dskx_spec = pl.BlockSpec((g_n, 1, rp), lambda c: (0, 0, 0))
    st_spec = pl.BlockSpec((None, g_n, n, rp), lambda c: (cidx(c), 0, 0, 0))
    return nc, row_spec, dtr_spec, par_spec, dskx_spec, st_spec


def _const_specs(consts):
    return [pl.BlockSpec(a.shape, lambda c: (0, 0)) for a in consts]


def _ssd_fwd(xbc, dtr, dtb, alog, dskx, pm, norm_g, dims, exch=None):
    h, g_n, r, p, n, t = dims
    l, conv = xbc.shape
    rp, hp = r * p, h * p
    nc, row_spec, dtr_spec, par_spec, dskx_spec, st_spec = _ssd_specs(l, g_n, r, p, n, t, conv, False)
    consts = _ssd_consts(r, p, t)

    def body(x_ref, dtr_ref, dtb_ref, alog_ref, dskx_ref, z_ref, ng_ref,
             e_ref, et_ref, tri_ref, trit_ref, it_ref, lm_ref,
             y_ref, st_ref, yn_ref, rn_ref, s_ref):
        @pl.when(pl.program_id(0) == 0)
        def _():
            s_ref[...] = jnp.zeros_like(s_ref)

        heads = _ssd_heads(dtr_ref, dtb_ref, alog_ref, e_ref[...], tri_ref[...], g_n, t)
        for g in range(g_n):
            xs = x_ref[:, g * rp:(g + 1) * rp]
            bm = x_ref[:, hp + g * n:hp + (g + 1) * n]
            cm = x_ref[:, hp + (g_n + g) * n:hp + (g_n + g + 1) * n]
            dtx, csx = heads[g][4:]
            ecsx, wx, elastx, lx, xdt, xblk, btile = _ssd_common(xs, bm, dtx, csx, it_ref[...], lm_ref[...], r, t)
            s_in = s_ref[g]
            st_ref[g] = s_in
            cbx = _bdot(cm, btile, NT)
            yd = _bdot(cbx * lx, xblk)
            yo = ecsx * _bdot(cm, s_in)
            y_ref[:, g * rp:(g + 1) * rp] = yd + yo + dskx_ref[g] * xs
            s_ref[g] = elastx * s_in + _bdot(bm, xdt * wx, TN)

        v = y_ref[...] * _silu(z_ref[...])
        rr = lax.rsqrt(jnp.mean(v * v, axis=-1, keepdims=True) + EPS)
        yn_ref[...] = (v * rr * ng_ref[...]).astype(BF)
        rn_ref[...] = rr

    return _hosted_call(
        "ssd_fwd", body, (nc,),
        [row_spec(conv), dtr_spec, par_spec, par_spec, dskx_spec, row_spec(hp),
         pl.BlockSpec((1, hp), lambda c: (0, 0)), *_const_specs(consts)],
        [row_spec(hp), st_spec, row_spec(hp), row_spec(1)],
        [jax.ShapeDtypeStruct((l, hp), F32), jax.ShapeDtypeStruct((nc, g_n, n, rp), F32),
         jax.ShapeDtypeStruct((l, hp), BF), jax.ShapeDtypeStruct((l, 1), F32)],
        [pltpu.VMEM((g_n, n, rp), F32)], ("arbitrary",),
        (xbc, dtr, dtb, alog, dskx, pm, norm_g, *consts), exch)


def _ssd_bwd(xbc, dtr, dtb, alog, dskx, states, dyn, y, pm, rstd, norm_g, dims, after=()):
    h, g_n, r, p, n, t = dims
    l, conv = xbc.shape
    rp, hp = r * p, h * p
    nc, row_spec, dtr_spec, par_spec, dskx_spec, st_spec = _ssd_specs(l, g_n, r, p, n, t, conv, True)
    consts = _ssd_consts(r, p, t)

    def fold_rows(v, rows):
        return sum(v[k * rows:(k + 1) * rows, :] for k in range(r))

    def body(x_ref, dtr_ref, dtb_ref, alog_ref, dskx_ref, st_ref, dyn_ref, y_ref, z_ref, rn_ref, ng_ref,
             e_ref, et_ref, tri_ref, trit_ref, it_ref, lm_ref,
             dx_ref, ddt_ref, dbias_ref, dalog_ref, dd_ref, dz_ref, dng_ref, ds_ref, dy_ref):
        @pl.when(pl.program_id(0) == 0)
        def _():
            ds_ref[...] = jnp.zeros_like(ds_ref)
            dbias_ref[...] = jnp.zeros_like(dbias_ref)
            dalog_ref[...] = jnp.zeros_like(dalog_ref)
            dd_ref[...] = jnp.zeros_like(dd_ref)
            dng_ref[...] = jnp.zeros_like(dng_ref)

        yv, zv, rr, dn = y_ref[...], z_ref[...], rn_ref[...], dyn_ref[...]
        sz = _silu(zv)
        vhat = yv * sz * rr
        dvh = dn * ng_ref[...]
        dv = rr * (dvh - vhat * jnp.mean(dvh * vhat, axis=-1, keepdims=True))
        dy_ref[...] = dv * sz
        dz_ref[...] = (dv * yv * _dsilu(zv)).astype(BF)
        dng_ref[...] += jnp.sum(dn * vhat, axis=0, keepdims=True)

        itile = it_ref[...]
        last_row = lax.broadcasted_iota(jnp.int32, (t, HEAD_LANES), 0) == t - 1
        heads = _ssd_heads(dtr_ref, dtb_ref, alog_ref, e_ref[...], tri_ref[...], g_n, t)
        to_fold = []
        for g in range(g_n):
            xs = x_ref[:, g * rp:(g + 1) * rp]
            bm = x_ref[:, hp + g * n:hp + (g + 1) * n]
            cm = x_ref[:, hp + (g_n + g) * n:hp + (g_n + g + 1) * n]
            dy = dy_ref[:, g * rp:(g + 1) * rp]
            dtx, csx = heads[g][4:]
            ecsx, wx, elastx, lx, xdt, xblk, btile = _ssd_common(xs, bm, dtx, csx, itile, lm_ref[...], r, t)
            s_in = st_ref[g]
            ds_out = ds_ref[g]

            cbx = _bdot(cm, btile, NT)
            amat = cbx * lx
            da = _bdot(dy, xblk, NT)
            dxdt = _diagonal_blocks(_bdot(amat, dy, TN), r, t)
            dcbx = da * lx
            q = da * amat
            dc = _bdot(dcbx, btile)
            db = fold_rows(_bdot(dcbx, cm, TN), t)

            g0 = _bdot(cm, s_in)
            dg0 = dy * ecsx
            dc = dc + _bdot(dg0, s_in, NT)

            z = _bdot(bm, ds_out)
            dxdt = dxdt + z * wx
            db = db + _bdot(xdt * wx, ds_out, NT)
            ds_ref[g] = elastx * ds_out + _bdot(cm, dg0, TN)

            dx_ref[:, g * rp:(g + 1) * rp] = dskx_ref[g] * dy + dxdt * dtx
            dx_ref[:, hp + g * n:hp + (g + 1) * n] = db
            dx_ref[:, hp + (g_n + g) * n:hp + (g_n + g + 1) * n] = dc

            dwx = z * xdt * wx
            dlastx = jnp.sum(dwx, axis=0, keepdims=True) + jnp.sum(ds_out * s_in, axis=0, keepdims=True) * elastx
            rows = jnp.concatenate([jnp.sum(dy * xs, axis=0, keepdims=True), dlastx, jnp.zeros((14, rp), F32)], axis=0)
            to_fold += [q - itile * jnp.sum(q, axis=0, keepdims=True) + dy * g0 * ecsx - dwx, dxdt * xs, rows]

        folded = _rows_times(to_fold, et_ref[...], FOLD_PIECES)
        dcs = []
        for g in range(g_n):
            f_cs, f_rows = folded[3 * g], folded[3 * g + 2]
            dd_ref[g] += f_rows[0:1]
            dcs.append(f_cs + jnp.where(last_row, f_rows[1:2], 0.0))
        dadt_all = _times_rows(trit_ref[...], jnp.concatenate(dcs, axis=1), EXPAND_PIECES)
        for g in range(g_n):
            pre, dt, a = heads[g][:3]
            dadt = dadt_all[:, g * HEAD_LANES:(g + 1) * HEAD_LANES]
            dalog_ref[g] += jnp.sum(dadt * dt, axis=0, keepdims=True) * a
            dpre = (dadt * a + folded[3 * g + 1]) * (1.0 / (1.0 + jnp.exp(-pre)))
            ddt_ref[g] = dpre
            dbias_ref[g] += jnp.sum(dpre, axis=0, keepdims=True)

    par_shape = jax.ShapeDtypeStruct((g_n, 1, HEAD_LANES), F32)
    vec = pl.BlockSpec((1, hp), lambda c: (0, 0))
    return _hosted_call(
        "ssd_bwd", body, (nc,),
        [row_spec(conv), dtr_spec, par_spec, par_spec, dskx_spec, st_spec, row_spec(hp), row_spec(hp), row_spec(hp),
         row_spec(1), vec, *_const_specs(consts)],
        [row_spec(conv), dtr_spec, par_spec, par_spec, par_spec, row_spec(hp), vec],
        [jax.ShapeDtypeStruct((l, conv), F32), jax.ShapeDtypeStruct((g_n, l, HEAD_LANES), F32),
         par_shape, par_shape, par_shape, jax.ShapeDtypeStruct((l, pm.shape[1]), BF),
         jax.ShapeDtypeStruct((1, hp), F32)],
        [pltpu.VMEM((g_n, n, rp), F32), pltpu.VMEM((t, hp), F32)], ("arbitrary",),
        (xbc, dtr, dtb, alog, dskx, states, dyn, y, pm, rstd, norm_g, *consts), after=after)


POOL_HALO = 16


def _pool_mix(name, src, dp, backward, out_dtype, into=None, cw=256, tr=512):
    l = src.shape[0]
    gd = dp // len(POOL_WINDOWS)
    cwl, trl = _t(gd, cw), _t(l, tr)
    nt = l // trl

    def body(x_ref, *rest):
        o_ref = rest[-1]
        gi = pl.program_id(0)
        for wi, win in enumerate(POOL_WINDOWS):
            @pl.when(gi == wi)
            def _(win=win):
                def step(i, carry):
                    r0 = pl.multiple_of(i * trl, trl)
                    cur = x_ref[pl.ds(r0, trl), :]
                    trow = r0 + lax.broadcasted_iota(jnp.int32, (trl, 1), 0)
                    cnt = jnp.minimum(trow + 1, win).astype(F32)
                    if not backward:
                        halo = x_ref[pl.ds(pl.multiple_of(jnp.maximum(r0 - POOL_HALO, 0), POOL_HALO), POOL_HALO), :]
                        halo = jnp.where(i > 0, halo, 0.0)
                        s = jnp.concatenate([halo, cur], axis=0)
                        sh = 1
                        while sh < win:
                            s = s + pltpu.roll(s, sh, 0)
                            sh *= 2
                        res = s[POOL_HALO:] / cnt - cur
                    else:
                        halo = x_ref[pl.ds(pl.multiple_of(jnp.minimum(r0 + trl, l - POOL_HALO), POOL_HALO),
                                           POOL_HALO), :]
                        hrow = r0 + trl + lax.broadcasted_iota(jnp.int32, (POOL_HALO, 1), 0)
                        hcnt = jnp.minimum(hrow + 1, win).astype(F32)
                        halo = jnp.where(i < nt - 1, halo / hcnt, 0.0)
                        s = jnp.concatenate([cur / cnt, halo], axis=0)
                        sh = 1
                        while sh < win:
                            s = s + pltpu.roll(s, trl + POOL_HALO - sh, 0)
                            sh *= 2
                        res = s[:trl] - cur
                    o_ref[pl.ds(r0, trl), :] = res.astype(o_ref.dtype)
                    return carry
                lax.fori_loop(0, nt, step, 0)

    col = pl.BlockSpec((l, cwl), lambda g, j: (0, g * (gd // cwl) + j))
    if into is None:
        return pl.pallas_call(
            body, grid=(len(POOL_WINDOWS), gd // cwl), in_specs=[col], out_specs=col,
            out_shape=jax.ShapeDtypeStruct((l, dp), out_dtype),
            compiler_params=_params(("parallel", "parallel")), name=name)(src)
    return pl.pallas_call(
        body, grid=(len(POOL_WINDOWS), gd // cwl), in_specs=[col, pl.BlockSpec(memory_space=pl.ANY)], out_specs=col,
        out_shape=jax.ShapeDtypeStruct(into.shape, into.dtype), input_output_aliases={1: 0},
        compiler_params=_params(("parallel", "parallel")), name=name)(src, into)


def _pool_out_bwd_data(dx, wp_out, mg, pu, scale, tm=1024, tn=512):
    l, d = dx.shape
    dp = wp_out.shape[0]
    tm, tn = _t(l, tm), _t(dp, tn)

    def epi(dyp, m, gt, sc):
        sg = _silu(gt)
        return dyp * sc * sg, dyp * m * sc * _dsilu(gt), jnp.sum(dyp * m * sg, axis=0, keepdims=True)

    tile = pl.BlockSpec((tm, tn), lambda j, i, k: (i, j))
    right = pl.BlockSpec((tm, tn), lambda j, i, k: (i, dp // tn + j))
    vec = pl.BlockSpec((1, tn), lambda j, i, k: (0, j))
    return _mm("pool_out_bwd_data", (dp // tn, l // tm, 1),
               dx, pl.BlockSpec((tm, d), lambda j, i, k: (i, 0)),
               wp_out, pl.BlockSpec((tn, d), lambda j, i, k: (j, 0)), NT,
               [jax.ShapeDtypeStruct((l, dp), BF), jax.ShapeDtypeStruct((l, 2 * dp), BF),
                jax.ShapeDtypeStruct((1, dp), F32)], [tile, right, vec], (tm, tn),
               (mg, pu, scale), (tile, right, vec), epi, summed=(2, 1))


def _group_mm_fwd(mp, wg, pu, scale, tm=1024, tn=1024, tk=1024):
    l, dp = mp.shape
    ng, gd = wg.shape[0], wg.shape[1]
    tm, tn, tk = _t(l, tm), _t(gd, tn), _t(gd, tk)

    def epi(acc, gate, sc):
        return acc, acc * sc * _silu(gate)

    out = pl.BlockSpec((tm, tn), lambda g, i, j, k: (i, g * (gd // tn) + j))
    return _mm("group_mm_fwd", (ng, l // tm, gd // tn, gd // tk),
               mp, pl.BlockSpec((tm, tk), lambda g, i, j, k: (i, g * (gd // tk) + k)),
               wg, pl.BlockSpec((None, tk, tn), lambda g, i, j, k: (g, k, j)), NN,
               [jax.ShapeDtypeStruct((l, dp), F32), jax.ShapeDtypeStruct((l, dp), BF)], [out, out], (tm, tn),
               (pu, scale),
               (pl.BlockSpec((tm, tn), lambda g, i, j, k: (i, (dp + g * gd) // tn + j)),
                pl.BlockSpec((1, tn), lambda g, i, j, k: (0, g * (gd // tn) + j))), epi)


def _group_mm_bwd_data(dmg, wg, after=(), tm=1024, tn=1024, tk=1024):
    l, dp = dmg.shape
    ng, gd = wg.shape[0], wg.shape[1]
    tm, tn, tk = _t(l, tm), _t(gd, tn), _t(gd, tk)
    return _mm("group_mm_bwd_data", (ng, l // tm, gd // tn, gd // tk),
               dmg, pl.BlockSpec((tm, tk), lambda g, i, j, k: (i, g * (gd // tk) + k)),
               wg, pl.BlockSpec((None, tn, tk), lambda g, i, j, k: (g, j, k)), NT,
               [jax.ShapeDtypeStruct((l, dp), F32)],
               [pl.BlockSpec((tm, tn), lambda g, i, j, k: (i, g * (gd // tn) + j))], (tm, tn), after=after)[0]


def _group_mm_bwd_weight(mp, dmg, ng, tm=512, tk=1024):
    l, dp = mp.shape
    gd = dp // ng
    tm, tk = _t(gd, tm), _t(l, tk)
    return _mm("group_mm_bwd_weight", (ng, gd // tm, l // tk),
               mp, pl.BlockSpec((tk, tm), lambda g, i, k: (k, g * (gd // tm) + i)),
               dmg, pl.BlockSpec((tk, gd), lambda g, i, k: (k, g)), TN,
               [jax.ShapeDtypeStruct((ng, gd, gd // 2), F32)],
               [pl.BlockSpec((None, tm, gd // 2), lambda g, i, k: (g, i, 0))], (tm, gd), epi=_pack_epi)[0]


def _cast_bf16(name, w, tr=256):
    r, c = w.shape
    tr = _row_tile(r, tr, 16)

    def body(w_ref, o_ref):
        o_ref[...] = w_ref[...].astype(BF)

    blk = pl.BlockSpec((tr, c), lambda i: (i, 0))
    return pl.pallas_call(body, grid=(r // tr,), in_specs=[blk], out_specs=blk,
                          out_shape=jax.ShapeDtypeStruct((r, c), BF), compiler_params=_params(("parallel",)),
                          name=name)(w)


def _pack_rows(name, w, tr=256):
    r, c = w.shape
    tr = _row_tile(r, tr)

    def body(w_ref, o_ref):
        o_ref[...] = _pack_pairs(w_ref[...])

    return pl.pallas_call(body, grid=(r // tr, 2), in_specs=[pl.BlockSpec((tr, c // 2), lambda i, q: (i, q))],
                          out_specs=pl.BlockSpec((None, tr, c // 4), lambda i, q: (q, i, 0)),
                          out_shape=jax.ShapeDtypeStruct((2, r, c // 4), F32),
                          compiler_params=_params(("parallel", "parallel")), name=name)(w)


def _unpack_rows(name, w0, w1, tr=512):
    r, h = w0.shape
    tr = _row_tile(r, tr, 16)

    def body(w0_ref, w1_ref, o_ref):
        for q, w_ref in enumerate((w0_ref, w1_ref)):
            hi, lo = _unpack_pairs(w_ref[...])
            o_ref[q, :, :h] = hi.astype(BF)
            o_ref[q, :, h:] = lo.astype(BF)

    words = pl.BlockSpec((tr, h), lambda i: (i, 0))
    return pl.pallas_call(body, grid=(r // tr,), in_specs=[words, words],
                          out_specs=pl.BlockSpec((2, tr, 2 * h), lambda i: (0, i, 0)),
                          out_shape=jax.ShapeDtypeStruct((2, r, 2 * h), BF), compiler_params=_params(("parallel",)),
                          name=name)(w0, w1)


def _reduce_packed(name, recv, tr=256):
    nd, r, h = recv.shape
    tr = _row_tile(r, tr)

    def body(p_ref, o_ref):
        hi, lo = _unpack_pairs(p_ref[0])
        for k in range(1, nd):
            a, b = _unpack_pairs(p_ref[k])
            hi, lo = hi + a, lo + b
        o_ref[:, :h] = hi
        o_ref[:, h:] = lo

    return pl.pallas_call(body, grid=(r // tr,), in_specs=[pl.BlockSpec((nd, tr, h), lambda i: (0, i, 0))],
                          out_specs=pl.BlockSpec((tr, 2 * h), lambda i: (i, 0)),
                          out_shape=jax.ShapeDtypeStruct((r, 2 * h), F32), compiler_params=_params(("parallel",)),
                          name=name)(recv)


def _adamw_math(w, g, m, v):
    m2 = ADAM_B1 * m + (1.0 - ADAM_B1) * g
    v2 = ADAM_B2 * v + (1.0 - ADAM_B2) * (g * g)
    m_hat = m2 / (1.0 - ADAM_B1 ** ADAM_STEP)
    v_hat = v2 / (1.0 - ADAM_B2 ** ADAM_STEP)
    delta = -ADAM_LR * (m_hat / (jnp.sqrt(v_hat) + ADAM_EPS) + ADAM_WD * w)
    return delta, m2, v2


def _adamw(name, w, g, m, v, tr=256):
    r, c = w.shape
    tr = _row_tile(r, tr)

    def body(w_ref, g_ref, m_ref, v_ref, d_ref, m2_ref, v2_ref):
        d, m2, v2 = _adamw_math(w_ref[...], g_ref[...], m_ref[...], v_ref[...])
        d_ref[...] = d
        m2_ref[...] = m2
        v2_ref[...] = v2

    blk = pl.BlockSpec((tr, c), lambda i: (i, 0))
    shp = jax.ShapeDtypeStruct((r, c), F32)
    return pl.pallas_call(body, grid=(r // tr,), in_specs=[blk] * 4, out_specs=[blk] * 3, out_shape=[shp] * 3,
                          compiler_params=_params(("parallel",)), name=name)(w, g, m, v)


def _sum_slots(name, a):
    nd, r, c = a.shape

    def body(a_ref, o_ref):
        s = a_ref[0]
        for k in range(1, nd):
            s = s + a_ref[k]
        o_ref[...] = s

    return pl.pallas_call(body, out_shape=jax.ShapeDtypeStruct((r, c), F32), name=name)(a)


def _head_rows(v, g_n, r):
    return jnp.pad(v.reshape(g_n, 1, r), ((0, 0), (0, 0), (0, HEAD_LANES - r)))


class _Later:
    def __init__(self, hosted, first_of, weights_of, pool_w_out_of, send):
        self.hosted, self.first_of, self.weights_of = hosted, first_of, weights_of
        self.pool_w_out_of, self.send = pool_w_out_of, send


def _listed(res):
    return res if isinstance(res, (list, tuple)) else [res]


def _local_step(x, target, ln_g, final_g, conv_b, dt_bias, a_log, d_skip, norm_g, later):
    l, d = x.shape
    di = norm_g.shape[1]
    h = dt_bias.shape[1]
    p = di // h
    conv = conv_b.shape[1]
    n, t = SSM_STATE, SSD_CHUNK
    g_n = (conv - di) // (2 * n)
    r = h // g_n
    nm = di + conv
    ng = len(POOL_WINDOWS)
    dims = (h, g_n, r, p, n, t)
    pw = d

    hosted = later.hosted
    h0, r0, *arrived_0 = _rms_fwd("rms0_fwd", x, ln_g[0:1], exch=hosted.get("rms0"))
    wt_in, conv_w, scale = later.first_of(arrived_0)
    dp = scale.shape[1]
    pm, *arrived_i = _listed(_mm_nt_halves("in_proj_main", h0, wt_in, 0, nm, exch=hosted.get("in_proj")))
    dtr = _mm_nt_halves("in_proj_dt", h0, wt_in, nm, h, tn=h)
    xbc = _conv_fwd(pm, di, conv_w, conv_b)
    dtb, alog = _head_rows(dt_bias, g_n, r), _head_rows(a_log, g_n, r)
    dskx = jnp.repeat(d_skip.reshape(g_n, 1, r), p, axis=2)
    dtr_g = jnp.pad(jnp.transpose(dtr.reshape(l, g_n, r), (1, 0, 2)), ((0, 0), (0, 0), (0, HEAD_LANES - r)))
    y, states, yn, r_n, *arrived_s = _ssd_fwd(xbc, dtr_g, dtb, alog, dskx, pm, norm_g, dims, exch=hosted.get("ssd"))
    w_out, wg, wp_in = later.weights_of(arrived_i, arrived_s)
    x1 = _mm_nn("ssm_out_proj", yn, w_out, add=x)

    h1, r1 = _rms_fwd("rms1_fwd", x1, ln_g[1:2])
    pu, *arrived_p = _listed(_mm_nn_blocked("pool_in_proj", h1, wp_in, exch=hosted.get("pool_in_proj")))
    wp_out = later.pool_w_out_of(arrived_p)
    mp = _pool_mix("pool_mix_fwd", pu, dp, False, BF)
    mg, yp = _group_mm_fwd(mp, wg, pu, scale)
    x2 = _mm_nn("pool_out_proj", yp, wp_out, add=x1)

    dx2, dx2_b, d_final_g, loss = _loss_head(x2, final_g, target)

    gw_pout = _mm_tn_packed("pool_out_bwd_weight", yp, dx2_b, pw)
    going = later.send("pool_w_out", gw_pout)
    dmg, dpu, d_scale = _pool_out_bwd_data(dx2_b, wp_out, mg, pu, scale)
    dmp = _group_mm_bwd_data(dmg, wg, after=going)
    gw_g = _group_mm_bwd_weight(mp, dmg, ng)
    going = later.send("pool_w_group", gw_g)
    dpu = _pool_mix("pool_mix_bwd", dmp, dp, True, BF, into=dpu)
    dh1 = _mm_nt_blocked("pool_in_bwd_data", dpu, wp_in, after=going)
    gw_pin = _mm_tn_packed("pool_in_bwd_weight", h1, dpu, 2 * dp // N_DEV)
    going = later.send("pool_w_in", gw_pin)
    dx1, dx1_b, d_ln1 = _rms_bwd("rms1_bwd", dh1, x1, r1, ln_g[1:2], dx2)

    dyn = _mm_nt("ssm_out_bwd_data", dx1_b, w_out, 0, di, tn=1024, after=going)
    gw_out = _mm_tn_packed("ssm_out_bwd_weight", yn, dx1_b, pw)
    going = later.send("ssm_w_out", gw_out)
    dxbc, ddt_g, dbias_g, dalog_g, dd_g, dproj, d_norm_g = _ssd_bwd(
        xbc, dtr_g, dtb, alog, dskx, states, dyn, y, pm, r_n, norm_g, dims, after=going)
    dproj, d_conv_w, d_conv_b = _conv_bwd(pm, di, conv_w, conv_b, dxbc, dproj)
    ddt = jnp.transpose(ddt_g[:, :, :r], (1, 0, 2)).reshape(l, h)
    gwt_in, going = [], ()
    for q in range(2):
        gq = _mm_tn_packed(f"in_proj_bwd_weight_{q}", dproj, h0, d // 2, rows=nm + h, b_blk=q, after=going)
        gq = _mm_tn_packed(f"in_proj_bwd_weight_dt_{q}", ddt, h0, d // 2, into=gq, b_blk=q)
        going = later.send(f"ssm_w_in_{q}", gq)
        gwt_in.append(gq)
    gwt_in = jnp.concatenate(gwt_in, axis=0)
    dh0 = _mm_nn_halves("in_proj_bwd_data_dt", ddt, wt_in, nm, after=going)
    dh0 = _mm_nn_halves("in_proj_bwd_data", dproj, wt_in, 0, add=dh0)
    grad_x, _, d_ln0 = _rms_bwd("rms0_bwd", dh0, x, r0, ln_g[0:1], dx1)

    def heads(v):
        return v[:, 0, :r].reshape(1, h)

    small = dict(ln_g=jnp.concatenate([d_ln0, d_ln1], axis=0), final_g=d_final_g, conv_w=d_conv_w, conv_b=d_conv_b,
                 dt_bias=heads(dbias_g), a_log=heads(dalog_g), d_skip=heads(dd_g), norm_g=d_norm_g, scale=d_scale)
    big = dict(ssm_w_in=gwt_in, ssm_w_out=gw_out, pool_w_in=gw_pin, pool_w_group=gw_g, pool_w_out=gw_pout)
    return loss, grad_x, small, big


SMALL_ORDER = ("ln_g", "final_g", "conv_w", "conv_b", "dt_bias", "a_log", "d_skip", "norm_g", "scale", "loss")


def _flatten_small(parts):
    flat = jnp.concatenate([parts[k].reshape(-1) for k in SMALL_ORDER])
    n = flat.shape[0]
    rows = -(-n // 1024) * 8
    return jnp.pad(flat, (0, rows * 128 - n)).reshape(rows, 128)


def _split_small(flat, shapes):
    flat = flat.reshape(-1)
    out, off = {}, 0
    for k in SMALL_ORDER:
        size = int(np.prod(shapes[k]))
        out[k] = flat[off:off + size].reshape(shapes[k])
        off += size
    return out


def kernel(x, ln_g, final_g, ssm_w_in, ssm_conv_w, ssm_conv_b, ssm_dt_bias, ssm_a_log, ssm_d, ssm_norm_g, ssm_w_out, pool_w_in, pool_w_group, pool_scale, pool_w_out, loss_target, m_ln_g, m_final_g, m_ssm_w_in, m_ssm_conv_w, m_ssm_conv_b, m_ssm_dt_bias, m_ssm_a_log, m_ssm_d, m_ssm_norm_g, m_ssm_w_out, m_pool_w_in, m_pool_w_group, m_pool_scale, m_pool_w_out, v_ln_g, v_final_g, v_ssm_w_in, v_ssm_conv_w, v_ssm_conv_b, v_ssm_dt_bias, v_ssm_a_log, v_ssm_d, v_ssm_norm_g, v_ssm_w_out, v_pool_w_in, v_pool_w_group, v_pool_scale, v_pool_w_out):
    l, d = x.shape[1], x.shape[2]
    me = 4 * lax.axis_index("x") + 2 * lax.axis_index("y") + lax.axis_index("c")
    ng, gds, gd = pool_w_group.shape[1], pool_w_group.shape[2], pool_w_group.shape[3]
    sin_s = ssm_w_in.shape[2]
    dp_s = pool_w_out.shape[1]
    conv_s = ssm_conv_w.shape[2]

    wt_in_s = _pack_rows("pack_w_in", jnp.transpose(ssm_w_in[0]))
    w_out_s = _cast_bf16("cast_w_out", ssm_w_out[0])
    wp_in_s = _cast_bf16("cast_pool_w_in", pool_w_in[0])
    wg_s = _cast_bf16("cast_pool_w_group", pool_w_group[0].reshape(ng * gds, gd))
    wp_out_s = _cast_bf16("cast_pool_w_out", pool_w_out[0])
    small_s = jnp.concatenate([ssm_conv_w[0].reshape(-1), pool_scale[0]])
    n_small = small_s.shape[0]
    small_s = jnp.pad(small_s, (0, -(-n_small // 1024) * 1024 - n_small)).reshape(-1, 128)
    hosted = dict(rms0=_Gather([wt_in_s[0], wt_in_s[1], small_s]), in_proj=_Gather([w_out_s, wg_s]),
                  ssd=_Gather([wp_in_s]), pool_in_proj=_Gather([wp_out_s]))

    def first_of(arrived_0):
        wt_in_g0, wt_in_g1, small_g = arrived_0
        wt_in = _unpack_rows("unpack_w_in", wt_in_g0.reshape(N_DEV * sin_s, d // 4),
                             wt_in_g1.reshape(N_DEV * sin_s, d // 4))
        small_all = small_g.reshape(N_DEV, -1)[:, :n_small]
        conv_w = jnp.transpose(small_all[:, :CONV_TAPS * conv_s].reshape(N_DEV, CONV_TAPS, conv_s), (1, 0, 2))
        return wt_in, conv_w.reshape(CONV_TAPS, -1), small_all[:, CONV_TAPS * conv_s:].reshape(1, -1)

    def weights_of(arrived_i, arrived_s):
        wg = jnp.transpose(arrived_i[1].reshape(N_DEV, ng, gds, gd), (1, 0, 2, 3)).reshape(ng, gd, gd)
        return arrived_i[0].reshape(-1, d), wg, arrived_s[0]

    def rows_major(gp):
        q, _, hw = gp.shape
        return jnp.transpose(gp.reshape(q, N_DEV, -1, hw), (1, 0, 2, 3))

    to_blocks = dict(ssm_w_in_0=rows_major, ssm_w_in_1=rows_major, ssm_w_out=rows_major, pool_w_out=rows_major,
                     pool_w_in=lambda gp: gp[:, None],
                     pool_w_group=lambda gp: jnp.transpose(gp.reshape(ng, N_DEV, gds, gd // 2), (1, 0, 2, 3)))
    travelling = {}

    def send(name, gp):
        *travelling[name], token = _scatter_start(f"scatter_{name}_start", to_blocks[name](gp))
        return (token,)

    later = _Later(hosted, first_of, weights_of, lambda arrived_p: arrived_p[0].reshape(-1, d), send)

    loss, grad_x, small, _ = _local_step(
        x[0], loss_target[0], ln_g, final_g.reshape(1, d), ssm_conv_b, ssm_dt_bias, ssm_a_log, ssm_d, ssm_norm_g, later)

    def arrived(name, after):
        src, land = _scatter_wait(f"scatter_{name}_wait", *travelling[name], after=after)
        own = lax.dynamic_slice_in_dim(src, me, 1, axis=0)
        return lax.dynamic_update_slice_in_dim(land, own, me, axis=0)

    def reduced(name, rv):
        q = rv.shape[1]
        cols = [_reduce_packed(f"{name}_{j}", rv[:, j]) for j in range(q)]
        return cols[0] if q == 1 else jnp.concatenate(cols, axis=1)

    grads = {}
    grads["pool_w_out"] = reduced("reduce_pool_w_out", arrived("pool_w_out", (grad_x,)))[None]
    grads["pool_w_group"] = _reduce_packed("reduce_pool_w_group", arrived("pool_w_group", (grad_x,)).reshape(
        N_DEV, ng * gds, gd // 2)).reshape(1, ng, gds, gd)
    grads["pool_w_in"] = reduced("reduce_pool_w_in", arrived("pool_w_in", (grad_x,)))[None]
    grads["ssm_w_out"] = reduced("reduce_w_out", arrived("ssm_w_out", (grad_x,)))[None]

    weights = dict(ln_g=ln_g, final_g=final_g, ssm_w_in=ssm_w_in, ssm_conv_w=ssm_conv_w, ssm_conv_b=ssm_conv_b,
                   ssm_dt_bias=ssm_dt_bias, ssm_a_log=ssm_a_log, ssm_d=ssm_d, ssm_norm_g=ssm_norm_g,
                   ssm_w_out=ssm_w_out, pool_w_in=pool_w_in, pool_w_group=pool_w_group, pool_scale=pool_scale,
                   pool_w_out=pool_w_out)
    m_in = dict(ln_g=m_ln_g, final_g=m_final_g, ssm_w_in=m_ssm_w_in, ssm_conv_w=m_ssm_conv_w, ssm_conv_b=m_ssm_conv_b,
                ssm_dt_bias=m_ssm_dt_bias, ssm_a_log=m_ssm_a_log, ssm_d=m_ssm_d, ssm_norm_g=m_ssm_norm_g,
                ssm_w_out=m_ssm_w_out, pool_w_in=m_pool_w_in, pool_w_group=m_pool_w_group, pool_scale=m_pool_scale,
                pool_w_out=m_pool_w_out)
    v_in = dict(ln_g=v_ln_g, final_g=v_final_g, ssm_w_in=v_ssm_w_in, ssm_conv_w=v_ssm_conv_w, ssm_conv_b=v_ssm_conv_b,
                ssm_dt_bias=v_ssm_dt_bias, ssm_a_log=v_ssm_a_log, ssm_d=v_ssm_d, ssm_norm_g=v_ssm_norm_g,
                ssm_w_out=v_ssm_w_out, pool_w_in=v_pool_w_in, pool_w_group=v_pool_w_group, pool_scale=v_pool_scale,
                pool_w_out=v_pool_w_out)
    names = list(weights)
    big_names = ("ssm_w_out", "pool_w_in", "pool_w_group", "pool_w_out", "ssm_w_in")
    delta, new_m, new_v = {}, {}, {}

    def adamw_big(k):
        shp = weights[k].shape
        two_d = (-1, shp[-1])
        dk, mk, vk = _adamw(f"adamw_{k}", weights[k].reshape(two_d), grads[k].reshape(two_d), m_in[k].reshape(two_d),
                            v_in[k].reshape(two_d))
        delta[k], new_m[k], new_v[k] = dk.reshape(shp), mk.reshape(shp), vk.reshape(shp)

    for k in big_names[:-1]:
        adamw_big(k)
    small_names = [k for k in names if k not in big_names]

    small["loss"] = loss
    shapes = {k: small[k].shape for k in SMALL_ORDER}
    gathered_small, = _exchange_alone("all_gather_small_grads", _Gather([_flatten_small(small)]),
                                      after=tuple(delta[k] for k in big_names[:-1]))
    summed = _split_small(_sum_slots("sum_small_grads", gathered_small), shapes)
    grads.update(ln_g=summed["ln_g"], final_g=summed["final_g"].reshape(d), ssm_conv_b=summed["conv_b"],
                 ssm_conv_w=lax.dynamic_slice_in_dim(summed["conv_w"], me * conv_s, conv_s, axis=1)[None],
                 ssm_dt_bias=summed["dt_bias"], ssm_a_log=summed["a_log"], ssm_d=summed["d_skip"],
                 ssm_norm_g=summed["norm_g"],
                 pool_scale=lax.dynamic_slice_in_dim(summed["scale"], me * dp_s, dp_s, axis=1))

    def packed(tree):
        flat = jnp.concatenate([tree[k].reshape(-1) for k in small_names])
        nn = flat.shape[0]
        return jnp.pad(flat, (0, -(-nn // 1024) * 1024 - nn), constant_values=1.0).reshape(-1, 128)

    ds, ms, vs = _adamw("adamw_small", packed(weights), packed(grads), packed(m_in), packed(v_in))
    off = 0
    for k in small_names:
        shp = weights[k].shape
        size = int(np.prod(shp))
        for res, arr in ((delta, ds), (new_m, ms), (new_v, vs)):
            res[k] = arr.reshape(-1)[off:off + size].reshape(shp)
        off += size

    k = "ssm_w_in"
    gt = jnp.concatenate([reduced(f"reduce_w_in_{q}", arrived(f"ssm_w_in_{q}", (ds,))) for q in range(2)], axis=1)
    dk, mk, vk = _adamw("adamw_ssm_w_in", jnp.transpose(weights[k][0]), gt, jnp.transpose(m_in[k][0]),
                        jnp.transpose(v_in[k][0]))
    grads[k], delta[k], new_m[k], new_v[k] = (jnp.transpose(t)[None] for t in (gt, dk, mk, vk))

    return (summed["loss"].reshape(()), grad_x[None], *[grads[k] for k in names], *[delta[k] for k in names],
            *[new_m[k] for k in names], *[new_v[k] for k in names])
```

```python
import math

import jax
import jax.numpy as jnp
import numpy as np
from jax import lax
from jax.experimental import pallas as pl
from jax.experimental.pallas import tpu as pltpu

F32 = jnp.float32
BF = jnp.bfloat16
U32 = jnp.uint32

N_DEV = 8
EPS = 1e-6
SSD_CHUNK = 64
SSM_STATE = 128
CONV_TAPS = 4
HEAD_LANES = 128
POOL_WINDOWS = (2, 4, 8, 16)
ADAM_LR, ADAM_B1, ADAM_B2, ADAM_EPS, ADAM_WD, ADAM_STEP = 0.001, 0.9, 0.999, 1e-08, 0.01, 10
VMEM_LIMIT = 56 * 1024 * 1024
NEG_BIG = -1e30
HAND_ON_AT = 7

NN = ((1,), (0,))
NT = ((1,), (1,))
TN = ((0,), (0,))
MESH = pl.DeviceIdType.MESH


def _t(dim, pref):
    return pref if dim % pref == 0 else dim


def _row_tile(rows, pref, mult=8):
    best = rows
    for cand in range(mult, min(rows, pref) + 1, mult):
        if rows % cand == 0:
            best = cand
    return best


def _params(sem=None):
    return pltpu.CompilerParams(dimension_semantics=sem, vmem_limit_bytes=VMEM_LIMIT)


def _silu(x):
    return x * (1.0 / (1.0 + jnp.exp(-x)))


def _dsilu(x):
    s = 1.0 / (1.0 + jnp.exp(-x))
    return s * (1.0 + x * (1.0 - s))


def _bdot(a, b, dims=NN):
    return lax.dot_general(a.astype(BF), b.astype(BF), (dims, ((), ())), preferred_element_type=F32)


def _pack_pairs(x):
    h = x.shape[1] // 2
    hi = lax.bitcast_convert_type(x[:, :h].astype(jnp.bfloat16).astype(F32), U32)
    lo = lax.bitcast_convert_type(x[:, h:].astype(jnp.bfloat16).astype(F32), U32)
    return lax.bitcast_convert_type(hi | (lo >> 16), F32)


def _unpack_pairs(w):
    u = lax.bitcast_convert_type(w, U32)
    hi = lax.bitcast_convert_type(u & jnp.uint32(0xFFFF0000), F32)
    lo = lax.bitcast_convert_type(u << 16, F32)
    return hi, lo


def _mesh_pos():
    return lax.axis_index("x"), lax.axis_index("y"), lax.axis_index("c")


def _slot(pos):
    return 4 * pos[0] + 2 * pos[1] + pos[2]


class _Gather:
    def __init__(self, arrays):
        self.arrays = list(arrays)
        self.out_shapes = [jax.ShapeDtypeStruct((N_DEV, *s.shape), s.dtype) for s in arrays]

    def phases(self, src, dst, send_sems, recv_sems, local_sems):
        n_arr = len(self.arrays)
        x, y, c = _mesh_pos()
        me, sibling = (x, y, c), (x, y, 1 - c)
        chips = [(1 - x, y), (x, 1 - y), (1 - x, 1 - y)]

        def copy(a, k, block, to, from_src):
            return pltpu.make_async_remote_copy(
                src_ref=src[a] if from_src else dst[a].at[_slot(block)], dst_ref=dst[a].at[_slot(block)],
                send_sem=send_sems.at[a * 7 + k], recv_sem=recv_sems.at[a * 7 + k], device_id=to, device_id_type=MESH)

        def mine(a):
            return pltpu.make_async_copy(src[a], dst[a].at[_slot(me)], local_sems.at[a])

        def first(a):
            return [copy(a, 0, me, sibling, True)] + [copy(a, 1 + j, me, (*chip, c), True)
                                                     for j, chip in enumerate(chips)]

        def start():
            for a in range(n_arr):
                mine(a).start()
                for cp in first(a):
                    cp.start()

        def middle():
            for a in range(n_arr):
                for j, chip in enumerate(chips):
                    copy(a, 1 + j, (*chip, c), me, False).wait_recv()
                    copy(a, 4 + j, (*chip, c), sibling, False).start()

        def finish():
            for a in range(n_arr):
                copy(a, 0, sibling, me, False).wait_recv()
                for j, chip in enumerate(chips):
                    copy(a, 4 + j, (*chip, 1 - c), me, False).wait_recv()
                for cp in first(a):
                    cp.wait_send()
                for j, chip in enumerate(chips):
                    copy(a, 4 + j, (*chip, c), sibling, False).wait_send()
                mine(a).wait()

        return start, middle, finish


def _hosted_call(name, body, grid, in_specs, out_specs, out_shape, scratch_shapes, sem, operands, exch=None,
                 after=(), into=None):
    if into is not None:
        n_lead = len(in_specs)
        inner = body

        def body(*refs):
            inner(*refs[:n_lead], *refs[n_lead + 1:])

        return pl.pallas_call(
            body, grid=grid, in_specs=[*in_specs, pl.BlockSpec(memory_space=pl.ANY)], out_specs=out_specs,
            out_shape=out_shape, scratch_shapes=scratch_shapes, input_output_aliases={n_lead: 0},
            compiler_params=_params(sem), name=name)(*operands, into)
    if after:
        n_lead = len(in_specs)
        inner = body

        def body(*refs):
            inner(*refs[:n_lead], *refs[n_lead + len(after):])

        in_specs = [*in_specs, *[pl.BlockSpec(memory_space=pl.ANY)] * len(after)]
        operands = (*operands, *after)
    if exch is None:
        return pl.pallas_call(body, grid=grid, in_specs=in_specs, out_specs=out_specs, out_shape=out_shape,
                              scratch_shapes=scratch_shapes, compiler_params=_params(sem), name=name)(*operands)
    n_in, n_out, n_scr, ne = len(in_specs), len(out_specs), len(scratch_shapes), len(exch.arrays)
    total = math.prod(grid)

    def wrapped(*refs):
        ins, ex_in = refs[:n_in], refs[n_in:n_in + ne]
        outs = refs[n_in + ne:n_in + ne + n_out]
        ex_out = refs[n_in + ne + n_out:n_in + 2 * ne + n_out]
        scr = refs[n_in + 2 * ne + n_out:n_in + 2 * ne + n_out + n_scr]
        step = 0
        for axis, size in enumerate(grid):
            step = step * size + pl.program_id(axis)
        start, middle, finish = exch.phases(ex_in, ex_out, *refs[-3:])
        pl.when(step == 0)(start)
        if middle is not None:
            pl.when(step == (total * HAND_ON_AT) // 8)(middle)
        body(*ins, *outs, *scr)
        pl.when(step == total - 1)(finish)

    hbm = pl.BlockSpec(memory_space=pl.ANY)
    sems = [pltpu.SemaphoreType.DMA((ne * 7,)), pltpu.SemaphoreType.DMA((ne * 7,)), pltpu.SemaphoreType.DMA((ne,))]
    return pl.pallas_call(
        wrapped, grid=grid, in_specs=[*in_specs, *[hbm] * ne], out_specs=[*out_specs, *[hbm] * ne],
        out_shape=[*out_shape, *exch.out_shapes], scratch_shapes=[*scratch_shapes, *sems],
        compiler_params=pltpu.CompilerParams(dimension_semantics=("arbitrary",) * len(grid),
                                             vmem_limit_bytes=VMEM_LIMIT, has_side_effects=True),
        name=name)(*operands, *exch.arrays)


def _flip_peers():
    x, y, c = _mesh_pos()
    peers = []
    for k in range(1, N_DEV):
        fx, fy, fc = (k >> 2) & 1, (k >> 1) & 1, k & 1
        peers.append((1 - x if fx else x, 1 - y if fy else y, 1 - c if fc else c))
    return (x, y, c), peers


def _split_scatter_copy(src, land, send_sems, recv_sems, k, me, peer, sending):
    return pltpu.make_async_remote_copy(
        src_ref=src.at[_slot(peer)], dst_ref=land.at[_slot(me) if sending else _slot(peer)],
        send_sem=send_sems.at[k], recv_sem=recv_sems.at[k], device_id=peer, device_id_type=MESH)


def _scatter_start(name, blocks):
    def body(src, land, send_sems, recv_sems, src_thru, land_thru, token):
        me, peers = _flip_peers()
        for k, peer in enumerate(peers):
            _split_scatter_copy(src, land, send_sems, recv_sems, k, me, peer, True).start()
        token[...] = jnp.zeros_like(token)

    hbm = pl.BlockSpec(memory_space=pltpu.HBM)
    sem = pl.BlockSpec(memory_space=pltpu.SEMAPHORE)
    return pl.pallas_call(
        body, name=name,
        out_shape=(pltpu.SemaphoreType.DMA((N_DEV - 1,)), pltpu.SemaphoreType.DMA((N_DEV - 1,)),
                   pltpu.HBM(blocks.shape, blocks.dtype), pltpu.HBM(blocks.shape, blocks.dtype),
                   jax.ShapeDtypeStruct((8, 128), F32)),
        in_specs=(hbm, hbm), out_specs=(sem, sem, hbm, hbm, pl.BlockSpec(memory_space=pltpu.VMEM)),
        input_output_aliases={0: 2, 1: 3},
        compiler_params=pltpu.CompilerParams(has_side_effects=pltpu.SideEffectType.DATAFLOW_SIDE_EFFECTING),
    )(pltpu.with_memory_space_constraint(blocks, pltpu.HBM),
      pltpu.with_memory_space_constraint(lax.empty(blocks.shape, blocks.dtype), pltpu.HBM))


def _scatter_wait(name, send_sems, recv_sems, src_thru, land_thru, after):
    def body(src, land, send_sems, recv_sems, *rest):
        me, peers = _flip_peers()
        for k, peer in enumerate(peers):
            _split_scatter_copy(src, land, send_sems, recv_sems, k, me, peer, True).wait_send()
            _split_scatter_copy(src, land, send_sems, recv_sems, k, me, peer, False).wait_recv()

    hbm = pl.BlockSpec(memory_space=pltpu.HBM)
    sem = pl.BlockSpec(memory_space=pltpu.SEMAPHORE)
    return pl.pallas_call(
        body, name=name,
        out_shape=(pltpu.HBM(src_thru.shape, src_thru.dtype), pltpu.HBM(land_thru.shape, land_thru.dtype)),
        in_specs=(hbm, hbm, sem, sem, *[pl.BlockSpec(memory_space=pl.ANY)] * len(after)), out_specs=(hbm, hbm),
        input_output_aliases={0: 0, 1: 1},
        compiler_params=pltpu.CompilerParams(has_side_effects=pltpu.SideEffectType.DATAFLOW_SIDE_EFFECTING),
    )(src_thru, land_thru, send_sems, recv_sems, *after)


def _exchange_alone(name, exch, after=()):
    def body():
        pass

    return _hosted_call(name, body, (1,), [], [], [], [], None, (), exch, after)


def _mm(name, grid, a, a_spec, b, b_spec, dims, outs, o_specs, acc_shape, extra=(), extra_specs=(), epi=None,
        exch=None, after=(), into=None, summed=None):
    nk = grid[-1]
    n_extra, n_out = len(extra), len(outs)

    def body(*refs):
        a_ref, b_ref = refs[0], refs[1]
        ex = refs[2:2 + n_extra]
        o_refs = refs[2 + n_extra:2 + n_extra + n_out]

        def write(res):
            res = (res,) if epi is None else epi(res, *[e[...] for e in ex])
            for idx, (o, r) in enumerate(zip(o_refs, res)):
                if summed is not None and idx == summed[0]:
                    first = pl.program_id(summed[1]) == 0

                    @pl.when(first)
                    def _(o=o, r=r):
                        o[...] = r.astype(o.dtype)

                    @pl.when(jnp.logical_not(first))
                    def _(o=o, r=r):
                        o[...] += r.astype(o.dtype)
                else:
                    o[...] = r.astype(o.dtype)

        if nk == 1:
            write(_bdot(a_ref[...], b_ref[...], dims))
            return
        acc = refs[-1]
        k = pl.program_id(len(grid) - 1)

        @pl.when(k == 0)
        def _():
            acc[...] = _bdot(a_ref[...], b_ref[...], dims)

        @pl.when(jnp.logical_and(k > 0, k < nk - 1))
        def _():
            acc[...] += _bdot(a_ref[...], b_ref[...], dims)

        @pl.when(k == nk - 1)
        def _():
            write(acc[...] + _bdot(a_ref[...], b_ref[...], dims))

    sem = ("parallel",) * (len(grid) - 1) + ("arbitrary",)
    if summed is not None:
        sem = ("arbitrary",) * len(grid)
    scratch = [] if nk == 1 else [pltpu.VMEM(acc_shape, F32)]
    return _hosted_call(name, body, grid, [a_spec, b_spec, *extra_specs], list(o_specs), list(outs),
                        scratch, sem, (a, b, *extra), exch, after, into)


def _add_epi(acc, add):
    return (acc + add,)


def _pack_epi(acc):
    return (_pack_pairs(acc),)


def _mm_nt(name, a, b, n_off, n, out_dtype=F32, tm=1024, tn=512, exch=None, after=()):
    m, kk = a.shape
    tm, tn = _t(m, tm), math.gcd(_t(n, tn), n_off)
    res = _mm(name, (m // tm, n // tn, 1),
              a, pl.BlockSpec((tm, kk), lambda i, j, k: (i, 0)),
              b, pl.BlockSpec((tn, kk), lambda i, j, k: (n_off // tn + j, 0)), NT,
              [jax.ShapeDtypeStruct((m, n), out_dtype)], [pl.BlockSpec((tm, tn), lambda i, j, k: (i, j))], (tm, tn),
              exch=exch, after=after)
    return res[0] if exch is None else res


def _mm_nt_halves(name, a, b2, n_off, n, tm=1024, tn=512, exch=None):
    m = a.shape[0]
    kh = b2.shape[2]
    tm, tn = _t(m, tm), math.gcd(_t(n, tn), n_off)

    def body(a_ref, b_ref, o_ref):
        o_ref[...] = _bdot(a_ref[:, :kh], b_ref[0], NT) + _bdot(a_ref[:, kh:], b_ref[1], NT)

    res = _hosted_call(name, body, (m // tm, n // tn),
                       [pl.BlockSpec((tm, 2 * kh), lambda i, j: (i, 0)),
                        pl.BlockSpec((2, tn, kh), lambda i, j: (0, n_off // tn + j, 0))],
                       [pl.BlockSpec((tm, tn), lambda i, j: (i, j))], [jax.ShapeDtypeStruct((m, n), F32)], [],
                       ("parallel", "parallel"), (a, b2), exch)
    return res[0] if exch is None else res


def _mm_nn_halves(name, a, b2, k_off=0, add=None, after=(), tm=1024, tk=1024):
    m, kk = a.shape
    half = b2.shape[2]
    tm, tk = _t(m, tm), math.gcd(_t(kk, tk), k_off)
    extra, especs, epi = (), (), None
    if add is not None:
        extra, especs, epi = (add,), (pl.BlockSpec((tm, half), lambda i, j, k: (i, j)),), _add_epi
    return _mm(name, (m // tm, 2, kk // tk),
               a, pl.BlockSpec((tm, tk), lambda i, j, k: (i, k)),
               b2, pl.BlockSpec((None, tk, half), lambda i, j, k: (j, k_off // tk + k, 0)), NN,
               [jax.ShapeDtypeStruct((m, 2 * half), F32)], [pl.BlockSpec((tm, half), lambda i, j, k: (i, j))],
               (tm, half), extra, especs, epi, after=after)[0]


def _mm_nn_blocked(name, a, b3, tm=1024, tn=512, exch=None):
    m, kk = a.shape
    nb, _, cs = b3.shape
    tm, tn = _t(m, tm), _t(cs, tn)
    per = cs // tn
    res = _mm(name, (m // tm, nb * per, 1),
              a, pl.BlockSpec((tm, kk), lambda i, j, k: (i, 0)),
              b3, pl.BlockSpec((None, kk, tn), lambda i, j, k: (j // per, 0, j % per)), NN,
              [jax.ShapeDtypeStruct((m, nb * cs), F32)], [pl.BlockSpec((tm, tn), lambda i, j, k: (i, j))],
              (tm, tn), exch=exch)
    return res[0] if exch is None else res


def _mm_nt_blocked(name, a, b3, after=(), tm=1024, tn=1024):
    m = a.shape[0]
    nb, n, cs = b3.shape
    tm, tn = _t(m, tm), _t(n, tn)
    return _mm(name, (m // tm, n // tn, nb),
               a, pl.BlockSpec((tm, cs), lambda i, j, k: (i, k)),
               b3, pl.BlockSpec((None, tn, cs), lambda i, j, k: (k, j, 0)), NT,
               [jax.ShapeDtypeStruct((m, n), F32)], [pl.BlockSpec((tm, tn), lambda i, j, k: (i, j))], (tm, tn),
               after=after)[0]


def _mm_nn(name, a, b, k_off=0, kk=None, add=None, out_dtype=F32, tm=1024, tn=1024, tk=1024, after=()):
    m = a.shape[0]
    kk = a.shape[1] if kk is None else kk
    n = b.shape[1]
    tm, tn, tk = _t(m, tm), _t(n, tn), math.gcd(_t(kk, tk), k_off)
    extra, especs, epi = (), (), None
    if add is not None:
        extra, especs, epi = (add,), (pl.BlockSpec((tm, tn), lambda i, j, k: (i, j)),), _add_epi
    return _mm(name, (m // tm, n // tn, kk // tk),
               a, pl.BlockSpec((tm, tk), lambda i, j, k: (i, k)),
               b, pl.BlockSpec((tk, tn), lambda i, j, k: (k_off // tk + k, j)), NN,
               [jax.ShapeDtypeStruct((m, n), out_dtype)], [pl.BlockSpec((tm, tn), lambda i, j, k: (i, j))], (tm, tn),
               extra, especs, epi, after=after)[0]


TN_ACC_ELEMENTS = 1 << 20


def _mm_tn_packed(name, a, b, pw, tk=1024, rows=None, into=None, b_blk=None, after=()):
    kk, m = a.shape
    nq = b.shape[1] // pw if b_blk is None else 1
    first = 0 if b_blk is None else b_blk
    tm, tk = _t(m, TN_ACC_ELEMENTS // pw), _t(kk, tk)
    rows = m if rows is None else rows
    row_blk = 0
    if into is not None:
        rows = into.shape[1]
        assert (rows - m) % tm == 0
        row_blk = (rows - m) // tm
    return _mm(name, (m // tm, nq, kk // tk),
               a, pl.BlockSpec((tk, tm), lambda i, j, k: (k, i)),
               b, pl.BlockSpec((tk, pw), lambda i, j, k: (k, first + j)), TN,
               [jax.ShapeDtypeStruct((nq, rows, pw // 2), F32)],
               [pl.BlockSpec((None, tm, pw // 2), lambda i, j, k: (j, row_blk + i, 0))], (tm, pw), epi=_pack_epi,
               into=into, after=after)[0]


def _rms_fwd(name, x, g, tm=256, exch=None):
    l, d = x.shape
    tm = _t(l, tm)

    def body(x_ref, g_ref, h_ref, r_ref):
        xv = x_ref[...]
        r = lax.rsqrt(jnp.mean(xv * xv, axis=-1, keepdims=True) + EPS)
        h_ref[...] = (xv * r * g_ref[...]).astype(BF)
        r_ref[...] = r

    return _hosted_call(
        name, body, (l // tm,),
        [pl.BlockSpec((tm, d), lambda i: (i, 0)), pl.BlockSpec((1, d), lambda i: (0, 0))],
        [pl.BlockSpec((tm, d), lambda i: (i, 0)), pl.BlockSpec((tm, 1), lambda i: (i, 0))],
        [jax.ShapeDtypeStruct((l, d), BF), jax.ShapeDtypeStruct((l, 1), F32)], [], ("parallel",), (x, g), exch)


def _rms_bwd(name, dh, x, r, g, dres, tm=256):
    l, d = x.shape
    tm = _t(l, tm)

    def body(dh_ref, x_ref, r_ref, g_ref, dres_ref, dx_ref, dxb_ref, dg_ref):
        i = pl.program_id(0)
        rr = r_ref[...]
        xhat = x_ref[...] * rr
        dhv = dh_ref[...]
        dxh = dhv * g_ref[...]
        dx = dres_ref[...] + rr * (dxh - xhat * jnp.mean(dxh * xhat, axis=-1, keepdims=True))
        dx_ref[...] = dx
        dxb_ref[...] = dx.astype(BF)

        @pl.when(i == 0)
        def _():
            dg_ref[...] = jnp.zeros_like(dg_ref)

        dg_ref[...] += jnp.sum(dhv * xhat, axis=0, keepdims=True)

    row = pl.BlockSpec((tm, d), lambda i: (i, 0))
    vec = pl.BlockSpec((1, d), lambda i: (0, 0))
    return pl.pallas_call(
        body, grid=(l // tm,), in_specs=[row, row, pl.BlockSpec((tm, 1), lambda i: (i, 0)), vec, row],
        out_specs=[row, row, vec],
        out_shape=[jax.ShapeDtypeStruct((l, d), F32), jax.ShapeDtypeStruct((l, d), BF),
                   jax.ShapeDtypeStruct((1, d), F32)],
        compiler_params=_params(("arbitrary",)), name=name)(dh, x, r, g, dres)


def _loss_head(x, g, target, tm=256):
    l, d = x.shape
    tm = _t(l, tm)

    def body(x_ref, g_ref, t_ref, dx_ref, dxb_ref, dg_ref, loss_ref):
        i = pl.program_id(0)
        xv = x_ref[...]
        gv = g_ref[...]
        r = lax.rsqrt(jnp.mean(xv * xv, axis=-1, keepdims=True) + EPS)
        xhat = xv * r
        e = xhat * gv - t_ref[...]
        dy = e * (1.0 / d)
        dxh = dy * gv
        dx = r * (dxh - xhat * jnp.mean(dxh * xhat, axis=-1, keepdims=True))
        dx_ref[...] = dx
        dxb_ref[...] = dx.astype(BF)

        @pl.when(i == 0)
        def _():
            dg_ref[...] = jnp.zeros_like(dg_ref)
            loss_ref[...] = jnp.zeros_like(loss_ref)

        dg_ref[...] += jnp.sum(dy * xhat, axis=0, keepdims=True)
        loss_ref[...] += 0.5 * jnp.sum(jnp.sum(e * e, axis=-1, keepdims=True) * (1.0 / d), axis=0, keepdims=True)

    row = pl.BlockSpec((tm, d), lambda i: (i, 0))
    vec = pl.BlockSpec((1, d), lambda i: (0, 0))
    return pl.pallas_call(
        body, grid=(l // tm,), in_specs=[row, vec, row],
        out_specs=[row, row, vec, pl.BlockSpec((1, 1), lambda i: (0, 0))],
        out_shape=[jax.ShapeDtypeStruct((l, d), F32), jax.ShapeDtypeStruct((l, d), BF),
                   jax.ShapeDtypeStruct((1, d), F32), jax.ShapeDtypeStruct((1, 1), F32)],
        compiler_params=_params(("arbitrary",)), name="loss_head")(x, g, target)


CONV_HALO = 8


def _conv_pre(x_ref, w_ref, b_ref, i, tr):
    r0 = pl.multiple_of(i * tr, tr)
    cur = x_ref[pl.ds(r0, tr), :]
    prev = x_ref[pl.ds(pl.multiple_of(jnp.maximum(r0 - CONV_HALO, 0), CONV_HALO), CONV_HALO), :]
    prev = jnp.where(i > 0, prev, 0.0)
    ext = jnp.concatenate([prev, cur], axis=0)
    taps = []
    for k in range(CONV_TAPS):
        s = CONV_TAPS - 1 - k
        taps.append(cur if s == 0 else pltpu.roll(ext, s, 0)[CONV_HALO:])
    pre = b_ref[...] + sum(w_ref[k:k + 1, :] * taps[k] for k in range(CONV_TAPS))
    return r0, pre, taps


def _conv_fwd(pm, col_off, conv_w, conv_b, cw=256, tr=512):
    l = pm.shape[0]
    c = conv_w.shape[1]
    cw, tr = _t(c, cw), _t(l, tr)
    assert col_off % cw == 0

    def body(x_ref, w_ref, b_ref, o_ref):
        def step(i, carry):
            r0, pre, _ = _conv_pre(x_ref, w_ref, b_ref, i, tr)
            o_ref[pl.ds(r0, tr), :] = _silu(pre)
            return carry
        lax.fori_loop(0, l // tr, step, 0)

    return pl.pallas_call(
        body, grid=(c // cw,),
        in_specs=[pl.BlockSpec((l, cw), lambda j: (0, col_off // cw + j)),
                  pl.BlockSpec((CONV_TAPS, cw), lambda j: (0, j)), pl.BlockSpec((1, cw), lambda j: (0, j))],
        out_specs=pl.BlockSpec((l, cw), lambda j: (0, j)), out_shape=jax.ShapeDtypeStruct((l, c), F32),
        compiler_params=_params(("parallel",)), name="conv_fwd")(pm, conv_w, conv_b)


def _conv_bwd(pm, col_off, conv_w, conv_b, dy, dproj, cw=256, tr=512):
    l = pm.shape[0]
    c = conv_w.shape[1]
    cw, tr = _t(c, cw), _t(l, tr)
    nt = l // tr

    def body(x_ref, w_ref, b_ref, dy_ref, _, dx_ref, dw_ref, db_ref, dpre_ref):
        def step1(i, carry):
            dws, db = carry
            r0, pre, taps = _conv_pre(x_ref, w_ref, b_ref, i, tr)
            dpre = dy_ref[pl.ds(r0, tr), :] * _dsilu(pre)
            dpre_ref[pl.ds(r0, tr), :] = dpre
            dws = tuple(dws[k] + jnp.sum(dpre * taps[k], axis=0, keepdims=True) for k in range(CONV_TAPS))
            return dws, db + jnp.sum(dpre, axis=0, keepdims=True)

        z = jnp.zeros((1, cw), F32)
        dws, db = lax.fori_loop(0, nt, step1, ((z,) * CONV_TAPS, z))
        for k in range(CONV_TAPS):
            dw_ref[k:k + 1, :] = dws[k]
        db_ref[...] = db

        def step2(i, carry):
            r0 = pl.multiple_of(i * tr, tr)
            cur = dpre_ref[pl.ds(r0, tr), :]
            nxt = dpre_ref[pl.ds(pl.multiple_of(jnp.minimum(r0 + tr, l - CONV_HALO), CONV_HALO), CONV_HALO), :]
            nxt = jnp.where(i < nt - 1, nxt, 0.0)
            ext = jnp.concatenate([cur, nxt], axis=0)
            acc = w_ref[CONV_TAPS - 1:CONV_TAPS, :] * cur
            for k in range(CONV_TAPS - 1):
                s = CONV_TAPS - 1 - k
                acc = acc + w_ref[k:k + 1, :] * pltpu.roll(ext, tr + CONV_HALO - s, 0)[:tr]
            dx_ref[pl.ds(r0, tr), :] = acc.astype(dx_ref.dtype)
            return carry
        lax.fori_loop(0, nt, step2, 0)

    col = pl.BlockSpec((l, cw), lambda j: (0, j))
    shifted = pl.BlockSpec((l, cw), lambda j: (0, col_off // cw + j))
    return pl.pallas_call(
        body, grid=(c // cw,),
        in_specs=[shifted, pl.BlockSpec((CONV_TAPS, cw), lambda j: (0, j)), pl.BlockSpec((1, cw), lambda j: (0, j)),
                  col, pl.BlockSpec(memory_space=pl.ANY)],
        out_specs=[shifted, pl.BlockSpec((CONV_TAPS, cw), lambda j: (0, j)), pl.BlockSpec((1, cw), lambda j: (0, j))],
        out_shape=[jax.ShapeDtypeStruct(dproj.shape, dproj.dtype), jax.ShapeDtypeStruct((CONV_TAPS, c), F32),
                   jax.ShapeDtypeStruct((1, c), F32)],
        scratch_shapes=[pltpu.VMEM((l, cw), F32)], input_output_aliases={4: 0},
        compiler_params=_params(("parallel",)), name="conv_bwd")(pm, conv_w, conv_b, dy, dproj)


def _split(x, pieces):
    out, rest = [], x
    for _ in range(pieces):
        piece = rest.astype(BF)
        out.append(piece)
        rest = rest - piece.astype(F32)
    return out


def _rows_times(xs, m_stack, pieces):
    x = xs[0] if len(xs) == 1 else jnp.concatenate(xs, axis=0)
    out = _bdot(jnp.concatenate(_split(x, pieces), axis=1), m_stack)
    sizes = [v.shape[0] for v in xs]
    offs = np.cumsum([0] + sizes)
    return [out[offs[i]:offs[i + 1]] for i in range(len(xs))]


def _times_rows(m_stack, x, pieces):
    return _bdot(m_stack, jnp.concatenate(_split(x, pieces), axis=0))


EXPAND_PIECES = 3
FOLD_PIECES = 2


def _ssd_consts(r, p, t):
    assert p == t, "heads expand to P lanes of the inputs and to T lanes of the decay matrices alike"
    assert LANES % p == 0 and (r * p) % LANES == 0, "whole heads per lane tile, whole lane tiles per group"
    rp = r * p
    tri = np.tril(np.ones((t, t), np.float32))
    ep = np.zeros((HEAD_LANES, rp), np.float32)
    ep[np.arange(rp) // p, np.arange(rp)] = 1.0
    itile = (np.arange(t)[:, None] == (np.arange(rp) % t)[None, :]).astype(np.float32)
    lmask = (np.arange(t)[:, None] >= (np.arange(rp) % t)[None, :]).astype(np.float32)
    return [jnp.asarray(np.concatenate([ep] * EXPAND_PIECES, axis=0), BF),
            jnp.asarray(np.concatenate([ep.T] * FOLD_PIECES, axis=0), BF),
            jnp.asarray(np.concatenate([tri] * EXPAND_PIECES, axis=1), BF),
            jnp.asarray(np.concatenate([tri.T] * EXPAND_PIECES, axis=1), BF),
            jnp.asarray(itile), jnp.asarray(lmask)]


def _ssd_heads(dtr_ref, dtb_ref, alog_ref, e_stack, tri_stack, g_n, t):
    pre = [dtr_ref[g] + dtb_ref[g] for g in range(g_n)]
    dt = [jnp.maximum(v, 0.0) + jnp.log(1.0 + jnp.exp(-jnp.abs(v))) for v in pre]
    a = [-jnp.exp(alog_ref[g]) for g in range(g_n)]
    adt = jnp.concatenate([dt[g] * a[g] for g in range(g_n)], axis=1)
    cs = _times_rows(tri_stack, adt, EXPAND_PIECES)
    cs = [cs[:, g * HEAD_LANES:(g + 1) * HEAD_LANES] for g in range(g_n)]
    at_lanes = _rows_times([v for g in range(g_n) for v in (dt[g], cs[g])], e_stack, EXPAND_PIECES)
    return [(pre[g], dt[g], a[g], cs[g][t - 1:t, :], at_lanes[2 * g], at_lanes[2 * g + 1]) for g in range(g_n)]


LANES = 128


def _head_lanes(k, p, rows):
    lane = lax.broadcasted_iota(jnp.int32, (rows, LANES), 1)
    j = k % (LANES // p)
    return jnp.logical_and(lane >= j * p, lane < (j + 1) * p)


def _block_diagonal(v, r, p):
    rows = []
    for k in range(r):
        c = k * p // LANES
        tiles = [jnp.zeros((v.shape[0], LANES), v.dtype)] * (r * p // LANES)
        tiles[c] = jnp.where(_head_lanes(k, p, v.shape[0]), v[:, c * LANES:(c + 1) * LANES], jnp.zeros((), v.dtype))
        rows.append(jnp.concatenate(tiles, axis=1))
    return jnp.concatenate(rows, axis=0)


def _diagonal_blocks(m, r, p):
    t = m.shape[0] // r
    tiles = []
    for c in range(r * p // LANES):
        heads = range(c * LANES // p, (c + 1) * LANES // p)
        acc = None
        for k in heads:
            blk = m[k * t:(k + 1) * t, c * LANES:(c + 1) * LANES]
            acc = blk if acc is None else jnp.where(_head_lanes(k, p, t), blk, acc)
        tiles.append(acc)
    return jnp.concatenate(tiles, axis=1)


def _ssd_common(xs, bm, dtx, csx, itile, lmask, r, t):
    ecsx = jnp.exp(csx)
    lastx = csx[t - 1:t, :]
    wx = jnp.exp(lastx - csx)
    elastx = jnp.exp(lastx)
    csrow = jnp.sum(csx * itile, axis=0, keepdims=True)
    lx = jnp.exp(jnp.where(lmask > 0.0, csx - csrow, NEG_BIG))
    xdt = xs * dtx
    xblk = _block_diagonal(xdt.astype(BF), r, t)
    btile = jnp.concatenate([bm.astype(BF)] * r, axis=0)
    return ecsx, wx, elastx, lx, xdt, xblk, btile


def _ssd_specs(l, g_n, r, p, n, t, conv, rev):
    nc = l // t
    rp = r * p
    cidx = (lambda c: nc - 1 - c) if rev else (lambda c: c)
    row_spec = lambda width: pl.BlockSpec((t, width), lambda c: (cidx(c), 0))
    dtr_spec = pl.BlockSpec((g_n, t, HEAD_LANES), lambda c: (0, cidx(c), 0))
    par_spec = pl.BlockSpec((g_n, 1, HEAD_LANES), lambda c: (0, 0, 0))
    dskx_spec = pl.BlockSpec((g_n, 1, rp), lambda c: (0, 0, 0))
    st_spec = pl.BlockSpec((None, g_n, n, rp), lambda c: (cidx(c), 0, 0, 0))
    return nc, row_spec, dtr_spec, par_spec, dskx_spec, st_spec


def _const_specs(consts):
    return [pl.BlockSpec(a.shape, lambda c: (0, 0)) for a in consts]


def _ssd_fwd(xbc, dtr, dtb, alog, dskx, pm, norm_g, dims, exch=None):
    h, g_n, r, p, n, t = dims
    l, conv = xbc.shape
    rp, hp = r * p, h * p
    nc, row_spec, dtr_spec, par_spec, dskx_spec, st_spec = _ssd_specs(l, g_n, r, p, n, t, conv, False)
    consts = _ssd_consts(r, p, t)

    def body(x_ref, dtr_ref, dtb_ref, alog_ref, dskx_ref, z_ref, ng_ref,
             e_ref, et_ref, tri_ref, trit_ref, it_ref, lm_ref,
             y_ref, st_ref, yn_ref, rn_ref, s_ref):
        @pl.when(pl.program_id(0) == 0)
        def _():
            s_ref[...] = jnp.zeros_like(s_ref)

        heads = _ssd_heads(dtr_ref, dtb_ref, alog_ref, e_ref[...], tri_ref[...], g_n, t)
        for g in range(g_n):
            xs = x_ref[:, g * rp:(g + 1) * rp]
            bm = x_ref[:, hp + g * n:hp + (g + 1) * n]
            cm = x_ref[:, hp + (g_n + g) * n:hp + (g_n + g + 1) * n]
            dtx, csx = heads[g][4:]
            ecsx, wx, elastx, lx, xdt, xblk, btile = _ssd_common(xs, bm, dtx, csx, it_ref[...], lm_ref[...], r, t)
            s_in = s_ref[g]
            st_ref[g] = s_in
            cbx = _bdot(cm, btile, NT)
            yd = _bdot(cbx * lx, xblk)
            yo = ecsx * _bdot(cm, s_in)
            y_ref[:, g * rp:(g + 1) * rp] = yd + yo + dskx_ref[g] * xs
            s_ref[g] = elastx * s_in + _bdot(bm, xdt * wx, TN)

        v = y_ref[...] * _silu(z_ref[...])
        rr = lax.rsqrt(jnp.mean(v * v, axis=-1, keepdims=True) + EPS)
        yn_ref[...] = (v * rr * ng_ref[...]).astype(BF)
        rn_ref[...] = rr

    return _hosted_call(
        "ssd_fwd", body, (nc,),
        [row_spec(conv), dtr_spec, par_spec, par_spec, dskx_spec, row_spec(hp),
         pl.BlockSpec((1, hp), lambda c: (0, 0)), *_const_specs(consts)],
        [row_spec(hp), st_spec, row_spec(hp), row_spec(1)],
        [jax.ShapeDtypeStruct((l, hp), F32), jax.ShapeDtypeStruct((nc, g_n, n, rp), F32),
         jax.ShapeDtypeStruct((l, hp), BF), jax.ShapeDtypeStruct((l, 1), F32)],
        [pltpu.VMEM((g_n, n, rp), F32)], ("arbitrary",),
        (xbc, dtr, dtb, alog, dskx, pm, norm_g, *consts), exch)


def _ssd_bwd(xbc, dtr, dtb, alog, dskx, states, dyn, y, pm, rstd, norm_g, dims, after=()):
    h, g_n, r, p, n, t = dims
    l, conv = xbc.shape
    rp, hp = r * p, h * p
    nc, row_spec, dtr_spec, par_spec, dskx_spec, st_spec = _ssd_specs(l, g_n, r, p, n, t, conv, True)
    consts = _ssd_consts(r, p, t)

    def fold_rows(v, rows):
        return sum(v[k * rows:(k + 1) * rows, :] for k in range(r))

    def body(x_ref, dtr_ref, dtb_ref, alog_ref, dskx_ref, st_ref, dyn_ref, y_ref, z_ref, rn_ref, ng_ref,
             e_ref, et_ref, tri_ref, trit_ref, it_ref, lm_ref,
             dx_ref, ddt_ref, dbias_ref, dalog_ref, dd_ref, dz_ref, dng_ref, ds_ref, dy_ref):
        @pl.when(pl.program_id(0) == 0)
        def _():
            ds_ref[...] = jnp.zeros_like(ds_ref)
            dbias_ref[...] = jnp.zeros_like(dbias_ref)
            dalog_ref[...] = jnp.zeros_like(dalog_ref)
            dd_ref[...] = jnp.zeros_like(dd_ref)
            dng_ref[...] = jnp.zeros_like(dng_ref)

        yv, zv, rr, dn = y_ref[...], z_ref[...], rn_ref[...], dyn_ref[...]
        sz = _silu(zv)
        vhat = yv * sz * rr
        dvh = dn * ng_ref[...]
        dv = rr * (dvh - vhat * jnp.mean(dvh * vhat, axis=-1, keepdims=True))
        dy_ref[...] = dv * sz
        dz_ref[...] = (dv * yv * _dsilu(zv)).astype(BF)
        dng_ref[...] += jnp.sum(dn * vhat, axis=0, keepdims=True)

        itile = it_ref[...]
        last_row = lax.broadcasted_iota(jnp.int32, (t, HEAD_LANES), 0) == t - 1
        heads = _ssd_heads(dtr_ref, dtb_ref, alog_ref, e_ref[...], tri_ref[...], g_n, t)
        to_fold = []
        for g in range(g_n):
            xs = x_ref[:, g * rp:(g + 1) * rp]
            bm = x_ref[:, hp + g * n:hp + (g + 1) * n]
            cm = x_ref[:, hp + (g_n + g) * n:hp + (g_n + g + 1) * n]
            dy = dy_ref[:, g * rp:(g + 1) * rp]
            dtx, csx = heads[g][4:]
            ecsx, wx, elastx, lx, xdt, xblk, btile = _ssd_common(xs, bm, dtx, csx, itile, lm_ref[...], r, t)
            s_in = st_ref[g]
            ds_out = ds_ref[g]

            cbx = _bdot(cm, btile, NT)
            amat = cbx * lx
            da = _bdot(dy, xblk, NT)
            dxdt = _diagonal_blocks(_bdot(amat, dy, TN), r, t)
            dcbx = da * lx
            q = da * amat
            dc = _bdot(dcbx, btile)
            db = fold_rows(_bdot(dcbx, cm, TN), t)

            g0 = _bdot(cm, s_in)
            dg0 = dy * ecsx
            dc = dc + _bdot(dg0, s_in, NT)

            z = _bdot(bm, ds_out)
            dxdt = dxdt + z * wx
            db = db + _bdot(xdt * wx, ds_out, NT)
            ds_ref[g] = elastx * ds_out + _bdot(cm, dg0, TN)

            dx_ref[:, g * rp:(g + 1) * rp] = dskx_ref[g] * dy + dxdt * dtx
            dx_ref[:, hp + g * n:hp + (g + 1) * n] = db
            dx_ref[:, hp + (g_n + g) * n:hp + (g_n + g + 1) * n] = dc

            dwx = z * xdt * wx
            dlastx = jnp.sum(dwx, axis=0, keepdims=True) + jnp.sum(ds_out * s_in, axis=0, keepdims=True) * elastx
            rows = jnp.concatenate([jnp.sum(dy * xs, axis=0, keepdims=True), dlastx, jnp.zeros((14, rp), F32)], axis=0)
            to_fold += [q - itile * jnp.sum(q, axis=0, keepdims=True) + dy * g0 * ecsx - dwx, dxdt * xs, rows]

        folded = _rows_times(to_fold, et_ref[...], FOLD_PIECES)
        dcs = []
        for g in range(g_n):
            f_cs, f_rows = folded[3 * g], folded[3 * g + 2]
            dd_ref[g] += f_rows[0:1]
            dcs.append(f_cs + jnp.where(last_row, f_rows[1:2], 0.0))
        dadt_all = _times_rows(trit_ref[...], jnp.concatenate(dcs, axis=1), EXPAND_PIECES)
        for g in range(g_n):
            pre, dt, a = heads[g][:3]
            dadt = dadt_all[:, g * HEAD_LANES:(g + 1) * HEAD_LANES]
            dalog_ref[g] += jnp.sum(dadt * dt, axis=0, keepdims=True) * a
            dpre = (dadt * a + folded[3 * g + 1]) * (1.0 / (1.0 + jnp.exp(-pre)))
            ddt_ref[g] = dpre
            dbias_ref[g] += jnp.sum(dpre, axis=0, keepdims=True)

    par_shape = jax.ShapeDtypeStruct((g_n, 1, HEAD_LANES), F32)
    vec = pl.BlockSpec((1, hp), lambda c: (0, 0))
    return _hosted_call(
        "ssd_bwd", body, (nc,),
        [row_spec(conv), dtr_spec, par_spec, par_spec, dskx_spec, st_spec, row_spec(hp), row_spec(hp), row_spec(hp),
         row_spec(1), vec, *_const_specs(consts)],
        [row_spec(conv), dtr_spec, par_spec, par_spec, par_spec, row_spec(hp), vec],
        [jax.ShapeDtypeStruct((l, conv), F32), jax.ShapeDtypeStruct((g_n, l, HEAD_LANES), F32),
         par_shape, par_shape, par_shape, jax.ShapeDtypeStruct((l, pm.shape[1]), BF),
         jax.ShapeDtypeStruct((1, hp), F32)],
        [pltpu.VMEM((g_n, n, rp), F32), pltpu.VMEM((t, hp), F32)], ("arbitrary",),
        (xbc, dtr, dtb, alog, dskx, states, dyn, y, pm, rstd, norm_g, *consts), after=after)


POOL_HALO = 16


def _pool_mix(name, src, dp, backward, out_dtype, into=None, cw=256, tr=512):
    l = src.shape[0]
    gd = dp // len(POOL_WINDOWS)
    cwl, trl = _t(gd, cw), _t(l, tr)
    nt = l // trl

    def body(x_ref, *rest):
        o_ref = rest[-1]
        gi = pl.program_id(0)
        for wi, win in enumerate(POOL_WINDOWS):
            @pl.when(gi == wi)
            def _(win=win):
                def step(i, carry):
                    r0 = pl.multiple_of(i * trl, trl)
                    cur = x_ref[pl.ds(r0, trl), :]
                    trow = r0 + lax.broadcasted_iota(jnp.int32, (trl, 1), 0)
                    cnt = jnp.minimum(trow + 1, win).astype(F32)
                    if not backward:
                        halo = x_ref[pl.ds(pl.multiple_of(jnp.maximum(r0 - POOL_HALO, 0), POOL_HALO), POOL_HALO), :]
                        halo = jnp.where(i > 0, halo, 0.0)
                        s = jnp.concatenate([halo, cur], axis=0)
                        sh = 1
                        while sh < win:
                            s = s + pltpu.roll(s, sh, 0)
                            sh *= 2
                        res = s[POOL_HALO:] / cnt - cur
                    else:
                        halo = x_ref[pl.ds(pl.multiple_of(jnp.minimum(r0 + trl, l - POOL_HALO), POOL_HALO),
                                           POOL_HALO), :]
                        hrow = r0 + trl + lax.broadcasted_iota(jnp.int32, (POOL_HALO, 1), 0)
                        hcnt = jnp.minimum(hrow + 1, win).astype(F32)
                        halo = jnp.where(i < nt - 1, halo / hcnt, 0.0)
                        s = jnp.concatenate([cur / cnt, halo], axis=0)
                        sh = 1
                        while sh < win:
                            s = s + pltpu.roll(s, trl + POOL_HALO - sh, 0)
                            sh *= 2
                        res = s[:trl] - cur
                    o_ref[pl.ds(r0, trl), :] = res.astype(o_ref.dtype)
                    return carry
                lax.fori_loop(0, nt, step, 0)

    col = pl.BlockSpec((l, cwl), lambda g, j: (0, g * (gd // cwl) + j))
    if into is None:
        return pl.pallas_call(
            body, grid=(len(POOL_WINDOWS), gd // cwl), in_specs=[col], out_specs=col,
            out_shape=jax.ShapeDtypeStruct((l, dp), out_dtype),
            compiler_params=_params(("parallel", "parallel")), name=name)(src)
    return pl.pallas_call(
        body, grid=(len(POOL_WINDOWS), gd // cwl), in_specs=[col, pl.BlockSpec(memory_space=pl.ANY)], out_specs=col,
        out_shape=jax.ShapeDtypeStruct(into.shape, into.dtype), input_output_aliases={1: 0},
        compiler_params=_params(("parallel", "parallel")), name=name)(src, into)


def _pool_out_bwd_data(dx, wp_out, mg, pu, scale, tm=1024, tn=512):
    l, d = dx.shape
    dp = wp_out.shape[0]
    tm, tn = _t(l, tm), _t(dp, tn)

    def epi(dyp, m, gt, sc):
        sg = _silu(gt)
        return dyp * sc * sg, dyp * m * sc * _dsilu(gt), jnp.sum(dyp * m * sg, axis=0, keepdims=True)

    tile = pl.BlockSpec((tm, tn), lambda j, i, k: (i, j))
    right = pl.BlockSpec((tm, tn), lambda j, i, k: (i, dp // tn + j))
    vec = pl.BlockSpec((1, tn), lambda j, i, k: (0, j))
    return _mm("pool_out_bwd_data", (dp // tn, l // tm, 1),
               dx, pl.BlockSpec((tm, d), lambda j, i, k: (i, 0)),
               wp_out, pl.BlockSpec((tn, d), lambda j, i, k: (j, 0)), NT,
               [jax.ShapeDtypeStruct((l, dp), BF), jax.ShapeDtypeStruct((l, 2 * dp), BF),
                jax.ShapeDtypeStruct((1, dp), F32)], [tile, right, vec], (tm, tn),
               (mg, pu, scale), (tile, right, vec), epi, summed=(2, 1))


def _group_mm_fwd(mp, wg, pu, scale, tm=1024, tn=1024, tk=1024):
    l, dp = mp.shape
    ng, gd = wg.shape[0], wg.shape[1]
    tm, tn, tk = _t(l, tm), _t(gd, tn), _t(gd, tk)

    def epi(acc, gate, sc):
        return acc, acc * sc * _silu(gate)

    out = pl.BlockSpec((tm, tn), lambda g, i, j, k: (i, g * (gd // tn) + j))
    return _mm("group_mm_fwd", (ng, l // tm, gd // tn, gd // tk),
               mp, pl.BlockSpec((tm, tk), lambda g, i, j, k: (i, g * (gd // tk) + k)),
               wg, pl.BlockSpec((None, tk, tn), lambda g, i, j, k: (g, k, j)), NN,
               [jax.ShapeDtypeStruct((l, dp), F32), jax.ShapeDtypeStruct((l, dp), BF)], [out, out], (tm, tn),
               (pu, scale),
               (pl.BlockSpec((tm, tn), lambda g, i, j, k: (i, (dp + g * gd) // tn + j)),
                pl.BlockSpec((1, tn), lambda g, i, j, k: (0, g * (gd // tn) + j))), epi)


def _group_mm_bwd_data(dmg, wg, after=(), tm=1024, tn=1024, tk=1024):
    l, dp = dmg.shape
    ng, gd = wg.shape[0], wg.shape[1]
    tm, tn, tk = _t(l, tm), _t(gd, tn), _t(gd, tk)
    return _mm("group_mm_bwd_data", (ng, l // tm, gd // tn, gd // tk),
               dmg, pl.BlockSpec((tm, tk), lambda g, i, j, k: (i, g * (gd // tk) + k)),
               wg, pl.BlockSpec((None, tn, tk), lambda g, i, j, k: (g, j, k)), NT,
               [jax.ShapeDtypeStruct((l, dp), F32)],
               [pl.BlockSpec((tm, tn), lambda g, i, j, k: (i, g * (gd // tn) + j))], (tm, tn), after=after)[0]


def _group_mm_bwd_weight(mp, dmg, ng, tm=512, tk=1024):
    l, dp = mp.shape
    gd = dp // ng
    tm, tk = _t(gd, tm), _t(l, tk)
    return _mm("group_mm_bwd_weight", (ng, gd // tm, l // tk),
               mp, pl.BlockSpec((tk, tm), lambda g, i, k: (k, g * (gd // tm) + i)),
               dmg, pl.BlockSpec((tk, gd), lambda g, i, k: (k, g)), TN,
               [jax.ShapeDtypeStruct((ng, gd, gd // 2), F32)],
               [pl.BlockSpec((None, tm, gd // 2), lambda g, i, k: (g, i, 0))], (tm, gd), epi=_pack_epi)[0]


def _cast_bf16(name, w, tr=256):
    r, c = w.shape
    tr = _row_tile(r, tr, 16)

    def body(w_ref, o_ref):
        o_ref[...] = w_ref[...].astype(BF)

    blk = pl.BlockSpec((tr, c), lambda i: (i, 0))
    return pl.pallas_call(body, grid=(r // tr,), in_specs=[blk], out_specs=blk,
                          out_shape=jax.ShapeDtypeStruct((r, c), BF), compiler_params=_params(("parallel",)),
                          name=name)(w)


def _pack_rows(name, w, tr=256):
    r, c = w.shape
    tr = _row_tile(r, tr)

    def body(w_ref, o_ref):
        o_ref[...] = _pack_pairs(w_ref[...])

    return pl.pallas_call(body, grid=(r // tr, 2), in_specs=[pl.BlockSpec((tr, c // 2), lambda i, q: (i, q))],
                          out_specs=pl.BlockSpec((None, tr, c // 4), lambda i, q: (q, i, 0)),
                          out_shape=jax.ShapeDtypeStruct((2, r, c // 4), F32),
                          compiler_params=_params(("parallel", "parallel")), name=name)(w)


def _unpack_rows(name, w0, w1, tr=512):
    r, h = w0.shape
    tr = _row_tile(r, tr, 16)

    def body(w0_ref, w1_ref, o_ref):
        for q, w_ref in enumerate((w0_ref, w1_ref)):
            hi, lo = _unpack_pairs(w_ref[...])
            o_ref[q, :, :h] = hi.astype(BF)
            o_ref[q, :, h:] = lo.astype(BF)

    words = pl.BlockSpec((tr, h), lambda i: (i, 0))
    return pl.pallas_call(body, grid=(r // tr,), in_specs=[words, words],
                          out_specs=pl.BlockSpec((2, tr, 2 * h), lambda i: (0, i, 0)),
                          out_shape=jax.ShapeDtypeStruct((2, r, 2 * h), BF), compiler_params=_params(("parallel",)),
                          name=name)(w0, w1)


def _reduce_packed(name, recv, tr=256):
    nd, r, h = recv.shape
    tr = _row_tile(r, tr)

    def body(p_ref, o_ref):
        hi, lo = _unpack_pairs(p_ref[0])
        for k in range(1, nd):
            a, b = _unpack_pairs(p_ref[k])
            hi, lo = hi + a, lo + b
        o_ref[:, :h] = hi
        o_ref[:, h:] = lo

    return pl.pallas_call(body, grid=(r // tr,), in_specs=[pl.BlockSpec((nd, tr, h), lambda i: (0, i, 0))],
                          out_specs=pl.BlockSpec((tr, 2 * h), lambda i: (i, 0)),
                          out_shape=jax.ShapeDtypeStruct((r, 2 * h), F32), compiler_params=_params(("parallel",)),
                          name=name)(recv)


def _adamw_math(w, g, m, v):
    m2 = ADAM_B1 * m + (1.0 - ADAM_B1) * g
    v2 = ADAM_B2 * v + (1.0 - ADAM_B2) * (g * g)
    m_hat = m2 / (1.0 - ADAM_B1 ** ADAM_STEP)
    v_hat = v2 / (1.0 - ADAM_B2 ** ADAM_STEP)
    delta = -ADAM_LR * (m_hat / (jnp.sqrt(v_hat) + ADAM_EPS) + ADAM_WD * w)
    return delta, m2, v2


def _adamw(name, w, g, m, v, tr=256):
    r, c = w.shape
    tr = _row_tile(r, tr)

    def body(w_ref, g_ref, m_ref, v_ref, d_ref, m2_ref, v2_ref):
        d, m2, v2 = _adamw_math(w_ref[...], g_ref[...], m_ref[...], v_ref[...])
        d_ref[...] = d
        m2_ref[...] = m2
        v2_ref[...] = v2

    blk = pl.BlockSpec((tr, c), lambda i: (i, 0))
    shp = jax.ShapeDtypeStruct((r, c), F32)
    return pl.pallas_call(body, grid=(r // tr,), in_specs=[blk] * 4, out_specs=[blk] * 3, out_shape=[shp] * 3,
                          compiler_params=_params(("parallel",)), name=name)(w, g, m, v)


def _sum_slots(name, a):
    nd, r, c = a.shape

    def body(a_ref, o_ref):
        s = a_ref[0]
        for k in range(1, nd):
            s = s + a_ref[k]
        o_ref[...] = s

    return pl.pallas_call(body, out_shape=jax.ShapeDtypeStruct((r, c), F32), name=name)(a)


def _head_rows(v, g_n, r):
    return jnp.pad(v.reshape(g_n, 1, r), ((0, 0), (0, 0), (0, HEAD_LANES - r)))


class _Later:
    def __init__(self, hosted, first_of, weights_of, pool_w_out_of, send):
        self.hosted, self.first_of, self.weights_of = hosted, first_of, weights_of
        self.pool_w_out_of, self.send = pool_w_out_of, send


def _listed(res):
    return res if isinstance(res, (list, tuple)) else [res]


def _local_step(x, target, ln_g, final_g, conv_b, dt_bias, a_log, d_skip, norm_g, later):
    l, d = x.shape
    di = norm_g.shape[1]
    h = dt_bias.shape[1]
    p = di // h
    conv = conv_b.shape[1]
    n, t = SSM_STATE, SSD_CHUNK
    g_n = (conv - di) // (2 * n)
    r = h // g_n
    nm = di + conv
    ng = len(POOL_WINDOWS)
    dims = (h, g_n, r, p, n, t)
    pw = d

    hosted = later.hosted
    h0, r0, *arrived_0 = _rms_fwd("rms0_fwd", x, ln_g[0:1], exch=hosted.get("rms0"))
    wt_in, conv_w, scale = later.first_of(arrived_0)
    dp = scale.shape[1]
    pm, *arrived_i = _listed(_mm_nt_halves("in_proj_main", h0, wt_in, 0, nm, exch=hosted.get("in_proj")))
    dtr = _mm_nt_halves("in_proj_dt", h0, wt_in, nm, h, tn=h)
    xbc = _conv_fwd(pm, di, conv_w, conv_b)
    dtb, alog = _head_rows(dt_bias, g_n, r), _head_rows(a_log, g_n, r)
    dskx = jnp.repeat(d_skip.reshape(g_n, 1, r), p, axis=2)
    dtr_g = jnp.pad(jnp.transpose(dtr.reshape(l, g_n, r), (1, 0, 2)), ((0, 0), (0, 0), (0, HEAD_LANES - r)))
    y, states, yn, r_n, *arrived_s = _ssd_fwd(xbc, dtr_g, dtb, alog, dskx, pm, norm_g, dims, exch=hosted.get("ssd"))
    w_out, wg, wp_in = later.weights_of(arrived_i, arrived_s)
    x1 = _mm_nn("ssm_out_proj", yn, w_out, add=x)

    h1, r1 = _rms_fwd("rms1_fwd", x1, ln_g[1:2])
    pu, *arrived_p = _listed(_mm_nn_blocked("pool_in_proj", h1, wp_in, exch=hosted.get("pool_in_proj")))
    wp_out = later.pool_w_out_of(arrived_p)
    mp = _pool_mix("pool_mix_fwd", pu, dp, False, BF)
    mg, yp = _group_mm_fwd(mp, wg, pu, scale)
    x2 = _mm_nn("pool_out_proj", yp, wp_out, add=x1)

    dx2, dx2_b, d_final_g, loss = _loss_head(x2, final_g, target)

    gw_pout = _mm_tn_packed("pool_out_bwd_weight", yp, dx2_b, pw)
    going = later.send("pool_w_out", gw_pout)
    dmg, dpu, d_scale = _pool_out_bwd_data(dx2_b, wp_out, mg, pu, scale)
    dmp = _group_mm_bwd_data(dmg, wg, after=going)
    gw_g = _group_mm_bwd_weight(mp, dmg, ng)
    going = later.send("pool_w_group", gw_g)
    dpu = _pool_mix("pool_mix_bwd", dmp, dp, True, BF, into=dpu)
    dh1 = _mm_nt_blocked("pool_in_bwd_data", dpu, wp_in, after=going)
    gw_pin = _mm_tn_packed("pool_in_bwd_weight", h1, dpu, 2 * dp // N_DEV)
    going = later.send("pool_w_in", gw_pin)
    dx1, dx1_b, d_ln1 = _rms_bwd("rms1_bwd", dh1, x1, r1, ln_g[1:2], dx2)

    dyn = _mm_nt("ssm_out_bwd_data", dx1_b, w_out, 0, di, tn=1024, after=going)
    gw_out = _mm_tn_packed("ssm_out_bwd_weight", yn, dx1_b, pw)
    going = later.send("ssm_w_out", gw_out)
    dxbc, ddt_g, dbias_g, dalog_g, dd_g, dproj, d_norm_g = _ssd_bwd(
        xbc, dtr_g, dtb, alog, dskx, states, dyn, y, pm, r_n, norm_g, dims, after=going)
    dproj, d_conv_w, d_conv_b = _conv_bwd(pm, di, conv_w, conv_b, dxbc, dproj)
    ddt = jnp.transpose(ddt_g[:, :, :r], (1, 0, 2)).reshape(l, h)
    gwt_in, going = [], ()
    for q in range(2):
        gq = _mm_tn_packed(f"in_proj_bwd_weight_{q}", dproj, h0, d // 2, rows=nm + h, b_blk=q, after=going)
        gq = _mm_tn_packed(f"in_proj_bwd_weight_dt_{q}", ddt, h0, d // 2, into=gq, b_blk=q)
        going = later.send(f"ssm_w_in_{q}", gq)
        gwt_in.append(gq)
    gwt_in = jnp.concatenate(gwt_in, axis=0)
    dh0 = _mm_nn_halves("in_proj_bwd_data_dt", ddt, wt_in, nm, after=going)
    dh0 = _mm_nn_halves("in_proj_bwd_data", dproj, wt_in, 0, add=dh0)
    grad_x, _, d_ln0 = _rms_bwd("rms0_bwd", dh0, x, r0, ln_g[0:1], dx1)

    def heads(v):
        return v[:, 0, :r].reshape(1, h)

    small = dict(ln_g=jnp.concatenate([d_ln0, d_ln1], axis=0), final_g=d_final_g, conv_w=d_conv_w, conv_b=d_conv_b,
                 dt_bias=heads(dbias_g), a_log=heads(dalog_g), d_skip=heads(dd_g), norm_g=d_norm_g, scale=d_scale)
    big = dict(ssm_w_in=gwt_in, ssm_w_out=gw_out, pool_w_in=gw_pin, pool_w_group=gw_g, pool_w_out=gw_pout)
    return loss, grad_x, small, big


SMALL_ORDER = ("ln_g", "final_g", "conv_w", "conv_b", "dt_bias", "a_log", "d_skip", "norm_g", "scale", "loss")


def _flatten_small(parts):
    flat = jnp.concatenate([parts[k].reshape(-1) for k in SMALL_ORDER])
    n = flat.shape[0]
    rows = -(-n // 1024) * 8
    return jnp.pad(flat, (0, rows * 128 - n)).reshape(rows, 128)


def _split_small(flat, shapes):
    flat = flat.reshape(-1)
    out, off = {}, 0
    for k in SMALL_ORDER:
        size = int(np.prod(shapes[k]))
        out[k] = flat[off:off + size].reshape(shapes[k])
        off += size
    return out


def kernel(x, ln_g, final_g, ssm_w_in, ssm_conv_w, ssm_conv_b, ssm_dt_bias, ssm_a_log, ssm_d, ssm_norm_g, ssm_w_out, pool_w_in, pool_w_group, pool_scale, pool_w_out, loss_target, m_ln_g, m_final_g, m_ssm_w_in, m_ssm_conv_w, m_ssm_conv_b, m_ssm_dt_bias, m_ssm_a_log, m_ssm_d, m_ssm_norm_g, m_ssm_w_out, m_pool_w_in, m_pool_w_group, m_pool_scale, m_pool_w_out, v_ln_g, v_final_g, v_ssm_w_in, v_ssm_conv_w, v_ssm_conv_b, v_ssm_dt_bias, v_ssm_a_log, v_ssm_d, v_ssm_norm_g, v_ssm_w_out, v_pool_w_in, v_pool_w_group, v_pool_scale, v_pool_w_out):
    l, d = x.shape[1], x.shape[2]
    me = 4 * lax.axis_index("x") + 2 * lax.axis_index("y") + lax.axis_index("c")
    ng, gds, gd = pool_w_group.shape[1], pool_w_group.shape[2], pool_w_group.shape[3]
    sin_s = ssm_w_in.shape[2]
    dp_s = pool_w_out.shape[1]
    conv_s = ssm_conv_w.shape[2]

    wt_in_s = _pack_rows("pack_w_in", jnp.transpose(ssm_w_in[0]))
    w_out_s = _cast_bf16("cast_w_out", ssm_w_out[0])
    wp_in_s = _cast_bf16("cast_pool_w_in", pool_w_in[0])
    wg_s = _cast_bf16("cast_pool_w_group", pool_w_group[0].reshape(ng * gds, gd))
    wp_out_s = _cast_bf16("cast_pool_w_out", pool_w_out[0])
    small_s = jnp.concatenate([ssm_conv_w[0].reshape(-1), pool_scale[0]])
    n_small = small_s.shape[0]
    small_s = jnp.pad(small_s, (0, -(-n_small // 1024) * 1024 - n_small)).reshape(-1, 128)
    hosted = dict(rms0=_Gather([wt_in_s[0], wt_in_s[1], small_s]), in_proj=_Gather([w_out_s, wg_s]),
                  ssd=_Gather([wp_in_s]), pool_in_proj=_Gather([wp_out_s]))

    def first_of(arrived_0):
        wt_in_g0, wt_in_g1, small_g = arrived_0
        wt_in = _unpack_rows("unpack_w_in", wt_in_g0.reshape(N_DEV * sin_s, d // 4),
                             wt_in_g1.reshape(N_DEV * sin_s, d // 4))
        small_all = small_g.reshape(N_DEV, -1)[:, :n_small]
        conv_w = jnp.transpose(small_all[:, :CONV_TAPS * conv_s].reshape(N_DEV, CONV_TAPS, conv_s), (1, 0, 2))
        return wt_in, conv_w.reshape(CONV_TAPS, -1), small_all[:, CONV_TAPS * conv_s:].reshape(1, -1)

    def weights_of(arrived_i, arrived_s):
        wg = jnp.transpose(arrived_i[1].reshape(N_DEV, ng, gds, gd), (1, 0, 2, 3)).reshape(ng, gd, gd)
        return arrived_i[0].reshape(-1, d), wg, arrived_s[0]

    def rows_major(gp):
        q, _, hw = gp.shape
        return jnp.transpose(gp.reshape(q, N_DEV, -1, hw), (1, 0, 2, 3))

    to_blocks = dict(ssm_w_in_0=rows_major, ssm_w_in_1=rows_major, ssm_w_out=rows_major, pool_w_out=rows_major,
                     pool_w_in=lambda gp: gp[:, None],
                     pool_w_group=lambda gp: jnp.transpose(gp.reshape(ng, N_DEV, gds, gd // 2), (1, 0, 2, 3)))
    travelling = {}

    def send(name, gp):
        *travelling[name], token = _scatter_start(f"scatter_{name}_start", to_blocks[name](gp))
        return (token,)

    later = _Later(hosted, first_of, weights_of, lambda arrived_p: arrived_p[0].reshape(-1, d), send)

    loss, grad_x, small, _ = _local_step(
        x[0], loss_target[0], ln_g, final_g.reshape(1, d), ssm_conv_b, ssm_dt_bias, ssm_a_log, ssm_d, ssm_norm_g, later)

    def arrived(name, after):
        src, land = _scatter_wait(f"scatter_{name}_wait", *travelling[name], after=after)
        own = lax.dynamic_slice_in_dim(src, me, 1, axis=0)
        return lax.dynamic_update_slice_in_dim(land, own, me, axis=0)

    def reduced(name, rv):
        q = rv.shape[1]
        cols = [_reduce_packed(f"{name}_{j}", rv[:, j]) for j in range(q)]
        return cols[0] if q == 1 else jnp.concatenate(cols, axis=1)

    grads = {}
    grads["pool_w_out"] = reduced("reduce_pool_w_out", arrived("pool_w_out", (grad_x,)))[None]
    grads["pool_w_group"] = _reduce_packed("reduce_pool_w_group", arrived("pool_w_group", (grad_x,)).reshape(
        N_DEV, ng * gds, gd // 2)).reshape(1, ng, gds, gd)
    grads["pool_w_in"] = reduced("reduce_pool_w_in", arrived("pool_w_in", (grad_x,)))[None]
    grads["ssm_w_out"] = reduced("reduce_w_out", arrived("ssm_w_out", (grad_x,)))[None]

    weights = dict(ln_g=ln_g, final_g=final_g, ssm_w_in=ssm_w_in, ssm_conv_w=ssm_conv_w, ssm_conv_b=ssm_conv_b,
                   ssm_dt_bias=ssm_dt_bias, ssm_a_log=ssm_a_log, ssm_d=ssm_d, ssm_norm_g=ssm_norm_g,
                   ssm_w_out=ssm_w_out, pool_w_in=pool_w_in, pool_w_group=pool_w_group, pool_scale=pool_scale,
                   pool_w_out=pool_w_out)
    m_in = dict(ln_g=m_ln_g, final_g=m_final_g, ssm_w_in=m_ssm_w_in, ssm_conv_w=m_ssm_conv_w, ssm_conv_b=m_ssm_conv_b,
                ssm_dt_bias=m_ssm_dt_bias, ssm_a_log=m_ssm_a_log, ssm_d=m_ssm_d, ssm_norm_g=m_ssm_norm_g,
                ssm_w_out=m_ssm_w_out, pool_w_in=m_pool_w_in, pool_w_group=m_pool_w_group, pool_scale=m_pool_scale,
                pool_w_out=m_pool_w_out)
    v_in = dict(ln_g=v_ln_g, final_g=v_final_g, ssm_w_in=v_ssm_w_in, ssm_conv_w=v_ssm_conv_w, ssm_conv_b=v_ssm_conv_b,
                ssm_dt_bias=v_ssm_dt_bias, ssm_a_log=v_ssm_a_log, ssm_d=v_ssm_d, ssm_norm_g=v_ssm_norm_g,
                ssm_w_out=v_ssm_w_out, pool_w_in=v_pool_w_in, pool_w_group=v_pool_w_group, pool_scale=v_pool_scale,
                pool_w_out=v_pool_w_out)
    names = list(weights)
    big_names = ("ssm_w_out", "pool_w_in", "pool_w_group", "pool_w_out", "ssm_w_in")
    delta, new_m, new_v = {}, {}, {}

    def adamw_big(k):
        shp = weights[k].shape
        two_d = (-1, shp[-1])
        dk, mk, vk = _adamw(f"adamw_{k}", weights[k].reshape(two_d), grads[k].reshape(two_d), m_in[k].reshape(two_d),
                            v_in[k].reshape(two_d))
        delta[k], new_m[k], new_v[k] = dk.reshape(shp), mk.reshape(shp), vk.reshape(shp)

    for k in big_names[:-1]:
        adamw_big(k)
    small_names = [k for k in names if k not in big_names]

    small["loss"] = loss
    shapes = {k: small[k].shape for k in SMALL_ORDER}
    gathered_small, = _exchange_alone("all_gather_small_grads", _Gather([_flatten_small(small)]),
                                      after=tuple(delta[k] for k in big_names[:-1]))
    summed = _split_small(_sum_slots("sum_small_grads", gathered_small), shapes)
    grads.update(ln_g=summed["ln_g"], final_g=summed["final_g"].reshape(d), ssm_conv_b=summed["conv_b"],
                 ssm_conv_w=lax.dynamic_slice_in_dim(summed["conv_w"], me * conv_s, conv_s, axis=1)[None],
                 ssm_dt_bias=summed["dt_bias"], ssm_a_log=summed["a_log"], ssm_d=summed["d_skip"],
                 ssm_norm_g=summed["norm_g"],
                 pool_scale=lax.dynamic_slice_in_dim(summed["scale"], me * dp_s, dp_s, axis=1))

    def packed(tree):
        flat = jnp.concatenate([tree[k].reshape(-1) for k in small_names])
        nn = flat.shape[0]
        return jnp.pad(flat, (0, -(-nn // 1024) * 1024 - nn), constant_values=1.0).reshape(-1, 128)

    ds, ms, vs = _adamw("adamw_small", packed(weights), packed(grads), packed(m_in), packed(v_in))
    off = 0
    for k in small_names:
        shp = weights[k].shape
        size = int(np.prod(shp))
        for res, arr in ((delta, ds), (new_m, ms), (new_v, vs)):
            res[k] = arr.reshape(-1)[off:off + size].reshape(shp)
        off += size

    k = "ssm_w_in"
    gt = jnp.concatenate([reduced(f"reduce_w_in_{q}", arrived(f"ssm_w_in_{q}", (ds,))) for q in range(2)], axis=1)
    dk, mk, vk = _adamw("adamw_ssm_w_in", jnp.transpose(weights[k][0]), gt, jnp.transpose(m_in[k][0]),
                        jnp.transpose(v_in[k][0]))
    grads[k], delta[k], new_m[k], new_v[k] = (jnp.transpose(t)[None] for t in (gt, dk, mk, vk))

    return (summed["loss"].reshape(()), grad_x[None], *[grads[k] for k in names], *[delta[k] for k in names],
            *[new_m[k] for k in names], *[new_v[k] for k in names])
```

```python
import math

import jax
import jax.numpy as jnp
import numpy as np
from jax import lax
from jax.experimental import pallas as pl
from jax.experimental.pallas import tpu as pltpu

F32 = jnp.float32
BF = jnp.bfloat16
U32 = jnp.uint32

N_DEV = 8
EPS = 1e-6
SSD_CHUNK = 64
SSM_STATE = 128
CONV_TAPS = 4
HEAD_LANES = 128
POOL_WINDOWS = (2, 4, 8, 16)
ADAM_LR, ADAM_B1, ADAM_B2, ADAM_EPS, ADAM_WD, ADAM_STEP = 0.001, 0.9, 0.999, 1e-08, 0.01, 10
VMEM_LIMIT = 56 * 1024 * 1024
NEG_BIG = -1e30
HAND_ON_AT = 7

NN = ((1,), (0,))
NT = ((1,), (1,))
TN = ((0,), (0,))
MESH = pl.DeviceIdType.MESH


def _t(dim, pref):
    return pref if dim % pref == 0 else dim


def _row_tile(rows, pref, mult=8):
    best = rows
    for cand in range(mult, min(rows, pref) + 1, mult):
        if rows % cand == 0:
            best = cand
    return best


def _params(sem=None):
    return pltpu.CompilerParams(dimension_semantics=sem, vmem_limit_bytes=VMEM_LIMIT)


def _silu(x):
    return x * (1.0 / (1.0 + jnp.exp(-x)))


def _dsilu(x):
    s = 1.0 / (1.0 + jnp.exp(-x))
    return s * (1.0 + x * (1.0 - s))


def _bdot(a, b, dims=NN):
    return lax.dot_general(a.astype(BF), b.astype(BF), (dims, ((), ())), preferred_element_type=F32)


def _pack_pairs(x):
    h = x.shape[1] // 2
    hi = lax.bitcast_convert_type(x[:, :h].astype(jnp.bfloat16).astype(F32), U32)
    lo = lax.bitcast_convert_type(x[:, h:].astype(jnp.bfloat16).astype(F32), U32)
    return lax.bitcast_convert_type(hi | (lo >> 16), F32)


def _unpack_pairs(w):
    u = lax.bitcast_convert_type(w, U32)
    hi = lax.bitcast_convert_type(u & jnp.uint32(0xFFFF0000), F32)
    lo = lax.bitcast_convert_type(u << 16, F32)
    return hi, lo


def _mesh_pos():
    return lax.axis_index("x"), lax.axis_index("y"), lax.axis_index("c")


def _slot(pos):
    return 4 * pos[0] + 2 * pos[1] + pos[2]


class _Gather:
    def __init__(self, arrays):
        self.arrays = list(arrays)
        self.out_shapes = [jax.ShapeDtypeStruct((N_DEV, *s.shape), s.dtype) for s in arrays]

    def phases(self, src, dst, send_sems, recv_sems, local_sems):
        n_arr = len(self.arrays)
        x, y, c = _mesh_pos()
        me, sibling = (x, y, c), (x, y, 1 - c)
        chips = [(1 - x, y), (x, 1 - y), (1 - x, 1 - y)]

        def copy(a, k, block, to, from_src):
            return pltpu.make_async_remote_copy(
                src_ref=src[a] if from_src else dst[a].at[_slot(block)], dst_ref=dst[a].at[_slot(block)],
                send_sem=send_sems.at[a * 7 + k], recv_sem=recv_sems.at[a * 7 + k], device_id=to, device_id_type=MESH)

        def mine(a):
            return pltpu.make_async_copy(src[a], dst[a].at[_slot(me)], local_sems.at[a])

        def first(a):
            return [copy(a, 0, me, sibling, True)] + [copy(a, 1 + j, me, (*chip, c), True)
                                                     for j, chip in enumerate(chips)]

        def start():
            for a in range(n_arr):
                mine(a).start()
                for cp in first(a):
                    cp.start()

        def middle():
            for a in range(n_arr):
                for j, chip in enumerate(chips):
                    copy(a, 1 + j, (*chip, c), me, False).wait_recv()
                    copy(a, 4 + j, (*chip, c), sibling, False).start()

        def finish():
            for a in range(n_arr):
                copy(a, 0, sibling, me, False).wait_recv()
                for j, chip in enumerate(chips):
                    copy(a, 4 + j, (*chip, 1 - c), me, False).wait_recv()
                for cp in first(a):
                    cp.wait_send()
                for j, chip in enumerate(chips):
                    copy(a, 4 + j, (*chip, c), sibling, False).wait_send()
                mine(a).wait()

        return start, middle, finish


def _hosted_call(name, body, grid, in_specs, out_specs, out_shape, scratch_shapes, sem, operands, exch=None,
                 after=(), into=None):
    if into is not None:
        n_lead = len(in_specs)
        inner = body

        def body(*refs):
            inner(*refs[:n_lead], *refs[n_lead + 1:])

        return pl.pallas_call(
            body, grid=grid, in_specs=[*in_specs, pl.BlockSpec(memory_space=pl.ANY)], out_specs=out_specs,
            out_shape=out_shape, scratch_shapes=scratch_shapes, input_output_aliases={n_lead: 0},
            compiler_params=_params(sem), name=name)(*operands, into)
    if after:
        n_lead = len(in_specs)
        inner = body

        def body(*refs):
            inner(*refs[:n_lead], *refs[n_lead + len(after):])

        in_specs = [*in_specs, *[pl.BlockSpec(memory_space=pl.ANY)] * len(after)]
        operands = (*operands, *after)
    if exch is None:
        return pl.pallas_call(body, grid=grid, in_specs=in_specs, out_specs=out_specs, out_shape=out_shape,
                              scratch_shapes=scratch_shapes, compiler_params=_params(sem), name=name)(*operands)
    n_in, n_out, n_scr, ne = len(in_specs), len(out_specs), len(scratch_shapes), len(exch.arrays)
    total = math.prod(grid)

    def wrapped(*refs):
        ins, ex_in = refs[:n_in], refs[n_in:n_in + ne]
        outs = refs[n_in + ne:n_in + ne + n_out]
        ex_out = refs[n_in + ne + n_out:n_in + 2 * ne + n_out]
        scr = refs[n_in + 2 * ne + n_out:n_in + 2 * ne + n_out + n_scr]
        step = 0
        for axis, size in enumerate(grid):
            step = step * size + pl.program_id(axis)
        start, middle, finish = exch.phases(ex_in, ex_out, *refs[-3:])
        pl.when(step == 0)(start)
        if middle is not None:
            pl.when(step == (total * HAND_ON_AT) // 8)(middle)
        body(*ins, *outs, *scr)
        pl.when(step == total - 1)(finish)

    hbm = pl.BlockSpec(memory_space=pl.ANY)
    sems = [pltpu.SemaphoreType.DMA((ne * 7,)), pltpu.SemaphoreType.DMA((ne * 7,)), pltpu.SemaphoreType.DMA((ne,))]
    return pl.pallas_call(
        wrapped, grid=grid, in_specs=[*in_specs, *[hbm] * ne], out_specs=[*out_specs, *[hbm] * ne],
        out_shape=[*out_shape, *exch.out_shapes], scratch_shapes=[*scratch_shapes, *sems],
        compiler_params=pltpu.CompilerParams(dimension_semantics=("arbitrary",) * len(grid),
                                             vmem_limit_bytes=VMEM_LIMIT, has_side_effects=True),
        name=name)(*operands, *exch.arrays)


def _flip_peers():
    x, y, c = _mesh_pos()
    peers = []
    for k in range(1, N_DEV):
        fx, fy, fc = (k >> 2) & 1, (k >> 1) & 1, k & 1
        peers.append((1 - x if fx else x, 1 - y if fy else y, 1 - c if fc else c))
    return (x, y, c), peers


def _split_scatter_copy(src, land, send_sems, recv_sems, k, me, peer, sending):
    return pltpu.make_async_remote_copy(
        src_ref=src.at[_slot(peer)], dst_ref=land.at[_slot(me) if sending else _slot(peer)],
        send_sem=send_sems.at[k], recv_sem=recv_sems.at[k], device_id=peer, device_id_type=MESH)


def _scatter_start(name, blocks):
    def body(src, land, send_sems, recv_sems, src_thru, land_thru, token):
        me, peers = _flip_peers()
        for k, peer in enumerate(peers):
            _split_scatter_copy(src, land, send_sems, recv_sems, k, me, peer, True).start()
        token[...] = jnp.zeros_like(token)

    hbm = pl.BlockSpec(memory_space=pltpu.HBM)
    sem = pl.BlockSpec(memory_space=pltpu.SEMAPHORE)
    return pl.pallas_call(
        body, name=name,
        out_shape=(pltpu.SemaphoreType.DMA((N_DEV - 1,)), pltpu.SemaphoreType.DMA((N_DEV - 1,)),
                   pltpu.HBM(blocks.shape, blocks.dtype), pltpu.HBM(blocks.shape, blocks.dtype),
                   jax.ShapeDtypeStruct((8, 128), F32)),
        in_specs=(hbm, hbm), out_specs=(sem, sem, hbm, hbm, pl.BlockSpec(memory_space=pltpu.VMEM)),
        input_output_aliases={0: 2, 1: 3},
        compiler_params=pltpu.CompilerParams(has_side_effects=pltpu.SideEffectType.DATAFLOW_SIDE_EFFECTING),
    )(pltpu.with_memory_space_constraint(blocks, pltpu.HBM),
      pltpu.with_memory_space_constraint(lax.empty(blocks.shape, blocks.dtype), pltpu.HBM))


def _scatter_wait(name, send_sems, recv_sems, src_thru, land_thru, after):
    def body(src, land, send_sems, recv_sems, *rest):
        me, peers = _flip_peers()
        for k, peer in enumerate(peers):
            _split_scatter_copy(src, land, send_sems, recv_sems, k, me, peer, True).wait_send()
            _split_scatter_copy(src, land, send_sems, recv_sems, k, me, peer, False).wait_recv()

    hbm = pl.BlockSpec(memory_space=pltpu.HBM)
    sem = pl.BlockSpec(memory_space=pltpu.SEMAPHORE)
    return pl.pallas_call(
        body, name=name,
        out_shape=(pltpu.HBM(src_thru.shape, src_thru.dtype), pltpu.HBM(land_thru.shape, land_thru.dtype)),
        in_specs=(hbm, hbm, sem, sem, *[pl.BlockSpec(memory_space=pl.ANY)] * len(after)), out_specs=(hbm, hbm),
        input_output_aliases={0: 0, 1: 1},
        compiler_params=pltpu.CompilerParams(has_side_effects=pltpu.SideEffectType.DATAFLOW_SIDE_EFFECTING),
    )(src_thru, land_thru, send_sems, recv_sems, *after)


def _exchange_alone(name, exch, after=()):
    def body():
        pass

    return _hosted_call(name, body, (1,), [], [], [], [], None, (), exch, after)


def _mm(name, grid, a, a_spec, b, b_spec, dims, outs, o_specs, acc_shape, extra=(), extra_specs=(), epi=None,
        exch=None, after=(), into=None, summed=None):
    nk = grid[-1]
    n_extra, n_out = len(extra), len(outs)

    def body(*refs):
        a_ref, b_ref = refs[0], refs[1]
        ex = refs[2:2 + n_extra]
        o_refs = refs[2 + n_extra:2 + n_extra + n_out]

        def write(res):
            res = (res,) if epi is None else epi(res, *[e[...] for e in ex])
            for idx, (o, r) in enumerate(zip(o_refs, res)):
                if summed is not None and idx == summed[0]:
                    first = pl.program_id(summed[1]) == 0

                    @pl.when(first)
                    def _(o=o, r=r):
                        o[...] = r.astype(o.dtype)

                    @pl.when(jnp.logical_not(first))
                    def _(o=o, r=r):
                        o[...] += r.astype(o.dtype)
                else:
                    o[...] = r.astype(o.dtype)

        if nk == 1:
            write(_bdot(a_ref[...], b_ref[...], dims))
            return
        acc = refs[-1]
        k = pl.program_id(len(grid) - 1)

        @pl.when(k == 0)
        def _():
            acc[...] = _bdot(a_ref[...], b_ref[...], dims)

        @pl.when(jnp.logical_and(k > 0, k < nk - 1))
        def _():
            acc[...] += _bdot(a_ref[...], b_ref[...], dims)

        @pl.when(k == nk - 1)
        def _():
            write(acc[...] + _bdot(a_ref[...], b_ref[...], dims))

    sem = ("parallel",) * (len(grid) - 1) + ("arbitrary",)
    if summed is not None:
        sem = ("arbitrary",) * len(grid)
    scratch = [] if nk == 1 else [pltpu.VMEM(acc_shape, F32)]
    return _hosted_call(name, body, grid, [a_spec, b_spec, *extra_specs], list(o_specs), list(outs),
                        scratch, sem, (a, b, *extra), exch, after, into)


def _add_epi(acc, add):
    return (acc + add,)


def _pack_epi(acc):
    return (_pack_pairs(acc),)


def _mm_nt(name, a, b, n_off, n, out_dtype=F32, tm=1024, tn=512, exch=None, after=()):
    m, kk = a.shape
    tm, tn = _t(m, tm), math.gcd(_t(n, tn), n_off)
    res = _mm(name, (m // tm, n // tn, 1),
              a, pl.BlockSpec((tm, kk), lambda i, j, k: (i, 0)),
              b, pl.BlockSpec((tn, kk), lambda i, j, k: (n_off // tn + j, 0)), NT,
              [jax.ShapeDtypeStruct((m, n), out_dtype)], [pl.BlockSpec((tm, tn), lambda i, j, k: (i, j))], (tm, tn),
              exch=exch, after=after)
    return res[0] if exch is None else res


def _mm_nt_halves(name, a, b2, n_off, n, tm=1024, tn=512, exch=None):
    m = a.shape[0]
    kh = b2.shape[2]
    tm, tn = _t(m, tm), math.gcd(_t(n, tn), n_off)

    def body(a_ref, b_ref, o_ref):
        o_ref[...] = _bdot(a_ref[:, :kh], b_ref[0], NT) + _bdot(a_ref[:, kh:], b_ref[1], NT)

    res = _hosted_call(name, body, (m // tm, n // tn),
                       [pl.BlockSpec((tm, 2 * kh), lambda i, j: (i, 0)),
                        pl.BlockSpec((2, tn, kh), lambda i, j: (0, n_off // tn + j, 0))],
                       [pl.BlockSpec((tm, tn), lambda i, j: (i, j))], [jax.ShapeDtypeStruct((m, n), F32)], [],
                       ("parallel", "parallel"), (a, b2), exch)
    return res[0] if exch is None else res


def _mm_nn_halves(name, a, b2, k_off=0, add=None, after=(), tm=1024, tk=1024):
    m, kk = a.shape
    half = b2.shape[2]
    tm, tk = _t(m, tm), math.gcd(_t(kk, tk), k_off)
    extra, especs, epi = (), (), None
    if add is not None:
        extra, especs, epi = (add,), (pl.BlockSpec((tm, half), lambda i, j, k: (i, j)),), _add_epi
    return _mm(name, (m // tm, 2, kk // tk),
               a, pl.BlockSpec((tm, tk), lambda i, j, k: (i, k)),
               b2, pl.BlockSpec((None, tk, half), lambda i, j, k: (j, k_off // tk + k, 0)), NN,
               [jax.ShapeDtypeStruct((m, 2 * half), F32)], [pl.BlockSpec((tm, half), lambda i, j, k: (i, j))],
               (tm, half), extra, especs, epi, after=after)[0]


def _mm_nn_blocked(name, a, b3, tm=1024, tn=512, exch=None):
    m, kk = a.shape
    nb, _, cs = b3.shape
    tm, tn = _t(m, tm), _t(cs, tn)
    per = cs // tn
    res = _mm(name, (m // tm, nb * per, 1),
              a, pl.BlockSpec((tm, kk), lambda i, j, k: (i, 0)),
              b3, pl.BlockSpec((None, kk, tn), lambda i, j, k: (j // per, 0, j % per)), NN,
              [jax.ShapeDtypeStruct((m, nb * cs), F32)], [pl.BlockSpec((tm, tn), lambda i, j, k: (i, j))],
              (tm, tn), exch=exch)
    return res[0] if exch is None else res


def _mm_nt_blocked(name, a, b3, after=(), tm=1024, tn=1024):
    m = a.shape[0]
    nb, n, cs = b3.shape
    tm, tn = _t(m, tm), _t(n, tn)
    return _mm(name, (m // tm, n // tn, nb),
               a, pl.BlockSpec((tm, cs), lambda i, j, k: (i, k)),
               b3, pl.BlockSpec((None, tn, cs), lambda i, j, k: (k, j, 0)), NT,
               [jax.ShapeDtypeStruct((m, n), F32)], [pl.BlockSpec((tm, tn), lambda i, j, k: (i, j))], (tm, tn),
               after=after)[0]


def _mm_nn(name, a, b, k_off=0, kk=None, add=None, out_dtype=F32, tm=1024, tn=1024, tk=1024, after=()):
    m = a.shape[0]
    kk = a.shape[1] if kk is None else kk
    n = b.shape[1]
    tm, tn, tk = _t(m, tm), _t(n, tn), math.gcd(_t(kk, tk), k_off)
    extra, especs, epi = (), (), None
    if add is not None:
        extra, especs, epi = (add,), (pl.BlockSpec((tm, tn), lambda i, j, k: (i, j)),), _add_epi
    return _mm(name, (m // tm, n // tn, kk // tk),
               a, pl.BlockSpec((tm, tk), lambda i, j, k: (i, k)),
               b, pl.BlockSpec((tk, tn), lambda i, j, k: (k_off // tk + k, j)), NN,
               [jax.ShapeDtypeStruct((m, n), out_dtype)], [pl.BlockSpec((tm, tn), lambda i, j, k: (i, j))], (tm, tn),
               extra, especs, epi, after=after)[0]


TN_ACC_ELEMENTS = 1 << 20


def _mm_tn_packed(name, a, b, pw, tk=1024, rows=None, into=None, b_blk=None, after=()):
    kk, m = a.shape
    nq = b.shape[1] // pw if b_blk is None else 1
    first = 0 if b_blk is None else b_blk
    tm, tk = _t(m, TN_ACC_ELEMENTS // pw), _t(kk, tk)
    rows = m if rows is None else rows
    row_blk = 0
    if into is not None:
        rows = into.shape[1]
        assert (rows - m) % tm == 0
        row_blk = (rows - m) // tm
    return _mm(name, (m // tm, nq, kk // tk),
               a, pl.BlockSpec((tk, tm), lambda i, j, k: (k, i)),
               b, pl.BlockSpec((tk, pw), lambda i, j, k: (k, first + j)), TN,
               [jax.ShapeDtypeStruct((nq, rows, pw // 2), F32)],
               [pl.BlockSpec((None, tm, pw // 2), lambda i, j, k: (j, row_blk + i, 0))], (tm, pw), epi=_pack_epi,
               into=into, after=after)[0]


def _rms_fwd(name, x, g, tm=256, exch=None):
    l, d = x.shape
    tm = _t(l, tm)

    def body(x_ref, g_ref, h_ref, r_ref):
        xv = x_ref[...]
        r = lax.rsqrt(jnp.mean(xv * xv, axis=-1, keepdims=True) + EPS)
        h_ref[...] = (xv * r * g_ref[...]).astype(BF)
        r_ref[...] = r

    return _hosted_call(
        name, body, (l // tm,),
        [pl.BlockSpec((tm, d), lambda i: (i, 0)), pl.BlockSpec((1, d), lambda i: (0, 0))],
        [pl.BlockSpec((tm, d), lambda i: (i, 0)), pl.BlockSpec((tm, 1), lambda i: (i, 0))],
        [jax.ShapeDtypeStruct((l, d), BF), jax.ShapeDtypeStruct((l, 1), F32)], [], ("parallel",), (x, g), exch)


def _rms_bwd(name, dh, x, r, g, dres, tm=256):
    l, d = x.shape
    tm = _t(l, tm)

    def body(dh_ref, x_ref, r_ref, g_ref, dres_ref, dx_ref, dxb_ref, dg_ref):
        i = pl.program_id(0)
        rr = r_ref[...]
        xhat = x_ref[...] * rr
        dhv = dh_ref[...]
        dxh = dhv * g_ref[...]
        dx = dres_ref[...] + rr * (dxh - xhat * jnp.mean(dxh * xhat, axis=-1, keepdims=True))
        dx_ref[...] = dx
        dxb_ref[...] = dx.astype(BF)

        @pl.when(i == 0)
        def _():
            dg_ref[...] = jnp.zeros_like(dg_ref)

        dg_ref[...] += jnp.sum(dhv * xhat, axis=0, keepdims=True)

    row = pl.BlockSpec((tm, d), lambda i: (i, 0))
    vec = pl.BlockSpec((1, d), lambda i: (0, 0))
    return pl.pallas_call(
        body, grid=(l // tm,), in_specs=[row, row, pl.BlockSpec((tm, 1), lambda i: (i, 0)), vec, row],
        out_specs=[row, row, vec],
        out_shape=[jax.ShapeDtypeStruct((l, d), F32), jax.ShapeDtypeStruct((l, d), BF),
                   jax.ShapeDtypeStruct((1, d), F32)],
        compiler_params=_params(("arbitrary",)), name=name)(dh, x, r, g, dres)


def _loss_head(x, g, target, tm=256):
    l, d = x.shape
    tm = _t(l, tm)

    def body(x_ref, g_ref, t_ref, dx_ref, dxb_ref, dg_ref, loss_ref):
        i = pl.program_id(0)
        xv = x_ref[...]
        gv = g_ref[...]
        r = lax.rsqrt(jnp.mean(xv * xv, axis=-1, keepdims=True) + EPS)
        xhat = xv * r
        e = xhat * gv - t_ref[...]
        dy = e * (1.0 / d)
        dxh = dy * gv
        dx = r * (dxh - xhat * jnp.mean(dxh * xhat, axis=-1, keepdims=True))
        dx_ref[...] = dx
        dxb_ref[...] = dx.astype(BF)

        @pl.when(i == 0)
        def _():
            dg_ref[...] = jnp.zeros_like(dg_ref)
            loss_ref[...] = jnp.zeros_like(loss_ref)

        dg_ref[...] += jnp.sum(dy * xhat, axis=0, keepdims=True)
        loss_ref[...] += 0.5 * jnp.sum(jnp.sum(e * e, axis=-1, keepdims=True) * (1.0 / d), axis=0, keepdims=True)

    row = pl.BlockSpec((tm, d), lambda i: (i, 0))
    vec = pl.BlockSpec((1, d), lambda i: (0, 0))
    return pl.pallas_call(
        body, grid=(l // tm,), in_specs=[row, vec, row],
        out_specs=[row, row, vec, pl.BlockSpec((1, 1), lambda i: (0, 0))],
        out_shape=[jax.ShapeDtypeStruct((l, d), F32), jax.ShapeDtypeStruct((l, d), BF),
                   jax.ShapeDtypeStruct((1, d), F32), jax.ShapeDtypeStruct((1, 1), F32)],
        compiler_params=_params(("arbitrary",)), name="loss_head")(x, g, target)


CONV_HALO = 8


def _conv_pre(x_ref, w_ref, b_ref, i, tr):
    r0 = pl.multiple_of(i * tr, tr)
    cur = x_ref[pl.ds(r0, tr), :]
    prev = x_ref[pl.ds(pl.multiple_of(jnp.maximum(r0 - CONV_HALO, 0), CONV_HALO), CONV_HALO), :]
    prev = jnp.where(i > 0, prev, 0.0)
    ext = jnp.concatenate([prev, cur], axis=0)
    taps = []
    for k in range(CONV_TAPS):
        s = CONV_TAPS - 1 - k
        taps.append(cur if s == 0 else pltpu.roll(ext, s, 0)[CONV_HALO:])
    pre = b_ref[...] + sum(w_ref[k:k + 1, :] * taps[k] for k in range(CONV_TAPS))
    return r0, pre, taps


def _conv_fwd(pm, col_off, conv_w, conv_b, cw=256, tr=512):
    l = pm.shape[0]
    c = conv_w.shape[1]
    cw, tr = _t(c, cw), _t(l, tr)
    assert col_off % cw == 0

    def body(x_ref, w_ref, b_ref, o_ref):
        def step(i, carry):
            r0, pre, _ = _conv_pre(x_ref, w_ref, b_ref, i, tr)
            o_ref[pl.ds(r0, tr), :] = _silu(pre)
            return carry
        lax.fori_loop(0, l // tr, step, 0)

    return pl.pallas_call(
        body, grid=(c // cw,),
        in_specs=[pl.BlockSpec((l, cw), lambda j: (0, col_off // cw + j)),
                  pl.BlockSpec((CONV_TAPS, cw), lambda j: (0, j)), pl.BlockSpec((1, cw), lambda j: (0, j))],
        out_specs=pl.BlockSpec((l, cw), lambda j: (0, j)), out_shape=jax.ShapeDtypeStruct((l, c), F32),
        compiler_params=_params(("parallel",)), name="conv_fwd")(pm, conv_w, conv_b)


def _conv_bwd(pm, col_off, conv_w, conv_b, dy, dproj, cw=256, tr=512):
    l = pm.shape[0]
    c = conv_w.shape[1]
    cw, tr = _t(c, cw), _t(l, tr)
    nt = l // tr

    def body(x_ref, w_ref, b_ref, dy_ref, _, dx_ref, dw_ref, db_ref, dpre_ref):
        def step1(i, carry):
            dws, db = carry
            r0, pre, taps = _conv_pre(x_ref, w_ref, b_ref, i, tr)
            dpre = dy_ref[pl.ds(r0, tr), :] * _dsilu(pre)
            dpre_ref[pl.ds(r0, tr), :] = dpre
            dws = tuple(dws[k] + jnp.sum(dpre * taps[k], axis=0, keepdims=True) for k in range(CONV_TAPS))
            return dws, db + jnp.sum(dpre, axis=0, keepdims=True)

        z = jnp.zeros((1, cw), F32)
        dws, db = lax.fori_loop(0, nt, step1, ((z,) * CONV_TAPS, z))
        for k in range(CONV_TAPS):
            dw_ref[k:k + 1, :] = dws[k]
        db_ref[...] = db

        def step2(i, carry):
            r0 = pl.multiple_of(i * tr, tr)
            cur = dpre_ref[pl.ds(r0, tr), :]
            nxt = dpre_ref[pl.ds(pl.multiple_of(jnp.minimum(r0 + tr, l - CONV_HALO), CONV_HALO), CONV_HALO), :]
            nxt = jnp.where(i < nt - 1, nxt, 0.0)
            ext = jnp.concatenate([cur, nxt], axis=0)
            acc = w_ref[CONV_TAPS - 1:CONV_TAPS, :] * cur
            for k in range(CONV_TAPS - 1):
                s = CONV_TAPS - 1 - k
                acc = acc + w_ref[k:k + 1, :] * pltpu.roll(ext, tr + CONV_HALO - s, 0)[:tr]
            dx_ref[pl.ds(r0, tr), :] = acc.astype(dx_ref.dtype)
            return carry
        lax.fori_loop(0, nt, step2, 0)

    col = pl.BlockSpec((l, cw), lambda j: (0, j))
    shifted = pl.BlockSpec((l, cw), lambda j: (0, col_off // cw + j))
    return pl.pallas_call(
        body, grid=(c // cw,),
        in_specs=[shifted, pl.BlockSpec((CONV_TAPS, cw), lambda j: (0, j)), pl.BlockSpec((1, cw), lambda j: (0, j)),
                  col, pl.BlockSpec(memory_space=pl.ANY)],
        out_specs=[shifted, pl.BlockSpec((CONV_TAPS, cw), lambda j: (0, j)), pl.BlockSpec((1, cw), lambda j: (0, j))],
        out_shape=[jax.ShapeDtypeStruct(dproj.shape, dproj.dtype), jax.ShapeDtypeStruct((CONV_TAPS, c), F32),
                   jax.ShapeDtypeStruct((1, c), F32)],
        scratch_shapes=[pltpu.VMEM((l, cw), F32)], input_output_aliases={4: 0},
        compiler_params=_params(("parallel",)), name="conv_bwd")(pm, conv_w, conv_b, dy, dproj)


def _split(x, pieces):
    out, rest = [], x
    for _ in range(pieces):
        piece = rest.astype(BF)
        out.append(piece)
        rest = rest - piece.astype(F32)
    return out


def _rows_times(xs, m_stack, pieces):
    x = xs[0] if len(xs) == 1 else jnp.concatenate(xs, axis=0)
    out = _bdot(jnp.concatenate(_split(x, pieces), axis=1), m_stack)
    sizes = [v.shape[0] for v in xs]
    offs = np.cumsum([0] + sizes)
    return [out[offs[i]:offs[i + 1]] for i in range(len(xs))]


def _times_rows(m_stack, x, pieces):
    return _bdot(m_stack, jnp.concatenate(_split(x, pieces), axis=0))


EXPAND_PIECES = 3
FOLD_PIECES = 2


def _ssd_consts(r, p, t):
    assert p == t, "heads expand to P lanes of the inputs and to T lanes of the decay matrices alike"
    assert LANES % p == 0 and (r * p) % LANES == 0, "whole heads per lane tile, whole lane tiles per group"
    rp = r * p
    tri = np.tril(np.ones((t, t), np.float32))
    ep = np.zeros((HEAD_LANES, rp), np.float32)
    ep[np.arange(rp) // p, np.arange(rp)] = 1.0
    itile = (np.arange(t)[:, None] == (np.arange(rp) % t)[None, :]).astype(np.float32)
    lmask = (np.arange(t)[:, None] >= (np.arange(rp) % t)[None, :]).astype(np.float32)
    return [jnp.asarray(np.concatenate([ep] * EXPAND_PIECES, axis=0), BF),
            jnp.asarray(np.concatenate([ep.T] * FOLD_PIECES, axis=0), BF),
            jnp.asarray(np.concatenate([tri] * EXPAND_PIECES, axis=1), BF),
            jnp.asarray(np.concatenate([tri.T] * EXPAND_PIECES, axis=1), BF),
            jnp.asarray(itile), jnp.asarray(lmask)]


def _ssd_heads(dtr_ref, dtb_ref, alog_ref, e_stack, tri_stack, g_n, t):
    pre = [dtr_ref[g] + dtb_ref[g] for g in range(g_n)]
    dt = [jnp.maximum(v, 0.0) + jnp.log(1.0 + jnp.exp(-jnp.abs(v))) for v in pre]
    a = [-jnp.exp(alog_ref[g]) for g in range(g_n)]
    adt = jnp.concatenate([dt[g] * a[g] for g in range(g_n)], axis=1)
    cs = _times_rows(tri_stack, adt, EXPAND_PIECES)
    cs = [cs[:, g * HEAD_LANES:(g + 1) * HEAD_LANES] for g in range(g_n)]
    at_lanes = _rows_times([v for g in range(g_n) for v in (dt[g], cs[g])], e_stack, EXPAND_PIECES)
    return [(pre[g], dt[g], a[g], cs[g][t - 1:t, :], at_lanes[2 * g], at_lanes[2 * g + 1]) for g in range(g_n)]


LANES = 128


def _head_lanes(k, p, rows):
    lane = lax.broadcasted_iota(jnp.int32, (rows, LANES), 1)
    j = k % (LANES // p)
    return jnp.logical_and(lane >= j * p, lane < (j + 1) * p)


def _block_diagonal(v, r, p):
    rows = []
    for k in range(r):
        c = k * p // LANES
        tiles = [jnp.zeros((v.shape[0], LANES), v.dtype)] * (r * p // LANES)
        tiles[c] = jnp.where(_head_lanes(k, p, v.shape[0]), v[:, c * LANES:(c + 1) * LANES], jnp.zeros((), v.dtype))
        rows.append(jnp.concatenate(tiles, axis=1))
    return jnp.concatenate(rows, axis=0)


def _diagonal_blocks(m, r, p):
    t = m.shape[0] // r
    tiles = []
    for c in range(r * p // LANES):
        heads = range(c * LANES // p, (c + 1) * LANES // p)
        acc = None
        for k in heads:
            blk = m[k * t:(k + 1) * t, c * LANES:(c + 1) * LANES]
            acc = blk if acc is None else jnp.where(_head_lanes(k, p, t), blk, acc)
        tiles.append(acc)
    return jnp.concatenate(tiles, axis=1)


def _ssd_common(xs, bm, dtx, csx, itile, lmask, r, t):
    ecsx = jnp.exp(csx)
    lastx = csx[t - 1:t, :]
    wx = jnp.exp(lastx - csx)
    elastx = jnp.exp(lastx)
    csrow = jnp.sum(csx * itile, axis=0, keepdims=True)
    lx = jnp.exp(jnp.where(lmask > 0.0, csx - csrow, NEG_BIG))
    xdt = xs * dtx
    xblk = _block_diagonal(xdt.astype(BF), r, t)
    btile = jnp.concatenate([bm.astype(BF)] * r, axis=0)
    return ecsx, wx, elastx, lx, xdt, xblk, btile


def _ssd_specs(l, g_n, r, p, n, t, conv, rev):
    nc = l // t
    rp = r * p
    cidx = (lambda c: nc - 1 - c) if rev else (lambda c: c)
    row_spec = lambda width: pl.BlockSpec((t, width), lambda c: (cidx(c), 0))
    dtr_spec = pl.BlockSpec((g_n, t, HEAD_LANES), lambda c: (0, cidx(c), 0))
    par_spec = pl.BlockSpec((g_n, 1, HEAD_LANES), lambda c: (0, 0, 0))
    dskx_spec = pl.BlockSpec((g_n, 1, rp), lambda c: (0, 0, 0))
    st_spec = pl.BlockSpec((None, g_n, n, rp), lambda c: (cidx(c), 0, 0, 0))
    return nc, row_spec, dtr_spec, par_spec, dskx_spec, st_spec


def _const_specs(consts):
    return [pl.BlockSpec(a.shape, lambda c: (0, 0)) for a in consts]


def _ssd_fwd(xbc, dtr, dtb, alog, dskx, pm, norm_g, dims, exch=None):
    h, g_n, r, p, n, t = dims
    l, conv = xbc.shape
    rp, hp = r * p, h * p
    nc, row_spec, dtr_spec, par_spec, dskx_spec, st_spec = _ssd_specs(l, g_n, r, p, n, t, conv, False)
    consts = _ssd_consts(r, p, t)

    def body(x_ref, dtr_ref, dtb_ref, alog_ref, dskx_ref, z_ref, ng_ref,
             e_ref, et_ref, tri_ref, trit_ref, it_ref, lm_ref,
             y_ref, st_ref, yn_ref, rn_ref, s_ref):
        @pl.when(pl.program_id(0) == 0)
        def _():
            s_ref[...] = jnp.zeros_like(s_ref)

        heads = _ssd_heads(dtr_ref, dtb_ref, alog_ref, e_ref[...], tri_ref[...], g_n, t)
        for g in range(g_n):
            xs = x_ref[:, g * rp:(g + 1) * rp]
            bm = x_ref[:, hp + g * n:hp + (g + 1) * n]
            cm = x_ref[:, hp + (g_n + g) * n:hp + (g_n + g + 1) * n]
            dtx, csx = heads[g][4:]
            ecsx, wx, elastx, lx, xdt, xblk, btile = _ssd_common(xs, bm, dtx, csx, it_ref[...], lm_ref[...], r, t)
            s_in = s_ref[g]
            st_ref[g] = s_in
            cbx = _bdot(cm, btile, NT)
            yd = _bdot(cbx * lx, xblk)
            yo = ecsx * _bdot(cm, s_in)
            y_ref[:, g * rp:(g + 1) * rp] = yd + yo + dskx_ref[g] * xs
            s_ref[g] = elastx * s_in + _bdot(bm, xdt * wx, TN)

        v = y_ref[...] * _silu(z_ref[...])
        rr = lax.rsqrt(jnp.mean(v * v, axis=-1, keepdims=True) + EPS)
        yn_ref[...] = (v * rr * ng_ref[...]).astype(BF)
        rn_ref[...] = rr

    return _hosted_call(
        "ssd_fwd", body, (nc,),
        [row_spec(conv), dtr_spec, par_spec, par_spec, dskx_spec, row_spec(hp),
         pl.BlockSpec((1, hp), lambda c: (0, 0)), *_const_specs(consts)],
        [row_spec(hp), st_spec, row_spec(hp), row_spec(1)],
        [jax.ShapeDtypeStruct((l, hp), F32), jax.ShapeDtypeStruct((nc, g_n, n, rp), F32),
         jax.ShapeDtypeStruct((l, hp), BF), jax.ShapeDtypeStruct((l, 1), F32)],
        [pltpu.VMEM((g_n, n, rp), F32)], ("arbitrary",),
        (xbc, dtr, dtb, alog, dskx, pm, norm_g, *consts), exch)


def _ssd_bwd(xbc, dtr, dtb, alog, dskx, states, dyn, y, pm, rstd, norm_g, dims, after=()):
    h, g_n, r, p, n, t = dims
    l, conv = xbc.shape
    rp, hp = r * p, h * p
    nc, row_spec, dtr_spec, par_spec, dskx_spec, st_spec = _ssd_specs(l, g_n, r, p, n, t, conv, True)
    consts = _ssd_consts(r, p, t)

    def fold_rows(v, rows):
        return sum(v[k * rows:(k + 1) * rows, :] for k in range(r))

    def body(x_ref, dtr_ref, dtb_ref, alog_ref, dskx_ref, st_ref, dyn_ref, y_ref, z_ref, rn_ref, ng_ref,
             e_ref, et_ref, tri_ref, trit_ref, it_ref, lm_ref,
             dx_ref, ddt_ref, dbias_ref, dalog_ref, dd_ref, dz_ref, dng_ref, ds_ref, dy_ref):
        @pl.when(pl.program_id(0) == 0)
        def _():
            ds_ref[...] = jnp.zeros_like(ds_ref)
            dbias_ref[...] = jnp.zeros_like(dbias_ref)
            dalog_ref[...] = jnp.zeros_like(dalog_ref)
            dd_ref[...] = jnp.zeros_like(dd_ref)
            dng_ref[...] = jnp.zeros_like(dng_ref)

        yv, zv, rr, dn = y_ref[...], z_ref[...], rn_ref[...], dyn_ref[...]
        sz = _silu(zv)
        vhat = yv * sz * rr
        dvh = dn * ng_ref[...]
        dv = rr * (dvh - vhat * jnp.mean(dvh * vhat, axis=-1, keepdims=True))
        dy_ref[...] = dv * sz
        dz_ref[...] = (dv * yv * _dsilu(zv)).astype(BF)
        dng_ref[...] += jnp.sum(dn * vhat, axis=0, keepdims=True)

        itile = it_ref[...]
        last_row = lax.broadcasted_iota(jnp.int32, (t, HEAD_LANES), 0) == t - 1
        heads = _ssd_heads(dtr_ref, dtb_ref, alog_ref, e_ref[...], tri_ref[...], g_n, t)
        to_fold = []
        for g in range(g_n):
            xs = x_ref[:, g * rp:(g + 1) * rp]
            bm = x_ref[:, hp + g * n:hp + (g + 1) * n]
            cm = x_ref[:, hp + (g_n + g) * n:hp + (g_n + g + 1) * n]
            dy = dy_ref[:, g * rp:(g + 1) * rp]
            dtx, csx = heads[g][4:]
            ecsx, wx, elastx, lx, xdt, xblk, btile = _ssd_common(xs, bm, dtx, csx, itile, lm_ref[...], r, t)
            s_in = st_ref[g]
            ds_out = ds_ref[g]

            cbx = _bdot(cm, btile, NT)
            amat = cbx * lx
            da = _bdot(dy, xblk, NT)
            dxdt = _diagonal_blocks(_bdot(amat, dy, TN), r, t)
            dcbx = da * lx
            q = da * amat
            dc = _bdot(dcbx, btile)
            db = fold_rows(_bdot(dcbx, cm, TN), t)

            g0 = _bdot(cm, s_in)
            dg0 = dy * ecsx
            dc = dc + _bdot(dg0, s_in, NT)

            z = _bdot(bm, ds_out)
            dxdt = dxdt + z * wx
            db = db + _bdot(xdt * wx, ds_out, NT)
            ds_ref[g] = elastx * ds_out + _bdot(cm, dg0, TN)

            dx_ref[:, g * rp:(g + 1) * rp] = dskx_ref[g] * dy + dxdt * dtx
            dx_ref[:, hp + g * n:hp + (g + 1) * n] = db
            dx_ref[:, hp + (g_n + g) * n:hp + (g_n + g + 1) * n] = dc

            dwx = z * xdt * wx
            dlastx = jnp.sum(dwx, axis=0, keepdims=True) + jnp.sum(ds_out * s_in, axis=0, keepdims=True) * elastx
            rows = jnp.concatenate([jnp.sum(dy * xs, axis=0, keepdims=True), dlastx, jnp.zeros((14, rp), F32)], axis=0)
            to_fold += [q - itile * jnp.sum(q, axis=0, keepdims=True) + dy * g0 * ecsx - dwx, dxdt * xs, rows]

        folded = _rows_times(to_fold, et_ref[...], FOLD_PIECES)
        dcs = []
        for g in range(g_n):
            f_cs, f_rows = folded[3 * g], folded[3 * g + 2]
            dd_ref[g] += f_rows[0:1]
            dcs.append(f_cs + jnp.where(last_row, f_rows[1:2], 0.0))
        dadt_all = _times_rows(trit_ref[...], jnp.concatenate(dcs, axis=1), EXPAND_PIECES)
        for g in range(g_n):
            pre, dt, a = heads[g][:3]
            dadt = dadt_all[:, g * HEAD_LANES:(g + 1) * HEAD_LANES]
            dalog_ref[g] += jnp.sum(dadt * dt, axis=0, keepdims=True) * a
            dpre = (dadt * a + folded[3 * g + 1]) * (1.0 / (1.0 + jnp.exp(-pre)))
            ddt_ref[g] = dpre
            dbias_ref[g] += jnp.sum(dpre, axis=0, keepdims=True)

    par_shape = jax.ShapeDtypeStruct((g_n, 1, HEAD_LANES), F32)
    vec = pl.BlockSpec((1, hp), lambda c: (0, 0))
    return _hosted_call(
        "ssd_bwd", body, (nc,),
        [row_spec(conv), dtr_spec, par_spec, par_spec, dskx_spec, st_spec, row_spec(hp), row_spec(hp), row_spec(hp),
         row_spec(1), vec, *_const_specs(consts)],
        [row_spec(conv), dtr_spec, par_spec, par_spec, par_spec, row_spec(hp), vec],
        [jax.ShapeDtypeStruct((l, conv), F32), jax.ShapeDtypeStruct((g_n, l, HEAD_LANES), F32),
         par_shape, par_shape, par_shape, jax.ShapeDtypeStruct((l, pm.shape[1]), BF),
         jax.ShapeDtypeStruct((1, hp), F32)],
        [pltpu.VMEM((g_n, n, rp), F32), pltpu.VMEM((t, hp), F32)], ("arbitrary",),
        (xbc, dtr, dtb, alog, dskx, states, dyn, y, pm, rstd, norm_g, *consts), after=after)


POOL_HALO = 16


def _pool_mix(name, src, dp, backward, out_dtype, into=None, cw=256, tr=512):
    l = src.shape[0]
    gd = dp // len(POOL_WINDOWS)
    cwl, trl = _t(gd, cw), _t(l, tr)
    nt = l // trl

    def body(x_ref, *rest):
        o_ref = rest[-1]
        gi = pl.program_id(0)
        for wi, win in enumerate(POOL_WINDOWS):
            @pl.when(gi == wi)
            def _(win=win):
                def step(i, carry):
                    r0 = pl.multiple_of(i * trl, trl)
                    cur = x_ref[pl.ds(r0, trl), :]
                    trow = r0 + lax.broadcasted_iota(jnp.int32, (trl, 1), 0)
                    cnt = jnp.minimum(trow + 1, win).astype(F32)
                    if not backward:
                        halo = x_ref[pl.ds(pl.multiple_of(jnp.maximum(r0 - POOL_HALO, 0), POOL_HALO), POOL_HALO), :]
                        halo = jnp.where(i > 0, halo, 0.0)
                        s = jnp.concatenate([halo, cur], axis=0)
                        sh = 1
                        while sh < win:
                            s = s + pltpu.roll(s, sh, 0)
                            sh *= 2
                        res = s[POOL_HALO:] / cnt - cur
                    else:
                        halo = x_ref[pl.ds(pl.multiple_of(jnp.minimum(r0 + trl, l - POOL_HALO), POOL_HALO),
                                           POOL_HALO), :]
                        hrow = r0 + trl + lax.broadcasted_iota(jnp.int32, (POOL_HALO, 1), 0)
                        hcnt = jnp.minimum(hrow + 1, win).astype(F32)
                        halo = jnp.where(i < nt - 1, halo / hcnt, 0.0)
                        s = jnp.concatenate([cur / cnt, halo], axis=0)
                        sh = 1
                        while sh < win:
                            s = s + pltpu.roll(s, trl + POOL_HALO - sh, 0)
                            sh *= 2
                        res = s[:trl] - cur
                    o_ref[pl.ds(r0, trl), :] = res.astype(o_ref.dtype)
                    return carry
                lax.fori_loop(0, nt, step, 0)

    col = pl.BlockSpec((l, cwl), lambda g, j: (0, g * (gd // cwl) + j))
    if into is None:
        return pl.pallas_call(
            body, grid=(len(POOL_WINDOWS), gd // cwl), in_specs=[col], out_specs=col,
            out_shape=jax.ShapeDtypeStruct((l, dp), out_dtype),
            compiler_params=_params(("parallel", "parallel")), name=name)(src)
    return pl.pallas_call(
        body, grid=(len(POOL_WINDOWS), gd // cwl), in_specs=[col, pl.BlockSpec(memory_space=pl.ANY)], out_specs=col,
        out_shape=jax.ShapeDtypeStruct(into.shape, into.dtype), input_output_aliases={1: 0},
        compiler_params=_params(("parallel", "parallel")), name=name)(src, into)


def _pool_out_bwd_data(dx, wp_out, mg, pu, scale, tm=1024, tn=512):
    l, d = dx.shape
    dp = wp_out.shape[0]
    tm, tn = _t(l, tm), _t(dp, tn)

    def epi(dyp, m, gt, sc):
        sg = _silu(gt)
        return dyp * sc * sg, dyp * m * sc * _dsilu(gt), jnp.sum(dyp * m * sg, axis=0, keepdims=True)

    tile = pl.BlockSpec((tm, tn), lambda j, i, k: (i, j))
    right = pl.BlockSpec((tm, tn), lambda j, i, k: (i, dp // tn + j))
    vec = pl.BlockSpec((1, tn), lambda j, i, k: (0, j))
    return _mm("pool_out_bwd_data", (dp // tn, l // tm, 1),
               dx, pl.BlockSpec((tm, d), lambda j, i, k: (i, 0)),
               wp_out, pl.BlockSpec((tn, d), lambda j, i, k: (j, 0)), NT,
               [jax.ShapeDtypeStruct((l, dp), BF), jax.ShapeDtypeStruct((l, 2 * dp), BF),
                jax.ShapeDtypeStruct((1, dp), F32)], [tile, right, vec], (tm, tn),
               (mg, pu, scale), (tile, right, vec), epi, summed=(2, 1))


def _group_mm_fwd(mp, wg, pu, scale, tm=1024, tn=1024, tk=1024):
    l, dp = mp.shape
    ng, gd = wg.shape[0], wg.shape[1]
    tm, tn, tk = _t(l, tm), _t(gd, tn), _t(gd, tk)

    def epi(acc, gate, sc):
        return acc, acc * sc * _silu(gate)

    out = pl.BlockSpec((tm, tn), lambda g, i, j, k: (i, g * (gd // tn) + j))
    return _mm("group_mm_fwd", (ng, l // tm, gd // tn, gd // tk),
               mp, pl.BlockSpec((tm, tk), lambda g, i, j, k: (i, g * (gd // tk) + k)),
               wg, pl.BlockSpec((None, tk, tn), lambda g, i, j, k: (g, k, j)), NN,
               [jax.ShapeDtypeStruct((l, dp), F32), jax.ShapeDtypeStruct((l, dp), BF)], [out, out], (tm, tn),
               (pu, scale),
               (pl.BlockSpec((tm, tn), lambda g, i, j, k: (i, (dp + g * gd) // tn + j)),
                pl.BlockSpec((1, tn), lambda g, i, j, k: (0, g * (gd // tn) + j))), epi)


def _group_mm_bwd_data(dmg, wg, after=(), tm=1024, tn=1024, tk=1024):
    l, dp = dmg.shape
    ng, gd = wg.shape[0], wg.shape[1]
    tm, tn, tk = _t(l, tm), _t(gd, tn), _t(gd, tk)
    return _mm("group_mm_bwd_data", (ng, l // tm, gd // tn, gd // tk),
               dmg, pl.BlockSpec((tm, tk), lambda g, i, j, k: (i, g * (gd // tk) + k)),
               wg, pl.BlockSpec((None, tn, tk), lambda g, i, j, k: (g, j, k)), NT,
               [jax.ShapeDtypeStruct((l, dp), F32)],
               [pl.BlockSpec((tm, tn), lambda g, i, j, k: (i, g * (gd // tn) + j))], (tm, tn), after=after)[0]


def _group_mm_bwd_weight(mp, dmg, ng, tm=512, tk=1024):
    l, dp = mp.shape
    gd = dp // ng
    tm, tk = _t(gd, tm), _t(l, tk)
    return _mm("group_mm_bwd_weight", (ng, gd // tm, l // tk),
               mp, pl.BlockSpec((tk, tm), lambda g, i, k: (k, g * (gd // tm) + i)),
               dmg, pl.BlockSpec((tk, gd), lambda g, i, k: (k, g)), TN,
               [jax.ShapeDtypeStruct((ng, gd, gd // 2), F32)],
               [pl.BlockSpec((None, tm, gd // 2), lambda g, i, k: (g, i, 0))], (tm, gd), epi=_pack_epi)[0]


def _cast_bf16(name, w, tr=256):
    r, c = w.shape
    tr = _row_tile(r, tr, 16)

    def body(w_ref, o_ref):
        o_ref[...] = w_ref[...].astype(BF)

    blk = pl.BlockSpec((tr, c), lambda i: (i, 0))
    return pl.pallas_call(body, grid=(r // tr,), in_specs=[blk], out_specs=blk,
                          out_shape=jax.ShapeDtypeStruct((r, c), BF), compiler_params=_params(("parallel",)),
                          name=name)(w)


def _pack_rows(name, w, tr=256):
    r, c = w.shape
    tr = _row_tile(r, tr)

    def body(w_ref, o_ref):
        o_ref[...] = _pack_pairs(w_ref[...])

    return pl.pallas_call(body, grid=(r // tr, 2), in_specs=[pl.BlockSpec((tr, c // 2), lambda i, q: (i, q))],
                          out_specs=pl.BlockSpec((None, tr, c // 4), lambda i, q: (q, i, 0)),
                          out_shape=jax.ShapeDtypeStruct((2, r, c // 4), F32),
                          compiler_params=_params(("parallel", "parallel")), name=name)(w)


def _unpack_rows(name, w0, w1, tr=512):
    r, h = w0.shape
    tr = _row_tile(r, tr, 16)

    def body(w0_ref, w1_ref, o_ref):
        for q, w_ref in enumerate((w0_ref, w1_ref)):
            hi, lo = _unpack_pairs(w_ref[...])
            o_ref[q, :, :h] = hi.astype(BF)
            o_ref[q, :, h:] = lo.astype(BF)

    words = pl.BlockSpec((tr, h), lambda i: (i, 0))
    return pl.pallas_call(body, grid=(r // tr,), in_specs=[words, words],
                          out_specs=pl.BlockSpec((2, tr, 2 * h), lambda i: (0, i, 0)),
                          out_shape=jax.ShapeDtypeStruct((2, r, 2 * h), BF), compiler_params=_params(("parallel",)),
                          name=name)(w0, w1)


def _reduce_packed(name, recv, tr=256):
    nd, r, h = recv.shape
    tr = _row_tile(r, tr)

    def body(p_ref, o_ref):
        hi, lo = _unpack_pairs(p_ref[0])
        for k in range(1, nd):
            a, b = _unpack_pairs(p_ref[k])
            hi, lo = hi + a, lo + b
        o_ref[:, :h] = hi
        o_ref[:, h:] = lo

    return pl.pallas_call(body, grid=(r // tr,), in_specs=[pl.BlockSpec((nd, tr, h), lambda i: (0, i, 0))],
                          out_specs=pl.BlockSpec((tr, 2 * h), lambda i: (i, 0)),
                          out_shape=jax.ShapeDtypeStruct((r, 2 * h), F32), compiler_params=_params(("parallel",)),
                          name=name)(recv)


def _adamw_math(w, g, m, v):
    m2 = ADAM_B1 * m + (1.0 - ADAM_B1) * g
    v2 = ADAM_B2 * v + (1.0 - ADAM_B2) * (g * g)
    m_hat = m2 / (1.0 - ADAM_B1 ** ADAM_STEP)
    v_hat = v2 / (1.0 - ADAM_B2 ** ADAM_STEP)
    delta = -ADAM_LR * (m_hat / (jnp.sqrt(v_hat) + ADAM_EPS) + ADAM_WD * w)
    return delta, m2, v2


def _adamw(name, w, g, m, v, tr=256):
    r, c = w.shape
    tr = _row_tile(r, tr)

    def body(w_ref, g_ref, m_ref, v_ref, d_ref, m2_ref, v2_ref):
        d, m2, v2 = _adamw_math(w_ref[...], g_ref[...], m_ref[...], v_ref[...])
        d_ref[...] = d
        m2_ref[...] = m2
        v2_ref[...] = v2

    blk = pl.BlockSpec((tr, c), lambda i: (i, 0))
    shp = jax.ShapeDtypeStruct((r, c), F32)
    return pl.pallas_call(body, grid=(r // tr,), in_specs=[blk] * 4, out_specs=[blk] * 3, out_shape=[shp] * 3,
                          compiler_params=_params(("parallel",)), name=name)(w, g, m, v)


def _sum_slots(name, a):
    nd, r, c = a.shape

    def body(a_ref, o_ref):
        s = a_ref[0]
        for k in range(1, nd):
            s = s + a_ref[k]
        o_ref[...] = s

    return pl.pallas_call(body, out_shape=jax.ShapeDtypeStruct((r, c), F32), name=name)(a)


def _head_rows(v, g_n, r):
    return jnp.pad(v.reshape(g_n, 1, r), ((0, 0), (0, 0), (0, HEAD_LANES - r)))


class _Later:
    def __init__(self, hosted, first_of, weights_of, pool_w_out_of, send):
        self.hosted, self.first_of, self.weights_of = hosted, first_of, weights_of
        self.pool_w_out_of, self.send = pool_w_out_of, send


def _listed(res):
    return res if isinstance(res, (list, tuple)) else [res]


def _local_step(x, target, ln_g, final_g, conv_b, dt_bias, a_log, d_skip, norm_g, later):
    l, d = x.shape
    di = norm_g.shape[1]
    h = dt_bias.shape[1]
    p = di // h
    conv = conv_b.shape[1]
    n, t = SSM_STATE, SSD_CHUNK
    g_n = (conv - di) // (2 * n)
    r = h // g_n
    nm = di + conv
    ng = len(POOL_WINDOWS)
    dims = (h, g_n, r, p, n, t)
    pw = d

    hosted = later.hosted
    h0, r0, *arrived_0 = _rms_fwd("rms0_fwd", x, ln_g[0:1], exch=hosted.get("rms0"))
    wt_in, conv_w, scale = later.first_of(arrived_0)
    dp = scale.shape[1]
    pm, *arrived_i = _listed(_mm_nt_halves("in_proj_main", h0, wt_in, 0, nm, exch=hosted.get("in_proj")))
    dtr = _mm_nt_halves("in_proj_dt", h0, wt_in, nm, h, tn=h)
    xbc = _conv_fwd(pm, di, conv_w, conv_b)
    dtb, alog = _head_rows(dt_bias, g_n, r), _head_rows(a_log, g_n, r)
    dskx = jnp.repeat(d_skip.reshape(g_n, 1, r), p, axis=2)
    dtr_g = jnp.pad(jnp.transpose(dtr.reshape(l, g_n, r), (1, 0, 2)), ((0, 0), (0, 0), (0, HEAD_LANES - r)))
    y, states, yn, r_n, *arrived_s = _ssd_fwd(xbc, dtr_g, dtb, alog, dskx, pm, norm_g, dims, exch=hosted.get("ssd"))
    w_out, wg, wp_in = later.weights_of(arrived_i, arrived_s)
    x1 = _mm_nn("ssm_out_proj", yn, w_out, add=x)

    h1, r1 = _rms_fwd("rms1_fwd", x1, ln_g[1:2])
    pu, *arrived_p = _listed(_mm_nn_blocked("pool_in_proj", h1, wp_in, exch=hosted.get("pool_in_proj")))
    wp_out = later.pool_w_out_of(arrived_p)
    mp = _pool_mix("pool_mix_fwd", pu, dp, False, BF)
    mg, yp = _group_mm_fwd(mp, wg, pu, scale)
    x2 = _mm_nn("pool_out_proj", yp, wp_out, add=x1)

    dx2, dx2_b, d_final_g, loss = _loss_head(x2, final_g, target)

    gw_pout = _mm_tn_packed("pool_out_bwd_weight", yp, dx2_b, pw)
    going = later.send("pool_w_out", gw_pout)
    dmg, dpu, d_scale = _pool_out_bwd_data(dx2_b, wp_out, mg, pu, scale)
    dmp = _group_mm_bwd_data(dmg, wg, after=going)
    gw_g = _group_mm_bwd_weight(mp, dmg, ng)
    going = later.send("pool_w_group", gw_g)
    dpu = _pool_mix("pool_mix_bwd", dmp, dp, True, BF, into=dpu)
    dh1 = _mm_nt_blocked("pool_in_bwd_data", dpu, wp_in, after=going)
    gw_pin = _mm_tn_packed("pool_in_bwd_weight", h1, dpu, 2 * dp // N_DEV)
    going = later.send("pool_w_in", gw_pin)
    dx1, dx1_b, d_ln1 = _rms_bwd("rms1_bwd", dh1, x1, r1, ln_g[1:2], dx2)

    dyn = _mm_nt("ssm_out_bwd_data", dx1_b, w_out, 0, di, tn=1024, after=going)
    gw_out = _mm_tn_packed("ssm_out_bwd_weight", yn, dx1_b, pw)
    going = later.send("ssm_w_out", gw_out)
    dxbc, ddt_g, dbias_g, dalog_g, dd_g, dproj, d_norm_g = _ssd_bwd(
        xbc, dtr_g, dtb, alog, dskx, states, dyn, y, pm, r_n, norm_g, dims, after=going)
    dproj, d_conv_w, d_conv_b = _conv_bwd(pm, di, conv_w, conv_b, dxbc, dproj)
    ddt = jnp.transpose(ddt_g[:, :, :r], (1, 0, 2)).reshape(l, h)
    gwt_in, going = [], ()
    for q in range(2):
        gq = _mm_tn_packed(f"in_proj_bwd_weight_{q}", dproj, h0, d // 2, rows=nm + h, b_blk=q, after=going)
        gq = _mm_tn_packed(f"in_proj_bwd_weight_dt_{q}", ddt, h0, d // 2, into=gq, b_blk=q)
        going = later.send(f"ssm_w_in_{q}", gq)
        gwt_in.append(gq)
    gwt_in = jnp.concatenate(gwt_in, axis=0)
    dh0 = _mm_nn_halves("in_proj_bwd_data_dt", ddt, wt_in, nm, after=going)
    dh0 = _mm_nn_halves("in_proj_bwd_data", dproj, wt_in, 0, add=dh0)
    grad_x, _, d_ln0 = _rms_bwd("rms0_bwd", dh0, x, r0, ln_g[0:1], dx1)

    def heads(v):
        return v[:, 0, :r].reshape(1, h)

    small = dict(ln_g=jnp.concatenate([d_ln0, d_ln1], axis=0), final_g=d_final_g, conv_w=d_conv_w, conv_b=d_conv_b,
                 dt_bias=heads(dbias_g), a_log=heads(dalog_g), d_skip=heads(dd_g), norm_g=d_norm_g, scale=d_scale)
    big = dict(ssm_w_in=gwt_in, ssm_w_out=gw_out, pool_w_in=gw_pin, pool_w_group=gw_g, pool_w_out=gw_pout)
    return loss, grad_x, small, big


SMALL_ORDER = ("ln_g", "final_g", "conv_w", "conv_b", "dt_bias", "a_log", "d_skip", "norm_g", "scale", "loss")


def _flatten_small(parts):
    flat = jnp.concatenate([parts[k].reshape(-1) for k in SMALL_ORDER])
    n = flat.shape[0]
    rows = -(-n // 1024) * 8
    return jnp.pad(flat, (0, rows * 128 - n)).reshape(rows, 128)


def _split_small(flat, shapes):
    flat = flat.reshape(-1)
    out, off = {}, 0
    for k in SMALL_ORDER:
        size = int(np.prod(shapes[k]))
        out[k] = flat[off:off + size].reshape(shapes[k])
        off += size
    return out


def kernel(x, ln_g, final_g, ssm_w_in, ssm_conv_w, ssm_conv_b, ssm_dt_bias, ssm_a_log, ssm_d, ssm_norm_g, ssm_w_out, pool_w_in, pool_w_group, pool_scale, pool_w_out, loss_target, m_ln_g, m_final_g, m_ssm_w_in, m_ssm_conv_w, m_ssm_conv_b, m_ssm_dt_bias, m_ssm_a_log, m_ssm_d, m_ssm_norm_g, m_ssm_w_out, m_pool_w_in, m_pool_w_group, m_pool_scale, m_pool_w_out, v_ln_g, v_final_g, v_ssm_w_in, v_ssm_conv_w, v_ssm_conv_b, v_ssm_dt_bias, v_ssm_a_log, v_ssm_d, v_ssm_norm_g, v_ssm_w_out, v_pool_w_in, v_pool_w_group, v_pool_scale, v_pool_w_out):
    l, d = x.shape[1], x.shape[2]
    me = 4 * lax.axis_index("x") + 2 * lax.axis_index("y") + lax.axis_index("c")
    ng, gds, gd = pool_w_group.shape[1], pool_w_group.shape[2], pool_w_group.shape[3]
    sin_s = ssm_w_in.shape[2]
    dp_s = pool_w_out.shape[1]
    conv_s = ssm_conv_w.shape[2]

    wt_in_s = _pack_rows("pack_w_in", jnp.transpose(ssm_w_in[0]))
    w_out_s = _cast_bf16("cast_w_out", ssm_w_out[0])
    wp_in_s = _cast_bf16("cast_pool_w_in", pool_w_in[0])
    wg_s = _cast_bf16("cast_pool_w_group", pool_w_group[0].reshape(ng * gds, gd))
    wp_out_s = _cast_bf16("cast_pool_w_out", pool_w_out[0])
    small_s = jnp.concatenate([ssm_conv_w[0].reshape(-1), pool_scale[0]])
    n_small = small_s.shape[0]
    small_s = jnp.pad(small_s, (0, -(-n_small // 1024) * 1024 - n_small)).reshape(-1, 128)
    hosted = dict(rms0=_Gather([wt_in_s[0], wt_in_s[1], small_s]), in_proj=_Gather([wp_in_s]),
                  ssd=_Gather([w_out_s, wg_s]), pool_in_proj=_Gather([wp_out_s]))

    def first_of(arrived_0):
        wt_in_g0, wt_in_g1, small_g = arrived_0
        wt_in = _unpack_rows("unpack_w_in", wt_in_g0.reshape(N_DEV * sin_s, d // 4),
                             wt_in_g1.reshape(N_DEV * sin_s, d // 4))
        small_all = small_g.reshape(N_DEV, -1)[:, :n_small]
        conv_w = jnp.transpose(small_all[:, :CONV_TAPS * conv_s].reshape(N_DEV, CONV_TAPS, conv_s), (1, 0, 2))
        return wt_in, conv_w.reshape(CONV_TAPS, -1), small_all[:, CONV_TAPS * conv_s:].reshape(1, -1)

    def weights_of(arrived_i, arrived_s):
        wg = jnp.transpose(arrived_s[1].reshape(N_DEV, ng, gds, gd), (1, 0, 2, 3)).reshape(ng, gd, gd)
        return arrived_s[0].reshape(-1, d), wg, arrived_i[0]

    def rows_major(gp):
        q, _, hw = gp.shape
        return jnp.transpose(gp.reshape(q, N_DEV, -1, hw), (1, 0, 2, 3))

    to_blocks = dict(ssm_w_in_0=rows_major, ssm_w_in_1=rows_major, ssm_w_out=rows_major, pool_w_out=rows_major,
                     pool_w_in=lambda gp: gp[:, None],
                     pool_w_group=lambda gp: jnp.transpose(gp.reshape(ng, N_DEV, gds, gd // 2), (1, 0, 2, 3)))
    travelling = {}

    def send(name, gp):
        *travelling[name], token = _scatter_start(f"scatter_{name}_start", to_blocks[name](gp))
        return (token,)

    later = _Later(hosted, first_of, weights_of, lambda arrived_p: arrived_p[0].reshape(-1, d), send)

    loss, grad_x, small, _ = _local_step(
        x[0], loss_target[0], ln_g, final_g.reshape(1, d), ssm_conv_b, ssm_dt_bias, ssm_a_log, ssm_d, ssm_norm_g, later)

    def arrived(name, after):
        src, land = _scatter_wait(f"scatter_{name}_wait", *travelling[name], after=after)
        own = lax.dynamic_slice_in_dim(src, me, 1, axis=0)
        return lax.dynamic_update_slice_in_dim(land, own, me, axis=0)

    def reduced(name, rv):
        q = rv.shape[1]
        cols = [_reduce_packed(f"{name}_{j}", rv[:, j]) for j in range(q)]
        return cols[0] if q == 1 else jnp.concatenate(cols, axis=1)

    grads = {}
    grads["pool_w_out"] = reduced("reduce_pool_w_out", arrived("pool_w_out", (grad_x,)))[None]
    grads["pool_w_group"] = _reduce_packed("reduce_pool_w_group", arrived("pool_w_group", (grad_x,)).reshape(
        N_DEV, ng * gds, gd // 2)).reshape(1, ng, gds, gd)
    grads["pool_w_in"] = reduced("reduce_pool_w_in", arrived("pool_w_in", (grad_x,)))[None]
    grads["ssm_w_out"] = reduced("reduce_w_out", arrived("ssm_w_out", (grad_x,)))[None]

    weights = dict(ln_g=ln_g, final_g=final_g, ssm_w_in=ssm_w_in, ssm_conv_w=ssm_conv_w, ssm_conv_b=ssm_conv_b,
                   ssm_dt_bias=ssm_dt_bias, ssm_a_log=ssm_a_log, ssm_d=ssm_d, ssm_norm_g=ssm_norm_g,
                   ssm_w_out=ssm_w_out, pool_w_in=pool_w_in, pool_w_group=pool_w_group, pool_scale=pool_scale,
                   pool_w_out=pool_w_out)
    m_in = dict(ln_g=m_ln_g, final_g=m_final_g, ssm_w_in=m_ssm_w_in, ssm_conv_w=m_ssm_conv_w, ssm_conv_b=m_ssm_conv_b,
                ssm_dt_bias=m_ssm_dt_bias, ssm_a_log=m_ssm_a_log, ssm_d=m_ssm_d, ssm_norm_g=m_ssm_norm_g,
                ssm_w_out=m_ssm_w_out, pool_w_in=m_pool_w_in, pool_w_group=m_pool_w_group, pool_scale=m_pool_scale,
                pool_w_out=m_pool_w_out)
    v_in = dict(ln_g=v_ln_g, final_g=v_final_g, ssm_w_in=v_ssm_w_in, ssm_conv_w=v_ssm_conv_w, ssm_conv_b=v_ssm_conv_b,
                ssm_dt_bias=v_ssm_dt_bias, ssm_a_log=v_ssm_a_log, ssm_d=v_ssm_d, ssm_norm_g=v_ssm_norm_g,
                ssm_w_out=v_ssm_w_out, pool_w_in=v_pool_w_in, pool_w_group=v_pool_w_group, pool_scale=v_pool_scale,
                pool_w_out=v_pool_w_out)
    names = list(weights)
    big_names = ("ssm_w_out", "pool_w_in", "pool_w_group", "pool_w_out", "ssm_w_in")
    delta, new_m, new_v = {}, {}, {}

    def adamw_big(k):
        shp = weights[k].shape
        two_d = (-1, shp[-1])
        dk, mk, vk = _adamw(f"adamw_{k}", weights[k].reshape(two_d), grads[k].reshape(two_d), m_in[k].reshape(two_d),
                            v_in[k].reshape(two_d))
        delta[k], new_m[k], new_v[k] = dk.reshape(shp), mk.reshape(shp), vk.reshape(shp)

    for k in big_names[:-1]:
        adamw_big(k)
    small_names = [k for k in names if k not in big_names]

    small["loss"] = loss
    shapes = {k: small[k].shape for k in SMALL_ORDER}
    gathered_small, = _exchange_alone("all_gather_small_grads", _Gather([_flatten_small(small)]),
                                      after=tuple(delta[k] for k in big_names[:-1]))
    summed = _split_small(_sum_slots("sum_small_grads", gathered_small), shapes)
    grads.update(ln_g=summed["ln_g"], final_g=summed["final_g"].reshape(d), ssm_conv_b=summed["conv_b"],
                 ssm_conv_w=lax.dynamic_slice_in_dim(summed["conv_w"], me * conv_s, conv_s, axis=1)[None],
                 ssm_dt_bias=summed["dt_bias"], ssm_a_log=summed["a_log"], ssm_d=summed["d_skip"],
                 ssm_norm_g=summed["norm_g"],
                 pool_scale=lax.dynamic_slice_in_dim(summed["scale"], me * dp_s, dp_s, axis=1))

    def packed(tree):
        flat = jnp.concatenate([tree[k].reshape(-1) for k in small_names])
        nn = flat.shape[0]
        return jnp.pad(flat, (0, -(-nn // 1024) * 1024 - nn), constant_values=1.0).reshape(-1, 128)

    ds, ms, vs = _adamw("adamw_small", packed(weights), packed(grads), packed(m_in), packed(v_in))
    off = 0
    for k in small_names:
        shp = weights[k].shape
        size = int(np.prod(shp))
        for res, arr in ((delta, ds), (new_m, ms), (new_v, vs)):
            res[k] = arr.reshape(-1)[off:off + size].reshape(shp)
        off += size

    k = "ssm_w_in"
    gt = jnp.concatenate([reduced(f"reduce_w_in_{q}", arrived(f"ssm_w_in_{q}", (ds,))) for q in range(2)], axis=1)
    dk, mk, vk = _adamw("adamw_ssm_w_in", jnp.transpose(weights[k][0]), gt, jnp.transpose(m_in[k][0]),
                        jnp.transpose(v_in[k][0]))
    grads[k], delta[k], new_m[k], new_v[k] = (jnp.transpose(t)[None] for t in (gt, dk, mk, vk))

    return (summed["loss"].reshape(()), grad_x[None], *[grads[k] for k in names], *[delta[k] for k in names],
            *[new_m[k] for k in names], *[new_v[k] for k in names])
```

```python
import math

import jax
import jax.numpy as jnp
import numpy as np
from jax import lax
from jax.experimental import pallas as pl
from jax.experimental.pallas import tpu as pltpu

F32 = jnp.float32
BF = jnp.bfloat16
U32 = jnp.uint32

N_DEV = 8
EPS = 1e-6
SSD_CHUNK = 64
SSM_STATE = 128
CONV_TAPS = 4
HEAD_LANES = 128
POOL_WINDOWS = (2, 4, 8, 16)
ADAM_LR, ADAM_B1, ADAM_B2, ADAM_EPS, ADAM_WD, ADAM_STEP = 0.001, 0.9, 0.999, 1e-08, 0.01, 10
VMEM_LIMIT = 56 * 1024 * 1024
NEG_BIG = -1e30
HAND_ON_AT = 7

NN = ((1,), (0,))
NT = ((1,), (1,))
TN = ((0,), (0,))
MESH = pl.DeviceIdType.MESH


def _t(dim, pref):
    return pref if dim % pref == 0 else dim


def _row_tile(rows, pref, mult=8):
    best = rows
    for cand in range(mult, min(rows, pref) + 1, mult):
        if rows % cand == 0:
            best = cand
    return best


def _params(sem=None):
    return pltpu.CompilerParams(dimension_semantics=sem, vmem_limit_bytes=VMEM_LIMIT)


def _silu(x):
    return x * (1.0 / (1.0 + jnp.exp(-x)))


def _dsilu(x):
    s = 1.0 / (1.0 + jnp.exp(-x))
    return s * (1.0 + x * (1.0 - s))


def _bdot(a, b, dims=NN):
    return lax.dot_general(a.astype(BF), b.astype(BF), (dims, ((), ())), preferred_element_type=F32)


def _pack_pairs(x):
    h = x.shape[1] // 2
    hi = lax.bitcast_convert_type(x[:, :h].astype(jnp.bfloat16).astype(F32), U32)
    lo = lax.bitcast_convert_type(x[:, h:].astype(jnp.bfloat16).astype(F32), U32)
    return lax.bitcast_convert_type(hi | (lo >> 16), F32)


def _unpack_pairs(w):
    u = lax.bitcast_convert_type(w, U32)
    hi = lax.bitcast_convert_type(u & jnp.uint32(0xFFFF0000), F32)
    lo = lax.bitcast_convert_type(u << 16, F32)
    return hi, lo


def _mesh_pos():
    return lax.axis_index("x"), lax.axis_index("y"), lax.axis_index("c")


def _slot(pos):
    return 4 * pos[0] + 2 * pos[1] + pos[2]


class _Gather:
    def __init__(self, arrays):
        self.arrays = list(arrays)
        self.out_shapes = [jax.ShapeDtypeStruct((N_DEV, *s.shape), s.dtype) for s in arrays]

    def phases(self, src, dst, send_sems, recv_sems, local_sems):
        n_arr = len(self.arrays)
        x, y, c = _mesh_pos()
        me, sibling = (x, y, c), (x, y, 1 - c)
        chips = [(1 - x, y), (x, 1 - y), (1 - x, 1 - y)]

        def copy(a, k, block, to, from_src):
            return pltpu.make_async_remote_copy(
                src_ref=src[a] if from_src else dst[a].at[_slot(block)], dst_ref=dst[a].at[_slot(block)],
                send_sem=send_sems.at[a * 7 + k], recv_sem=recv_sems.at[a * 7 + k], device_id=to, device_id_type=MESH)

        def mine(a):
            return pltpu.make_async_copy(src[a], dst[a].at[_slot(me)], local_sems.at[a])

        def first(a):
            return [copy(a, 0, me, sibling, True)] + [copy(a, 1 + j, me, (*chip, c), True)
                                                     for j, chip in enumerate(chips)]

        def start():
            for a in range(n_arr):
                mine(a).start()
                for cp in first(a):
                    cp.start()

        def middle():
            for a in range(n_arr):
                for j, chip in enumerate(chips):
                    copy(a, 1 + j, (*chip, c), me, False).wait_recv()
                    copy(a, 4 + j, (*chip, c), sibling, False).start()

        def finish():
            for a in range(n_arr):
                copy(a, 0, sibling, me, False).wait_recv()
                for j, chip in enumerate(chips):
                    copy(a, 4 + j, (*chip, 1 - c), me, False).wait_recv()
                for cp in first(a):
                    cp.wait_send()
                for j, chip in enumerate(chips):
                    copy(a, 4 + j, (*chip, c), sibling, False).wait_send()
                mine(a).wait()

        return start, middle, finish


def _hosted_call(name, body, grid, in_specs, out_specs, out_shape, scratch_shapes, sem, operands, exch=None,
                 after=(), into=None):
    if into is not None:
        n_lead = len(in_specs)
        inner = body

        def body(*refs):
            inner(*refs[:n_lead], *refs[n_lead + 1:])

        return pl.pallas_call(
            body, grid=grid, in_specs=[*in_specs, pl.BlockSpec(memory_space=pl.ANY)], out_specs=out_specs,
            out_shape=out_shape, scratch_shapes=scratch_shapes, input_output_aliases={n_lead: 0},
            compiler_params=_params(sem), name=name)(*operands, into)
    if after:
        n_lead = len(in_specs)
        inner = body

        def body(*refs):
            inner(*refs[:n_lead], *refs[n_lead + len(after):])

        in_specs = [*in_specs, *[pl.BlockSpec(memory_space=pl.ANY)] * len(after)]
        operands = (*operands, *after)
    if exch is None:
        return pl.pallas_call(body, grid=grid, in_specs=in_specs, out_specs=out_specs, out_shape=out_shape,
                              scratch_shapes=scratch_shapes, compiler_params=_params(sem), name=name)(*operands)
    n_in, n_out, n_scr, ne = len(in_specs), len(out_specs), len(scratch_shapes), len(exch.arrays)
    total = math.prod(grid)

    def wrapped(*refs):
        ins, ex_in = refs[:n_in], refs[n_in:n_in + ne]
        outs = refs[n_in + ne:n_in + ne + n_out]
        ex_out = refs[n_in + ne + n_out:n_in + 2 * ne + n_out]
        scr = refs[n_in + 2 * ne + n_out:n_in + 2 * ne + n_out + n_scr]
        step = 0
        for axis, size in enumerate(grid):
            step = step * size + pl.program_id(axis)
        start, middle, finish = exch.phases(ex_in, ex_out, *refs[-3:])
        pl.when(step == 0)(start)
        if middle is not None:
            pl.when(step == (total * HAND_ON_AT) // 8)(middle)
        body(*ins, *outs, *scr)
        pl.when(step == total - 1)(finish)

    hbm = pl.BlockSpec(memory_space=pl.ANY)
    sems = [pltpu.SemaphoreType.DMA((ne * 7,)), pltpu.SemaphoreType.DMA((ne * 7,)), pltpu.SemaphoreType.DMA((ne,))]
    return pl.pallas_call(
        wrapped, grid=grid, in_specs=[*in_specs, *[hbm] * ne], out_specs=[*out_specs, *[hbm] * ne],
        out_shape=[*out_shape, *exch.out_shapes], scratch_shapes=[*scratch_shapes, *sems],
        compiler_params=pltpu.CompilerParams(dimension_semantics=("arbitrary",) * len(grid),
                                             vmem_limit_bytes=VMEM_LIMIT, has_side_effects=True),
        name=name)(*operands, *exch.arrays)


def _flip_peers():
    x, y, c = _mesh_pos()
    peers = []
    for k in range(1, N_DEV):
        fx, fy, fc = (k >> 2) & 1, (k >> 1) & 1, k & 1
        peers.append((1 - x if fx else x, 1 - y if fy else y, 1 - c if fc else c))
    return (x, y, c), peers


def _split_scatter_copy(src, land, send_sems, recv_sems, k, me, peer, sending):
    return pltpu.make_async_remote_copy(
        src_ref=src.at[_slot(peer)], dst_ref=land.at[_slot(me) if sending else _slot(peer)],
        send_sem=send_sems.at[k], recv_sem=recv_sems.at[k], device_id=peer, device_id_type=MESH)


def _scatter_start(name, blocks):
    def body(src, land, send_sems, recv_sems, src_thru, land_thru, token):
        me, peers = _flip_peers()
        for k, peer in enumerate(peers):
            _split_scatter_copy(src, land, send_sems, recv_sems, k, me, peer, True).start()
        token[...] = jnp.zeros_like(token)

    hbm = pl.BlockSpec(memory_space=pltpu.HBM)
    sem = pl.BlockSpec(memory_space=pltpu.SEMAPHORE)
    return pl.pallas_call(
        body, name=name,
        out_shape=(pltpu.SemaphoreType.DMA((N_DEV - 1,)), pltpu.SemaphoreType.DMA((N_DEV - 1,)),
                   pltpu.HBM(blocks.shape, blocks.dtype), pltpu.HBM(blocks.shape, blocks.dtype),
                   jax.ShapeDtypeStruct((8, 128), F32)),
        in_specs=(hbm, hbm), out_specs=(sem, sem, hbm, hbm, pl.BlockSpec(memory_space=pltpu.VMEM)),
        input_output_aliases={0: 2, 1: 3},
        compiler_params=pltpu.CompilerParams(has_side_effects=pltpu.SideEffectType.DATAFLOW_SIDE_EFFECTING),
    )(pltpu.with_memory_space_constraint(blocks, pltpu.HBM),
      pltpu.with_memory_space_constraint(lax.empty(blocks.shape, blocks.dtype), pltpu.HBM))


def _scatter_wait(name, send_sems, recv_sems, src_thru, land_thru, after):
    def body(src, land, send_sems, recv_sems, *rest):
        me, peers = _flip_peers()
        for k, peer in enumerate(peers):
            _split_scatter_copy(src, land, send_sems, recv_sems, k, me, peer, True).wait_send()
            _split_scatter_copy(src, land, send_sems, recv_sems, k, me, peer, False).wait_recv()

    hbm = pl.BlockSpec(memory_space=pltpu.HBM)
    sem = pl.BlockSpec(memory_space=pltpu.SEMAPHORE)
    return pl.pallas_call(
        body, name=name,
        out_shape=(pltpu.HBM(src_thru.shape, src_thru.dtype), pltpu.HBM(land_thru.shape, land_thru.dtype)),
        in_specs=(hbm, hbm, sem, sem, *[pl.BlockSpec(memory_space=pl.ANY)] * len(after)), out_specs=(hbm, hbm),
        input_output_aliases={0: 0, 1: 1},
        compiler_params=pltpu.CompilerParams(has_side_effects=pltpu.SideEffectType.DATAFLOW_SIDE_EFFECTING),
    )(src_thru, land_thru, send_sems, recv_sems, *after)


def _exchange_alone(name, exch, after=()):
    def body():
        pass

    return _hosted_call(name, body, (1,), [], [], [], [], None, (), exch, after)


def _mm(name, grid, a, a_spec, b, b_spec, dims, outs, o_specs, acc_shape, extra=(), extra_specs=(), epi=None,
        exch=None, after=(), into=None, summed=None):
    nk = grid[-1]
    n_extra, n_out = len(extra), len(outs)

    def body(*refs):
        a_ref, b_ref = refs[0], refs[1]
        ex = refs[2:2 + n_extra]
        o_refs = refs[2 + n_extra:2 + n_extra + n_out]

        def write(res):
            res = (res,) if epi is None else epi(res, *[e[...] for e in ex])
            for idx, (o, r) in enumerate(zip(o_refs, res)):
                if summed is not None and idx == summed[0]:
                    first = pl.program_id(summed[1]) == 0

                    @pl.when(first)
                    def _(o=o, r=r):
                        o[...] = r.astype(o.dtype)

                    @pl.when(jnp.logical_not(first))
                    def _(o=o, r=r):
                        o[...] += r.astype(o.dtype)
                else:
                    o[...] = r.astype(o.dtype)

        if nk == 1:
            write(_bdot(a_ref[...], b_ref[...], dims))
            return
        acc = refs[-1]
        k = pl.program_id(len(grid) - 1)

        @pl.when(k == 0)
        def _():
            acc[...] = _bdot(a_ref[...], b_ref[...], dims)

        @pl.when(jnp.logical_and(k > 0, k < nk - 1))
        def _():
            acc[...] += _bdot(a_ref[...], b_ref[...], dims)

        @pl.when(k == nk - 1)
        def _():
            write(acc[...] + _bdot(a_ref[...], b_ref[...], dims))

    sem = ("parallel",) * (len(grid) - 1) + ("arbitrary",)
    if summed is not None:
        sem = ("arbitrary",) * len(grid)
    scratch = [] if nk == 1 else [pltpu.VMEM(acc_shape, F32)]
    return _hosted_call(name, body, grid, [a_spec, b_spec, *extra_specs], list(o_specs), list(outs),
                        scratch, sem, (a, b, *extra), exch, after, into)


def _add_epi(acc, add):
    return (acc + add,)


def _pack_epi(acc):
    return (_pack_pairs(acc),)


def _mm_nt(name, a, b, n_off, n, out_dtype=F32, tm=1024, tn=512, exch=None, after=()):
    m, kk = a.shape
    tm, tn = _t(m, tm), math.gcd(_t(n, tn), n_off)
    res = _mm(name, (m // tm, n // tn, 1),
              a, pl.BlockSpec((tm, kk), lambda i, j, k: (i, 0)),
              b, pl.BlockSpec((tn, kk), lambda i, j, k: (n_off // tn + j, 0)), NT,
              [jax.ShapeDtypeStruct((m, n), out_dtype)], [pl.BlockSpec((tm, tn), lambda i, j, k: (i, j))], (tm, tn),
              exch=exch, after=after)
    return res[0] if exch is None else res


def _mm_nt_halves(name, a, b2, n_off, n, tm=1024, tn=512, exch=None):
    m = a.shape[0]
    kh = b2.shape[2]
    tm, tn = _t(m, tm), math.gcd(_t(n, tn), n_off)

    def body(a_ref, b_ref, o_ref):
        o_ref[...] = _bdot(a_ref[:, :kh], b_ref[0], NT) + _bdot(a_ref[:, kh:], b_ref[1], NT)

    res = _hosted_call(name, body, (m // tm, n // tn),
                       [pl.BlockSpec((tm, 2 * kh), lambda i, j: (i, 0)),
                        pl.BlockSpec((2, tn, kh), lambda i, j: (0, n_off // tn + j, 0))],
                       [pl.BlockSpec((tm, tn), lambda i, j: (i, j))], [jax.ShapeDtypeStruct((m, n), F32)], [],
                       ("parallel", "parallel"), (a, b2), exch)
    return res[0] if exch is None else res


def _mm_nn_halves(name, a, b2, k_off=0, add=None, after=(), tm=1024, tk=1024):
    m, kk = a.shape
    half = b2.shape[2]
    tm, tk = _t(m, tm), math.gcd(_t(kk, tk), k_off)
    extra, especs, epi = (), (), None
    if add is not None:
        extra, especs, epi = (add,), (pl.BlockSpec((tm, half), lambda i, j, k: (i, j)),), _add_epi
    return _mm(name, (m // tm, 2, kk // tk),
               a, pl.BlockSpec((tm, tk), lambda i, j, k: (i, k)),
               b2, pl.BlockSpec((None, tk, half), lambda i, j, k: (j, k_off // tk + k, 0)), NN,
               [jax.ShapeDtypeStruct((m, 2 * half), F32)], [pl.BlockSpec((tm, half), lambda i, j, k: (i, j))],
               (tm, half), extra, especs, epi, after=after)[0]


def _mm_nn_blocked(name, a, b3, tm=1024, tn=512, exch=None):
    m, kk = a.shape
    nb, _, cs = b3.shape
    tm, tn = _t(m, tm), _t(cs, tn)
    per = cs // tn
    res = _mm(name, (m // tm, nb * per, 1),
              a, pl.BlockSpec((tm, kk), lambda i, j, k: (i, 0)),
              b3, pl.BlockSpec((None, kk, tn), lambda i, j, k: (j // per, 0, j % per)), NN,
              [jax.ShapeDtypeStruct((m, nb * cs), F32)], [pl.BlockSpec((tm, tn), lambda i, j, k: (i, j))],
              (tm, tn), exch=exch)
    return res[0] if exch is None else res


def _mm_nt_blocked(name, a, b3, after=(), tm=1024, tn=1024):
    m = a.shape[0]
    nb, n, cs = b3.shape
    tm, tn = _t(m, tm), _t(n, tn)
    return _mm(name, (m // tm, n // tn, nb),
               a, pl.BlockSpec((tm, cs), lambda i, j, k: (i, k)),
               b3, pl.BlockSpec((None, tn, cs), lambda i, j, k: (k, j, 0)), NT,
               [jax.ShapeDtypeStruct((m, n), F32)], [pl.BlockSpec((tm, tn), lambda i, j, k: (i, j))], (tm, tn),
               after=after)[0]


def _mm_nn(name, a, b, k_off=0, kk=None, add=None, out_dtype=F32, tm=1024, tn=1024, tk=1024, after=()):
    m = a.shape[0]
    kk = a.shape[1] if kk is None else kk
    n = b.shape[1]
    tm, tn, tk = _t(m, tm), _t(n, tn), math.gcd(_t(kk, tk), k_off)
    extra, especs, epi = (), (), None
    if add is not None:
        extra, especs, epi = (add,), (pl.BlockSpec((tm, tn), lambda i, j, k: (i, j)),), _add_epi
    return _mm(name, (m // tm, n // tn, kk // tk),
               a, pl.BlockSpec((tm, tk), lambda i, j, k: (i, k)),
               b, pl.BlockSpec((tk, tn), lambda i, j, k: (k_off // tk + k, j)), NN,
               [jax.ShapeDtypeStruct((m, n), out_dtype)], [pl.BlockSpec((tm, tn), lambda i, j, k: (i, j))], (tm, tn),
               extra, especs, epi, after=after)[0]


TN_ACC_ELEMENTS = 1 << 20


def _mm_tn_packed(name, a, b, pw, tk=1024, rows=None, into=None, b_blk=None, after=()):
    kk, m = a.shape
    nq = b.shape[1] // pw if b_blk is None else 1
    first = 0 if b_blk is None else b_blk
    tm, tk = _t(m, TN_ACC_ELEMENTS // pw), _t(kk, tk)
    rows = m if rows is None else rows
    row_blk = 0
    if into is not None:
        rows = into.shape[1]
        assert (rows - m) % tm == 0
        row_blk = (rows - m) // tm
    return _mm(name, (m // tm, nq, kk // tk),
               a, pl.BlockSpec((tk, tm), lambda i, j, k: (k, i)),
               b, pl.BlockSpec((tk, pw), lambda i, j, k: (k, first + j)), TN,
               [jax.ShapeDtypeStruct((nq, rows, pw // 2), F32)],
               [pl.BlockSpec((None, tm, pw // 2), lambda i, j, k: (j, row_blk + i, 0))], (tm, pw), epi=_pack_epi,
               into=into, after=after)[0]


def _rms_fwd(name, x, g, tm=256, exch=None):
    l, d = x.shape
    tm = _t(l, tm)

    def body(x_ref, g_ref, h_ref, r_ref):
        xv = x_ref[...]
        r = lax.rsqrt(jnp.mean(xv * xv, axis=-1, keepdims=True) + EPS)
        h_ref[...] = (xv * r * g_ref[...]).astype(BF)
        r_ref[...] = r

    return _hosted_call(
        name, body, (l // tm,),
        [pl.BlockSpec((tm, d), lambda i: (i, 0)), pl.BlockSpec((1, d), lambda i: (0, 0))],
        [pl.BlockSpec((tm, d), lambda i: (i, 0)), pl.BlockSpec((tm, 1), lambda i: (i, 0))],
        [jax.ShapeDtypeStruct((l, d), BF), jax.ShapeDtypeStruct((l, 1), F32)], [], ("parallel",), (x, g), exch)


def _rms_bwd(name, dh, x, r, g, dres, tm=256, bf16_copy=True):
    l, d = x.shape
    tm = _t(l, tm)

    def body(dh_ref, x_ref, r_ref, g_ref, dres_ref, dx_ref, *rest):
        dg_ref = rest[-1]
        i = pl.program_id(0)
        rr = r_ref[...]
        xhat = x_ref[...] * rr
        dhv = dh_ref[...]
        dxh = dhv * g_ref[...]
        dx = dres_ref[...] + rr * (dxh - xhat * jnp.mean(dxh * xhat, axis=-1, keepdims=True))
        dx_ref[...] = dx
        if bf16_copy:
            rest[0][...] = dx.astype(BF)

        @pl.when(i == 0)
        def _():
            dg_ref[...] = jnp.zeros_like(dg_ref)

        dg_ref[...] += jnp.sum(dhv * xhat, axis=0, keepdims=True)

    row = pl.BlockSpec((tm, d), lambda i: (i, 0))
    vec = pl.BlockSpec((1, d), lambda i: (0, 0))
    copy_spec = [row] if bf16_copy else []
    copy_shape = [jax.ShapeDtypeStruct((l, d), BF)] if bf16_copy else []
    return pl.pallas_call(
        body, grid=(l // tm,), in_specs=[row, row, pl.BlockSpec((tm, 1), lambda i: (i, 0)), vec, row],
        out_specs=[row, *copy_spec, vec],
        out_shape=[jax.ShapeDtypeStruct((l, d), F32), *copy_shape, jax.ShapeDtypeStruct((1, d), F32)],
        compiler_params=_params(("arbitrary",)), name=name)(dh, x, r, g, dres)


def _loss_head(x, g, target, tm=256):
    l, d = x.shape
    tm = _t(l, tm)

    def body(x_ref, g_ref, t_ref, dx_ref, dxb_ref, dg_ref, loss_ref):
        i = pl.program_id(0)
        xv = x_ref[...]
        gv = g_ref[...]
        r = lax.rsqrt(jnp.mean(xv * xv, axis=-1, keepdims=True) + EPS)
        xhat = xv * r
        e = xhat * gv - t_ref[...]
        dy = e * (1.0 / d)
        dxh = dy * gv
        dx = r * (dxh - xhat * jnp.mean(dxh * xhat, axis=-1, keepdims=True))
        dx_ref[...] = dx
        dxb_ref[...] = dx.astype(BF)

        @pl.when(i == 0)
        def _():
            dg_ref[...] = jnp.zeros_like(dg_ref)
            loss_ref[...] = jnp.zeros_like(loss_ref)

        dg_ref[...] += jnp.sum(dy * xhat, axis=0, keepdims=True)
        loss_ref[...] += 0.5 * jnp.sum(jnp.sum(e * e, axis=-1, keepdims=True) * (1.0 / d), axis=0, keepdims=True)

    row = pl.BlockSpec((tm, d), lambda i: (i, 0))
    vec = pl.BlockSpec((1, d), lambda i: (0, 0))
    return pl.pallas_call(
        body, grid=(l // tm,), in_specs=[row, vec, row],
        out_specs=[row, row, vec, pl.BlockSpec((1, 1), lambda i: (0, 0))],
        out_shape=[jax.ShapeDtypeStruct((l, d), F32), jax.ShapeDtypeStruct((l, d), BF),
                   jax.ShapeDtypeStruct((1, d), F32), jax.ShapeDtypeStruct((1, 1), F32)],
        compiler_params=_params(("arbitrary",)), name="loss_head")(x, g, target)


CONV_HALO = 8


def _conv_pre(x_ref, w_ref, b_ref, i, tr):
    r0 = pl.multiple_of(i * tr, tr)
    cur = x_ref[pl.ds(r0, tr), :]
    prev = x_ref[pl.ds(pl.multiple_of(jnp.maximum(r0 - CONV_HALO, 0), CONV_HALO), CONV_HALO), :]
    prev = jnp.where(i > 0, prev, 0.0)
    ext = jnp.concatenate([prev, cur], axis=0)
    taps = []
    for k in range(CONV_TAPS):
        s = CONV_TAPS - 1 - k
        taps.append(cur if s == 0 else pltpu.roll(ext, s, 0)[CONV_HALO:])
    pre = b_ref[...] + sum(w_ref[k:k + 1, :] * taps[k] for k in range(CONV_TAPS))
    return r0, pre, taps


def _conv_fwd(pm, col_off, conv_w, conv_b, cw=256, tr=512):
    l = pm.shape[0]
    c = conv_w.shape[1]
    cw, tr = _t(c, cw), _t(l, tr)
    assert col_off % cw == 0

    def body(x_ref, w_ref, b_ref, o_ref):
        def step(i, carry):
            r0, pre, _ = _conv_pre(x_ref, w_ref, b_ref, i, tr)
            o_ref[pl.ds(r0, tr), :] = _silu(pre)
            return carry
        lax.fori_loop(0, l // tr, step, 0)

    return pl.pallas_call(
        body, grid=(c // cw,),
        in_specs=[pl.BlockSpec((l, cw), lambda j: (0, col_off // cw + j)),
                  pl.BlockSpec((CONV_TAPS, cw), lambda j: (0, j)), pl.BlockSpec((1, cw), lambda j: (0, j))],
        out_specs=pl.BlockSpec((l, cw), lambda j: (0, j)), out_shape=jax.ShapeDtypeStruct((l, c), F32),
        compiler_params=_params(("parallel",)), name="conv_fwd")(pm, conv_w, conv_b)


def _conv_bwd(pm, col_off, conv_w, conv_b, dy, dproj, cw=256, tr=512):
    l = pm.shape[0]
    c = conv_w.shape[1]
    cw, tr = _t(c, cw), _t(l, tr)
    nt = l // tr

    def body(x_ref, w_ref, b_ref, dy_ref, _, dx_ref, dw_ref, db_ref, dpre_ref):
        def step1(i, carry):
            dws, db = carry
            r0, pre, taps = _conv_pre(x_ref, w_ref, b_ref, i, tr)
            dpre = dy_ref[pl.ds(r0, tr), :] * _dsilu(pre)
            dpre_ref[pl.ds(r0, tr), :] = dpre
            dws = tuple(dws[k] + jnp.sum(dpre * taps[k], axis=0, keepdims=True) for k in range(CONV_TAPS))
            return dws, db + jnp.sum(dpre, axis=0, keepdims=True)

        z = jnp.zeros((1, cw), F32)
        dws, db = lax.fori_loop(0, nt, step1, ((z,) * CONV_TAPS, z))
        for k in range(CONV_TAPS):
            dw_ref[k:k + 1, :] = dws[k]
        db_ref[...] = db

        def step2(i, carry):
            r0 = pl.multiple_of(i * tr, tr)
            cur = dpre_ref[pl.ds(r0, tr), :]
            nxt = dpre_ref[pl.ds(pl.multiple_of(jnp.minimum(r0 + tr, l - CONV_HALO), CONV_HALO), CONV_HALO), :]
            nxt = jnp.where(i < nt - 1, nxt, 0.0)
            ext = jnp.concatenate([cur, nxt], axis=0)
            acc = w_ref[CONV_TAPS - 1:CONV_TAPS, :] * cur
            for k in range(CONV_TAPS - 1):
                s = CONV_TAPS - 1 - k
                acc = acc + w_ref[k:k + 1, :] * pltpu.roll(ext, tr + CONV_HALO - s, 0)[:tr]
            dx_ref[pl.ds(r0, tr), :] = acc.astype(dx_ref.dtype)
            return carry
        lax.fori_loop(0, nt, step2, 0)

    col = pl.BlockSpec((l, cw), lambda j: (0, j))
    shifted = pl.BlockSpec((l, cw), lambda j: (0, col_off // cw + j))
    return pl.pallas_call(
        body, grid=(c // cw,),
        in_specs=[shifted, pl.BlockSpec((CONV_TAPS, cw), lambda j: (0, j)), pl.BlockSpec((1, cw), lambda j: (0, j)),
                  col, pl.BlockSpec(memory_space=pl.ANY)],
        out_specs=[shifted, pl.BlockSpec((CONV_TAPS, cw), lambda j: (0, j)), pl.BlockSpec((1, cw), lambda j: (0, j))],
        out_shape=[jax.ShapeDtypeStruct(dproj.shape, dproj.dtype), jax.ShapeDtypeStruct((CONV_TAPS, c), F32),
                   jax.ShapeDtypeStruct((1, c), F32)],
        scratch_shapes=[pltpu.VMEM((l, cw), F32)], input_output_aliases={4: 0},
        compiler_params=_params(("parallel",)), name="conv_bwd")(pm, conv_w, conv_b, dy, dproj)


def _split(x, pieces):
    out, rest = [], x
    for _ in range(pieces):
        piece = rest.astype(BF)
        out.append(piece)
        rest = rest - piece.astype(F32)
    return out


def _rows_times(xs, m_stack, pieces):
    x = xs[0] if len(xs) == 1 else jnp.concatenate(xs, axis=0)
    out = _bdot(jnp.concatenate(_split(x, pieces), axis=1), m_stack)
    sizes = [v.shape[0] for v in xs]
    offs = np.cumsum([0] + sizes)
    return [out[offs[i]:offs[i + 1]] for i in range(len(xs))]


def _times_rows(m_stack, x, pieces):
    return _bdot(m_stack, jnp.concatenate(_split(x, pieces), axis=0))


EXPAND_PIECES = 3
FOLD_PIECES = 2


def _ssd_consts(r, p, t):
    assert p == t, "heads expand to P lanes of the inputs and to T lanes of the decay matrices alike"
    assert LANES % p == 0 and (r * p) % LANES == 0, "whole heads per lane tile, whole lane tiles per group"
    rp = r * p
    tri = np.tril(np.ones((t, t), np.float32))
    ep = np.zeros((HEAD_LANES, rp), np.float32)
    ep[np.arange(rp) // p, np.arange(rp)] = 1.0
    itile = (np.arange(t)[:, None] == (np.arange(rp) % t)[None, :]).astype(np.float32)
    lmask = (np.arange(t)[:, None] >= (np.arange(rp) % t)[None, :]).astype(np.float32)
    return [jnp.asarray(np.concatenate([ep] * EXPAND_PIECES, axis=0), BF),
            jnp.asarray(np.concatenate([ep.T] * FOLD_PIECES, axis=0), BF),
            jnp.asarray(np.concatenate([tri] * EXPAND_PIECES, axis=1), BF),
            jnp.asarray(np.concatenate([tri.T] * EXPAND_PIECES, axis=1), BF),
            jnp.asarray(itile), jnp.asarray(lmask)]


def _ssd_heads(dtr_ref, dtb_ref, alog_ref, e_stack, tri_stack, g_n, t):
    pre = [dtr_ref[g] + dtb_ref[g] for g in range(g_n)]
    dt = [jnp.maximum(v, 0.0) + jnp.log(1.0 + jnp.exp(-jnp.abs(v))) for v in pre]
    a = [-jnp.exp(alog_ref[g]) for g in range(g_n)]
    adt = jnp.concatenate([dt[g] * a[g] for g in range(g_n)], axis=1)
    cs = _times_rows(tri_stack, adt, EXPAND_PIECES)
    cs = [cs[:, g * HEAD_LANES:(g + 1) * HEAD_LANES] for g in range(g_n)]
    at_lanes = _rows_times([v for g in range(g_n) for v in (dt[g], cs[g])], e_stack, EXPAND_PIECES)
    return [(pre[g], dt[g], a[g], cs[g][t - 1:t, :], at_lanes[2 * g], at_lanes[2 * g + 1]) for g in range(g_n)]


LANES = 128


def _head_lanes(k, p, rows):
    lane = lax.broadcasted_iota(jnp.int32, (rows, LANES), 1)
    j = k % (LANES // p)
    return jnp.logical_and(lane >= j * p, lane < (j + 1) * p)


def _block_diagonal(v, r, p):
    rows = []
    for k in range(r):
        c = k * p // LANES
        tiles = [jnp.zeros((v.shape[0], LANES), v.dtype)] * (r * p // LANES)
        tiles[c] = jnp.where(_head_lanes(k, p, v.shape[0]), v[:, c * LANES:(c + 1) * LANES], jnp.zeros((), v.dtype))
        rows.append(jnp.concatenate(tiles, axis=1))
    return jnp.concatenate(rows, axis=0)


def _diagonal_blocks(m, r, p):
    t = m.shape[0] // r
    tiles = []
    for c in range(r * p // LANES):
        heads = range(c * LANES // p, (c + 1) * LANES // p)
        acc = None
        for k in heads:
            blk = m[k * t:(k + 1) * t, c * LANES:(c + 1) * LANES]
            acc = blk if acc is None else jnp.where(_head_lanes(k, p, t), blk, acc)
        tiles.append(acc)
    return jnp.concatenate(tiles, axis=1)


def _ssd_common(xs, bm, dtx, csx, itile, lmask, r, t):
    ecsx = jnp.exp(csx)
    lastx = csx[t - 1:t, :]
    wx = jnp.exp(lastx - csx)
    elastx = jnp.exp(lastx)
    csrow = jnp.sum(csx * itile, axis=0, keepdims=True)
    lx = jnp.exp(jnp.where(lmask > 0.0, csx - csrow, NEG_BIG))
    xdt = xs * dtx
    xblk = _block_diagonal(xdt.astype(BF), r, t)
    btile = jnp.concatenate([bm.astype(BF)] * r, axis=0)
    return ecsx, wx, elastx, lx, xdt, xblk, btile


def _ssd_specs(l, g_n, r, p, n, t, conv, rev):
    nc = l // t
    rp = r * p
    cidx = (lambda c: nc - 1 - c) if rev else (lambda c: c)
    row_spec = lambda width: pl.BlockSpec((t, width), lambda c: (cidx(c), 0))
    dtr_spec = pl.BlockSpec((g_n, t, HEAD_LANES), lambda c: (0, cidx(c), 0))
    par_spec = pl.BlockSpec((g_n, 1, HEAD_LANES), lambda c: (0, 0, 0))
    dskx_spec = pl.BlockSpec((g_n, 1, rp), lambda c: (0, 0, 0))
    st_spec = pl.BlockSpec((None, g_n, n, rp), lambda c: (cidx(c), 0, 0, 0))
    return nc, row_spec, dtr_spec, par_spec, dskx_spec, st_spec


def _const_specs(consts):
    return [pl.BlockSpec(a.shape, lambda c: (0, 0)) for a in consts]


def _ssd_fwd(xbc, dtr, dtb, alog, dskx, pm, norm_g, dims, exch=None):
    h, g_n, r, p, n, t = dims
    l, conv = xbc.shape
    rp, hp = r * p, h * p
    nc, row_spec, dtr_spec, par_spec, dskx_spec, st_spec = _ssd_specs(l, g_n, r, p, n, t, conv, False)
    consts = _ssd_consts(r, p, t)

    def body(x_ref, dtr_ref, dtb_ref, alog_ref, dskx_ref, z_ref, ng_ref,
             e_ref, et_ref, tri_ref, trit_ref, it_ref, lm_ref,
             y_ref, st_ref, yn_ref, rn_ref, s_ref):
        @pl.when(pl.program_id(0) == 0)
        def _():
            s_ref[...] = jnp.zeros_like(s_ref)

        heads = _ssd_heads(dtr_ref, dtb_ref, alog_ref, e_ref[...], tri_ref[...], g_n, t)
        for g in range(g_n):
            xs = x_ref[:, g * rp:(g + 1) * rp]
            bm = x_ref[:, hp + g * n:hp + (g + 1) * n]
            cm = x_ref[:, hp + (g_n + g) * n:hp + (g_n + g + 1) * n]
            dtx, csx = heads[g][4:]
            ecsx, wx, elastx, lx, xdt, xblk, btile = _ssd_common(xs, bm, dtx, csx, it_ref[...], lm_ref[...], r, t)
            s_in = s_ref[g]
            st_ref[g] = s_in
            cbx = _bdot(cm, btile, NT)
            yd = _bdot(cbx * lx, xblk)
            yo = ecsx * _bdot(cm, s_in)
            y_ref[:, g * rp:(g + 1) * rp] = yd + yo + dskx_ref[g] * xs
            s_ref[g] = elastx * s_in + _bdot(bm, xdt * wx, TN)

        v = y_ref[...] * _silu(z_ref[...])
        rr = lax.rsqrt(jnp.mean(v * v, axis=-1, keepdims=True) + EPS)
        yn_ref[...] = (v * rr * ng_ref[...]).astype(BF)
        rn_ref[...] = rr

    return _hosted_call(
        "ssd_fwd", body, (nc,),
        [row_spec(conv), dtr_spec, par_spec, par_spec, dskx_spec, row_spec(hp),
         pl.BlockSpec((1, hp), lambda c: (0, 0)), *_const_specs(consts)],
        [row_spec(hp), st_spec, row_spec(hp), row_spec(1)],
        [jax.ShapeDtypeStruct((l, hp), F32), jax.ShapeDtypeStruct((nc, g_n, n, rp), F32),
         jax.ShapeDtypeStruct((l, hp), BF), jax.ShapeDtypeStruct((l, 1), F32)],
        [pltpu.VMEM((g_n, n, rp), F32)], ("arbitrary",),
        (xbc, dtr, dtb, alog, dskx, pm, norm_g, *consts), exch)


def _ssd_bwd(xbc, dtr, dtb, alog, dskx, states, dyn, y, pm, rstd, norm_g, dims, after=()):
    h, g_n, r, p, n, t = dims
    l, conv = xbc.shape
    rp, hp = r * p, h * p
    nc, row_spec, dtr_spec, par_spec, dskx_spec, st_spec = _ssd_specs(l, g_n, r, p, n, t, conv, True)
    consts = _ssd_consts(r, p, t)

    def fold_rows(v, rows):
        return sum(v[k * rows:(k + 1) * rows, :] for k in range(r))

    def body(x_ref, dtr_ref, dtb_ref, alog_ref, dskx_ref, st_ref, dyn_ref, y_ref, z_ref, rn_ref, ng_ref,
             e_ref, et_ref, tri_ref, trit_ref, it_ref, lm_ref,
             dx_ref, ddt_ref, dbias_ref, dalog_ref, dd_ref, dz_ref, dng_ref, ds_ref, dy_ref):
        @pl.when(pl.program_id(0) == 0)
        def _():
            ds_ref[...] = jnp.zeros_like(ds_ref)
            dbias_ref[...] = jnp.zeros_like(dbias_ref)
            dalog_ref[...] = jnp.zeros_like(dalog_ref)
            dd_ref[...] = jnp.zeros_like(dd_ref)
            dng_ref[...] = jnp.zeros_like(dng_ref)

        yv, zv, rr, dn = y_ref[...], z_ref[...], rn_ref[...], dyn_ref[...]
        sz = _silu(zv)
        vhat = yv * sz * rr
        dvh = dn * ng_ref[...]
        dv = rr * (dvh - vhat * jnp.mean(dvh * vhat, axis=-1, keepdims=True))
        dy_ref[...] = dv * sz
        dz_ref[...] = (dv * yv * _dsilu(zv)).astype(BF)
        dng_ref[...] += jnp.sum(dn * vhat, axis=0, keepdims=True)

        itile = it_ref[...]
        last_row = lax.broadcasted_iota(jnp.int32, (t, HEAD_LANES), 0) == t - 1
        heads = _ssd_heads(dtr_ref, dtb_ref, alog_ref, e_ref[...], tri_ref[...], g_n, t)
        to_fold = []
        for g in range(g_n):
            xs = x_ref[:, g * rp:(g + 1) * rp]
            bm = x_ref[:, hp + g * n:hp + (g + 1) * n]
            cm = x_ref[:, hp + (g_n + g) * n:hp + (g_n + g + 1) * n]
            dy = dy_ref[:, g * rp:(g + 1) * rp]
            dtx, csx = heads[g][4:]
            ecsx, wx, elastx, lx, xdt, xblk, btile = _ssd_common(xs, bm, dtx, csx, itile, lm_ref[...], r, t)
            s_in = st_ref[g]
            ds_out = ds_ref[g]

            cbx = _bdot(cm, btile, NT)
            amat = cbx * lx
            da = _bdot(dy, xblk, NT)
            dxdt = _diagonal_blocks(_bdot(amat, dy, TN), r, t)
            dcbx = da * lx
            q = da * amat
            dc = _bdot(dcbx, btile)
            db = fold_rows(_bdot(dcbx, cm, TN), t)

            g0 = _bdot(cm, s_in)
            dg0 = dy * ecsx
            dc = dc + _bdot(dg0, s_in, NT)

            z = _bdot(bm, ds_out)
            dxdt = dxdt + z * wx
            db = db + _bdot(xdt * wx, ds_out, NT)
            ds_ref[g] = elastx * ds_out + _bdot(cm, dg0, TN)

            dx_ref[:, g * rp:(g + 1) * rp] = dskx_ref[g] * dy + dxdt * dtx
            dx_ref[:, hp + g * n:hp + (g + 1) * n] = db
            dx_ref[:, hp + (g_n + g) * n:hp + (g_n + g + 1) * n] = dc

            dwx = z * xdt * wx
            dlastx = jnp.sum(dwx, axis=0, keepdims=True) + jnp.sum(ds_out * s_in, axis=0, keepdims=True) * elastx
            rows = jnp.concatenate([jnp.sum(dy * xs, axis=0, keepdims=True), dlastx, jnp.zeros((14, rp), F32)], axis=0)
            to_fold += [q - itile * jnp.sum(q, axis=0, keepdims=True) + dy * g0 * ecsx - dwx, dxdt * xs, rows]

        folded = _rows_times(to_fold, et_ref[...], FOLD_PIECES)
        dcs = []
        for g in range(g_n):
            f_cs, f_rows = folded[3 * g], folded[3 * g + 2]
            dd_ref[g] += f_rows[0:1]
            dcs.append(f_cs + jnp.where(last_row, f_rows[1:2], 0.0))
        dadt_all = _times_rows(trit_ref[...], jnp.concatenate(dcs, axis=1), EXPAND_PIECES)
        for g in range(g_n):
            pre, dt, a = heads[g][:3]
            dadt = dadt_all[:, g * HEAD_LANES:(g + 1) * HEAD_LANES]
            dalog_ref[g] += jnp.sum(dadt * dt, axis=0, keepdims=True) * a
            dpre = (dadt * a + folded[3 * g + 1]) * (1.0 / (1.0 + jnp.exp(-pre)))
            ddt_ref[g] = dpre
            dbias_ref[g] += jnp.sum(dpre, axis=0, keepdims=True)

    par_shape = jax.ShapeDtypeStruct((g_n, 1, HEAD_LANES), F32)
    vec = pl.BlockSpec((1, hp), lambda c: (0, 0))
    return _hosted_call(
        "ssd_bwd", body, (nc,),
        [row_spec(conv), dtr_spec, par_spec, par_spec, dskx_spec, st_spec, row_spec(hp), row_spec(hp), row_spec(hp),
         row_spec(1), vec, *_const_specs(consts)],
        [row_spec(conv), dtr_spec, par_spec, par_spec, par_spec, row_spec(hp), vec],
        [jax.ShapeDtypeStruct((l, conv), F32), jax.ShapeDtypeStruct((g_n, l, HEAD_LANES), F32),
         par_shape, par_shape, par_shape, jax.ShapeDtypeStruct((l, pm.shape[1]), BF),
         jax.ShapeDtypeStruct((1, hp), F32)],
        [pltpu.VMEM((g_n, n, rp), F32), pltpu.VMEM((t, hp), F32)], ("arbitrary",),
        (xbc, dtr, dtb, alog, dskx, states, dyn, y, pm, rstd, norm_g, *consts), after=after)


POOL_HALO = 16


def _pool_mix(name, src, dp, backward, out_dtype, into=None, cw=256, tr=512):
    l = src.shape[0]
    gd = dp // len(POOL_WINDOWS)
    cwl, trl = _t(gd, cw), _t(l, tr)
    nt = l // trl

    def body(x_ref, *rest):
        o_ref = rest[-1]
        gi = pl.program_id(0)
        for wi, win in enumerate(POOL_WINDOWS):
            @pl.when(gi == wi)
            def _(win=win):
                def step(i, carry):
                    r0 = pl.multiple_of(i * trl, trl)
                    cur = x_ref[pl.ds(r0, trl), :]
                    trow = r0 + lax.broadcasted_iota(jnp.int32, (trl, 1), 0)
                    cnt = jnp.minimum(trow + 1, win).astype(F32)
                    if not backward:
                        halo = x_ref[pl.ds(pl.multiple_of(jnp.maximum(r0 - POOL_HALO, 0), POOL_HALO), POOL_HALO), :]
                        halo = jnp.where(i > 0, halo, 0.0)
                        s = jnp.concatenate([halo, cur], axis=0)
                        sh = 1
                        while sh < win:
                            s = s + pltpu.roll(s, sh, 0)
                            sh *= 2
                        res = s[POOL_HALO:] / cnt - cur
                    else:
                        halo = x_ref[pl.ds(pl.multiple_of(jnp.minimum(r0 + trl, l - POOL_HALO), POOL_HALO),
                                           POOL_HALO), :]
                        hrow = r0 + trl + lax.broadcasted_iota(jnp.int32, (POOL_HALO, 1), 0)
                        hcnt = jnp.minimum(hrow + 1, win).astype(F32)
                        halo = jnp.where(i < nt - 1, halo / hcnt, 0.0)
                        s = jnp.concatenate([cur / cnt, halo], axis=0)
                        sh = 1
                        while sh < win:
                            s = s + pltpu.roll(s, trl + POOL_HALO - sh, 0)
                            sh *= 2
                        res = s[:trl] - cur
                    o_ref[pl.ds(r0, trl), :] = res.astype(o_ref.dtype)
                    return carry
                lax.fori_loop(0, nt, step, 0)

    col = pl.BlockSpec((l, cwl), lambda g, j: (0, g * (gd // cwl) + j))
    if into is None:
        return pl.pallas_call(
            body, grid=(len(POOL_WINDOWS), gd // cwl), in_specs=[col], out_specs=col,
            out_shape=jax.ShapeDtypeStruct((l, dp), out_dtype),
            compiler_params=_params(("parallel", "parallel")), name=name)(src)
    return pl.pallas_call(
        body, grid=(len(POOL_WINDOWS), gd // cwl), in_specs=[col, pl.BlockSpec(memory_space=pl.ANY)], out_specs=col,
        out_shape=jax.ShapeDtypeStruct(into.shape, into.dtype), input_output_aliases={1: 0},
        compiler_params=_params(("parallel", "parallel")), name=name)(src, into)


def _pool_out_bwd_data(dx, wp_out, mg, pu, scale, tm=1024, tn=512):
    l, d = dx.shape
    dp = wp_out.shape[0]
    tm, tn = _t(l, tm), _t(dp, tn)

    def epi(dyp, m, gt, sc):
        sg = _silu(gt)
        return dyp * sc * sg, dyp * m * sc * _dsilu(gt), jnp.sum(dyp * m * sg, axis=0, keepdims=True)

    tile = pl.BlockSpec((tm, tn), lambda j, i, k: (i, j))
    right = pl.BlockSpec((tm, tn), lambda j, i, k: (i, dp // tn + j))
    vec = pl.BlockSpec((1, tn), lambda j, i, k: (0, j))
    return _mm("pool_out_bwd_data", (dp // tn, l // tm, 1),
               dx, pl.BlockSpec((tm, d), lambda j, i, k: (i, 0)),
               wp_out, pl.BlockSpec((tn, d), lambda j, i, k: (j, 0)), NT,
               [jax.ShapeDtypeStruct((l, dp), BF), jax.ShapeDtypeStruct((l, 2 * dp), BF),
                jax.ShapeDtypeStruct((1, dp), F32)], [tile, right, vec], (tm, tn),
               (mg, pu, scale), (tile, right, vec), epi, summed=(2, 1))


def _group_mm_fwd(mp, wg, pu, scale, tm=1024, tn=1024, tk=1024):
    l, dp = mp.shape
    ng, gd = wg.shape[0], wg.shape[1]
    tm, tn, tk = _t(l, tm), _t(gd, tn), _t(gd, tk)

    def epi(acc, gate, sc):
        return acc, acc * sc * _silu(gate)

    out = pl.BlockSpec((tm, tn), lambda g, i, j, k: (i, g * (gd // tn) + j))
    return _mm("group_mm_fwd", (ng, l // tm, gd // tn, gd // tk),
               mp, pl.BlockSpec((tm, tk), lambda g, i, j, k: (i, g * (gd // tk) + k)),
               wg, pl.BlockSpec((None, tk, tn), lambda g, i, j, k: (g, k, j)), NN,
               [jax.ShapeDtypeStruct((l, dp), F32), jax.ShapeDtypeStruct((l, dp), BF)], [out, out], (tm, tn),
               (pu, scale),
               (pl.BlockSpec((tm, tn), lambda g, i, j, k: (i, (dp + g * gd) // tn + j)),
                pl.BlockSpec((1, tn), lambda g, i, j, k: (0, g * (gd // tn) + j))), epi)


def _group_mm_bwd_data(dmg, wg, after=(), tm=1024, tn=1024, tk=1024):
    l, dp = dmg.shape
    ng, gd = wg.shape[0], wg.shape[1]
    tm, tn, tk = _t(l, tm), _t(gd, tn), _t(gd, tk)
    return _mm("group_mm_bwd_data", (ng, l // tm, gd // tn, gd // tk),
               dmg, pl.BlockSpec((tm, tk), lambda g, i, j, k: (i, g * (gd // tk) + k)),
               wg, pl.BlockSpec((None, tn, tk), lambda g, i, j, k: (g, j, k)), NT,
               [jax.ShapeDtypeStruct((l, dp), F32)],
               [pl.BlockSpec((tm, tn), lambda g, i, j, k: (i, g * (gd // tn) + j))], (tm, tn), after=after)[0]


def _group_mm_bwd_weight(mp, dmg, ng, tm=512, tk=1024):
    l, dp = mp.shape
    gd = dp // ng
    tm, tk = _t(gd, tm), _t(l, tk)
    return _mm("group_mm_bwd_weight", (ng, gd // tm, l // tk),
               mp, pl.BlockSpec((tk, tm), lambda g, i, k: (k, g * (gd // tm) + i)),
               dmg, pl.BlockSpec((tk, gd), lambda g, i, k: (k, g)), TN,
               [jax.ShapeDtypeStruct((ng, gd, gd // 2), F32)],
               [pl.BlockSpec((None, tm, gd // 2), lambda g, i, k: (g, i, 0))], (tm, gd), epi=_pack_epi)[0]


def _cast_bf16(name, w, tr=256):
    r, c = w.shape
    tr = _row_tile(r, tr, 16)

    def body(w_ref, o_ref):
        o_ref[...] = w_ref[...].astype(BF)

    blk = pl.BlockSpec((tr, c), lambda i: (i, 0))
    return pl.pallas_call(body, grid=(r // tr,), in_specs=[blk], out_specs=blk,
                          out_shape=jax.ShapeDtypeStruct((r, c), BF), compiler_params=_params(("parallel",)),
                          name=name)(w)


def _pack_rows(name, w, tr=256):
    r, c = w.shape
    tr = _row_tile(r, tr)

    def body(w_ref, o_ref):
        o_ref[...] = _pack_pairs(w_ref[...])

    return pl.pallas_call(body, grid=(r // tr, 2), in_specs=[pl.BlockSpec((tr, c // 2), lambda i, q: (i, q))],
                          out_specs=pl.BlockSpec((None, tr, c // 4), lambda i, q: (q, i, 0)),
                          out_shape=jax.ShapeDtypeStruct((2, r, c // 4), F32),
                          compiler_params=_params(("parallel", "parallel")), name=name)(w)


def _unpack_rows(name, w0, w1, tr=512):
    r, h = w0.shape
    tr = _row_tile(r, tr, 16)

    def body(w0_ref, w1_ref, o_ref):
        for q, w_ref in enumerate((w0_ref, w1_ref)):
            hi, lo = _unpack_pairs(w_ref[...])
            o_ref[q, :, :h] = hi.astype(BF)
            o_ref[q, :, h:] = lo.astype(BF)

    words = pl.BlockSpec((tr, h), lambda i: (i, 0))
    return pl.pallas_call(body, grid=(r // tr,), in_specs=[words, words],
                          out_specs=pl.BlockSpec((2, tr, 2 * h), lambda i: (0, i, 0)),
                          out_shape=jax.ShapeDtypeStruct((2, r, 2 * h), BF), compiler_params=_params(("parallel",)),
                          name=name)(w0, w1)


def _reduce_packed(name, recv, tr=256):
    nd, r, h = recv.shape
    tr = _row_tile(r, tr)

    def body(p_ref, o_ref):
        hi, lo = _unpack_pairs(p_ref[0])
        for k in range(1, nd):
            a, b = _unpack_pairs(p_ref[k])
            hi, lo = hi + a, lo + b
        o_ref[:, :h] = hi
        o_ref[:, h:] = lo

    return pl.pallas_call(body, grid=(r // tr,), in_specs=[pl.BlockSpec((nd, tr, h), lambda i: (0, i, 0))],
                          out_specs=pl.BlockSpec((tr, 2 * h), lambda i: (i, 0)),
                          out_shape=jax.ShapeDtypeStruct((r, 2 * h), F32), compiler_params=_params(("parallel",)),
                          name=name)(recv)


def _adamw_math(w, g, m, v):
    m2 = ADAM_B1 * m + (1.0 - ADAM_B1) * g
    v2 = ADAM_B2 * v + (1.0 - ADAM_B2) * (g * g)
    m_hat = m2 / (1.0 - ADAM_B1 ** ADAM_STEP)
    v_hat = v2 / (1.0 - ADAM_B2 ** ADAM_STEP)
    delta = -ADAM_LR * (m_hat / (jnp.sqrt(v_hat) + ADAM_EPS) + ADAM_WD * w)
    return delta, m2, v2


def _adamw(name, w, g, m, v, tr=256):
    r, c = w.shape
    tr = _row_tile(r, tr)

    def body(w_ref, g_ref, m_ref, v_ref, d_ref, m2_ref, v2_ref):
        d, m2, v2 = _adamw_math(w_ref[...], g_ref[...], m_ref[...], v_ref[...])
        d_ref[...] = d
        m2_ref[...] = m2
        v2_ref[...] = v2

    blk = pl.BlockSpec((tr, c), lambda i: (i, 0))
    shp = jax.ShapeDtypeStruct((r, c), F32)
    return pl.pallas_call(body, grid=(r // tr,), in_specs=[blk] * 4, out_specs=[blk] * 3, out_shape=[shp] * 3,
                          compiler_params=_params(("parallel",)), name=name)(w, g, m, v)


def _sum_slots(name, a):
    nd, r, c = a.shape

    def body(a_ref, o_ref):
        s = a_ref[0]
        for k in range(1, nd):
            s = s + a_ref[k]
        o_ref[...] = s

    return pl.pallas_call(body, out_shape=jax.ShapeDtypeStruct((r, c), F32), name=name)(a)


def _head_rows(v, g_n, r):
    return jnp.pad(v.reshape(g_n, 1, r), ((0, 0), (0, 0), (0, HEAD_LANES - r)))


class _Later:
    def __init__(self, hosted, first_of, weights_of, pool_w_out_of, send):
        self.hosted, self.first_of, self.weights_of = hosted, first_of, weights_of
        self.pool_w_out_of, self.send = pool_w_out_of, send


def _listed(res):
    return res if isinstance(res, (list, tuple)) else [res]


def _local_step(x, target, ln_g, final_g, conv_b, dt_bias, a_log, d_skip, norm_g, later):
    l, d = x.shape
    di = norm_g.shape[1]
    h = dt_bias.shape[1]
    p = di // h
    conv = conv_b.shape[1]
    n, t = SSM_STATE, SSD_CHUNK
    g_n = (conv - di) // (2 * n)
    r = h // g_n
    nm = di + conv
    ng = len(POOL_WINDOWS)
    dims = (h, g_n, r, p, n, t)
    pw = d

    hosted = later.hosted
    h0, r0, *arrived_0 = _rms_fwd("rms0_fwd", x, ln_g[0:1], exch=hosted.get("rms0"))
    wt_in, conv_w, scale = later.first_of(arrived_0)
    dp = scale.shape[1]
    pm, *arrived_i = _listed(_mm_nt_halves("in_proj_main", h0, wt_in, 0, nm, exch=hosted.get("in_proj")))
    dtr = _mm_nt_halves("in_proj_dt", h0, wt_in, nm, h, tn=h)
    xbc = _conv_fwd(pm, di, conv_w, conv_b)
    dtb, alog = _head_rows(dt_bias, g_n, r), _head_rows(a_log, g_n, r)
    dskx = jnp.repeat(d_skip.reshape(g_n, 1, r), p, axis=2)
    dtr_g = jnp.pad(jnp.transpose(dtr.reshape(l, g_n, r), (1, 0, 2)), ((0, 0), (0, 0), (0, HEAD_LANES - r)))
    y, states, yn, r_n, *arrived_s = _ssd_fwd(xbc, dtr_g, dtb, alog, dskx, pm, norm_g, dims, exch=hosted.get("ssd"))
    w_out, wg, wp_in = later.weights_of(arrived_i, arrived_s)
    x1 = _mm_nn("ssm_out_proj", yn, w_out, add=x)

    h1, r1 = _rms_fwd("rms1_fwd", x1, ln_g[1:2])
    pu, *arrived_p = _listed(_mm_nn_blocked("pool_in_proj", h1, wp_in, exch=hosted.get("pool_in_proj")))
    wp_out = later.pool_w_out_of(arrived_p)
    mp = _pool_mix("pool_mix_fwd", pu, dp, False, BF)
    mg, yp = _group_mm_fwd(mp, wg, pu, scale)
    x2 = _mm_nn("pool_out_proj", yp, wp_out, add=x1)

    dx2, dx2_b, d_final_g, loss = _loss_head(x2, final_g, target)

    gw_pout = _mm_tn_packed("pool_out_bwd_weight", yp, dx2_b, pw)
    going = later.send("pool_w_out", gw_pout)
    dmg, dpu, d_scale = _pool_out_bwd_data(dx2_b, wp_out, mg, pu, scale)
    dmp = _group_mm_bwd_data(dmg, wg, after=going)
    gw_g = _group_mm_bwd_weight(mp, dmg, ng)
    going = later.send("pool_w_group", gw_g)
    dpu = _pool_mix("pool_mix_bwd", dmp, dp, True, BF, into=dpu)
    dh1 = _mm_nt_blocked("pool_in_bwd_data", dpu, wp_in, after=going)
    gw_pin = _mm_tn_packed("pool_in_bwd_weight", h1, dpu, 2 * dp // N_DEV)
    going = later.send("pool_w_in", gw_pin)
    dx1, dx1_b, d_ln1 = _rms_bwd("rms1_bwd", dh1, x1, r1, ln_g[1:2], dx2)

    dyn = _mm_nt("ssm_out_bwd_data", dx1_b, w_out, 0, di, tn=1024, after=going)
    gw_out = _mm_tn_packed("ssm_out_bwd_weight", yn, dx1_b, pw)
    going = later.send("ssm_w_out", gw_out)
    dxbc, ddt_g, dbias_g, dalog_g, dd_g, dproj, d_norm_g = _ssd_bwd(
        xbc, dtr_g, dtb, alog, dskx, states, dyn, y, pm, r_n, norm_g, dims, after=going)
    dproj, d_conv_w, d_conv_b = _conv_bwd(pm, di, conv_w, conv_b, dxbc, dproj)
    ddt = jnp.transpose(ddt_g[:, :, :r], (1, 0, 2)).reshape(l, h)
    gwt_in, going = [], ()
    for q in range(2):
        gq = _mm_tn_packed(f"in_proj_bwd_weight_{q}", dproj, h0, d // 2, rows=nm + h, b_blk=q, after=going)
        gq = _mm_tn_packed(f"in_proj_bwd_weight_dt_{q}", ddt, h0, d // 2, into=gq, b_blk=q)
        going = later.send(f"ssm_w_in_{q}", gq)
        gwt_in.append(gq)
    gwt_in = jnp.concatenate(gwt_in, axis=0)
    dh0 = _mm_nn_halves("in_proj_bwd_data_dt", ddt, wt_in, nm, after=going)
    dh0 = _mm_nn_halves("in_proj_bwd_data", dproj, wt_in, 0, add=dh0)
    grad_x, d_ln0 = _rms_bwd("rms0_bwd", dh0, x, r0, ln_g[0:1], dx1, bf16_copy=False)

    def heads(v):
        return v[:, 0, :r].reshape(1, h)

    small = dict(ln_g=jnp.concatenate([d_ln0, d_ln1], axis=0), final_g=d_final_g, conv_w=d_conv_w, conv_b=d_conv_b,
                 dt_bias=heads(dbias_g), a_log=heads(dalog_g), d_skip=heads(dd_g), norm_g=d_norm_g, scale=d_scale)
    big = dict(ssm_w_in=gwt_in, ssm_w_out=gw_out, pool_w_in=gw_pin, pool_w_group=gw_g, pool_w_out=gw_pout)
    return loss, grad_x, small, big


SMALL_ORDER = ("ln_g", "final_g", "conv_w", "conv_b", "dt_bias", "a_log", "d_skip", "norm_g", "scale", "loss")


def _flatten_small(parts):
    flat = jnp.concatenate([parts[k].reshape(-1) for k in SMALL_ORDER])
    n = flat.shape[0]
    rows = -(-n // 1024) * 8
    return jnp.pad(flat, (0, rows * 128 - n)).reshape(rows, 128)


def _split_small(flat, shapes):
    flat = flat.reshape(-1)
    out, off = {}, 0
    for k in SMALL_ORDER:
        size = int(np.prod(shapes[k]))
        out[k] = flat[off:off + size].reshape(shapes[k])
        off += size
    return out


def kernel(x, ln_g, final_g, ssm_w_in, ssm_conv_w, ssm_conv_b, ssm_dt_bias, ssm_a_log, ssm_d, ssm_norm_g, ssm_w_out, pool_w_in, pool_w_group, pool_scale, pool_w_out, loss_target, m_ln_g, m_final_g, m_ssm_w_in, m_ssm_conv_w, m_ssm_conv_b, m_ssm_dt_bias, m_ssm_a_log, m_ssm_d, m_ssm_norm_g, m_ssm_w_out, m_pool_w_in, m_pool_w_group, m_pool_scale, m_pool_w_out, v_ln_g, v_final_g, v_ssm_w_in, v_ssm_conv_w, v_ssm_conv_b, v_ssm_dt_bias, v_ssm_a_log, v_ssm_d, v_ssm_norm_g, v_ssm_w_out, v_pool_w_in, v_pool_w_group, v_pool_scale, v_pool_w_out):
    l, d = x.shape[1], x.shape[2]
    me = 4 * lax.axis_index("x") + 2 * lax.axis_index("y") + lax.axis_index("c")
    ng, gds, gd = pool_w_group.shape[1], pool_w_group.shape[2], pool_w_group.shape[3]
    sin_s = ssm_w_in.shape[2]
    dp_s = pool_w_out.shape[1]
    conv_s = ssm_conv_w.shape[2]

    wt_in_s = _pack_rows("pack_w_in", jnp.transpose(ssm_w_in[0]))
    w_out_s = _cast_bf16("cast_w_out", ssm_w_out[0])
    wp_in_s = _cast_bf16("cast_pool_w_in", pool_w_in[0])
    wg_s = _cast_bf16("cast_pool_w_group", pool_w_group[0].reshape(ng * gds, gd))
    wp_out_s = _cast_bf16("cast_pool_w_out", pool_w_out[0])
    small_s = jnp.concatenate([ssm_conv_w[0].reshape(-1), pool_scale[0]])
    n_small = small_s.shape[0]
    small_s = jnp.pad(small_s, (0, -(-n_small // 1024) * 1024 - n_small)).reshape(-1, 128)
    hosted = dict(rms0=_Gather([wt_in_s[0], wt_in_s[1], small_s]), in_proj=_Gather([w_out_s, wg_s]),
                  ssd=_Gather([wp_in_s]), pool_in_proj=_Gather([wp_out_s]))

    def first_of(arrived_0):
        wt_in_g0, wt_in_g1, small_g = arrived_0
        wt_in = _unpack_rows("unpack_w_in", wt_in_g0.reshape(N_DEV * sin_s, d // 4),
                             wt_in_g1.reshape(N_DEV * sin_s, d // 4))
        small_all = small_g.reshape(N_DEV, -1)[:, :n_small]
        conv_w = jnp.transpose(small_all[:, :CONV_TAPS * conv_s].reshape(N_DEV, CONV_TAPS, conv_s), (1, 0, 2))
        return wt_in, conv_w.reshape(CONV_TAPS, -1), small_all[:, CONV_TAPS * conv_s:].reshape(1, -1)

    def weights_of(arrived_i, arrived_s):
        wg = jnp.transpose(arrived_i[1].reshape(N_DEV, ng, gds, gd), (1, 0, 2, 3)).reshape(ng, gd, gd)
        return arrived_i[0].reshape(-1, d), wg, arrived_s[0]

    def rows_major(gp):
        q, _, hw = gp.shape
        return jnp.transpose(gp.reshape(q, N_DEV, -1, hw), (1, 0, 2, 3))

    to_blocks = dict(ssm_w_in_0=rows_major, ssm_w_in_1=rows_major, ssm_w_out=rows_major, pool_w_out=rows_major,
                     pool_w_in=lambda gp: gp[:, None],
                     pool_w_group=lambda gp: jnp.transpose(gp.reshape(ng, N_DEV, gds, gd // 2), (1, 0, 2, 3)))
    travelling = {}

    def send(name, gp):
        *travelling[name], token = _scatter_start(f"scatter_{name}_start", to_blocks[name](gp))
        return (token,)

    later = _Later(hosted, first_of, weights_of, lambda arrived_p: arrived_p[0].reshape(-1, d), send)

    loss, grad_x, small, _ = _local_step(
        x[0], loss_target[0], ln_g, final_g.reshape(1, d), ssm_conv_b, ssm_dt_bias, ssm_a_log, ssm_d, ssm_norm_g, later)

    def arrived(name, after):
        src, land = _scatter_wait(f"scatter_{name}_wait", *travelling[name], after=after)
        own = lax.dynamic_slice_in_dim(src, me, 1, axis=0)
        return lax.dynamic_update_slice_in_dim(land, own, me, axis=0)

    def reduced(name, rv):
        q = rv.shape[1]
        cols = [_reduce_packed(f"{name}_{j}", rv[:, j]) for j in range(q)]
        return cols[0] if q == 1 else jnp.concatenate(cols, axis=1)

    grads = {}
    grads["pool_w_out"] = reduced("reduce_pool_w_out", arrived("pool_w_out", (grad_x,)))[None]
    grads["pool_w_group"] = _reduce_packed("reduce_pool_w_group", arrived("pool_w_group", (grad_x,)).reshape(
        N_DEV, ng * gds, gd // 2)).reshape(1, ng, gds, gd)
    grads["pool_w_in"] = reduced("reduce_pool_w_in", arrived("pool_w_in", (grad_x,)))[None]
    grads["ssm_w_out"] = reduced("reduce_w_out", arrived("ssm_w_out", (grad_x,)))[None]

    weights = dict(ln_g=ln_g, final_g=final_g, ssm_w_in=ssm_w_in, ssm_conv_w=ssm_conv_w, ssm_conv_b=ssm_conv_b,
                   ssm_dt_bias=ssm_dt_bias, ssm_a_log=ssm_a_log, ssm_d=ssm_d, ssm_norm_g=ssm_norm_g,
                   ssm_w_out=ssm_w_out, pool_w_in=pool_w_in, pool_w_group=pool_w_group, pool_scale=pool_scale,
                   pool_w_out=pool_w_out)
    m_in = dict(ln_g=m_ln_g, final_g=m_final_g, ssm_w_in=m_ssm_w_in, ssm_conv_w=m_ssm_conv_w, ssm_conv_b=m_ssm_conv_b,
                ssm_dt_bias=m_ssm_dt_bias, ssm_a_log=m_ssm_a_log, ssm_d=m_ssm_d, ssm_norm_g=m_ssm_norm_g,
                ssm_w_out=m_ssm_w_out, pool_w_in=m_pool_w_in, pool_w_group=m_pool_w_group, pool_scale=m_pool_scale,
                pool_w_out=m_pool_w_out)
    v_in = dict(ln_g=v_ln_g, final_g=v_final_g, ssm_w_in=v_ssm_w_in, ssm_conv_w=v_ssm_conv_w, ssm_conv_b=v_ssm_conv_b,
                ssm_dt_bias=v_ssm_dt_bias, ssm_a_log=v_ssm_a_log, ssm_d=v_ssm_d, ssm_norm_g=v_ssm_norm_g,
                ssm_w_out=v_ssm_w_out, pool_w_in=v_pool_w_in, pool_w_group=v_pool_w_group, pool_scale=v_pool_scale,
                pool_w_out=v_pool_w_out)
    names = list(weights)
    big_names = ("ssm_w_out", "pool_w_in", "pool_w_group", "pool_w_out", "ssm_w_in")
    delta, new_m, new_v = {}, {}, {}

    def adamw_big(k):
        shp = weights[k].shape
        two_d = (-1, shp[-1])
        dk, mk, vk = _adamw(f"adamw_{k}", weights[k].reshape(two_d), grads[k].reshape(two_d), m_in[k].reshape(two_d),
                            v_in[k].reshape(two_d))
        delta[k], new_m[k], new_v[k] = dk.reshape(shp), mk.reshape(shp), vk.reshape(shp)

    for k in big_names[:-1]:
        adamw_big(k)
    small_names = [k for k in names if k not in big_names]

    small["loss"] = loss
    shapes = {k: small[k].shape for k in SMALL_ORDER}
    gathered_small, = _exchange_alone("all_gather_small_grads", _Gather([_flatten_small(small)]),
                                      after=tuple(delta[k] for k in big_names[:-1]))
    summed = _split_small(_sum_slots("sum_small_grads", gathered_small), shapes)
    grads.update(ln_g=summed["ln_g"], final_g=summed["final_g"].reshape(d), ssm_conv_b=summed["conv_b"],
                 ssm_conv_w=lax.dynamic_slice_in_dim(summed["conv_w"], me * conv_s, conv_s, axis=1)[None],
                 ssm_dt_bias=summed["dt_bias"], ssm_a_log=summed["a_log"], ssm_d=summed["d_skip"],
                 ssm_norm_g=summed["norm_g"],
                 pool_scale=lax.dynamic_slice_in_dim(summed["scale"], me * dp_s, dp_s, axis=1))

    def packed(tree):
        flat = jnp.concatenate([tree[k].reshape(-1) for k in small_names])
        nn = flat.shape[0]
        return jnp.pad(flat, (0, -(-nn // 1024) * 1024 - nn), constant_values=1.0).reshape(-1, 128)

    ds, ms, vs = _adamw("adamw_small", packed(weights), packed(grads), packed(m_in), packed(v_in))
    off = 0
    for k in small_names:
        shp = weights[k].shape
        size = int(np.prod(shp))
        for res, arr in ((delta, ds), (new_m, ms), (new_v, vs)):
            res[k] = arr.reshape(-1)[off:off + size].reshape(shp)
        off += size

    k = "ssm_w_in"
    gt = jnp.concatenate([reduced(f"reduce_w_in_{q}", arrived(f"ssm_w_in_{q}", (ds,))) for q in range(2)], axis=1)
    dk, mk, vk = _adamw("adamw_ssm_w_in", jnp.transpose(weights[k][0]), gt, jnp.transpose(m_in[k][0]),
                        jnp.transpose(v_in[k][0]))
    grads[k], delta[k], new_m[k], new_v[k] = (jnp.transpose(t)[None] for t in (gt, dk, mk, vk))

    return (summed["loss"].reshape(()), grad_x[None], *[grads[k] for k in names], *[delta[k] for k in names],
            *[new_m[k] for k in names], *[new_v[k] for k in names])
```
